```python
import math
import jax, jax.numpy as jnp
from jax import lax
import numpy as np

D_MODEL = 1024
BATCH = 8
SEQ = 8192
DEPTH = 2

CHUNK = 64
LEFT_CHUNKS = 8
BAND = (LEFT_CHUNKS + 1) * CHUNK
S5_WIDTH = D_MODEL // 2
S5_GROUP = 16
S5_GROUPS = S5_WIDTH // S5_GROUP
S5_STATE = 64
CONV_WIDTH = D_MODEL - S5_WIDTH
CONV_KERNEL = 31
ATT_HEADS = 16
ATT_HEAD_DIM = D_MODEL // ATT_HEADS
MAX_REL = 128
MEM_LEN = 256
XA_HEADS = 4
XA_HEAD_DIM = D_MODEL // XA_HEADS
EPS = 1e-6
N_EVEN = (DEPTH + 1) // 2
N_ODD = DEPTH // 2
EVEN_IN = 2 * S5_WIDTH + 3 * CONV_WIDTH
ODD_IN = 4 * D_MODEL

kernel_name = 'streaming_hybrid_s5_conv_chunkattn'


def rms_norm(x, g):
    xf = x.astype(jnp.float32)
    y = xf * lax.rsqrt(jnp.mean(xf * xf, axis=-1, keepdims=True) + EPS)
    return (y * g.astype(jnp.float32)).astype(x.dtype)


def _complex_affine_combine(e1, e2):
    a1r, a1i, b1r, b1i = e1
    a2r, a2i, b2r, b2i = e2
    ar = a2r * a1r - a2i * a1i
    ai = a2r * a1i + a2i * a1r
    br = a2r * b1r - a2i * b1i + b2r
    bi = a2r * b1i + a2i * b1r + b2i
    return (ar, ai, br, bi)


def s5_mixer(u, lam_re, lam_im, log_dt, b_re, b_im, c_re, c_im, d_skip, glu_w, glu_b):
    bsz, L, _ = u.shape
    f32 = jnp.float32
    uf = u.astype(f32)
    ug = uf.reshape(bsz, L, S5_GROUPS, S5_GROUP)
    lr = lam_re.astype(f32)
    li = lam_im.astype(f32)
    dt = jnp.exp(log_dt.astype(f32))[:, None]
    mag = jnp.exp(lr * dt)
    ab_re = mag * jnp.cos(li * dt)
    ab_im = mag * jnp.sin(li * dt)
    den = lr * lr + li * li
    nr = ab_re - 1.0
    coef_re = (nr * lr + ab_im * li) / den
    coef_im = (ab_im * lr - nr * li) / den
    br = b_re.astype(f32)
    bi = b_im.astype(f32)
    bb_re = coef_re[..., None] * br - coef_im[..., None] * bi
    bb_im = coef_re[..., None] * bi + coef_im[..., None] * br
    bu_re = jnp.einsum('blgc,gpc->blgp', ug, bb_re)
    bu_im = jnp.einsum('blgc,gpc->blgp', ug, bb_im)
    a_re = jnp.broadcast_to(ab_re, bu_re.shape)
    a_im = jnp.broadcast_to(ab_im, bu_im.shape)
    _, _, h_re, h_im = lax.associative_scan(_complex_affine_combine, (a_re, a_im, bu_re, bu_im), axis=1)
    y = (jnp.einsum('blgp,gcp->blgc', h_re, c_re.astype(f32))
         - jnp.einsum('blgp,gcp->blgc', h_im, c_im.astype(f32)))
    y = y.reshape(bsz, L, S5_WIDTH) + d_skip.astype(f32) * uf
    z = jax.nn.gelu(y)
    out = z * jax.nn.sigmoid(z @ glu_w.astype(f32) + glu_b.astype(f32))
    return out.astype(u.dtype)


def conv_module(val, glu_gate, conv_w, conv_b, ln_g, ln_b):
    v = val * jax.nn.sigmoid(glu_gate)
    vpad = jnp.pad(v, ((0, 0), (CONV_KERNEL - 1, 0), (0, 0)))
    c = lax.conv_general_dilated(vpad, conv_w[:, None, :].astype(v.dtype), window_strides=(1,),
                                 padding='VALID', dimension_numbers=('NWC', 'WIO', 'NWC'),
                                 feature_group_count=CONV_WIDTH) + conv_b.astype(v.dtype)
    cf = c.astype(jnp.float32)
    mu = jnp.mean(cf, axis=-1, keepdims=True)
    var = jnp.mean(jnp.square(cf - mu), axis=-1, keepdims=True)
    cn = (cf - mu) * lax.rsqrt(var + EPS) * ln_g.astype(jnp.float32) + ln_b.astype(jnp.float32)
    return jax.nn.silu(cn).astype(val.dtype)


def ssm_conv_layer(h, w_in, lam_re, lam_im, log_dt, b_re, b_im, c_re, c_im, d_skip, glu_w, glu_b,
                   conv_w, conv_b, ln_g, ln_b, w_out):
    z = h @ w_in
    cuts = [S5_WIDTH, 2 * S5_WIDTH, 2 * S5_WIDTH + CONV_WIDTH, 2 * S5_WIDTH + 2 * CONV_WIDTH]
    u_a, gate_a, val_b, glu_b_in, gate_b = jnp.split(z, cuts, axis=-1)
    y_a = s5_mixer(u_a, lam_re, lam_im, log_dt, b_re, b_im, c_re, c_im, d_skip, glu_w, glu_b) * jax.nn.silu(gate_a)
    y_b = conv_module(val_b, glu_b_in, conv_w, conv_b, ln_g, ln_b) * jax.nn.silu(gate_b)
    return jnp.concatenate([y_a, y_b], axis=-1) @ w_out


def chunk_attention_layer(h, w_in, rel_bias, w_out):
    bsz, L, _ = h.shape
    q, k, v, g = jnp.split(h @ w_in, 4, axis=-1)
    q = q.reshape(bsz, L, ATT_HEADS, ATT_HEAD_DIM)
    k = k.reshape(bsz, L, ATT_HEADS, ATT_HEAD_DIM)
    v = v.reshape(bsz, L, ATT_HEADS, ATT_HEAD_DIM)
    pad = BAND - CHUNK
    kp = jnp.pad(k, ((0, 0), (pad, 0), (0, 0), (0, 0)))
    vp = jnp.pad(v, ((0, 0), (pad, 0), (0, 0), (0, 0)))
    qi = jnp.arange(CHUNK)[:, None]
    kj = jnp.arange(BAND)[None, :]
    rel = jnp.clip(qi - kj + pad, -MAX_REL, MAX_REL) + MAX_REL
    bias = rel_bias.astype(jnp.float32)[:, rel]
    scale = ATT_HEAD_DIM ** -0.5

    def one_chunk(c):
        start = c * CHUNK
        qc = lax.dynamic_slice_in_dim(q, start, CHUNK, axis=1)
        kc = lax.dynamic_slice_in_dim(kp, start, BAND, axis=1)
        vc = lax.dynamic_slice_in_dim(vp, start, BAND, axis=1)
        s = jnp.einsum('bqhd,bkhd->bhqk', qc, kc).astype(jnp.float32) * scale + bias
        valid = (start - pad + jnp.arange(BAND)) >= 0
        s = jnp.where(valid[None, None, None, :], s, -1e30)
        p = jax.nn.softmax(s, axis=-1).astype(h.dtype)
        return jnp.einsum('bhqk,bkhd->bqhd', p, vc)

    o = lax.map(one_chunk, jnp.arange(L // CHUNK))
    o = jnp.moveaxis(o, 0, 1).reshape(bsz, L, D_MODEL)
    return (o * jax.nn.silu(g)) @ w_out


def mem_cross_attention(h, mem_n, w_qg, w_kv, w_o):
    bsz, L, _ = h.shape
    q, g = jnp.split(h @ w_qg, 2, axis=-1)
    k, v = jnp.split(mem_n @ w_kv, 2, axis=-1)
    q = q.reshape(bsz, L, XA_HEADS, XA_HEAD_DIM)
    k = k.reshape(bsz, -1, XA_HEADS, XA_HEAD_DIM)
    v = v.reshape(bsz, -1, XA_HEADS, XA_HEAD_DIM)
    s = jnp.einsum('bqhd,bkhd->bhqk', q, k).astype(jnp.float32) * (XA_HEAD_DIM ** -0.5)
    p = jax.nn.softmax(s, axis=-1).astype(h.dtype)
    o = jnp.einsum('bhqk,bkhd->bqhd', p, v).reshape(bsz, L, D_MODEL)
    return (o * jax.nn.silu(g)) @ w_o


def _fwd_setup_inputs(seed: int = 0) -> dict:
    key = jax.random.key(seed)
    ks = iter(jax.random.split(key, 40))

    def nrm(shape, scale):
        return jax.random.normal(next(ks), shape, jnp.float32) * scale

    def gain(shape):
        return 1.0 + nrm(shape, 0.02)

    lam_im_base = jnp.pi * jnp.arange(S5_STATE, dtype=jnp.float32)
    return {
        'x': nrm((BATCH, SEQ, D_MODEL), 1.0),
        'mem': nrm((BATCH, MEM_LEN, D_MODEL), 1.0),
        'mem_norm_g': gain((D_MODEL,)),
        'ev_norm_g': gain((N_EVEN, D_MODEL)),
        'ev_w_in': nrm((N_EVEN, D_MODEL, EVEN_IN), D_MODEL ** -0.5),
        'ev_s5_lambda_re': -0.5 + nrm((N_EVEN, S5_GROUPS, S5_STATE), 0.01),
        'ev_s5_lambda_im': lam_im_base + nrm((N_EVEN, S5_GROUPS, S5_STATE), 0.01),
        'ev_s5_log_dt': jax.random.uniform(next(ks), (N_EVEN, S5_GROUPS), jnp.float32,
                                           math.log(1e-3), math.log(1e-1)),
        'ev_s5_b_re': nrm((N_EVEN, S5_GROUPS, S5_STATE, S5_GROUP), (2 * S5_GROUP) ** -0.5),
        'ev_s5_b_im': nrm((N_EVEN, S5_GROUPS, S5_STATE, S5_GROUP), (2 * S5_GROUP) ** -0.5),
        'ev_s5_c_re': nrm((N_EVEN, S5_GROUPS, S5_GROUP, S5_STATE), S5_STATE ** -0.5),
        'ev_s5_c_im': nrm((N_EVEN, S5_GROUPS, S5_GROUP, S5_STATE), S5_STATE ** -0.5),
        'ev_s5_d': nrm((N_EVEN, S5_WIDTH), 1.0),
        'ev_s5_glu_w': nrm((N_EVEN, S5_WIDTH, S5_WIDTH), S5_WIDTH ** -0.5),
        'ev_s5_glu_b': nrm((N_EVEN, S5_WIDTH), 0.01),
        'ev_conv_w': nrm((N_EVEN, CONV_KERNEL, CONV_WIDTH), CONV_KERNEL ** -0.5),
        'ev_conv_b': nrm((N_EVEN, CONV_WIDTH), 0.01),
        'ev_conv_ln_g': gain((N_EVEN, CONV_WIDTH)),
        'ev_conv_ln_b': nrm((N_EVEN, CONV_WIDTH), 0.01),
        'ev_w_out': nrm((N_EVEN, D_MODEL, D_MODEL), D_MODEL ** -0.5),
        'od_norm_g': gain((N_ODD, D_MODEL)),
        'od_w_in': nrm((N_ODD, D_MODEL, ODD_IN), D_MODEL ** -0.5),
        'od_rel_bias': nrm((N_ODD, ATT_HEADS, 2 * MAX_REL + 1), 0.1),
        'od_w_out': nrm((N_ODD, D_MODEL, D_MODEL), D_MODEL ** -0.5),
        'xa_norm_g': gain((DEPTH, D_MODEL)),
        'xa_w_qg': nrm((DEPTH, D_MODEL, 2 * D_MODEL), D_MODEL ** -0.5),
        'xa_w_kv': nrm((DEPTH, D_MODEL, 2 * D_MODEL), D_MODEL ** -0.5),
        'xa_w_o': nrm((DEPTH, D_MODEL, D_MODEL), D_MODEL ** -0.5),
        'final_norm_g': gain((D_MODEL,)),
    }


def _fwd_reference(x, mem, mem_norm_g, ev_norm_g, ev_w_in, ev_s5_lambda_re, ev_s5_lambda_im, ev_s5_log_dt,
              ev_s5_b_re, ev_s5_b_im, ev_s5_c_re, ev_s5_c_im, ev_s5_d, ev_s5_glu_w, ev_s5_glu_b,
              ev_conv_w, ev_conv_b, ev_conv_ln_g, ev_conv_ln_b, ev_w_out,
              od_norm_g, od_w_in, od_rel_bias, od_w_out,
              xa_norm_g, xa_w_qg, xa_w_kv, xa_w_o, final_norm_g):
    mem_n = rms_norm(mem, mem_norm_g)
    for layer in range(DEPTH):
        i = layer // 2
        if layer % 2 == 0:
            x = x + ssm_conv_layer(rms_norm(x, ev_norm_g[i]), ev_w_in[i], ev_s5_lambda_re[i], ev_s5_lambda_im[i],
                                   ev_s5_log_dt[i], ev_s5_b_re[i], ev_s5_b_im[i], ev_s5_c_re[i], ev_s5_c_im[i],
                                   ev_s5_d[i], ev_s5_glu_w[i], ev_s5_glu_b[i], ev_conv_w[i], ev_conv_b[i],
                                   ev_conv_ln_g[i], ev_conv_ln_b[i], ev_w_out[i])
        else:
            x = x + chunk_attention_layer(rms_norm(x, od_norm_g[i]), od_w_in[i], od_rel_bias[i], od_w_out[i])
        x = x + mem_cross_attention(rms_norm(x, xa_norm_g[layer]), mem_n, xa_w_qg[layer], xa_w_kv[layer], xa_w_o[layer])
    return rms_norm(x, final_norm_g)


import jax as _jax
import jax.numpy as _jnp

TWIN_FORMAT = 'train_step'
FWD_PARAMS = ['x', 'mem', 'mem_norm_g', 'ev_norm_g', 'ev_w_in', 'ev_s5_lambda_re', 'ev_s5_lambda_im', 'ev_s5_log_dt', 'ev_s5_b_re', 'ev_s5_b_im', 'ev_s5_c_re', 'ev_s5_c_im', 'ev_s5_d', 'ev_s5_glu_w', 'ev_s5_glu_b', 'ev_conv_w', 'ev_conv_b', 'ev_conv_ln_g', 'ev_conv_ln_b', 'ev_w_out', 'od_norm_g', 'od_w_in', 'od_rel_bias', 'od_w_out', 'xa_norm_g', 'xa_w_qg', 'xa_w_kv', 'xa_w_o', 'final_norm_g']
TWIN_WEIGHTS = ['mem_norm_g', 'ev_norm_g', 'ev_w_in', 'ev_s5_lambda_re', 'ev_s5_lambda_im', 'ev_s5_log_dt', 'ev_s5_b_re', 'ev_s5_b_im', 'ev_s5_c_re', 'ev_s5_c_im', 'ev_s5_d', 'ev_s5_glu_w', 'ev_s5_glu_b', 'ev_conv_w', 'ev_conv_b', 'ev_conv_ln_g', 'ev_conv_ln_b', 'ev_w_out', 'od_norm_g', 'od_w_in', 'od_rel_bias', 'od_w_out', 'xa_norm_g', 'xa_w_qg', 'xa_w_kv', 'xa_w_o', 'final_norm_g']
TWIN_DIFF_INPUT = 'x'
TWIN_INPUTS = ['x', 'mem', 'mem_norm_g', 'ev_norm_g', 'ev_w_in', 'ev_s5_lambda_re', 'ev_s5_lambda_im', 'ev_s5_log_dt', 'ev_s5_b_re', 'ev_s5_b_im', 'ev_s5_c_re', 'ev_s5_c_im', 'ev_s5_d', 'ev_s5_glu_w', 'ev_s5_glu_b', 'ev_conv_w', 'ev_conv_b', 'ev_conv_ln_g', 'ev_conv_ln_b', 'ev_w_out', 'od_norm_g', 'od_w_in', 'od_rel_bias', 'od_w_out', 'xa_norm_g', 'xa_w_qg', 'xa_w_kv', 'xa_w_o', 'final_norm_g', 'loss_target', 'm_mem_norm_g', 'm_ev_norm_g', 'm_ev_w_in', 'm_ev_s5_lambda_re', 'm_ev_s5_lambda_im', 'm_ev_s5_log_dt', 'm_ev_s5_b_re', 'm_ev_s5_b_im', 'm_ev_s5_c_re', 'm_ev_s5_c_im', 'm_ev_s5_d', 'm_ev_s5_glu_w', 'm_ev_s5_glu_b', 'm_ev_conv_w', 'm_ev_conv_b', 'm_ev_conv_ln_g', 'm_ev_conv_ln_b', 'm_ev_w_out', 'm_od_norm_g', 'm_od_w_in', 'm_od_rel_bias', 'm_od_w_out', 'm_xa_norm_g', 'm_xa_w_qg', 'm_xa_w_kv', 'm_xa_w_o', 'm_final_norm_g', 'v_mem_norm_g', 'v_ev_norm_g', 'v_ev_w_in', 'v_ev_s5_lambda_re', 'v_ev_s5_lambda_im', 'v_ev_s5_log_dt', 'v_ev_s5_b_re', 'v_ev_s5_b_im', 'v_ev_s5_c_re', 'v_ev_s5_c_im', 'v_ev_s5_d', 'v_ev_s5_glu_w', 'v_ev_s5_glu_b', 'v_ev_conv_w', 'v_ev_conv_b', 'v_ev_conv_ln_g', 'v_ev_conv_ln_b', 'v_ev_w_out', 'v_od_norm_g', 'v_od_w_in', 'v_od_rel_bias', 'v_od_w_out', 'v_xa_norm_g', 'v_xa_w_qg', 'v_xa_w_kv', 'v_xa_w_o', 'v_final_norm_g']
TWIN_OUTPUTS = ['loss', 'grad_x', 'grad_mem_norm_g', 'grad_ev_norm_g', 'grad_ev_w_in', 'grad_ev_s5_lambda_re', 'grad_ev_s5_lambda_im', 'grad_ev_s5_log_dt', 'grad_ev_s5_b_re', 'grad_ev_s5_b_im', 'grad_ev_s5_c_re', 'grad_ev_s5_c_im', 'grad_ev_s5_d', 'grad_ev_s5_glu_w', 'grad_ev_s5_glu_b', 'grad_ev_conv_w', 'grad_ev_conv_b', 'grad_ev_conv_ln_g', 'grad_ev_conv_ln_b', 'grad_ev_w_out', 'grad_od_norm_g', 'grad_od_w_in', 'grad_od_rel_bias', 'grad_od_w_out', 'grad_xa_norm_g', 'grad_xa_w_qg', 'grad_xa_w_kv', 'grad_xa_w_o', 'grad_final_norm_g', 'delta_mem_norm_g', 'delta_ev_norm_g', 'delta_ev_w_in', 'delta_ev_s5_lambda_re', 'delta_ev_s5_lambda_im', 'delta_ev_s5_log_dt', 'delta_ev_s5_b_re', 'delta_ev_s5_b_im', 'delta_ev_s5_c_re', 'delta_ev_s5_c_im', 'delta_ev_s5_d', 'delta_ev_s5_glu_w', 'delta_ev_s5_glu_b', 'delta_ev_conv_w', 'delta_ev_conv_b', 'delta_ev_conv_ln_g', 'delta_ev_conv_ln_b', 'delta_ev_w_out', 'delta_od_norm_g', 'delta_od_w_in', 'delta_od_rel_bias', 'delta_od_w_out', 'delta_xa_norm_g', 'delta_xa_w_qg', 'delta_xa_w_kv', 'delta_xa_w_o', 'delta_final_norm_g', 'new_m_mem_norm_g', 'new_m_ev_norm_g', 'new_m_ev_w_in', 'new_m_ev_s5_lambda_re', 'new_m_ev_s5_lambda_im', 'new_m_ev_s5_log_dt', 'new_m_ev_s5_b_re', 'new_m_ev_s5_b_im', 'new_m_ev_s5_c_re', 'new_m_ev_s5_c_im', 'new_m_ev_s5_d', 'new_m_ev_s5_glu_w', 'new_m_ev_s5_glu_b', 'new_m_ev_conv_w', 'new_m_ev_conv_b', 'new_m_ev_conv_ln_g', 'new_m_ev_conv_ln_b', 'new_m_ev_w_out', 'new_m_od_norm_g', 'new_m_od_w_in', 'new_m_od_rel_bias', 'new_m_od_w_out', 'new_m_xa_norm_g', 'new_m_xa_w_qg', 'new_m_xa_w_kv', 'new_m_xa_w_o', 'new_m_final_norm_g', 'new_v_mem_norm_g', 'new_v_ev_norm_g', 'new_v_ev_w_in', 'new_v_ev_s5_lambda_re', 'new_v_ev_s5_lambda_im', 'new_v_ev_s5_log_dt', 'new_v_ev_s5_b_re', 'new_v_ev_s5_b_im', 'new_v_ev_s5_c_re', 'new_v_ev_s5_c_im', 'new_v_ev_s5_d', 'new_v_ev_s5_glu_w', 'new_v_ev_s5_glu_b', 'new_v_ev_conv_w', 'new_v_ev_conv_b', 'new_v_ev_conv_ln_g', 'new_v_ev_conv_ln_b', 'new_v_ev_w_out', 'new_v_od_norm_g', 'new_v_od_w_in', 'new_v_od_rel_bias', 'new_v_od_w_out', 'new_v_xa_norm_g', 'new_v_xa_w_qg', 'new_v_xa_w_kv', 'new_v_xa_w_o', 'new_v_final_norm_g']
TWIN_LEAF_KINDS = {'loss': 'loss', 'grad_x': 'grad_x', 'grad_mem_norm_g': 'grad_w', 'grad_ev_norm_g': 'grad_w', 'grad_ev_w_in': 'grad_w', 'grad_ev_s5_lambda_re': 'grad_w', 'grad_ev_s5_lambda_im': 'grad_w', 'grad_ev_s5_log_dt': 'grad_w', 'grad_ev_s5_b_re': 'grad_w', 'grad_ev_s5_b_im': 'grad_w', 'grad_ev_s5_c_re': 'grad_w', 'grad_ev_s5_c_im': 'grad_w', 'grad_ev_s5_d': 'grad_w', 'grad_ev_s5_glu_w': 'grad_w', 'grad_ev_s5_glu_b': 'grad_w', 'grad_ev_conv_w': 'grad_w', 'grad_ev_conv_b': 'grad_w', 'grad_ev_conv_ln_g': 'grad_w', 'grad_ev_conv_ln_b': 'grad_w', 'grad_ev_w_out': 'grad_w', 'grad_od_norm_g': 'grad_w', 'grad_od_w_in': 'grad_w', 'grad_od_rel_bias': 'grad_w', 'grad_od_w_out': 'grad_w', 'grad_xa_norm_g': 'grad_w', 'grad_xa_w_qg': 'grad_w', 'grad_xa_w_kv': 'grad_w', 'grad_xa_w_o': 'grad_w', 'grad_final_norm_g': 'grad_w', 'delta_mem_norm_g': 'delta_w', 'delta_ev_norm_g': 'delta_w', 'delta_ev_w_in': 'delta_w', 'delta_ev_s5_lambda_re': 'delta_w', 'delta_ev_s5_lambda_im': 'delta_w', 'delta_ev_s5_log_dt': 'delta_w', 'delta_ev_s5_b_re': 'delta_w', 'delta_ev_s5_b_im': 'delta_w', 'delta_ev_s5_c_re': 'delta_w', 'delta_ev_s5_c_im': 'delta_w', 'delta_ev_s5_d': 'delta_w', 'delta_ev_s5_glu_w': 'delta_w', 'delta_ev_s5_glu_b': 'delta_w', 'delta_ev_conv_w': 'delta_w', 'delta_ev_conv_b': 'delta_w', 'delta_ev_conv_ln_g': 'delta_w', 'delta_ev_conv_ln_b': 'delta_w', 'delta_ev_w_out': 'delta_w', 'delta_od_norm_g': 'delta_w', 'delta_od_w_in': 'delta_w', 'delta_od_rel_bias': 'delta_w', 'delta_od_w_out': 'delta_w', 'delta_xa_norm_g': 'delta_w', 'delta_xa_w_qg': 'delta_w', 'delta_xa_w_kv': 'delta_w', 'delta_xa_w_o': 'delta_w', 'delta_final_norm_g': 'delta_w', 'new_m_mem_norm_g': 'new_m', 'new_m_ev_norm_g': 'new_m', 'new_m_ev_w_in': 'new_m', 'new_m_ev_s5_lambda_re': 'new_m', 'new_m_ev_s5_lambda_im': 'new_m', 'new_m_ev_s5_log_dt': 'new_m', 'new_m_ev_s5_b_re': 'new_m', 'new_m_ev_s5_b_im': 'new_m', 'new_m_ev_s5_c_re': 'new_m', 'new_m_ev_s5_c_im': 'new_m', 'new_m_ev_s5_d': 'new_m', 'new_m_ev_s5_glu_w': 'new_m', 'new_m_ev_s5_glu_b': 'new_m', 'new_m_ev_conv_w': 'new_m', 'new_m_ev_conv_b': 'new_m', 'new_m_ev_conv_ln_g': 'new_m', 'new_m_ev_conv_ln_b': 'new_m', 'new_m_ev_w_out': 'new_m', 'new_m_od_norm_g': 'new_m', 'new_m_od_w_in': 'new_m', 'new_m_od_rel_bias': 'new_m', 'new_m_od_w_out': 'new_m', 'new_m_xa_norm_g': 'new_m', 'new_m_xa_w_qg': 'new_m', 'new_m_xa_w_kv': 'new_m', 'new_m_xa_w_o': 'new_m', 'new_m_final_norm_g': 'new_m', 'new_v_mem_norm_g': 'new_v', 'new_v_ev_norm_g': 'new_v', 'new_v_ev_w_in': 'new_v', 'new_v_ev_s5_lambda_re': 'new_v', 'new_v_ev_s5_lambda_im': 'new_v', 'new_v_ev_s5_log_dt': 'new_v', 'new_v_ev_s5_b_re': 'new_v', 'new_v_ev_s5_b_im': 'new_v', 'new_v_ev_s5_c_re': 'new_v', 'new_v_ev_s5_c_im': 'new_v', 'new_v_ev_s5_d': 'new_v', 'new_v_ev_s5_glu_w': 'new_v', 'new_v_ev_s5_glu_b': 'new_v', 'new_v_ev_conv_w': 'new_v', 'new_v_ev_conv_b': 'new_v', 'new_v_ev_conv_ln_g': 'new_v', 'new_v_ev_conv_ln_b': 'new_v', 'new_v_ev_w_out': 'new_v', 'new_v_od_norm_g': 'new_v', 'new_v_od_w_in': 'new_v', 'new_v_od_rel_bias': 'new_v', 'new_v_od_w_out': 'new_v', 'new_v_xa_norm_g': 'new_v', 'new_v_xa_w_qg': 'new_v', 'new_v_xa_w_kv': 'new_v', 'new_v_xa_w_o': 'new_v', 'new_v_final_norm_g': 'new_v'}


def _forward(args):
    return _fwd_reference(*[args[k] for k in FWD_PARAMS])


def _output_shape():
    def fwd():
        inp = _fwd_setup_inputs(0)
        return _fwd_reference(*[inp[k] for k in FWD_PARAMS])
    out = _jax.eval_shape(fwd)
    return out.shape, out.dtype

N_MICROBATCH = 1
ADAM_LR = 0.001
ADAM_B1 = 0.9
ADAM_B2 = 0.999
ADAM_EPS = 1e-08
ADAM_WD = 0.01
ADAM_STEP = 10
PER_EXAMPLE_BATCH_AXIS = {'x': 0, 'mem': 0, 'loss_target': 0}
SHARED_INPUTS = []
_WEIGHT_DTYPES = {'mem_norm_g': _jnp.float32, 'ev_norm_g': _jnp.float32, 'ev_w_in': _jnp.float32, 'ev_s5_lambda_re': _jnp.float32, 'ev_s5_lambda_im': _jnp.float32, 'ev_s5_log_dt': _jnp.float32, 'ev_s5_b_re': _jnp.float32, 'ev_s5_b_im': _jnp.float32, 'ev_s5_c_re': _jnp.float32, 'ev_s5_c_im': _jnp.float32, 'ev_s5_d': _jnp.float32, 'ev_s5_glu_w': _jnp.float32, 'ev_s5_glu_b': _jnp.float32, 'ev_conv_w': _jnp.float32, 'ev_conv_b': _jnp.float32, 'ev_conv_ln_g': _jnp.float32, 'ev_conv_ln_b': _jnp.float32, 'ev_w_out': _jnp.float32, 'od_norm_g': _jnp.float32, 'od_w_in': _jnp.float32, 'od_rel_bias': _jnp.float32, 'od_w_out': _jnp.float32, 'xa_norm_g': _jnp.float32, 'xa_w_qg': _jnp.float32, 'xa_w_kv': _jnp.float32, 'xa_w_o': _jnp.float32, 'final_norm_g': _jnp.float32}
MOMENT_SCALE = {'mem_norm_g': 3.416314e-02, 'ev_norm_g': 1.061319e-01, 'ev_w_in': 6.786721e-02, 'ev_s5_lambda_re': 4.558937e-03, 'ev_s5_lambda_im': 4.503303e-03, 'ev_s5_log_dt': 2.660675e+00, 'ev_s5_b_re': 2.447005e-03, 'ev_s5_b_im': 2.500464e-03, 'ev_s5_c_re': 3.525304e-03, 'ev_s5_c_im': 3.589881e-03, 'ev_s5_d': 5.399586e-02, 'ev_s5_glu_w': 1.421924e-02, 'ev_s5_glu_b': 2.248135e-02, 'ev_conv_w': 9.015025e-02, 'ev_conv_b': 1.964339e-01, 'ev_conv_ln_g': 1.060205e-01, 'ev_conv_ln_b': 9.230009e-02, 'ev_w_out': 7.167856e-02, 'od_norm_g': 4.267540e-02, 'od_w_in': 2.140684e-02, 'od_rel_bias': 9.636745e-03, 'od_w_out': 2.208684e-02, 'xa_norm_g': 2.232295e-02, 'xa_w_qg': 1.541828e-02, 'xa_w_kv': 1.520858e-02, 'xa_w_o': 1.532816e-02, 'final_norm_g': 6.398363e+01}


def _to_microbatches(a, axis):
    t = _jnp.moveaxis(a, axis, 0)
    t = t.reshape((N_MICROBATCH, t.shape[0] // N_MICROBATCH) + t.shape[1:])
    return _jnp.moveaxis(t, 1, axis + 1)


def setup_inputs(seed: int = 0) -> dict:
    inp = _fwd_setup_inputs(seed)
    key = _jax.random.fold_in(_jax.random.key(seed), 7919)
    shape, _ = _output_shape()
    out = dict(inp)
    out["loss_target"] = _jax.random.normal(_jax.random.fold_in(key, 0), shape, _jnp.float32)
    for i, name in enumerate(TWIN_WEIGHTS):
        w = inp[name].astype(_jnp.float32)
        if MOMENT_SCALE is None:
            s = _jnp.sqrt(_jnp.mean(_jnp.square(w)) + 1e-30)
        else:
            s = MOMENT_SCALE[name]
        km, kv = _jax.random.split(_jax.random.fold_in(key, i + 1))
        out[name] = w
        out["m_" + name] = s * _jax.random.normal(km, w.shape, _jnp.float32)
        out["v_" + name] = (s * s) * _jax.random.uniform(kv, w.shape, _jnp.float32, 0.5, 1.5)
    if N_MICROBATCH > 1:
        for name, axis in PER_EXAMPLE_BATCH_AXIS.items():
            out[name] = _to_microbatches(out[name], axis)
    return {'x': out['x'], 'mem': out['mem'], 'mem_norm_g': out['mem_norm_g'], 'ev_norm_g': out['ev_norm_g'], 'ev_w_in': out['ev_w_in'], 'ev_s5_lambda_re': out['ev_s5_lambda_re'], 'ev_s5_lambda_im': out['ev_s5_lambda_im'], 'ev_s5_log_dt': out['ev_s5_log_dt'], 'ev_s5_b_re': out['ev_s5_b_re'], 'ev_s5_b_im': out['ev_s5_b_im'], 'ev_s5_c_re': out['ev_s5_c_re'], 'ev_s5_c_im': out['ev_s5_c_im'], 'ev_s5_d': out['ev_s5_d'], 'ev_s5_glu_w': out['ev_s5_glu_w'], 'ev_s5_glu_b': out['ev_s5_glu_b'], 'ev_conv_w': out['ev_conv_w'], 'ev_conv_b': out['ev_conv_b'], 'ev_conv_ln_g': out['ev_conv_ln_g'], 'ev_conv_ln_b': out['ev_conv_ln_b'], 'ev_w_out': out['ev_w_out'], 'od_norm_g': out['od_norm_g'], 'od_w_in': out['od_w_in'], 'od_rel_bias': out['od_rel_bias'], 'od_w_out': out['od_w_out'], 'xa_norm_g': out['xa_norm_g'], 'xa_w_qg': out['xa_w_qg'], 'xa_w_kv': out['xa_w_kv'], 'xa_w_o': out['xa_w_o'], 'final_norm_g': out['final_norm_g'], 'loss_target': out['loss_target'], 'm_mem_norm_g': out['m_mem_norm_g'], 'm_ev_norm_g': out['m_ev_norm_g'], 'm_ev_w_in': out['m_ev_w_in'], 'm_ev_s5_lambda_re': out['m_ev_s5_lambda_re'], 'm_ev_s5_lambda_im': out['m_ev_s5_lambda_im'], 'm_ev_s5_log_dt': out['m_ev_s5_log_dt'], 'm_ev_s5_b_re': out['m_ev_s5_b_re'], 'm_ev_s5_b_im': out['m_ev_s5_b_im'], 'm_ev_s5_c_re': out['m_ev_s5_c_re'], 'm_ev_s5_c_im': out['m_ev_s5_c_im'], 'm_ev_s5_d': out['m_ev_s5_d'], 'm_ev_s5_glu_w': out['m_ev_s5_glu_w'], 'm_ev_s5_glu_b': out['m_ev_s5_glu_b'], 'm_ev_conv_w': out['m_ev_conv_w'], 'm_ev_conv_b': out['m_ev_conv_b'], 'm_ev_conv_ln_g': out['m_ev_conv_ln_g'], 'm_ev_conv_ln_b': out['m_ev_conv_ln_b'], 'm_ev_w_out': out['m_ev_w_out'], 'm_od_norm_g': out['m_od_norm_g'], 'm_od_w_in': out['m_od_w_in'], 'm_od_rel_bias': out['m_od_rel_bias'], 'm_od_w_out': out['m_od_w_out'], 'm_xa_norm_g': out['m_xa_norm_g'], 'm_xa_w_qg': out['m_xa_w_qg'], 'm_xa_w_kv': out['m_xa_w_kv'], 'm_xa_w_o': out['m_xa_w_o'], 'm_final_norm_g': out['m_final_norm_g'], 'v_mem_norm_g': out['v_mem_norm_g'], 'v_ev_norm_g': out['v_ev_norm_g'], 'v_ev_w_in': out['v_ev_w_in'], 'v_ev_s5_lambda_re': out['v_ev_s5_lambda_re'], 'v_ev_s5_lambda_im': out['v_ev_s5_lambda_im'], 'v_ev_s5_log_dt': out['v_ev_s5_log_dt'], 'v_ev_s5_b_re': out['v_ev_s5_b_re'], 'v_ev_s5_b_im': out['v_ev_s5_b_im'], 'v_ev_s5_c_re': out['v_ev_s5_c_re'], 'v_ev_s5_c_im': out['v_ev_s5_c_im'], 'v_ev_s5_d': out['v_ev_s5_d'], 'v_ev_s5_glu_w': out['v_ev_s5_glu_w'], 'v_ev_s5_glu_b': out['v_ev_s5_glu_b'], 'v_ev_conv_w': out['v_ev_conv_w'], 'v_ev_conv_b': out['v_ev_conv_b'], 'v_ev_conv_ln_g': out['v_ev_conv_ln_g'], 'v_ev_conv_ln_b': out['v_ev_conv_ln_b'], 'v_ev_w_out': out['v_ev_w_out'], 'v_od_norm_g': out['v_od_norm_g'], 'v_od_w_in': out['v_od_w_in'], 'v_od_rel_bias': out['v_od_rel_bias'], 'v_od_w_out': out['v_od_w_out'], 'v_xa_norm_g': out['v_xa_norm_g'], 'v_xa_w_qg': out['v_xa_w_qg'], 'v_xa_w_kv': out['v_xa_w_kv'], 'v_xa_w_o': out['v_xa_w_o'], 'v_final_norm_g': out['v_final_norm_g']}


def _loss(weights, diff, rest, loss_target):
    with _jax.named_scope("forward"):
        args = {**rest, TWIN_DIFF_INPUT: diff, **{k: w.astype(_WEIGHT_DTYPES[k]) for k, w in weights.items()}}
        y = _forward(args)
    with _jax.named_scope("loss_head"):
        err = _jnp.square(y.astype(_jnp.float32) - loss_target)
        return 0.5 * _jnp.sum(_jnp.mean(err, axis=-1)) if err.ndim else 0.5 * err


def _adamw(w, g, m, v):
    m = ADAM_B1 * m + (1.0 - ADAM_B1) * g
    v = ADAM_B2 * v + (1.0 - ADAM_B2) * _jnp.square(g)
    m_hat = m / (1.0 - ADAM_B1 ** ADAM_STEP)
    v_hat = v / (1.0 - ADAM_B2 ** ADAM_STEP)
    delta = -ADAM_LR * (m_hat / (_jnp.sqrt(v_hat) + ADAM_EPS) + ADAM_WD * w)
    return delta, m, v


def reference(x, mem, mem_norm_g, ev_norm_g, ev_w_in, ev_s5_lambda_re, ev_s5_lambda_im, ev_s5_log_dt, ev_s5_b_re, ev_s5_b_im, ev_s5_c_re, ev_s5_c_im, ev_s5_d, ev_s5_glu_w, ev_s5_glu_b, ev_conv_w, ev_conv_b, ev_conv_ln_g, ev_conv_ln_b, ev_w_out, od_norm_g, od_w_in, od_rel_bias, od_w_out, xa_norm_g, xa_w_qg, xa_w_kv, xa_w_o, final_norm_g, loss_target, m_mem_norm_g, m_ev_norm_g, m_ev_w_in, m_ev_s5_lambda_re, m_ev_s5_lambda_im, m_ev_s5_log_dt, m_ev_s5_b_re, m_ev_s5_b_im, m_ev_s5_c_re, m_ev_s5_c_im, m_ev_s5_d, m_ev_s5_glu_w, m_ev_s5_glu_b, m_ev_conv_w, m_ev_conv_b, m_ev_conv_ln_g, m_ev_conv_ln_b, m_ev_w_out, m_od_norm_g, m_od_w_in, m_od_rel_bias, m_od_w_out, m_xa_norm_g, m_xa_w_qg, m_xa_w_kv, m_xa_w_o, m_final_norm_g, v_mem_norm_g, v_ev_norm_g, v_ev_w_in, v_ev_s5_lambda_re, v_ev_s5_lambda_im, v_ev_s5_log_dt, v_ev_s5_b_re, v_ev_s5_b_im, v_ev_s5_c_re, v_ev_s5_c_im, v_ev_s5_d, v_ev_s5_glu_w, v_ev_s5_glu_b, v_ev_conv_w, v_ev_conv_b, v_ev_conv_ln_g, v_ev_conv_ln_b, v_ev_w_out, v_od_norm_g, v_od_w_in, v_od_rel_bias, v_od_w_out, v_xa_norm_g, v_xa_w_qg, v_xa_w_kv, v_xa_w_o, v_final_norm_g):
    given = dict(x=x, mem=mem, mem_norm_g=mem_norm_g, ev_norm_g=ev_norm_g, ev_w_in=ev_w_in, ev_s5_lambda_re=ev_s5_lambda_re, ev_s5_lambda_im=ev_s5_lambda_im, ev_s5_log_dt=ev_s5_log_dt, ev_s5_b_re=ev_s5_b_re, ev_s5_b_im=ev_s5_b_im, ev_s5_c_re=ev_s5_c_re, ev_s5_c_im=ev_s5_c_im, ev_s5_d=ev_s5_d, ev_s5_glu_w=ev_s5_glu_w, ev_s5_glu_b=ev_s5_glu_b, ev_conv_w=ev_conv_w, ev_conv_b=ev_conv_b, ev_conv_ln_g=ev_conv_ln_g, ev_conv_ln_b=ev_conv_ln_b, ev_w_out=ev_w_out, od_norm_g=od_norm_g, od_w_in=od_w_in, od_rel_bias=od_rel_bias, od_w_out=od_w_out, xa_norm_g=xa_norm_g, xa_w_qg=xa_w_qg, xa_w_kv=xa_w_kv, xa_w_o=xa_w_o, final_norm_g=final_norm_g, loss_target=loss_target, m_mem_norm_g=m_mem_norm_g, m_ev_norm_g=m_ev_norm_g, m_ev_w_in=m_ev_w_in, m_ev_s5_lambda_re=m_ev_s5_lambda_re, m_ev_s5_lambda_im=m_ev_s5_lambda_im, m_ev_s5_log_dt=m_ev_s5_log_dt, m_ev_s5_b_re=m_ev_s5_b_re, m_ev_s5_b_im=m_ev_s5_b_im, m_ev_s5_c_re=m_ev_s5_c_re, m_ev_s5_c_im=m_ev_s5_c_im, m_ev_s5_d=m_ev_s5_d, m_ev_s5_glu_w=m_ev_s5_glu_w, m_ev_s5_glu_b=m_ev_s5_glu_b, m_ev_conv_w=m_ev_conv_w, m_ev_conv_b=m_ev_conv_b, m_ev_conv_ln_g=m_ev_conv_ln_g, m_ev_conv_ln_b=m_ev_conv_ln_b, m_ev_w_out=m_ev_w_out, m_od_norm_g=m_od_norm_g, m_od_w_in=m_od_w_in, m_od_rel_bias=m_od_rel_bias, m_od_w_out=m_od_w_out, m_xa_norm_g=m_xa_norm_g, m_xa_w_qg=m_xa_w_qg, m_xa_w_kv=m_xa_w_kv, m_xa_w_o=m_xa_w_o, m_final_norm_g=m_final_norm_g, v_mem_norm_g=v_mem_norm_g, v_ev_norm_g=v_ev_norm_g, v_ev_w_in=v_ev_w_in, v_ev_s5_lambda_re=v_ev_s5_lambda_re, v_ev_s5_lambda_im=v_ev_s5_lambda_im, v_ev_s5_log_dt=v_ev_s5_log_dt, v_ev_s5_b_re=v_ev_s5_b_re, v_ev_s5_b_im=v_ev_s5_b_im, v_ev_s5_c_re=v_ev_s5_c_re, v_ev_s5_c_im=v_ev_s5_c_im, v_ev_s5_d=v_ev_s5_d, v_ev_s5_glu_w=v_ev_s5_glu_w, v_ev_s5_glu_b=v_ev_s5_glu_b, v_ev_conv_w=v_ev_conv_w, v_ev_conv_b=v_ev_conv_b, v_ev_conv_ln_g=v_ev_conv_ln_g, v_ev_conv_ln_b=v_ev_conv_ln_b, v_ev_w_out=v_ev_w_out, v_od_norm_g=v_od_norm_g, v_od_w_in=v_od_w_in, v_od_rel_bias=v_od_rel_bias, v_od_w_out=v_od_w_out, v_xa_norm_g=v_xa_norm_g, v_xa_w_qg=v_xa_w_qg, v_xa_w_kv=v_xa_w_kv, v_xa_w_o=v_xa_w_o, v_final_norm_g=v_final_norm_g)
    weights = {n: given[n] for n in TWIN_WEIGHTS}
    shared = {n: given[n] for n in SHARED_INPUTS}
    per_example = {n: given[n] for n in ['x', 'mem']}
    grad_fn = _jax.value_and_grad(_loss, argnums=(0, 1))

    def one_microbatch(ex, loss_target):
        ex = dict(ex)
        diff = ex.pop(TWIN_DIFF_INPUT)
        return grad_fn(weights, diff, {**shared, **ex}, loss_target)

    if N_MICROBATCH == 1:
        loss, (grad_w, grad_x) = one_microbatch(per_example, given["loss_target"])
    else:
        def body(carry, xs):
            loss_sum, grad_sum = carry
            l_k, (gw_k, gx_k) = one_microbatch(xs[0], xs[1])
            with _jax.named_scope("update"):
                return (loss_sum + l_k, _jax.tree.map(_jnp.add, grad_sum, gw_k)), gx_k

        init = (_jnp.zeros((), _jnp.float32), _jax.tree.map(_jnp.zeros_like, weights))
        (loss, grad_w), grad_x = _jax.lax.scan(body, init, (per_example, given["loss_target"]))
    with _jax.named_scope("update"):
        delta_w, new_m, new_v = {}, {}, {}
        for n in TWIN_WEIGHTS:
            delta_w[n], new_m[n], new_v[n] = _adamw(weights[n], grad_w[n], given["m_" + n], given["v_" + n])
    return (loss, grad_x, *[grad_w[n] for n in TWIN_WEIGHTS], *[delta_w[n] for n in TWIN_WEIGHTS],
            *[new_m[n] for n in TWIN_WEIGHTS], *[new_v[n] for n in TWIN_WEIGHTS])
```

```python
import functools
import math

import jax
import jax.numpy as jnp
import numpy as np
from jax import lax
from jax.experimental import pallas as pl
from jax.experimental.pallas import tpu as pltpu

F32 = jnp.float32
BF16 = jnp.bfloat16

D_MODEL = 1024
CHUNK = 64
LEFT_CHUNKS = 8
S5_WIDTH = 512
S5_GROUP = 16
S5_GROUPS = 32
S5_STATE = 64
S5_COLS = S5_GROUPS * S5_STATE
S5_SPLIT = 4
S5_CC = S5_COLS // S5_SPLIT
S5_UC = S5_WIDTH // S5_SPLIT
CONV_WIDTH = 512
CONV_KERNEL = 31
CONV_HALO = 32
ATT_HEADS = 16
ATT_HEAD_DIM = 64
MAX_REL = 128
MEM_LEN = 256
XA_HEADS = 4
XA_HEAD_DIM = 256
EPS = 1e-6
EVEN_IN = 2560
ODD_IN = 4096

ADAM_LR = 0.001
ADAM_B1 = 0.9
ADAM_B2 = 0.999
ADAM_EPS = 1e-08
ADAM_WD = 0.01
ADAM_STEP = 10

ROW_TILE = 256
ATT_QB = 256
ATT_PAD = LEFT_CHUNKS * CHUNK
ATT_WIN = ATT_PAD + ATT_QB
VMEM_LIMIT_V7X = 56 * 1024 * 1024
NEG = -1e30

MESH = pl.DeviceIdType.MESH


def _cp(*sem, vmem=VMEM_LIMIT_V7X):
    return pltpu.CompilerParams(dimension_semantics=sem if sem else None, vmem_limit_bytes=vmem)


def _full(shape):
    n = len(shape)
    return pl.BlockSpec(shape, lambda *_: (0,) * n)


LANES = 128


def _lane_tile(n, cap):
    return max(t for t in range(LANES, min(n, cap) + 1, LANES) if n % t == 0)


def _sigmoid(x):
    return 1.0 / (1.0 + jnp.exp(-x))


def _silu(x):
    return x * _sigmoid(x)


def _dsilu(x):
    s = _sigmoid(x)
    return s * (1.0 + x * (1.0 - s))


_GELU_C = math.sqrt(2.0 / math.pi)


def _gelu(x):
    return 0.5 * x * (1.0 + jnp.tanh(_GELU_C * (x + 0.044715 * x * x * x)))


def _dgelu(x):
    t = jnp.tanh(_GELU_C * (x + 0.044715 * x * x * x))
    return 0.5 * (1.0 + t) + 0.5 * x * (1.0 - t * t) * _GELU_C * (1.0 + 3.0 * 0.044715 * x * x)


def _dot(a, b):
    return jnp.dot(a, b, preferred_element_type=F32)


def _dot_nt(a, b):
    return lax.dot_general(a, b, (((1,), (1,)), ((), ())), preferred_element_type=F32)


def _dot_tn(a, b):
    return lax.dot_general(a, b, (((0,), (0,)), ((), ())), preferred_element_type=F32)


def _rms_parts(xv):
    inv = lax.rsqrt(jnp.mean(xv * xv, axis=-1, keepdims=True) + EPS)
    return inv, xv * inv


def _rms_bwd(xv, g, dh):
    inv, xhat = _rms_parts(xv)
    dg = jnp.sum(dh * xhat, axis=0, keepdims=True)
    dxh = dh * g
    dx = inv * (dxh - xhat * jnp.mean(dxh * xhat, axis=-1, keepdims=True))
    return dx, dg


def norm_mm(x, g, w, splits, name, with_h=True, tm=ROW_TILE):
    M, D = x.shape
    tm = min(tm, M)

    def body(x_ref, g_ref, w_ref, *outs):
        _, xhat = _rms_parts(x_ref[...])
        hb = (xhat * g_ref[...]).astype(BF16)
        for o, (s, n, dt) in zip(outs, splits):
            o[...] = _dot(hb, w_ref[:, s:s + n]).astype(dt)
        if with_h:
            outs[-1][...] = hb

    out_shape = [jax.ShapeDtypeStruct((M, n), dt) for (_, n, dt) in splits]
    out_specs = [pl.BlockSpec((tm, n), lambda i: (i, 0)) for (_, n, _) in splits]
    if with_h:
        out_shape.append(jax.ShapeDtypeStruct((M, D), BF16))
        out_specs.append(pl.BlockSpec((tm, D), lambda i: (i, 0)))
    return pl.pallas_call(
        body, name=name, grid=(M // tm,),
        in_specs=[pl.BlockSpec((tm, D), lambda i: (i, 0)), _full(g.shape), _full(w.shape)],
        out_specs=out_specs, out_shape=out_shape, compiler_params=_cp("parallel"),
    )(x, g, w)


def mm_res(a, w, res, name, tm=ROW_TILE):
    M, K = a.shape
    N = w.shape[1]
    tm = min(tm, M)

    def body(a_ref, w_ref, r_ref, o_ref):
        o_ref[...] = r_ref[...] + _dot(a_ref[...].astype(BF16), w_ref[...])

    return pl.pallas_call(
        body, name=name, grid=(M // tm,),
        in_specs=[pl.BlockSpec((tm, K), lambda i: (i, 0)), _full(w.shape), pl.BlockSpec((tm, N), lambda i: (i, 0))],
        out_specs=pl.BlockSpec((tm, N), lambda i: (i, 0)),
        out_shape=jax.ShapeDtypeStruct((M, N), F32), compiler_params=_cp("parallel"),
    )(a, w, res)


def mm_plain(a, w, name, out_dtype=F32, tm=ROW_TILE):
    M, K = a.shape
    N = w.shape[1]
    tm = min(tm, M)

    def body(a_ref, w_ref, o_ref):
        o_ref[...] = _dot(a_ref[...].astype(BF16), w_ref[...].astype(BF16)).astype(out_dtype)

    return pl.pallas_call(
        body, name=name, grid=(M // tm,),
        in_specs=[pl.BlockSpec((tm, K), lambda i: (i, 0)), _full(w.shape)],
        out_specs=pl.BlockSpec((tm, N), lambda i: (i, 0)),
        out_shape=jax.ShapeDtypeStruct((M, N), out_dtype), compiler_params=_cp("parallel"),
    )(a, w)


def mm_nt(dy, w, name, out_dtype=F32, tm=ROW_TILE):
    M, N = dy.shape
    K = w.shape[0]
    tm = min(tm, M)

    def body(d_ref, w_ref, o_ref):
        o_ref[...] = _dot_nt(d_ref[...].astype(BF16), w_ref[...].astype(BF16)).astype(out_dtype)

    return pl.pallas_call(
        body, name=name, grid=(M // tm,),
        in_specs=[pl.BlockSpec((tm, N), lambda i: (i, 0)), _full(w.shape)],
        out_specs=pl.BlockSpec((tm, K), lambda i: (i, 0)),
        out_shape=jax.ShapeDtypeStruct((M, K), out_dtype), compiler_params=_cp("parallel"),
    )(dy, w)


def mm_nt_normbwd(dy, w, x, g, dx_out, name, tm=ROW_TILE):
    M, N = dy.shape
    D = w.shape[0]
    tm = min(tm, M)

    def body(d_ref, w_ref, x_ref, g_ref, dxo_ref, dx_ref, dg_ref):
        dh = _dot_nt(d_ref[...].astype(BF16), w_ref[...])
        dx, dg = _rms_bwd(x_ref[...], g_ref[...], dh)
        dx_ref[...] = dxo_ref[...] + dx

        @pl.when(pl.program_id(0) == 0)
        def _():
            dg_ref[...] = jnp.zeros_like(dg_ref)

        dg_ref[...] += dg

    row = lambda n: pl.BlockSpec((tm, n), lambda i: (i, 0))
    return pl.pallas_call(
        body, name=name, grid=(M // tm,),
        in_specs=[row(N), _full(w.shape), row(D), _full(g.shape), row(D)],
        out_specs=[row(D), _full((1, D))],
        out_shape=[jax.ShapeDtypeStruct((M, D), F32), jax.ShapeDtypeStruct((1, D), F32)],
        compiler_params=_cp("arbitrary"),
    )(dy, w, x, g, dx_out)


def mm_tn(a, b, name, bm=512, bn=1024, bl=512):
    L, K = a.shape
    N = b.shape[1]
    bm, bn, bl = _lane_tile(K, bm), _lane_tile(N, bn), min(bl, L)
    nl = L // bl

    def body(a_ref, b_ref, o_ref, acc):
        l = pl.program_id(2)

        @pl.when(l == 0)
        def _():
            acc[...] = jnp.zeros_like(acc)

        acc[...] += _dot_tn(a_ref[...].astype(BF16), b_ref[...].astype(BF16))

        @pl.when(l == nl - 1)
        def _():
            o_ref[...] = acc[...]

    return pl.pallas_call(
        body, name=name, grid=(K // bm, N // bn, nl),
        in_specs=[pl.BlockSpec((bl, bm), lambda i, j, l: (l, i)), pl.BlockSpec((bl, bn), lambda i, j, l: (l, j))],
        out_specs=pl.BlockSpec((bm, bn), lambda i, j, l: (i, j)),
        out_shape=jax.ShapeDtypeStruct((K, N), F32),
        scratch_shapes=[pltpu.VMEM((bm, bn), F32)],
        compiler_params=_cp("parallel", "parallel", "arbitrary"),
    )(a, b)


def rms_fwd(x, g, name):
    def body(x_ref, g_ref, o_ref, ob_ref):
        _, xhat = _rms_parts(x_ref[...])
        y = xhat * g_ref[...]
        o_ref[...] = y
        ob_ref[...] = y.astype(BF16)

    return pl.pallas_call(
        body, name=name,
        out_shape=[jax.ShapeDtypeStruct(x.shape, F32), jax.ShapeDtypeStruct(x.shape, BF16)],
    )(x, g)


def rms_dgain(x, dy0, dy1, name):
    def body(x_ref, d0_ref, d1_ref, o_ref):
        _, xhat = _rms_parts(x_ref[...])
        o_ref[...] = jnp.sum((d0_ref[...] + d1_ref[...]) * xhat, axis=0, keepdims=True)

    return pl.pallas_call(body, name=name, out_shape=jax.ShapeDtypeStruct((1, x.shape[1]), F32))(x, dy0, dy1)


def _s5_discretise(lr, li, logdt, bt_re, bt_im):
    dt = jnp.exp(logdt)
    mag = jnp.exp(lr * dt)
    ab_re = mag * jnp.cos(li * dt)
    ab_im = mag * jnp.sin(li * dt)
    den = lr * lr + li * li
    nr = ab_re - 1.0
    coef_re = (nr * lr + ab_im * li) / den
    coef_im = (ab_im * lr - nr * li) / den
    cr = coef_re[:, None, :]
    ci = coef_im[:, None, :]
    bb_re = cr * bt_re - ci * bt_im
    bb_im = cr * bt_im + ci * bt_re
    return ab_re, ab_im, bb_re, bb_im


def s5_param_fwd(lr, li, logdt, bt_re, bt_im):
    def body(lr_ref, li_ref, ld_ref, br_ref, bi_ref, bbr_ref, bbi_ref):
        _, _, bb_re, bb_im = _s5_discretise(lr_ref[...], li_ref[...], ld_ref[...], br_ref[...], bi_ref[...])
        bbr_ref[...] = bb_re
        bbi_ref[...] = bb_im

    sh = jax.ShapeDtypeStruct(bt_re.shape, F32)
    return pl.pallas_call(body, name="s5_param_fwd", out_shape=[sh, sh])(lr, li, logdt, bt_re, bt_im)


def s5_param_bwd(lr, li, logdt, bt_re, bt_im, d_ab_re, d_ab_im, d_bb_re, d_bb_im):
    def body(lr_ref, li_ref, ld_ref, br_ref, bi_ref, dar_ref, dai_ref, dbr_ref, dbi_ref,
             o_lr, o_li, o_ld, o_br, o_bi):
        _, vjp = jax.vjp(_s5_discretise, lr_ref[...], li_ref[...], ld_ref[...], br_ref[...], bi_ref[...])
        g = vjp((dar_ref[...], dai_ref[...], dbr_ref[...], dbi_ref[...]))
        for o, v in zip((o_lr, o_li, o_ld, o_br, o_bi), g):
            o[...] = v

    shapes = [jax.ShapeDtypeStruct(a.shape, F32) for a in (lr, li, logdt, bt_re, bt_im)]
    return pl.pallas_call(body, name="s5_param_bwd", out_shape=shapes)(
        lr, li, logdt, bt_re, bt_im, d_ab_re, d_ab_im, d_bb_re, d_bb_im)


def s5_tables(lr_flat, li_flat, logdt_flat):
    def body(lr_ref, li_ref, ld_ref, tab_ref):
        dt = jnp.exp(ld_ref[...])
        a = lr_ref[...] * dt
        th = li_ref[...] * dt
        row = lax.broadcasted_iota(jnp.int32, (8, 1), 0)
        rowf = row.astype(F32)

        def power(e, sign):
            m = jnp.exp(e * a)
            return m * jnp.cos(e * th), sign * m * jnp.sin(e * th)

        k = 0
        for sign, fwd in ((1.0, True), (-1.0, False)):
            for s in (1, 2, 4):
                pr, pi = power(jnp.full((8, 1), float(s), F32), sign)
                keep = (row >= s) if fwd else (row + s < 8)
                tab_ref[k] = jnp.where(keep, pr, 0.0)
                tab_ref[k + 1] = jnp.where(keep, pi, 0.0)
                k += 2
            e = rowf + 1.0 if fwd else 8.0 - rowf
            pr, pi = power(e, sign)
            tab_ref[k] = pr
            tab_ref[k + 1] = pi
            k += 2

    return pl.pallas_call(body, name="s5_tables",
                          out_shape=jax.ShapeDtypeStruct((16, 8, S5_COLS), F32))(lr_flat, li_flat, logdt_flat)


def _scan_block(a, b, tabs, base, cr, ci, reverse):
    for n, s in enumerate((1, 2, 4)):
        mr = tabs[base + 2 * n]
        mi = tabs[base + 2 * n + 1]
        sh = (8 - s) if reverse else s
        ar = pltpu.roll(a, sh, 0)
        br = pltpu.roll(b, sh, 0)
        a, b = a + mr * ar - mi * br, b + mr * br + mi * ar
    pr = tabs[base + 6]
    pi = tabs[base + 7]
    a, b = a + pr * cr - pi * ci, b + pr * ci + pi * cr
    return a, b


def s5_fwd(z, bbd_re, bbd_im, ccd_re, ccd_im, tab, dskip, tm=ROW_TILE):
    L = z.shape[0]
    tm = min(tm, L)
    nt = L // tm

    def body(u_ref, bbr_ref, bbi_ref, ccr_ref, cci_ref, tab_ref, d_ref, y_ref, ck_ref, xr, xi, car):
        t = pl.program_id(1)

        @pl.when(t == 0)
        def _():
            car[...] = jnp.zeros_like(car)

        u = u_ref[...]
        ub = u.astype(BF16)
        xr[...] = _dot(ub, bbr_ref[...])
        xi[...] = _dot(ub, bbi_ref[...])
        tabs = [tab_ref[k] for k in range(8)]

        def blk(i, c):
            r0 = pl.multiple_of(i * 8, 8)
            a, b = _scan_block(xr[pl.ds(r0, 8), :], xi[pl.ds(r0, 8), :], tabs, 0, c[0], c[1], False)
            xr[pl.ds(r0, 8), :] = a
            xi[pl.ds(r0, 8), :] = b
            return a[7:8, :], b[7:8, :]

        cr, ci = lax.fori_loop(0, tm // 8, blk, (car[0:1, :], car[1:2, :]))
        car[0:1, :] = cr
        car[1:2, :] = ci
        ck_ref[0:1, :] = cr
        ck_ref[1:2, :] = ci
        y_ref[...] = (_dot(xr[...].astype(BF16), ccr_ref[...]) - _dot(xi[...].astype(BF16), cci_ref[...])
                      + d_ref[...] * u)

    return pl.pallas_call(
        body, name="s5_fwd", grid=(S5_SPLIT, nt),
        in_specs=[pl.BlockSpec((tm, S5_UC), lambda j, t: (t, j)),
                  pl.BlockSpec((None, S5_UC, S5_CC), lambda j, t: (j, 0, 0)),
                  pl.BlockSpec((None, S5_UC, S5_CC), lambda j, t: (j, 0, 0)),
                  pl.BlockSpec((None, S5_CC, S5_UC), lambda j, t: (j, 0, 0)),
                  pl.BlockSpec((None, S5_CC, S5_UC), lambda j, t: (j, 0, 0)),
                  pl.BlockSpec((8, 8, S5_CC), lambda j, t: (0, 0, j)),
                  pl.BlockSpec((1, S5_UC), lambda j, t: (0, j))],
        out_specs=[pl.BlockSpec((tm, S5_UC), lambda j, t: (t, j)),
                   pl.BlockSpec((None, 2, S5_CC), lambda j, t: (t, 0, j))],
        out_shape=[jax.ShapeDtypeStruct((L, S5_WIDTH), F32), jax.ShapeDtypeStruct((nt, 2, S5_COLS), F32)],
        scratch_shapes=[pltpu.VMEM((tm, S5_CC), F32), pltpu.VMEM((tm, S5_CC), F32), pltpu.VMEM((2, S5_CC), F32)],
        compiler_params=_cp("parallel", "arbitrary"),
    )(z, bbd_re, bbd_im, ccd_re, ccd_im, tab, dskip)


def s5_bwd(z, dy, ckpt, bbd_re, bbd_im, ccd_re, ccd_im, tab, dskip, tm=ROW_TILE):
    L = z.shape[0]
    tm = min(tm, L)
    nt = L // tm

    def body(u_ref, dy_ref, ck_ref, bbr_ref, bbi_ref, ccr_ref, cci_ref, tab_ref, d_ref,
             du_ref, da_ref, dbr_ref, dbi_ref, dcr_ref, dci_ref, dd_ref, hr, hi, gr, gi, car, acr, aci):
        t = pl.program_id(1)
        tt = nt - 1 - t

        @pl.when(t == 0)
        def _():
            for r in (car, acr, aci, dbr_ref, dbi_ref, dcr_ref, dci_ref, dd_ref):
                r[...] = jnp.zeros_like(r)

        u = u_ref[...]
        ub = u.astype(BF16)
        dyv = dy_ref[...]
        dyb = dyv.astype(BF16)
        tabs = [tab_ref[k] for k in range(16)]

        live = (tt > 0).astype(F32)
        c0r = ck_ref[0:1, :] * live
        c0i = ck_ref[1:2, :] * live
        hr[0:8, :] = jnp.broadcast_to(c0r, (8, S5_CC))
        hi[0:8, :] = jnp.broadcast_to(c0i, (8, S5_CC))
        hr[8:, :] = _dot(ub, bbr_ref[...])
        hi[8:, :] = _dot(ub, bbi_ref[...])

        def fblk(i, c):
            r0 = pl.multiple_of(i * 8 + 8, 8)
            a, b = _scan_block(hr[pl.ds(r0, 8), :], hi[pl.ds(r0, 8), :], tabs, 0, c[0], c[1], False)
            hr[pl.ds(r0, 8), :] = a
            hi[pl.ds(r0, 8), :] = b
            return a[7:8, :], b[7:8, :]

        lax.fori_loop(0, tm // 8, fblk, (c0r, c0i))
        hrb = hr[8:, :].astype(BF16)
        hib = hi[8:, :].astype(BF16)
        dcr_ref[...] += _dot_tn(hrb, dyb)
        dci_ref[...] -= _dot_tn(hib, dyb)

        gr[...] = _dot_nt(dyb, ccr_ref[...])
        gi[...] = -_dot_nt(dyb, cci_ref[...])
        row0 = lax.broadcasted_iota(jnp.int32, (8, S5_CC), 0) == 0

        def rblk(k, c):
            i = tm // 8 - 1 - k
            r0 = pl.multiple_of(i * 8, 8)
            a, b = _scan_block(gr[pl.ds(r0, 8), :], gi[pl.ds(r0, 8), :], tabs, 8, c[0], c[1], True)
            gr[pl.ds(r0, 8), :] = a
            gi[pl.ds(r0, 8), :] = b
            r1 = pl.multiple_of(i * 8 + 8, 8)
            hpr = jnp.where(row0, pltpu.roll(hr[pl.ds(r0, 8), :], 1, 0), pltpu.roll(hr[pl.ds(r1, 8), :], 1, 0))
            hpi = jnp.where(row0, pltpu.roll(hi[pl.ds(r0, 8), :], 1, 0), pltpu.roll(hi[pl.ds(r1, 8), :], 1, 0))
            acr[...] += a * hpr + b * hpi
            aci[...] += b * hpr - a * hpi
            return a[0:1, :], b[0:1, :]

        cr, ci = lax.fori_loop(0, tm // 8, rblk, (car[0:1, :], car[1:2, :]))
        car[0:1, :] = cr
        car[1:2, :] = ci

        grb = gr[...].astype(BF16)
        gib = gi[...].astype(BF16)
        du_ref[...] = _dot_nt(grb, bbr_ref[...]) + _dot_nt(gib, bbi_ref[...]) + d_ref[...] * dyv
        dbr_ref[...] += _dot_tn(ub, grb)
        dbi_ref[...] += _dot_tn(ub, gib)
        dd_ref[...] += jnp.sum(dyv * u, axis=0, keepdims=True)

        @pl.when(t == nt - 1)
        def _():
            da_ref[0:1, :] = jnp.sum(acr[...], axis=0, keepdims=True)
            da_ref[1:2, :] = jnp.sum(aci[...], axis=0, keepdims=True)

    chunk = lambda a, b: pl.BlockSpec((None, a, b), lambda j, t: (j, 0, 0))
    return pl.pallas_call(
        body, name="s5_bwd", grid=(S5_SPLIT, nt),
        in_specs=[pl.BlockSpec((tm, S5_UC), lambda j, t: (nt - 1 - t, j)),
                  pl.BlockSpec((tm, S5_UC), lambda j, t: (nt - 1 - t, j)),
                  pl.BlockSpec((None, 2, S5_CC), lambda j, t: (jnp.maximum(nt - 2 - t, 0), 0, j)),
                  chunk(S5_UC, S5_CC), chunk(S5_UC, S5_CC), chunk(S5_CC, S5_UC), chunk(S5_CC, S5_UC),
                  pl.BlockSpec((16, 8, S5_CC), lambda j, t: (0, 0, j)),
                  pl.BlockSpec((1, S5_UC), lambda j, t: (0, j))],
        out_specs=[pl.BlockSpec((tm, S5_UC), lambda j, t: (nt - 1 - t, j)),
                   pl.BlockSpec((None, 2, S5_CC), lambda j, t: (j, 0, 0)),
                   chunk(S5_UC, S5_CC), chunk(S5_UC, S5_CC), chunk(S5_CC, S5_UC), chunk(S5_CC, S5_UC),
                   pl.BlockSpec((1, S5_UC), lambda j, t: (0, j))],
        out_shape=[jax.ShapeDtypeStruct((L, S5_WIDTH), F32),
                   jax.ShapeDtypeStruct((S5_SPLIT, 2, S5_CC), F32),
                   jax.ShapeDtypeStruct((S5_SPLIT, S5_UC, S5_CC), F32),
                   jax.ShapeDtypeStruct((S5_SPLIT, S5_UC, S5_CC), F32),
                   jax.ShapeDtypeStruct((S5_SPLIT, S5_CC, S5_UC), F32),
                   jax.ShapeDtypeStruct((S5_SPLIT, S5_CC, S5_UC), F32),
                   jax.ShapeDtypeStruct((1, S5_WIDTH), F32)],
        scratch_shapes=[pltpu.VMEM((tm + 8, S5_CC), F32), pltpu.VMEM((tm + 8, S5_CC), F32),
                        pltpu.VMEM((tm, S5_CC), F32), pltpu.VMEM((tm, S5_CC), F32),
                        pltpu.VMEM((2, S5_CC), F32), pltpu.VMEM((8, S5_CC), F32), pltpu.VMEM((8, S5_CC), F32)],
        compiler_params=_cp("parallel", "arbitrary"),
    )(z, dy, ckpt, bbd_re, bbd_im, ccd_re, ccd_im, tab, dskip)


_EYE8 = np.eye(S5_GROUPS // S5_SPLIT, dtype=np.float32)


def _blockdiag(a):
    g, r, c = a.shape
    a = a.reshape(S5_SPLIT, g // S5_SPLIT, r, c)
    out = a[:, :, :, None, :] * _EYE8[None, :, None, :, None].astype(a.dtype)
    return out.reshape(S5_SPLIT, (g // S5_SPLIT) * r, (g // S5_SPLIT) * c)


def _blockdiag_extract(a, r, c):
    n = S5_GROUPS // S5_SPLIT
    a = a.reshape(S5_SPLIT, n, r, n, c)
    d = jnp.einsum("jgrhc,gh->jgrc", a, _EYE8)
    return d.reshape(S5_GROUPS, r, c)


def s5_mixer_core_fwd(z, lam_re, lam_im, log_dt, b_re, b_im, c_re, c_im, d_skip):
    bt_re = jnp.swapaxes(b_re, 1, 2)
    bt_im = jnp.swapaxes(b_im, 1, 2)
    logdt = log_dt.reshape(S5_GROUPS, 1)
    bb_re, bb_im = s5_param_fwd(lam_re, lam_im, logdt, bt_re, bt_im)
    flat = lambda a: a.reshape(1, S5_COLS)
    tab = s5_tables(flat(lam_re), flat(lam_im), flat(jnp.broadcast_to(logdt, (S5_GROUPS, S5_STATE))))
    bbd_re = _blockdiag(bb_re).astype(BF16)
    bbd_im = _blockdiag(bb_im).astype(BF16)
    ccd_re = _blockdiag(jnp.swapaxes(c_re, 1, 2)).astype(BF16)
    ccd_im = _blockdiag(jnp.swapaxes(c_im, 1, 2)).astype(BF16)
    dsk = d_skip.reshape(1, S5_WIDTH)
    y, ckpt = s5_fwd(z, bbd_re, bbd_im, ccd_re, ccd_im, tab, dsk)
    saved = (logdt, bt_re, bt_im, bbd_re, bbd_im, ccd_re, ccd_im, tab, dsk, ckpt)
    return y, saved


def s5_mixer_core_bwd(z, dy, lam_re, lam_im, saved):
    logdt, bt_re, bt_im, bbd_re, bbd_im, ccd_re, ccd_im, tab, dsk, ckpt = saved
    du, da, dbr, dbi, dcr, dci, dd = s5_bwd(z, dy, ckpt, bbd_re, bbd_im, ccd_re, ccd_im, tab, dsk)
    d_ab_re = da[:, 0, :].reshape(S5_GROUPS, S5_STATE)
    d_ab_im = da[:, 1, :].reshape(S5_GROUPS, S5_STATE)
    d_bb_re = _blockdiag_extract(dbr, S5_GROUP, S5_STATE)
    d_bb_im = _blockdiag_extract(dbi, S5_GROUP, S5_STATE)
    g_lr, g_li, g_ld, g_btr, g_bti = s5_param_bwd(lam_re, lam_im, logdt, bt_re, bt_im,
                                                  d_ab_re, d_ab_im, d_bb_re, d_bb_im)
    g_cre = jnp.swapaxes(_blockdiag_extract(dcr, S5_STATE, S5_GROUP), 1, 2)
    g_cim = jnp.swapaxes(_blockdiag_extract(dci, S5_STATE, S5_GROUP), 1, 2)
    grads = dict(lambda_re=g_lr, lambda_im=g_li, log_dt=g_ld.reshape(S5_GROUPS),
                 b_re=jnp.swapaxes(g_btr, 1, 2), b_im=jnp.swapaxes(g_bti, 1, 2),
                 c_re=g_cre, c_im=g_cim, d=dd.reshape(S5_WIDTH))
    return du, grads


Z_U, Z_GA, Z_VAL, Z_GLU, Z_GB = range(5)


def conv_fwd(z, conv_w, conv_b, tm=ROW_TILE):
    L = z.shape[0]
    tm = min(tm, L)
    nt = L // tm
    hb = tm // CONV_HALO
    C = CONV_WIDTH

    def body(val_ref, glu_ref, valh_ref, gluh_ref, w_ref, b_ref, c_ref, vbuf):
        live = (pl.program_id(0) > 0).astype(F32)
        vbuf[0:CONV_HALO, :] = valh_ref[...] * _sigmoid(gluh_ref[...]) * live
        vbuf[CONV_HALO:, :] = val_ref[...] * _sigmoid(glu_ref[...])
        acc = jnp.broadcast_to(b_ref[...], (tm, C))
        for k in range(CONV_KERNEL):
            acc = acc + w_ref[k:k + 1, :] * vbuf[pl.ds(CONV_HALO - CONV_KERNEL + 1 + k, tm), :]
        c_ref[...] = acc

    cur = lambda col: pl.BlockSpec((tm, C), lambda t: (t, col))
    prev = lambda col: pl.BlockSpec((CONV_HALO, C), lambda t: (jnp.maximum(t * hb - 1, 0), col))
    return pl.pallas_call(
        body, name="conv_fwd", grid=(nt,),
        in_specs=[cur(Z_VAL), cur(Z_GLU), prev(Z_VAL), prev(Z_GLU), _full(conv_w.shape), _full(conv_b.shape)],
        out_specs=pl.BlockSpec((tm, C), lambda t: (t, 0)),
        out_shape=jax.ShapeDtypeStruct((L, C), F32),
        scratch_shapes=[pltpu.VMEM((tm + CONV_HALO, C), F32)],
        compiler_params=_cp("parallel"),
    )(z, z, z, z, conv_w, conv_b)


def conv_bwd(z, dc, conv_w, tm=ROW_TILE):
    L = z.shape[0]
    tm = min(tm, L)
    nt = L // tm
    hb = tm // CONV_HALO
    nh = L // CONV_HALO
    C = CONV_WIDTH
    off = CONV_HALO - CONV_KERNEL + 1

    def body(val_ref, glu_ref, valh_ref, gluh_ref, dc_ref, dcn_ref, w_ref, dval_ref, dglu_ref, dw_ref, db_ref,
             vbuf, dbuf):
        t = pl.program_id(0)

        @pl.when(t == 0)
        def _():
            dw_ref[...] = jnp.zeros_like(dw_ref)
            db_ref[...] = jnp.zeros_like(db_ref)

        val = val_ref[...]
        sg = _sigmoid(glu_ref[...])
        vbuf[0:CONV_HALO, :] = valh_ref[...] * _sigmoid(gluh_ref[...]) * (t > 0).astype(F32)
        vbuf[CONV_HALO:, :] = val * sg
        dcv = dc_ref[...]
        dbuf[0:tm, :] = dcv
        dbuf[tm:, :] = dcn_ref[...] * (t < nt - 1).astype(F32)
        dv = jnp.zeros((tm, C), F32)
        for k in range(CONV_KERNEL):
            dv = dv + w_ref[k:k + 1, :] * dbuf[pl.ds(CONV_KERNEL - 1 - k, tm), :]
            dw_ref[k:k + 1, :] += jnp.sum(dcv * vbuf[pl.ds(off + k, tm), :], axis=0, keepdims=True)
        db_ref[...] += jnp.sum(dcv, axis=0, keepdims=True)
        dval_ref[...] = dv * sg
        dglu_ref[...] = dv * val * sg * (1.0 - sg)

    cur = lambda col: pl.BlockSpec((tm, C), lambda t: (t, col))
    prev = lambda col: pl.BlockSpec((CONV_HALO, C), lambda t: (jnp.maximum(t * hb - 1, 0), col))
    nxt = pl.BlockSpec((CONV_HALO, C), lambda t: (jnp.minimum((t + 1) * hb, nh - 1), 0))
    row = pl.BlockSpec((tm, C), lambda t: (t, 0))
    return pl.pallas_call(
        body, name="conv_bwd", grid=(nt,),
        in_specs=[cur(Z_VAL), cur(Z_GLU), prev(Z_VAL), prev(Z_GLU), row, nxt, _full(conv_w.shape)],
        out_specs=[row, row, _full((CONV_HALO, C)), _full((1, C))],
        out_shape=[jax.ShapeDtypeStruct((L, C), F32), jax.ShapeDtypeStruct((L, C), F32),
                   jax.ShapeDtypeStruct((CONV_HALO, C), F32), jax.ShapeDtypeStruct((1, C), F32)],
        scratch_shapes=[pltpu.VMEM((tm + CONV_HALO, C), F32), pltpu.VMEM((tm + CONV_HALO, C), F32)],
        compiler_params=_cp("arbitrary"),
    )(z, z, z, z, dc, dc, conv_w)


def _ln_parts(c):
    mu = jnp.mean(c, axis=-1, keepdims=True)
    cc = c - mu
    rstd = lax.rsqrt(jnp.mean(cc * cc, axis=-1, keepdims=True) + EPS)
    return rstd, cc * rstd


def _ev_tail_branches(ys, ga, c, gb, wglu, bglu, lng, lnb):
    z1 = _gelu(ys)
    z1b = z1.astype(BF16)
    sg = _sigmoid(_dot(z1b, wglu) + bglu)
    out = z1 * sg
    rstd, chat = _ln_parts(c)
    cn = chat * lng + lnb
    return z1, z1b, sg, out, rstd, chat, cn


def ev_tail_fwd(ys, z, c, x0, wglu, bglu, lng, lnb, wout, tm=ROW_TILE):
    L, D = x0.shape
    tm = min(tm, L)
    W = S5_WIDTH

    def body(ys_ref, ga_ref, c_ref, gb_ref, x_ref, wglu_ref, bglu_ref, lng_ref, lnb_ref, wout_ref, o_ref):
        _, _, _, out, _, _, cn = _ev_tail_branches(ys_ref[...], ga_ref[...], c_ref[...], gb_ref[...],
                                                   wglu_ref[...], bglu_ref[...], lng_ref[...], lnb_ref[...])
        ya = (out * _silu(ga_ref[...])).astype(BF16)
        yb = (_silu(cn) * _silu(gb_ref[...])).astype(BF16)
        o_ref[...] = x_ref[...] + _dot(ya, wout_ref[0:W, :]) + _dot(yb, wout_ref[W:, :])

    row = lambda n, col=0: pl.BlockSpec((tm, n), lambda t: (t, col))
    return pl.pallas_call(
        body, name="ev_tail_fwd", grid=(L // tm,),
        in_specs=[row(W), row(W, Z_GA), row(W), row(W, Z_GB), row(D), _full(wglu.shape), _full(bglu.shape),
                  _full(lng.shape), _full(lnb.shape), _full(wout.shape)],
        out_specs=row(D), out_shape=jax.ShapeDtypeStruct((L, D), F32), compiler_params=_cp("parallel"),
    )(ys, z, c, z, x0, wglu, bglu, lng, lnb, wout)


def ev_tail_bwd(ys, z, c, dx1, wglu, bglu, lng, lnb, wout, tm=ROW_TILE):
    L, D = dx1.shape
    tm = min(tm, L)
    W = S5_WIDTH

    def body(ys_ref, ga_ref, c_ref, gb_ref, dx_ref, wglu_ref, bglu_ref, lng_ref, lnb_ref, wout_ref,
             dys_ref, dc_ref, dga_ref, dgb_ref, r_ref, z1_ref, dt_ref, dbg_ref, dlg_ref, dlb_ref):
        @pl.when(pl.program_id(0) == 0)
        def _():
            for r in (dbg_ref, dlg_ref, dlb_ref):
                r[...] = jnp.zeros_like(r)

        ys, ga, gb = ys_ref[...], ga_ref[...], gb_ref[...]
        z1, z1b, sg, out, rstd, chat, cn = _ev_tail_branches(ys, ga, c_ref[...], gb, wglu_ref[...], bglu_ref[...],
                                                             lng_ref[...], lnb_ref[...])
        sga, sgb, scn = _silu(ga), _silu(gb), _silu(cn)
        r_ref[:, 0:W] = (out * sga).astype(BF16)
        r_ref[:, W:] = (scn * sgb).astype(BF16)
        dxb = dx_ref[...].astype(BF16)
        dra = _dot_nt(dxb, wout_ref[0:W, :])
        drb = _dot_nt(dxb, wout_ref[W:, :])
        dga_ref[...] = dra * out * _dsilu(ga)
        dout = dra * sga
        dt = dout * z1 * sg * (1.0 - sg)
        dtb = dt.astype(BF16)
        dz1 = dout * sg + _dot_nt(dtb, wglu_ref[...])
        dys_ref[...] = dz1 * _dgelu(ys)
        z1_ref[...] = z1b
        dt_ref[...] = dtb
        dbg_ref[...] += jnp.sum(dt, axis=0, keepdims=True)
        dgb_ref[...] = drb * scn * _dsilu(gb)
        dcn = drb * sgb * _dsilu(cn)
        dlg_ref[...] += jnp.sum(dcn * chat, axis=0, keepdims=True)
        dlb_ref[...] += jnp.sum(dcn, axis=0, keepdims=True)
        dch = dcn * lng_ref[...]
        dc_ref[...] = rstd * (dch - jnp.mean(dch, axis=-1, keepdims=True)
                              - chat * jnp.mean(dch * chat, axis=-1, keepdims=True))

    row = lambda n, col=0: pl.BlockSpec((tm, n), lambda t: (t, col))
    f = lambda n, dt: jax.ShapeDtypeStruct((L, n), dt)
    vec = jax.ShapeDtypeStruct((1, W), F32)
    return pl.pallas_call(
        body, name="ev_tail_bwd", grid=(L // tm,),
        in_specs=[row(W), row(W, Z_GA), row(W), row(W, Z_GB), row(D), _full(wglu.shape), _full(bglu.shape),
                  _full(lng.shape), _full(lnb.shape), _full(wout.shape)],
        out_specs=[row(W), row(W), row(W), row(W), row(D), row(W), row(W), _full((1, W)), _full((1, W)), _full((1, W))],
        out_shape=[f(W, F32), f(W, F32), f(W, F32), f(W, F32), f(D, BF16), f(W, BF16), f(W, BF16), vec, vec, vec],
        compiler_params=_cp("arbitrary"),
    )(ys, z, c, z, dx1, wglu, bglu, lng, lnb, wout)


XA_SCALE = XA_HEAD_DIM ** -0.5


def _xa_forward(xv, g, wqg, k, v):
    _, xhat = _rms_parts(xv)
    hb = (xhat * g).astype(BF16)
    q = _dot(hb, wqg[:, 0:D_MODEL])
    gate = _dot(hb, wqg[:, D_MODEL:])
    qb = q.astype(BF16)
    ps, os_ = [], []
    for h in range(XA_HEADS):
        sl = slice(h * XA_HEAD_DIM, (h + 1) * XA_HEAD_DIM)
        s = _dot_nt(qb[:, sl], k[:, sl]) * XA_SCALE
        e = jnp.exp(s - jnp.max(s, axis=-1, keepdims=True))
        p = e / jnp.sum(e, axis=-1, keepdims=True)
        ps.append(p)
        os_.append(_dot(p.astype(BF16), v[:, sl]))
    return hb, qb, gate, ps, jnp.concatenate(os_, axis=1)


def xa_fwd(x, g, wqg, k, v, wo, name, tm=ROW_TILE):
    L, D = x.shape
    tm = min(tm, L)

    def body(x_ref, g_ref, wqg_ref, k_ref, v_ref, wo_ref, o_ref):
        xv = x_ref[...]
        _, _, gate, _, o = _xa_forward(xv, g_ref[...], wqg_ref, k_ref[...], v_ref[...])
        o_ref[...] = xv + _dot((o * _silu(gate)).astype(BF16), wo_ref[...])

    row = pl.BlockSpec((tm, D), lambda t: (t, 0))
    return pl.pallas_call(
        body, name=name, grid=(L // tm,),
        in_specs=[row, _full(g.shape), _full(wqg.shape), _full(k.shape), _full(v.shape), _full(wo.shape)],
        out_specs=row, out_shape=jax.ShapeDtypeStruct((L, D), F32), compiler_params=_cp("parallel"),
    )(x, g, wqg, k, v, wo)


def xa_bwd(x, dxo, g, wqg, k, v, wo, name, tm=ROW_TILE):
    L, D = x.shape
    tm = min(tm, L)

    def body(x_ref, dxo_ref, g_ref, wqg_ref, k_ref, v_ref, wo_ref,
             dx_ref, dqg_ref, h_ref, r_ref, dk_ref, dv_ref, dg_ref):
        @pl.when(pl.program_id(0) == 0)
        def _():
            for r in (dk_ref, dv_ref, dg_ref):
                r[...] = jnp.zeros_like(r)

        xv = x_ref[...]
        kv_, vv = k_ref[...], v_ref[...]
        hb, qb, gate, ps, o = _xa_forward(xv, g_ref[...], wqg_ref, kv_, vv)
        sgate = _silu(gate)
        h_ref[...] = hb
        r_ref[...] = (o * sgate).astype(BF16)
        dxo = dxo_ref[...]
        dr = _dot_nt(dxo.astype(BF16), wo_ref[...])
        do = dr * sgate
        dqg_ref[:, D:] = (dr * o * _dsilu(gate)).astype(BF16)
        dob = do.astype(BF16)
        for h in range(XA_HEADS):
            sl = slice(h * XA_HEAD_DIM, (h + 1) * XA_HEAD_DIM)
            p = ps[h]
            pb = p.astype(BF16)
            dp = _dot_nt(dob[:, sl], vv[:, sl])
            dv_ref[:, sl] += _dot_tn(pb, dob[:, sl])
            ds = p * (dp - jnp.sum(dp * p, axis=-1, keepdims=True))
            dsb = (ds * XA_SCALE).astype(BF16)
            dqg_ref[:, sl] = _dot(dsb, kv_[:, sl]).astype(BF16)
            dk_ref[:, sl] += _dot_tn(dsb, qb[:, sl])
        dh = _dot_nt(dqg_ref[...], wqg_ref[...])
        dx, dg = _rms_bwd(xv, g_ref[...], dh)
        dx_ref[...] = dxo + dx
        dg_ref[...] += dg

    row = lambda n: pl.BlockSpec((tm, n), lambda t: (t, 0))
    kv_shape = jax.ShapeDtypeStruct(k.shape, F32)
    return pl.pallas_call(
        body, name=name, grid=(L // tm,),
        in_specs=[row(D), row(D), _full(g.shape), _full(wqg.shape), _full(k.shape), _full(v.shape), _full(wo.shape)],
        out_specs=[row(D), row(2 * D), row(D), row(D), _full(k.shape), _full(v.shape), _full((1, D))],
        out_shape=[jax.ShapeDtypeStruct((L, D), F32), jax.ShapeDtypeStruct((L, 2 * D), BF16),
                   jax.ShapeDtypeStruct((L, D), BF16), jax.ShapeDtypeStruct((L, D), BF16),
                   kv_shape, kv_shape, jax.ShapeDtypeStruct((1, D), F32)],
        compiler_params=_cp("arbitrary"),
    )(x, dxo, g, wqg, k, v, wo)


ATT_SCALE = ATT_HEAD_DIM ** -0.5
ATT_PAIRS = ATT_HEADS // 2


def _rel_index():
    qi = np.arange(ATT_QB)[:, None]
    kj = np.arange(ATT_WIN)[None, :]
    dist = qi + ATT_PAD - kj
    idx = np.clip(dist, -MAX_REL, MAX_REL) + MAX_REL
    dc = (qi // CHUNK + LEFT_CHUNKS) - kj // CHUNK
    return idx.astype(np.int32), (dc >= 0) & (dc <= LEFT_CHUNKS)


def att_bias(rel_bias):
    idx, band = _rel_index()
    return jnp.where(band[None], rel_bias[:, idx], NEG)


def _ca_scores(qh, kw, bias, kvalid):
    s = _dot_nt(qh, kw) * ATT_SCALE + bias
    s = jnp.where(kvalid, s, NEG)
    e = jnp.exp(s - jnp.max(s, axis=-1, keepdims=True))
    return e / jnp.sum(e, axis=-1, keepdims=True)


def ca_fwd(q, kvp, gate, bias):
    L, D = q.shape
    Lp = kvp.shape[0]
    nb = L // ATT_QB

    def body(q_ref, k_ref, v_ref, g_ref, b_ref, r_ref):
        w = pl.multiple_of(pl.program_id(1) * ATT_QB, ATT_QB)
        kw = k_ref[pl.ds(w, ATT_WIN), :]
        vw = v_ref[pl.ds(w, ATT_WIN), :]
        qv = q_ref[...]
        first = lax.broadcasted_iota(jnp.int32, (1, 128), 1) < ATT_HEAD_DIM
        kvalid = (w + lax.broadcasted_iota(jnp.int32, (1, ATT_WIN), 1)) >= ATT_PAD
        outs = []
        for hh, m in enumerate((first, jnp.logical_not(first))):
            p = _ca_scores(jnp.where(m, qv, jnp.zeros_like(qv)), kw, b_ref[hh], kvalid)
            outs.append(_dot(p.astype(BF16), vw))
        o = jnp.where(first, outs[0], outs[1])
        r_ref[...] = (o * _silu(g_ref[...])).astype(BF16)

    blk = pl.BlockSpec((ATT_QB, 128), lambda hp, b: (b, hp))
    return pl.pallas_call(
        body, name="ca_fwd", grid=(ATT_PAIRS, nb),
        in_specs=[blk, pl.BlockSpec((Lp, 128), lambda hp, b: (0, hp)),
                  pl.BlockSpec((Lp, 128), lambda hp, b: (0, ATT_PAIRS + hp)), blk,
                  pl.BlockSpec((2, ATT_QB, ATT_WIN), lambda hp, b: (hp, 0, 0))],
        out_specs=blk, out_shape=jax.ShapeDtypeStruct((L, D), BF16),
        compiler_params=_cp("parallel", "arbitrary"),
    )(q, kvp, kvp, gate, bias)


def ca_bwd(q, kvp, gate, bias, dr):
    L, D = q.shape
    Lp = kvp.shape[0]
    nb = L // ATT_QB

    def body(q_ref, k_ref, v_ref, g_ref, b_ref, dr_ref, dq_ref, dg_ref, dk_ref, dv_ref, db_ref):
        b = pl.program_id(1)

        @pl.when(b == 0)
        def _():
            for r in (dk_ref, dv_ref, db_ref):
                r[...] = jnp.zeros_like(r)

        w = pl.multiple_of(b * ATT_QB, ATT_QB)
        kw = k_ref[pl.ds(w, ATT_WIN), :]
        vw = v_ref[pl.ds(w, ATT_WIN), :]
        qv = q_ref[...]
        gate_v = g_ref[...]
        drv = dr_ref[...]
        do = drv * _silu(gate_v)
        first = lax.broadcasted_iota(jnp.int32, (1, 128), 1) < ATT_HEAD_DIM
        kvalid = (w + lax.broadcasted_iota(jnp.int32, (1, ATT_WIN), 1)) >= ATT_PAD
        outs, dqs = [], []
        dkw = jnp.zeros((ATT_WIN, 128), F32)
        dvw = jnp.zeros((ATT_WIN, 128), F32)
        for hh, m in enumerate((first, jnp.logical_not(first))):
            qh = jnp.where(m, qv, jnp.zeros_like(qv))
            p = _ca_scores(qh, kw, b_ref[hh], kvalid)
            pb = p.astype(BF16)
            outs.append(_dot(pb, vw))
            doh = jnp.where(m, do, 0.0).astype(BF16)
            dp = _dot_nt(doh, vw)
            dvw = dvw + _dot_tn(pb, doh)
            ds = p * (dp - jnp.sum(dp * p, axis=-1, keepdims=True))
            db_ref[hh] += ds
            dsb = (ds * ATT_SCALE).astype(BF16)
            dqs.append(_dot(dsb, kw))
            dkw = dkw + _dot_tn(dsb, qh)
        o = jnp.where(first, outs[0], outs[1])
        dg_ref[...] = (drv * o * _dsilu(gate_v)).astype(BF16)
        dq_ref[...] = jnp.where(first, dqs[0], dqs[1]).astype(BF16)
        dk_ref[pl.ds(w, ATT_WIN), :] += dkw
        dv_ref[pl.ds(w, ATT_WIN), :] += dvw

    blk = pl.BlockSpec((ATT_QB, 128), lambda hp, b: (b, hp))
    kblk = pl.BlockSpec((Lp, 128), lambda hp, b: (0, hp))
    vblk = pl.BlockSpec((Lp, 128), lambda hp, b: (0, ATT_PAIRS + hp))
    bblk = pl.BlockSpec((2, ATT_QB, ATT_WIN), lambda hp, b: (hp, 0, 0))
    return pl.pallas_call(
        body, name="ca_bwd", grid=(ATT_PAIRS, nb),
        in_specs=[blk, kblk, vblk, blk, bblk, blk],
        out_specs=[blk, blk, kblk, kblk, bblk],
        out_shape=[jax.ShapeDtypeStruct((L, D), BF16), jax.ShapeDtypeStruct((L, D), BF16),
                   jax.ShapeDtypeStruct((Lp, D), F32), jax.ShapeDtypeStruct((Lp, D), F32),
                   jax.ShapeDtypeStruct(bias.shape, F32)],
        compiler_params=_cp("parallel", "arbitrary"),
    )(q, kvp, kvp, gate, bias, dr)


def relbias_bwd(dbias):
    H = dbias.shape[0]
    WP = 1024
    NR = 384

    def body(x_ref, o_ref):
        x = jnp.concatenate([x_ref[...], jnp.zeros((ATT_QB, WP - ATT_WIN), F32)], axis=1)
        row = lax.broadcasted_iota(jnp.int32, (ATT_QB, 1), 0)
        for bit in range(8):
            x = jnp.where(((row >> bit) & 1) == 1, pltpu.roll(x, 1 << bit, 1), x)
        col = jnp.sum(x, axis=0, keepdims=True)
        j = lax.broadcasted_iota(jnp.int32, (WP, NR), 0)
        r = lax.broadcasted_iota(jnp.int32, (WP, NR), 1)
        dist = (ATT_WIN - 1) - j
        onehot = (jnp.clip(dist, -MAX_REL, MAX_REL) + MAX_REL == r).astype(BF16)
        acc = jnp.zeros((8, NR), F32)
        rem = jnp.broadcast_to(col, (8, WP))
        for _ in range(3):
            part = rem.astype(BF16)
            acc = acc + _dot(part, onehot)
            rem = rem - part.astype(F32)
        o_ref[...] = acc

    flipped = dbias[:, ::-1, :]
    out = pl.pallas_call(
        body, name="relbias_bwd", grid=(H,),
        in_specs=[pl.BlockSpec((None, ATT_QB, ATT_WIN), lambda h: (h, 0, 0))],
        out_specs=pl.BlockSpec((None, 8, NR), lambda h: (h, 0, 0)),
        out_shape=jax.ShapeDtypeStruct((H, 8, NR), F32), compiler_params=_cp("parallel"),
    )(flipped)
    return out[:, 0, :2 * MAX_REL + 1]


def loss_bwd(x, target, g, tm=ROW_TILE):
    L, D = x.shape
    tm = min(tm, L)

    def body(x_ref, t_ref, g_ref, loss_ref, dx_ref, dg_ref):
        @pl.when(pl.program_id(0) == 0)
        def _():
            loss_ref[...] = jnp.zeros_like(loss_ref)
            dg_ref[...] = jnp.zeros_like(dg_ref)

        xv = x_ref[...]
        gv = g_ref[...]
        _, xhat = _rms_parts(xv)
        err = xhat * gv - t_ref[...]
        loss_ref[...] += 0.5 * jnp.sum(jnp.sum(err * err, axis=-1, keepdims=True), axis=0, keepdims=True) / D
        dx, dg = _rms_bwd(xv, gv, err / D)
        dx_ref[...] = dx
        dg_ref[...] += dg

    row = pl.BlockSpec((tm, D), lambda t: (t, 0))
    return pl.pallas_call(
        body, name="loss_bwd", grid=(L // tm,),
        in_specs=[row, row, _full(g.shape)],
        out_specs=[_full((1, 128)), row, _full((1, D))],
        out_shape=[jax.ShapeDtypeStruct((1, 128), F32), jax.ShapeDtypeStruct((L, D), F32),
                   jax.ShapeDtypeStruct((1, D), F32)],
        compiler_params=_cp("arbitrary"),
    )(x, target, g)


def adamw(w, g, m, v, name, tr=512):
    R, C = w.shape
    tr = min(tr, R)
    c1 = 1.0 / (1.0 - ADAM_B1 ** ADAM_STEP)
    c2 = 1.0 / (1.0 - ADAM_B2 ** ADAM_STEP)

    def body(w_ref, g_ref, m_ref, v_ref, d_ref, mo_ref, vo_ref):
        gv = g_ref[...]
        mn = ADAM_B1 * m_ref[...] + (1.0 - ADAM_B1) * gv
        vn = ADAM_B2 * v_ref[...] + (1.0 - ADAM_B2) * gv * gv
        d_ref[...] = -ADAM_LR * ((mn * c1) / (jnp.sqrt(vn * c2) + ADAM_EPS) + ADAM_WD * w_ref[...])
        mo_ref[...] = mn
        vo_ref[...] = vn

    blk = pl.BlockSpec((tr, C), lambda i: (i, 0))
    sh = jax.ShapeDtypeStruct((R, C), F32)
    return pl.pallas_call(
        body, name=name, grid=(R // tr,), in_specs=[blk] * 4, out_specs=[blk] * 3,
        out_shape=[sh] * 3, compiler_params=_cp("parallel"),
    )(w, g, m, v)


def local_step(x, mem, target, p):
    row = lambda a: a.reshape(1, -1)
    D = D_MODEL
    g = {}

    memn, memn_b = rms_fwd(mem, row(p["mem_norm_g"]), "mem_norm")
    kvs = [mm_plain(memn_b, p["xa_w_kv"][l], f"xa_kv{l}", out_dtype=BF16) for l in range(2)]
    ks = [kv[:, :D] for kv in kvs]
    vs = [kv[:, D:] for kv in kvs]

    z, h0b = norm_mm(x, p["ev_norm_g"], p["ev_w_in"][0], [(0, EVEN_IN, F32)], "ev_in")
    ys, s5_saved = s5_mixer_core_fwd(z, p["ev_s5_lambda_re"][0], p["ev_s5_lambda_im"][0], p["ev_s5_log_dt"][0],
                                     p["ev_s5_b_re"][0], p["ev_s5_b_im"][0], p["ev_s5_c_re"][0], p["ev_s5_c_im"][0],
                                     p["ev_s5_d"][0])
    conv_w = p["ev_conv_w"][0]
    c = conv_fwd(z, conv_w, p["ev_conv_b"])
    tail = (p["ev_s5_glu_w"][0], p["ev_s5_glu_b"], p["ev_conv_ln_g"], p["ev_conv_ln_b"], p["ev_w_out"][0])
    x1 = ev_tail_fwd(ys, z, c, x, *tail)
    xa0 = (row(p["xa_norm_g"][0]), p["xa_w_qg"][0], ks[0], vs[0], p["xa_w_o"][0])
    x2 = xa_fwd(x1, *xa0, "xa_fwd0")

    q, kvb, gate, h1b = norm_mm(x2, p["od_norm_g"], p["od_w_in"][0],
                                [(0, D, BF16), (D, 2 * D, BF16), (3 * D, D, F32)], "od_in")
    kvp = jnp.pad(kvb, ((ATT_PAD, 0), (0, 0)))
    bias = att_bias(p["od_rel_bias"][0])
    r = ca_fwd(q, kvp, gate, bias)
    x3 = mm_res(r, p["od_w_out"][0], x2, "od_out")
    xa1 = (row(p["xa_norm_g"][1]), p["xa_w_qg"][1], ks[1], vs[1], p["xa_w_o"][1])
    x4 = xa_fwd(x3, *xa1, "xa_fwd1")

    loss, dx4, dgf = loss_bwd(x4, target, row(p["final_norm_g"]))
    g["final_norm_g"] = dgf.reshape(D)

    dx3, dqg1, hx1, rx1, dk1, dv1, dgxa1 = xa_bwd(x3, dx4, *xa1, "xa_bwd1")
    dwqg1 = mm_tn(hx1, dqg1, "xa_dwqg1")
    dwo1 = mm_tn(rx1, dx4, "xa_dwo1")

    g["od_w_out"] = mm_tn(r, dx3, "od_dwout")[None]
    dr = mm_nt(dx3, p["od_w_out"][0], "od_out_bwd")
    dq, dgate, dkp, dvp, dbias = ca_bwd(q, kvp, gate, bias, dr)
    dqkvg = jnp.concatenate([dq, dkp[ATT_PAD:].astype(BF16), dvp[ATT_PAD:].astype(BF16), dgate], axis=1)
    g["od_w_in"] = mm_tn(h1b, dqkvg, "od_dwin")[None]
    dx2, dgod = mm_nt_normbwd(dqkvg, p["od_w_in"][0], x2, p["od_norm_g"], dx3, "od_in_bwd")
    g["od_norm_g"] = dgod
    g["od_rel_bias"] = relbias_bwd(dbias)[None]

    dx1, dqg0, hx0, rx0, dk0, dv0, dgxa0 = xa_bwd(x1, dx2, *xa0, "xa_bwd0")
    dwqg0 = mm_tn(hx0, dqg0, "xa_dwqg0")
    dwo0 = mm_tn(rx0, dx2, "xa_dwo0")
    g["xa_w_qg"] = jnp.stack([dwqg0, dwqg1])
    g["xa_w_o"] = jnp.stack([dwo0, dwo1])
    g["xa_norm_g"] = jnp.concatenate([dgxa0, dgxa1], axis=0)

    dys, dc, dga, dgb, ra, z1b, dtb, dbglu, dlng, dlnb = ev_tail_bwd(ys, z, c, dx1, *tail)
    g["ev_w_out"] = mm_tn(ra, dx1, "ev_dwout")[None]
    g["ev_s5_glu_w"] = mm_tn(z1b, dtb, "ev_dwglu")[None]
    g["ev_s5_glu_b"], g["ev_conv_ln_g"], g["ev_conv_ln_b"] = dbglu, dlng, dlnb
    dval, dglu, dconvw, dconvb = conv_bwd(z, dc, conv_w)
    g["ev_conv_w"] = dconvw[None, :CONV_KERNEL]
    g["ev_conv_b"] = dconvb
    du, s5g = s5_mixer_core_bwd(z, dys, p["ev_s5_lambda_re"][0], p["ev_s5_lambda_im"][0], s5_saved)
    for n, v in s5g.items():
        g["ev_s5_" + n] = v[None]
    dz = jnp.concatenate([du, dga, dval, dglu, dgb], axis=1).astype(BF16)
    g["ev_w_in"] = mm_tn(h0b, dz, "ev_dwin")[None]
    grad_x, dgev = mm_nt_normbwd(dz, p["ev_w_in"][0], x, p["ev_norm_g"], dx1, "ev_in_bwd")
    g["ev_norm_g"] = dgev

    dkv0 = jnp.concatenate([dk0, dv0], axis=1)
    dkv1 = jnp.concatenate([dk1, dv1], axis=1)
    g["xa_w_kv"] = jnp.stack([mm_tn(memn_b, dkv0, "xa_dwkv0"), mm_tn(memn_b, dkv1, "xa_dwkv1")])
    dmem0 = mm_nt(dkv0, p["xa_w_kv"][0], "xa_kv_bwd0")
    dmem1 = mm_nt(dkv1, p["xa_w_kv"][1], "xa_kv_bwd1")
    g["mem_norm_g"] = rms_dgain(mem, dmem0, dmem1, "mem_norm_bwd").reshape(D)
    return loss, grad_x, g


ANY = pl.BlockSpec(memory_space=pl.ANY)
N_CHIPS = 4
N_DEV = 8


def _me():
    return lax.axis_index("x"), lax.axis_index("y"), lax.axis_index("c")


def _other_chips(x, y):
    return [(1 - x, y), (x, 1 - y), (1 - x, 1 - y)]


def _remote(src, dst, send_sems, recv_sems, k, to):
    return pltpu.make_async_remote_copy(src_ref=src, dst_ref=dst, send_sem=send_sems.at[k], recv_sem=recv_sems.at[k],
                                        device_id=to, device_id_type=MESH)


def allgather_chip_blocks(wp):
    R, C = wp.shape
    H = R // 2

    def body(w_ref, out_ref, send_sems, recv_sems, local_sem):
        x, y, c = _me()
        sib = (x, y, 1 - c)
        chips = _other_chips(x, y)

        def half(px, py, h):
            return out_ref.at[2 * px + py, pl.ds(h * H, H), :]

        mine = pltpu.make_async_copy(w_ref, out_ref.at[2 * x + y], local_sem)
        mine.start()
        first = [_remote(w_ref.at[pl.ds(c * H, H), :], half(x, y, c), send_sems, recv_sems, j, (*chip, c))
                 for j, chip in enumerate(chips)]
        for cp in first:
            cp.start()
        passed = [_remote(half(*chip, c), half(*chip, c), send_sems, recv_sems, 3 + j, sib)
                  for j, chip in enumerate(chips)]
        for j, chip in enumerate(chips):
            _remote(half(*chip, c), half(*chip, c), send_sems, recv_sems, j, (*chip, c)).wait_recv()
            passed[j].start()
        for j, chip in enumerate(chips):
            _remote(half(*chip, 1 - c), half(*chip, 1 - c), send_sems, recv_sems, 3 + j, sib).wait_recv()
        for cp in first + passed:
            cp.wait_send()
        mine.wait()

    return pl.pallas_call(
        body, name="allgather_chip_blocks", in_specs=[ANY], out_specs=ANY,
        out_shape=jax.ShapeDtypeStruct((N_CHIPS, R, C), wp.dtype),
        scratch_shapes=[pltpu.SemaphoreType.DMA((6,)), pltpu.SemaphoreType.DMA((6,)), pltpu.SemaphoreType.DMA],
    )(wp)


def allgather_devices(v):
    R, C = v.shape

    def body(v_ref, out_ref, send_sems, recv_sems, local_sem):
        x, y, c = _me()
        sib = (x, y, 1 - c)
        chips = _other_chips(x, y)

        def blk(px, py, pc):
            return out_ref.at[4 * px + 2 * py + pc]

        mine = pltpu.make_async_copy(v_ref, blk(x, y, c), local_sem)
        mine.start()
        first = [_remote(v_ref, blk(x, y, c), send_sems, recv_sems, 0, sib)]
        first += [_remote(v_ref, blk(x, y, c), send_sems, recv_sems, 1 + j, (*chip, c)) for j, chip in enumerate(chips)]
        for cp in first:
            cp.start()
        passed = [_remote(blk(*chip, c), blk(*chip, c), send_sems, recv_sems, 4 + j, sib) for j, chip in enumerate(chips)]
        for j, chip in enumerate(chips):
            _remote(blk(*chip, c), blk(*chip, c), send_sems, recv_sems, 1 + j, (*chip, c)).wait_recv()
            passed[j].start()
        _remote(blk(x, y, 1 - c), blk(x, y, 1 - c), send_sems, recv_sems, 0, sib).wait_recv()
        for j, chip in enumerate(chips):
            _remote(blk(*chip, 1 - c), blk(*chip, 1 - c), send_sems, recv_sems, 4 + j, sib).wait_recv()
        for cp in first + passed:
            cp.wait_send()
        mine.wait()

    return pl.pallas_call(
        body, name="allgather_devices", in_specs=[ANY], out_specs=ANY,
        out_shape=jax.ShapeDtypeStruct((N_DEV, R, C), v.dtype),
        scratch_shapes=[pltpu.SemaphoreType.DMA((7,)), pltpu.SemaphoreType.DMA((7,)), pltpu.SemaphoreType.DMA],
    )(v)


def sibling_send_other_half(g):
    n, _, H, C = g.shape

    def body(g_ref, recv_ref, send_sems, recv_sems):
        x, y, c = _me()
        cp = _remote(g_ref.at[:, 1 - c], recv_ref, send_sems, recv_sems, 0, (x, y, 1 - c))
        cp.start()
        cp.wait()

    return pl.pallas_call(
        body, name="sibling_send_other_half", in_specs=[ANY], out_specs=ANY,
        out_shape=jax.ShapeDtypeStruct((n, H, C), g.dtype),
        scratch_shapes=[pltpu.SemaphoreType.DMA((1,)), pltpu.SemaphoreType.DMA((1,))],
    )(g)


def chips_exchange(a):
    _, H, C = a.shape

    def body(a_ref, recv_ref, send_sems, recv_sems):
        x, y, c = _me()
        chips = _other_chips(x, y)
        cps = [_remote(a_ref.at[2 * cx + cy], recv_ref.at[j], send_sems, recv_sems, j, (cx, cy, c))
               for j, (cx, cy) in enumerate(chips)]
        for cp in cps:
            cp.start()
        for cp in cps:
            cp.wait()

    return pl.pallas_call(
        body, name="chips_exchange", in_specs=[ANY], out_specs=ANY,
        out_shape=jax.ShapeDtypeStruct((3, H, C), a.dtype),
        scratch_shapes=[pltpu.SemaphoreType.DMA((3,)), pltpu.SemaphoreType.DMA((3,))],
    )(a)


def sibling_share(f):
    H, C = f.shape

    def body(f_ref, out_ref, send_sems, recv_sems, local_sem):
        x, y, c = _me()
        mine = pltpu.make_async_copy(f_ref, out_ref.at[c], local_sem)
        mine.start()
        cp = _remote(f_ref, out_ref.at[c], send_sems, recv_sems, 0, (x, y, 1 - c))
        cp.start()
        _remote(f_ref, out_ref.at[1 - c], send_sems, recv_sems, 0, (x, y, 1 - c)).wait_recv()
        cp.wait_send()
        mine.wait()

    return pl.pallas_call(
        body, name="sibling_share", in_specs=[ANY], out_specs=ANY,
        out_shape=jax.ShapeDtypeStruct((2, H, C), f.dtype),
        scratch_shapes=[pltpu.SemaphoreType.DMA((1,)), pltpu.SemaphoreType.DMA((1,)), pltpu.SemaphoreType.DMA],
    )(f)


SUM_ROWS = 480


def sum_with_sibling(g, recv, c):
    n, _, H, C = g.shape
    tr = min(SUM_ROWS, H)

    def body(c_ref, g_ref, r_ref, o_ref):
        o_ref[...] = (g_ref[...].astype(F32) + r_ref[...].astype(F32)).astype(o_ref.dtype)

    return pl.pallas_call(
        body, name="sum_with_sibling",
        grid_spec=pltpu.PrefetchScalarGridSpec(
            num_scalar_prefetch=1, grid=(n, H // tr),
            in_specs=[pl.BlockSpec((None, None, tr, C), lambda s, i, c_ref: (s, c_ref[0], i, 0)),
                      pl.BlockSpec((None, tr, C), lambda s, i, c_ref: (s, i, 0))],
            out_specs=pl.BlockSpec((None, tr, C), lambda s, i, c_ref: (s, i, 0))),
        out_shape=jax.ShapeDtypeStruct((n, H, C), g.dtype), compiler_params=_cp("parallel", "parallel"),
    )(c, g, recv)


def sum_chips(a, recv, s):
    _, H, C = a.shape
    tr = min(SUM_ROWS, H)

    def body(s_ref, a_ref, r_ref, o_ref):
        acc = a_ref[...].astype(F32)
        for j in range(3):
            acc = acc + r_ref[j].astype(F32)
        o_ref[...] = acc

    return pl.pallas_call(
        body, name="sum_chips",
        grid_spec=pltpu.PrefetchScalarGridSpec(
            num_scalar_prefetch=1, grid=(H // tr,),
            in_specs=[pl.BlockSpec((None, tr, C), lambda i, s_ref: (s_ref[0], i, 0)),
                      pl.BlockSpec((3, tr, C), lambda i, s_ref: (0, i, 0))],
            out_specs=pl.BlockSpec((tr, C), lambda i, s_ref: (i, 0))),
        out_shape=jax.ShapeDtypeStruct((H, C), F32), compiler_params=_cp("parallel"),
    )(s, a, recv)


def sum_devices(gathered):
    _, R, C = gathered.shape

    def body(g_ref, o_ref):
        acc = g_ref[0]
        for d in range(1, N_DEV):
            acc = acc + g_ref[d]
        o_ref[...] = acc

    return pl.pallas_call(body, name="sum_devices", out_shape=jax.ShapeDtypeStruct((R, C), F32),
                          compiler_params=_cp())(gathered)


PACK_COLS = 1024
BIG = (("ev_w_in", 2), ("ev_s5_glu_w", 1), ("ev_w_out", 1), ("od_w_in", 2), ("od_w_out", 1),
       ("xa_w_qg", 2), ("xa_w_kv", 2), ("xa_w_o", 1))
SHARDED_F32 = (("ev_conv_w", 2), ("od_norm_g", 1))
SMALL = ("mem_norm_g", "ev_norm_g", "ev_s5_lambda_re", "ev_s5_lambda_im", "ev_s5_log_dt", "ev_s5_b_re", "ev_s5_b_im",
         "ev_s5_c_re", "ev_s5_c_im", "ev_s5_d", "ev_s5_glu_b", "ev_conv_b", "ev_conv_ln_g", "ev_conv_ln_b",
         "od_rel_bias", "xa_norm_g", "final_norm_g")
WEIGHTS = ("mem_norm_g", "ev_norm_g", "ev_w_in", "ev_s5_lambda_re", "ev_s5_lambda_im", "ev_s5_log_dt", "ev_s5_b_re",
           "ev_s5_b_im", "ev_s5_c_re", "ev_s5_c_im", "ev_s5_d", "ev_s5_glu_w", "ev_s5_glu_b", "ev_conv_w", "ev_conv_b",
           "ev_conv_ln_g", "ev_conv_ln_b", "ev_w_out", "od_norm_g", "od_w_in", "od_rel_bias", "od_w_out", "xa_norm_g",
           "xa_w_qg", "xa_w_kv", "xa_w_o", "final_norm_g")


def _pack(parts, row_multiple):
    flat = jnp.concatenate([p.reshape(-1) for p in parts])
    unit = PACK_COLS * row_multiple
    pad = (-flat.shape[0]) % unit
    if pad:
        flat = jnp.concatenate([flat, jnp.zeros((pad,), flat.dtype)])
    return flat.reshape(-1, PACK_COLS)


def _unpack(packed, shapes):
    lead = packed.shape[:-2]
    flat = packed.reshape(lead + (-1,))
    out, off = [], 0
    for sh in shapes:
        n = math.prod(sh)
        out.append(flat[..., off:off + n].reshape(lead + tuple(sh)))
        off += n
    return out


def _join_shards(stacked, axis):
    return jnp.concatenate([stacked[s] for s in range(N_CHIPS)], axis=axis)


def _shard_of(full, axis, s):
    n = full.shape[axis] // N_CHIPS
    return lax.dynamic_slice_in_dim(full, s * n, n, axis=axis)


def kernel(x, mem, mem_norm_g, ev_norm_g, ev_w_in, ev_s5_lambda_re, ev_s5_lambda_im, ev_s5_log_dt, ev_s5_b_re, ev_s5_b_im, ev_s5_c_re, ev_s5_c_im, ev_s5_d, ev_s5_glu_w, ev_s5_glu_b, ev_conv_w, ev_conv_b, ev_conv_ln_g, ev_conv_ln_b, ev_w_out, od_norm_g, od_w_in, od_rel_bias, od_w_out, xa_norm_g, xa_w_qg, xa_w_kv, xa_w_o, final_norm_g, loss_target, m_mem_norm_g, m_ev_norm_g, m_ev_w_in, m_ev_s5_lambda_re, m_ev_s5_lambda_im, m_ev_s5_log_dt, m_ev_s5_b_re, m_ev_s5_b_im, m_ev_s5_c_re, m_ev_s5_c_im, m_ev_s5_d, m_ev_s5_glu_w, m_ev_s5_glu_b, m_ev_conv_w, m_ev_conv_b, m_ev_conv_ln_g, m_ev_conv_ln_b, m_ev_w_out, m_od_norm_g, m_od_w_in, m_od_rel_bias, m_od_w_out, m_xa_norm_g, m_xa_w_qg, m_xa_w_kv, m_xa_w_o, m_final_norm_g, v_mem_norm_g, v_ev_norm_g, v_ev_w_in, v_ev_s5_lambda_re, v_ev_s5_lambda_im, v_ev_s5_log_dt, v_ev_s5_b_re, v_ev_s5_b_im, v_ev_s5_c_re, v_ev_s5_c_im, v_ev_s5_d, v_ev_s5_glu_w, v_ev_s5_glu_b, v_ev_conv_w, v_ev_conv_b, v_ev_conv_ln_g, v_ev_conv_ln_b, v_ev_w_out, v_od_norm_g, v_od_w_in, v_od_rel_bias, v_od_w_out, v_xa_norm_g, v_xa_w_qg, v_xa_w_kv, v_xa_w_o, v_final_norm_g):
    a = dict(locals())
    w = {n: a[n] for n in WEIGHTS}
    shard = 2 * lax.axis_index("x") + lax.axis_index("y")
    core = lax.axis_index("c")

    parts = [w[n].astype(BF16) for n, _ in BIG]
    parts += [lax.bitcast_convert_type(w[n], BF16) for n, _ in SHARDED_F32]
    gathered = allgather_chip_blocks(_pack(parts, 32))
    shapes = [w[n].shape for n, _ in BIG] + [w[n].shape + (2,) for n, _ in SHARDED_F32]
    pieces = _unpack(gathered, shapes)
    p = dict(w)
    for (n, axis), piece in zip(BIG, pieces):
        p[n] = _join_shards(piece, axis)
    for (n, axis), piece in zip(SHARDED_F32, pieces[len(BIG):]):
        p[n] = _join_shards(lax.bitcast_convert_type(piece, F32), axis)

    loss, grad_x, g = local_step(x[0], mem[0], loss_target[0], p)
    loss = lax.psum(loss[0, 0], ("x", "y", "c"))

    per_chip = [_pack([_shard_of(g[n], axis, s) for n, axis in BIG], 32) for s in range(N_CHIPS)]
    rows = per_chip[0].shape[0]
    gp = jnp.stack(per_chip).astype(BF16).reshape(N_CHIPS, 2, rows // 2, PACK_COLS)
    from_sibling = sibling_send_other_half(gp)
    chip_sum = sum_with_sibling(gp, from_sibling, core.reshape(1).astype(jnp.int32))
    from_chips = chips_exchange(chip_sum)
    half = sum_chips(chip_sum, from_chips, shard.reshape(1).astype(jnp.int32))
    g_big = sibling_share(half).reshape(rows, PACK_COLS)

    small_full = SMALL + tuple(n for n, _ in SHARDED_F32)
    gs = sum_devices(allgather_devices(_pack([g[n] for n in small_full], 8)))
    gsum = dict(zip(small_full, _unpack(gs, [g[n].shape for n in small_full])))
    for n, axis in SHARDED_F32:
        gsum[n] = _shard_of(gsum[n], axis, shard)

    big_names = [n for n, _ in BIG]
    d_big, m_big, v_big = adamw(_pack([w[n] for n in big_names], 32), g_big,
                                _pack([a["m_" + n] for n in big_names], 32),
                                _pack([a["v_" + n] for n in big_names], 32), "adamw_matrices", tr=SUM_ROWS)
    small_names = list(small_full)
    g_small = _pack([gsum[n] for n in small_names], 8)
    d_small, m_small, v_small = adamw(_pack([w[n] for n in small_names], 8), g_small,
                                      _pack([a["m_" + n] for n in small_names], 8),
                                      _pack([a["v_" + n] for n in small_names], 8), "adamw_vectors", tr=g_small.shape[0])
    out = {}
    for tag, big, small in (("grad", g_big, g_small), ("delta", d_big, d_small), ("m", m_big, m_small), ("v", v_big, v_small)):
        vals = dict(zip(big_names, _unpack(big, [w[n].shape for n in big_names])))
        vals.update(zip(small_names, _unpack(small, [w[n].shape for n in small_names])))
        out[tag] = vals
    res = [loss, grad_x[None]]
    for tag in ("grad", "delta", "m", "v"):
        res += [out[tag][n] for n in WEIGHTS]
    return tuple(res)
```

```python
import math

import jax
import jax.numpy as jnp
import numpy as np
from jax import lax
from jax.experimental import pallas as pl
from jax.experimental.pallas import tpu as pltpu

F32 = jnp.float32
BF16 = jnp.bfloat16

D_MODEL = 1024
CHUNK = 64
LEFT_CHUNKS = 8
S5_WIDTH = 512
S5_GROUP = 16
S5_GROUPS = 32
S5_STATE = 64
S5_COLS = S5_GROUPS * S5_STATE
S5_SPLIT = 4
S5_CC = S5_COLS // S5_SPLIT
S5_UC = S5_WIDTH // S5_SPLIT
CONV_WIDTH = 512
CONV_KERNEL = 31
CONV_HALO = 32
ATT_HEADS = 16
ATT_HEAD_DIM = 64
MAX_REL = 128
MEM_LEN = 256
XA_HEADS = 4
XA_HEAD_DIM = 256
EPS = 1e-6
EVEN_IN = 2560
ODD_IN = 4096

ADAM_LR = 0.001
ADAM_B1 = 0.9
ADAM_B2 = 0.999
ADAM_EPS = 1e-08
ADAM_WD = 0.01
ADAM_STEP = 10

ROW_TILE = 256
ATT_QB = 256
ATT_PAD = LEFT_CHUNKS * CHUNK
ATT_WIN = ATT_PAD + ATT_QB
VMEM_LIMIT_V7X = 56 * 1024 * 1024
NEG = -1e30
LANES = 128
N_CHIPS = 4
N_DEV = 8

MESH = pl.DeviceIdType.MESH
ANY = pl.BlockSpec(memory_space=pl.ANY)


def _cp(*sem, vmem=VMEM_LIMIT_V7X):
    return pltpu.CompilerParams(dimension_semantics=sem if sem else None, vmem_limit_bytes=vmem)


def _full(shape):
    n = len(shape)
    return pl.BlockSpec(shape, lambda *_: (0,) * n)


def _wspec(w, layer=None):
    if layer is None:
        return _full(w.shape)
    s, _, r, c = w.shape
    return pl.BlockSpec((s, None, r, c), lambda *_: (0, layer, 0, 0))


def _lane_tile(n, cap):
    return max(t for t in range(LANES, min(n, cap) + 1, LANES) if n % t == 0)


def _sigmoid(x):
    return 1.0 / (1.0 + jnp.exp(-x))


def _silu(x):
    return x * _sigmoid(x)


def _dsilu(x):
    s = _sigmoid(x)
    return s * (1.0 + x * (1.0 - s))


_GELU_C = math.sqrt(2.0 / math.pi)


def _gelu(x):
    return 0.5 * x * (1.0 + jnp.tanh(_GELU_C * (x + 0.044715 * x * x * x)))


def _dgelu(x):
    t = jnp.tanh(_GELU_C * (x + 0.044715 * x * x * x))
    return 0.5 * (1.0 + t) + 0.5 * x * (1.0 - t * t) * _GELU_C * (1.0 + 3.0 * 0.044715 * x * x)


def _dot(a, b):
    return jnp.dot(a, b, preferred_element_type=F32)


def _dot_nt(a, b):
    return lax.dot_general(a, b, (((1,), (1,)), ((), ())), preferred_element_type=F32)


def _dot_tn(a, b):
    return lax.dot_general(a, b, (((0,), (0,)), ((), ())), preferred_element_type=F32)


def _dot_cols(a, w4, shards=range(N_CHIPS)):
    return jnp.concatenate([_dot(a, w4[s]) for s in shards], axis=1)


def _dot_rows(a, w4):
    r = w4.shape[1]
    acc = _dot(a[:, 0:r], w4[0])
    for s in range(1, N_CHIPS):
        acc = acc + _dot(a[:, s * r:(s + 1) * r], w4[s])
    return acc


def _dot_nt_cols(dys, w4):
    acc = _dot_nt(dys[0], w4[0])
    for s in range(1, N_CHIPS):
        acc = acc + _dot_nt(dys[s], w4[s])
    return acc


def _dot_nt_rows(dy, w4):
    return jnp.concatenate([_dot_nt(dy, w4[s]) for s in range(N_CHIPS)], axis=1)


def _col_pieces(v, n):
    return [v[:, s * n:(s + 1) * n] for s in range(N_CHIPS)]


def _rms_parts(xv):
    inv = lax.rsqrt(jnp.mean(xv * xv, axis=-1, keepdims=True) + EPS)
    return inv, xv * inv


def _rms_bwd(xv, g, dh):
    inv, xhat = _rms_parts(xv)
    dg = jnp.sum(dh * xhat, axis=0, keepdims=True)
    dxh = dh * g
    dx = inv * (dxh - xhat * jnp.mean(dxh * xhat, axis=-1, keepdims=True))
    return dx, dg


def norm_mm(x, g, w4, groups, name, tm=ROW_TILE):
    M, D = x.shape
    n = w4.shape[2]
    tm = min(tm, M)

    def body(x_ref, g_ref, w_ref, *outs):
        _, xhat = _rms_parts(x_ref[...])
        hb = (xhat * g_ref[...]).astype(BF16)
        for o, (shards, dt, _) in zip(outs, groups):
            o[...] = _dot_cols(hb, w_ref, shards).astype(dt)
        outs[-1][...] = hb

    out_shape = [jax.ShapeDtypeStruct((M + pad, len(sh) * n), dt) for (sh, dt, pad) in groups]
    out_specs = [pl.BlockSpec((tm, len(sh) * n), lambda i, p=pad // tm: (i + p, 0)) for (sh, _, pad) in groups]
    out_shape.append(jax.ShapeDtypeStruct((M, D), BF16))
    out_specs.append(pl.BlockSpec((tm, D), lambda i: (i, 0)))
    return pl.pallas_call(
        body, name=name, grid=(M // tm,),
        in_specs=[pl.BlockSpec((tm, D), lambda i: (i, 0)), _full(g.shape), _full(w4.shape)],
        out_specs=out_specs, out_shape=out_shape, compiler_params=_cp("parallel"),
    )(x, g, w4)


def zero_rows(buf, rows, name, tm=ROW_TILE):
    C = buf.shape[1]

    def body(b_ref, o_ref):
        o_ref[...] = jnp.zeros_like(o_ref)

    return pl.pallas_call(
        body, name=name, grid=(rows // tm,), in_specs=[ANY],
        out_specs=pl.BlockSpec((tm, C), lambda i: (i, 0)),
        out_shape=jax.ShapeDtypeStruct(buf.shape, buf.dtype), input_output_aliases={0: 0},
        compiler_params=_cp("parallel"),
    )(buf)


def mm_res(a, w4, res, name, tm=ROW_TILE):
    M, K = a.shape
    N = w4.shape[2]
    tm = min(tm, M)

    def body(a_ref, w_ref, r_ref, o_ref):
        o_ref[...] = r_ref[...] + _dot_rows(a_ref[...], w_ref)

    return pl.pallas_call(
        body, name=name, grid=(M // tm,),
        in_specs=[pl.BlockSpec((tm, K), lambda i: (i, 0)), _full(w4.shape), pl.BlockSpec((tm, N), lambda i: (i, 0))],
        out_specs=pl.BlockSpec((tm, N), lambda i: (i, 0)),
        out_shape=jax.ShapeDtypeStruct((M, N), F32), compiler_params=_cp("parallel"),
    )(a, w4, res)


def mm_cols(a, w, layer, name, out_dtype):
    M = a.shape[0]
    n = w.shape[3]

    def body(a_ref, w_ref, o_ref):
        o_ref[...] = _dot_cols(a_ref[...], w_ref).astype(out_dtype)

    return pl.pallas_call(
        body, name=name, grid=(1,), in_specs=[_full(a.shape), _wspec(w, layer)],
        out_specs=_full((M, N_CHIPS * n)), out_shape=jax.ShapeDtypeStruct((M, N_CHIPS * n), out_dtype),
        compiler_params=_cp("arbitrary"),
    )(a, w)


def mm_nt_cols(dy, w, layer, name):
    M = dy.shape[0]
    K, n = w.shape[2], w.shape[3]

    def body(d_ref, w_ref, o_ref):
        o_ref[...] = _dot_nt_cols(_col_pieces(d_ref[...].astype(BF16), n), w_ref)

    return pl.pallas_call(
        body, name=name, grid=(1,), in_specs=[_full(dy.shape), _wspec(w, layer)],
        out_specs=_full((M, K)), out_shape=jax.ShapeDtypeStruct((M, K), F32), compiler_params=_cp("arbitrary"),
    )(dy, w)


def mm_nt_rows(dy, w4, name, tm=ROW_TILE):
    M, N = dy.shape
    K = N_CHIPS * w4.shape[1]
    tm = min(tm, M)

    def body(d_ref, w_ref, o_ref):
        o_ref[...] = _dot_nt_rows(d_ref[...].astype(BF16), w_ref)

    return pl.pallas_call(
        body, name=name, grid=(M // tm,),
        in_specs=[pl.BlockSpec((tm, N), lambda i: (i, 0)), _full(w4.shape)],
        out_specs=pl.BlockSpec((tm, K), lambda i: (i, 0)),
        out_shape=jax.ShapeDtypeStruct((M, K), F32), compiler_params=_cp("parallel"),
    )(dy, w4)


def mm_nt_normbwd(dys, offs, w4, x, g, dx_out, name, tm=ROW_TILE):
    M, D = x.shape
    n = w4.shape[2]
    tm = min(tm, M)
    nd = len(dys)

    def body(*refs):
        d_refs = refs[:nd]
        w_ref, x_ref, g_ref, dxo_ref, dx_ref, dg_ref = refs[nd:]
        if nd == 1:
            pieces = _col_pieces(d_refs[0][...].astype(BF16), n)
        else:
            pieces = [r[...].astype(BF16) for r in d_refs]
        dh = _dot_nt_cols(pieces, w_ref)
        dx, dg = _rms_bwd(x_ref[...], g_ref[...], dh)
        dx_ref[...] = dxo_ref[...] + dx

        @pl.when(pl.program_id(0) == 0)
        def _():
            dg_ref[...] = jnp.zeros_like(dg_ref)

        dg_ref[...] += dg

    row = lambda c, off=0: pl.BlockSpec((tm, c), lambda i, p=off // tm: (i + p, 0))
    return pl.pallas_call(
        body, name=name, grid=(M // tm,),
        in_specs=[row(d.shape[1], off) for d, off in zip(dys, offs)] + [_full(w4.shape), row(D), _full(g.shape), row(D)],
        out_specs=[row(D), _full((1, D))],
        out_shape=[jax.ShapeDtypeStruct((M, D), F32), jax.ShapeDtypeStruct((1, D), F32)],
        compiler_params=_cp("arbitrary"),
    )(*dys, w4, x, g, dx_out)


def mm_tn(a, b, name, layout, into=None, b_off=0, out_dtype=BF16, bm=512, bn=1024, bl=512):
    L, K = a.shape
    N = b.shape[1]
    kind = layout[0]
    arg = layout[1] if len(layout) > 1 else None
    if kind == "cols":
        bn = _lane_tile(N // N_CHIPS, bn)
    if kind == "rows":
        bm = min(bm, K // N_CHIPS)
    bm, bn, bl = _lane_tile(K, bm), _lane_tile(N, bn), min(bl, L)
    assert L % bl == 0 and b_off % bl == 0, (L, bl, b_off)
    nl = L // bl
    pc =(N // N_CHIPS) // bn if kind == "cols" else None
    pr = (K // N_CHIPS) // bm if kind == "rows" else None
    if kind == "plain":
        oshape, oblock, oidx = (K, N), (bm, bn), lambda i, j, l: (i, j)
    elif kind == "slab":
        oshape, oblock, oidx = (N_CHIPS, K, N), (None, bm, bn), lambda i, j, l: (arg, i, j)
    elif kind == "cols" and arg is None:
        oshape, oblock, oidx = (N_CHIPS, K, N // N_CHIPS), (None, bm, bn), lambda i, j, l: (j // pc, i, j % pc)
    elif kind == "cols":
        oshape, oblock = (N_CHIPS, 2, K, N // N_CHIPS), (None, None, bm, bn)
        oidx = lambda i, j, l: (j // pc, arg, i, j % pc)
    elif kind == "rows" and arg is None:
        oshape, oblock, oidx = (N_CHIPS, K // N_CHIPS, N), (None, bm, bn), lambda i, j, l: (i // pr, i % pr, j)
    else:
        oshape, oblock = (N_CHIPS, 2, K // N_CHIPS, N), (None, None, bm, bn)
        oidx = lambda i, j, l: (i // pr, arg, i % pr, j)

    def body(a_ref, b_ref, *rest):
        o_ref, acc = rest[-2], rest[-1]
        l = pl.program_id(2)

        @pl.when(l == 0)
        def _():
            acc[...] = jnp.zeros_like(acc)

        acc[...] += _dot_tn(a_ref[...].astype(BF16), b_ref[...].astype(BF16))

        @pl.when(l == nl - 1)
        def _():
            o_ref[...] = acc[...].astype(out_dtype)

    in_specs = [pl.BlockSpec((bl, bm), lambda i, j, l: (l, i)),
                pl.BlockSpec((bl, bn), lambda i, j, l, p=b_off // bl: (l + p, j))]
    args = [a, b]
    alias = {}
    if into is not None:
        in_specs.append(ANY)
        args.append(into)
        alias = {2: 0}
    return pl.pallas_call(
        body, name=name, grid=(K // bm, N // bn, nl), in_specs=in_specs,
        out_specs=pl.BlockSpec(oblock, oidx), out_shape=jax.ShapeDtypeStruct(oshape, out_dtype),
        scratch_shapes=[pltpu.VMEM((bm, bn), F32)], input_output_aliases=alias,
        compiler_params=_cp("parallel", "parallel", "arbitrary"),
    )(*args)


def rms_fwd(x, g, name):
    def body(x_ref, g_ref, ob_ref):
        _, xhat = _rms_parts(x_ref[...])
        ob_ref[...] = (xhat * g_ref[...]).astype(BF16)

    return pl.pallas_call(body, name=name, out_shape=jax.ShapeDtypeStruct(x.shape, BF16))(x, g)


def rms_dgain(x, dy0, dy1, name):
    def body(x_ref, d0_ref, d1_ref, o_ref):
        _, xhat = _rms_parts(x_ref[...])
        o_ref[...] = jnp.sum((d0_ref[...] + d1_ref[...]) * xhat, axis=0, keepdims=True)

    return pl.pallas_call(body, name=name, out_shape=jax.ShapeDtypeStruct((1, x.shape[1]), F32))(x, dy0, dy1)


def _s5_discretise(lr, li, logdt, bt_re, bt_im):
    dt = jnp.exp(logdt)
    mag = jnp.exp(lr * dt)
    ab_re = mag * jnp.cos(li * dt)
    ab_im = mag * jnp.sin(li * dt)
    den = lr * lr + li * li
    nr = ab_re - 1.0
    coef_re = (nr * lr + ab_im * li) / den
    coef_im = (ab_im * lr - nr * li) / den
    cr = coef_re[:, None, :]
    ci = coef_im[:, None, :]
    bb_re = cr * bt_re - ci * bt_im
    bb_im = cr * bt_im + ci * bt_re
    return ab_re, ab_im, bb_re, bb_im


def s5_param_fwd(lr, li, logdt, bt_re, bt_im):
    def body(lr_ref, li_ref, ld_ref, br_ref, bi_ref, bbr_ref, bbi_ref):
        _, _, bb_re, bb_im = _s5_discretise(lr_ref[...], li_ref[...], ld_ref[...], br_ref[...], bi_ref[...])
        bbr_ref[...] = bb_re
        bbi_ref[...] = bb_im

    sh = jax.ShapeDtypeStruct(bt_re.shape, F32)
    return pl.pallas_call(body, name="s5_param_fwd", out_shape=[sh, sh])(lr, li, logdt, bt_re, bt_im)


def s5_param_bwd(lr, li, logdt, bt_re, bt_im, d_ab_re, d_ab_im, d_bb_re, d_bb_im):
    def body(lr_ref, li_ref, ld_ref, br_ref, bi_ref, dar_ref, dai_ref, dbr_ref, dbi_ref,
             o_lr, o_li, o_ld, o_br, o_bi):
        _, vjp = jax.vjp(_s5_discretise, lr_ref[...], li_ref[...], ld_ref[...], br_ref[...], bi_ref[...])
        g = vjp((dar_ref[...], dai_ref[...], dbr_ref[...], dbi_ref[...]))
        for o, v in zip((o_lr, o_li, o_ld, o_br, o_bi), g):
            o[...] = v

    shapes = [jax.ShapeDtypeStruct(a.shape, F32) for a in (lr, li, logdt, bt_re, bt_im)]
    return pl.pallas_call(body, name="s5_param_bwd", out_shape=shapes)(
        lr, li, logdt, bt_re, bt_im, d_ab_re, d_ab_im, d_bb_re, d_bb_im)


def s5_tables(lr_flat, li_flat, logdt_flat):
    def body(lr_ref, li_ref, ld_ref, tab_ref):
        dt = jnp.exp(ld_ref[...])
        a = lr_ref[...] * dt
        th = li_ref[...] * dt
        row = lax.broadcasted_iota(jnp.int32, (8, 1), 0)
        rowf = row.astype(F32)

        def power(e, sign):
            m = jnp.exp(e * a)
            return m * jnp.cos(e * th), sign * m * jnp.sin(e * th)

        k = 0
        for sign, fwd in ((1.0, True), (-1.0, False)):
            for s in (1, 2, 4):
                pr, pi = power(jnp.full((8, 1), float(s), F32), sign)
                keep = (row >= s) if fwd else (row + s < 8)
                tab_ref[k] = jnp.where(keep, pr, 0.0)
                tab_ref[k + 1] = jnp.where(keep, pi, 0.0)
                k += 2
            e = rowf + 1.0 if fwd else 8.0 - rowf
            pr, pi = power(e, sign)
            tab_ref[k] = pr
            tab_ref[k + 1] = pi
            k += 2

    return pl.pallas_call(body, name="s5_tables",
                          out_shape=jax.ShapeDtypeStruct((16, 8, S5_COLS), F32))(lr_flat, li_flat, logdt_flat)


def _scan_block(a, b, tabs, base, cr, ci, reverse):
    for n, s in enumerate((1, 2, 4)):
        mr = tabs[base + 2 * n]
        mi = tabs[base + 2 * n + 1]
        sh = (8 - s) if reverse else s
        ar = pltpu.roll(a, sh, 0)
        br = pltpu.roll(b, sh, 0)
        a, b = a + mr * ar - mi * br, b + mr * br + mi * ar
    pr = tabs[base + 6]
    pi = tabs[base + 7]
    a, b = a + pr * cr - pi * ci, b + pr * ci + pi * cr
    return a, b


def s5_fwd(z, bbd_re, bbd_im, ccd_re, ccd_im, tab, dskip, tm=ROW_TILE):
    L = z.shape[0]
    tm = min(tm, L)
    nt = L // tm

    def body(u_ref, bbr_ref, bbi_ref, ccr_ref, cci_ref, tab_ref, d_ref, y_ref, ck_ref, xr, xi, car):
        t = pl.program_id(1)

        @pl.when(t == 0)
        def _():
            car[...] = jnp.zeros_like(car)

        u = u_ref[...]
        ub = u.astype(BF16)
        xr[...] = _dot(ub, bbr_ref[...])
        xi[...] = _dot(ub, bbi_ref[...])
        tabs = [tab_ref[k] for k in range(8)]

        def blk(i, c):
            r0 = pl.multiple_of(i * 8, 8)
            a, b = _scan_block(xr[pl.ds(r0, 8), :], xi[pl.ds(r0, 8), :], tabs, 0, c[0], c[1], False)
            xr[pl.ds(r0, 8), :] = a
            xi[pl.ds(r0, 8), :] = b
            return a[7:8, :], b[7:8, :]

        cr, ci = lax.fori_loop(0, tm // 8, blk, (car[0:1, :], car[1:2, :]))
        car[0:1, :] = cr
        car[1:2, :] = ci
        ck_ref[0:1, :] = cr
        ck_ref[1:2, :] = ci
        y_ref[...] = (_dot(xr[...].astype(BF16), ccr_ref[...]) - _dot(xi[...].astype(BF16), cci_ref[...])
                      + d_ref[...] * u)

    return pl.pallas_call(
        body, name="s5_fwd", grid=(S5_SPLIT, nt),
        in_specs=[pl.BlockSpec((tm, S5_UC), lambda j, t: (t, j)),
                  pl.BlockSpec((None, S5_UC, S5_CC), lambda j, t: (j, 0, 0)),
                  pl.BlockSpec((None, S5_UC, S5_CC), lambda j, t: (j, 0, 0)),
                  pl.BlockSpec((None, S5_CC, S5_UC), lambda j, t: (j, 0, 0)),
                  pl.BlockSpec((None, S5_CC, S5_UC), lambda j, t: (j, 0, 0)),
                  pl.BlockSpec((8, 8, S5_CC), lambda j, t: (0, 0, j)),
                  pl.BlockSpec((1, S5_UC), lambda j, t: (0, j))],
        out_specs=[pl.BlockSpec((tm, S5_UC), lambda j, t: (t, j)),
                   pl.BlockSpec((None, 2, S5_CC), lambda j, t: (t, 0, j))],
        out_shape=[jax.ShapeDtypeStruct((L, S5_WIDTH), F32), jax.ShapeDtypeStruct((nt, 2, S5_COLS), F32)],
        scratch_shapes=[pltpu.VMEM((tm, S5_CC), F32), pltpu.VMEM((tm, S5_CC), F32), pltpu.VMEM((2, S5_CC), F32)],
        compiler_params=_cp("parallel", "arbitrary"),
    )(z, bbd_re, bbd_im, ccd_re, ccd_im, tab, dskip)


def s5_bwd(z, dy, dz, ckpt, bbd_re, bbd_im, ccd_re, ccd_im, tab, dskip, tm=ROW_TILE):
    L = z.shape[0]
    tm = min(tm, L)
    nt = L // tm

    def body(u_ref, dy_ref, dz_ref, ck_ref, bbr_ref, bbi_ref, ccr_ref, cci_ref, tab_ref, d_ref,
             du_ref, da_ref, dbr_ref, dbi_ref, dcr_ref, dci_ref, dd_ref, hr, hi, gr, gi, car, acr, aci):
        t = pl.program_id(1)
        tt = nt - 1 - t

        @pl.when(t == 0)
        def _():
            for r in (car, acr, aci, dbr_ref, dbi_ref, dcr_ref, dci_ref, dd_ref):
                r[...] = jnp.zeros_like(r)

        u = u_ref[...]
        ub = u.astype(BF16)
        dyv = dy_ref[...]
        dyb = dyv.astype(BF16)
        tabs = [tab_ref[k] for k in range(16)]

        live = (tt > 0).astype(F32)
        c0r = ck_ref[0:1, :] * live
        c0i = ck_ref[1:2, :] * live
        hr[0:8, :] = jnp.broadcast_to(c0r, (8, S5_CC))
        hi[0:8, :] = jnp.broadcast_to(c0i, (8, S5_CC))
        hr[8:, :] = _dot(ub, bbr_ref[...])
        hi[8:, :] = _dot(ub, bbi_ref[...])

        def fblk(i, c):
            r0 = pl.multiple_of(i * 8 + 8, 8)
            a, b = _scan_block(hr[pl.ds(r0, 8), :], hi[pl.ds(r0, 8), :], tabs, 0, c[0], c[1], False)
            hr[pl.ds(r0, 8), :] = a
            hi[pl.ds(r0, 8), :] = b
            return a[7:8, :], b[7:8, :]

        lax.fori_loop(0, tm // 8, fblk, (c0r, c0i))
        hrb = hr[8:, :].astype(BF16)
        hib = hi[8:, :].astype(BF16)
        dcr_ref[...] += _dot_tn(hrb, dyb)
        dci_ref[...] -= _dot_tn(hib, dyb)

        gr[...] = _dot_nt(dyb, ccr_ref[...])
        gi[...] = -_dot_nt(dyb, cci_ref[...])
        row0 = lax.broadcasted_iota(jnp.int32, (8, S5_CC), 0) == 0

        def rblk(k, c):
            i = tm // 8 - 1 - k
            r0 = pl.multiple_of(i * 8, 8)
            a, b = _scan_block(gr[pl.ds(r0, 8), :], gi[pl.ds(r0, 8), :], tabs, 8, c[0], c[1], True)
            gr[pl.ds(r0, 8), :] = a
            gi[pl.ds(r0, 8), :] = b
            r1 = pl.multiple_of(i * 8 + 8, 8)
            hpr = jnp.where(row0, pltpu.roll(hr[pl.ds(r0, 8), :], 1, 0), pltpu.roll(hr[pl.ds(r1, 8), :], 1, 0))
            hpi = jnp.where(row0, pltpu.roll(hi[pl.ds(r0, 8), :], 1, 0), pltpu.roll(hi[pl.ds(r1, 8), :], 1, 0))
            acr[...] += a * hpr + b * hpi
            aci[...] += b * hpr - a * hpi
            return a[0:1, :], b[0:1, :]

        cr, ci = lax.fori_loop(0, tm // 8, rblk, (car[0:1, :], car[1:2, :]))
        car[0:1, :] = cr
        car[1:2, :] = ci

        grb = gr[...].astype(BF16)
        gib = gi[...].astype(BF16)
        du_ref[...] = (_dot_nt(grb, bbr_ref[...]) + _dot_nt(gib, bbi_ref[...]) + d_ref[...] * dyv).astype(BF16)
        dbr_ref[...] += _dot_tn(ub, grb)
        dbi_ref[...] += _dot_tn(ub, gib)
        dd_ref[...] += jnp.sum(dyv * u, axis=0, keepdims=True)

        @pl.when(t == nt - 1)
        def _():
            da_ref[0:1, :] = jnp.sum(acr[...], axis=0, keepdims=True)
            da_ref[1:2, :] = jnp.sum(aci[...], axis=0, keepdims=True)

    chunk = lambda a, b: pl.BlockSpec((None, a, b), lambda j, t: (j, 0, 0))
    return pl.pallas_call(
        body, name="s5_bwd", grid=(S5_SPLIT, nt),
        in_specs=[pl.BlockSpec((tm, S5_UC), lambda j, t: (nt - 1 - t, j)),
                  pl.BlockSpec((tm, S5_UC), lambda j, t: (nt - 1 - t, j)),
                  ANY,
                  pl.BlockSpec((None, 2, S5_CC), lambda j, t: (jnp.maximum(nt - 2 - t, 0), 0, j)),
                  chunk(S5_UC, S5_CC), chunk(S5_UC, S5_CC), chunk(S5_CC, S5_UC), chunk(S5_CC, S5_UC),
                  pl.BlockSpec((16, 8, S5_CC), lambda j, t: (0, 0, j)),
                  pl.BlockSpec((1, S5_UC), lambda j, t: (0, j))],
        out_specs=[pl.BlockSpec((tm, S5_UC), lambda j, t: (nt - 1 - t, j)),
                   pl.BlockSpec((None, 2, S5_CC), lambda j, t: (j, 0, 0)),
                   chunk(S5_UC, S5_CC), chunk(S5_UC, S5_CC), chunk(S5_CC, S5_UC), chunk(S5_CC, S5_UC),
                   pl.BlockSpec((1, S5_UC), lambda j, t: (0, j))],
        out_shape=[jax.ShapeDtypeStruct(dz.shape, dz.dtype),
                   jax.ShapeDtypeStruct((S5_SPLIT, 2, S5_CC), F32),
                   jax.ShapeDtypeStruct((S5_SPLIT, S5_UC, S5_CC), F32),
                   jax.ShapeDtypeStruct((S5_SPLIT, S5_UC, S5_CC), F32),
                   jax.ShapeDtypeStruct((S5_SPLIT, S5_CC, S5_UC), F32),
                   jax.ShapeDtypeStruct((S5_SPLIT, S5_CC, S5_UC), F32),
                   jax.ShapeDtypeStruct((1, S5_WIDTH), F32)],
        scratch_shapes=[pltpu.VMEM((tm + 8, S5_CC), F32), pltpu.VMEM((tm + 8, S5_CC), F32),
                        pltpu.VMEM((tm, S5_CC), F32), pltpu.VMEM((tm, S5_CC), F32),
                        pltpu.VMEM((2, S5_CC), F32), pltpu.VMEM((8, S5_CC), F32), pltpu.VMEM((8, S5_CC), F32)],
        input_output_aliases={2: 0},
        compiler_params=_cp("parallel", "arbitrary"),
    )(z, dy, dz, ckpt, bbd_re, bbd_im, ccd_re, ccd_im, tab, dskip)


_EYE8 = np.eye(S5_GROUPS // S5_SPLIT, dtype=np.float32)


def _blockdiag(a):
    g, r, c = a.shape
    a = a.reshape(S5_SPLIT, g // S5_SPLIT, r, c)
    out = a[:, :, :, None, :] * _EYE8[None, :, None, :, None].astype(a.dtype)
    return out.reshape(S5_SPLIT, (g // S5_SPLIT) * r, (g // S5_SPLIT) * c)


def _blockdiag_extract(a, r, c):
    n = S5_GROUPS // S5_SPLIT
    a = a.reshape(S5_SPLIT, n, r, n, c)
    d = jnp.stack([a[:, k, :, k, :] for k in range(n)], axis=1)
    return d.reshape(S5_GROUPS, r, c)


def s5_mixer_core_fwd(z, lam_re, lam_im, log_dt, b_re, b_im, c_re, c_im, d_skip):
    bt_re = jnp.swapaxes(b_re, 1, 2)
    bt_im = jnp.swapaxes(b_im, 1, 2)
    logdt = log_dt.reshape(S5_GROUPS, 1)
    bb_re, bb_im = s5_param_fwd(lam_re, lam_im, logdt, bt_re, bt_im)
    flat = lambda a: a.reshape(1, S5_COLS)
    tab = s5_tables(flat(lam_re), flat(lam_im), flat(jnp.broadcast_to(logdt, (S5_GROUPS, S5_STATE))))
    bbd_re = _blockdiag(bb_re).astype(BF16)
    bbd_im = _blockdiag(bb_im).astype(BF16)
    ccd_re = _blockdiag(jnp.swapaxes(c_re, 1, 2)).astype(BF16)
    ccd_im = _blockdiag(jnp.swapaxes(c_im, 1, 2)).astype(BF16)
    dsk = d_skip.reshape(1, S5_WIDTH)
    y, ckpt = s5_fwd(z, bbd_re, bbd_im, ccd_re, ccd_im, tab, dsk)
    saved = (logdt, bt_re, bt_im, bbd_re, bbd_im, ccd_re, ccd_im, tab, dsk, ckpt)
    return y, saved


def s5_mixer_core_bwd(z, dy, dz, lam_re, lam_im, saved):
    logdt, bt_re, bt_im, bbd_re, bbd_im, ccd_re, ccd_im, tab, dsk, ckpt = saved
    dz, da, dbr, dbi, dcr, dci, dd = s5_bwd(z, dy, dz, ckpt, bbd_re, bbd_im, ccd_re, ccd_im, tab, dsk)
    d_ab_re = da[:, 0, :].reshape(S5_GROUPS, S5_STATE)
    d_ab_im = da[:, 1, :].reshape(S5_GROUPS, S5_STATE)
    d_bb_re = _blockdiag_extract(dbr, S5_GROUP, S5_STATE)
    d_bb_im = _blockdiag_extract(dbi, S5_GROUP, S5_STATE)
    g_lr, g_li, g_ld, g_btr, g_bti = s5_param_bwd(lam_re, lam_im, logdt, bt_re, bt_im,
                                                  d_ab_re, d_ab_im, d_bb_re, d_bb_im)
    g_cre = jnp.swapaxes(_blockdiag_extract(dcr, S5_STATE, S5_GROUP), 1, 2)
    g_cim = jnp.swapaxes(_blockdiag_extract(dci, S5_STATE, S5_GROUP), 1, 2)
    grads = dict(lambda_re=g_lr, lambda_im=g_li, log_dt=g_ld.reshape(S5_GROUPS),
                 b_re=jnp.swapaxes(g_btr, 1, 2), b_im=jnp.swapaxes(g_bti, 1, 2),
                 c_re=g_cre, c_im=g_cim, d=dd.reshape(S5_WIDTH))
    return dz, grads


Z_U, Z_GA, Z_VAL, Z_GLU, Z_GB = range(5)


def conv_fwd(z, conv_w, conv_b, tm=ROW_TILE):
    L = z.shape[0]
    tm = min(tm, L)
    nt = L // tm
    hb = tm // CONV_HALO
    C = CONV_WIDTH

    def body(val_ref, glu_ref, valh_ref, gluh_ref, w_ref, b_ref, c_ref, vbuf):
        live = (pl.program_id(0) > 0).astype(F32)
        vbuf[0:CONV_HALO, :] = valh_ref[...] * _sigmoid(gluh_ref[...]) * live
        vbuf[CONV_HALO:, :] = val_ref[...] * _sigmoid(glu_ref[...])
        acc = jnp.broadcast_to(b_ref[...], (tm, C))
        for k in range(CONV_KERNEL):
            acc = acc + w_ref[k:k + 1, :] * vbuf[pl.ds(CONV_HALO - CONV_KERNEL + 1 + k, tm), :]
        c_ref[...] = acc

    cur = lambda col: pl.BlockSpec((tm, C), lambda t: (t, col))
    prev = lambda col: pl.BlockSpec((CONV_HALO, C), lambda t: (jnp.maximum(t * hb - 1, 0), col))
    return pl.pallas_call(
        body, name="conv_fwd", grid=(nt,),
        in_specs=[cur(Z_VAL), cur(Z_GLU), prev(Z_VAL), prev(Z_GLU), _full(conv_w.shape), _full(conv_b.shape)],
        out_specs=pl.BlockSpec((tm, C), lambda t: (t, 0)),
        out_shape=jax.ShapeDtypeStruct((L, C), F32),
        scratch_shapes=[pltpu.VMEM((tm + CONV_HALO, C), F32)],
        compiler_params=_cp("parallel"),
    )(z, z, z, z, conv_w, conv_b)


def conv_bwd(z, dc, dz, conv_w, tm=ROW_TILE):
    L = z.shape[0]
    tm = min(tm, L)
    nt = L // tm
    hb = tm // CONV_HALO
    nh = L // CONV_HALO
    C = CONV_WIDTH
    off = CONV_HALO - CONV_KERNEL + 1

    def body(val_ref, glu_ref, valh_ref, gluh_ref, dc_ref, dcn_ref, dz_ref, w_ref, dvg_ref, dw_ref, db_ref,
             vbuf, dbuf):
        t = pl.program_id(0)

        @pl.when(t == 0)
        def _():
            dw_ref[...] = jnp.zeros_like(dw_ref)
            db_ref[...] = jnp.zeros_like(db_ref)

        val = val_ref[...]
        sg = _sigmoid(glu_ref[...])
        vbuf[0:CONV_HALO, :] = valh_ref[...] * _sigmoid(gluh_ref[...]) * (t > 0).astype(F32)
        vbuf[CONV_HALO:, :] = val * sg
        dcv = dc_ref[...]
        dbuf[0:tm, :] = dcv
        dbuf[tm:, :] = dcn_ref[...] * (t < nt - 1).astype(F32)
        dv = jnp.zeros((tm, C), F32)
        for k in range(CONV_KERNEL):
            dv = dv + w_ref[k:k + 1, :] * dbuf[pl.ds(CONV_KERNEL - 1 - k, tm), :]
            dw_ref[k:k + 1, :] += jnp.sum(dcv * vbuf[pl.ds(off + k, tm), :], axis=0, keepdims=True)
        db_ref[...] += jnp.sum(dcv, axis=0, keepdims=True)
        dvg_ref[:, 0:C] = (dv * sg).astype(BF16)
        dvg_ref[:, C:] = (dv * val * sg * (1.0 - sg)).astype(BF16)

    cur = lambda col: pl.BlockSpec((tm, C), lambda t: (t, col))
    prev = lambda col: pl.BlockSpec((CONV_HALO, C), lambda t: (jnp.maximum(t * hb - 1, 0), col))
    nxt = pl.BlockSpec((CONV_HALO, C), lambda t: (jnp.minimum((t + 1) * hb, nh - 1), 0))
    row = pl.BlockSpec((tm, C), lambda t: (t, 0))
    return pl.pallas_call(
        body, name="conv_bwd", grid=(nt,),
        in_specs=[cur(Z_VAL), cur(Z_GLU), prev(Z_VAL), prev(Z_GLU), row, nxt, ANY, _full(conv_w.shape)],
        out_specs=[pl.BlockSpec((tm, 2 * C), lambda t: (t, 1)), _full((CONV_HALO, C)), _full((1, C))],
        out_shape=[jax.ShapeDtypeStruct(dz.shape, dz.dtype),
                   jax.ShapeDtypeStruct((CONV_HALO, C), F32), jax.ShapeDtypeStruct((1, C), F32)],
        scratch_shapes=[pltpu.VMEM((tm + CONV_HALO, C), F32), pltpu.VMEM((tm + CONV_HALO, C), F32)],
        input_output_aliases={6: 0},
        compiler_params=_cp("arbitrary"),
    )(z, z, z, z, dc, dc, dz, conv_w)


def _ln_parts(c):
    mu = jnp.mean(c, axis=-1, keepdims=True)
    cc = c - mu
    rstd = lax.rsqrt(jnp.mean(cc * cc, axis=-1, keepdims=True) + EPS)
    return rstd, cc * rstd


def _ev_tail_branches(ys, c, wglu, bglu, lng, lnb):
    z1 = _gelu(ys)
    z1b = z1.astype(BF16)
    sg = _sigmoid(_dot_rows(z1b, wglu) + bglu)
    out = z1 * sg
    rstd, chat = _ln_parts(c)
    cn = chat * lng + lnb
    return z1, z1b, sg, out, rstd, chat, cn


def ev_tail_fwd(ys, z, c, x0, wglu, bglu, lng, lnb, wout, tm=ROW_TILE):
    L, D = x0.shape
    tm = min(tm, L)
    W = S5_WIDTH

    def body(ys_ref, ga_ref, c_ref, gb_ref, x_ref, wglu_ref, bglu_ref, lng_ref, lnb_ref, wout_ref, o_ref):
        _, _, _, out, _, _, cn = _ev_tail_branches(ys_ref[...], c_ref[...], wglu_ref, bglu_ref[...],
                                                   lng_ref[...], lnb_ref[...])
        ya = (out * _silu(ga_ref[...])).astype(BF16)
        yb = (_silu(cn) * _silu(gb_ref[...])).astype(BF16)
        o_ref[...] = x_ref[...] + _dot_rows(jnp.concatenate([ya, yb], axis=1), wout_ref)

    row = lambda n, col=0: pl.BlockSpec((tm, n), lambda t: (t, col))
    return pl.pallas_call(
        body, name="ev_tail_fwd", grid=(L // tm,),
        in_specs=[row(W), row(W, Z_GA), row(W), row(W, Z_GB), row(D), _full(wglu.shape), _full(bglu.shape),
                  _full(lng.shape), _full(lnb.shape), _full(wout.shape)],
        out_specs=row(D), out_shape=jax.ShapeDtypeStruct((L, D), F32), compiler_params=_cp("parallel"),
    )(ys, z, c, z, x0, wglu, bglu, lng, lnb, wout)


def ev_tail_bwd(ys, z, c, dx1, wglu, bglu, lng, lnb, wout, tm=ROW_TILE):
    L, D = dx1.shape
    tm = min(tm, L)
    W = S5_WIDTH

    def body(ys_ref, ga_ref, c_ref, gb_ref, dx_ref, wglu_ref, bglu_ref, lng_ref, lnb_ref, wout_ref,
             dys_ref, dc_ref, dz_ref, r_ref, z1_ref, dt_ref, dbg_ref, dlg_ref, dlb_ref):
        @pl.when(pl.program_id(0) == 0)
        def _():
            for r in (dbg_ref, dlg_ref, dlb_ref):
                r[...] = jnp.zeros_like(r)

        ys, ga, gb = ys_ref[...], ga_ref[...], gb_ref[...]
        z1, z1b, sg, out, rstd, chat, cn = _ev_tail_branches(ys, c_ref[...], wglu_ref, bglu_ref[...],
                                                             lng_ref[...], lnb_ref[...])
        sga, sgb, scn = _silu(ga), _silu(gb), _silu(cn)
        r_ref[:, 0:W] = (out * sga).astype(BF16)
        r_ref[:, W:] = (scn * sgb).astype(BF16)
        dr = _dot_nt_rows(dx_ref[...].astype(BF16), wout_ref)
        dra, drb = dr[:, 0:W], dr[:, W:]
        dz_ref[...] = jnp.zeros_like(dz_ref)
        dz_ref[:, Z_GA * W:(Z_GA + 1) * W] = (dra * out * _dsilu(ga)).astype(BF16)
        dout = dra * sga
        dt = dout * z1 * sg * (1.0 - sg)
        dtb = dt.astype(BF16)
        dz1 = dout * sg + _dot_nt_rows(dtb, wglu_ref)
        dys_ref[...] = dz1 * _dgelu(ys)
        z1_ref[...] = z1b
        dt_ref[...] = dtb
        dbg_ref[...] += jnp.sum(dt, axis=0, keepdims=True)
        dz_ref[:, Z_GB * W:(Z_GB + 1) * W] = (drb * scn * _dsilu(gb)).astype(BF16)
        dcn = drb * sgb * _dsilu(cn)
        dlg_ref[...] += jnp.sum(dcn * chat, axis=0, keepdims=True)
        dlb_ref[...] += jnp.sum(dcn, axis=0, keepdims=True)
        dch = dcn * lng_ref[...]
        dc_ref[...] = rstd * (dch - jnp.mean(dch, axis=-1, keepdims=True)
                              - chat * jnp.mean(dch * chat, axis=-1, keepdims=True))

    row = lambda n, col=0: pl.BlockSpec((tm, n), lambda t: (t, col))
    f = lambda n, dt: jax.ShapeDtypeStruct((L, n), dt)
    vec = jax.ShapeDtypeStruct((1, W), F32)
    return pl.pallas_call(
        body, name="ev_tail_bwd", grid=(L // tm,),
        in_specs=[row(W), row(W, Z_GA), row(W), row(W, Z_GB), row(D), _full(wglu.shape), _full(bglu.shape),
                  _full(lng.shape), _full(lnb.shape), _full(wout.shape)],
        out_specs=[row(W), row(W), row(EVEN_IN), row(D), row(W), row(W), _full((1, W)), _full((1, W)), _full((1, W))],
        out_shape=[f(W, F32), f(W, F32), f(EVEN_IN, BF16), f(D, BF16), f(W, BF16), f(W, BF16), vec, vec, vec],
        compiler_params=_cp("arbitrary"),
    )(ys, z, c, z, dx1, wglu, bglu, lng, lnb, wout)


XA_SCALE = XA_HEAD_DIM ** -0.5


def _xa_forward(xv, g, wqg, kv):
    D = D_MODEL
    _, xhat = _rms_parts(xv)
    hb = (xhat * g).astype(BF16)
    qb = _dot_cols(hb, wqg, (0, 1)).astype(BF16)
    gate = _dot_cols(hb, wqg, (2, 3))
    ps, os_ = [], []
    for h in range(XA_HEADS):
        lo, hi = h * XA_HEAD_DIM, (h + 1) * XA_HEAD_DIM
        s = _dot_nt(qb[:, lo:hi], kv[:, lo:hi]) * XA_SCALE
        e = jnp.exp(s - jnp.max(s, axis=-1, keepdims=True))
        p = e / jnp.sum(e, axis=-1, keepdims=True)
        ps.append(p)
        os_.append(_dot(p.astype(BF16), kv[:, D + lo:D + hi]))
    return hb, qb, gate, ps, jnp.concatenate(os_, axis=1)


def xa_fwd(x, g, wqg, kv, wo, layer, name, tm=ROW_TILE):
    L, D = x.shape
    tm = min(tm, L)

    def body(x_ref, g_ref, wqg_ref, kv_ref, wo_ref, o_ref):
        xv = x_ref[...]
        _, _, gate, _, o = _xa_forward(xv, g_ref[...], wqg_ref, kv_ref[...])
        o_ref[...] = xv + _dot_rows((o * _silu(gate)).astype(BF16), wo_ref)

    row = pl.BlockSpec((tm, D), lambda t: (t, 0))
    return pl.pallas_call(
        body, name=name, grid=(L // tm,),
        in_specs=[row, _full(g.shape), _wspec(wqg, layer), _full(kv.shape), _wspec(wo, layer)],
        out_specs=row, out_shape=jax.ShapeDtypeStruct((L, D), F32), compiler_params=_cp("parallel"),
    )(x, g, wqg, kv, wo)


def xa_bwd(x, dxo, g, wqg, kv, wo, layer, name, tm=ROW_TILE):
    L, D = x.shape
    tm = min(tm, L)

    def body(x_ref, dxo_ref, g_ref, wqg_ref, kv_ref, wo_ref, dx_ref, dqg_ref, h_ref, r_ref, dkv_ref, dg_ref):
        @pl.when(pl.program_id(0) == 0)
        def _():
            dkv_ref[...] = jnp.zeros_like(dkv_ref)
            dg_ref[...] = jnp.zeros_like(dg_ref)

        xv = x_ref[...]
        kv = kv_ref[...]
        hb, qb, gate, ps, o = _xa_forward(xv, g_ref[...], wqg_ref, kv)
        sgate = _silu(gate)
        h_ref[...] = hb
        r_ref[...] = (o * sgate).astype(BF16)
        dxo = dxo_ref[...]
        dr = _dot_nt_rows(dxo.astype(BF16), wo_ref)
        do = dr * sgate
        dqg_ref[:, D:] = (dr * o * _dsilu(gate)).astype(BF16)
        dob = do.astype(BF16)
        for h in range(XA_HEADS):
            lo, hi = h * XA_HEAD_DIM, (h + 1) * XA_HEAD_DIM
            p = ps[h]
            pb = p.astype(BF16)
            dp = _dot_nt(dob[:, lo:hi], kv[:, D + lo:D + hi])
            dkv_ref[:, D + lo:D + hi] += _dot_tn(pb, dob[:, lo:hi])
            ds = p * (dp - jnp.sum(dp * p, axis=-1, keepdims=True))
            dsb = (ds * XA_SCALE).astype(BF16)
            dqg_ref[:, lo:hi] = _dot(dsb, kv[:, lo:hi]).astype(BF16)
            dkv_ref[:, lo:hi] += _dot_tn(dsb, qb[:, lo:hi])
        dh = _dot_nt_cols(_col_pieces(dqg_ref[...], D // 2), wqg_ref)
        dx, dg = _rms_bwd(xv, g_ref[...], dh)
        dx_ref[...] = dxo + dx
        dg_ref[...] += dg

    row = lambda n: pl.BlockSpec((tm, n), lambda t: (t, 0))
    return pl.pallas_call(
        body, name=name, grid=(L // tm,),
        in_specs=[row(D), row(D), _full(g.shape), _wspec(wqg, layer), _full(kv.shape), _wspec(wo, layer)],
        out_specs=[row(D), row(2 * D), row(D), row(D), _full(kv.shape), _full((1, D))],
        out_shape=[jax.ShapeDtypeStruct((L, D), F32), jax.ShapeDtypeStruct((L, 2 * D), BF16),
                   jax.ShapeDtypeStruct((L, D), BF16), jax.ShapeDtypeStruct((L, D), BF16),
                   jax.ShapeDtypeStruct(kv.shape, F32), jax.ShapeDtypeStruct((1, D), F32)],
        compiler_params=_cp("arbitrary"),
    )(x, dxo, g, wqg, kv, wo)


ATT_SCALE = ATT_HEAD_DIM ** -0.5
ATT_PAIRS = ATT_HEADS // 2
SKEW_LANES = 1024
REL_LANES = 384


def _skew(x, left):
    amt = (ATT_QB - 1) - lax.broadcasted_iota(jnp.int32, (ATT_QB, 1), 0)
    for bit in range(8):
        sh = (SKEW_LANES - (1 << bit)) if left else (1 << bit)
        x = jnp.where(((amt >> bit) & 1) == 1, pltpu.roll(x, sh, 1), x)
    return x


def _dist_onehot(shape, dist_axis):
    j = lax.broadcasted_iota(jnp.int32, shape, dist_axis)
    r = lax.broadcasted_iota(jnp.int32, shape, 1 - dist_axis)
    return (jnp.clip((ATT_WIN - 1) - j, -MAX_REL, MAX_REL) + MAX_REL == r).astype(BF16)


def _dot_exact(v, onehot):
    acc = jnp.zeros((v.shape[0], onehot.shape[1]), F32)
    rem = v
    for _ in range(3):
        part = rem.astype(BF16)
        acc = acc + _dot(part, onehot)
        rem = rem - part.astype(F32)
    return acc


def att_bias(rel_bias):
    H = rel_bias.shape[0]
    rb = jnp.pad(rel_bias, ((0, 0), (0, REL_LANES - rel_bias.shape[1]))).reshape(H, 1, REL_LANES)

    def body(rb_ref, o_ref):
        by_col = _dot_exact(jnp.broadcast_to(rb_ref[...], (8, REL_LANES)), _dist_onehot((REL_LANES, SKEW_LANES), 1))
        x = _skew(jnp.broadcast_to(by_col[0:1, :], (ATT_QB, SKEW_LANES)), left=True)[:, 0:ATT_WIN]
        qc = lax.broadcasted_iota(jnp.int32, (ATT_QB, 1), 0) // CHUNK + LEFT_CHUNKS
        kc = lax.broadcasted_iota(jnp.int32, (1, ATT_WIN), 1) // CHUNK
        dc = qc - kc
        o_ref[...] = jnp.where((dc >= 0) & (dc <= LEFT_CHUNKS), x, NEG)

    return pl.pallas_call(
        body, name="att_bias", grid=(H,),
        in_specs=[pl.BlockSpec((None, 1, REL_LANES), lambda h: (h, 0, 0))],
        out_specs=pl.BlockSpec((None, ATT_QB, ATT_WIN), lambda h: (h, 0, 0)),
        out_shape=jax.ShapeDtypeStruct((H, ATT_QB, ATT_WIN), F32), compiler_params=_cp("parallel"),
    )(rb)


def relbias_bwd(dbias):
    H = dbias.shape[0]

    def body(x_ref, o_ref):
        x = jnp.concatenate([x_ref[...], jnp.zeros((ATT_QB, SKEW_LANES - ATT_WIN), F32)], axis=1)
        col = jnp.sum(_skew(x, left=False), axis=0, keepdims=True)
        o_ref[...] = _dot_exact(jnp.broadcast_to(col, (8, SKEW_LANES)), _dist_onehot((SKEW_LANES, REL_LANES), 0))

    out = pl.pallas_call(
        body, name="relbias_bwd", grid=(H,),
        in_specs=[pl.BlockSpec((None, ATT_QB, ATT_WIN), lambda h: (h, 0, 0))],
        out_specs=pl.BlockSpec((None, 8, REL_LANES), lambda h: (h, 0, 0)),
        out_shape=jax.ShapeDtypeStruct((H, 8, REL_LANES), F32), compiler_params=_cp("parallel"),
    )(dbias)
    return out[:, 0, :2 * MAX_REL + 1]


def _ca_scores(qh, kw, bias, kvalid):
    s = _dot_nt(qh, kw) * ATT_SCALE + bias
    s = jnp.where(kvalid, s, NEG)
    e = jnp.exp(s - jnp.max(s, axis=-1, keepdims=True))
    return e / jnp.sum(e, axis=-1, keepdims=True)


def ca_fwd(q, kvp, gate, bias):
    L, D = q.shape
    Lp = kvp.shape[0]
    nb = L // ATT_QB

    def body(q_ref, k_ref, v_ref, g_ref, b_ref, r_ref):
        w = pl.multiple_of(pl.program_id(1) * ATT_QB, ATT_QB)
        kw = k_ref[pl.ds(w, ATT_WIN), :]
        vw = v_ref[pl.ds(w, ATT_WIN), :]
        qv = q_ref[...]
        first = lax.broadcasted_iota(jnp.int32, (1, 128), 1) < ATT_HEAD_DIM
        kvalid = (w + lax.broadcasted_iota(jnp.int32, (1, ATT_WIN), 1)) >= ATT_PAD
        outs = []
        for hh, m in enumerate((first, jnp.logical_not(first))):
            p = _ca_scores(jnp.where(m, qv, jnp.zeros_like(qv)), kw, b_ref[hh], kvalid)
            outs.append(_dot(p.astype(BF16), vw))
        o = jnp.where(first, outs[0], outs[1])
        r_ref[...] = (o * _silu(g_ref[...])).astype(BF16)

    blk = pl.BlockSpec((ATT_QB, 128), lambda hp, b: (b, hp))
    return pl.pallas_call(
        body, name="ca_fwd", grid=(ATT_PAIRS, nb),
        in_specs=[blk, pl.BlockSpec((Lp, 128), lambda hp, b: (0, hp)),
                  pl.BlockSpec((Lp, 128), lambda hp, b: (0, ATT_PAIRS + hp)), blk,
                  pl.BlockSpec((2, ATT_QB, ATT_WIN), lambda hp, b: (hp, 0, 0))],
        out_specs=blk, out_shape=jax.ShapeDtypeStruct((L, D), BF16),
        compiler_params=_cp("parallel", "arbitrary"),
    )(q, kvp, kvp, gate, bias)


def ca_bwd(q, kvp, gate, bias, dr):
    L, D = q.shape
    Lp = kvp.shape[0]
    nb = L // ATT_QB

    def body(q_ref, k_ref, v_ref, g_ref, b_ref, dr_ref, dq_ref, dg_ref, dk_ref, dv_ref, db_ref):
        b = pl.program_id(1)

        @pl.when(b == 0)
        def _():
            for r in (dk_ref, dv_ref, db_ref):
                r[...] = jnp.zeros_like(r)

        w = pl.multiple_of(b * ATT_QB, ATT_QB)
        kw = k_ref[pl.ds(w, ATT_WIN), :]
        vw = v_ref[pl.ds(w, ATT_WIN), :]
        qv = q_ref[...]
        gate_v = g_ref[...]
        drv = dr_ref[...]
        do = drv * _silu(gate_v)
        first = lax.broadcasted_iota(jnp.int32, (1, 128), 1) < ATT_HEAD_DIM
        kvalid = (w + lax.broadcasted_iota(jnp.int32, (1, ATT_WIN), 1)) >= ATT_PAD
        outs, dqs = [], []
        dkw = jnp.zeros((ATT_WIN, 128), F32)
        dvw = jnp.zeros((ATT_WIN, 128), F32)
        for hh, m in enumerate((first, jnp.logical_not(first))):
            qh = jnp.where(m, qv, jnp.zeros_like(qv))
            p = _ca_scores(qh, kw, b_ref[hh], kvalid)
            pb = p.astype(BF16)
            outs.append(_dot(pb, vw))
            doh = jnp.where(m, do, 0.0).astype(BF16)
            dp = _dot_nt(doh, vw)
            dvw = dvw + _dot_tn(pb, doh)
            ds = p * (dp - jnp.sum(dp * p, axis=-1, keepdims=True))
            db_ref[hh] += ds
            dsb = (ds * ATT_SCALE).astype(BF16)
            dqs.append(_dot(dsb, kw))
            dkw = dkw + _dot_tn(dsb, qh)
        o = jnp.where(first, outs[0], outs[1])
        dg_ref[...] = (drv * o * _dsilu(gate_v)).astype(BF16)
        dq_ref[...] = jnp.where(first, dqs[0], dqs[1]).astype(BF16)
        dk_ref[pl.ds(w, ATT_WIN), :] += dkw
        dv_ref[pl.ds(w, ATT_WIN), :] += dvw

    blk = pl.BlockSpec((ATT_QB, 128), lambda hp, b: (b, hp))
    kblk = pl.BlockSpec((Lp, 128), lambda hp, b: (0, hp))
    vblk = pl.BlockSpec((Lp, 128), lambda hp, b: (0, ATT_PAIRS + hp))
    bblk = pl.BlockSpec((2, ATT_QB, ATT_WIN), lambda hp, b: (hp, 0, 0))
    return pl.pallas_call(
        body, name="ca_bwd", grid=(ATT_PAIRS, nb),
        in_specs=[blk, kblk, vblk, blk, bblk, blk],
        out_specs=[blk, blk, kblk, kblk, bblk],
        out_shape=[jax.ShapeDtypeStruct((L, D), BF16), jax.ShapeDtypeStruct((L, D), BF16),
                   jax.ShapeDtypeStruct((Lp, D), F32), jax.ShapeDtypeStruct((Lp, D), F32),
                   jax.ShapeDtypeStruct(bias.shape, F32)],
        compiler_params=_cp("parallel", "arbitrary"),
    )(q, kvp, kvp, gate, bias, dr)


def loss_bwd(x, target, g, tm=ROW_TILE):
    L, D = x.shape
    tm = min(tm, L)

    def body(x_ref, t_ref, g_ref, loss_ref, dx_ref, dg_ref):
        @pl.when(pl.program_id(0) == 0)
        def _():
            loss_ref[...] = jnp.zeros_like(loss_ref)
            dg_ref[...] = jnp.zeros_like(dg_ref)

        xv = x_ref[...]
        gv = g_ref[...]
        _, xhat = _rms_parts(xv)
        err = xhat * gv - t_ref[...]
        loss_ref[...] += 0.5 * jnp.sum(jnp.sum(err * err, axis=-1, keepdims=True), axis=0, keepdims=True) / D
        dx, dg = _rms_bwd(xv, gv, err / D)
        dx_ref[...] = dx
        dg_ref[...] += dg

    row = pl.BlockSpec((tm, D), lambda t: (t, 0))
    return pl.pallas_call(
        body, name="loss_bwd", grid=(L // tm,),
        in_specs=[row, row, _full(g.shape)],
        out_specs=[_full((1, 128)), row, _full((1, D))],
        out_shape=[jax.ShapeDtypeStruct((1, 128), F32), jax.ShapeDtypeStruct((L, D), F32),
                   jax.ShapeDtypeStruct((1, D), F32)],
        compiler_params=_cp("arbitrary"),
    )(x, target, g)


_ADAM_C1 = 1.0 / (1.0 - ADAM_B1 ** ADAM_STEP)
_ADAM_C2 = 1.0 / (1.0 - ADAM_B2 ** ADAM_STEP)


def _adam_update(w, g, m, v):
    mn = ADAM_B1 * m + (1.0 - ADAM_B1) * g
    vn = ADAM_B2 * v + (1.0 - ADAM_B2) * g * g
    delta = -ADAM_LR * ((mn * _ADAM_C1) / (jnp.sqrt(vn * _ADAM_C2) + ADAM_EPS) + ADAM_WD * w)
    return delta, mn, vn


def adamw(w, g, m, v, name, tr=512):
    R, C = w.shape
    tr = min(tr, R)

    def body(w_ref, g_ref, m_ref, v_ref, d_ref, mo_ref, vo_ref):
        d_ref[...], mo_ref[...], vo_ref[...] = _adam_update(w_ref[...], g_ref[...], m_ref[...], v_ref[...])

    blk = pl.BlockSpec((tr, C), lambda i: (i, 0))
    sh = jax.ShapeDtypeStruct((R, C), F32)
    return pl.pallas_call(
        body, name=name, grid=(R // tr,), in_specs=[blk] * 4, out_specs=[blk] * 3,
        out_shape=[sh] * 3, compiler_params=_cp("parallel"),
    )(w, g, m, v)


def adamw_allreduce(gathered, w, m, v, shard, name):
    R, C = w.shape
    sharded = gathered.shape[2] != C

    def body(s_ref, ga_ref, w_ref, m_ref, v_ref, g_ref, d_ref, mo_ref, vo_ref):
        g = ga_ref[0]
        for d in range(1, N_DEV):
            g = g + ga_ref[d]
        g_ref[...] = g
        d_ref[...], mo_ref[...], vo_ref[...] = _adam_update(w_ref[...], g, m_ref[...], v_ref[...])

    blk = pl.BlockSpec((R, C), lambda i, s_ref: (0, 0))
    gblk = pl.BlockSpec((N_DEV, R, C), (lambda i, s_ref: (0, 0, s_ref[0])) if sharded else (lambda i, s_ref: (0, 0, 0)))
    sh = jax.ShapeDtypeStruct((R, C), F32)
    return pl.pallas_call(
        body, name=name,
        grid_spec=pltpu.PrefetchScalarGridSpec(num_scalar_prefetch=1, grid=(1,), in_specs=[gblk, blk, blk, blk],
                                               out_specs=[blk] * 4),
        out_shape=[sh] * 4, compiler_params=_cp("arbitrary"),
    )(shard, gathered, w, m, v)


def local_step(x, mem, target, p, gw):
    row = lambda a: a.reshape(1, -1)
    D = D_MODEL
    L = x.shape[0]
    g, big = {}, {}

    memn_b = rms_fwd(mem, row(p["mem_norm_g"]), "mem_norm")
    kvs = [mm_cols(memn_b, gw["xa_w_kv"], l, f"xa_kv{l}", BF16) for l in range(2)]

    z, h0b = norm_mm(x, p["ev_norm_g"], gw["ev_w_in"], [((0, 1, 2, 3), F32, 0)], "ev_in")
    ys, s5_saved = s5_mixer_core_fwd(z, p["ev_s5_lambda_re"][0], p["ev_s5_lambda_im"][0], p["ev_s5_log_dt"][0],
                                     p["ev_s5_b_re"][0], p["ev_s5_b_im"][0], p["ev_s5_c_re"][0], p["ev_s5_c_im"][0],
                                     p["ev_s5_d"][0])
    conv_w = p["ev_conv_w"][0]
    c = conv_fwd(z, conv_w, p["ev_conv_b"])
    tail = (gw["ev_s5_glu_w"], p["ev_s5_glu_b"], p["ev_conv_ln_g"], p["ev_conv_ln_b"], gw["ev_w_out"])
    x1 = ev_tail_fwd(ys, z, c, x, *tail)
    xa0 = (row(p["xa_norm_g"][0]), gw["xa_w_qg"], kvs[0], gw["xa_w_o"], 0)
    x2 = xa_fwd(x1, *xa0, "xa_fwd0")

    q, kvp, gate, h1b = norm_mm(x2, p["od_norm_g"], gw["od_w_in"],
                                [((0,), BF16, 0), ((1, 2), BF16, ATT_PAD), ((3,), F32, 0)], "od_in")
    kvp = zero_rows(kvp, ATT_PAD, "od_kv_pad")
    bias = att_bias(p["od_rel_bias"][0])
    r = ca_fwd(q, kvp, gate, bias)
    x3 = mm_res(r, gw["od_w_out"], x2, "od_out")
    xa1 = (row(p["xa_norm_g"][1]), gw["xa_w_qg"], kvs[1], gw["xa_w_o"], 1)
    x4 = xa_fwd(x3, *xa1, "xa_fwd1")

    loss, dx4, dgf = loss_bwd(x4, target, row(p["final_norm_g"]))
    g["final_norm_g"] = dgf.reshape(D)

    dx3, dqg1, hx1, rx1, dkv1, dgxa1 = xa_bwd(x3, dx4, *xa1, "xa_bwd1")
    dwqg = mm_tn(hx1, dqg1, "xa_dwqg1", ("cols", 1))
    dwo = mm_tn(rx1, dx4, "xa_dwo1", ("rows", 1))

    big["od_w_out"] = mm_tn(r, dx3, "od_dwout", ("rows",))
    dr = mm_nt_rows(dx3, gw["od_w_out"], "od_out_bwd")
    dq, dgate, dkp, dvp, dbias = ca_bwd(q, kvp, gate, bias, dr)
    pieces, offs = (dq, dkp, dvp, dgate), (0, ATT_PAD, ATT_PAD, 0)
    dwin = None
    for s in range(N_CHIPS):
        dwin = mm_tn(h1b, pieces[s], f"od_dwin{s}", ("slab", s), into=dwin, b_off=offs[s])
    big["od_w_in"] = dwin
    dx2, dgod = mm_nt_normbwd(pieces, offs, gw["od_w_in"], x2, p["od_norm_g"], dx3, "od_in_bwd")
    g["od_norm_g"] = dgod
    g["od_rel_bias"] = relbias_bwd(dbias)[None]

    dx1, dqg0, hx0, rx0, dkv0, dgxa0 = xa_bwd(x1, dx2, *xa0, "xa_bwd0")
    big["xa_w_qg"] = mm_tn(hx0, dqg0, "xa_dwqg0", ("cols", 0), into=dwqg)
    big["xa_w_o"] = mm_tn(rx0, dx2, "xa_dwo0", ("rows", 0), into=dwo)
    g["xa_norm_g"] = jnp.concatenate([dgxa0, dgxa1], axis=0)

    dys, dc, dz, ra, z1b, dtb, dbglu, dlng, dlnb = ev_tail_bwd(ys, z, c, dx1, *tail)
    big["ev_w_out"] = mm_tn(ra, dx1, "ev_dwout", ("rows",))
    big["ev_s5_glu_w"] = mm_tn(z1b, dtb, "ev_dwglu", ("rows",))
    g["ev_s5_glu_b"], g["ev_conv_ln_g"], g["ev_conv_ln_b"] = dbglu, dlng, dlnb
    dz, dconvw, dconvb = conv_bwd(z, dc, dz, conv_w)
    g["ev_conv_w"] = dconvw[None, :CONV_KERNEL]
    g["ev_conv_b"] = dconvb
    dz, s5g = s5_mixer_core_bwd(z, dys, dz, p["ev_s5_lambda_re"][0], p["ev_s5_lambda_im"][0], s5_saved)
    for n, v in s5g.items():
        g["ev_s5_" + n] = v[None]
    big["ev_w_in"] = mm_tn(h0b, dz, "ev_dwin", ("cols",))
    grad_x, dgev = mm_nt_normbwd((dz,), (0,), gw["ev_w_in"], x, p["ev_norm_g"], dx1, "ev_in_bwd")
    g["ev_norm_g"] = dgev

    dwkv = mm_tn(memn_b, dkv1, "xa_dwkv1", ("cols", 1), bl=MEM_LEN)
    big["xa_w_kv"] = mm_tn(memn_b, dkv0, "xa_dwkv0", ("cols", 0), into=dwkv, bl=MEM_LEN)
    dmem0 = mm_nt_cols(dkv0, gw["xa_w_kv"], 0, "xa_kv_bwd0")
    dmem1 = mm_nt_cols(dkv1, gw["xa_w_kv"], 1, "xa_kv_bwd1")
    g["mem_norm_g"] = rms_dgain(mem, dmem0, dmem1, "mem_norm_bwd").reshape(D)
    return loss, grad_x, g, big


def _me():
    return lax.axis_index("x"), lax.axis_index("y"), lax.axis_index("c")


def _other_chips(x, y):
    return [(1 - x, y), (x, 1 - y), (1 - x, 1 - y)]


def _remote(src, dst, send_sems, recv_sems, k, to):
    return pltpu.make_async_remote_copy(src_ref=src, dst_ref=dst, send_sem=send_sems.at[k], recv_sem=recv_sems.at[k],
                                        device_id=to, device_id_type=MESH)


def _rows_half(ref, h):
    H = ref.shape[-2] // 2
    return ref.at[(slice(None),) * (len(ref.shape) - 2) + (pl.ds(h * H, H), slice(None))]


def allgather_chip_blocks(halved, whole):
    nh, nw = len(halved), len(whole)
    n = nh + nw

    def body(*refs):
        ins, outs = refs[:n], refs[n:2 * n]
        send_sems, recv_sems, local_sems = refs[2 * n:]
        x, y, c = _me()
        sib = (x, y, 1 - c)
        chips = _other_chips(x, y)
        me = 2 * x + y
        local = [pltpu.make_async_copy(ins[i], outs[i].at[me], local_sems.at[i]) for i in range(n)]
        for cp in local:
            cp.start()
        first, passed = [], []
        for i in range(n):
            for j, (cx, cy) in enumerate(chips):
                if i < nh:
                    src, dst = _rows_half(ins[i], c), _rows_half(outs[i].at[me], c)
                    k = 6 * i + j
                else:
                    src, dst = ins[i], outs[i].at[me]
                    k = 6 * nh + 3 * (i - nh) + j
                first.append(_remote(src, dst, send_sems, recv_sems, k, (cx, cy, c)))
        for cp in first:
            cp.start()
        for j, (cx, cy) in enumerate(chips):
            for i in range(nh):
                got = _rows_half(outs[i].at[2 * cx + cy], c)
                _remote(got, got, send_sems, recv_sems, 6 * i + j, (cx, cy, c)).wait_recv()
                fw = _remote(got, got, send_sems, recv_sems, 6 * i + 3 + j, sib)
                fw.start()
                passed.append(fw)
        for j, (cx, cy) in enumerate(chips):
            for i in range(nh):
                got = _rows_half(outs[i].at[2 * cx + cy], 1 - c)
                _remote(got, got, send_sems, recv_sems, 6 * i + 3 + j, sib).wait_recv()
            for i in range(nh, n):
                got = outs[i].at[2 * cx + cy]
                _remote(got, got, send_sems, recv_sems, 6 * nh + 3 * (i - nh) + j, (cx, cy, c)).wait_recv()
        for cp in first + passed:
            cp.wait_send()
        for cp in local:
            cp.wait()

    arrays = list(halved) + list(whole)
    nsem = 6 * nh + 3 * nw
    return pl.pallas_call(
        body, name="allgather_chip_blocks", in_specs=[ANY] * n, out_specs=[ANY] * n,
        out_shape=[jax.ShapeDtypeStruct((N_CHIPS,) + a.shape, a.dtype) for a in arrays],
        scratch_shapes=[pltpu.SemaphoreType.DMA((nsem,)), pltpu.SemaphoreType.DMA((nsem,)),
                        pltpu.SemaphoreType.DMA((n,))],
    )(*arrays)


def allgather_devices(vs):
    n = len(vs)

    def body(*refs):
        ins, outs = refs[:n], refs[n:2 * n]
        send_sems, recv_sems, local_sems = refs[2 * n:]
        x, y, c = _me()
        sib = (x, y, 1 - c)
        chips = _other_chips(x, y)
        me = 4 * x + 2 * y + c
        local = [pltpu.make_async_copy(ins[i], outs[i].at[me], local_sems.at[i]) for i in range(n)]
        for cp in local:
            cp.start()
        first, passed = [], []
        for i in range(n):
            first.append(_remote(ins[i], outs[i].at[me], send_sems, recv_sems, 7 * i, sib))
            for j, (cx, cy) in enumerate(chips):
                first.append(_remote(ins[i], outs[i].at[me], send_sems, recv_sems, 7 * i + 1 + j, (cx, cy, c)))
        for cp in first:
            cp.start()
        for j, (cx, cy) in enumerate(chips):
            for i in range(n):
                got = outs[i].at[4 * cx + 2 * cy + c]
                _remote(got, got, send_sems, recv_sems, 7 * i + 1 + j, (cx, cy, c)).wait_recv()
                fw = _remote(got, got, send_sems, recv_sems, 7 * i + 4 + j, sib)
                fw.start()
                passed.append(fw)
        for i in range(n):
            got = outs[i].at[4 * x + 2 * y + (1 - c)]
            _remote(got, got, send_sems, recv_sems, 7 * i, sib).wait_recv()
            for j, (cx, cy) in enumerate(chips):
                got = outs[i].at[4 * cx + 2 * cy + (1 - c)]
                _remote(got, got, send_sems, recv_sems, 7 * i + 4 + j, sib).wait_recv()
        for cp in first + passed:
            cp.wait_send()
        for cp in local:
            cp.wait()

    return pl.pallas_call(
        body, name="allgather_devices", in_specs=[ANY] * n, out_specs=[ANY] * n,
        out_shape=[jax.ShapeDtypeStruct((N_DEV,) + v.shape, v.dtype) for v in vs],
        scratch_shapes=[pltpu.SemaphoreType.DMA((7 * n,)), pltpu.SemaphoreType.DMA((7 * n,)),
                        pltpu.SemaphoreType.DMA((n,))],
    )(*vs)


def sibling_send_other_half(gs):
    n = len(gs)

    def body(*refs):
        ins, outs = refs[:n], refs[n:2 * n]
        send_sems, recv_sems = refs[2 * n:]
        x, y, c = _me()
        cps = [_remote(_rows_half(ins[i], 1 - c), outs[i], send_sems, recv_sems, i, (x, y, 1 - c)) for i in range(n)]
        for cp in cps:
            cp.start()
        for cp in cps:
            cp.wait()

    return pl.pallas_call(
        body, name="sibling_send_other_half", in_specs=[ANY] * n, out_specs=[ANY] * n,
        out_shape=[jax.ShapeDtypeStruct((g.shape[0], g.shape[1] // 2, g.shape[2]), g.dtype) for g in gs],
        scratch_shapes=[pltpu.SemaphoreType.DMA((n,)), pltpu.SemaphoreType.DMA((n,))],
    )(*gs)


def chips_exchange(parts):
    n = len(parts)

    def body(*refs):
        ins, outs = refs[:n], refs[n:2 * n]
        send_sems, recv_sems = refs[2 * n:]
        x, y, c = _me()
        cps = []
        for i in range(n):
            nl = ins[i].shape[0] // N_CHIPS
            for j, (cx, cy) in enumerate(_other_chips(x, y)):
                cps.append(_remote(ins[i].at[pl.ds((2 * cx + cy) * nl, nl)], outs[i].at[j], send_sems, recv_sems,
                                   3 * i + j, (cx, cy, c)))
        for cp in cps:
            cp.start()
        for cp in cps:
            cp.wait()

    return pl.pallas_call(
        body, name="chips_exchange", in_specs=[ANY] * n, out_specs=[ANY] * n,
        out_shape=[jax.ShapeDtypeStruct((3, a.shape[0] // N_CHIPS) + a.shape[1:], a.dtype) for a in parts],
        scratch_shapes=[pltpu.SemaphoreType.DMA((3 * n,)), pltpu.SemaphoreType.DMA((3 * n,))],
    )(*parts)


def sibling_share(halves):
    n = len(halves)

    def body(*refs):
        ins, outs = refs[:n], refs[n:2 * n]
        send_sems, recv_sems, local_sems = refs[2 * n:]
        x, y, c = _me()
        local = [pltpu.make_async_copy(ins[i], _rows_half(outs[i], c), local_sems.at[i]) for i in range(n)]
        for cp in local:
            cp.start()
        cps = [_remote(ins[i], _rows_half(outs[i], c), send_sems, recv_sems, i, (x, y, 1 - c)) for i in range(n)]
        for cp in cps:
            cp.start()
        for i in range(n):
            got = _rows_half(outs[i], 1 - c)
            _remote(got, got, send_sems, recv_sems, i, (x, y, 1 - c)).wait_recv()
        for cp in cps:
            cp.wait_send()
        for cp in local:
            cp.wait()

    return pl.pallas_call(
        body, name="sibling_share", in_specs=[ANY] * n, out_specs=[ANY] * n,
        out_shape=[jax.ShapeDtypeStruct((h.shape[0], 2 * h.shape[1], h.shape[2]), h.dtype) for h in halves],
        scratch_shapes=[pltpu.SemaphoreType.DMA((n,)), pltpu.SemaphoreType.DMA((n,)), pltpu.SemaphoreType.DMA((n,))],
    )(*halves)


def sum_with_sibling(g, recv, core, name):
    S, H, C = recv.shape
    tr = min(512, H)

    def body(c_ref, g_ref, r_ref, o_ref):
        o_ref[...] = (g_ref[...].astype(F32) + r_ref[...].astype(F32)).astype(o_ref.dtype)

    nb = H // tr
    return pl.pallas_call(
        body, name=name,
        grid_spec=pltpu.PrefetchScalarGridSpec(
            num_scalar_prefetch=1, grid=(S, nb),
            in_specs=[pl.BlockSpec((None, tr, C), lambda s, i, c_ref: (s, c_ref[0] * nb + i, 0)),
                      pl.BlockSpec((None, tr, C), lambda s, i, c_ref: (s, i, 0))],
            out_specs=pl.BlockSpec((None, tr, C), lambda s, i, c_ref: (s, i, 0))),
        out_shape=jax.ShapeDtypeStruct((S, H, C), g.dtype), compiler_params=_cp("parallel", "parallel"),
    )(core, g, recv)


def sum_chips(a, recv, shard, name):
    _, nl, H, C = recv.shape
    tr = min(512, H)

    def body(s_ref, a_ref, r_ref, o_ref):
        acc = a_ref[...].astype(F32)
        for j in range(3):
            acc = acc + r_ref[j].astype(F32)
        o_ref[...] = acc

    return pl.pallas_call(
        body, name=name,
        grid_spec=pltpu.PrefetchScalarGridSpec(
            num_scalar_prefetch=1, grid=(nl, H // tr),
            in_specs=[pl.BlockSpec((None, tr, C), lambda l, i, s_ref: (s_ref[0] * nl + l, i, 0)),
                      pl.BlockSpec((3, None, tr, C), lambda l, i, s_ref: (0, l, i, 0))],
            out_specs=pl.BlockSpec((None, tr, C), lambda l, i, s_ref: (l, i, 0))),
        out_shape=jax.ShapeDtypeStruct((nl, H, C), F32), compiler_params=_cp("parallel", "parallel"),
    )(shard, a, recv)


BIG = ("ev_w_in", "ev_s5_glu_w", "ev_w_out", "od_w_in", "od_w_out", "xa_w_qg", "xa_w_kv", "xa_w_o")
SHARDED_F32 = (("ev_conv_w", 2), ("od_norm_g", 1))
SMALL = ("mem_norm_g", "ev_norm_g", "ev_s5_lambda_re", "ev_s5_lambda_im", "ev_s5_log_dt", "ev_s5_b_re", "ev_s5_b_im",
         "ev_s5_c_re", "ev_s5_c_im", "ev_s5_d", "ev_s5_glu_b", "ev_conv_b", "ev_conv_ln_g", "ev_conv_ln_b",
         "od_rel_bias", "xa_norm_g", "final_norm_g")
WEIGHTS = ("mem_norm_g", "ev_norm_g", "ev_w_in", "ev_s5_lambda_re", "ev_s5_lambda_im", "ev_s5_log_dt", "ev_s5_b_re",
           "ev_s5_b_im", "ev_s5_c_re", "ev_s5_c_im", "ev_s5_d", "ev_s5_glu_w", "ev_s5_glu_b", "ev_conv_w", "ev_conv_b",
           "ev_conv_ln_g", "ev_conv_ln_b", "ev_w_out", "od_norm_g", "od_w_in", "od_rel_bias", "od_w_out", "xa_norm_g",
           "xa_w_qg", "xa_w_kv", "xa_w_o", "final_norm_g")


def _as2d(a):
    return a.reshape(1, -1) if a.ndim == 1 else a.reshape(-1, a.shape[-1])


def kernel(x, mem, mem_norm_g, ev_norm_g, ev_w_in, ev_s5_lambda_re, ev_s5_lambda_im, ev_s5_log_dt, ev_s5_b_re, ev_s5_b_im, ev_s5_c_re, ev_s5_c_im, ev_s5_d, ev_s5_glu_w, ev_s5_glu_b, ev_conv_w, ev_conv_b, ev_conv_ln_g, ev_conv_ln_b, ev_w_out, od_norm_g, od_w_in, od_rel_bias, od_w_out, xa_norm_g, xa_w_qg, xa_w_kv, xa_w_o, final_norm_g, loss_target, m_mem_norm_g, m_ev_norm_g, m_ev_w_in, m_ev_s5_lambda_re, m_ev_s5_lambda_im, m_ev_s5_log_dt, m_ev_s5_b_re, m_ev_s5_b_im, m_ev_s5_c_re, m_ev_s5_c_im, m_ev_s5_d, m_ev_s5_glu_w, m_ev_s5_glu_b, m_ev_conv_w, m_ev_conv_b, m_ev_conv_ln_g, m_ev_conv_ln_b, m_ev_w_out, m_od_norm_g, m_od_w_in, m_od_rel_bias, m_od_w_out, m_xa_norm_g, m_xa_w_qg, m_xa_w_kv, m_xa_w_o, m_final_norm_g, v_mem_norm_g, v_ev_norm_g, v_ev_w_in, v_ev_s5_lambda_re, v_ev_s5_lambda_im, v_ev_s5_log_dt, v_ev_s5_b_re, v_ev_s5_b_im, v_ev_s5_c_re, v_ev_s5_c_im, v_ev_s5_d, v_ev_s5_glu_w, v_ev_s5_glu_b, v_ev_conv_w, v_ev_conv_b, v_ev_conv_ln_g, v_ev_conv_ln_b, v_ev_w_out, v_od_norm_g, v_od_w_in, v_od_rel_bias, v_od_w_out, v_xa_norm_g, v_xa_w_qg, v_xa_w_kv, v_xa_w_o, v_final_norm_g):
    a = dict(locals())
    w = {n: a[n] for n in WEIGHTS}
    shard = (2 * lax.axis_index("x") + lax.axis_index("y")).reshape(1).astype(jnp.int32)
    core = lax.axis_index("c").reshape(1).astype(jnp.int32)

    halved = [w[n].astype(BF16).reshape((-1,) + w[n].shape[-2:]) for n in BIG]
    halved = [h.reshape(-1, h.shape[-1]) for h in halved]
    whole = [_as2d(w[n]) for n, _ in SHARDED_F32]
    gathered = allgather_chip_blocks(halved, whole)
    gw = {}
    for n, gth in zip(BIG, gathered):
        layers, R, C = (w[n].shape[0],) + w[n].shape[-2:]
        gw[n] = gth.reshape((N_CHIPS, layers, R, C) if layers > 1 else (N_CHIPS, R, C))
    p = {n: w[n] for n in SMALL}
    conv_g, odn_g = gathered[len(BIG):]
    p["ev_conv_w"] = jnp.concatenate([conv_g[s] for s in range(N_CHIPS)], axis=1)[None]
    p["od_norm_g"] = odn_g.reshape(1, D_MODEL)

    loss, grad_x, g, big = local_step(x[0], mem[0], loss_target[0], p, gw)
    loss = lax.psum(loss[0, 0], ("x", "y", "c"))

    gs = [big[n].reshape((-1,) + big[n].shape[-2:]) for n in BIG]
    from_sibling = sibling_send_other_half(gs)
    chip_sums = [sum_with_sibling(gi, ri, core, "sum_sibling_" + n) for n, gi, ri in zip(BIG, gs, from_sibling)]
    from_chips = chips_exchange(chip_sums)
    halves = [sum_chips(ci, ri, shard, "sum_chips_" + n) for n, ci, ri in zip(BIG, chip_sums, from_chips)]
    g_big = dict(zip(BIG, sibling_share(halves)))

    out = {tag: {} for tag in ("grad", "delta", "m", "v")}
    for n in BIG:
        sh = w[n].shape
        to2d = lambda t: t.reshape(-1, sh[-1])
        gn = to2d(g_big[n])
        d, mn, vn = adamw(to2d(w[n]), gn, to2d(a["m_" + n]), to2d(a["v_" + n]), "adamw_" + n)
        for tag, val in zip(("grad", "delta", "m", "v"), (gn, d, mn, vn)):
            out[tag][n] = val.reshape(sh)

    small = SMALL + tuple(n for n, _ in SHARDED_F32)
    gath = allgather_devices([_as2d(g[n]) for n in small])
    for n, gt in zip(small, gath):
        sh = w[n].shape
        to2d = lambda t: _as2d(t)
        gn, d, mn, vn = adamw_allreduce(gt, to2d(w[n]), to2d(a["m_" + n]), to2d(a["v_" + n]), shard, "adamw_" + n)
        for tag, val in zip(("grad", "delta", "m", "v"), (gn, d, mn, vn)):
            out[tag][n] = val.reshape(sh)

    res = [loss, grad_x[None]]
    for tag in ("grad", "delta", "m", "v"):
        res += [out[tag][n] for n in WEIGHTS]
    return tuple(res)
```

```python
import math

import jax
import jax.numpy as jnp
import numpy as np
from jax import lax
from jax.experimental import pallas as pl
from jax.experimental.pallas import tpu as pltpu

F32 = jnp.float32
BF16 = jnp.bfloat16

D_MODEL = 1024
CHUNK = 64
LEFT_CHUNKS = 8
S5_WIDTH = 512
S5_GROUP = 16
S5_GROUPS = 32
S5_STATE = 64
S5_COLS = S5_GROUPS * S5_STATE
S5_SPLIT = 4
S5_CC = S5_COLS // S5_SPLIT
S5_UC = S5_WIDTH // S5_SPLIT
CONV_WIDTH = 512
CONV_KERNEL = 31
CONV_HALO = 32
ATT_HEADS = 16
ATT_HEAD_DIM = 64
MAX_REL = 128
MEM_LEN = 256
XA_HEADS = 4
XA_HEAD_DIM = 256
EPS = 1e-6
EVEN_IN = 2560
ODD_IN = 4096

ADAM_LR = 0.001
ADAM_B1 = 0.9
ADAM_B2 = 0.999
ADAM_EPS = 1e-08
ADAM_WD = 0.01
ADAM_STEP = 10

ROW_TILE = 256
ATT_QB = 256
ATT_PAD = LEFT_CHUNKS * CHUNK
ATT_WIN = ATT_PAD + ATT_QB
VMEM_LIMIT_V7X = 56 * 1024 * 1024
NEG = -1e30
LANES = 128
N_CHIPS = 4
N_DEV = 8

MESH = pl.DeviceIdType.MESH
ANY = pl.BlockSpec(memory_space=pl.ANY)


def _cp(*sem, vmem=VMEM_LIMIT_V7X):
    return pltpu.CompilerParams(dimension_semantics=sem if sem else None, vmem_limit_bytes=vmem)


def _full(shape):
    n = len(shape)
    return pl.BlockSpec(shape, lambda *_: (0,) * n)


def _wspec(w, layer=None):
    if layer is None:
        return _full(w.shape)
    s, _, r, c = w.shape
    return pl.BlockSpec((s, None, r, c), lambda *_: (0, layer, 0, 0))


def _lane_tile(n, cap):
    return max(t for t in range(LANES, min(n, cap) + 1, LANES) if n % t == 0)


def _sigmoid(x):
    return 1.0 / (1.0 + jnp.exp(-x))


def _silu(x):
    return x * _sigmoid(x)


def _dsilu(x):
    s = _sigmoid(x)
    return s * (1.0 + x * (1.0 - s))


_GELU_C = math.sqrt(2.0 / math.pi)


def _gelu(x):
    return 0.5 * x * (1.0 + jnp.tanh(_GELU_C * (x + 0.044715 * x * x * x)))


def _dgelu(x):
    t = jnp.tanh(_GELU_C * (x + 0.044715 * x * x * x))
    return 0.5 * (1.0 + t) + 0.5 * x * (1.0 - t * t) * _GELU_C * (1.0 + 3.0 * 0.044715 * x * x)


def _dot(a, b):
    return jnp.dot(a, b, preferred_element_type=F32)


def _dot_nt(a, b):
    return lax.dot_general(a, b, (((1,), (1,)), ((), ())), preferred_element_type=F32)


def _dot_tn(a, b):
    return lax.dot_general(a, b, (((0,), (0,)), ((), ())), preferred_element_type=F32)


def _dot_cols(a, w4, shards=range(N_CHIPS)):
    return jnp.concatenate([_dot(a, w4[s]) for s in shards], axis=1)


def _dot_rows(a, w4):
    r = w4.shape[1]
    acc = _dot(a[:, 0:r], w4[0])
    for s in range(1, N_CHIPS):
        acc = acc + _dot(a[:, s * r:(s + 1) * r], w4[s])
    return acc


def _dot_nt_cols(dys, w4):
    acc = _dot_nt(dys[0], w4[0])
    for s in range(1, N_CHIPS):
        acc = acc + _dot_nt(dys[s], w4[s])
    return acc


def _dot_nt_rows(dy, w4):
    return jnp.concatenate([_dot_nt(dy, w4[s]) for s in range(N_CHIPS)], axis=1)


def _col_pieces(v, n):
    return [v[:, s * n:(s + 1) * n] for s in range(N_CHIPS)]


def _rms_parts(xv):
    inv = lax.rsqrt(jnp.mean(xv * xv, axis=-1, keepdims=True) + EPS)
    return inv, xv * inv


def _rms_bwd(xv, g, dh):
    inv, xhat = _rms_parts(xv)
    dg = jnp.sum(dh * xhat, axis=0, keepdims=True)
    dxh = dh * g
    dx = inv * (dxh - xhat * jnp.mean(dxh * xhat, axis=-1, keepdims=True))
    return dx, dg


def norm_mm(x, g, w4, groups, name, tm=ROW_TILE):
    M, D = x.shape
    n = w4.shape[2]
    tm = min(tm, M)

    def body(x_ref, g_ref, w_ref, *outs):
        _, xhat = _rms_parts(x_ref[...])
        hb = (xhat * g_ref[...]).astype(BF16)
        for o, (shards, dt, _) in zip(outs, groups):
            o[...] = _dot_cols(hb, w_ref, shards).astype(dt)
        outs[-1][...] = hb

    out_shape = [jax.ShapeDtypeStruct((M + pad, len(sh) * n), dt) for (sh, dt, pad) in groups]
    out_specs = [pl.BlockSpec((tm, len(sh) * n), lambda i, p=pad // tm: (i + p, 0)) for (sh, _, pad) in groups]
    out_shape.append(jax.ShapeDtypeStruct((M, D), BF16))
    out_specs.append(pl.BlockSpec((tm, D), lambda i: (i, 0)))
    return pl.pallas_call(
        body, name=name, grid=(M // tm,),
        in_specs=[pl.BlockSpec((tm, D), lambda i: (i, 0)), _full(g.shape), _full(w4.shape)],
        out_specs=out_specs, out_shape=out_shape, compiler_params=_cp("parallel"),
    )(x, g, w4)


def zero_rows(buf, rows, name, tm=ROW_TILE):
    C = buf.shape[1]

    def body(b_ref, o_ref):
        o_ref[...] = jnp.zeros_like(o_ref)

    return pl.pallas_call(
        body, name=name, grid=(rows // tm,), in_specs=[ANY],
        out_specs=pl.BlockSpec((tm, C), lambda i: (i, 0)),
        out_shape=jax.ShapeDtypeStruct(buf.shape, buf.dtype), input_output_aliases={0: 0},
        compiler_params=_cp("parallel"),
    )(buf)


def mm_res(a, w4, res, name, tm=ROW_TILE):
    M, K = a.shape
    N = w4.shape[2]
    tm = min(tm, M)

    def body(a_ref, w_ref, r_ref, o_ref):
        o_ref[...] = r_ref[...] + _dot_rows(a_ref[...], w_ref)

    return pl.pallas_call(
        body, name=name, grid=(M // tm,),
        in_specs=[pl.BlockSpec((tm, K), lambda i: (i, 0)), _full(w4.shape), pl.BlockSpec((tm, N), lambda i: (i, 0))],
        out_specs=pl.BlockSpec((tm, N), lambda i: (i, 0)),
        out_shape=jax.ShapeDtypeStruct((M, N), F32), compiler_params=_cp("parallel"),
    )(a, w4, res)


def mm_cols(a, w, layer, name, out_dtype):
    M = a.shape[0]
    n = w.shape[3]

    def body(a_ref, w_ref, o_ref):
        o_ref[...] = _dot_cols(a_ref[...], w_ref).astype(out_dtype)

    return pl.pallas_call(
        body, name=name, grid=(1,), in_specs=[_full(a.shape), _wspec(w, layer)],
        out_specs=_full((M, N_CHIPS * n)), out_shape=jax.ShapeDtypeStruct((M, N_CHIPS * n), out_dtype),
        compiler_params=_cp("arbitrary"),
    )(a, w)


def mm_nt_cols(dy, w, layer, name):
    M = dy.shape[0]
    K, n = w.shape[2], w.shape[3]

    def body(d_ref, w_ref, o_ref):
        o_ref[...] = _dot_nt_cols(_col_pieces(d_ref[...].astype(BF16), n), w_ref)

    return pl.pallas_call(
        body, name=name, grid=(1,), in_specs=[_full(dy.shape), _wspec(w, layer)],
        out_specs=_full((M, K)), out_shape=jax.ShapeDtypeStruct((M, K), F32), compiler_params=_cp("arbitrary"),
    )(dy, w)


def mm_nt_rows(dy, w4, name, tm=ROW_TILE):
    M, N = dy.shape
    K = N_CHIPS * w4.shape[1]
    tm = min(tm, M)

    def body(d_ref, w_ref, o_ref):
        o_ref[...] = _dot_nt_rows(d_ref[...].astype(BF16), w_ref)

    return pl.pallas_call(
        body, name=name, grid=(M // tm,),
        in_specs=[pl.BlockSpec((tm, N), lambda i: (i, 0)), _full(w4.shape)],
        out_specs=pl.BlockSpec((tm, K), lambda i: (i, 0)),
        out_shape=jax.ShapeDtypeStruct((M, K), F32), compiler_params=_cp("parallel"),
    )(dy, w4)


def mm_nt_normbwd(dys, offs, w4, x, g, dx_out, name, tm=ROW_TILE):
    M, D = x.shape
    n = w4.shape[2]
    tm = min(tm, M)
    nd = len(dys)

    def body(*refs):
        d_refs = refs[:nd]
        w_ref, x_ref, g_ref, dxo_ref, dx_ref, dg_ref = refs[nd:]
        if nd == 1:
            pieces = _col_pieces(d_refs[0][...].astype(BF16), n)
        else:
            pieces = [r[...].astype(BF16) for r in d_refs]
        dh = _dot_nt_cols(pieces, w_ref)
        dx, dg = _rms_bwd(x_ref[...], g_ref[...], dh)
        dx_ref[...] = dxo_ref[...] + dx

        @pl.when(pl.program_id(0) == 0)
        def _():
            dg_ref[...] = jnp.zeros_like(dg_ref)

        dg_ref[...] += dg

    row = lambda c, off=0: pl.BlockSpec((tm, c), lambda i, p=off // tm: (i + p, 0))
    return pl.pallas_call(
        body, name=name, grid=(M // tm,),
        in_specs=[row(d.shape[1], off) for d, off in zip(dys, offs)] + [_full(w4.shape), row(D), _full(g.shape), row(D)],
        out_specs=[row(D), _full((1, D))],
        out_shape=[jax.ShapeDtypeStruct((M, D), F32), jax.ShapeDtypeStruct((1, D), F32)],
        compiler_params=_cp("arbitrary"),
    )(*dys, w4, x, g, dx_out)


def mm_tn(a, b, name, layout, into=None, b_off=0, out_dtype=BF16, bm=1024, bn=1280, bl=1024):
    L, K = a.shape
    N = b.shape[1]
    kind = layout[0]
    arg = layout[1] if len(layout) > 1 else None
    bm, bn, bl = _lane_tile(K, bm), _lane_tile(N, bn), min(bl, L)
    assert L % bl == 0 and b_off % bl == 0, (L, bl, b_off)
    nl = L // bl
    n_sh, r_sh = N // N_CHIPS, K // N_CHIPS
    lay = (None,) if arg is None else (None, None)
    mid = () if arg is None else (arg,)
    gs = 1
    if kind == "plain":
        oshape, oblock, oidx = (K, N), (bm, bn), lambda i, j, l: (i, j)
    elif kind == "slab":
        oshape, oblock, oidx = (N_CHIPS, K, N), (None, bm, bn), lambda i, j, l: (arg, i, j)
    elif kind == "cols":
        bn = max(bn - bn % n_sh, n_sh) if bn >= n_sh else _lane_tile(n_sh, bn)
        gs = max(bn // n_sh, 1)
        per = n_sh // bn if gs == 1 else 1
        oshape = (N_CHIPS,) + ((2,) if arg is not None else ()) + (K, n_sh)
        oblock = ((gs,) if gs > 1 else (None,)) + lay[1:] + (bm, min(bn, n_sh))
        oidx = lambda i, j, l: (j // per,) + mid + (i, j % per)
    else:
        bm = max(bm - bm % r_sh, r_sh) if bm >= r_sh else _lane_tile(r_sh, bm)
        gs = max(bm // r_sh, 1)
        per = r_sh // bm if gs == 1 else 1
        oshape = (N_CHIPS,) + ((2,) if arg is not None else ()) + (r_sh, N)
        oblock = ((gs,) if gs > 1 else (None,)) + lay[1:] + (min(bm, r_sh), bn)
        oidx = lambda i, j, l: (i // per,) + mid + (i % per, j)
    assert K % bm == 0 and N % bn == 0, (K, bm, N, bn)

    def body(a_ref, b_ref, *rest):
        o_ref, acc = rest[-2], rest[-1]
        l = pl.program_id(2)

        @pl.when(l == 0)
        def _():
            acc[...] = jnp.zeros_like(acc)

        acc[...] += _dot_tn(a_ref[...].astype(BF16), b_ref[...].astype(BF16))

        @pl.when(l == nl - 1)
        def _():
            if gs == 1:
                o_ref[...] = acc[...].astype(out_dtype)
            elif kind == "cols":
                for t in range(gs):
                    o_ref[t] = acc[:, t * n_sh:(t + 1) * n_sh].astype(out_dtype)
            else:
                for t in range(gs):
                    o_ref[t] = acc[t * r_sh:(t + 1) * r_sh, :].astype(out_dtype)

    in_specs = [pl.BlockSpec((bl, bm), lambda i, j, l: (l, i)),
                pl.BlockSpec((bl, bn), lambda i, j, l, p=b_off // bl: (l + p, j))]
    args = [a, b]
    alias = {}
    if into is not None:
        in_specs.append(ANY)
        args.append(into)
        alias = {2: 0}
    return pl.pallas_call(
        body, name=name, grid=(K // bm, N // bn, nl), in_specs=in_specs,
        out_specs=pl.BlockSpec(oblock, oidx), out_shape=jax.ShapeDtypeStruct(oshape, out_dtype),
        scratch_shapes=[pltpu.VMEM((bm, bn), F32)], input_output_aliases=alias,
        compiler_params=_cp("parallel", "parallel", "arbitrary"),
    )(*args)


def rms_fwd(x, g, name):
    def body(x_ref, g_ref, ob_ref):
        _, xhat = _rms_parts(x_ref[...])
        ob_ref[...] = (xhat * g_ref[...]).astype(BF16)

    return pl.pallas_call(body, name=name, out_shape=jax.ShapeDtypeStruct(x.shape, BF16))(x, g)


def rms_dgain(x, dy0, dy1, name):
    def body(x_ref, d0_ref, d1_ref, o_ref):
        _, xhat = _rms_parts(x_ref[...])
        o_ref[...] = jnp.sum((d0_ref[...] + d1_ref[...]) * xhat, axis=0, keepdims=True)

    return pl.pallas_call(body, name=name, out_shape=jax.ShapeDtypeStruct((1, x.shape[1]), F32))(x, dy0, dy1)


def _s5_discretise(lr, li, logdt, bt_re, bt_im):
    dt = jnp.exp(logdt)
    mag = jnp.exp(lr * dt)
    ab_re = mag * jnp.cos(li * dt)
    ab_im = mag * jnp.sin(li * dt)
    den = lr * lr + li * li
    nr = ab_re - 1.0
    coef_re = (nr * lr + ab_im * li) / den
    coef_im = (ab_im * lr - nr * li) / den
    cr = coef_re[:, None, :]
    ci = coef_im[:, None, :]
    bb_re = cr * bt_re - ci * bt_im
    bb_im = cr * bt_im + ci * bt_re
    return ab_re, ab_im, bb_re, bb_im


def s5_param_fwd(lr, li, logdt, bt_re, bt_im):
    def body(lr_ref, li_ref, ld_ref, br_ref, bi_ref, bbr_ref, bbi_ref):
        _, _, bb_re, bb_im = _s5_discretise(lr_ref[...], li_ref[...], ld_ref[...], br_ref[...], bi_ref[...])
        bbr_ref[...] = bb_re
        bbi_ref[...] = bb_im

    sh = jax.ShapeDtypeStruct(bt_re.shape, F32)
    return pl.pallas_call(body, name="s5_param_fwd", out_shape=[sh, sh])(lr, li, logdt, bt_re, bt_im)


def s5_param_bwd(lr, li, logdt, bt_re, bt_im, d_ab_re, d_ab_im, d_bb_re, d_bb_im):
    def body(lr_ref, li_ref, ld_ref, br_ref, bi_ref, dar_ref, dai_ref, dbr_ref, dbi_ref,
             o_lr, o_li, o_ld, o_br, o_bi):
        _, vjp = jax.vjp(_s5_discretise, lr_ref[...], li_ref[...], ld_ref[...], br_ref[...], bi_ref[...])
        g = vjp((dar_ref[...], dai_ref[...], dbr_ref[...], dbi_ref[...]))
        for o, v in zip((o_lr, o_li, o_ld, o_br, o_bi), g):
            o[...] = v

    shapes = [jax.ShapeDtypeStruct(a.shape, F32) for a in (lr, li, logdt, bt_re, bt_im)]
    return pl.pallas_call(body, name="s5_param_bwd", out_shape=shapes)(
        lr, li, logdt, bt_re, bt_im, d_ab_re, d_ab_im, d_bb_re, d_bb_im)


def s5_tables(lr_flat, li_flat, logdt_flat):
    def body(lr_ref, li_ref, ld_ref, tab_ref):
        dt = jnp.exp(ld_ref[...])
        a = lr_ref[...] * dt
        th = li_ref[...] * dt
        row = lax.broadcasted_iota(jnp.int32, (8, 1), 0)
        rowf = row.astype(F32)

        def power(e, sign):
            m = jnp.exp(e * a)
            return m * jnp.cos(e * th), sign * m * jnp.sin(e * th)

        k = 0
        for sign, fwd in ((1.0, True), (-1.0, False)):
            for s in (1, 2, 4):
                pr, pi = power(jnp.full((8, 1), float(s), F32), sign)
                keep = (row >= s) if fwd else (row + s < 8)
                tab_ref[k] = jnp.where(keep, pr, 0.0)
                tab_ref[k + 1] = jnp.where(keep, pi, 0.0)
                k += 2
            e = rowf + 1.0 if fwd else 8.0 - rowf
            pr, pi = power(e, sign)
            tab_ref[k] = pr
            tab_ref[k + 1] = pi
            k += 2

    return pl.pallas_call(body, name="s5_tables",
                          out_shape=jax.ShapeDtypeStruct((16, 8, S5_COLS), F32))(lr_flat, li_flat, logdt_flat)


def _scan_block(a, b, tabs, base, cr, ci, reverse):
    for n, s in enumerate((1, 2, 4)):
        mr = tabs[base + 2 * n]
        mi = tabs[base + 2 * n + 1]
        sh = (8 - s) if reverse else s
        ar = pltpu.roll(a, sh, 0)
        br = pltpu.roll(b, sh, 0)
        a, b = a + mr * ar - mi * br, b + mr * br + mi * ar
    pr = tabs[base + 6]
    pi = tabs[base + 7]
    a, b = a + pr * cr - pi * ci, b + pr * ci + pi * cr
    return a, b


def s5_fwd(z, bbd_re, bbd_im, ccd_re, ccd_im, tab, dskip, tm=ROW_TILE):
    L = z.shape[0]
    tm = min(tm, L)
    nt = L // tm

    def body(u_ref, bbr_ref, bbi_ref, ccr_ref, cci_ref, tab_ref, d_ref, y_ref, ck_ref, xr, xi, car):
        t = pl.program_id(1)

        @pl.when(t == 0)
        def _():
            car[...] = jnp.zeros_like(car)

        u = u_ref[...]
        ub = u.astype(BF16)
        xr[...] = _dot(ub, bbr_ref[...])
        xi[...] = _dot(ub, bbi_ref[...])
        tabs = [tab_ref[k] for k in range(8)]

        def blk(i, c):
            r0 = pl.multiple_of(i * 8, 8)
            a, b = _scan_block(xr[pl.ds(r0, 8), :], xi[pl.ds(r0, 8), :], tabs, 0, c[0], c[1], False)
            xr[pl.ds(r0, 8), :] = a
            xi[pl.ds(r0, 8), :] = b
            return a[7:8, :], b[7:8, :]

        cr, ci = lax.fori_loop(0, tm // 8, blk, (car[0:1, :], car[1:2, :]))
        car[0:1, :] = cr
        car[1:2, :] = ci
        ck_ref[0:1, :] = cr
        ck_ref[1:2, :] = ci
        y_ref[...] = (_dot(xr[...].astype(BF16), ccr_ref[...]) - _dot(xi[...].astype(BF16), cci_ref[...])
                      + d_ref[...] * u)

    return pl.pallas_call(
        body, name="s5_fwd", grid=(S5_SPLIT, nt),
        in_specs=[pl.BlockSpec((tm, S5_UC), lambda j, t: (t, j)),
                  pl.BlockSpec((None, S5_UC, S5_CC), lambda j, t: (j, 0, 0)),
                  pl.BlockSpec((None, S5_UC, S5_CC), lambda j, t: (j, 0, 0)),
                  pl.BlockSpec((None, S5_CC, S5_UC), lambda j, t: (j, 0, 0)),
                  pl.BlockSpec((None, S5_CC, S5_UC), lambda j, t: (j, 0, 0)),
                  pl.BlockSpec((8, 8, S5_CC), lambda j, t: (0, 0, j)),
                  pl.BlockSpec((1, S5_UC), lambda j, t: (0, j))],
        out_specs=[pl.BlockSpec((tm, S5_UC), lambda j, t: (t, j)),
                   pl.BlockSpec((None, 2, S5_CC), lambda j, t: (t, 0, j))],
        out_shape=[jax.ShapeDtypeStruct((L, S5_WIDTH), F32), jax.ShapeDtypeStruct((nt, 2, S5_COLS), F32)],
        scratch_shapes=[pltpu.VMEM((tm, S5_CC), F32), pltpu.VMEM((tm, S5_CC), F32), pltpu.VMEM((2, S5_CC), F32)],
        compiler_params=_cp("parallel", "arbitrary"),
    )(z, bbd_re, bbd_im, ccd_re, ccd_im, tab, dskip)


def s5_bwd(z, dy, dz, ckpt, bbd_re, bbd_im, ccd_re, ccd_im, tab, dskip, tm=ROW_TILE):
    L = z.shape[0]
    tm = min(tm, L)
    nt = L // tm

    def body(u_ref, dy_ref, dz_ref, ck_ref, bbr_ref, bbi_ref, ccr_ref, cci_ref, tab_ref, d_ref,
             du_ref, da_ref, dbr_ref, dbi_ref, dcr_ref, dci_ref, dd_ref, hr, hi, gr, gi, car, acr, aci):
        t = pl.program_id(1)
        tt = nt - 1 - t

        @pl.when(t == 0)
        def _():
            for r in (car, acr, aci, dbr_ref, dbi_ref, dcr_ref, dci_ref, dd_ref):
                r[...] = jnp.zeros_like(r)

        u = u_ref[...]
        ub = u.astype(BF16)
        dyv = dy_ref[...]
        dyb = dyv.astype(BF16)
        tabs = [tab_ref[k] for k in range(16)]

        live = (tt > 0).astype(F32)
        c0r = ck_ref[0:1, :] * live
        c0i = ck_ref[1:2, :] * live
        hr[0:8, :] = jnp.broadcast_to(c0r, (8, S5_CC))
        hi[0:8, :] = jnp.broadcast_to(c0i, (8, S5_CC))
        hr[8:, :] = _dot(ub, bbr_ref[...])
        hi[8:, :] = _dot(ub, bbi_ref[...])

        def fblk(i, c):
            r0 = pl.multiple_of(i * 8 + 8, 8)
            a, b = _scan_block(hr[pl.ds(r0, 8), :], hi[pl.ds(r0, 8), :], tabs, 0, c[0], c[1], False)
            hr[pl.ds(r0, 8), :] = a
            hi[pl.ds(r0, 8), :] = b
            return a[7:8, :], b[7:8, :]

        lax.fori_loop(0, tm // 8, fblk, (c0r, c0i))
        hrb = hr[8:, :].astype(BF16)
        hib = hi[8:, :].astype(BF16)
        dcr_ref[...] += _dot_tn(hrb, dyb)
        dci_ref[...] -= _dot_tn(hib, dyb)

        gr[...] = _dot_nt(dyb, ccr_ref[...])
        gi[...] = -_dot_nt(dyb, cci_ref[...])
        row0 = lax.broadcasted_iota(jnp.int32, (8, S5_CC), 0) == 0

        def rblk(k, c):
            i = tm // 8 - 1 - k
            r0 = pl.multiple_of(i * 8, 8)
            a, b = _scan_block(gr[pl.ds(r0, 8), :], gi[pl.ds(r0, 8), :], tabs, 8, c[0], c[1], True)
            gr[pl.ds(r0, 8), :] = a
            gi[pl.ds(r0, 8), :] = b
            r1 = pl.multiple_of(i * 8 + 8, 8)
            hpr = jnp.where(row0, pltpu.roll(hr[pl.ds(r0, 8), :], 1, 0), pltpu.roll(hr[pl.ds(r1, 8), :], 1, 0))
            hpi = jnp.where(row0, pltpu.roll(hi[pl.ds(r0, 8), :], 1, 0), pltpu.roll(hi[pl.ds(r1, 8), :], 1, 0))
            acr[...] += a * hpr + b * hpi
            aci[...] += b * hpr - a * hpi
            return a[0:1, :], b[0:1, :]

        cr, ci = lax.fori_loop(0, tm // 8, rblk, (car[0:1, :], car[1:2, :]))
        car[0:1, :] = cr
        car[1:2, :] = ci

        grb = gr[...].astype(BF16)
        gib = gi[...].astype(BF16)
        du_ref[...] = (_dot_nt(grb, bbr_ref[...]) + _dot_nt(gib, bbi_ref[...]) + d_ref[...] * dyv).astype(BF16)
        dbr_ref[...] += _dot_tn(ub, grb)
        dbi_ref[...] += _dot_tn(ub, gib)
        dd_ref[...] += jnp.sum(dyv * u, axis=0, keepdims=True)

        @pl.when(t == nt - 1)
        def _():
            da_ref[0:1, :] = jnp.sum(acr[...], axis=0, keepdims=True)
            da_ref[1:2, :] = jnp.sum(aci[...], axis=0, keepdims=True)

    chunk = lambda a, b: pl.BlockSpec((None, a, b), lambda j, t: (j, 0, 0))
    return pl.pallas_call(
        body, name="s5_bwd", grid=(S5_SPLIT, nt),
        in_specs=[pl.BlockSpec((tm, S5_UC), lambda j, t: (nt - 1 - t, j)),
                  pl.BlockSpec((tm, S5_UC), lambda j, t: (nt - 1 - t, j)),
                  ANY,
                  pl.BlockSpec((None, 2, S5_CC), lambda j, t: (jnp.maximum(nt - 2 - t, 0), 0, j)),
                  chunk(S5_UC, S5_CC), chunk(S5_UC, S5_CC), chunk(S5_CC, S5_UC), chunk(S5_CC, S5_UC),
                  pl.BlockSpec((16, 8, S5_CC), lambda j, t: (0, 0, j)),
                  pl.BlockSpec((1, S5_UC), lambda j, t: (0, j))],
        out_specs=[pl.BlockSpec((tm, S5_UC), lambda j, t: (nt - 1 - t, j)),
                   pl.BlockSpec((None, 2, S5_CC), lambda j, t: (j, 0, 0)),
                   chunk(S5_UC, S5_CC), chunk(S5_UC, S5_CC), chunk(S5_CC, S5_UC), chunk(S5_CC, S5_UC),
                   pl.BlockSpec((1, S5_UC), lambda j, t: (0, j))],
        out_shape=[jax.ShapeDtypeStruct(dz.shape, dz.dtype),
                   jax.ShapeDtypeStruct((S5_SPLIT, 2, S5_CC), F32),
                   jax.ShapeDtypeStruct((S5_SPLIT, S5_UC, S5_CC), F32),
                   jax.ShapeDtypeStruct((S5_SPLIT, S5_UC, S5_CC), F32),
                   jax.ShapeDtypeStruct((S5_SPLIT, S5_CC, S5_UC), F32),
                   jax.ShapeDtypeStruct((S5_SPLIT, S5_CC, S5_UC), F32),
                   jax.ShapeDtypeStruct((1, S5_WIDTH), F32)],
        scratch_shapes=[pltpu.VMEM((tm + 8, S5_CC), F32), pltpu.VMEM((tm + 8, S5_CC), F32),
                        pltpu.VMEM((tm, S5_CC), F32), pltpu.VMEM((tm, S5_CC), F32),
                        pltpu.VMEM((2, S5_CC), F32), pltpu.VMEM((8, S5_CC), F32), pltpu.VMEM((8, S5_CC), F32)],
        input_output_aliases={2: 0},
        compiler_params=_cp("parallel", "arbitrary"),
    )(z, dy, dz, ckpt, bbd_re, bbd_im, ccd_re, ccd_im, tab, dskip)


_EYE8 = np.eye(S5_GROUPS // S5_SPLIT, dtype=np.float32)


def _blockdiag(a):
    g, r, c = a.shape
    a = a.reshape(S5_SPLIT, g // S5_SPLIT, r, c)
    out = a[:, :, :, None, :] * _EYE8[None, :, None, :, None].astype(a.dtype)
    return out.reshape(S5_SPLIT, (g // S5_SPLIT) * r, (g // S5_SPLIT) * c)


def _blockdiag_extract(a, r, c):
    n = S5_GROUPS // S5_SPLIT
    a = a.reshape(S5_SPLIT, n, r, n, c)
    d = jnp.stack([a[:, k, :, k, :] for k in range(n)], axis=1)
    return d.reshape(S5_GROUPS, r, c)


def s5_mixer_core_fwd(z, lam_re, lam_im, log_dt, b_re, b_im, c_re, c_im, d_skip):
    bt_re = jnp.swapaxes(b_re, 1, 2)
    bt_im = jnp.swapaxes(b_im, 1, 2)
    logdt = log_dt.reshape(S5_GROUPS, 1)
    bb_re, bb_im = s5_param_fwd(lam_re, lam_im, logdt, bt_re, bt_im)
    flat = lambda a: a.reshape(1, S5_COLS)
    tab = s5_tables(flat(lam_re), flat(lam_im), flat(jnp.broadcast_to(logdt, (S5_GROUPS, S5_STATE))))
    bbd_re = _blockdiag(bb_re).astype(BF16)
    bbd_im = _blockdiag(bb_im).astype(BF16)
    ccd_re = _blockdiag(jnp.swapaxes(c_re, 1, 2)).astype(BF16)
    ccd_im = _blockdiag(jnp.swapaxes(c_im, 1, 2)).astype(BF16)
    dsk = d_skip.reshape(1, S5_WIDTH)
    y, ckpt = s5_fwd(z, bbd_re, bbd_im, ccd_re, ccd_im, tab, dsk)
    saved = (logdt, bt_re, bt_im, bbd_re, bbd_im, ccd_re, ccd_im, tab, dsk, ckpt)
    return y, saved


def s5_mixer_core_bwd(z, dy, dz, lam_re, lam_im, saved):
    logdt, bt_re, bt_im, bbd_re, bbd_im, ccd_re, ccd_im, tab, dsk, ckpt = saved
    dz, da, dbr, dbi, dcr, dci, dd = s5_bwd(z, dy, dz, ckpt, bbd_re, bbd_im, ccd_re, ccd_im, tab, dsk)
    d_ab_re = da[:, 0, :].reshape(S5_GROUPS, S5_STATE)
    d_ab_im = da[:, 1, :].reshape(S5_GROUPS, S5_STATE)
    d_bb_re = _blockdiag_extract(dbr, S5_GROUP, S5_STATE)
    d_bb_im = _blockdiag_extract(dbi, S5_GROUP, S5_STATE)
    g_lr, g_li, g_ld, g_btr, g_bti = s5_param_bwd(lam_re, lam_im, logdt, bt_re, bt_im,
                                                  d_ab_re, d_ab_im, d_bb_re, d_bb_im)
    g_cre = jnp.swapaxes(_blockdiag_extract(dcr, S5_STATE, S5_GROUP), 1, 2)
    g_cim = jnp.swapaxes(_blockdiag_extract(dci, S5_STATE, S5_GROUP), 1, 2)
    grads = dict(lambda_re=g_lr, lambda_im=g_li, log_dt=g_ld.reshape(S5_GROUPS),
                 b_re=jnp.swapaxes(g_btr, 1, 2), b_im=jnp.swapaxes(g_bti, 1, 2),
                 c_re=g_cre, c_im=g_cim, d=dd.reshape(S5_WIDTH))
    return dz, grads


Z_U, Z_GA, Z_VAL, Z_GLU, Z_GB = range(5)
SUBLANES = 8


def _shifted_copies(buf, tm):
    n = tm + CONV_HALO - SUBLANES
    for r in range(1, SUBLANES):
        buf[r, 0:n, :] = buf[0, pl.ds(r, n), :]


def _shifted_rows(buf, start, tm):
    return buf[start % SUBLANES, pl.ds(start - start % SUBLANES, tm), :]


def conv_fwd(z, conv_w, conv_b, tm=ROW_TILE):
    L = z.shape[0]
    tm = min(tm, L)
    nt = L // tm
    hb = tm // CONV_HALO
    C = CONV_WIDTH

    def body(val_ref, glu_ref, valh_ref, gluh_ref, w_ref, b_ref, c_ref, vsh):
        live = (pl.program_id(0) > 0).astype(F32)
        vsh[0, 0:CONV_HALO, :] = valh_ref[...] * _sigmoid(gluh_ref[...]) * live
        vsh[0, CONV_HALO:, :] = val_ref[...] * _sigmoid(glu_ref[...])
        _shifted_copies(vsh, tm)
        acc = jnp.broadcast_to(b_ref[...], (tm, C))
        for k in range(CONV_KERNEL):
            acc = acc + w_ref[k:k + 1, :] * _shifted_rows(vsh, CONV_HALO - CONV_KERNEL + 1 + k, tm)
        c_ref[...] = acc

    cur = lambda col: pl.BlockSpec((tm, C), lambda t: (t, col))
    prev = lambda col: pl.BlockSpec((CONV_HALO, C), lambda t: (jnp.maximum(t * hb - 1, 0), col))
    return pl.pallas_call(
        body, name="conv_fwd", grid=(nt,),
        in_specs=[cur(Z_VAL), cur(Z_GLU), prev(Z_VAL), prev(Z_GLU), _full(conv_w.shape), _full(conv_b.shape)],
        out_specs=pl.BlockSpec((tm, C), lambda t: (t, 0)),
        out_shape=jax.ShapeDtypeStruct((L, C), F32),
        scratch_shapes=[pltpu.VMEM((8, tm + CONV_HALO, C), F32)],
        compiler_params=_cp("parallel"),
    )(z, z, z, z, conv_w, conv_b)


def conv_bwd(z, dc, dz, conv_w, tm=ROW_TILE):
    L = z.shape[0]
    tm = min(tm, L)
    nt = L // tm
    hb = tm // CONV_HALO
    nh = L // CONV_HALO
    C = CONV_WIDTH
    off = CONV_HALO - CONV_KERNEL + 1

    def body(val_ref, glu_ref, valh_ref, gluh_ref, dc_ref, dcn_ref, dz_ref, w_ref, dvg_ref, dw_ref, db_ref,
             vsh, dsh, wacc):
        t = pl.program_id(0)

        @pl.when(t == 0)
        def _():
            wacc[...] = jnp.zeros_like(wacc)
            db_ref[...] = jnp.zeros_like(db_ref)

        val = val_ref[...]
        sg = _sigmoid(glu_ref[...])
        vsh[0, 0:CONV_HALO, :] = valh_ref[...] * _sigmoid(gluh_ref[...]) * (t > 0).astype(F32)
        vsh[0, CONV_HALO:, :] = val * sg
        dcv = dc_ref[...]
        dsh[0, 0:tm, :] = dcv
        dsh[0, tm:, :] = dcn_ref[...] * (t < nt - 1).astype(F32)
        _shifted_copies(vsh, tm)
        _shifted_copies(dsh, tm)
        dv = jnp.zeros((tm, C), F32)
        for k in range(CONV_KERNEL):
            dv = dv + w_ref[k:k + 1, :] * _shifted_rows(dsh, CONV_KERNEL - 1 - k, tm)
            prod = dcv * _shifted_rows(vsh, off + k, tm)
            wacc[k] += jnp.sum(prod.reshape(tm // SUBLANES, SUBLANES, C), axis=0)
        db_ref[...] += jnp.sum(dcv, axis=0, keepdims=True)
        dvg_ref[:, 0:C] = (dv * sg).astype(BF16)
        dvg_ref[:, C:] = (dv * val * sg * (1.0 - sg)).astype(BF16)

        @pl.when(t == nt - 1)
        def _():
            dw_ref[...] = jnp.sum(wacc[...], axis=1)

    cur = lambda col: pl.BlockSpec((tm, C), lambda t: (t, col))
    prev = lambda col: pl.BlockSpec((CONV_HALO, C), lambda t: (jnp.maximum(t * hb - 1, 0), col))
    nxt = pl.BlockSpec((CONV_HALO, C), lambda t: (jnp.minimum((t + 1) * hb, nh - 1), 0))
    row = pl.BlockSpec((tm, C), lambda t: (t, 0))
    return pl.pallas_call(
        body, name="conv_bwd", grid=(nt,),
        in_specs=[cur(Z_VAL), cur(Z_GLU), prev(Z_VAL), prev(Z_GLU), row, nxt, ANY, _full(conv_w.shape)],
        out_specs=[pl.BlockSpec((tm, 2 * C), lambda t: (t, 1)), _full((CONV_HALO, C)), _full((1, C))],
        out_shape=[jax.ShapeDtypeStruct(dz.shape, dz.dtype),
                   jax.ShapeDtypeStruct((CONV_HALO, C), F32), jax.ShapeDtypeStruct((1, C), F32)],
        scratch_shapes=[pltpu.VMEM((8, tm + CONV_HALO, C), F32), pltpu.VMEM((8, tm + CONV_HALO, C), F32),
                        pltpu.VMEM((CONV_HALO, SUBLANES, C), F32)],
        input_output_aliases={6: 0},
        compiler_params=_cp("arbitrary"),
    )(z, z, z, z, dc, dc, dz, conv_w)


def _ln_parts(c):
    mu = jnp.mean(c, axis=-1, keepdims=True)
    cc = c - mu
    rstd = lax.rsqrt(jnp.mean(cc * cc, axis=-1, keepdims=True) + EPS)
    return rstd, cc * rstd


def _ev_tail_branches(ys, c, wglu, bglu, lng, lnb):
    z1 = _gelu(ys)
    z1b = z1.astype(BF16)
    sg = _sigmoid(_dot_rows(z1b, wglu) + bglu)
    out = z1 * sg
    rstd, chat = _ln_parts(c)
    cn = chat * lng + lnb
    return z1, z1b, sg, out, rstd, chat, cn


def ev_tail_fwd(ys, z, c, x0, wglu, bglu, lng, lnb, wout, tm=ROW_TILE):
    L, D = x0.shape
    tm = min(tm, L)
    W = S5_WIDTH

    def body(ys_ref, ga_ref, c_ref, gb_ref, x_ref, wglu_ref, bglu_ref, lng_ref, lnb_ref, wout_ref, o_ref):
        _, _, _, out, _, _, cn = _ev_tail_branches(ys_ref[...], c_ref[...], wglu_ref, bglu_ref[...],
                                                   lng_ref[...], lnb_ref[...])
        ya = (out * _silu(ga_ref[...])).astype(BF16)
        yb = (_silu(cn) * _silu(gb_ref[...])).astype(BF16)
        o_ref[...] = x_ref[...] + _dot_rows(jnp.concatenate([ya, yb], axis=1), wout_ref)

    row = lambda n, col=0: pl.BlockSpec((tm, n), lambda t: (t, col))
    return pl.pallas_call(
        body, name="ev_tail_fwd", grid=(L // tm,),
        in_specs=[row(W), row(W, Z_GA), row(W), row(W, Z_GB), row(D), _full(wglu.shape), _full(bglu.shape),
                  _full(lng.shape), _full(lnb.shape), _full(wout.shape)],
        out_specs=row(D), out_shape=jax.ShapeDtypeStruct((L, D), F32), compiler_params=_cp("parallel"),
    )(ys, z, c, z, x0, wglu, bglu, lng, lnb, wout)


def ev_tail_bwd(ys, z, c, dx1, wglu, bglu, lng, lnb, wout, tm=ROW_TILE):
    L, D = dx1.shape
    tm = min(tm, L)
    W = S5_WIDTH

    def body(ys_ref, ga_ref, c_ref, gb_ref, dx_ref, wglu_ref, bglu_ref, lng_ref, lnb_ref, wout_ref,
             dys_ref, dc_ref, dz_ref, r_ref, z1_ref, dt_ref, dbg_ref, dlg_ref, dlb_ref):
        @pl.when(pl.program_id(0) == 0)
        def _():
            for r in (dbg_ref, dlg_ref, dlb_ref):
                r[...] = jnp.zeros_like(r)

        ys, ga, gb = ys_ref[...], ga_ref[...], gb_ref[...]
        z1, z1b, sg, out, rstd, chat, cn = _ev_tail_branches(ys, c_ref[...], wglu_ref, bglu_ref[...],
                                                             lng_ref[...], lnb_ref[...])
        sga, sgb, scn = _silu(ga), _silu(gb), _silu(cn)
        r_ref[:, 0:W] = (out * sga).astype(BF16)
        r_ref[:, W:] = (scn * sgb).astype(BF16)
        dr = _dot_nt_rows(dx_ref[...].astype(BF16), wout_ref)
        dra, drb = dr[:, 0:W], dr[:, W:]
        dz_ref[...] = jnp.zeros_like(dz_ref)
        dz_ref[:, Z_GA * W:(Z_GA + 1) * W] = (dra * out * _dsilu(ga)).astype(BF16)
        dout = dra * sga
        dt = dout * z1 * sg * (1.0 - sg)
        dtb = dt.astype(BF16)
        dz1 = dout * sg + _dot_nt_rows(dtb, wglu_ref)
        dys_ref[...] = dz1 * _dgelu(ys)
        z1_ref[...] = z1b
        dt_ref[...] = dtb
        dbg_ref[...] += jnp.sum(dt, axis=0, keepdims=True)
        dz_ref[:, Z_GB * W:(Z_GB + 1) * W] = (drb * scn * _dsilu(gb)).astype(BF16)
        dcn = drb * sgb * _dsilu(cn)
        dlg_ref[...] += jnp.sum(dcn * chat, axis=0, keepdims=True)
        dlb_ref[...] += jnp.sum(dcn, axis=0, keepdims=True)
        dch = dcn * lng_ref[...]
        dc_ref[...] = rstd * (dch - jnp.mean(dch, axis=-1, keepdims=True)
                              - chat * jnp.mean(dch * chat, axis=-1, keepdims=True))

    row = lambda n, col=0: pl.BlockSpec((tm, n), lambda t: (t, col))
    f = lambda n, dt: jax.ShapeDtypeStruct((L, n), dt)
    vec = jax.ShapeDtypeStruct((1, W), F32)
    return pl.pallas_call(
        body, name="ev_tail_bwd", grid=(L // tm,),
        in_specs=[row(W), row(W, Z_GA), row(W), row(W, Z_GB), row(D), _full(wglu.shape), _full(bglu.shape),
                  _full(lng.shape), _full(lnb.shape), _full(wout.shape)],
        out_specs=[row(W), row(W), row(EVEN_IN), row(D), row(W), row(W), _full((1, W)), _full((1, W)), _full((1, W))],
        out_shape=[f(W, F32), f(W, F32), f(EVEN_IN, BF16), f(D, BF16), f(W, BF16), f(W, BF16), vec, vec, vec],
        compiler_params=_cp("arbitrary"),
    )(ys, z, c, z, dx1, wglu, bglu, lng, lnb, wout)


XA_SCALE = XA_HEAD_DIM ** -0.5


def _xa_forward(xv, g, wqg, kv):
    D = D_MODEL
    _, xhat = _rms_parts(xv)
    hb = (xhat * g).astype(BF16)
    qb = _dot_cols(hb, wqg, (0, 1)).astype(BF16)
    gate = _dot_cols(hb, wqg, (2, 3))
    ps, os_ = [], []
    for h in range(XA_HEADS):
        lo, hi = h * XA_HEAD_DIM, (h + 1) * XA_HEAD_DIM
        s = _dot_nt(qb[:, lo:hi], kv[:, lo:hi]) * XA_SCALE
        e = jnp.exp(s - jnp.max(s, axis=-1, keepdims=True))
        p = e / jnp.sum(e, axis=-1, keepdims=True)
        ps.append(p)
        os_.append(_dot(p.astype(BF16), kv[:, D + lo:D + hi]))
    return hb, qb, gate, ps, jnp.concatenate(os_, axis=1)


def xa_fwd(x, g, wqg, kv, wo, layer, name, tm=ROW_TILE):
    L, D = x.shape
    tm = min(tm, L)

    def body(x_ref, g_ref, wqg_ref, kv_ref, wo_ref, o_ref):
        xv = x_ref[...]
        _, _, gate, _, o = _xa_forward(xv, g_ref[...], wqg_ref, kv_ref[...])
        o_ref[...] = xv + _dot_rows((o * _silu(gate)).astype(BF16), wo_ref)

    row = pl.BlockSpec((tm, D), lambda t: (t, 0))
    return pl.pallas_call(
        body, name=name, grid=(L // tm,),
        in_specs=[row, _full(g.shape), _wspec(wqg, layer), _full(kv.shape), _wspec(wo, layer)],
        out_specs=row, out_shape=jax.ShapeDtypeStruct((L, D), F32), compiler_params=_cp("parallel"),
    )(x, g, wqg, kv, wo)


def xa_bwd(x, dxo, g, wqg, kv, wo, layer, name, tm=ROW_TILE):
    L, D = x.shape
    tm = min(tm, L)

    def body(x_ref, dxo_ref, g_ref, wqg_ref, kv_ref, wo_ref, dx_ref, dqg_ref, h_ref, r_ref, dkv_ref, dg_ref):
        @pl.when(pl.program_id(0) == 0)
        def _():
            dkv_ref[...] = jnp.zeros_like(dkv_ref)
            dg_ref[...] = jnp.zeros_like(dg_ref)

        xv = x_ref[...]
        kv = kv_ref[...]
        hb, qb, gate, ps, o = _xa_forward(xv, g_ref[...], wqg_ref, kv)
        sgate = _silu(gate)
        h_ref[...] = hb
        r_ref[...] = (o * sgate).astype(BF16)
        dxo = dxo_ref[...]
        dr = _dot_nt_rows(dxo.astype(BF16), wo_ref)
        do = dr * sgate
        dqg_ref[:, D:] = (dr * o * _dsilu(gate)).astype(BF16)
        dob = do.astype(BF16)
        for h in range(XA_HEADS):
            lo, hi = h * XA_HEAD_DIM, (h + 1) * XA_HEAD_DIM
            p = ps[h]
            pb = p.astype(BF16)
            dp = _dot_nt(dob[:, lo:hi], kv[:, D + lo:D + hi])
            dkv_ref[:, D + lo:D + hi] += _dot_tn(pb, dob[:, lo:hi])
            ds = p * (dp - jnp.sum(dp * p, axis=-1, keepdims=True))
            dsb = (ds * XA_SCALE).astype(BF16)
            dqg_ref[:, lo:hi] = _dot(dsb, kv[:, lo:hi]).astype(BF16)
            dkv_ref[:, lo:hi] += _dot_tn(dsb, qb[:, lo:hi])
        dh = _dot_nt_cols(_col_pieces(dqg_ref[...], D // 2), wqg_ref)
        dx, dg = _rms_bwd(xv, g_ref[...], dh)
        dx_ref[...] = dxo + dx
        dg_ref[...] += dg

    row = lambda n: pl.BlockSpec((tm, n), lambda t: (t, 0))
    return pl.pallas_call(
        body, name=name, grid=(L // tm,),
        in_specs=[row(D), row(D), _full(g.shape), _wspec(wqg, layer), _full(kv.shape), _wspec(wo, layer)],
        out_specs=[row(D), row(2 * D), row(D), row(D), _full(kv.shape), _full((1, D))],
        out_shape=[jax.ShapeDtypeStruct((L, D), F32), jax.ShapeDtypeStruct((L, 2 * D), BF16),
                   jax.ShapeDtypeStruct((L, D), BF16), jax.ShapeDtypeStruct((L, D), BF16),
                   jax.ShapeDtypeStruct(kv.shape, F32), jax.ShapeDtypeStruct((1, D), F32)],
        compiler_params=_cp("arbitrary"),
    )(x, dxo, g, wqg, kv, wo)


ATT_SCALE = ATT_HEAD_DIM ** -0.5
ATT_PAIRS = ATT_HEADS // 2
SKEW_LANES = 1024
REL_LANES = 384


def _skew(x, left):
    amt = (ATT_QB - 1) - lax.broadcasted_iota(jnp.int32, (ATT_QB, 1), 0)
    for bit in range(8):
        sh = (SKEW_LANES - (1 << bit)) if left else (1 << bit)
        x = jnp.where(((amt >> bit) & 1) == 1, pltpu.roll(x, sh, 1), x)
    return x


def _dist_onehot(shape, dist_axis):
    j = lax.broadcasted_iota(jnp.int32, shape, dist_axis)
    r = lax.broadcasted_iota(jnp.int32, shape, 1 - dist_axis)
    return (jnp.clip((ATT_WIN - 1) - j, -MAX_REL, MAX_REL) + MAX_REL == r).astype(BF16)


def _dot_exact(v, onehot):
    acc = jnp.zeros((v.shape[0], onehot.shape[1]), F32)
    rem = v
    for _ in range(3):
        part = rem.astype(BF16)
        acc = acc + _dot(part, onehot)
        rem = rem - part.astype(F32)
    return acc


ATT_EDGE = ATT_PAD // ATT_QB


def att_bias(rel_bias):
    H = rel_bias.shape[0]
    rb = jnp.pad(rel_bias, ((0, 0), (0, REL_LANES - rel_bias.shape[1]))).reshape(H, 1, REL_LANES)

    def body(rb_ref, o_ref):
        by_col = _dot_exact(jnp.broadcast_to(rb_ref[...], (8, REL_LANES)), _dist_onehot((REL_LANES, SKEW_LANES), 1))
        x = _skew(jnp.broadcast_to(by_col[0:1, :], (ATT_QB, SKEW_LANES)), left=True)[:, 0:ATT_WIN]
        qc = lax.broadcasted_iota(jnp.int32, (ATT_QB, 1), 0) // CHUNK + LEFT_CHUNKS
        col = lax.broadcasted_iota(jnp.int32, (1, ATT_WIN), 1)
        dc = qc - col // CHUNK
        band = (dc >= 0) & (dc <= LEFT_CHUNKS)
        for blk in range(ATT_EDGE + 1):
            o_ref[blk] = jnp.where(band & (col >= ATT_PAD - blk * ATT_QB), x, NEG)

    return pl.pallas_call(
        body, name="att_bias", grid=(H,),
        in_specs=[pl.BlockSpec((None, 1, REL_LANES), lambda h: (h, 0, 0))],
        out_specs=pl.BlockSpec((ATT_EDGE + 1, None, ATT_QB, ATT_WIN), lambda h: (0, h, 0, 0)),
        out_shape=jax.ShapeDtypeStruct((ATT_EDGE + 1, H, ATT_QB, ATT_WIN), F32), compiler_params=_cp("parallel"),
    )(rb)


def relbias_bwd(dbias):
    H = dbias.shape[0]

    def body(x_ref, o_ref):
        x = jnp.concatenate([x_ref[...], jnp.zeros((ATT_QB, SKEW_LANES - ATT_WIN), F32)], axis=1)
        col = jnp.sum(_skew(x, left=False), axis=0, keepdims=True)
        o_ref[...] = _dot_exact(jnp.broadcast_to(col, (8, SKEW_LANES)), _dist_onehot((SKEW_LANES, REL_LANES), 0))

    out = pl.pallas_call(
        body, name="relbias_bwd", grid=(H,),
        in_specs=[pl.BlockSpec((None, ATT_QB, ATT_WIN), lambda h: (h, 0, 0))],
        out_specs=pl.BlockSpec((None, 8, REL_LANES), lambda h: (h, 0, 0)),
        out_shape=jax.ShapeDtypeStruct((H, 8, REL_LANES), F32), compiler_params=_cp("parallel"),
    )(dbias)
    return out[:, 0, :2 * MAX_REL + 1]


def _ca_scores(qh, kw, bias):
    s = _dot_nt(qh, kw) + bias
    e = jnp.exp(s - jnp.max(s, axis=-1, keepdims=True))
    return e, 1.0 / jnp.sum(e, axis=-1, keepdims=True)


def _ca_head(qv, m):
    return jnp.where(m, qv, jnp.zeros_like(qv)) * ATT_SCALE


def _ca_bias_spec():
    return pl.BlockSpec((None, 2, ATT_QB, ATT_WIN), lambda hp, b: (jnp.minimum(b, ATT_EDGE), hp, 0, 0))


def ca_fwd(q, kvp, gate, bias):
    L, D = q.shape
    Lp = kvp.shape[0]
    nb = L // ATT_QB

    def body(q_ref, k_ref, v_ref, g_ref, b_ref, r_ref):
        w = pl.multiple_of(pl.program_id(1) * ATT_QB, ATT_QB)
        kw = k_ref[pl.ds(w, ATT_WIN), :]
        vw = v_ref[pl.ds(w, ATT_WIN), :]
        qv = q_ref[...]
        first = lax.broadcasted_iota(jnp.int32, (1, 128), 1) < ATT_HEAD_DIM
        outs = []
        for hh, m in enumerate((first, jnp.logical_not(first))):
            e, inv = _ca_scores(_ca_head(qv, m), kw, b_ref[hh])
            outs.append(_dot(e.astype(BF16), vw) * inv)
        o = jnp.where(first, outs[0], outs[1])
        r_ref[...] = (o * _silu(g_ref[...])).astype(BF16)

    blk = pl.BlockSpec((ATT_QB, 128), lambda hp, b: (b, hp))
    return pl.pallas_call(
        body, name="ca_fwd", grid=(ATT_PAIRS, nb),
        in_specs=[blk, pl.BlockSpec((Lp, 128), lambda hp, b: (0, hp)),
                  pl.BlockSpec((Lp, 128), lambda hp, b: (0, ATT_PAIRS + hp)), blk, _ca_bias_spec()],
        out_specs=blk, out_shape=jax.ShapeDtypeStruct((L, D), BF16),
        compiler_params=_cp("parallel", "arbitrary"),
    )(q, kvp, kvp, gate, bias)


def ca_bwd(q, kvp, gate, bias, dr):
    L, D = q.shape
    Lp = kvp.shape[0]
    nb = L // ATT_QB

    def body(q_ref, k_ref, v_ref, g_ref, b_ref, dr_ref, dq_ref, dg_ref, dk_ref, dv_ref, db_ref):
        b = pl.program_id(1)

        @pl.when(b == 0)
        def _():
            for r in (dk_ref, dv_ref, db_ref):
                r[...] = jnp.zeros_like(r)

        w = pl.multiple_of(b * ATT_QB, ATT_QB)
        kw = k_ref[pl.ds(w, ATT_WIN), :]
        vw = v_ref[pl.ds(w, ATT_WIN), :]
        qv = q_ref[...]
        gate_v = g_ref[...]
        drv = dr_ref[...]
        do = drv * _silu(gate_v)
        first = lax.broadcasted_iota(jnp.int32, (1, 128), 1) < ATT_HEAD_DIM
        outs, dqs = [], []
        dkw = jnp.zeros((ATT_WIN, 128), F32)
        dvw = jnp.zeros((ATT_WIN, 128), F32)
        for hh, m in enumerate((first, jnp.logical_not(first))):
            qh = _ca_head(qv, m)
            e, inv = _ca_scores(qh, kw, b_ref[hh])
            eb = e.astype(BF16)
            outs.append(_dot(eb, vw) * inv)
            doh = jnp.where(m, do, 0.0)
            dp = _dot_nt(doh.astype(BF16), vw)
            dvw = dvw + _dot_tn(eb, (doh * inv).astype(BF16))
            rs = jnp.sum(dp * e, axis=-1, keepdims=True) * inv
            ds = e * ((dp - rs) * inv)
            db_ref[hh] += ds
            dsb = ds.astype(BF16)
            dqs.append(_dot(dsb, kw))
            dkw = dkw + _dot_tn(dsb, qh)
        o = jnp.where(first, outs[0], outs[1])
        dg_ref[...] = (drv * o * _dsilu(gate_v)).astype(BF16)
        dq_ref[...] = (jnp.where(first, dqs[0], dqs[1]) * ATT_SCALE).astype(BF16)
        dk_ref[pl.ds(w, ATT_WIN), :] += dkw
        dv_ref[pl.ds(w, ATT_WIN), :] += dvw

    blk = pl.BlockSpec((ATT_QB, 128), lambda hp, b: (b, hp))
    kblk = pl.BlockSpec((Lp, 128), lambda hp, b: (0, hp))
    vblk = pl.BlockSpec((Lp, 128), lambda hp, b: (0, ATT_PAIRS + hp))
    bblk = pl.BlockSpec((2, ATT_QB, ATT_WIN), lambda hp, b: (hp, 0, 0))
    return pl.pallas_call(
        body, name="ca_bwd", grid=(ATT_PAIRS, nb),
        in_specs=[blk, kblk, vblk, blk, _ca_bias_spec(), blk],
        out_specs=[blk, blk, kblk, kblk, bblk],
        out_shape=[jax.ShapeDtypeStruct((L, D), BF16), jax.ShapeDtypeStruct((L, D), BF16),
                   jax.ShapeDtypeStruct((Lp, D), F32), jax.ShapeDtypeStruct((Lp, D), F32),
                   jax.ShapeDtypeStruct(bias.shape[1:], F32)],
        compiler_params=_cp("parallel", "arbitrary"),
    )(q, kvp, kvp, gate, bias, dr)


def loss_bwd(x, target, g, tm=ROW_TILE):
    L, D = x.shape
    tm = min(tm, L)

    def body(x_ref, t_ref, g_ref, loss_ref, dx_ref, dg_ref):
        @pl.when(pl.program_id(0) == 0)
        def _():
            loss_ref[...] = jnp.zeros_like(loss_ref)
            dg_ref[...] = jnp.zeros_like(dg_ref)

        xv = x_ref[...]
        gv = g_ref[...]
        _, xhat = _rms_parts(xv)
        err = xhat * gv - t_ref[...]
        loss_ref[...] += 0.5 * jnp.sum(jnp.sum(err * err, axis=-1, keepdims=True), axis=0, keepdims=True) / D
        dx, dg = _rms_bwd(xv, gv, err / D)
        dx_ref[...] = dx
        dg_ref[...] += dg

    row = pl.BlockSpec((tm, D), lambda t: (t, 0))
    return pl.pallas_call(
        body, name="loss_bwd", grid=(L // tm,),
        in_specs=[row, row, _full(g.shape)],
        out_specs=[_full((1, 128)), row, _full((1, D))],
        out_shape=[jax.ShapeDtypeStruct((1, 128), F32), jax.ShapeDtypeStruct((L, D), F32),
                   jax.ShapeDtypeStruct((1, D), F32)],
        compiler_params=_cp("arbitrary"),
    )(x, target, g)


_ADAM_C1 = 1.0 / (1.0 - ADAM_B1 ** ADAM_STEP)
_ADAM_C2 = 1.0 / (1.0 - ADAM_B2 ** ADAM_STEP)


def _adam_update(w, g, m, v):
    mn = ADAM_B1 * m + (1.0 - ADAM_B1) * g
    vn = ADAM_B2 * v + (1.0 - ADAM_B2) * g * g
    delta = -ADAM_LR * ((mn * _ADAM_C1) / (jnp.sqrt(vn * _ADAM_C2) + ADAM_EPS) + ADAM_WD * w)
    return delta, mn, vn


def adamw(w, g, m, v, name, tr=512):
    R, C = w.shape
    tr = min(tr, R)

    def body(w_ref, g_ref, m_ref, v_ref, d_ref, mo_ref, vo_ref):
        d_ref[...], mo_ref[...], vo_ref[...] = _adam_update(w_ref[...], g_ref[...], m_ref[...], v_ref[...])

    blk = pl.BlockSpec((tr, C), lambda i: (i, 0))
    sh = jax.ShapeDtypeStruct((R, C), F32)
    return pl.pallas_call(
        body, name=name, grid=(R // tr,), in_specs=[blk] * 4, out_specs=[blk] * 3,
        out_shape=[sh] * 3, compiler_params=_cp("parallel"),
    )(w, g, m, v)


def adamw_allreduce(gathered, w, m, v, shard, name, slot=None):
    R, C = w.shape
    sharded = slot is None and gathered.shape[2] != C

    def body(s_ref, ga_ref, w_ref, m_ref, v_ref, g_ref, d_ref, mo_ref, vo_ref):
        take = (lambda d: ga_ref[d]) if slot is None else (lambda d: ga_ref[d, slot:slot + R, 0:C])
        g = take(0)
        for d in range(1, N_DEV):
            g = g + take(d)
        g_ref[...] = g
        d_ref[...], mo_ref[...], vo_ref[...] = _adam_update(w_ref[...], g, m_ref[...], v_ref[...])

    blk = pl.BlockSpec((R, C), lambda i, s_ref: (0, 0))
    if slot is not None:
        gblk = pl.BlockSpec(gathered.shape, lambda i, s_ref: (0, 0, 0))
    else:
        gblk = pl.BlockSpec((N_DEV, R, C),
                            (lambda i, s_ref: (0, 0, s_ref[0])) if sharded else (lambda i, s_ref: (0, 0, 0)))
    sh = jax.ShapeDtypeStruct((R, C), F32)
    return pl.pallas_call(
        body, name=name,
        grid_spec=pltpu.PrefetchScalarGridSpec(num_scalar_prefetch=1, grid=(1,), in_specs=[gblk, blk, blk, blk],
                                               out_specs=[blk] * 4),
        out_shape=[sh] * 4, compiler_params=_cp("arbitrary"),
    )(shard, gathered, w, m, v)


def local_step(x, mem, target, p, gw):
    row = lambda a: a.reshape(1, -1)
    D = D_MODEL
    L = x.shape[0]
    g, big = {}, {}

    memn_b = rms_fwd(mem, row(p["mem_norm_g"]), "mem_norm")
    kvs = [mm_cols(memn_b, gw["xa_w_kv"], l, f"xa_kv{l}", BF16) for l in range(2)]

    z, h0b = norm_mm(x, p["ev_norm_g"], gw["ev_w_in"], [((0, 1, 2, 3), F32, 0)], "ev_in")
    ys, s5_saved = s5_mixer_core_fwd(z, p["ev_s5_lambda_re"][0], p["ev_s5_lambda_im"][0], p["ev_s5_log_dt"][0],
                                     p["ev_s5_b_re"][0], p["ev_s5_b_im"][0], p["ev_s5_c_re"][0], p["ev_s5_c_im"][0],
                                     p["ev_s5_d"][0])
    conv_w = p["ev_conv_w"][0]
    c = conv_fwd(z, conv_w, p["ev_conv_b"])
    tail = (gw["ev_s5_glu_w"], p["ev_s5_glu_b"], p["ev_conv_ln_g"], p["ev_conv_ln_b"], gw["ev_w_out"])
    x1 = ev_tail_fwd(ys, z, c, x, *tail)
    xa0 = (row(p["xa_norm_g"][0]), gw["xa_w_qg"], kvs[0], gw["xa_w_o"], 0)
    x2 = xa_fwd(x1, *xa0, "xa_fwd0")

    q, kvp, gate, h1b = norm_mm(x2, p["od_norm_g"], gw["od_w_in"],
                                [((0,), BF16, 0), ((1, 2), BF16, ATT_PAD), ((3,), F32, 0)], "od_in")
    kvp = zero_rows(kvp, ATT_PAD, "od_kv_pad")
    bias = att_bias(p["od_rel_bias"][0])
    r = ca_fwd(q, kvp, gate, bias)
    x3 = mm_res(r, gw["od_w_out"], x2, "od_out")
    xa1 = (row(p["xa_norm_g"][1]), gw["xa_w_qg"], kvs[1], gw["xa_w_o"], 1)
    x4 = xa_fwd(x3, *xa1, "xa_fwd1")

    loss, dx4, dgf = loss_bwd(x4, target, row(p["final_norm_g"]))
    g["final_norm_g"] = dgf.reshape(D)

    dx3, dqg1, hx1, rx1, dkv1, dgxa1 = xa_bwd(x3, dx4, *xa1, "xa_bwd1")
    dwqg = mm_tn(hx1, dqg1, "xa_dwqg1", ("cols", 1))
    dwo = mm_tn(rx1, dx4, "xa_dwo1", ("rows", 1))

    big["od_w_out"] = mm_tn(r, dx3, "od_dwout", ("rows",))
    dr = mm_nt_rows(dx3, gw["od_w_out"], "od_out_bwd")
    dq, dgate, dkp, dvp, dbias = ca_bwd(q, kvp, gate, bias, dr)
    pieces, offs = (dq, dkp, dvp, dgate), (0, ATT_PAD, ATT_PAD, 0)
    dwin = None
    for s in range(N_CHIPS):
        dwin = mm_tn(h1b, pieces[s], f"od_dwin{s}", ("slab", s), into=dwin, b_off=offs[s], bl=ATT_PAD)
    big["od_w_in"] = dwin
    dx2, dgod = mm_nt_normbwd(pieces, offs, gw["od_w_in"], x2, p["od_norm_g"], dx3, "od_in_bwd")
    g["od_norm_g"] = dgod
    g["od_rel_bias"] = relbias_bwd(dbias)[None]

    dx1, dqg0, hx0, rx0, dkv0, dgxa0 = xa_bwd(x1, dx2, *xa0, "xa_bwd0")
    big["xa_w_qg"] = mm_tn(hx0, dqg0, "xa_dwqg0", ("cols", 0), into=dwqg)
    big["xa_w_o"] = mm_tn(rx0, dx2, "xa_dwo0", ("rows", 0), into=dwo)
    g["xa_norm_g"] = jnp.concatenate([dgxa0, dgxa1], axis=0)

    dys, dc, dz, ra, z1b, dtb, dbglu, dlng, dlnb = ev_tail_bwd(ys, z, c, dx1, *tail)
    big["ev_w_out"] = mm_tn(ra, dx1, "ev_dwout", ("rows",))
    big["ev_s5_glu_w"] = mm_tn(z1b, dtb, "ev_dwglu", ("rows",))
    g["ev_s5_glu_b"], g["ev_conv_ln_g"], g["ev_conv_ln_b"] = dbglu, dlng, dlnb
    dz, dconvw, dconvb = conv_bwd(z, dc, dz, conv_w)
    g["ev_conv_w"] = dconvw[None, :CONV_KERNEL]
    g["ev_conv_b"] = dconvb
    dz, s5g = s5_mixer_core_bwd(z, dys, dz, p["ev_s5_lambda_re"][0], p["ev_s5_lambda_im"][0], s5_saved)
    for n, v in s5g.items():
        g["ev_s5_" + n] = v[None]
    big["ev_w_in"] = mm_tn(h0b, dz, "ev_dwin", ("cols",))
    grad_x, dgev = mm_nt_normbwd((dz,), (0,), gw["ev_w_in"], x, p["ev_norm_g"], dx1, "ev_in_bwd")
    g["ev_norm_g"] = dgev

    dwkv = mm_tn(memn_b, dkv1, "xa_dwkv1", ("cols", 1), bl=MEM_LEN)
    big["xa_w_kv"] = mm_tn(memn_b, dkv0, "xa_dwkv0", ("cols", 0), into=dwkv, bl=MEM_LEN)
    dmem0 = mm_nt_cols(dkv0, gw["xa_w_kv"], 0, "xa_kv_bwd0")
    dmem1 = mm_nt_cols(dkv1, gw["xa_w_kv"], 1, "xa_kv_bwd1")
    g["mem_norm_g"] = rms_dgain(mem, dmem0, dmem1, "mem_norm_bwd").reshape(D)
    return loss, grad_x, g, big


def _me():
    return lax.axis_index("x"), lax.axis_index("y"), lax.axis_index("c")


def _other_chips(x, y):
    return [(1 - x, y), (x, 1 - y), (1 - x, 1 - y)]


def _remote(src, dst, send_sems, recv_sems, k, to):
    return pltpu.make_async_remote_copy(src_ref=src, dst_ref=dst, send_sem=send_sems.at[k], recv_sem=recv_sems.at[k],
                                        device_id=to, device_id_type=MESH)


def _rows_half(ref, h):
    H = ref.shape[-2] // 2
    return ref.at[(slice(None),) * (len(ref.shape) - 2) + (pl.ds(h * H, H), slice(None))]


def allgather_chip_blocks(halved, whole):
    nh, nw = len(halved), len(whole)
    n = nh + nw

    def body(*refs):
        ins, outs = refs[:n], refs[n:2 * n]
        send_sems, recv_sems, local_sems = refs[2 * n:]
        x, y, c = _me()
        sib = (x, y, 1 - c)
        chips = _other_chips(x, y)
        me = 2 * x + y
        local = [pltpu.make_async_copy(ins[i], outs[i].at[me], local_sems.at[i]) for i in range(n)]
        for cp in local:
            cp.start()
        first, passed = [], []
        for i in range(n):
            for j, (cx, cy) in enumerate(chips):
                if i < nh:
                    src, dst = _rows_half(ins[i], c), _rows_half(outs[i].at[me], c)
                    k = 6 * i + j
                else:
                    src, dst = ins[i], outs[i].at[me]
                    k = 6 * nh + 3 * (i - nh) + j
                first.append(_remote(src, dst, send_sems, recv_sems, k, (cx, cy, c)))
        for cp in first:
            cp.start()
        for j, (cx, cy) in enumerate(chips):
            for i in range(nh):
                got = _rows_half(outs[i].at[2 * cx + cy], c)
                _remote(got, got, send_sems, recv_sems, 6 * i + j, (cx, cy, c)).wait_recv()
                fw = _remote(got, got, send_sems, recv_sems, 6 * i + 3 + j, sib)
                fw.start()
                passed.append(fw)
        for j, (cx, cy) in enumerate(chips):
            for i in range(nh):
                got = _rows_half(outs[i].at[2 * cx + cy], 1 - c)
                _remote(got, got, send_sems, recv_sems, 6 * i + 3 + j, sib).wait_recv()
            for i in range(nh, n):
                got = outs[i].at[2 * cx + cy]
                _remote(got, got, send_sems, recv_sems, 6 * nh + 3 * (i - nh) + j, (cx, cy, c)).wait_recv()
        for cp in first + passed:
            cp.wait_send()
        for cp in local:
            cp.wait()

    arrays = list(halved) + list(whole)
    nsem = 6 * nh + 3 * nw
    return pl.pallas_call(
        body, name="allgather_chip_blocks", in_specs=[ANY] * n, out_specs=[ANY] * n,
        out_shape=[jax.ShapeDtypeStruct((N_CHIPS,) + a.shape, a.dtype) for a in arrays],
        scratch_shapes=[pltpu.SemaphoreType.DMA((nsem,)), pltpu.SemaphoreType.DMA((nsem,)),
                        pltpu.SemaphoreType.DMA((n,))],
    )(*arrays)


def allgather_devices(vs):
    n = len(vs)

    def body(*refs):
        ins, outs = refs[:n], refs[n:2 * n]
        send_sems, recv_sems, local_sems = refs[2 * n:]
        x, y, c = _me()
        sib = (x, y, 1 - c)
        chips = _other_chips(x, y)
        me = 4 * x + 2 * y + c
        local = [pltpu.make_async_copy(ins[i], outs[i].at[me], local_sems.at[i]) for i in range(n)]
        for cp in local:
            cp.start()
        first, passed = [], []
        for i in range(n):
            first.append(_remote(ins[i], outs[i].at[me], send_sems, recv_sems, 7 * i, sib))
            for j, (cx, cy) in enumerate(chips):
                first.append(_remote(ins[i], outs[i].at[me], send_sems, recv_sems, 7 * i + 1 + j, (cx, cy, c)))
        for cp in first:
            cp.start()
        for j, (cx, cy) in enumerate(chips):
            for i in range(n):
                got = outs[i].at[4 * cx + 2 * cy + c]
                _remote(got, got, send_sems, recv_sems, 7 * i + 1 + j, (cx, cy, c)).wait_recv()
                fw = _remote(got, got, send_sems, recv_sems, 7 * i + 4 + j, sib)
                fw.start()
                passed.append(fw)
        for i in range(n):
            got = outs[i].at[4 * x + 2 * y + (1 - c)]
            _remote(got, got, send_sems, recv_sems, 7 * i, sib).wait_recv()
            for j, (cx, cy) in enumerate(chips):
                got = outs[i].at[4 * cx + 2 * cy + (1 - c)]
                _remote(got, got, send_sems, recv_sems, 7 * i + 4 + j, sib).wait_recv()
        for cp in first + passed:
            cp.wait_send()
        for cp in local:
            cp.wait()

    return pl.pallas_call(
        body, name="allgather_devices", in_specs=[ANY] * n, out_specs=[ANY] * n,
        out_shape=[jax.ShapeDtypeStruct((N_DEV,) + v.shape, v.dtype) for v in vs],
        scratch_shapes=[pltpu.SemaphoreType.DMA((7 * n,)), pltpu.SemaphoreType.DMA((7 * n,)),
                        pltpu.SemaphoreType.DMA((n,))],
    )(*vs)


def sibling_send_other_half(gs):
    n = len(gs)

    def body(*refs):
        ins, outs = refs[:n], refs[n:2 * n]
        send_sems, recv_sems = refs[2 * n:]
        x, y, c = _me()
        cps = [_remote(_rows_half(ins[i], 1 - c), outs[i], send_sems, recv_sems, i, (x, y, 1 - c)) for i in range(n)]
        for cp in cps:
            cp.start()
        for cp in cps:
            cp.wait()

    return pl.pallas_call(
        body, name="sibling_send_other_half", in_specs=[ANY] * n, out_specs=[ANY] * n,
        out_shape=[jax.ShapeDtypeStruct((g.shape[0], g.shape[1] // 2, g.shape[2]), g.dtype) for g in gs],
        scratch_shapes=[pltpu.SemaphoreType.DMA((n,)), pltpu.SemaphoreType.DMA((n,))],
    )(*gs)


def chips_exchange(parts):
    n = len(parts)

    def body(*refs):
        ins, outs = refs[:n], refs[n:2 * n]
        send_sems, recv_sems = refs[2 * n:]
        x, y, c = _me()
        cps = []
        for i in range(n):
            nl = ins[i].shape[0] // N_CHIPS
            for j, (cx, cy) in enumerate(_other_chips(x, y)):
                cps.append(_remote(ins[i].at[pl.ds((2 * cx + cy) * nl, nl)], outs[i].at[j], send_sems, recv_sems,
                                   3 * i + j, (cx, cy, c)))
        for cp in cps:
            cp.start()
        for cp in cps:
            cp.wait()

    return pl.pallas_call(
        body, name="chips_exchange", in_specs=[ANY] * n, out_specs=[ANY] * n,
        out_shape=[jax.ShapeDtypeStruct((3, a.shape[0] // N_CHIPS) + a.shape[1:], a.dtype) for a in parts],
        scratch_shapes=[pltpu.SemaphoreType.DMA((3 * n,)), pltpu.SemaphoreType.DMA((3 * n,))],
    )(*parts)


def sibling_share(fulls):
    n = len(fulls)

    def body(*refs):
        outs = refs[n:2 * n]
        send_sems, recv_sems = refs[2 * n:]
        x, y, c = _me()
        cps = [_remote(_rows_half(outs[i], c), _rows_half(outs[i], c), send_sems, recv_sems, i, (x, y, 1 - c))
               for i in range(n)]
        for cp in cps:
            cp.start()
        for i in range(n):
            got = _rows_half(outs[i], 1 - c)
            _remote(got, got, send_sems, recv_sems, i, (x, y, 1 - c)).wait_recv()
        for cp in cps:
            cp.wait_send()

    return pl.pallas_call(
        body, name="sibling_share", in_specs=[ANY] * n, out_specs=[ANY] * n,
        out_shape=[jax.ShapeDtypeStruct(f.shape, f.dtype) for f in fulls],
        input_output_aliases={i: i for i in range(n)},
        scratch_shapes=[pltpu.SemaphoreType.DMA((n,)), pltpu.SemaphoreType.DMA((n,))],
    )(*fulls)


def sum_with_sibling(g, recv, core, name):
    S, H, C = recv.shape
    tr = min(512, H)

    def body(c_ref, g_ref, r_ref, o_ref):
        o_ref[...] = (g_ref[...].astype(F32) + r_ref[...].astype(F32)).astype(o_ref.dtype)

    nb = H // tr
    return pl.pallas_call(
        body, name=name,
        grid_spec=pltpu.PrefetchScalarGridSpec(
            num_scalar_prefetch=1, grid=(S, nb),
            in_specs=[pl.BlockSpec((None, tr, C), lambda s, i, c_ref: (s, c_ref[0] * nb + i, 0)),
                      pl.BlockSpec((None, tr, C), lambda s, i, c_ref: (s, i, 0))],
            out_specs=pl.BlockSpec((None, tr, C), lambda s, i, c_ref: (s, i, 0))),
        out_shape=jax.ShapeDtypeStruct((S, H, C), g.dtype), compiler_params=_cp("parallel", "parallel"),
    )(core, g, recv)


def sum_chips(a, recv, place, name):
    _, nl, H, C = recv.shape
    tr = min(512, H)
    nb = H // tr

    def body(p_ref, a_ref, r_ref, o_ref):
        acc = a_ref[...].astype(F32)
        for j in range(3):
            acc = acc + r_ref[j].astype(F32)
        o_ref[...] = acc

    return pl.pallas_call(
        body, name=name,
        grid_spec=pltpu.PrefetchScalarGridSpec(
            num_scalar_prefetch=1, grid=(nl, nb),
            in_specs=[pl.BlockSpec((None, tr, C), lambda l, i, p_ref: (p_ref[0] * nl + l, i, 0)),
                      pl.BlockSpec((3, None, tr, C), lambda l, i, p_ref: (0, l, i, 0))],
            out_specs=pl.BlockSpec((None, tr, C), lambda l, i, p_ref: (l, p_ref[1] * nb + i, 0))),
        out_shape=jax.ShapeDtypeStruct((nl, 2 * H, C), F32), compiler_params=_cp("parallel", "parallel"),
    )(place, a, recv)


def pack_rows(arrays, name):
    starts, r0 = [], 0
    for a in arrays:
        starts.append(r0)
        r0 += -(-a.shape[0] // SUBLANES) * SUBLANES
    n = len(arrays)

    def body(*refs):
        o_ref = refs[n]
        o_ref[...] = jnp.zeros_like(o_ref)
        for a_ref, s in zip(refs[:n], starts):
            r, c = a_ref.shape
            o_ref[s:s + r, 0:c] = a_ref[...]

    out = pl.pallas_call(body, name=name, out_shape=jax.ShapeDtypeStruct((r0, PACK_COLS), F32))(*arrays)
    return out, starts


BIG = ("ev_w_in", "ev_s5_glu_w", "ev_w_out", "od_w_in", "od_w_out", "xa_w_qg", "xa_w_kv", "xa_w_o")
SHARDED_F32 = (("ev_conv_w", 2), ("od_norm_g", 1))
SMALL = ("mem_norm_g", "ev_norm_g", "ev_s5_lambda_re", "ev_s5_lambda_im", "ev_s5_log_dt", "ev_s5_b_re", "ev_s5_b_im",
         "ev_s5_c_re", "ev_s5_c_im", "ev_s5_d", "ev_s5_glu_b", "ev_conv_b", "ev_conv_ln_g", "ev_conv_ln_b",
         "od_rel_bias", "xa_norm_g", "final_norm_g")
NARROW = ("ev_s5_b_re", "ev_s5_b_im", "ev_s5_c_re", "ev_s5_c_im")
PACK_COLS = 1024
WEIGHTS = ("mem_norm_g", "ev_norm_g", "ev_w_in", "ev_s5_lambda_re", "ev_s5_lambda_im", "ev_s5_log_dt", "ev_s5_b_re",
           "ev_s5_b_im", "ev_s5_c_re", "ev_s5_c_im", "ev_s5_d", "ev_s5_glu_w", "ev_s5_glu_b", "ev_conv_w", "ev_conv_b",
           "ev_conv_ln_g", "ev_conv_ln_b", "ev_w_out", "od_norm_g", "od_w_in", "od_rel_bias", "od_w_out", "xa_norm_g",
           "xa_w_qg", "xa_w_kv", "xa_w_o", "final_norm_g")


def _as2d(a):
    return a.reshape(1, -1) if a.ndim == 1 else a.reshape(-1, a.shape[-1])


def kernel(x, mem, mem_norm_g, ev_norm_g, ev_w_in, ev_s5_lambda_re, ev_s5_lambda_im, ev_s5_log_dt, ev_s5_b_re, ev_s5_b_im, ev_s5_c_re, ev_s5_c_im, ev_s5_d, ev_s5_glu_w, ev_s5_glu_b, ev_conv_w, ev_conv_b, ev_conv_ln_g, ev_conv_ln_b, ev_w_out, od_norm_g, od_w_in, od_rel_bias, od_w_out, xa_norm_g, xa_w_qg, xa_w_kv, xa_w_o, final_norm_g, loss_target, m_mem_norm_g, m_ev_norm_g, m_ev_w_in, m_ev_s5_lambda_re, m_ev_s5_lambda_im, m_ev_s5_log_dt, m_ev_s5_b_re, m_ev_s5_b_im, m_ev_s5_c_re, m_ev_s5_c_im, m_ev_s5_d, m_ev_s5_glu_w, m_ev_s5_glu_b, m_ev_conv_w, m_ev_conv_b, m_ev_conv_ln_g, m_ev_conv_ln_b, m_ev_w_out, m_od_norm_g, m_od_w_in, m_od_rel_bias, m_od_w_out, m_xa_norm_g, m_xa_w_qg, m_xa_w_kv, m_xa_w_o, m_final_norm_g, v_mem_norm_g, v_ev_norm_g, v_ev_w_in, v_ev_s5_lambda_re, v_ev_s5_lambda_im, v_ev_s5_log_dt, v_ev_s5_b_re, v_ev_s5_b_im, v_ev_s5_c_re, v_ev_s5_c_im, v_ev_s5_d, v_ev_s5_glu_w, v_ev_s5_glu_b, v_ev_conv_w, v_ev_conv_b, v_ev_conv_ln_g, v_ev_conv_ln_b, v_ev_w_out, v_od_norm_g, v_od_w_in, v_od_rel_bias, v_od_w_out, v_xa_norm_g, v_xa_w_qg, v_xa_w_kv, v_xa_w_o, v_final_norm_g):
    a = dict(locals())
    w = {n: a[n] for n in WEIGHTS}
    shard = (2 * lax.axis_index("x") + lax.axis_index("y")).reshape(1).astype(jnp.int32)
    core = lax.axis_index("c").reshape(1).astype(jnp.int32)

    halved = [w[n].astype(BF16).reshape((-1,) + w[n].shape[-2:]) for n in BIG]
    halved = [h.reshape(-1, h.shape[-1]) for h in halved]
    whole = [_as2d(w[n]) for n, _ in SHARDED_F32]
    gathered = allgather_chip_blocks(halved, whole)
    gw = {}
    for n, gth in zip(BIG, gathered):
        layers, R, C = (w[n].shape[0],) + w[n].shape[-2:]
        gw[n] = gth.reshape((N_CHIPS, layers, R, C) if layers > 1 else (N_CHIPS, R, C))
    p = {n: w[n] for n in SMALL}
    conv_g, odn_g = gathered[len(BIG):]
    p["ev_conv_w"] = jnp.concatenate([conv_g[s] for s in range(N_CHIPS)], axis=1)[None]
    p["od_norm_g"] = odn_g.reshape(1, D_MODEL)

    loss, grad_x, g, big = local_step(x[0], mem[0], loss_target[0], p, gw)
    loss = lax.psum(loss[0, 0], ("x", "y", "c"))

    gs = [big[n].reshape((-1,) + big[n].shape[-2:]) for n in BIG]
    from_sibling = sibling_send_other_half(gs)
    chip_sums = [sum_with_sibling(gi, ri, core, "sum_sibling_" + n) for n, gi, ri in zip(BIG, gs, from_sibling)]
    from_chips = chips_exchange(chip_sums)
    place = jnp.concatenate([shard, core])
    halves = [sum_chips(ci, ri, place, "sum_chips_" + n) for n, ci, ri in zip(BIG, chip_sums, from_chips)]
    g_big = dict(zip(BIG, sibling_share(halves)))

    out = {tag: {} for tag in ("grad", "delta", "m", "v")}
    for n in BIG:
        sh = w[n].shape
        to2d = lambda t: t.reshape(-1, sh[-1])
        gn = to2d(g_big[n])
        d, mn, vn = adamw(to2d(w[n]), gn, to2d(a["m_" + n]), to2d(a["v_" + n]), "adamw_" + n)
        for tag, val in zip(("grad", "delta", "m", "v"), (gn, d, mn, vn)):
            out[tag][n] = val.reshape(sh)

    packed_names = [n for n in SMALL if n not in NARROW]
    single_names = list(NARROW) + [n for n, _ in SHARDED_F32]
    packed, slots = pack_rows([_as2d(g[n]) for n in packed_names], "pack_small_grads")
    gath = allgather_devices([packed] + [_as2d(g[n]) for n in single_names])
    jobs = [(n, gath[0], s) for n, s in zip(packed_names, slots)]
    jobs += [(n, gt, None) for n, gt in zip(single_names, gath[1:])]
    for n, gt, slot in jobs:
        sh = w[n].shape
        gn, d, mn, vn = adamw_allreduce(gt, _as2d(w[n]), _as2d(a["m_" + n]), _as2d(a["v_" + n]), shard,
                                        "adamw_" + n, slot=slot)
        for tag, val in zip(("grad", "delta", "m", "v"), (gn, d, mn, vn)):
            out[tag][n] = val.reshape(sh)

    res = [loss, grad_x[None]]
    for tag in ("grad", "delta", "m", "v"):
        res += [out[tag][n] for n in WEIGHTS]
    return tuple(res)
```

```python
import math

import jax
import jax.numpy as jnp
import numpy as np
from jax import lax
from jax.experimental import pallas as pl
from jax.experimental.pallas import tpu as pltpu

F32 = jnp.float32
BF16 = jnp.bfloat16

D_MODEL = 1024
CHUNK = 64
LEFT_CHUNKS = 8
S5_WIDTH = 512
S5_GROUP = 16
S5_GROUPS = 32
S5_STATE = 64
S5_COLS = S5_GROUPS * S5_STATE
S5_SPLIT = 4
S5_CC = S5_COLS // S5_SPLIT
S5_UC = S5_WIDTH // S5_SPLIT
CONV_WIDTH = 512
CONV_KERNEL = 31
CONV_HALO = 32
ATT_HEADS = 16
ATT_HEAD_DIM = 64
MAX_REL = 128
MEM_LEN = 256
XA_HEADS = 4
XA_HEAD_DIM = 256
EPS = 1e-6
EVEN_IN = 2560
ODD_IN = 4096

ADAM_LR = 0.001
ADAM_B1 = 0.9
ADAM_B2 = 0.999
ADAM_EPS = 1e-08
ADAM_WD = 0.01
ADAM_STEP = 10

ROW_TILE = 256
ATT_QB = 256
ATT_PAD = LEFT_CHUNKS * CHUNK
ATT_WIN = ATT_PAD + ATT_QB
VMEM_LIMIT_V7X = 56 * 1024 * 1024
NEG = -1e30
LANES = 128
N_CHIPS = 4
N_DEV = 8

MESH = pl.DeviceIdType.MESH
ANY = pl.BlockSpec(memory_space=pl.ANY)


def _cp(*sem, vmem=VMEM_LIMIT_V7X):
    return pltpu.CompilerParams(dimension_semantics=sem if sem else None, vmem_limit_bytes=vmem)


def _full(shape):
    n = len(shape)
    return pl.BlockSpec(shape, lambda *_: (0,) * n)


def _wspec(w, layer=None):
    if layer is None:
        return _full(w.shape)
    s, _, r, c = w.shape
    return pl.BlockSpec((s, None, r, c), lambda *_: (0, layer, 0, 0))


def _lane_tile(n, cap):
    return max(t for t in range(LANES, min(n, cap) + 1, LANES) if n % t == 0)


def _sigmoid(x):
    return 1.0 / (1.0 + jnp.exp(-x))


def _silu(x):
    return x * _sigmoid(x)


def _dsilu(x):
    s = _sigmoid(x)
    return s * (1.0 + x * (1.0 - s))


_GELU_C = math.sqrt(2.0 / math.pi)


def _gelu(x):
    return 0.5 * x * (1.0 + jnp.tanh(_GELU_C * (x + 0.044715 * x * x * x)))


def _dgelu(x):
    t = jnp.tanh(_GELU_C * (x + 0.044715 * x * x * x))
    return 0.5 * (1.0 + t) + 0.5 * x * (1.0 - t * t) * _GELU_C * (1.0 + 3.0 * 0.044715 * x * x)


def _dot(a, b):
    return jnp.dot(a, b, preferred_element_type=F32)


def _dot_nt(a, b):
    return lax.dot_general(a, b, (((1,), (1,)), ((), ())), preferred_element_type=F32)


def _dot_tn(a, b):
    return lax.dot_general(a, b, (((0,), (0,)), ((), ())), preferred_element_type=F32)


def _dot_cols(a, w4, shards=range(N_CHIPS)):
    return jnp.concatenate([_dot(a, w4[s]) for s in shards], axis=1)


def _dot_rows(a, w4):
    r = w4.shape[1]
    acc = _dot(a[:, 0:r], w4[0])
    for s in range(1, N_CHIPS):
        acc = acc + _dot(a[:, s * r:(s + 1) * r], w4[s])
    return acc


def _dot_nt_cols(dys, w4):
    acc = _dot_nt(dys[0], w4[0])
    for s in range(1, N_CHIPS):
        acc = acc + _dot_nt(dys[s], w4[s])
    return acc


def _dot_nt_rows(dy, w4):
    return jnp.concatenate([_dot_nt(dy, w4[s]) for s in range(N_CHIPS)], axis=1)


def _col_pieces(v, n):
    return [v[:, s * n:(s + 1) * n] for s in range(N_CHIPS)]


def _rms_parts(xv):
    inv = lax.rsqrt(jnp.mean(xv * xv, axis=-1, keepdims=True) + EPS)
    return inv, xv * inv


def _rms_bwd(xv, g, dh):
    inv, xhat = _rms_parts(xv)
    dg = jnp.sum(dh * xhat, axis=0, keepdims=True)
    dxh = dh * g
    dx = inv * (dxh - xhat * jnp.mean(dxh * xhat, axis=-1, keepdims=True))
    return dx, dg


def norm_mm(x, g, w4, groups, name, tm=ROW_TILE):
    M, D = x.shape
    n = w4.shape[2]
    tm = min(tm, M)

    def body(x_ref, g_ref, w_ref, *outs):
        _, xhat = _rms_parts(x_ref[...])
        hb = (xhat * g_ref[...]).astype(BF16)
        for o, (shards, dt, _) in zip(outs, groups):
            o[...] = _dot_cols(hb, w_ref, shards).astype(dt)
        outs[-1][...] = hb

    out_shape = [jax.ShapeDtypeStruct((M + pad, len(sh) * n), dt) for (sh, dt, pad) in groups]
    out_specs = [pl.BlockSpec((tm, len(sh) * n), lambda i, p=pad // tm: (i + p, 0)) for (sh, _, pad) in groups]
    out_shape.append(jax.ShapeDtypeStruct((M, D), BF16))
    out_specs.append(pl.BlockSpec((tm, D), lambda i: (i, 0)))
    return pl.pallas_call(
        body, name=name, grid=(M // tm,),
        in_specs=[pl.BlockSpec((tm, D), lambda i: (i, 0)), _full(g.shape), _full(w4.shape)],
        out_specs=out_specs, out_shape=out_shape, compiler_params=_cp("parallel"),
    )(x, g, w4)


def zero_rows(buf, rows, name, tm=ROW_TILE):
    C = buf.shape[1]

    def body(b_ref, o_ref):
        o_ref[...] = jnp.zeros_like(o_ref)

    return pl.pallas_call(
        body, name=name, grid=(rows // tm,), in_specs=[ANY],
        out_specs=pl.BlockSpec((tm, C), lambda i: (i, 0)),
        out_shape=jax.ShapeDtypeStruct(buf.shape, buf.dtype), input_output_aliases={0: 0},
        compiler_params=_cp("parallel"),
    )(buf)


def mm_res(a, w4, res, name, tm=ROW_TILE):
    M, K = a.shape
    N = w4.shape[2]
    tm = min(tm, M)

    def body(a_ref, w_ref, r_ref, o_ref):
        o_ref[...] = r_ref[...] + _dot_rows(a_ref[...], w_ref)

    return pl.pallas_call(
        body, name=name, grid=(M // tm,),
        in_specs=[pl.BlockSpec((tm, K), lambda i: (i, 0)), _full(w4.shape), pl.BlockSpec((tm, N), lambda i: (i, 0))],
        out_specs=pl.BlockSpec((tm, N), lambda i: (i, 0)),
        out_shape=jax.ShapeDtypeStruct((M, N), F32), compiler_params=_cp("parallel"),
    )(a, w4, res)


def mm_cols(a, w, layer, name, out_dtype):
    M = a.shape[0]
    n = w.shape[3]

    def body(a_ref, w_ref, o_ref):
        o_ref[...] = _dot_cols(a_ref[...], w_ref).astype(out_dtype)

    return pl.pallas_call(
        body, name=name, grid=(1,), in_specs=[_full(a.shape), _wspec(w, layer)],
        out_specs=_full((M, N_CHIPS * n)), out_shape=jax.ShapeDtypeStruct((M, N_CHIPS * n), out_dtype),
        compiler_params=_cp("arbitrary"),
    )(a, w)


def mm_nt_cols(dy, w, layer, name):
    M = dy.shape[0]
    K, n = w.shape[2], w.shape[3]

    def body(d_ref, w_ref, o_ref):
        o_ref[...] = _dot_nt_cols(_col_pieces(d_ref[...].astype(BF16), n), w_ref)

    return pl.pallas_call(
        body, name=name, grid=(1,), in_specs=[_full(dy.shape), _wspec(w, layer)],
        out_specs=_full((M, K)), out_shape=jax.ShapeDtypeStruct((M, K), F32), compiler_params=_cp("arbitrary"),
    )(dy, w)


def mm_nt_rows(dy, w4, name, tm=ROW_TILE):
    M, N = dy.shape
    K = N_CHIPS * w4.shape[1]
    tm = min(tm, M)

    def body(d_ref, w_ref, o_ref):
        o_ref[...] = _dot_nt_rows(d_ref[...].astype(BF16), w_ref)

    return pl.pallas_call(
        body, name=name, grid=(M // tm,),
        in_specs=[pl.BlockSpec((tm, N), lambda i: (i, 0)), _full(w4.shape)],
        out_specs=pl.BlockSpec((tm, K), lambda i: (i, 0)),
        out_shape=jax.ShapeDtypeStruct((M, K), F32), compiler_params=_cp("parallel"),
    )(dy, w4)


def mm_nt_normbwd(dys, offs, w4, x, g, dx_out, name, tm=ROW_TILE):
    M, D = x.shape
    n = w4.shape[2]
    tm = min(tm, M)
    nd = len(dys)

    def body(*refs):
        d_refs = refs[:nd]
        w_ref, x_ref, g_ref, dxo_ref, dx_ref, dg_ref = refs[nd:]
        if nd == 1:
            pieces = _col_pieces(d_refs[0][...].astype(BF16), n)
        else:
            pieces = [r[...].astype(BF16) for r in d_refs]
        dh = _dot_nt_cols(pieces, w_ref)
        dx, dg = _rms_bwd(x_ref[...], g_ref[...], dh)
        dx_ref[...] = dxo_ref[...] + dx

        @pl.when(pl.program_id(0) == 0)
        def _():
            dg_ref[...] = jnp.zeros_like(dg_ref)

        dg_ref[...] += dg

    row = lambda c, off=0: pl.BlockSpec((tm, c), lambda i, p=off // tm: (i + p, 0))
    return pl.pallas_call(
        body, name=name, grid=(M // tm,),
        in_specs=[row(d.shape[1], off) for d, off in zip(dys, offs)] + [_full(w4.shape), row(D), _full(g.shape), row(D)],
        out_specs=[row(D), _full((1, D))],
        out_shape=[jax.ShapeDtypeStruct((M, D), F32), jax.ShapeDtypeStruct((1, D), F32)],
        compiler_params=_cp("arbitrary"),
    )(*dys, w4, x, g, dx_out)


def mm_tn(a, b, name, layout, into=None, b_off=0, out_dtype=BF16, bm=1024, bn=1280, bl=1024):
    L, K = a.shape
    N = b.shape[1]
    kind = layout[0]
    arg = layout[1] if len(layout) > 1 else None
    bm, bn, bl = _lane_tile(K, bm), _lane_tile(N, bn), min(bl, L)
    assert L % bl == 0 and b_off % bl == 0, (L, bl, b_off)
    nl = L // bl
    n_sh, r_sh = N // N_CHIPS, K // N_CHIPS
    lay = (None,) if arg is None else (None, None)
    mid = () if arg is None else (arg,)
    gs = 1
    if kind == "plain":
        oshape, oblock, oidx = (K, N), (bm, bn), lambda i, j, l: (i, j)
    elif kind == "slab":
        oshape, oblock, oidx = (N_CHIPS, K, N), (None, bm, bn), lambda i, j, l: (arg, i, j)
    elif kind == "cols":
        bn = max(bn - bn % n_sh, n_sh) if bn >= n_sh else _lane_tile(n_sh, bn)
        gs = max(bn // n_sh, 1)
        per = n_sh // bn if gs == 1 else 1
        oshape = (N_CHIPS,) + ((2,) if arg is not None else ()) + (K, n_sh)
        oblock = ((gs,) if gs > 1 else (None,)) + lay[1:] + (bm, min(bn, n_sh))
        oidx = lambda i, j, l: (j // per,) + mid + (i, j % per)
    else:
        bm = max(bm - bm % r_sh, r_sh) if bm >= r_sh else _lane_tile(r_sh, bm)
        gs = max(bm // r_sh, 1)
        per = r_sh // bm if gs == 1 else 1
        oshape = (N_CHIPS,) + ((2,) if arg is not None else ()) + (r_sh, N)
        oblock = ((gs,) if gs > 1 else (None,)) + lay[1:] + (min(bm, r_sh), bn)
        oidx = lambda i, j, l: (i // per,) + mid + (i % per, j)
    assert K % bm == 0 and N % bn == 0, (K, bm, N, bn)

    def body(a_ref, b_ref, *rest):
        o_ref, acc = rest[-2], rest[-1]
        l = pl.program_id(2)

        @pl.when(l == 0)
        def _():
            acc[...] = jnp.zeros_like(acc)

        acc[...] += _dot_tn(a_ref[...].astype(BF16), b_ref[...].astype(BF16))

        @pl.when(l == nl - 1)
        def _():
            if gs == 1:
                o_ref[...] = acc[...].astype(out_dtype)
            elif kind == "cols":
                for t in range(gs):
                    o_ref[t] = acc[:, t * n_sh:(t + 1) * n_sh].astype(out_dtype)
            else:
                for t in range(gs):
                    o_ref[t] = acc[t * r_sh:(t + 1) * r_sh, :].astype(out_dtype)

    in_specs = [pl.BlockSpec((bl, bm), lambda i, j, l: (l, i)),
                pl.BlockSpec((bl, bn), lambda i, j, l, p=b_off // bl: (l + p, j))]
    args = [a, b]
    alias = {}
    if into is not None:
        in_specs.append(ANY)
        args.append(into)
        alias = {2: 0}
    return pl.pallas_call(
        body, name=name, grid=(K // bm, N // bn, nl), in_specs=in_specs,
        out_specs=pl.BlockSpec(oblock, oidx), out_shape=jax.ShapeDtypeStruct(oshape, out_dtype),
        scratch_shapes=[pltpu.VMEM((bm, bn), F32)], input_output_aliases=alias,
        compiler_params=_cp("parallel", "parallel", "arbitrary"),
    )(*args)


def rms_fwd(x, g, name):
    def body(x_ref, g_ref, ob_ref):
        _, xhat = _rms_parts(x_ref[...])
        ob_ref[...] = (xhat * g_ref[...]).astype(BF16)

    return pl.pallas_call(body, name=name, out_shape=jax.ShapeDtypeStruct(x.shape, BF16))(x, g)


def rms_dgain(x, dy0, dy1, name):
    def body(x_ref, d0_ref, d1_ref, o_ref):
        _, xhat = _rms_parts(x_ref[...])
        o_ref[...] = jnp.sum((d0_ref[...] + d1_ref[...]) * xhat, axis=0, keepdims=True)

    return pl.pallas_call(body, name=name, out_shape=jax.ShapeDtypeStruct((1, x.shape[1]), F32))(x, dy0, dy1)


def _s5_discretise(lr, li, logdt, bt_re, bt_im):
    dt = jnp.exp(logdt)
    mag = jnp.exp(lr * dt)
    ab_re = mag * jnp.cos(li * dt)
    ab_im = mag * jnp.sin(li * dt)
    den = lr * lr + li * li
    nr = ab_re - 1.0
    coef_re = (nr * lr + ab_im * li) / den
    coef_im = (ab_im * lr - nr * li) / den
    cr = coef_re[:, None, :]
    ci = coef_im[:, None, :]
    bb_re = cr * bt_re - ci * bt_im
    bb_im = cr * bt_im + ci * bt_re
    return ab_re, ab_im, bb_re, bb_im


def s5_param_fwd(lr, li, logdt, bt_re, bt_im):
    def body(lr_ref, li_ref, ld_ref, br_ref, bi_ref, bbr_ref, bbi_ref):
        _, _, bb_re, bb_im = _s5_discretise(lr_ref[...], li_ref[...], ld_ref[...], br_ref[...], bi_ref[...])
        bbr_ref[...] = bb_re
        bbi_ref[...] = bb_im

    sh = jax.ShapeDtypeStruct(bt_re.shape, F32)
    return pl.pallas_call(body, name="s5_param_fwd", out_shape=[sh, sh])(lr, li, logdt, bt_re, bt_im)


def s5_param_bwd(lr, li, logdt, bt_re, bt_im, d_ab_re, d_ab_im, d_bb_re, d_bb_im):
    def body(lr_ref, li_ref, ld_ref, br_ref, bi_ref, dar_ref, dai_ref, dbr_ref, dbi_ref,
             o_lr, o_li, o_ld, o_br, o_bi):
        _, vjp = jax.vjp(_s5_discretise, lr_ref[...], li_ref[...], ld_ref[...], br_ref[...], bi_ref[...])
        g = vjp((dar_ref[...], dai_ref[...], dbr_ref[...], dbi_ref[...]))
        for o, v in zip((o_lr, o_li, o_ld), g[:3]):
            o[...] = v
        for o, v in zip((o_br, o_bi), g[3:]):
            for c in range(S5_GROUP):
                o[:, c * S5_STATE:(c + 1) * S5_STATE] = v[:, c, :]

    dense = jax.ShapeDtypeStruct((S5_GROUPS, S5_GROUP * S5_STATE), F32)
    shapes = [jax.ShapeDtypeStruct(a.shape, F32) for a in (lr, li, logdt)] + [dense, dense]
    return pl.pallas_call(body, name="s5_param_bwd", out_shape=shapes)(
        lr, li, logdt, bt_re, bt_im, d_ab_re, d_ab_im, d_bb_re, d_bb_im)


def s5_tables(lr_flat, li_flat, logdt_flat):
    def body(lr_ref, li_ref, ld_ref, tab_ref):
        dt = jnp.exp(ld_ref[...])
        a = lr_ref[...] * dt
        th = li_ref[...] * dt
        row = lax.broadcasted_iota(jnp.int32, (8, 1), 0)
        rowf = row.astype(F32)

        def power(e, sign):
            m = jnp.exp(e * a)
            return m * jnp.cos(e * th), sign * m * jnp.sin(e * th)

        k = 0
        for sign, fwd in ((1.0, True), (-1.0, False)):
            for s in (1, 2, 4):
                pr, pi = power(jnp.full((8, 1), float(s), F32), sign)
                keep = (row >= s) if fwd else (row + s < 8)
                tab_ref[k] = jnp.where(keep, pr, 0.0)
                tab_ref[k + 1] = jnp.where(keep, pi, 0.0)
                k += 2
            e = rowf + 1.0 if fwd else 8.0 - rowf
            pr, pi = power(e, sign)
            tab_ref[k] = pr
            tab_ref[k + 1] = pi
            k += 2

    return pl.pallas_call(body, name="s5_tables",
                          out_shape=jax.ShapeDtypeStruct((16, 8, S5_COLS), F32))(lr_flat, li_flat, logdt_flat)


def _scan_block(a, b, tabs, base, cr, ci, reverse):
    for n, s in enumerate((1, 2, 4)):
        mr = tabs[base + 2 * n]
        mi = tabs[base + 2 * n + 1]
        sh = (8 - s) if reverse else s
        ar = pltpu.roll(a, sh, 0)
        br = pltpu.roll(b, sh, 0)
        a, b = a + mr * ar - mi * br, b + mr * br + mi * ar
    pr = tabs[base + 6]
    pi = tabs[base + 7]
    a, b = a + pr * cr - pi * ci, b + pr * ci + pi * cr
    return a, b


class Carried:
    def __init__(self, arrays, out_shapes, sems, start, middle, finish):
        self.arrays, self.out_shapes, self.sems = list(arrays), list(out_shapes), list(sems)
        self.start, self.middle, self.finish = start, middle, finish

    def split(self, refs, n_in, n_out, n_scratch):
        a, o, s = len(self.arrays), len(self.out_shapes), len(self.sems)
        own_in, car_in = refs[:n_in], refs[n_in:n_in + a]
        own_out, car_out = refs[n_in + a:n_in + a + n_out], refs[n_in + a + n_out:n_in + a + n_out + o]
        rest = refs[n_in + a + n_out + o:]
        return own_in + own_out + rest[:n_scratch], (car_in, car_out, rest[n_scratch:n_scratch + s])

    def hooks(self, parts, n_chunks, nt):
        j, t = pl.program_id(0), pl.program_id(1)

        def top():
            pl.when((j == 0) & (t == 0))(lambda: self.start(*parts))
            if self.middle is not None:
                pl.when((j == n_chunks // 2) & (t == 0))(lambda: self.middle(*parts))

        def end():
            pl.when((j == n_chunks - 1) & (t == nt - 1))(lambda: self.finish(*parts))

        return top, end


def s5_fwd(z, bbd_re, bbd_im, ccd_re, ccd_im, tab, dskip, tm=ROW_TILE, carried=None):
    L = z.shape[0]
    tm = min(tm, L)
    nt = L // tm

    def body(*refs):
        top = end = None
        if carried is not None:
            refs, parts = carried.split(refs, 7, 2, 3)
            top, end = carried.hooks(parts, S5_SPLIT, nt)
            top()
        u_ref, bbr_ref, bbi_ref, ccr_ref, cci_ref, tab_ref, d_ref, y_ref, ck_ref, xr, xi, car = refs
        t = pl.program_id(1)

        @pl.when(t == 0)
        def _():
            car[...] = jnp.zeros_like(car)

        u = u_ref[...]
        ub = u.astype(BF16)
        xr[...] = _dot(ub, bbr_ref[...])
        xi[...] = _dot(ub, bbi_ref[...])
        tabs = [tab_ref[k] for k in range(8)]

        def blk(i, c):
            r0 = pl.multiple_of(i * 8, 8)
            a, b = _scan_block(xr[pl.ds(r0, 8), :], xi[pl.ds(r0, 8), :], tabs, 0, c[0], c[1], False)
            xr[pl.ds(r0, 8), :] = a
            xi[pl.ds(r0, 8), :] = b
            return a[7:8, :], b[7:8, :]

        cr, ci = lax.fori_loop(0, tm // 8, blk, (car[0:1, :], car[1:2, :]))
        car[0:1, :] = cr
        car[1:2, :] = ci
        ck_ref[0:1, :] = cr
        ck_ref[1:2, :] = ci
        y_ref[...] = (_dot(xr[...].astype(BF16), ccr_ref[...]) - _dot(xi[...].astype(BF16), cci_ref[...])
                      + d_ref[...] * u)
        if end is not None:
            end()

    extra = carried.arrays if carried is not None else []
    extra_out = carried.out_shapes if carried is not None else []
    extra_sems = carried.sems if carried is not None else []
    return pl.pallas_call(
        body, name="s5_fwd", grid=(S5_SPLIT, nt),
        in_specs=[pl.BlockSpec((tm, S5_UC), lambda j, t: (t, j)),
                  pl.BlockSpec((None, S5_UC, S5_CC), lambda j, t: (j, 0, 0)),
                  pl.BlockSpec((None, S5_UC, S5_CC), lambda j, t: (j, 0, 0)),
                  pl.BlockSpec((None, S5_CC, S5_UC), lambda j, t: (j, 0, 0)),
                  pl.BlockSpec((None, S5_CC, S5_UC), lambda j, t: (j, 0, 0)),
                  pl.BlockSpec((8, 8, S5_CC), lambda j, t: (0, 0, j)),
                  pl.BlockSpec((1, S5_UC), lambda j, t: (0, j))] + [ANY] * len(extra),
        out_specs=[pl.BlockSpec((tm, S5_UC), lambda j, t: (t, j)),
                   pl.BlockSpec((None, 2, S5_CC), lambda j, t: (t, 0, j))] + [ANY] * len(extra_out),
        out_shape=[jax.ShapeDtypeStruct((L, S5_WIDTH), F32), jax.ShapeDtypeStruct((nt, 2, S5_COLS), F32)] + extra_out,
        scratch_shapes=[pltpu.VMEM((tm, S5_CC), F32), pltpu.VMEM((tm, S5_CC), F32), pltpu.VMEM((2, S5_CC), F32)]
        + extra_sems,
        compiler_params=_cp("arbitrary" if carried is not None else "parallel", "arbitrary"),
    )(z, bbd_re, bbd_im, ccd_re, ccd_im, tab, dskip, *extra)


def s5_bwd(z, dy, dz, ckpt, bbd_re, bbd_im, ccd_re, ccd_im, tab, dskip, tm=ROW_TILE, carried=None):
    L = z.shape[0]
    tm = min(tm, L)
    nt = L // tm

    def body(*refs):
        top = end = None
        if carried is not None:
            refs, parts = carried.split(refs, 10, 7, 7)
            top, end = carried.hooks(parts, S5_SPLIT, nt)
            top()
        (u_ref, dy_ref, dz_ref, ck_ref, bbr_ref, bbi_ref, ccr_ref, cci_ref, tab_ref, d_ref,
         du_ref, da_ref, dbr_ref, dbi_ref, dcr_ref, dci_ref, dd_ref, hr, hi, gr, gi, car, acr, aci) = refs
        t = pl.program_id(1)
        tt = nt - 1 - t

        @pl.when(t == 0)
        def _():
            for r in (car, acr, aci, dbr_ref, dbi_ref, dcr_ref, dci_ref, dd_ref):
                r[...] = jnp.zeros_like(r)

        u = u_ref[...]
        ub = u.astype(BF16)
        dyv = dy_ref[...]
        dyb = dyv.astype(BF16)
        tabs = [tab_ref[k] for k in range(16)]

        live = (tt > 0).astype(F32)
        c0r = ck_ref[0:1, :] * live
        c0i = ck_ref[1:2, :] * live
        hr[0:8, :] = jnp.broadcast_to(c0r, (8, S5_CC))
        hi[0:8, :] = jnp.broadcast_to(c0i, (8, S5_CC))
        hr[8:, :] = _dot(ub, bbr_ref[...])
        hi[8:, :] = _dot(ub, bbi_ref[...])

        def fblk(i, c):
            r0 = pl.multiple_of(i * 8 + 8, 8)
            a, b = _scan_block(hr[pl.ds(r0, 8), :], hi[pl.ds(r0, 8), :], tabs, 0, c[0], c[1], False)
            hr[pl.ds(r0, 8), :] = a
            hi[pl.ds(r0, 8), :] = b
            return a[7:8, :], b[7:8, :]

        lax.fori_loop(0, tm // 8, fblk, (c0r, c0i))
        hrb = hr[8:, :].astype(BF16)
        hib = hi[8:, :].astype(BF16)
        dcr_ref[...] += _dot_tn(hrb, dyb)
        dci_ref[...] -= _dot_tn(hib, dyb)

        gr[...] = _dot_nt(dyb, ccr_ref[...])
        gi[...] = -_dot_nt(dyb, cci_ref[...])
        row0 = lax.broadcasted_iota(jnp.int32, (8, S5_CC), 0) == 0

        def rblk(k, c):
            i = tm // 8 - 1 - k
            r0 = pl.multiple_of(i * 8, 8)
            a, b = _scan_block(gr[pl.ds(r0, 8), :], gi[pl.ds(r0, 8), :], tabs, 8, c[0], c[1], True)
            gr[pl.ds(r0, 8), :] = a
            gi[pl.ds(r0, 8), :] = b
            r1 = pl.multiple_of(i * 8 + 8, 8)
            hpr = jnp.where(row0, pltpu.roll(hr[pl.ds(r0, 8), :], 1, 0), pltpu.roll(hr[pl.ds(r1, 8), :], 1, 0))
            hpi = jnp.where(row0, pltpu.roll(hi[pl.ds(r0, 8), :], 1, 0), pltpu.roll(hi[pl.ds(r1, 8), :], 1, 0))
            acr[...] += a * hpr + b * hpi
            aci[...] += b * hpr - a * hpi
            return a[0:1, :], b[0:1, :]

        cr, ci = lax.fori_loop(0, tm // 8, rblk, (car[0:1, :], car[1:2, :]))
        car[0:1, :] = cr
        car[1:2, :] = ci

        grb = gr[...].astype(BF16)
        gib = gi[...].astype(BF16)
        du_ref[...] = (_dot_nt(grb, bbr_ref[...]) + _dot_nt(gib, bbi_ref[...]) + d_ref[...] * dyv).astype(BF16)
        dbr_ref[...] += _dot_tn(ub, grb)
        dbi_ref[...] += _dot_tn(ub, gib)
        dd_ref[...] += jnp.sum(dyv * u, axis=0, keepdims=True)

        @pl.when(t == nt - 1)
        def _():
            da_ref[0:1, :] = jnp.sum(acr[...], axis=0, keepdims=True)
            da_ref[1:2, :] = jnp.sum(aci[...], axis=0, keepdims=True)

        if end is not None:
            end()

    extra = carried.arrays if carried is not None else []
    extra_out = carried.out_shapes if carried is not None else []
    extra_sems = carried.sems if carried is not None else []
    chunk = lambda a, b: pl.BlockSpec((None, a, b), lambda j, t: (j, 0, 0))
    return pl.pallas_call(
        body, name="s5_bwd", grid=(S5_SPLIT, nt),
        in_specs=[pl.BlockSpec((tm, S5_UC), lambda j, t: (nt - 1 - t, j)),
                  pl.BlockSpec((tm, S5_UC), lambda j, t: (nt - 1 - t, j)),
                  ANY,
                  pl.BlockSpec((None, 2, S5_CC), lambda j, t: (jnp.maximum(nt - 2 - t, 0), 0, j)),
                  chunk(S5_UC, S5_CC), chunk(S5_UC, S5_CC), chunk(S5_CC, S5_UC), chunk(S5_CC, S5_UC),
                  pl.BlockSpec((16, 8, S5_CC), lambda j, t: (0, 0, j)),
                  pl.BlockSpec((1, S5_UC), lambda j, t: (0, j))] + [ANY] * len(extra),
        out_specs=[pl.BlockSpec((tm, S5_UC), lambda j, t: (nt - 1 - t, j)),
                   pl.BlockSpec((None, 2, S5_CC), lambda j, t: (j, 0, 0)),
                   chunk(S5_UC, S5_CC), chunk(S5_UC, S5_CC), chunk(S5_CC, S5_UC), chunk(S5_CC, S5_UC),
                   pl.BlockSpec((1, S5_UC), lambda j, t: (0, j))] + [ANY] * len(extra_out),
        out_shape=[jax.ShapeDtypeStruct(dz.shape, dz.dtype),
                   jax.ShapeDtypeStruct((S5_SPLIT, 2, S5_CC), F32),
                   jax.ShapeDtypeStruct((S5_SPLIT, S5_UC, S5_CC), F32),
                   jax.ShapeDtypeStruct((S5_SPLIT, S5_UC, S5_CC), F32),
                   jax.ShapeDtypeStruct((S5_SPLIT, S5_CC, S5_UC), F32),
                   jax.ShapeDtypeStruct((S5_SPLIT, S5_CC, S5_UC), F32),
                   jax.ShapeDtypeStruct((1, S5_WIDTH), F32)] + extra_out,
        scratch_shapes=[pltpu.VMEM((tm + 8, S5_CC), F32), pltpu.VMEM((tm + 8, S5_CC), F32),
                        pltpu.VMEM((tm, S5_CC), F32), pltpu.VMEM((tm, S5_CC), F32),
                        pltpu.VMEM((2, S5_CC), F32), pltpu.VMEM((8, S5_CC), F32), pltpu.VMEM((8, S5_CC), F32)]
        + extra_sems,
        input_output_aliases={2: 0},
        compiler_params=_cp("arbitrary" if carried is not None else "parallel", "arbitrary"),
    )(z, dy, dz, ckpt, bbd_re, bbd_im, ccd_re, ccd_im, tab, dskip, *extra)


_EYE8 = np.eye(S5_GROUPS // S5_SPLIT, dtype=np.float32)


def _blockdiag(a):
    g, r, c = a.shape
    a = a.reshape(S5_SPLIT, g // S5_SPLIT, r, c)
    out = a[:, :, :, None, :] * _EYE8[None, :, None, :, None].astype(a.dtype)
    return out.reshape(S5_SPLIT, (g // S5_SPLIT) * r, (g // S5_SPLIT) * c)


def _blockdiag_extract(a, r, c):
    n = S5_GROUPS // S5_SPLIT
    a = a.reshape(S5_SPLIT, n, r, n, c)
    d = jnp.stack([a[:, k, :, k, :] for k in range(n)], axis=1)
    return d.reshape(S5_GROUPS, r, c)


def s5_mixer_core_fwd(z, lam_re, lam_im, log_dt, b_re, b_im, c_re, c_im, d_skip, carried=None):
    bt_re = jnp.swapaxes(b_re, 1, 2)
    bt_im = jnp.swapaxes(b_im, 1, 2)
    logdt = log_dt.reshape(S5_GROUPS, 1)
    bb_re, bb_im = s5_param_fwd(lam_re, lam_im, logdt, bt_re, bt_im)
    flat = lambda a: a.reshape(1, S5_COLS)
    tab = s5_tables(flat(lam_re), flat(lam_im), flat(jnp.broadcast_to(logdt, (S5_GROUPS, S5_STATE))))
    bbd_re = _blockdiag(bb_re).astype(BF16)
    bbd_im = _blockdiag(bb_im).astype(BF16)
    ccd_re = _blockdiag(jnp.swapaxes(c_re, 1, 2)).astype(BF16)
    ccd_im = _blockdiag(jnp.swapaxes(c_im, 1, 2)).astype(BF16)
    dsk = d_skip.reshape(1, S5_WIDTH)
    y, ckpt, *landed = s5_fwd(z, bbd_re, bbd_im, ccd_re, ccd_im, tab, dsk, carried=carried)
    saved = (logdt, bt_re, bt_im, bbd_re, bbd_im, ccd_re, ccd_im, tab, dsk, ckpt)
    return y, saved, landed


def s5_b_from_dense(dense):
    return jnp.swapaxes(dense.reshape(S5_GROUPS, S5_GROUP, S5_STATE), 1, 2)


def s5_mixer_core_bwd(z, dy, dz, lam_re, lam_im, saved, carried=None):
    logdt, bt_re, bt_im, bbd_re, bbd_im, ccd_re, ccd_im, tab, dsk, ckpt = saved
    dz, da, dbr, dbi, dcr, dci, dd, *landed = s5_bwd(z, dy, dz, ckpt, bbd_re, bbd_im, ccd_re, ccd_im, tab, dsk,
                                                     carried=carried)
    d_ab_re = da[:, 0, :].reshape(S5_GROUPS, S5_STATE)
    d_ab_im = da[:, 1, :].reshape(S5_GROUPS, S5_STATE)
    d_bb_re = _blockdiag_extract(dbr, S5_GROUP, S5_STATE)
    d_bb_im = _blockdiag_extract(dbi, S5_GROUP, S5_STATE)
    g_lr, g_li, g_ld, g_btr, g_bti = s5_param_bwd(lam_re, lam_im, logdt, bt_re, bt_im,
                                                  d_ab_re, d_ab_im, d_bb_re, d_bb_im)
    g_cre = jnp.swapaxes(_blockdiag_extract(dcr, S5_STATE, S5_GROUP), 1, 2)
    g_cim = jnp.swapaxes(_blockdiag_extract(dci, S5_STATE, S5_GROUP), 1, 2)
    grads = dict(lambda_re=g_lr, lambda_im=g_li, log_dt=g_ld.reshape(S5_GROUPS), b_re=g_btr, b_im=g_bti,
                 c_re=g_cre, c_im=g_cim, d=dd.reshape(S5_WIDTH))
    return dz, grads, landed


Z_U, Z_GA, Z_VAL, Z_GLU, Z_GB = range(5)
SUBLANES = 8


def _shifted_copies(buf, tm):
    n = tm + CONV_HALO - SUBLANES
    for r in range(1, SUBLANES):
        buf[r, 0:n, :] = buf[0, pl.ds(r, n), :]


def _shifted_rows(buf, start, tm):
    return buf[start % SUBLANES, pl.ds(start - start % SUBLANES, tm), :]


def conv_fwd(z, conv_w, conv_b, tm=ROW_TILE):
    L = z.shape[0]
    tm = min(tm, L)
    nt = L // tm
    hb = tm // CONV_HALO
    C = CONV_WIDTH

    def body(val_ref, glu_ref, valh_ref, gluh_ref, w_ref, b_ref, c_ref, vsh):
        live = (pl.program_id(0) > 0).astype(F32)
        vsh[0, 0:CONV_HALO, :] = valh_ref[...] * _sigmoid(gluh_ref[...]) * live
        vsh[0, CONV_HALO:, :] = val_ref[...] * _sigmoid(glu_ref[...])
        _shifted_copies(vsh, tm)
        acc = jnp.broadcast_to(b_ref[...], (tm, C))
        for k in range(CONV_KERNEL):
            acc = acc + w_ref[k:k + 1, :] * _shifted_rows(vsh, CONV_HALO - CONV_KERNEL + 1 + k, tm)
        c_ref[...] = acc

    cur = lambda col: pl.BlockSpec((tm, C), lambda t: (t, col))
    prev = lambda col: pl.BlockSpec((CONV_HALO, C), lambda t: (jnp.maximum(t * hb - 1, 0), col))
    return pl.pallas_call(
        body, name="conv_fwd", grid=(nt,),
        in_specs=[cur(Z_VAL), cur(Z_GLU), prev(Z_VAL), prev(Z_GLU), _full(conv_w.shape), _full(conv_b.shape)],
        out_specs=pl.BlockSpec((tm, C), lambda t: (t, 0)),
        out_shape=jax.ShapeDtypeStruct((L, C), F32),
        scratch_shapes=[pltpu.VMEM((8, tm + CONV_HALO, C), F32)],
        compiler_params=_cp("parallel"),
    )(z, z, z, z, conv_w, conv_b)


def conv_bwd(z, dc, dz, conv_w, tm=ROW_TILE):
    L = z.shape[0]
    tm = min(tm, L)
    nt = L // tm
    hb = tm // CONV_HALO
    nh = L // CONV_HALO
    C = CONV_WIDTH
    off = CONV_HALO - CONV_KERNEL + 1

    def body(val_ref, glu_ref, valh_ref, gluh_ref, dc_ref, dcn_ref, dz_ref, w_ref, dvg_ref, dw_ref, db_ref,
             vsh, dsh, wacc):
        t = pl.program_id(0)

        @pl.when(t == 0)
        def _():
            wacc[...] = jnp.zeros_like(wacc)
            db_ref[...] = jnp.zeros_like(db_ref)

        val = val_ref[...]
        sg = _sigmoid(glu_ref[...])
        vsh[0, 0:CONV_HALO, :] = valh_ref[...] * _sigmoid(gluh_ref[...]) * (t > 0).astype(F32)
        vsh[0, CONV_HALO:, :] = val * sg
        dcv = dc_ref[...]
        dsh[0, 0:tm, :] = dcv
        dsh[0, tm:, :] = dcn_ref[...] * (t < nt - 1).astype(F32)
        _shifted_copies(vsh, tm)
        _shifted_copies(dsh, tm)
        dv = jnp.zeros((tm, C), F32)
        for k in range(CONV_KERNEL):
            dv = dv + w_ref[k:k + 1, :] * _shifted_rows(dsh, CONV_KERNEL - 1 - k, tm)
            prod = dcv * _shifted_rows(vsh, off + k, tm)
            wacc[k] += jnp.sum(prod.reshape(tm // SUBLANES, SUBLANES, C), axis=0)
        db_ref[...] += jnp.sum(dcv, axis=0, keepdims=True)
        dvg_ref[:, 0:C] = (dv * sg).astype(BF16)
        dvg_ref[:, C:] = (dv * val * sg * (1.0 - sg)).astype(BF16)

        @pl.when(t == nt - 1)
        def _():
            dw_ref[...] = jnp.sum(wacc[...], axis=1)

    cur = lambda col: pl.BlockSpec((tm, C), lambda t: (t, col))
    prev = lambda col: pl.BlockSpec((CONV_HALO, C), lambda t: (jnp.maximum(t * hb - 1, 0), col))
    nxt = pl.BlockSpec((CONV_HALO, C), lambda t: (jnp.minimum((t + 1) * hb, nh - 1), 0))
    row = pl.BlockSpec((tm, C), lambda t: (t, 0))
    return pl.pallas_call(
        body, name="conv_bwd", grid=(nt,),
        in_specs=[cur(Z_VAL), cur(Z_GLU), prev(Z_VAL), prev(Z_GLU), row, nxt, ANY, _full(conv_w.shape)],
        out_specs=[pl.BlockSpec((tm, 2 * C), lambda t: (t, 1)), _full((CONV_HALO, C)), _full((1, C))],
        out_shape=[jax.ShapeDtypeStruct(dz.shape, dz.dtype),
                   jax.ShapeDtypeStruct((CONV_HALO, C), F32), jax.ShapeDtypeStruct((1, C), F32)],
        scratch_shapes=[pltpu.VMEM((8, tm + CONV_HALO, C), F32), pltpu.VMEM((8, tm + CONV_HALO, C), F32),
                        pltpu.VMEM((CONV_HALO, SUBLANES, C), F32)],
        input_output_aliases={6: 0},
        compiler_params=_cp("arbitrary"),
    )(z, z, z, z, dc, dc, dz, conv_w)


def _ln_parts(c):
    mu = jnp.mean(c, axis=-1, keepdims=True)
    cc = c - mu
    rstd = lax.rsqrt(jnp.mean(cc * cc, axis=-1, keepdims=True) + EPS)
    return rstd, cc * rstd


def _ev_tail_branches(ys, c, wglu, bglu, lng, lnb):
    z1 = _gelu(ys)
    z1b = z1.astype(BF16)
    sg = _sigmoid(_dot_rows(z1b, wglu) + bglu)
    out = z1 * sg
    rstd, chat = _ln_parts(c)
    cn = chat * lng + lnb
    return z1, z1b, sg, out, rstd, chat, cn


def ev_tail_fwd(ys, z, c, x0, wglu, bglu, lng, lnb, wout, tm=ROW_TILE):
    L, D = x0.shape
    tm = min(tm, L)
    W = S5_WIDTH

    def body(ys_ref, ga_ref, c_ref, gb_ref, x_ref, wglu_ref, bglu_ref, lng_ref, lnb_ref, wout_ref, o_ref):
        _, _, _, out, _, _, cn = _ev_tail_branches(ys_ref[...], c_ref[...], wglu_ref, bglu_ref[...],
                                                   lng_ref[...], lnb_ref[...])
        ya = (out * _silu(ga_ref[...])).astype(BF16)
        yb = (_silu(cn) * _silu(gb_ref[...])).astype(BF16)
        o_ref[...] = x_ref[...] + _dot_rows(jnp.concatenate([ya, yb], axis=1), wout_ref)

    row = lambda n, col=0: pl.BlockSpec((tm, n), lambda t: (t, col))
    return pl.pallas_call(
        body, name="ev_tail_fwd", grid=(L // tm,),
        in_specs=[row(W), row(W, Z_GA), row(W), row(W, Z_GB), row(D), _full(wglu.shape), _full(bglu.shape),
                  _full(lng.shape), _full(lnb.shape), _full(wout.shape)],
        out_specs=row(D), out_shape=jax.ShapeDtypeStruct((L, D), F32), compiler_params=_cp("parallel"),
    )(ys, z, c, z, x0, wglu, bglu, lng, lnb, wout)


def ev_tail_bwd(ys, z, c, dx1, wglu, bglu, lng, lnb, wout, tm=ROW_TILE):
    L, D = dx1.shape
    tm = min(tm, L)
    W = S5_WIDTH

    def body(ys_ref, ga_ref, c_ref, gb_ref, dx_ref, wglu_ref, bglu_ref, lng_ref, lnb_ref, wout_ref,
             dys_ref, dc_ref, dz_ref, r_ref, z1_ref, dt_ref, dbg_ref, dlg_ref, dlb_ref):
        @pl.when(pl.program_id(0) == 0)
        def _():
            for r in (dbg_ref, dlg_ref, dlb_ref):
                r[...] = jnp.zeros_like(r)

        ys, ga, gb = ys_ref[...], ga_ref[...], gb_ref[...]
        z1, z1b, sg, out, rstd, chat, cn = _ev_tail_branches(ys, c_ref[...], wglu_ref, bglu_ref[...],
                                                             lng_ref[...], lnb_ref[...])
        sga, sgb, scn = _silu(ga), _silu(gb), _silu(cn)
        r_ref[:, 0:W] = (out * sga).astype(BF16)
        r_ref[:, W:] = (scn * sgb).astype(BF16)
        dr = _dot_nt_rows(dx_ref[...].astype(BF16), wout_ref)
        dra, drb = dr[:, 0:W], dr[:, W:]
        dz_ref[...] = jnp.zeros_like(dz_ref)
        dz_ref[:, Z_GA * W:(Z_GA + 1) * W] = (dra * out * _dsilu(ga)).astype(BF16)
        dout = dra * sga
        dt = dout * z1 * sg * (1.0 - sg)
        dtb = dt.astype(BF16)
        dz1 = dout * sg + _dot_nt_rows(dtb, wglu_ref)
        dys_ref[...] = dz1 * _dgelu(ys)
        z1_ref[...] = z1b
        dt_ref[...] = dtb
        dbg_ref[...] += jnp.sum(dt, axis=0, keepdims=True)
        dz_ref[:, Z_GB * W:(Z_GB + 1) * W] = (drb * scn * _dsilu(gb)).astype(BF16)
        dcn = drb * sgb * _dsilu(cn)
        dlg_ref[...] += jnp.sum(dcn * chat, axis=0, keepdims=True)
        dlb_ref[...] += jnp.sum(dcn, axis=0, keepdims=True)
        dch = dcn * lng_ref[...]
        dc_ref[...] = rstd * (dch - jnp.mean(dch, axis=-1, keepdims=True)
                              - chat * jnp.mean(dch * chat, axis=-1, keepdims=True))

    row = lambda n, col=0: pl.BlockSpec((tm, n), lambda t: (t, col))
    f = lambda n, dt: jax.ShapeDtypeStruct((L, n), dt)
    vec = jax.ShapeDtypeStruct((1, W), F32)
    return pl.pallas_call(
        body, name="ev_tail_bwd", grid=(L // tm,),
        in_specs=[row(W), row(W, Z_GA), row(W), row(W, Z_GB), row(D), _full(wglu.shape), _full(bglu.shape),
                  _full(lng.shape), _full(lnb.shape), _full(wout.shape)],
        out_specs=[row(W), row(W), row(EVEN_IN), row(D), row(W), row(W), _full((1, W)), _full((1, W)), _full((1, W))],
        out_shape=[f(W, F32), f(W, F32), f(EVEN_IN, BF16), f(D, BF16), f(W, BF16), f(W, BF16), vec, vec, vec],
        compiler_params=_cp("arbitrary"),
    )(ys, z, c, z, dx1, wglu, bglu, lng, lnb, wout)


XA_SCALE = XA_HEAD_DIM ** -0.5


def _xa_forward(xv, g, wqg, kv):
    D = D_MODEL
    _, xhat = _rms_parts(xv)
    hb = (xhat * g).astype(BF16)
    qb = _dot_cols(hb, wqg, (0, 1)).astype(BF16)
    gate = _dot_cols(hb, wqg, (2, 3))
    ps, os_ = [], []
    for h in range(XA_HEADS):
        lo, hi = h * XA_HEAD_DIM, (h + 1) * XA_HEAD_DIM
        s = _dot_nt(qb[:, lo:hi], kv[:, lo:hi]) * XA_SCALE
        e = jnp.exp(s - jnp.max(s, axis=-1, keepdims=True))
        p = e / jnp.sum(e, axis=-1, keepdims=True)
        ps.append(p)
        os_.append(_dot(p.astype(BF16), kv[:, D + lo:D + hi]))
    return hb, qb, gate, ps, jnp.concatenate(os_, axis=1)


def xa_fwd(x, g, wqg, kv, wo, layer, name, tm=ROW_TILE):
    L, D = x.shape
    tm = min(tm, L)

    def body(x_ref, g_ref, wqg_ref, kv_ref, wo_ref, o_ref):
        xv = x_ref[...]
        _, _, gate, _, o = _xa_forward(xv, g_ref[...], wqg_ref, kv_ref[...])
        o_ref[...] = xv + _dot_rows((o * _silu(gate)).astype(BF16), wo_ref)

    row = pl.BlockSpec((tm, D), lambda t: (t, 0))
    return pl.pallas_call(
        body, name=name, grid=(L // tm,),
        in_specs=[row, _full(g.shape), _wspec(wqg, layer), _full(kv.shape), _wspec(wo, layer)],
        out_specs=row, out_shape=jax.ShapeDtypeStruct((L, D), F32), compiler_params=_cp("parallel"),
    )(x, g, wqg, kv, wo)


def xa_bwd(x, dxo, g, wqg, kv, wo, layer, name, tm=ROW_TILE):
    L, D = x.shape
    tm = min(tm, L)

    def body(x_ref, dxo_ref, g_ref, wqg_ref, kv_ref, wo_ref, dx_ref, dqg_ref, h_ref, r_ref, dkv_ref, dg_ref):
        @pl.when(pl.program_id(0) == 0)
        def _():
            dkv_ref[...] = jnp.zeros_like(dkv_ref)
            dg_ref[...] = jnp.zeros_like(dg_ref)

        xv = x_ref[...]
        kv = kv_ref[...]
        hb, qb, gate, ps, o = _xa_forward(xv, g_ref[...], wqg_ref, kv)
        sgate = _silu(gate)
        h_ref[...] = hb
        r_ref[...] = (o * sgate).astype(BF16)
        dxo = dxo_ref[...]
        dr = _dot_nt_rows(dxo.astype(BF16), wo_ref)
        do = dr * sgate
        dqg_ref[:, D:] = (dr * o * _dsilu(gate)).astype(BF16)
        dob = do.astype(BF16)
        for h in range(XA_HEADS):
            lo, hi = h * XA_HEAD_DIM, (h + 1) * XA_HEAD_DIM
            p = ps[h]
            pb = p.astype(BF16)
            dp = _dot_nt(dob[:, lo:hi], kv[:, D + lo:D + hi])
            dkv_ref[:, D + lo:D + hi] += _dot_tn(pb, dob[:, lo:hi])
            ds = p * (dp - jnp.sum(dp * p, axis=-1, keepdims=True))
            dsb = (ds * XA_SCALE).astype(BF16)
            dqg_ref[:, lo:hi] = _dot(dsb, kv[:, lo:hi]).astype(BF16)
            dkv_ref[:, lo:hi] += _dot_tn(dsb, qb[:, lo:hi])
        dh = _dot_nt_cols(_col_pieces(dqg_ref[...], D // 2), wqg_ref)
        dx, dg = _rms_bwd(xv, g_ref[...], dh)
        dx_ref[...] = dxo + dx
        dg_ref[...] += dg

    row = lambda n: pl.BlockSpec((tm, n), lambda t: (t, 0))
    return pl.pallas_call(
        body, name=name, grid=(L // tm,),
        in_specs=[row(D), row(D), _full(g.shape), _wspec(wqg, layer), _full(kv.shape), _wspec(wo, layer)],
        out_specs=[row(D), row(2 * D), row(D), row(D), _full(kv.shape), _full((1, D))],
        out_shape=[jax.ShapeDtypeStruct((L, D), F32), jax.ShapeDtypeStruct((L, 2 * D), BF16),
                   jax.ShapeDtypeStruct((L, D), BF16), jax.ShapeDtypeStruct((L, D), BF16),
                   jax.ShapeDtypeStruct(kv.shape, F32), jax.ShapeDtypeStruct((1, D), F32)],
        compiler_params=_cp("arbitrary"),
    )(x, dxo, g, wqg, kv, wo)


ATT_SCALE = ATT_HEAD_DIM ** -0.5
ATT_PAIRS = ATT_HEADS // 2
SKEW_LANES = 1024
REL_LANES = 384


def _skew(x, left):
    amt = (ATT_QB - 1) - lax.broadcasted_iota(jnp.int32, (ATT_QB, 1), 0)
    for bit in range(8):
        sh = (SKEW_LANES - (1 << bit)) if left else (1 << bit)
        x = jnp.where(((amt >> bit) & 1) == 1, pltpu.roll(x, sh, 1), x)
    return x


def _dist_onehot(shape, dist_axis):
    j = lax.broadcasted_iota(jnp.int32, shape, dist_axis)
    r = lax.broadcasted_iota(jnp.int32, shape, 1 - dist_axis)
    return (jnp.clip((ATT_WIN - 1) - j, -MAX_REL, MAX_REL) + MAX_REL == r).astype(BF16)


def _dot_exact(v, onehot):
    acc = jnp.zeros((v.shape[0], onehot.shape[1]), F32)
    rem = v
    for _ in range(3):
        part = rem.astype(BF16)
        acc = acc + _dot(part, onehot)
        rem = rem - part.astype(F32)
    return acc


ATT_EDGE = ATT_PAD // ATT_QB


def att_bias(rel_bias):
    H = rel_bias.shape[0]
    rb = jnp.pad(rel_bias, ((0, 0), (0, REL_LANES - rel_bias.shape[1]))).reshape(H, 1, REL_LANES)

    def body(rb_ref, o_ref):
        by_col = _dot_exact(jnp.broadcast_to(rb_ref[...], (8, REL_LANES)), _dist_onehot((REL_LANES, SKEW_LANES), 1))
        x = _skew(jnp.broadcast_to(by_col[0:1, :], (ATT_QB, SKEW_LANES)), left=True)[:, 0:ATT_WIN]
        qc = lax.broadcasted_iota(jnp.int32, (ATT_QB, 1), 0) // CHUNK + LEFT_CHUNKS
        col = lax.broadcasted_iota(jnp.int32, (1, ATT_WIN), 1)
        dc = qc - col // CHUNK
        band = (dc >= 0) & (dc <= LEFT_CHUNKS)
        for blk in range(ATT_EDGE + 1):
            o_ref[blk] = jnp.where(band & (col >= ATT_PAD - blk * ATT_QB), x, NEG)

    return pl.pallas_call(
        body, name="att_bias", grid=(H,),
        in_specs=[pl.BlockSpec((None, 1, REL_LANES), lambda h: (h, 0, 0))],
        out_specs=pl.BlockSpec((ATT_EDGE + 1, None, ATT_QB, ATT_WIN), lambda h: (0, h, 0, 0)),
        out_shape=jax.ShapeDtypeStruct((ATT_EDGE + 1, H, ATT_QB, ATT_WIN), F32), compiler_params=_cp("parallel"),
    )(rb)


def relbias_bwd(dbias):
    H = dbias.shape[0]

    def body(x_ref, o_ref):
        x = jnp.concatenate([x_ref[...], jnp.zeros((ATT_QB, SKEW_LANES - ATT_WIN), F32)], axis=1)
        col = jnp.sum(_skew(x, left=False), axis=0, keepdims=True)
        o_ref[...] = _dot_exact(jnp.broadcast_to(col, (8, SKEW_LANES)), _dist_onehot((SKEW_LANES, REL_LANES), 0))

    out = pl.pallas_call(
        body, name="relbias_bwd", grid=(H,),
        in_specs=[pl.BlockSpec((None, ATT_QB, ATT_WIN), lambda h: (h, 0, 0))],
        out_specs=pl.BlockSpec((None, 8, REL_LANES), lambda h: (h, 0, 0)),
        out_shape=jax.ShapeDtypeStruct((H, 8, REL_LANES), F32), compiler_params=_cp("parallel"),
    )(dbias)
    return out[:, 0, :2 * MAX_REL + 1]


def _ca_scores(qh, kw, bias):
    s = _dot_nt(qh, kw) + bias
    e = jnp.exp(s - jnp.max(s, axis=-1, keepdims=True))
    return e, 1.0 / jnp.sum(e, axis=-1, keepdims=True)


def _ca_head(qv, m):
    return jnp.where(m, qv, jnp.zeros_like(qv)) * ATT_SCALE


def _ca_bias_spec():
    return pl.BlockSpec((None, 2, ATT_QB, ATT_WIN), lambda hp, b: (jnp.minimum(b, ATT_EDGE), hp, 0, 0))


def ca_fwd(q, kvp, gate, bias):
    L, D = q.shape
    Lp = kvp.shape[0]
    nb = L // ATT_QB

    def body(q_ref, k_ref, v_ref, g_ref, b_ref, r_ref):
        w = pl.multiple_of(pl.program_id(1) * ATT_QB, ATT_QB)
        kw = k_ref[pl.ds(w, ATT_WIN), :]
        vw = v_ref[pl.ds(w, ATT_WIN), :]
        qv = q_ref[...]
        first = lax.broadcasted_iota(jnp.int32, (1, 128), 1) < ATT_HEAD_DIM
        outs = []
        for hh, m in enumerate((first, jnp.logical_not(first))):
            e, inv = _ca_scores(_ca_head(qv, m), kw, b_ref[hh])
            outs.append(_dot(e.astype(BF16), vw) * inv)
        o = jnp.where(first, outs[0], outs[1])
        r_ref[...] = (o * _silu(g_ref[...])).astype(BF16)

    blk = pl.BlockSpec((ATT_QB, 128), lambda hp, b: (b, hp))
    return pl.pallas_call(
        body, name="ca_fwd", grid=(ATT_PAIRS, nb),
        in_specs=[blk, pl.BlockSpec((Lp, 128), lambda hp, b: (0, hp)),
                  pl.BlockSpec((Lp, 128), lambda hp, b: (0, ATT_PAIRS + hp)), blk, _ca_bias_spec()],
        out_specs=blk, out_shape=jax.ShapeDtypeStruct((L, D), BF16),
        compiler_params=_cp("parallel", "arbitrary"),
    )(q, kvp, kvp, gate, bias)


def ca_bwd(q, kvp, gate, bias, dr):
    L, D = q.shape
    Lp = kvp.shape[0]
    nb = L // ATT_QB

    def body(q_ref, k_ref, v_ref, g_ref, b_ref, dr_ref, dq_ref, dg_ref, dk_ref, dv_ref, db_ref):
        b = pl.program_id(1)

        @pl.when(b == 0)
        def _():
            for r in (dk_ref, dv_ref, db_ref):
                r[...] = jnp.zeros_like(r)

        w = pl.multiple_of(b * ATT_QB, ATT_QB)
        kw = k_ref[pl.ds(w, ATT_WIN), :]
        vw = v_ref[pl.ds(w, ATT_WIN), :]
        qv = q_ref[...]
        gate_v = g_ref[...]
        drv = dr_ref[...]
        do = drv * _silu(gate_v)
        first = lax.broadcasted_iota(jnp.int32, (1, 128), 1) < ATT_HEAD_DIM
        outs, dqs = [], []
        dkw = jnp.zeros((ATT_WIN, 128), F32)
        dvw = jnp.zeros((ATT_WIN, 128), F32)
        for hh, m in enumerate((first, jnp.logical_not(first))):
            qh = _ca_head(qv, m)
            e, inv = _ca_scores(qh, kw, b_ref[hh])
            eb = e.astype(BF16)
            outs.append(_dot(eb, vw) * inv)
            doh = jnp.where(m, do, 0.0)
            dp = _dot_nt(doh.astype(BF16), vw)
            dvw = dvw + _dot_tn(eb, (doh * inv).astype(BF16))
            rs = jnp.sum(dp * e, axis=-1, keepdims=True) * inv
            ds = e * ((dp - rs) * inv)
            db_ref[hh] += ds
            dsb = ds.astype(BF16)
            dqs.append(_dot(dsb, kw))
            dkw = dkw + _dot_tn(dsb, qh)
        o = jnp.where(first, outs[0], outs[1])
        dg_ref[...] = (drv * o * _dsilu(gate_v)).astype(BF16)
        dq_ref[...] = (jnp.where(first, dqs[0], dqs[1]) * ATT_SCALE).astype(BF16)
        dk_ref[pl.ds(w, ATT_WIN), :] += dkw
        dv_ref[pl.ds(w, ATT_WIN), :] += dvw

    blk = pl.BlockSpec((ATT_QB, 128), lambda hp, b: (b, hp))
    kblk = pl.BlockSpec((Lp, 128), lambda hp, b: (0, hp))
    vblk = pl.BlockSpec((Lp, 128), lambda hp, b: (0, ATT_PAIRS + hp))
    bblk = pl.BlockSpec((2, ATT_QB, ATT_WIN), lambda hp, b: (hp, 0, 0))
    return pl.pallas_call(
        body, name="ca_bwd", grid=(ATT_PAIRS, nb),
        in_specs=[blk, kblk, vblk, blk, _ca_bias_spec(), blk],
        out_specs=[blk, blk, kblk, kblk, bblk],
        out_shape=[jax.ShapeDtypeStruct((L, D), BF16), jax.ShapeDtypeStruct((L, D), BF16),
                   jax.ShapeDtypeStruct((Lp, D), F32), jax.ShapeDtypeStruct((Lp, D), F32),
                   jax.ShapeDtypeStruct(bias.shape[1:], F32)],
        compiler_params=_cp("parallel", "arbitrary"),
    )(q, kvp, kvp, gate, bias, dr)


def loss_bwd(x, target, g, tm=ROW_TILE):
    L, D = x.shape
    tm = min(tm, L)

    def body(x_ref, t_ref, g_ref, loss_ref, dx_ref, dg_ref):
        @pl.when(pl.program_id(0) == 0)
        def _():
            loss_ref[...] = jnp.zeros_like(loss_ref)
            dg_ref[...] = jnp.zeros_like(dg_ref)

        xv = x_ref[...]
        gv = g_ref[...]
        _, xhat = _rms_parts(xv)
        err = xhat * gv - t_ref[...]
        loss_ref[...] += 0.5 * jnp.sum(jnp.sum(err * err, axis=-1, keepdims=True), axis=0, keepdims=True) / D
        dx, dg = _rms_bwd(xv, gv, err / D)
        dx_ref[...] = dx
        dg_ref[...] += dg

    row = pl.BlockSpec((tm, D), lambda t: (t, 0))
    return pl.pallas_call(
        body, name="loss_bwd", grid=(L // tm,),
        in_specs=[row, row, _full(g.shape)],
        out_specs=[_full((1, 128)), row, _full((1, D))],
        out_shape=[jax.ShapeDtypeStruct((1, 128), F32), jax.ShapeDtypeStruct((L, D), F32),
                   jax.ShapeDtypeStruct((1, D), F32)],
        compiler_params=_cp("arbitrary"),
    )(x, target, g)


_ADAM_C1 = 1.0 / (1.0 - ADAM_B1 ** ADAM_STEP)
_ADAM_C2 = 1.0 / (1.0 - ADAM_B2 ** ADAM_STEP)


def _adam_update(w, g, m, v):
    mn = ADAM_B1 * m + (1.0 - ADAM_B1) * g
    vn = ADAM_B2 * v + (1.0 - ADAM_B2) * g * g
    delta = -ADAM_LR * ((mn * _ADAM_C1) / (jnp.sqrt(vn * _ADAM_C2) + ADAM_EPS) + ADAM_WD * w)
    return delta, mn, vn


def adamw(w, g, m, v, name, tr=512):
    R, C = w.shape
    tr = min(tr, R)

    def body(w_ref, g_ref, m_ref, v_ref, d_ref, mo_ref, vo_ref):
        d_ref[...], mo_ref[...], vo_ref[...] = _adam_update(w_ref[...], g_ref[...], m_ref[...], v_ref[...])

    blk = pl.BlockSpec((tr, C), lambda i: (i, 0))
    sh = jax.ShapeDtypeStruct((R, C), F32)
    return pl.pallas_call(
        body, name=name, grid=(R // tr,), in_specs=[blk] * 4, out_specs=[blk] * 3,
        out_shape=[sh] * 3, compiler_params=_cp("parallel"),
    )(w, g, m, v)


def adamw_allreduce(gathered, w, m, v, shard, name, slot=None):
    R, C = w.shape
    sharded = slot is None and gathered.shape[2] != C

    def body(s_ref, ga_ref, w_ref, m_ref, v_ref, g_ref, d_ref, mo_ref, vo_ref):
        take = (lambda d: ga_ref[d]) if slot is None else (lambda d: ga_ref[d, slot:slot + R, 0:C])
        g = take(0)
        for d in range(1, N_DEV):
            g = g + take(d)
        g_ref[...] = g
        d_ref[...], mo_ref[...], vo_ref[...] = _adam_update(w_ref[...], g, m_ref[...], v_ref[...])

    blk = pl.BlockSpec((R, C), lambda i, s_ref: (0, 0))
    if slot is not None:
        gblk = pl.BlockSpec(gathered.shape, lambda i, s_ref: (0, 0, 0))
    else:
        gblk = pl.BlockSpec((N_DEV, R, C),
                            (lambda i, s_ref: (0, 0, s_ref[0])) if sharded else (lambda i, s_ref: (0, 0, 0)))
    sh = jax.ShapeDtypeStruct((R, C), F32)
    return pl.pallas_call(
        body, name=name,
        grid_spec=pltpu.PrefetchScalarGridSpec(num_scalar_prefetch=1, grid=(1,), in_specs=[gblk, blk, blk, blk],
                                               out_specs=[blk] * 4),
        out_shape=[sh] * 4, compiler_params=_cp("arbitrary"),
    )(shard, gathered, w, m, v)


LATE = ("od_w_in", "od_w_out", "xa_w_qg", "xa_w_kv", "xa_w_o")
EARLY_GRADS = ("od_w_in", "od_w_out", "xa_w_qg", "xa_w_kv", "xa_w_o", "ev_w_out", "ev_s5_glu_w")


def _reduce_to_chip(gs, names, core, tag):
    from_sibling = sibling_send_other_half(gs, "sibling_send_" + tag)
    return [sum_with_sibling(gi, ri, core, "sum_sibling_" + n) for n, gi, ri in zip(names, gs, from_sibling)]


def local_step(x, mem, target, p, gw, late, place, core):
    row = lambda a: a.reshape(1, -1)
    D = D_MODEL
    L = x.shape[0]
    g, big = {}, {}
    gw = dict(gw)

    z, h0b = norm_mm(x, p["ev_norm_g"], gw["ev_w_in"], [((0, 1, 2, 3), F32, 0)], "ev_in")
    ys, s5_saved, landed = s5_mixer_core_fwd(
        z, p["ev_s5_lambda_re"][0], p["ev_s5_lambda_im"][0], p["ev_s5_log_dt"][0], p["ev_s5_b_re"][0],
        p["ev_s5_b_im"][0], p["ev_s5_c_re"][0], p["ev_s5_c_im"][0], p["ev_s5_d"][0],
        carried=carried_allgather([late[n] for n in LATE]))
    for n, gth in zip(LATE, landed):
        rows = gth.shape[1]
        gw[n] = gth.reshape(N_CHIPS, 2, rows // 2, gth.shape[2]) if n.startswith("xa_") else gth
    memn_b = rms_fwd(mem, row(p["mem_norm_g"]), "mem_norm")
    kvs = [mm_cols(memn_b, gw["xa_w_kv"], l, f"xa_kv{l}", BF16) for l in range(2)]
    conv_w = p["ev_conv_w"][0]
    c = conv_fwd(z, conv_w, p["ev_conv_b"])
    tail = (gw["ev_s5_glu_w"], p["ev_s5_glu_b"], p["ev_conv_ln_g"], p["ev_conv_ln_b"], gw["ev_w_out"])
    x1 = ev_tail_fwd(ys, z, c, x, *tail)
    xa0 = (row(p["xa_norm_g"][0]), gw["xa_w_qg"], kvs[0], gw["xa_w_o"], 0)
    x2 = xa_fwd(x1, *xa0, "xa_fwd0")

    q, kvp, gate, h1b = norm_mm(x2, p["od_norm_g"], gw["od_w_in"],
                                [((0,), BF16, 0), ((1, 2), BF16, ATT_PAD), ((3,), F32, 0)], "od_in")
    kvp = zero_rows(kvp, ATT_PAD, "od_kv_pad")
    bias = att_bias(p["od_rel_bias"][0])
    r = ca_fwd(q, kvp, gate, bias)
    x3 = mm_res(r, gw["od_w_out"], x2, "od_out")
    xa1 = (row(p["xa_norm_g"][1]), gw["xa_w_qg"], kvs[1], gw["xa_w_o"], 1)
    x4 = xa_fwd(x3, *xa1, "xa_fwd1")

    loss, dx4, dgf = loss_bwd(x4, target, row(p["final_norm_g"]))
    g["final_norm_g"] = dgf.reshape(D)

    dx3, dqg1, hx1, rx1, dkv1, dgxa1 = xa_bwd(x3, dx4, *xa1, "xa_bwd1")
    dwqg = mm_tn(hx1, dqg1, "xa_dwqg1", ("cols", 1))
    dwo = mm_tn(rx1, dx4, "xa_dwo1", ("rows", 1))

    big["od_w_out"] = mm_tn(r, dx3, "od_dwout", ("rows",))
    dr = mm_nt_rows(dx3, gw["od_w_out"], "od_out_bwd")
    dq, dgate, dkp, dvp, dbias = ca_bwd(q, kvp, gate, bias, dr)
    pieces, offs = (dq, dkp, dvp, dgate), (0, ATT_PAD, ATT_PAD, 0)
    dwin = None
    for s in range(N_CHIPS):
        dwin = mm_tn(h1b, pieces[s], f"od_dwin{s}", ("slab", s), into=dwin, b_off=offs[s], bl=ATT_PAD)
    big["od_w_in"] = dwin
    dx2, dgod = mm_nt_normbwd(pieces, offs, gw["od_w_in"], x2, p["od_norm_g"], dx3, "od_in_bwd")
    g["od_norm_g"] = dgod
    g["od_rel_bias"] = relbias_bwd(dbias)[None]

    dx1, dqg0, hx0, rx0, dkv0, dgxa0 = xa_bwd(x1, dx2, *xa0, "xa_bwd0")
    big["xa_w_qg"] = mm_tn(hx0, dqg0, "xa_dwqg0", ("cols", 0), into=dwqg)
    big["xa_w_o"] = mm_tn(rx0, dx2, "xa_dwo0", ("rows", 0), into=dwo)
    g["xa_norm_g"] = jnp.concatenate([dgxa0, dgxa1], axis=0)

    dys, dc, dz, ra, z1b, dtb, dbglu, dlng, dlnb = ev_tail_bwd(ys, z, c, dx1, *tail)
    big["ev_w_out"] = mm_tn(ra, dx1, "ev_dwout", ("rows",))
    big["ev_s5_glu_w"] = mm_tn(z1b, dtb, "ev_dwglu", ("rows",))
    g["ev_s5_glu_b"], g["ev_conv_ln_g"], g["ev_conv_ln_b"] = dbglu, dlng, dlnb
    dz, dconvw, dconvb = conv_bwd(z, dc, dz, conv_w)
    g["ev_conv_w"] = dconvw[None, :CONV_KERNEL]
    g["ev_conv_b"] = dconvb

    dwkv = mm_tn(memn_b, dkv1, "xa_dwkv1", ("cols", 1), bl=MEM_LEN)
    big["xa_w_kv"] = mm_tn(memn_b, dkv0, "xa_dwkv0", ("cols", 0), into=dwkv, bl=MEM_LEN)
    dmem0 = mm_nt_cols(dkv0, gw["xa_w_kv"], 0, "xa_kv_bwd0")
    dmem1 = mm_nt_cols(dkv1, gw["xa_w_kv"], 1, "xa_kv_bwd1")
    g["mem_norm_g"] = rms_dgain(mem, dmem0, dmem1, "mem_norm_bwd").reshape(D)

    shard_major = lambda t: t.reshape((-1,) + t.shape[-2:])
    chip_sums = _reduce_to_chip([shard_major(big[n]) for n in EARLY_GRADS], EARLY_GRADS, core, "early")
    dz, s5g, from_chips = s5_mixer_core_bwd(z, dys, dz, p["ev_s5_lambda_re"][0], p["ev_s5_lambda_im"][0], s5_saved,
                                            carried=carried_chips_exchange(chip_sums))
    reduced = {n: sum_chips(ci, ri, place, "sum_chips_" + n) for n, ci, ri in zip(EARLY_GRADS, chip_sums, from_chips)}
    for n, v in s5g.items():
        g["ev_s5_" + n] = v[None]
    dwin_ev = mm_tn(h0b, dz, "ev_dwin", ("cols",))
    grad_x, dgev = mm_nt_normbwd((dz,), (0,), gw["ev_w_in"], x, p["ev_norm_g"], dx1, "ev_in_bwd")
    g["ev_norm_g"] = dgev
    chip_sum = _reduce_to_chip([dwin_ev], ["ev_w_in"], core, "last")
    reduced["ev_w_in"] = sum_chips(chip_sum[0], chips_exchange(chip_sum)[0], place, "sum_chips_ev_w_in")
    return loss, grad_x, g, reduced


def _me():
    return lax.axis_index("x"), lax.axis_index("y"), lax.axis_index("c")


def _other_chips(x, y):
    return [(1 - x, y), (x, 1 - y), (1 - x, 1 - y)]


def _remote(src, dst, send_sems, recv_sems, k, to):
    return pltpu.make_async_remote_copy(src_ref=src, dst_ref=dst, send_sem=send_sems.at[k], recv_sem=recv_sems.at[k],
                                        device_id=to, device_id_type=MESH)


def _rows_half(ref, h):
    H = ref.shape[-2] // 2
    return ref.at[(slice(None),) * (len(ref.shape) - 2) + (pl.ds(h * H, H), slice(None))]


def allgather_chip_blocks(halved, whole):
    nh, nw = len(halved), len(whole)
    n = nh + nw

    def body(*refs):
        ins, outs = refs[:n], refs[n:2 * n]
        send_sems, recv_sems, local_sems = refs[2 * n:]
        x, y, c = _me()
        sib = (x, y, 1 - c)
        chips = _other_chips(x, y)
        me = 2 * x + y
        local = [pltpu.make_async_copy(ins[i], outs[i].at[me], local_sems.at[i]) for i in range(n)]
        for cp in local:
            cp.start()
        first, passed = [], []
        for i in range(n):
            for j, (cx, cy) in enumerate(chips):
                if i < nh:
                    src, dst = _rows_half(ins[i], c), _rows_half(outs[i].at[me], c)
                    k = 6 * i + j
                else:
                    src, dst = ins[i], outs[i].at[me]
                    k = 6 * nh + 3 * (i - nh) + j
                first.append(_remote(src, dst, send_sems, recv_sems, k, (cx, cy, c)))
        for cp in first:
            cp.start()
        for j, (cx, cy) in enumerate(chips):
            for i in range(nh):
                got = _rows_half(outs[i].at[2 * cx + cy], c)
                _remote(got, got, send_sems, recv_sems, 6 * i + j, (cx, cy, c)).wait_recv()
                fw = _remote(got, got, send_sems, recv_sems, 6 * i + 3 + j, sib)
                fw.start()
                passed.append(fw)
        for j, (cx, cy) in enumerate(chips):
            for i in range(nh):
                got = _rows_half(outs[i].at[2 * cx + cy], 1 - c)
                _remote(got, got, send_sems, recv_sems, 6 * i + 3 + j, sib).wait_recv()
            for i in range(nh, n):
                got = outs[i].at[2 * cx + cy]
                _remote(got, got, send_sems, recv_sems, 6 * nh + 3 * (i - nh) + j, (cx, cy, c)).wait_recv()
        for cp in first + passed:
            cp.wait_send()
        for cp in local:
            cp.wait()

    arrays = list(halved) + list(whole)
    nsem = 6 * nh + 3 * nw
    return pl.pallas_call(
        body, name="allgather_chip_blocks", in_specs=[ANY] * n, out_specs=[ANY] * n,
        out_shape=[jax.ShapeDtypeStruct((N_CHIPS,) + a.shape, a.dtype) for a in arrays],
        scratch_shapes=[pltpu.SemaphoreType.DMA((nsem,)), pltpu.SemaphoreType.DMA((nsem,)),
                        pltpu.SemaphoreType.DMA((n,))],
    )(*arrays)


def allgather_devices(vs):
    n = len(vs)

    def body(*refs):
        ins, outs = refs[:n], refs[n:2 * n]
        send_sems, recv_sems, local_sems = refs[2 * n:]
        x, y, c = _me()
        sib = (x, y, 1 - c)
        chips = _other_chips(x, y)
        me = 4 * x + 2 * y + c
        local = [pltpu.make_async_copy(ins[i], outs[i].at[me], local_sems.at[i]) for i in range(n)]
        for cp in local:
            cp.start()
        first, passed = [], []
        for i in range(n):
            first.append(_remote(ins[i], outs[i].at[me], send_sems, recv_sems, 7 * i, sib))
            for j, (cx, cy) in enumerate(chips):
                first.append(_remote(ins[i], outs[i].at[me], send_sems, recv_sems, 7 * i + 1 + j, (cx, cy, c)))
        for cp in first:
            cp.start()
        for j, (cx, cy) in enumerate(chips):
            for i in range(n):
                got = outs[i].at[4 * cx + 2 * cy + c]
                _remote(got, got, send_sems, recv_sems, 7 * i + 1 + j, (cx, cy, c)).wait_recv()
                fw = _remote(got, got, send_sems, recv_sems, 7 * i + 4 + j, sib)
                fw.start()
                passed.append(fw)
        for i in range(n):
            got = outs[i].at[4 * x + 2 * y + (1 - c)]
            _remote(got, got, send_sems, recv_sems, 7 * i, sib).wait_recv()
            for j, (cx, cy) in enumerate(chips):
                got = outs[i].at[4 * cx + 2 * cy + (1 - c)]
                _remote(got, got, send_sems, recv_sems, 7 * i + 4 + j, sib).wait_recv()
        for cp in first + passed:
            cp.wait_send()
        for cp in local:
            cp.wait()

    return pl.pallas_call(
        body, name="allgather_devices", in_specs=[ANY] * n, out_specs=[ANY] * n,
        out_shape=[jax.ShapeDtypeStruct((N_DEV,) + v.shape, v.dtype) for v in vs],
        scratch_shapes=[pltpu.SemaphoreType.DMA((7 * n,)), pltpu.SemaphoreType.DMA((7 * n,)),
                        pltpu.SemaphoreType.DMA((n,))],
    )(*vs)


def sibling_send_other_half(gs, name):
    n = len(gs)

    def body(*refs):
        ins, outs = refs[:n], refs[n:2 * n]
        send_sems, recv_sems = refs[2 * n:]
        x, y, c = _me()
        cps = [_remote(_rows_half(ins[i], 1 - c), outs[i], send_sems, recv_sems, i, (x, y, 1 - c)) for i in range(n)]
        for cp in cps:
            cp.start()
        for cp in cps:
            cp.wait()

    return pl.pallas_call(
        body, name=name, in_specs=[ANY] * n, out_specs=[ANY] * n,
        out_shape=[jax.ShapeDtypeStruct((g.shape[0], g.shape[1] // 2, g.shape[2]), g.dtype) for g in gs],
        scratch_shapes=[pltpu.SemaphoreType.DMA((n,)), pltpu.SemaphoreType.DMA((n,))],
    )(*gs)


def chips_exchange(parts):
    n = len(parts)

    def body(*refs):
        ins, outs = refs[:n], refs[n:2 * n]
        send_sems, recv_sems = refs[2 * n:]
        x, y, c = _me()
        cps = []
        for i in range(n):
            nl = ins[i].shape[0] // N_CHIPS
            for j, (cx, cy) in enumerate(_other_chips(x, y)):
                cps.append(_remote(ins[i].at[pl.ds((2 * cx + cy) * nl, nl)], outs[i].at[j], send_sems, recv_sems,
                                   3 * i + j, (cx, cy, c)))
        for cp in cps:
            cp.start()
        for cp in cps:
            cp.wait()

    return pl.pallas_call(
        body, name="chips_exchange", in_specs=[ANY] * n, out_specs=[ANY] * n,
        out_shape=[jax.ShapeDtypeStruct((3, a.shape[0] // N_CHIPS) + a.shape[1:], a.dtype) for a in parts],
        scratch_shapes=[pltpu.SemaphoreType.DMA((3 * n,)), pltpu.SemaphoreType.DMA((3 * n,))],
    )(*parts)


def sibling_share(fulls):
    n = len(fulls)

    def body(*refs):
        outs = refs[n:2 * n]
        send_sems, recv_sems = refs[2 * n:]
        x, y, c = _me()
        cps = [_remote(_rows_half(outs[i], c), _rows_half(outs[i], c), send_sems, recv_sems, i, (x, y, 1 - c))
               for i in range(n)]
        for cp in cps:
            cp.start()
        for i in range(n):
            got = _rows_half(outs[i], 1 - c)
            _remote(got, got, send_sems, recv_sems, i, (x, y, 1 - c)).wait_recv()
        for cp in cps:
            cp.wait_send()

    return pl.pallas_call(
        body, name="sibling_share", in_specs=[ANY] * n, out_specs=[ANY] * n,
        out_shape=[jax.ShapeDtypeStruct(f.shape, f.dtype) for f in fulls],
        input_output_aliases={i: i for i in range(n)},
        scratch_shapes=[pltpu.SemaphoreType.DMA((n,)), pltpu.SemaphoreType.DMA((n,))],
    )(*fulls)


def sum_with_sibling(g, recv, core, name):
    S, H, C = recv.shape
    tr = min(512, H)

    def body(c_ref, g_ref, r_ref, o_ref):
        o_ref[...] = (g_ref[...].astype(F32) + r_ref[...].astype(F32)).astype(o_ref.dtype)

    nb = H // tr
    return pl.pallas_call(
        body, name=name,
        grid_spec=pltpu.PrefetchScalarGridSpec(
            num_scalar_prefetch=1, grid=(S, nb),
            in_specs=[pl.BlockSpec((None, tr, C), lambda s, i, c_ref: (s, c_ref[0] * nb + i, 0)),
                      pl.BlockSpec((None, tr, C), lambda s, i, c_ref: (s, i, 0))],
            out_specs=pl.BlockSpec((None, tr, C), lambda s, i, c_ref: (s, i, 0))),
        out_shape=jax.ShapeDtypeStruct((S, H, C), g.dtype), compiler_params=_cp("parallel", "parallel"),
    )(core, g, recv)


def sum_chips(a, recv, place, name):
    _, nl, H, C = recv.shape
    tr = min(512, H)
    nb = H // tr

    def body(p_ref, a_ref, r_ref, o_ref):
        acc = a_ref[...].astype(F32)
        for j in range(3):
            acc = acc + r_ref[j].astype(F32)
        o_ref[...] = acc

    return pl.pallas_call(
        body, name=name,
        grid_spec=pltpu.PrefetchScalarGridSpec(
            num_scalar_prefetch=1, grid=(nl, nb),
            in_specs=[pl.BlockSpec((None, tr, C), lambda l, i, p_ref: (p_ref[0] * nl + l, i, 0)),
                      pl.BlockSpec((3, None, tr, C), lambda l, i, p_ref: (0, l, i, 0))],
            out_specs=pl.BlockSpec((None, tr, C), lambda l, i, p_ref: (l, p_ref[1] * nb + i, 0))),
        out_shape=jax.ShapeDtypeStruct((nl, 2 * H, C), F32), compiler_params=_cp("parallel", "parallel"),
    )(place, a, recv)


def pack_rows(arrays, name):
    starts, r0 = [], 0
    for a in arrays:
        if a.shape[0] >= SUBLANES:
            r0 = -(-r0 // SUBLANES) * SUBLANES
        starts.append(r0)
        r0 += a.shape[0]
    r0 = -(-r0 // SUBLANES) * SUBLANES
    n = len(arrays)

    def body(*refs):
        o_ref = refs[n]
        o_ref[...] = jnp.zeros_like(o_ref)
        for a_ref, s in zip(refs[:n], starts):
            r, c = a_ref.shape
            o_ref[s:s + r, 0:c] = a_ref[...]

    out = pl.pallas_call(body, name=name, out_shape=jax.ShapeDtypeStruct((r0, PACK_COLS), F32))(*arrays)
    return out, starts


def sum_slot(gathered, slot, shape, name):
    r, c = shape

    def body(ga_ref, o_ref):
        acc = ga_ref[0, slot:slot + r, 0:c]
        for d in range(1, N_DEV):
            acc = acc + ga_ref[d, slot:slot + r, 0:c]
        o_ref[...] = acc

    return pl.pallas_call(body, name=name, out_shape=jax.ShapeDtypeStruct((r, c), F32))(gathered)


def carried_allgather(blocks):
    n = len(blocks)

    def first_hop(ins, outs, sems, i, j, chip, x, y, c):
        me = 2 * x + y
        return _remote(_rows_half(ins[i], c), _rows_half(outs[i].at[me], c), sems[0], sems[1], 6 * i + j, (*chip, c))

    def start(ins, outs, sems):
        x, y, c = _me()
        for i in range(n):
            pltpu.make_async_copy(ins[i], outs[i].at[2 * x + y], sems[2].at[i]).start()
        for i in range(n):
            for j, chip in enumerate(_other_chips(x, y)):
                first_hop(ins, outs, sems, i, j, chip, x, y, c).start()

    def finish(ins, outs, sems):
        x, y, c = _me()
        sib = (x, y, 1 - c)
        chips = _other_chips(x, y)
        passed = []
        for j, (cx, cy) in enumerate(chips):
            for i in range(n):
                got = _rows_half(outs[i].at[2 * cx + cy], c)
                _remote(got, got, sems[0], sems[1], 6 * i + j, (cx, cy, c)).wait_recv()
                fw = _remote(got, got, sems[0], sems[1], 6 * i + 3 + j, sib)
                fw.start()
                passed.append(fw)
        for j, (cx, cy) in enumerate(chips):
            for i in range(n):
                got = _rows_half(outs[i].at[2 * cx + cy], 1 - c)
                _remote(got, got, sems[0], sems[1], 6 * i + 3 + j, sib).wait_recv()
        for i in range(n):
            for j, chip in enumerate(chips):
                first_hop(ins, outs, sems, i, j, chip, x, y, c).wait_send()
        for fw in passed:
            fw.wait_send()
        for i in range(n):
            pltpu.make_async_copy(ins[i], outs[i].at[2 * x + y], sems[2].at[i]).wait()

    return Carried(blocks, [jax.ShapeDtypeStruct((N_CHIPS,) + b.shape, b.dtype) for b in blocks],
                   [pltpu.SemaphoreType.DMA((6 * n,)), pltpu.SemaphoreType.DMA((6 * n,)), pltpu.SemaphoreType.DMA((n,))],
                   start, None, finish)


def carried_chips_exchange(parts):
    n = len(parts)

    def copies(ins, outs, sems):
        x, y, c = _me()
        cps = []
        for i in range(n):
            nl = ins[i].shape[0] // N_CHIPS
            for j, (cx, cy) in enumerate(_other_chips(x, y)):
                cps.append(_remote(ins[i].at[pl.ds((2 * cx + cy) * nl, nl)], outs[i].at[j], sems[0], sems[1],
                                   3 * i + j, (cx, cy, c)))
        return cps

    def start(ins, outs, sems):
        for cp in copies(ins, outs, sems):
            cp.start()

    def finish(ins, outs, sems):
        for cp in copies(ins, outs, sems):
            cp.wait()

    return Carried(parts, [jax.ShapeDtypeStruct((3, a.shape[0] // N_CHIPS) + a.shape[1:], a.dtype) for a in parts],
                   [pltpu.SemaphoreType.DMA((3 * n,)), pltpu.SemaphoreType.DMA((3 * n,))], start, None, finish)


BIG = ("ev_w_in", "ev_s5_glu_w", "ev_w_out", "od_w_in", "od_w_out", "xa_w_qg", "xa_w_kv", "xa_w_o")
SHARDED_F32 = (("ev_conv_w", 2), ("od_norm_g", 1))
SMALL = ("mem_norm_g", "ev_norm_g", "ev_s5_lambda_re", "ev_s5_lambda_im", "ev_s5_log_dt", "ev_s5_b_re", "ev_s5_b_im",
         "ev_s5_c_re", "ev_s5_c_im", "ev_s5_d", "ev_s5_glu_b", "ev_conv_b", "ev_conv_ln_g", "ev_conv_ln_b",
         "od_rel_bias", "xa_norm_g", "final_norm_g")
NARROW = ("ev_s5_c_re", "ev_s5_c_im")
DENSE_B = ("ev_s5_b_re", "ev_s5_b_im")
PACK_COLS = 1024
WEIGHTS = ("mem_norm_g", "ev_norm_g", "ev_w_in", "ev_s5_lambda_re", "ev_s5_lambda_im", "ev_s5_log_dt", "ev_s5_b_re",
           "ev_s5_b_im", "ev_s5_c_re", "ev_s5_c_im", "ev_s5_d", "ev_s5_glu_w", "ev_s5_glu_b", "ev_conv_w", "ev_conv_b",
           "ev_conv_ln_g", "ev_conv_ln_b", "ev_w_out", "od_norm_g", "od_w_in", "od_rel_bias", "od_w_out", "xa_norm_g",
           "xa_w_qg", "xa_w_kv", "xa_w_o", "final_norm_g")


def _as2d(a):
    return a.reshape(1, -1) if a.ndim == 1 else a.reshape(-1, a.shape[-1])


def kernel(x, mem, mem_norm_g, ev_norm_g, ev_w_in, ev_s5_lambda_re, ev_s5_lambda_im, ev_s5_log_dt, ev_s5_b_re, ev_s5_b_im, ev_s5_c_re, ev_s5_c_im, ev_s5_d, ev_s5_glu_w, ev_s5_glu_b, ev_conv_w, ev_conv_b, ev_conv_ln_g, ev_conv_ln_b, ev_w_out, od_norm_g, od_w_in, od_rel_bias, od_w_out, xa_norm_g, xa_w_qg, xa_w_kv, xa_w_o, final_norm_g, loss_target, m_mem_norm_g, m_ev_norm_g, m_ev_w_in, m_ev_s5_lambda_re, m_ev_s5_lambda_im, m_ev_s5_log_dt, m_ev_s5_b_re, m_ev_s5_b_im, m_ev_s5_c_re, m_ev_s5_c_im, m_ev_s5_d, m_ev_s5_glu_w, m_ev_s5_glu_b, m_ev_conv_w, m_ev_conv_b, m_ev_conv_ln_g, m_ev_conv_ln_b, m_ev_w_out, m_od_norm_g, m_od_w_in, m_od_rel_bias, m_od_w_out, m_xa_norm_g, m_xa_w_qg, m_xa_w_kv, m_xa_w_o, m_final_norm_g, v_mem_norm_g, v_ev_norm_g, v_ev_w_in, v_ev_s5_lambda_re, v_ev_s5_lambda_im, v_ev_s5_log_dt, v_ev_s5_b_re, v_ev_s5_b_im, v_ev_s5_c_re, v_ev_s5_c_im, v_ev_s5_d, v_ev_s5_glu_w, v_ev_s5_glu_b, v_ev_conv_w, v_ev_conv_b, v_ev_conv_ln_g, v_ev_conv_ln_b, v_ev_w_out, v_od_norm_g, v_od_w_in, v_od_rel_bias, v_od_w_out, v_xa_norm_g, v_xa_w_qg, v_xa_w_kv, v_xa_w_o, v_final_norm_g):
    a = dict(locals())
    w = {n: a[n] for n in WEIGHTS}
    shard = (2 * lax.axis_index("x") + lax.axis_index("y")).reshape(1).astype(jnp.int32)
    core = lax.axis_index("c").reshape(1).astype(jnp.int32)

    place = jnp.concatenate([shard, core])

    blocks = {n: w[n].astype(BF16).reshape(-1, w[n].shape[-1]) for n in BIG}
    early = [n for n in BIG if n not in LATE]
    gathered = allgather_chip_blocks([blocks[n] for n in early], [_as2d(w[n]) for n, _ in SHARDED_F32])
    gw = dict(zip(early, gathered))
    p = {n: w[n] for n in SMALL}
    conv_g, odn_g = gathered[len(early):]
    p["ev_conv_w"] = jnp.concatenate([conv_g[s] for s in range(N_CHIPS)], axis=1)[None]
    p["od_norm_g"] = odn_g.reshape(1, D_MODEL)

    loss, grad_x, g, reduced = local_step(x[0], mem[0], loss_target[0], p, gw, {n: blocks[n] for n in LATE},
                                          place, core)
    loss = lax.psum(loss[0, 0], ("x", "y", "c"))
    g_big = dict(zip(BIG, sibling_share([reduced[n] for n in BIG])))

    out = {tag: {} for tag in ("grad", "delta", "m", "v")}
    for n in BIG:
        sh = w[n].shape
        to2d = lambda t: t.reshape(-1, sh[-1])
        gn = to2d(g_big[n])
        d, mn, vn = adamw(to2d(w[n]), gn, to2d(a["m_" + n]), to2d(a["v_" + n]), "adamw_" + n)
        for tag, val in zip(("grad", "delta", "m", "v"), (gn, d, mn, vn)):
            out[tag][n] = val.reshape(sh)

    packed_names = [n for n in SMALL if n not in NARROW]
    single_names = list(NARROW) + [n for n, _ in SHARDED_F32]
    packed, slots = pack_rows([_as2d(g[n]) for n in packed_names], "pack_small_grads")
    gath = allgather_devices([packed] + [_as2d(g[n]) for n in single_names])
    jobs = [(n, gath[0], s) for n, s in zip(packed_names, slots)]
    jobs += [(n, gt, None) for n, gt in zip(single_names, gath[1:])]
    for n, gt, slot in jobs:
        sh = w[n].shape
        w2, m2, v2 = _as2d(w[n]), _as2d(a["m_" + n]), _as2d(a["v_" + n])
        if n in DENSE_B:
            gn = _as2d(s5_b_from_dense(sum_slot(gt, slot, g[n].shape[-2:], "sum_" + n)))
            d, mn, vn = adamw(w2, gn, m2, v2, "adamw_" + n)
        else:
            gn, d, mn, vn = adamw_allreduce(gt, w2, m2, v2, shard, "adamw_" + n, slot=slot)
        for tag, val in zip(("grad", "delta", "m", "v"), (gn, d, mn, vn)):
            out[tag][n] = val.reshape(sh)

    res = [loss, grad_x[None]]
    for tag in ("grad", "delta", "m", "v"):
        res += [out[tag][n] for n in WEIGHTS]
    return tuple(res)
```

```python
import math

import jax
import jax.numpy as jnp
import numpy as np
from jax import lax
from jax.experimental import pallas as pl
from jax.experimental.pallas import tpu as pltpu

F32 = jnp.float32
BF16 = jnp.bfloat16

D_MODEL = 1024
CHUNK = 64
LEFT_CHUNKS = 8
S5_WIDTH = 512
S5_GROUP = 16
S5_GROUPS = 32
S5_STATE = 64
S5_COLS = S5_GROUPS * S5_STATE
S5_SPLIT = 4
S5_CC = S5_COLS // S5_SPLIT
S5_UC = S5_WIDTH // S5_SPLIT
CONV_WIDTH = 512
CONV_KERNEL = 31
CONV_HALO = 32
ATT_HEADS = 16
ATT_HEAD_DIM = 64
MAX_REL = 128
MEM_LEN = 256
XA_HEADS = 4
XA_HEAD_DIM = 256
EPS = 1e-6
EVEN_IN = 2560
ODD_IN = 4096

ADAM_LR = 0.001
ADAM_B1 = 0.9
ADAM_B2 = 0.999
ADAM_EPS = 1e-08
ADAM_WD = 0.01
ADAM_STEP = 10

ROW_TILE = 256
MM_TILE = 512
ATT_QB = 256
ATT_PAD = LEFT_CHUNKS * CHUNK
ATT_WIN = ATT_PAD + ATT_QB
VMEM_LIMIT_V7X = 56 * 1024 * 1024
NEG = -1e30
LANES = 128
N_CHIPS = 4
N_DEV = 8

MESH = pl.DeviceIdType.MESH
ANY = pl.BlockSpec(memory_space=pl.ANY)


def _cp(*sem, vmem=VMEM_LIMIT_V7X):
    return pltpu.CompilerParams(dimension_semantics=sem if sem else None, vmem_limit_bytes=vmem)


def _full(shape):
    n = len(shape)
    return pl.BlockSpec(shape, lambda *_: (0,) * n)


def _wspec(w, layer=None):
    if layer is None:
        return _full(w.shape)
    s, _, r, c = w.shape
    return pl.BlockSpec((s, None, r, c), lambda *_: (0, layer, 0, 0))


def _lane_tile(n, cap):
    return max(t for t in range(LANES, min(n, cap) + 1, LANES) if n % t == 0)


def _sigmoid(x):
    return 1.0 / (1.0 + jnp.exp(-x))


def _silu(x):
    return x * _sigmoid(x)


def _dsilu(x):
    s = _sigmoid(x)
    return s * (1.0 + x * (1.0 - s))


_GELU_C = math.sqrt(2.0 / math.pi)


def _gelu(x):
    return 0.5 * x * (1.0 + jnp.tanh(_GELU_C * (x + 0.044715 * x * x * x)))


def _dgelu(x):
    t = jnp.tanh(_GELU_C * (x + 0.044715 * x * x * x))
    return 0.5 * (1.0 + t) + 0.5 * x * (1.0 - t * t) * _GELU_C * (1.0 + 3.0 * 0.044715 * x * x)


def _dot(a, b):
    return jnp.dot(a, b, preferred_element_type=F32)


def _dot_nt(a, b):
    return lax.dot_general(a, b, (((1,), (1,)), ((), ())), preferred_element_type=F32)


def _dot_tn(a, b):
    return lax.dot_general(a, b, (((0,), (0,)), ((), ())), preferred_element_type=F32)


def _dot_cols(a, w4, shards=range(N_CHIPS)):
    return jnp.concatenate([_dot(a, w4[s]) for s in shards], axis=1)


def _dot_rows(a, w4):
    r = w4.shape[1]
    acc = _dot(a[:, 0:r], w4[0])
    for s in range(1, N_CHIPS):
        acc = acc + _dot(a[:, s * r:(s + 1) * r], w4[s])
    return acc


def _dot_nt_cols(dys, w4):
    acc = _dot_nt(dys[0], w4[0])
    for s in range(1, N_CHIPS):
        acc = acc + _dot_nt(dys[s], w4[s])
    return acc


def _dot_nt_rows(dy, w4):
    return jnp.concatenate([_dot_nt(dy, w4[s]) for s in range(N_CHIPS)], axis=1)


def _col_pieces(v, n):
    return [v[:, s * n:(s + 1) * n] for s in range(N_CHIPS)]


def _rms_parts(xv):
    inv = lax.rsqrt(jnp.mean(xv * xv, axis=-1, keepdims=True) + EPS)
    return inv, xv * inv


def _rms_bwd(xv, g, dh):
    inv, xhat = _rms_parts(xv)
    dg = jnp.sum(dh * xhat, axis=0, keepdims=True)
    dxh = dh * g
    dx = inv * (dxh - xhat * jnp.mean(dxh * xhat, axis=-1, keepdims=True))
    return dx, dg


def norm_mm(x, g, w4, groups, name, tm=MM_TILE):
    M, D = x.shape
    n = w4.shape[2]
    tm = min(tm, M)

    def body(x_ref, g_ref, w_ref, *outs):
        _, xhat = _rms_parts(x_ref[...])
        hb = (xhat * g_ref[...]).astype(BF16)
        for o, (shards, dt, _) in zip(outs, groups):
            o[...] = _dot_cols(hb, w_ref, shards).astype(dt)
        outs[-1][...] = hb

    out_shape = [jax.ShapeDtypeStruct((M + pad, len(sh) * n), dt) for (sh, dt, pad) in groups]
    out_specs = [pl.BlockSpec((tm, len(sh) * n), lambda i, p=pad // tm: (i + p, 0)) for (sh, _, pad) in groups]
    out_shape.append(jax.ShapeDtypeStruct((M, D), BF16))
    out_specs.append(pl.BlockSpec((tm, D), lambda i: (i, 0)))
    return pl.pallas_call(
        body, name=name, grid=(M // tm,),
        in_specs=[pl.BlockSpec((tm, D), lambda i: (i, 0)), _full(g.shape), _full(w4.shape)],
        out_specs=out_specs, out_shape=out_shape, compiler_params=_cp("parallel"),
    )(x, g, w4)


def zero_rows(buf, rows, name, tm=ROW_TILE):
    C = buf.shape[1]

    def body(b_ref, o_ref):
        o_ref[...] = jnp.zeros_like(o_ref)

    return pl.pallas_call(
        body, name=name, grid=(rows // tm,), in_specs=[ANY],
        out_specs=pl.BlockSpec((tm, C), lambda i: (i, 0)),
        out_shape=jax.ShapeDtypeStruct(buf.shape, buf.dtype), input_output_aliases={0: 0},
        compiler_params=_cp("parallel"),
    )(buf)


def mm_res(a, w4, res, name, tm=MM_TILE):
    M, K = a.shape
    N = w4.shape[2]
    tm = min(tm, M)

    def body(a_ref, w_ref, r_ref, o_ref):
        o_ref[...] = r_ref[...] + _dot_rows(a_ref[...], w_ref)

    return pl.pallas_call(
        body, name=name, grid=(M // tm,),
        in_specs=[pl.BlockSpec((tm, K), lambda i: (i, 0)), _full(w4.shape), pl.BlockSpec((tm, N), lambda i: (i, 0))],
        out_specs=pl.BlockSpec((tm, N), lambda i: (i, 0)),
        out_shape=jax.ShapeDtypeStruct((M, N), F32), compiler_params=_cp("parallel"),
    )(a, w4, res)


def mm_cols(a, w, layer, name, out_dtype):
    M = a.shape[0]
    n = w.shape[3]

    def body(a_ref, w_ref, o_ref):
        o_ref[...] = _dot_cols(a_ref[...], w_ref).astype(out_dtype)

    return pl.pallas_call(
        body, name=name, grid=(1,), in_specs=[_full(a.shape), _wspec(w, layer)],
        out_specs=_full((M, N_CHIPS * n)), out_shape=jax.ShapeDtypeStruct((M, N_CHIPS * n), out_dtype),
        compiler_params=_cp("arbitrary"),
    )(a, w)


def mm_nt_cols(dy, w, layer, name):
    M = dy.shape[0]
    K, n = w.shape[2], w.shape[3]

    def body(d_ref, w_ref, o_ref):
        o_ref[...] = _dot_nt_cols(_col_pieces(d_ref[...].astype(BF16), n), w_ref)

    return pl.pallas_call(
        body, name=name, grid=(1,), in_specs=[_full(dy.shape), _wspec(w, layer)],
        out_specs=_full((M, K)), out_shape=jax.ShapeDtypeStruct((M, K), F32), compiler_params=_cp("arbitrary"),
    )(dy, w)


def mm_nt_rows(dy, w4, name, tm=MM_TILE):
    M, N = dy.shape
    K = N_CHIPS * w4.shape[1]
    tm = min(tm, M)

    def body(d_ref, w_ref, o_ref):
        o_ref[...] = _dot_nt_rows(d_ref[...].astype(BF16), w_ref)

    return pl.pallas_call(
        body, name=name, grid=(M // tm,),
        in_specs=[pl.BlockSpec((tm, N), lambda i: (i, 0)), _full(w4.shape)],
        out_specs=pl.BlockSpec((tm, K), lambda i: (i, 0)),
        out_shape=jax.ShapeDtypeStruct((M, K), F32), compiler_params=_cp("parallel"),
    )(dy, w4)


def mm_nt_normbwd(dys, offs, w4, x, g, dx_out, name, tm=MM_TILE):
    M, D = x.shape
    n = w4.shape[2]
    tm = min(tm, M)
    nd = len(dys)

    def body(*refs):
        d_refs = refs[:nd]
        w_ref, x_ref, g_ref, dxo_ref, dx_ref, dg_ref = refs[nd:]
        if nd == 1:
            pieces = _col_pieces(d_refs[0][...].astype(BF16), n)
        else:
            pieces = [r[...].astype(BF16) for r in d_refs]
        dh = _dot_nt_cols(pieces, w_ref)
        dx, dg = _rms_bwd(x_ref[...], g_ref[...], dh)
        dx_ref[...] = dxo_ref[...] + dx

        @pl.when(pl.program_id(0) == 0)
        def _():
            dg_ref[...] = jnp.zeros_like(dg_ref)

        dg_ref[...] += dg

    row = lambda c, off=0: pl.BlockSpec((tm, c), lambda i, p=off // tm: (i + p, 0))
    return pl.pallas_call(
        body, name=name, grid=(M // tm,),
        in_specs=[row(d.shape[1], off) for d, off in zip(dys, offs)] + [_full(w4.shape), row(D), _full(g.shape), row(D)],
        out_specs=[row(D), _full((1, D))],
        out_shape=[jax.ShapeDtypeStruct((M, D), F32), jax.ShapeDtypeStruct((1, D), F32)],
        compiler_params=_cp("arbitrary"),
    )(*dys, w4, x, g, dx_out)


def mm_tn(a, b, name, layout, into=None, b_off=0, out_dtype=BF16, bm=1024, bn=1280, bl=1024):
    L, K = a.shape
    N = b.shape[1]
    kind = layout[0]
    arg = layout[1] if len(layout) > 1 else None
    bm, bn, bl = _lane_tile(K, bm), _lane_tile(N, bn), min(bl, L)
    assert L % bl == 0 and b_off % bl == 0, (L, bl, b_off)
    nl = L // bl
    n_sh, r_sh = N // N_CHIPS, K // N_CHIPS
    lay = (None,) if arg is None else (None, None)
    mid = () if arg is None else (arg,)
    gs = 1
    if kind == "plain":
        oshape, oblock, oidx = (K, N), (bm, bn), lambda i, j, l: (i, j)
    elif kind == "slab":
        oshape, oblock, oidx = (N_CHIPS, K, N), (None, bm, bn), lambda i, j, l: (arg, i, j)
    elif kind == "cols":
        bn = max(bn - bn % n_sh, n_sh) if bn >= n_sh else _lane_tile(n_sh, bn)
        gs = max(bn // n_sh, 1)
        per = n_sh // bn if gs == 1 else 1
        oshape = (N_CHIPS,) + ((2,) if arg is not None else ()) + (K, n_sh)
        oblock = ((gs,) if gs > 1 else (None,)) + lay[1:] + (bm, min(bn, n_sh))
        oidx = lambda i, j, l: (j // per,) + mid + (i, j % per)
    else:
        bm = max(bm - bm % r_sh, r_sh) if bm >= r_sh else _lane_tile(r_sh, bm)
        gs = max(bm // r_sh, 1)
        per = r_sh // bm if gs == 1 else 1
        oshape = (N_CHIPS,) + ((2,) if arg is not None else ()) + (r_sh, N)
        oblock = ((gs,) if gs > 1 else (None,)) + lay[1:] + (min(bm, r_sh), bn)
        oidx = lambda i, j, l: (i // per,) + mid + (i % per, j)
    assert K % bm == 0 and N % bn == 0, (K, bm, N, bn)

    def body(a_ref, b_ref, *rest):
        o_ref, acc = rest[-2], rest[-1]
        l = pl.program_id(2)

        @pl.when(l == 0)
        def _():
            acc[...] = jnp.zeros_like(acc)

        acc[...] += _dot_tn(a_ref[...].astype(BF16), b_ref[...].astype(BF16))

        @pl.when(l == nl - 1)
        def _():
            if gs == 1:
                o_ref[...] = acc[...].astype(out_dtype)
            elif kind == "cols":
                for t in range(gs):
                    o_ref[t] = acc[:, t * n_sh:(t + 1) * n_sh].astype(out_dtype)
            else:
                for t in range(gs):
                    o_ref[t] = acc[t * r_sh:(t + 1) * r_sh, :].astype(out_dtype)

    in_specs = [pl.BlockSpec((bl, bm), lambda i, j, l: (l, i)),
                pl.BlockSpec((bl, bn), lambda i, j, l, p=b_off // bl: (l + p, j))]
    args = [a, b]
    alias = {}
    if into is not None:
        in_specs.append(ANY)
        args.append(into)
        alias = {2: 0}
    return pl.pallas_call(
        body, name=name, grid=(K // bm, N // bn, nl), in_specs=in_specs,
        out_specs=pl.BlockSpec(oblock, oidx), out_shape=jax.ShapeDtypeStruct(oshape, out_dtype),
        scratch_shapes=[pltpu.VMEM((bm, bn), F32)], input_output_aliases=alias,
        compiler_params=_cp("parallel", "parallel", "arbitrary"),
    )(*args)


def rms_fwd(x, g, name):
    def body(x_ref, g_ref, ob_ref):
        _, xhat = _rms_parts(x_ref[...])
        ob_ref[...] = (xhat * g_ref[...]).astype(BF16)

    return pl.pallas_call(body, name=name, out_shape=jax.ShapeDtypeStruct(x.shape, BF16))(x, g)


def rms_dgain(x, dy0, dy1, name):
    def body(x_ref, d0_ref, d1_ref, o_ref):
        _, xhat = _rms_parts(x_ref[...])
        o_ref[...] = jnp.sum((d0_ref[...] + d1_ref[...]) * xhat, axis=0, keepdims=True)

    return pl.pallas_call(body, name=name, out_shape=jax.ShapeDtypeStruct((1, x.shape[1]), F32))(x, dy0, dy1)


def _s5_discretise(lr, li, logdt, bt_re, bt_im):
    dt = jnp.exp(logdt)
    mag = jnp.exp(lr * dt)
    ab_re = mag * jnp.cos(li * dt)
    ab_im = mag * jnp.sin(li * dt)
    den = lr * lr + li * li
    nr = ab_re - 1.0
    coef_re = (nr * lr + ab_im * li) / den
    coef_im = (ab_im * lr - nr * li) / den
    cr = coef_re[:, None, :]
    ci = coef_im[:, None, :]
    bb_re = cr * bt_re - ci * bt_im
    bb_im = cr * bt_im + ci * bt_re
    return ab_re, ab_im, bb_re, bb_im


def s5_param_fwd(lr, li, logdt, bt_re, bt_im):
    def body(lr_ref, li_ref, ld_ref, br_ref, bi_ref, bbr_ref, bbi_ref):
        _, _, bb_re, bb_im = _s5_discretise(lr_ref[...], li_ref[...], ld_ref[...], br_ref[...], bi_ref[...])
        bbr_ref[...] = bb_re
        bbi_ref[...] = bb_im

    sh = jax.ShapeDtypeStruct(bt_re.shape, F32)
    return pl.pallas_call(body, name="s5_param_fwd", out_shape=[sh, sh])(lr, li, logdt, bt_re, bt_im)


def s5_param_bwd(lr, li, logdt, bt_re, bt_im, d_ab_re, d_ab_im, d_bb_re, d_bb_im):
    def body(lr_ref, li_ref, ld_ref, br_ref, bi_ref, dar_ref, dai_ref, dbr_ref, dbi_ref,
             o_lr, o_li, o_ld, o_br, o_bi):
        _, vjp = jax.vjp(_s5_discretise, lr_ref[...], li_ref[...], ld_ref[...], br_ref[...], bi_ref[...])
        g = vjp((dar_ref[...], dai_ref[...], dbr_ref[...], dbi_ref[...]))
        for o, v in zip((o_lr, o_li, o_ld), g[:3]):
            o[...] = v
        for o, v in zip((o_br, o_bi), g[3:]):
            for c in range(S5_GROUP):
                o[:, c * S5_STATE:(c + 1) * S5_STATE] = v[:, c, :]

    dense = jax.ShapeDtypeStruct((S5_GROUPS, S5_GROUP * S5_STATE), F32)
    shapes = [jax.ShapeDtypeStruct(a.shape, F32) for a in (lr, li, logdt)] + [dense, dense]
    return pl.pallas_call(body, name="s5_param_bwd", out_shape=shapes)(
        lr, li, logdt, bt_re, bt_im, d_ab_re, d_ab_im, d_bb_re, d_bb_im)


def s5_tables(lr_flat, li_flat, logdt_flat):
    def body(lr_ref, li_ref, ld_ref, tab_ref):
        dt = jnp.exp(ld_ref[...])
        a = lr_ref[...] * dt
        th = li_ref[...] * dt
        row = lax.broadcasted_iota(jnp.int32, (8, 1), 0)
        rowf = row.astype(F32)

        def power(e, sign):
            m = jnp.exp(e * a)
            return m * jnp.cos(e * th), sign * m * jnp.sin(e * th)

        k = 0
        for sign, fwd in ((1.0, True), (-1.0, False)):
            for s in (1, 2, 4):
                pr, pi = power(jnp.full((8, 1), float(s), F32), sign)
                keep = (row >= s) if fwd else (row + s < 8)
                tab_ref[k] = jnp.where(keep, pr, 0.0)
                tab_ref[k + 1] = jnp.where(keep, pi, 0.0)
                k += 2
            e = rowf + 1.0 if fwd else 8.0 - rowf
            pr, pi = power(e, sign)
            tab_ref[k] = pr
            tab_ref[k + 1] = pi
            k += 2

    return pl.pallas_call(body, name="s5_tables",
                          out_shape=jax.ShapeDtypeStruct((16, 8, S5_COLS), F32))(lr_flat, li_flat, logdt_flat)


def _scan_block(a, b, tabs, base, cr, ci, reverse):
    for n, s in enumerate((1, 2, 4)):
        mr = tabs[base + 2 * n]
        mi = tabs[base + 2 * n + 1]
        sh = (8 - s) if reverse else s
        ar = pltpu.roll(a, sh, 0)
        br = pltpu.roll(b, sh, 0)
        a, b = a + mr * ar - mi * br, b + mr * br + mi * ar
    pr = tabs[base + 6]
    pi = tabs[base + 7]
    a, b = a + pr * cr - pi * ci, b + pr * ci + pi * cr
    return a, b


class Carried:
    def __init__(self, arrays, out_shapes, sems, start, middle, finish):
        self.arrays, self.out_shapes, self.sems = list(arrays), list(out_shapes), list(sems)
        self.start, self.middle, self.finish = start, middle, finish

    def split(self, refs, n_in, n_out, n_scratch):
        a, o, s = len(self.arrays), len(self.out_shapes), len(self.sems)
        own_in, car_in = refs[:n_in], refs[n_in:n_in + a]
        own_out, car_out = refs[n_in + a:n_in + a + n_out], refs[n_in + a + n_out:n_in + a + n_out + o]
        rest = refs[n_in + a + n_out + o:]
        return own_in + own_out + rest[:n_scratch], (car_in, car_out, rest[n_scratch:n_scratch + s])

    def hooks(self, parts, n_chunks, nt):
        j, t = pl.program_id(0), pl.program_id(1)

        def top():
            pl.when((j == 0) & (t == 0))(lambda: self.start(*parts))
            if self.middle is not None:
                pl.when((j == n_chunks // 2) & (t == 0))(lambda: self.middle(*parts))

        def end():
            pl.when((j == n_chunks - 1) & (t == nt - 1))(lambda: self.finish(*parts))

        return top, end


def s5_fwd(z, bbd_re, bbd_im, ccd_re, ccd_im, tab, dskip, tm=ROW_TILE, carried=None):
    L = z.shape[0]
    tm = min(tm, L)
    nt = L // tm

    def body(*refs):
        top = end = None
        if carried is not None:
            refs, parts = carried.split(refs, 7, 2, 3)
            top, end = carried.hooks(parts, S5_SPLIT, nt)
            top()
        u_ref, bbr_ref, bbi_ref, ccr_ref, cci_ref, tab_ref, d_ref, y_ref, ck_ref, xr, xi, car = refs
        t = pl.program_id(1)

        @pl.when(t == 0)
        def _():
            car[...] = jnp.zeros_like(car)

        u = u_ref[...]
        ub = u.astype(BF16)
        xr[...] = _dot(ub, bbr_ref[...])
        xi[...] = _dot(ub, bbi_ref[...])
        tabs = [tab_ref[k] for k in range(8)]

        def blk(i, c):
            r0 = pl.multiple_of(i * 8, 8)
            a, b = _scan_block(xr[pl.ds(r0, 8), :], xi[pl.ds(r0, 8), :], tabs, 0, c[0], c[1], False)
            xr[pl.ds(r0, 8), :] = a
            xi[pl.ds(r0, 8), :] = b
            return a[7:8, :], b[7:8, :]

        cr, ci = lax.fori_loop(0, tm // 8, blk, (car[0:1, :], car[1:2, :]))
        car[0:1, :] = cr
        car[1:2, :] = ci
        ck_ref[0:1, :] = cr
        ck_ref[1:2, :] = ci
        y_ref[...] = (_dot(xr[...].astype(BF16), ccr_ref[...]) - _dot(xi[...].astype(BF16), cci_ref[...])
                      + d_ref[...] * u)
        if end is not None:
            end()

    extra = carried.arrays if carried is not None else []
    extra_out = carried.out_shapes if carried is not None else []
    extra_sems = carried.sems if carried is not None else []
    return pl.pallas_call(
        body, name="s5_fwd", grid=(S5_SPLIT, nt),
        in_specs=[pl.BlockSpec((tm, S5_UC), lambda j, t: (t, j)),
                  pl.BlockSpec((None, S5_UC, S5_CC), lambda j, t: (j, 0, 0)),
                  pl.BlockSpec((None, S5_UC, S5_CC), lambda j, t: (j, 0, 0)),
                  pl.BlockSpec((None, S5_CC, S5_UC), lambda j, t: (j, 0, 0)),
                  pl.BlockSpec((None, S5_CC, S5_UC), lambda j, t: (j, 0, 0)),
                  pl.BlockSpec((8, 8, S5_CC), lambda j, t: (0, 0, j)),
                  pl.BlockSpec((1, S5_UC), lambda j, t: (0, j))] + [ANY] * len(extra),
        out_specs=[pl.BlockSpec((tm, S5_UC), lambda j, t: (t, j)),
                   pl.BlockSpec((None, 2, S5_CC), lambda j, t: (t, 0, j))] + [ANY] * len(extra_out),
        out_shape=[jax.ShapeDtypeStruct((L, S5_WIDTH), F32), jax.ShapeDtypeStruct((nt, 2, S5_COLS), F32)] + extra_out,
        scratch_shapes=[pltpu.VMEM((tm, S5_CC), F32), pltpu.VMEM((tm, S5_CC), F32), pltpu.VMEM((2, S5_CC), F32)]
        + extra_sems,
        compiler_params=_cp("arbitrary" if carried is not None else "parallel", "arbitrary"),
    )(z, bbd_re, bbd_im, ccd_re, ccd_im, tab, dskip, *extra)


def s5_bwd(z, dy, dz, ckpt, bbd_re, bbd_im, ccd_re, ccd_im, tab, dskip, tm=ROW_TILE, carried=None):
    L = z.shape[0]
    tm = min(tm, L)
    nt = L // tm

    def body(*refs):
        top = end = None
        if carried is not None:
            refs, parts = carried.split(refs, 10, 7, 7)
            top, end = carried.hooks(parts, S5_SPLIT, nt)
            top()
        (u_ref, dy_ref, dz_ref, ck_ref, bbr_ref, bbi_ref, ccr_ref, cci_ref, tab_ref, d_ref,
         du_ref, da_ref, dbr_ref, dbi_ref, dcr_ref, dci_ref, dd_ref, hr, hi, gr, gi, car, acr, aci) = refs
        t = pl.program_id(1)
        tt = nt - 1 - t

        @pl.when(t == 0)
        def _():
            for r in (car, acr, aci, dbr_ref, dbi_ref, dcr_ref, dci_ref, dd_ref):
                r[...] = jnp.zeros_like(r)

        u = u_ref[...]
        ub = u.astype(BF16)
        dyv = dy_ref[...]
        dyb = dyv.astype(BF16)
        tabs = [tab_ref[k] for k in range(16)]

        live = (tt > 0).astype(F32)
        c0r = ck_ref[0:1, :] * live
        c0i = ck_ref[1:2, :] * live
        hr[0:8, :] = jnp.broadcast_to(c0r, (8, S5_CC))
        hi[0:8, :] = jnp.broadcast_to(c0i, (8, S5_CC))
        hr[8:, :] = _dot(ub, bbr_ref[...])
        hi[8:, :] = _dot(ub, bbi_ref[...])

        def fblk(i, c):
            r0 = pl.multiple_of(i * 8 + 8, 8)
            a, b = _scan_block(hr[pl.ds(r0, 8), :], hi[pl.ds(r0, 8), :], tabs, 0, c[0], c[1], False)
            hr[pl.ds(r0, 8), :] = a
            hi[pl.ds(r0, 8), :] = b
            return a[7:8, :], b[7:8, :]

        lax.fori_loop(0, tm // 8, fblk, (c0r, c0i))
        hrb = hr[8:, :].astype(BF16)
        hib = hi[8:, :].astype(BF16)
        dcr_ref[...] += _dot_tn(hrb, dyb)
        dci_ref[...] -= _dot_tn(hib, dyb)

        gr[...] = _dot_nt(dyb, ccr_ref[...])
        gi[...] = -_dot_nt(dyb, cci_ref[...])
        row0 = lax.broadcasted_iota(jnp.int32, (8, S5_CC), 0) == 0

        def rblk(k, c):
            i = tm // 8 - 1 - k
            r0 = pl.multiple_of(i * 8, 8)
            a, b = _scan_block(gr[pl.ds(r0, 8), :], gi[pl.ds(r0, 8), :], tabs, 8, c[0], c[1], True)
            gr[pl.ds(r0, 8), :] = a
            gi[pl.ds(r0, 8), :] = b
            r1 = pl.multiple_of(i * 8 + 8, 8)
            hpr = jnp.where(row0, pltpu.roll(hr[pl.ds(r0, 8), :], 1, 0), pltpu.roll(hr[pl.ds(r1, 8), :], 1, 0))
            hpi = jnp.where(row0, pltpu.roll(hi[pl.ds(r0, 8), :], 1, 0), pltpu.roll(hi[pl.ds(r1, 8), :], 1, 0))
            acr[...] += a * hpr + b * hpi
            aci[...] += b * hpr - a * hpi
            return a[0:1, :], b[0:1, :]

        cr, ci = lax.fori_loop(0, tm // 8, rblk, (car[0:1, :], car[1:2, :]))
        car[0:1, :] = cr
        car[1:2, :] = ci

        grb = gr[...].astype(BF16)
        gib = gi[...].astype(BF16)
        du_ref[...] = (_dot_nt(grb, bbr_ref[...]) + _dot_nt(gib, bbi_ref[...]) + d_ref[...] * dyv).astype(BF16)
        dbr_ref[...] += _dot_tn(ub, grb)
        dbi_ref[...] += _dot_tn(ub, gib)
        dd_ref[...] += jnp.sum(dyv * u, axis=0, keepdims=True)

        @pl.when(t == nt - 1)
        def _():
            da_ref[0:1, :] = jnp.sum(acr[...], axis=0, keepdims=True)
            da_ref[1:2, :] = jnp.sum(aci[...], axis=0, keepdims=True)

        if end is not None:
            end()

    extra = carried.arrays if carried is not None else []
    extra_out = carried.out_shapes if carried is not None else []
    extra_sems = carried.sems if carried is not None else []
    chunk = lambda a, b: pl.BlockSpec((None, a, b), lambda j, t: (j, 0, 0))
    return pl.pallas_call(
        body, name="s5_bwd", grid=(S5_SPLIT, nt),
        in_specs=[pl.BlockSpec((tm, S5_UC), lambda j, t: (nt - 1 - t, j)),
                  pl.BlockSpec((tm, S5_UC), lambda j, t: (nt - 1 - t, j)),
                  ANY,
                  pl.BlockSpec((None, 2, S5_CC), lambda j, t: (jnp.maximum(nt - 2 - t, 0), 0, j)),
                  chunk(S5_UC, S5_CC), chunk(S5_UC, S5_CC), chunk(S5_CC, S5_UC), chunk(S5_CC, S5_UC),
                  pl.BlockSpec((16, 8, S5_CC), lambda j, t: (0, 0, j)),
                  pl.BlockSpec((1, S5_UC), lambda j, t: (0, j))] + [ANY] * len(extra),
        out_specs=[pl.BlockSpec((tm, S5_UC), lambda j, t: (nt - 1 - t, j)),
                   pl.BlockSpec((None, 2, S5_CC), lambda j, t: (j, 0, 0)),
                   chunk(S5_UC, S5_CC), chunk(S5_UC, S5_CC), chunk(S5_CC, S5_UC), chunk(S5_CC, S5_UC),
                   pl.BlockSpec((1, S5_UC), lambda j, t: (0, j))] + [ANY] * len(extra_out),
        out_shape=[jax.ShapeDtypeStruct(dz.shape, dz.dtype),
                   jax.ShapeDtypeStruct((S5_SPLIT, 2, S5_CC), F32),
                   jax.ShapeDtypeStruct((S5_SPLIT, S5_UC, S5_CC), F32),
                   jax.ShapeDtypeStruct((S5_SPLIT, S5_UC, S5_CC), F32),
                   jax.ShapeDtypeStruct((S5_SPLIT, S5_CC, S5_UC), F32),
                   jax.ShapeDtypeStruct((S5_SPLIT, S5_CC, S5_UC), F32),
                   jax.ShapeDtypeStruct((1, S5_WIDTH), F32)] + extra_out,
        scratch_shapes=[pltpu.VMEM((tm + 8, S5_CC), F32), pltpu.VMEM((tm + 8, S5_CC), F32),
                        pltpu.VMEM((tm, S5_CC), F32), pltpu.VMEM((tm, S5_CC), F32),
                        pltpu.VMEM((2, S5_CC), F32), pltpu.VMEM((8, S5_CC), F32), pltpu.VMEM((8, S5_CC), F32)]
        + extra_sems,
        input_output_aliases={2: 0},
        compiler_params=_cp("arbitrary" if carried is not None else "parallel", "arbitrary"),
    )(z, dy, dz, ckpt, bbd_re, bbd_im, ccd_re, ccd_im, tab, dskip, *extra)


_EYE8 = np.eye(S5_GROUPS // S5_SPLIT, dtype=np.float32)


def _blockdiag(a):
    g, r, c = a.shape
    a = a.reshape(S5_SPLIT, g // S5_SPLIT, r, c)
    out = a[:, :, :, None, :] * _EYE8[None, :, None, :, None].astype(a.dtype)
    return out.reshape(S5_SPLIT, (g // S5_SPLIT) * r, (g // S5_SPLIT) * c)


def _blockdiag_extract(a, r, c):
    n = S5_GROUPS // S5_SPLIT
    a = a.reshape(S5_SPLIT, n, r, n, c)
    d = jnp.stack([a[:, k, :, k, :] for k in range(n)], axis=1)
    return d.reshape(S5_GROUPS, r, c)


def s5_mixer_core_fwd(z, lam_re, lam_im, log_dt, b_re, b_im, c_re, c_im, d_skip, carried=None):
    bt_re = jnp.swapaxes(b_re, 1, 2)
    bt_im = jnp.swapaxes(b_im, 1, 2)
    logdt = log_dt.reshape(S5_GROUPS, 1)
    bb_re, bb_im = s5_param_fwd(lam_re, lam_im, logdt, bt_re, bt_im)
    flat = lambda a: a.reshape(1, S5_COLS)
    tab = s5_tables(flat(lam_re), flat(lam_im), flat(jnp.broadcast_to(logdt, (S5_GROUPS, S5_STATE))))
    bbd_re = _blockdiag(bb_re).astype(BF16)
    bbd_im = _blockdiag(bb_im).astype(BF16)
    ccd_re = _blockdiag(jnp.swapaxes(c_re, 1, 2)).astype(BF16)
    ccd_im = _blockdiag(jnp.swapaxes(c_im, 1, 2)).astype(BF16)
    dsk = d_skip.reshape(1, S5_WIDTH)
    y, ckpt, *landed = s5_fwd(z, bbd_re, bbd_im, ccd_re, ccd_im, tab, dsk, carried=carried)
    saved = (logdt, bt_re, bt_im, bbd_re, bbd_im, ccd_re, ccd_im, tab, dsk, ckpt)
    return y, saved, landed


def s5_b_from_dense(dense):
    return jnp.swapaxes(dense.reshape(S5_GROUPS, S5_GROUP, S5_STATE), 1, 2)


def s5_mixer_core_bwd(z, dy, dz, lam_re, lam_im, saved, carried=None):
    logdt, bt_re, bt_im, bbd_re, bbd_im, ccd_re, ccd_im, tab, dsk, ckpt = saved
    dz, da, dbr, dbi, dcr, dci, dd, *landed = s5_bwd(z, dy, dz, ckpt, bbd_re, bbd_im, ccd_re, ccd_im, tab, dsk,
                                                     carried=carried)
    d_ab_re = da[:, 0, :].reshape(S5_GROUPS, S5_STATE)
    d_ab_im = da[:, 1, :].reshape(S5_GROUPS, S5_STATE)
    d_bb_re = _blockdiag_extract(dbr, S5_GROUP, S5_STATE)
    d_bb_im = _blockdiag_extract(dbi, S5_GROUP, S5_STATE)
    g_lr, g_li, g_ld, g_btr, g_bti = s5_param_bwd(lam_re, lam_im, logdt, bt_re, bt_im,
                                                  d_ab_re, d_ab_im, d_bb_re, d_bb_im)
    g_cre = jnp.swapaxes(_blockdiag_extract(dcr, S5_STATE, S5_GROUP), 1, 2)
    g_cim = jnp.swapaxes(_blockdiag_extract(dci, S5_STATE, S5_GROUP), 1, 2)
    grads = dict(lambda_re=g_lr, lambda_im=g_li, log_dt=g_ld.reshape(S5_GROUPS), b_re=g_btr, b_im=g_bti,
                 c_re=g_cre, c_im=g_cim, d=dd.reshape(S5_WIDTH))
    return dz, grads, landed


Z_U, Z_GA, Z_VAL, Z_GLU, Z_GB = range(5)
SUBLANES = 8


def _shifted_copies(buf, tm):
    n = tm + CONV_HALO - SUBLANES
    for r in range(1, SUBLANES):
        buf[r, 0:n, :] = buf[0, pl.ds(r, n), :]


CONV_ROWS = 32


def _shifted_rows(buf, start, rows, base=0):
    return buf[start % SUBLANES, pl.ds(base + (start - start % SUBLANES), rows), :]


def conv_fwd(z, conv_w, conv_b, tm=ROW_TILE):
    L = z.shape[0]
    tm = min(tm, L)
    nt = L // tm
    hb = tm // CONV_HALO
    C = CONV_WIDTH

    def body(val_ref, glu_ref, valh_ref, gluh_ref, w_ref, b_ref, c_ref, vsh):
        live = (pl.program_id(0) > 0).astype(F32)
        vsh[0, 0:CONV_HALO, :] = valh_ref[...] * _sigmoid(gluh_ref[...]) * live
        vsh[0, CONV_HALO:, :] = val_ref[...] * _sigmoid(glu_ref[...])
        _shifted_copies(vsh, tm)

        def rows(i, carry):
            base = pl.multiple_of(i * CONV_ROWS, CONV_ROWS)
            acc = jnp.broadcast_to(b_ref[...], (CONV_ROWS, C))
            for k in range(CONV_KERNEL):
                acc = acc + w_ref[k:k + 1, :] * _shifted_rows(vsh, CONV_HALO - CONV_KERNEL + 1 + k, CONV_ROWS, base)
            c_ref[pl.ds(base, CONV_ROWS), :] = acc
            return carry

        lax.fori_loop(0, tm // CONV_ROWS, rows, 0)

    cur = lambda col: pl.BlockSpec((tm, C), lambda t: (t, col))
    prev = lambda col: pl.BlockSpec((CONV_HALO, C), lambda t: (jnp.maximum(t * hb - 1, 0), col))
    return pl.pallas_call(
        body, name="conv_fwd", grid=(nt,),
        in_specs=[cur(Z_VAL), cur(Z_GLU), prev(Z_VAL), prev(Z_GLU), _full(conv_w.shape), _full(conv_b.shape)],
        out_specs=pl.BlockSpec((tm, C), lambda t: (t, 0)),
        out_shape=jax.ShapeDtypeStruct((L, C), F32),
        scratch_shapes=[pltpu.VMEM((8, tm + CONV_HALO, C), F32)],
        compiler_params=_cp("parallel"),
    )(z, z, z, z, conv_w, conv_b)


def conv_bwd(z, dc, dz, conv_w, tm=ROW_TILE):
    L = z.shape[0]
    tm = min(tm, L)
    nt = L // tm
    hb = tm // CONV_HALO
    nh = L // CONV_HALO
    C = CONV_WIDTH
    off = CONV_HALO - CONV_KERNEL + 1

    def body(val_ref, glu_ref, valh_ref, gluh_ref, dc_ref, dcn_ref, dz_ref, w_ref, dvg_ref, dw_ref, db_ref,
             vsh, dsh, wacc):
        t = pl.program_id(0)

        @pl.when(t == 0)
        def _():
            wacc[...] = jnp.zeros_like(wacc)
            db_ref[...] = jnp.zeros_like(db_ref)

        val = val_ref[...]
        sg = _sigmoid(glu_ref[...])
        vsh[0, 0:CONV_HALO, :] = valh_ref[...] * _sigmoid(gluh_ref[...]) * (t > 0).astype(F32)
        vsh[0, CONV_HALO:, :] = val * sg
        dcv = dc_ref[...]
        dsh[0, 0:tm, :] = dcv
        dsh[0, tm:, :] = dcn_ref[...] * (t < nt - 1).astype(F32)
        _shifted_copies(vsh, tm)
        _shifted_copies(dsh, tm)

        def rows(i, carry):
            base = pl.multiple_of(i * CONV_ROWS, CONV_ROWS)
            dcr = dc_ref[pl.ds(base, CONV_ROWS), :]
            dv = jnp.zeros((CONV_ROWS, C), F32)
            for k in range(CONV_KERNEL):
                dv = dv + w_ref[k:k + 1, :] * _shifted_rows(dsh, CONV_KERNEL - 1 - k, CONV_ROWS, base)
                prod = dcr * _shifted_rows(vsh, off + k, CONV_ROWS, base)
                wacc[k] += jnp.sum(prod.reshape(CONV_ROWS // SUBLANES, SUBLANES, C), axis=0)
            valr = val_ref[pl.ds(base, CONV_ROWS), :]
            sgr = _sigmoid(glu_ref[pl.ds(base, CONV_ROWS), :])
            dvg_ref[pl.ds(base, CONV_ROWS), 0:C] = (dv * sgr).astype(BF16)
            dvg_ref[pl.ds(base, CONV_ROWS), C:] = (dv * valr * sgr * (1.0 - sgr)).astype(BF16)
            return carry

        lax.fori_loop(0, tm // CONV_ROWS, rows, 0)
        db_ref[...] += jnp.sum(dcv, axis=0, keepdims=True)

        @pl.when(t == nt - 1)
        def _():
            dw_ref[...] = jnp.sum(wacc[...], axis=1)

    cur = lambda col: pl.BlockSpec((tm, C), lambda t: (t, col))
    prev = lambda col: pl.BlockSpec((CONV_HALO, C), lambda t: (jnp.maximum(t * hb - 1, 0), col))
    nxt = pl.BlockSpec((CONV_HALO, C), lambda t: (jnp.minimum((t + 1) * hb, nh - 1), 0))
    row = pl.BlockSpec((tm, C), lambda t: (t, 0))
    return pl.pallas_call(
        body, name="conv_bwd", grid=(nt,),
        in_specs=[cur(Z_VAL), cur(Z_GLU), prev(Z_VAL), prev(Z_GLU), row, nxt, ANY, _full(conv_w.shape)],
        out_specs=[pl.BlockSpec((tm, 2 * C), lambda t: (t, 1)), _full((CONV_HALO, C)), _full((1, C))],
        out_shape=[jax.ShapeDtypeStruct(dz.shape, dz.dtype),
                   jax.ShapeDtypeStruct((CONV_HALO, C), F32), jax.ShapeDtypeStruct((1, C), F32)],
        scratch_shapes=[pltpu.VMEM((8, tm + CONV_HALO, C), F32), pltpu.VMEM((8, tm + CONV_HALO, C), F32),
                        pltpu.VMEM((CONV_HALO, SUBLANES, C), F32)],
        input_output_aliases={6: 0},
        compiler_params=_cp("arbitrary"),
    )(z, z, z, z, dc, dc, dz, conv_w)


def _ln_parts(c):
    mu = jnp.mean(c, axis=-1, keepdims=True)
    cc = c - mu
    rstd = lax.rsqrt(jnp.mean(cc * cc, axis=-1, keepdims=True) + EPS)
    return rstd, cc * rstd


def _ev_tail_branches(ys, c, wglu, bglu, lng, lnb):
    z1 = _gelu(ys)
    z1b = z1.astype(BF16)
    sg = _sigmoid(_dot_rows(z1b, wglu) + bglu)
    out = z1 * sg
    rstd, chat = _ln_parts(c)
    cn = chat * lng + lnb
    return z1, z1b, sg, out, rstd, chat, cn


def ev_tail_fwd(ys, z, c, x0, wglu, bglu, lng, lnb, wout, tm=ROW_TILE):
    L, D = x0.shape
    tm = min(tm, L)
    W = S5_WIDTH

    def body(ys_ref, ga_ref, c_ref, gb_ref, x_ref, wglu_ref, bglu_ref, lng_ref, lnb_ref, wout_ref, o_ref):
        _, _, _, out, _, _, cn = _ev_tail_branches(ys_ref[...], c_ref[...], wglu_ref, bglu_ref[...],
                                                   lng_ref[...], lnb_ref[...])
        ya = (out * _silu(ga_ref[...])).astype(BF16)
        yb = (_silu(cn) * _silu(gb_ref[...])).astype(BF16)
        o_ref[...] = x_ref[...] + _dot_rows(jnp.concatenate([ya, yb], axis=1), wout_ref)

    row = lambda n, col=0: pl.BlockSpec((tm, n), lambda t: (t, col))
    return pl.pallas_call(
        body, name="ev_tail_fwd", grid=(L // tm,),
        in_specs=[row(W), row(W, Z_GA), row(W), row(W, Z_GB), row(D), _full(wglu.shape), _full(bglu.shape),
                  _full(lng.shape), _full(lnb.shape), _full(wout.shape)],
        out_specs=row(D), out_shape=jax.ShapeDtypeStruct((L, D), F32), compiler_params=_cp("parallel"),
    )(ys, z, c, z, x0, wglu, bglu, lng, lnb, wout)


def ev_tail_bwd(ys, z, c, dx1, wglu, bglu, lng, lnb, wout, tm=ROW_TILE):
    L, D = dx1.shape
    tm = min(tm, L)
    W = S5_WIDTH

    def body(ys_ref, ga_ref, c_ref, gb_ref, dx_ref, wglu_ref, bglu_ref, lng_ref, lnb_ref, wout_ref,
             dys_ref, dc_ref, dz_ref, r_ref, z1_ref, dt_ref, dbg_ref, dlg_ref, dlb_ref):
        @pl.when(pl.program_id(0) == 0)
        def _():
            for r in (dbg_ref, dlg_ref, dlb_ref):
                r[...] = jnp.zeros_like(r)

        ys, ga, gb = ys_ref[...], ga_ref[...], gb_ref[...]
        z1, z1b, sg, out, rstd, chat, cn = _ev_tail_branches(ys, c_ref[...], wglu_ref, bglu_ref[...],
                                                             lng_ref[...], lnb_ref[...])
        sga, sgb, scn = _silu(ga), _silu(gb), _silu(cn)
        r_ref[:, 0:W] = (out * sga).astype(BF16)
        r_ref[:, W:] = (scn * sgb).astype(BF16)
        dr = _dot_nt_rows(dx_ref[...].astype(BF16), wout_ref)
        dra, drb = dr[:, 0:W], dr[:, W:]
        dz_ref[...] = jnp.zeros_like(dz_ref)
        dz_ref[:, Z_GA * W:(Z_GA + 1) * W] = (dra * out * _dsilu(ga)).astype(BF16)
        dout = dra * sga
        dt = dout * z1 * sg * (1.0 - sg)
        dtb = dt.astype(BF16)
        dz1 = dout * sg + _dot_nt_rows(dtb, wglu_ref)
        dys_ref[...] = dz1 * _dgelu(ys)
        z1_ref[...] = z1b
        dt_ref[...] = dtb
        dbg_ref[...] += jnp.sum(dt, axis=0, keepdims=True)
        dz_ref[:, Z_GB * W:(Z_GB + 1) * W] = (drb * scn * _dsilu(gb)).astype(BF16)
        dcn = drb * sgb * _dsilu(cn)
        dlg_ref[...] += jnp.sum(dcn * chat, axis=0, keepdims=True)
        dlb_ref[...] += jnp.sum(dcn, axis=0, keepdims=True)
        dch = dcn * lng_ref[...]
        dc_ref[...] = rstd * (dch - jnp.mean(dch, axis=-1, keepdims=True)
                              - chat * jnp.mean(dch * chat, axis=-1, keepdims=True))

    row = lambda n, col=0: pl.BlockSpec((tm, n), lambda t: (t, col))
    f = lambda n, dt: jax.ShapeDtypeStruct((L, n), dt)
    vec = jax.ShapeDtypeStruct((1, W), F32)
    return pl.pallas_call(
        body, name="ev_tail_bwd", grid=(L // tm,),
        in_specs=[row(W), row(W, Z_GA), row(W), row(W, Z_GB), row(D), _full(wglu.shape), _full(bglu.shape),
                  _full(lng.shape), _full(lnb.shape), _full(wout.shape)],
        out_specs=[row(W), row(W), row(EVEN_IN), row(D), row(W), row(W), _full((1, W)), _full((1, W)), _full((1, W))],
        out_shape=[f(W, F32), f(W, F32), f(EVEN_IN, BF16), f(D, BF16), f(W, BF16), f(W, BF16), vec, vec, vec],
        compiler_params=_cp("arbitrary"),
    )(ys, z, c, z, dx1, wglu, bglu, lng, lnb, wout)


XA_SCALE = XA_HEAD_DIM ** -0.5


def _xa_forward(xv, g, wqg, kv):
    D = D_MODEL
    _, xhat = _rms_parts(xv)
    hb = (xhat * g).astype(BF16)
    qb = _dot_cols(hb, wqg, (0, 1)).astype(BF16)
    gate = _dot_cols(hb, wqg, (2, 3))
    ps, os_ = [], []
    for h in range(XA_HEADS):
        lo, hi = h * XA_HEAD_DIM, (h + 1) * XA_HEAD_DIM
        s = _dot_nt(qb[:, lo:hi], kv[:, lo:hi]) * XA_SCALE
        e = jnp.exp(s - jnp.max(s, axis=-1, keepdims=True))
        p = e / jnp.sum(e, axis=-1, keepdims=True)
        ps.append(p)
        os_.append(_dot(p.astype(BF16), kv[:, D + lo:D + hi]))
    return hb, qb, gate, ps, jnp.concatenate(os_, axis=1)


def xa_fwd(x, g, wqg, kv, wo, layer, name, tm=MM_TILE):
    L, D = x.shape
    tm = min(tm, L)

    def body(x_ref, g_ref, wqg_ref, kv_ref, wo_ref, o_ref):
        xv = x_ref[...]
        _, _, gate, _, o = _xa_forward(xv, g_ref[...], wqg_ref, kv_ref[...])
        o_ref[...] = xv + _dot_rows((o * _silu(gate)).astype(BF16), wo_ref)

    row = pl.BlockSpec((tm, D), lambda t: (t, 0))
    return pl.pallas_call(
        body, name=name, grid=(L // tm,),
        in_specs=[row, _full(g.shape), _wspec(wqg, layer), _full(kv.shape), _wspec(wo, layer)],
        out_specs=row, out_shape=jax.ShapeDtypeStruct((L, D), F32), compiler_params=_cp("parallel"),
    )(x, g, wqg, kv, wo)


def xa_bwd(x, dxo, g, wqg, kv, wo, layer, name, tm=ROW_TILE):
    L, D = x.shape
    tm = min(tm, L)

    def body(x_ref, dxo_ref, g_ref, wqg_ref, kv_ref, wo_ref, dx_ref, dqg_ref, h_ref, r_ref, dkv_ref, dg_ref):
        @pl.when(pl.program_id(0) == 0)
        def _():
            dkv_ref[...] = jnp.zeros_like(dkv_ref)
            dg_ref[...] = jnp.zeros_like(dg_ref)

        xv = x_ref[...]
        kv = kv_ref[...]
        hb, qb, gate, ps, o = _xa_forward(xv, g_ref[...], wqg_ref, kv)
        sgate = _silu(gate)
        h_ref[...] = hb
        r_ref[...] = (o * sgate).astype(BF16)
        dxo = dxo_ref[...]
        dr = _dot_nt_rows(dxo.astype(BF16), wo_ref)
        do = dr * sgate
        dqg_ref[:, D:] = (dr * o * _dsilu(gate)).astype(BF16)
        dob = do.astype(BF16)
        for h in range(XA_HEADS):
            lo, hi = h * XA_HEAD_DIM, (h + 1) * XA_HEAD_DIM
            p = ps[h]
            pb = p.astype(BF16)
            dp = _dot_nt(dob[:, lo:hi], kv[:, D + lo:D + hi])
            dkv_ref[:, D + lo:D + hi] += _dot_tn(pb, dob[:, lo:hi])
            ds = p * (dp - jnp.sum(dp * p, axis=-1, keepdims=True))
            dsb = (ds * XA_SCALE).astype(BF16)
            dqg_ref[:, lo:hi] = _dot(dsb, kv[:, lo:hi]).astype(BF16)
            dkv_ref[:, lo:hi] += _dot_tn(dsb, qb[:, lo:hi])
        dh = _dot_nt_cols(_col_pieces(dqg_ref[...], D // 2), wqg_ref)
        dx, dg = _rms_bwd(xv, g_ref[...], dh)
        dx_ref[...] = dxo + dx
        dg_ref[...] += dg

    row = lambda n: pl.BlockSpec((tm, n), lambda t: (t, 0))
    return pl.pallas_call(
        body, name=name, grid=(L // tm,),
        in_specs=[row(D), row(D), _full(g.shape), _wspec(wqg, layer), _full(kv.shape), _wspec(wo, layer)],
        out_specs=[row(D), row(2 * D), row(D), row(D), _full(kv.shape), _full((1, D))],
        out_shape=[jax.ShapeDtypeStruct((L, D), F32), jax.ShapeDtypeStruct((L, 2 * D), BF16),
                   jax.ShapeDtypeStruct((L, D), BF16), jax.ShapeDtypeStruct((L, D), BF16),
                   jax.ShapeDtypeStruct(kv.shape, F32), jax.ShapeDtypeStruct((1, D), F32)],
        compiler_params=_cp("arbitrary"),
    )(x, dxo, g, wqg, kv, wo)


ATT_SCALE = ATT_HEAD_DIM ** -0.5
ATT_PAIRS = ATT_HEADS // 2
SKEW_LANES = 1024
REL_LANES = 384


def _skew(x, left):
    amt = (ATT_QB - 1) - lax.broadcasted_iota(jnp.int32, (ATT_QB, 1), 0)
    for bit in range(8):
        sh = (SKEW_LANES - (1 << bit)) if left else (1 << bit)
        x = jnp.where(((amt >> bit) & 1) == 1, pltpu.roll(x, sh, 1), x)
    return x


def _dist_onehot(shape, dist_axis):
    j = lax.broadcasted_iota(jnp.int32, shape, dist_axis)
    r = lax.broadcasted_iota(jnp.int32, shape, 1 - dist_axis)
    return (jnp.clip((ATT_WIN - 1) - j, -MAX_REL, MAX_REL) + MAX_REL == r).astype(BF16)


def _dot_exact(v, onehot):
    acc = jnp.zeros((v.shape[0], onehot.shape[1]), F32)
    rem = v
    for _ in range(3):
        part = rem.astype(BF16)
        acc = acc + _dot(part, onehot)
        rem = rem - part.astype(F32)
    return acc


ATT_EDGE = ATT_PAD // ATT_QB


def att_bias(rel_bias):
    H = rel_bias.shape[0]
    rb = jnp.pad(rel_bias, ((0, 0), (0, REL_LANES - rel_bias.shape[1]))).reshape(H, 1, REL_LANES)

    def body(rb_ref, o_ref):
        by_col = _dot_exact(jnp.broadcast_to(rb_ref[...], (8, REL_LANES)), _dist_onehot((REL_LANES, SKEW_LANES), 1))
        x = _skew(jnp.broadcast_to(by_col[0:1, :], (ATT_QB, SKEW_LANES)), left=True)[:, 0:ATT_WIN]
        qc = lax.broadcasted_iota(jnp.int32, (ATT_QB, 1), 0) // CHUNK + LEFT_CHUNKS
        col = lax.broadcasted_iota(jnp.int32, (1, ATT_WIN), 1)
        dc = qc - col // CHUNK
        band = (dc >= 0) & (dc <= LEFT_CHUNKS)
        for blk in range(ATT_EDGE + 1):
            o_ref[blk] = jnp.where(band & (col >= ATT_PAD - blk * ATT_QB), x, NEG)

    return pl.pallas_call(
        body, name="att_bias", grid=(H,),
        in_specs=[pl.BlockSpec((None, 1, REL_LANES), lambda h: (h, 0, 0))],
        out_specs=pl.BlockSpec((ATT_EDGE + 1, None, ATT_QB, ATT_WIN), lambda h: (0, h, 0, 0)),
        out_shape=jax.ShapeDtypeStruct((ATT_EDGE + 1, H, ATT_QB, ATT_WIN), F32), compiler_params=_cp("parallel"),
    )(rb)


def relbias_bwd(dbias):
    H = dbias.shape[0]

    def body(x_ref, o_ref):
        x = jnp.concatenate([x_ref[...], jnp.zeros((ATT_QB, SKEW_LANES - ATT_WIN), F32)], axis=1)
        col = jnp.sum(_skew(x, left=False), axis=0, keepdims=True)
        o_ref[...] = _dot_exact(jnp.broadcast_to(col, (8, SKEW_LANES)), _dist_onehot((SKEW_LANES, REL_LANES), 0))

    out = pl.pallas_call(
        body, name="relbias_bwd", grid=(H,),
        in_specs=[pl.BlockSpec((None, ATT_QB, ATT_WIN), lambda h: (h, 0, 0))],
        out_specs=pl.BlockSpec((None, 8, REL_LANES), lambda h: (h, 0, 0)),
        out_shape=jax.ShapeDtypeStruct((H, 8, REL_LANES), F32), compiler_params=_cp("parallel"),
    )(dbias)
    return out[:, 0, :2 * MAX_REL + 1]


def _ca_scores(qh, kw, bias):
    s = _dot_nt(qh, kw) + bias
    e = jnp.exp(s - jnp.max(s, axis=-1, keepdims=True))
    return e, 1.0 / jnp.sum(e, axis=-1, keepdims=True)


def _ca_head(qv, m):
    return jnp.where(m, qv, jnp.zeros_like(qv)) * ATT_SCALE


def _ca_bias_spec():
    return pl.BlockSpec((None, 2, ATT_QB, ATT_WIN), lambda hp, b: (jnp.minimum(b, ATT_EDGE), hp, 0, 0))


def ca_fwd(q, kvp, gate, bias):
    L, D = q.shape
    Lp = kvp.shape[0]
    nb = L // ATT_QB

    def body(q_ref, k_ref, v_ref, g_ref, b_ref, r_ref, o_ref):
        w = pl.multiple_of(pl.program_id(1) * ATT_QB, ATT_QB)
        kw = k_ref[pl.ds(w, ATT_WIN), :]
        vw = v_ref[pl.ds(w, ATT_WIN), :]
        qv = q_ref[...]
        first = lax.broadcasted_iota(jnp.int32, (1, 128), 1) < ATT_HEAD_DIM
        outs = []
        for hh, m in enumerate((first, jnp.logical_not(first))):
            e, inv = _ca_scores(_ca_head(qv, m), kw, b_ref[hh])
            outs.append(_dot(e.astype(BF16), vw) * inv)
        o = jnp.where(first, outs[0], outs[1])
        r_ref[...] = (o * _silu(g_ref[...])).astype(BF16)
        o_ref[...] = o.astype(BF16)

    blk = pl.BlockSpec((ATT_QB, 128), lambda hp, b: (b, hp))
    return pl.pallas_call(
        body, name="ca_fwd", grid=(ATT_PAIRS, nb),
        in_specs=[blk, pl.BlockSpec((Lp, 128), lambda hp, b: (0, hp)),
                  pl.BlockSpec((Lp, 128), lambda hp, b: (0, ATT_PAIRS + hp)), blk, _ca_bias_spec()],
        out_specs=[blk, blk], out_shape=[jax.ShapeDtypeStruct((L, D), BF16), jax.ShapeDtypeStruct((L, D), BF16)],
        compiler_params=_cp("parallel", "arbitrary"),
    )(q, kvp, kvp, gate, bias)


def ca_bwd(q, kvp, gate, bias, dr, o):
    L, D = q.shape
    Lp = kvp.shape[0]
    nb = L // ATT_QB

    def body(q_ref, k_ref, v_ref, g_ref, b_ref, dr_ref, o_ref, dq_ref, dg_ref, dk_ref, dv_ref, db_ref):
        b = pl.program_id(1)

        @pl.when(b == 0)
        def _():
            for r in (dk_ref, dv_ref, db_ref):
                r[...] = jnp.zeros_like(r)

        w = pl.multiple_of(b * ATT_QB, ATT_QB)
        kw = k_ref[pl.ds(w, ATT_WIN), :]
        vw = v_ref[pl.ds(w, ATT_WIN), :]
        qv = q_ref[...]
        gate_v = g_ref[...]
        drv = dr_ref[...]
        o = o_ref[...].astype(F32)
        do = drv * _silu(gate_v)
        doo = do * o
        first = lax.broadcasted_iota(jnp.int32, (1, 128), 1) < ATT_HEAD_DIM
        dqs = []
        dkw = jnp.zeros((ATT_WIN, 128), F32)
        dvw = jnp.zeros((ATT_WIN, 128), F32)
        for hh, m in enumerate((first, jnp.logical_not(first))):
            qh = _ca_head(qv, m)
            e, inv = _ca_scores(qh, kw, b_ref[hh])
            eb = e.astype(BF16)
            doh = jnp.where(m, do, 0.0)
            dp = _dot_nt(doh.astype(BF16), vw)
            dvw = dvw + _dot_tn(eb, (doh * inv).astype(BF16))
            rs = jnp.sum(jnp.where(m, doo, 0.0), axis=-1, keepdims=True)
            ds = e * ((dp - rs) * inv)
            db_ref[hh] += ds
            dsb = ds.astype(BF16)
            dqs.append(_dot(dsb, kw))
            dkw = dkw + _dot_tn(dsb, qh)
        dg_ref[...] = (drv * o * _dsilu(gate_v)).astype(BF16)
        dq_ref[...] = (jnp.where(first, dqs[0], dqs[1]) * ATT_SCALE).astype(BF16)
        dk_ref[pl.ds(w, ATT_WIN), :] += dkw
        dv_ref[pl.ds(w, ATT_WIN), :] += dvw

    blk = pl.BlockSpec((ATT_QB, 128), lambda hp, b: (b, hp))
    kblk = pl.BlockSpec((Lp, 128), lambda hp, b: (0, hp))
    vblk = pl.BlockSpec((Lp, 128), lambda hp, b: (0, ATT_PAIRS + hp))
    bblk = pl.BlockSpec((2, ATT_QB, ATT_WIN), lambda hp, b: (hp, 0, 0))
    return pl.pallas_call(
        body, name="ca_bwd", grid=(ATT_PAIRS, nb),
        in_specs=[blk, kblk, vblk, blk, _ca_bias_spec(), blk, blk],
        out_specs=[blk, blk, kblk, kblk, bblk],
        out_shape=[jax.ShapeDtypeStruct((L, D), BF16), jax.ShapeDtypeStruct((L, D), BF16),
                   jax.ShapeDtypeStruct((Lp, D), F32), jax.ShapeDtypeStruct((Lp, D), F32),
                   jax.ShapeDtypeStruct(bias.shape[1:], F32)],
        compiler_params=_cp("parallel", "arbitrary"),
    )(q, kvp, kvp, gate, bias, dr, o)


def loss_bwd(x, target, g, tm=ROW_TILE):
    L, D = x.shape
    tm = min(tm, L)

    def body(x_ref, t_ref, g_ref, loss_ref, dx_ref, dg_ref):
        @pl.when(pl.program_id(0) == 0)
        def _():
            loss_ref[...] = jnp.zeros_like(loss_ref)
            dg_ref[...] = jnp.zeros_like(dg_ref)

        xv = x_ref[...]
        gv = g_ref[...]
        _, xhat = _rms_parts(xv)
        err = xhat * gv - t_ref[...]
        loss_ref[...] += 0.5 * jnp.sum(jnp.sum(err * err, axis=-1, keepdims=True), axis=0, keepdims=True) / D
        dx, dg = _rms_bwd(xv, gv, err / D)
        dx_ref[...] = dx
        dg_ref[...] += dg

    row = pl.BlockSpec((tm, D), lambda t: (t, 0))
    return pl.pallas_call(
        body, name="loss_bwd", grid=(L // tm,),
        in_specs=[row, row, _full(g.shape)],
        out_specs=[_full((1, 128)), row, _full((1, D))],
        out_shape=[jax.ShapeDtypeStruct((1, 128), F32), jax.ShapeDtypeStruct((L, D), F32),
                   jax.ShapeDtypeStruct((1, D), F32)],
        compiler_params=_cp("arbitrary"),
    )(x, target, g)


_ADAM_C1 = 1.0 / (1.0 - ADAM_B1 ** ADAM_STEP)
_ADAM_C2 = 1.0 / (1.0 - ADAM_B2 ** ADAM_STEP)


def _adam_update(w, g, m, v):
    mn = ADAM_B1 * m + (1.0 - ADAM_B1) * g
    vn = ADAM_B2 * v + (1.0 - ADAM_B2) * g * g
    delta = -ADAM_LR * ((mn * _ADAM_C1) / (jnp.sqrt(vn * _ADAM_C2) + ADAM_EPS) + ADAM_WD * w)
    return delta, mn, vn


def adamw(w, g, m, v, name, tr=512):
    R, C = w.shape
    tr = min(tr, R)

    def body(w_ref, g_ref, m_ref, v_ref, d_ref, mo_ref, vo_ref):
        d_ref[...], mo_ref[...], vo_ref[...] = _adam_update(w_ref[...], g_ref[...], m_ref[...], v_ref[...])

    blk = pl.BlockSpec((tr, C), lambda i: (i, 0))
    sh = jax.ShapeDtypeStruct((R, C), F32)
    return pl.pallas_call(
        body, name=name, grid=(R // tr,), in_specs=[blk] * 4, out_specs=[blk] * 3,
        out_shape=[sh] * 3, compiler_params=_cp("parallel"),
    )(w, g, m, v)


def adamw_allreduce(gathered, w, m, v, shard, name, slot=None):
    R, C = w.shape
    sharded = slot is None and gathered.shape[2] != C

    def body(s_ref, ga_ref, w_ref, m_ref, v_ref, g_ref, d_ref, mo_ref, vo_ref):
        take = (lambda d: ga_ref[d]) if slot is None else (lambda d: ga_ref[d, slot:slot + R, 0:C])
        g = take(0)
        for d in range(1, N_DEV):
            g = g + take(d)
        g_ref[...] = g
        d_ref[...], mo_ref[...], vo_ref[...] = _adam_update(w_ref[...], g, m_ref[...], v_ref[...])

    blk = pl.BlockSpec((R, C), lambda i, s_ref: (0, 0))
    if slot is not None:
        gblk = pl.BlockSpec(gathered.shape, lambda i, s_ref: (0, 0, 0))
    else:
        gblk = pl.BlockSpec((N_DEV, R, C),
                            (lambda i, s_ref: (0, 0, s_ref[0])) if sharded else (lambda i, s_ref: (0, 0, 0)))
    sh = jax.ShapeDtypeStruct((R, C), F32)
    return pl.pallas_call(
        body, name=name,
        grid_spec=pltpu.PrefetchScalarGridSpec(num_scalar_prefetch=1, grid=(1,), in_specs=[gblk, blk, blk, blk],
                                               out_specs=[blk] * 4),
        out_shape=[sh] * 4, compiler_params=_cp("arbitrary"),
    )(shard, gathered, w, m, v)


LATE = ("od_w_in", "od_w_out", "xa_w_qg", "xa_w_kv", "xa_w_o")
EARLY_GRADS = ("od_w_in", "od_w_out", "xa_w_qg", "xa_w_kv", "xa_w_o", "ev_w_out", "ev_s5_glu_w")


def _reduce_to_chip(gs, names, core, tag):
    from_sibling = sibling_send_other_half(gs, "sibling_send_" + tag)
    return [sum_with_sibling(gi, ri, core, "sum_sibling_" + n) for n, gi, ri in zip(names, gs, from_sibling)]


def local_step(x, mem, target, p, gw, late, place, core):
    row = lambda a: a.reshape(1, -1)
    D = D_MODEL
    L = x.shape[0]
    g, big = {}, {}
    gw = dict(gw)

    z, h0b = norm_mm(x, p["ev_norm_g"], gw["ev_w_in"], [((0, 1, 2, 3), F32, 0)], "ev_in")
    ys, s5_saved, landed = s5_mixer_core_fwd(
        z, p["ev_s5_lambda_re"][0], p["ev_s5_lambda_im"][0], p["ev_s5_log_dt"][0], p["ev_s5_b_re"][0],
        p["ev_s5_b_im"][0], p["ev_s5_c_re"][0], p["ev_s5_c_im"][0], p["ev_s5_d"][0],
        carried=carried_allgather([late[n] for n in LATE]))
    for n, gth in zip(LATE, landed):
        rows = gth.shape[1]
        gw[n] = gth.reshape(N_CHIPS, 2, rows // 2, gth.shape[2]) if n.startswith("xa_") else gth
    memn_b = rms_fwd(mem, row(p["mem_norm_g"]), "mem_norm")
    kvs = [mm_cols(memn_b, gw["xa_w_kv"], l, f"xa_kv{l}", BF16) for l in range(2)]
    conv_w = p["ev_conv_w"][0]
    c = conv_fwd(z, conv_w, p["ev_conv_b"])
    tail = (gw["ev_s5_glu_w"], p["ev_s5_glu_b"], p["ev_conv_ln_g"], p["ev_conv_ln_b"], gw["ev_w_out"])
    x1 = ev_tail_fwd(ys, z, c, x, *tail)
    xa0 = (row(p["xa_norm_g"][0]), gw["xa_w_qg"], kvs[0], gw["xa_w_o"], 0)
    x2 = xa_fwd(x1, *xa0, "xa_fwd0")

    q, kvp, gate, h1b = norm_mm(x2, p["od_norm_g"], gw["od_w_in"],
                                [((0,), BF16, 0), ((1, 2), BF16, ATT_PAD), ((3,), F32, 0)], "od_in")
    kvp = zero_rows(kvp, ATT_PAD, "od_kv_pad")
    bias = att_bias(p["od_rel_bias"][0])
    r, att_o = ca_fwd(q, kvp, gate, bias)
    x3 = mm_res(r, gw["od_w_out"], x2, "od_out")
    xa1 = (row(p["xa_norm_g"][1]), gw["xa_w_qg"], kvs[1], gw["xa_w_o"], 1)
    x4 = xa_fwd(x3, *xa1, "xa_fwd1")

    loss, dx4, dgf = loss_bwd(x4, target, row(p["final_norm_g"]))
    g["final_norm_g"] = dgf.reshape(D)

    dx3, dqg1, hx1, rx1, dkv1, dgxa1 = xa_bwd(x3, dx4, *xa1, "xa_bwd1")
    dwqg = mm_tn(hx1, dqg1, "xa_dwqg1", ("cols", 1))
    dwo = mm_tn(rx1, dx4, "xa_dwo1", ("rows", 1))

    big["od_w_out"] = mm_tn(r, dx3, "od_dwout", ("rows",))
    dr = mm_nt_rows(dx3, gw["od_w_out"], "od_out_bwd")
    dq, dgate, dkp, dvp, dbias = ca_bwd(q, kvp, gate, bias, dr, att_o)
    pieces, offs = (dq, dkp, dvp, dgate), (0, ATT_PAD, ATT_PAD, 0)
    dwin = None
    for s in range(N_CHIPS):
        dwin = mm_tn(h1b, pieces[s], f"od_dwin{s}", ("slab", s), into=dwin, b_off=offs[s],
                     bl=ATT_PAD if offs[s] else 1024)
    big["od_w_in"] = dwin
    dx2, dgod = mm_nt_normbwd(pieces, offs, gw["od_w_in"], x2, p["od_norm_g"], dx3, "od_in_bwd")
    g["od_norm_g"] = dgod
    g["od_rel_bias"] = relbias_bwd(dbias)[None]

    dx1, dqg0, hx0, rx0, dkv0, dgxa0 = xa_bwd(x1, dx2, *xa0, "xa_bwd0")
    big["xa_w_qg"] = mm_tn(hx0, dqg0, "xa_dwqg0", ("cols", 0), into=dwqg)
    big["xa_w_o"] = mm_tn(rx0, dx2, "xa_dwo0", ("rows", 0), into=dwo)
    g["xa_norm_g"] = jnp.concatenate([dgxa0, dgxa1], axis=0)

    dys, dc, dz, ra, z1b, dtb, dbglu, dlng, dlnb = ev_tail_bwd(ys, z, c, dx1, *tail)
    big["ev_w_out"] = mm_tn(ra, dx1, "ev_dwout", ("rows",))
    big["ev_s5_glu_w"] = mm_tn(z1b, dtb, "ev_dwglu", ("rows",))
    g["ev_s5_glu_b"], g["ev_conv_ln_g"], g["ev_conv_ln_b"] = dbglu, dlng, dlnb
    dz, dconvw, dconvb = conv_bwd(z, dc, dz, conv_w)
    g["ev_conv_w"] = dconvw[None, :CONV_KERNEL]
    g["ev_conv_b"] = dconvb

    dwkv = mm_tn(memn_b, dkv1, "xa_dwkv1", ("cols", 1), bl=MEM_LEN)
    big["xa_w_kv"] = mm_tn(memn_b, dkv0, "xa_dwkv0", ("cols", 0), into=dwkv, bl=MEM_LEN)
    dmem0 = mm_nt_cols(dkv0, gw["xa_w_kv"], 0, "xa_kv_bwd0")
    dmem1 = mm_nt_cols(dkv1, gw["xa_w_kv"], 1, "xa_kv_bwd1")
    g["mem_norm_g"] = rms_dgain(mem, dmem0, dmem1, "mem_norm_bwd").reshape(D)

    shard_major = lambda t: t.reshape((-1,) + t.shape[-2:])
    chip_sums = _reduce_to_chip([shard_major(big[n]) for n in EARLY_GRADS], EARLY_GRADS, core, "early")
    dz, s5g, from_chips = s5_mixer_core_bwd(z, dys, dz, p["ev_s5_lambda_re"][0], p["ev_s5_lambda_im"][0], s5_saved,
                                            carried=carried_chips_exchange(chip_sums))
    reduced = {n: sum_chips(ci, ri, place, "sum_chips_" + n) for n, ci, ri in zip(EARLY_GRADS, chip_sums, from_chips)}
    for n, v in s5g.items():
        g["ev_s5_" + n] = v[None]
    dwin_ev = mm_tn(h0b, dz, "ev_dwin", ("cols",))
    grad_x, dgev = mm_nt_normbwd((dz,), (0,), gw["ev_w_in"], x, p["ev_norm_g"], dx1, "ev_in_bwd")
    g["ev_norm_g"] = dgev
    chip_sum = _reduce_to_chip([dwin_ev], ["ev_w_in"], core, "last")
    reduced["ev_w_in"] = sum_chips(chip_sum[0], chips_exchange(chip_sum)[0], place, "sum_chips_ev_w_in")
    return loss, grad_x, g, reduced


def _me():
    return lax.axis_index("x"), lax.axis_index("y"), lax.axis_index("c")


def _other_chips(x, y):
    return [(1 - x, y), (x, 1 - y), (1 - x, 1 - y)]


def _remote(src, dst, send_sems, recv_sems, k, to):
    return pltpu.make_async_remote_copy(src_ref=src, dst_ref=dst, send_sem=send_sems.at[k], recv_sem=recv_sems.at[k],
                                        device_id=to, device_id_type=MESH)


def _rows_half(ref, h):
    H = ref.shape[-2] // 2
    return ref.at[(slice(None),) * (len(ref.shape) - 2) + (pl.ds(h * H, H), slice(None))]


def allgather_chip_blocks(halved, whole):
    nh, nw = len(halved), len(whole)
    n = nh + nw

    def body(*refs):
        ins, outs = refs[:n], refs[n:2 * n]
        send_sems, recv_sems, local_sems = refs[2 * n:]
        x, y, c = _me()
        sib = (x, y, 1 - c)
        chips = _other_chips(x, y)
        me = 2 * x + y
        local = [pltpu.make_async_copy(ins[i], outs[i].at[me], local_sems.at[i]) for i in range(n)]
        for cp in local:
            cp.start()
        first, passed = [], []
        for i in range(n):
            for j, (cx, cy) in enumerate(chips):
                if i < nh:
                    src, dst = _rows_half(ins[i], c), _rows_half(outs[i].at[me], c)
                    k = 6 * i + j
                else:
                    src, dst = ins[i], outs[i].at[me]
                    k = 6 * nh + 3 * (i - nh) + j
                first.append(_remote(src, dst, send_sems, recv_sems, k, (cx, cy, c)))
        for cp in first:
            cp.start()
        for j, (cx, cy) in enumerate(chips):
            for i in range(nh):
                got = _rows_half(outs[i].at[2 * cx + cy], c)
                _remote(got, got, send_sems, recv_sems, 6 * i + j, (cx, cy, c)).wait_recv()
                fw = _remote(got, got, send_sems, recv_sems, 6 * i + 3 + j, sib)
                fw.start()
                passed.append(fw)
        for j, (cx, cy) in enumerate(chips):
            for i in range(nh):
                got = _rows_half(outs[i].at[2 * cx + cy], 1 - c)
                _remote(got, got, send_sems, recv_sems, 6 * i + 3 + j, sib).wait_recv()
            for i in range(nh, n):
                got = outs[i].at[2 * cx + cy]
                _remote(got, got, send_sems, recv_sems, 6 * nh + 3 * (i - nh) + j, (cx, cy, c)).wait_recv()
        for cp in first + passed:
            cp.wait_send()
        for cp in local:
            cp.wait()

    arrays = list(halved) + list(whole)
    nsem = 6 * nh + 3 * nw
    return pl.pallas_call(
        body, name="allgather_chip_blocks", in_specs=[ANY] * n, out_specs=[ANY] * n,
        out_shape=[jax.ShapeDtypeStruct((N_CHIPS,) + a.shape, a.dtype) for a in arrays],
        scratch_shapes=[pltpu.SemaphoreType.DMA((nsem,)), pltpu.SemaphoreType.DMA((nsem,)),
                        pltpu.SemaphoreType.DMA((n,))],
    )(*arrays)


def allgather_devices(vs):
    n = len(vs)

    def body(*refs):
        ins, outs = refs[:n], refs[n:2 * n]
        send_sems, recv_sems, local_sems = refs[2 * n:]
        x, y, c = _me()
        sib = (x, y, 1 - c)
        chips = _other_chips(x, y)
        me = 4 * x + 2 * y + c
        local = [pltpu.make_async_copy(ins[i], outs[i].at[me], local_sems.at[i]) for i in range(n)]
        for cp in local:
            cp.start()
        first, passed = [], []
        for i in range(n):
            first.append(_remote(ins[i], outs[i].at[me], send_sems, recv_sems, 7 * i, sib))
            for j, (cx, cy) in enumerate(chips):
                first.append(_remote(ins[i], outs[i].at[me], send_sems, recv_sems, 7 * i + 1 + j, (cx, cy, c)))
        for cp in first:
            cp.start()
        for j, (cx, cy) in enumerate(chips):
            for i in range(n):
                got = outs[i].at[4 * cx + 2 * cy + c]
                _remote(got, got, send_sems, recv_sems, 7 * i + 1 + j, (cx, cy, c)).wait_recv()
                fw = _remote(got, got, send_sems, recv_sems, 7 * i + 4 + j, sib)
                fw.start()
                passed.append(fw)
        for i in range(n):
            got = outs[i].at[4 * x + 2 * y + (1 - c)]
            _remote(got, got, send_sems, recv_sems, 7 * i, sib).wait_recv()
            for j, (cx, cy) in enumerate(chips):
                got = outs[i].at[4 * cx + 2 * cy + (1 - c)]
                _remote(got, got, send_sems, recv_sems, 7 * i + 4 + j, sib).wait_recv()
        for cp in first + passed:
            cp.wait_send()
        for cp in local:
            cp.wait()

    return pl.pallas_call(
        body, name="allgather_devices", in_specs=[ANY] * n, out_specs=[ANY] * n,
        out_shape=[jax.ShapeDtypeStruct((N_DEV,) + v.shape, v.dtype) for v in vs],
        scratch_shapes=[pltpu.SemaphoreType.DMA((7 * n,)), pltpu.SemaphoreType.DMA((7 * n,)),
                        pltpu.SemaphoreType.DMA((n,))],
    )(*vs)


def sibling_send_other_half(gs, name):
    n = len(gs)

    def body(*refs):
        ins, outs = refs[:n], refs[n:2 * n]
        send_sems, recv_sems = refs[2 * n:]
        x, y, c = _me()
        cps = [_remote(_rows_half(ins[i], 1 - c), outs[i], send_sems, recv_sems, i, (x, y, 1 - c)) for i in range(n)]
        for cp in cps:
            cp.start()
        for cp in cps:
            cp.wait()

    return pl.pallas_call(
        body, name=name, in_specs=[ANY] * n, out_specs=[ANY] * n,
        out_shape=[jax.ShapeDtypeStruct((g.shape[0], g.shape[1] // 2, g.shape[2]), g.dtype) for g in gs],
        scratch_shapes=[pltpu.SemaphoreType.DMA((n,)), pltpu.SemaphoreType.DMA((n,))],
    )(*gs)


def chips_exchange(parts):
    n = len(parts)

    def body(*refs):
        ins, outs = refs[:n], refs[n:2 * n]
        send_sems, recv_sems = refs[2 * n:]
        x, y, c = _me()
        cps = []
        for i in range(n):
            nl = ins[i].shape[0] // N_CHIPS
            for j, (cx, cy) in enumerate(_other_chips(x, y)):
                cps.append(_remote(ins[i].at[pl.ds((2 * cx + cy) * nl, nl)], outs[i].at[j], send_sems, recv_sems,
                                   3 * i + j, (cx, cy, c)))
        for cp in cps:
            cp.start()
        for cp in cps:
            cp.wait()

    return pl.pallas_call(
        body, name="chips_exchange", in_specs=[ANY] * n, out_specs=[ANY] * n,
        out_shape=[jax.ShapeDtypeStruct((3, a.shape[0] // N_CHIPS) + a.shape[1:], a.dtype) for a in parts],
        scratch_shapes=[pltpu.SemaphoreType.DMA((3 * n,)), pltpu.SemaphoreType.DMA((3 * n,))],
    )(*parts)


def sibling_share(fulls):
    n = len(fulls)

    def body(*refs):
        outs = refs[n:2 * n]
        send_sems, recv_sems = refs[2 * n:]
        x, y, c = _me()
        cps = [_remote(_rows_half(outs[i], c), _rows_half(outs[i], c), send_sems, recv_sems, i, (x, y, 1 - c))
               for i in range(n)]
        for cp in cps:
            cp.start()
        for i in range(n):
            got = _rows_half(outs[i], 1 - c)
            _remote(got, got, send_sems, recv_sems, i, (x, y, 1 - c)).wait_recv()
        for cp in cps:
            cp.wait_send()

    return pl.pallas_call(
        body, name="sibling_share", in_specs=[ANY] * n, out_specs=[ANY] * n,
        out_shape=[jax.ShapeDtypeStruct(f.shape, f.dtype) for f in fulls],
        input_output_aliases={i: i for i in range(n)},
        scratch_shapes=[pltpu.SemaphoreType.DMA((n,)), pltpu.SemaphoreType.DMA((n,))],
    )(*fulls)


def sum_with_sibling(g, recv, core, name):
    S, H, C = recv.shape
    tr = min(512, H)

    def body(c_ref, g_ref, r_ref, o_ref):
        o_ref[...] = (g_ref[...].astype(F32) + r_ref[...].astype(F32)).astype(o_ref.dtype)

    nb = H // tr
    return pl.pallas_call(
        body, name=name,
        grid_spec=pltpu.PrefetchScalarGridSpec(
            num_scalar_prefetch=1, grid=(S, nb),
            in_specs=[pl.BlockSpec((None, tr, C), lambda s, i, c_ref: (s, c_ref[0] * nb + i, 0)),
                      pl.BlockSpec((None, tr, C), lambda s, i, c_ref: (s, i, 0))],
            out_specs=pl.BlockSpec((None, tr, C), lambda s, i, c_ref: (s, i, 0))),
        out_shape=jax.ShapeDtypeStruct((S, H, C), g.dtype), compiler_params=_cp("parallel", "parallel"),
    )(core, g, recv)


def sum_chips(a, recv, place, name):
    _, nl, H, C = recv.shape
    tr = min(512, H)
    nb = H // tr

    def body(p_ref, a_ref, r_ref, o_ref):
        acc = a_ref[...].astype(F32)
        for j in range(3):
            acc = acc + r_ref[j].astype(F32)
        o_ref[...] = acc

    return pl.pallas_call(
        body, name=name,
        grid_spec=pltpu.PrefetchScalarGridSpec(
            num_scalar_prefetch=1, grid=(nl, nb),
            in_specs=[pl.BlockSpec((None, tr, C), lambda l, i, p_ref: (p_ref[0] * nl + l, i, 0)),
                      pl.BlockSpec((3, None, tr, C), lambda l, i, p_ref: (0, l, i, 0))],
            out_specs=pl.BlockSpec((None, tr, C), lambda l, i, p_ref: (l, p_ref[1] * nb + i, 0))),
        out_shape=jax.ShapeDtypeStruct((nl, 2 * H, C), F32), compiler_params=_cp("parallel", "parallel"),
    )(place, a, recv)


def pack_rows(arrays, name):
    starts, r0 = [], 0
    for a in arrays:
        if a.shape[0] >= SUBLANES:
            r0 = -(-r0 // SUBLANES) * SUBLANES
        starts.append(r0)
        r0 += a.shape[0]
    r0 = -(-r0 // SUBLANES) * SUBLANES
    n = len(arrays)

    def body(*refs):
        o_ref = refs[n]
        o_ref[...] = jnp.zeros_like(o_ref)
        for a_ref, s in zip(refs[:n], starts):
            r, c = a_ref.shape
            o_ref[s:s + r, 0:c] = a_ref[...]

    out = pl.pallas_call(body, name=name, out_shape=jax.ShapeDtypeStruct((r0, PACK_COLS), F32))(*arrays)
    return out, starts


def sum_slot(gathered, slot, shape, name):
    r, c = shape

    def body(ga_ref, o_ref):
        acc = ga_ref[0, slot:slot + r, 0:c]
        for d in range(1, N_DEV):
            acc = acc + ga_ref[d, slot:slot + r, 0:c]
        o_ref[...] = acc

    return pl.pallas_call(body, name=name, out_shape=jax.ShapeDtypeStruct((r, c), F32))(gathered)


def carried_allgather(blocks):
    n = len(blocks)

    def first_hop(ins, outs, sems, i, j, chip, x, y, c):
        me = 2 * x + y
        return _remote(_rows_half(ins[i], c), _rows_half(outs[i].at[me], c), sems[0], sems[1], 6 * i + j, (*chip, c))

    def start(ins, outs, sems):
        x, y, c = _me()
        for i in range(n):
            pltpu.make_async_copy(ins[i], outs[i].at[2 * x + y], sems[2].at[i]).start()
        for i in range(n):
            for j, chip in enumerate(_other_chips(x, y)):
                first_hop(ins, outs, sems, i, j, chip, x, y, c).start()

    def finish(ins, outs, sems):
        x, y, c = _me()
        sib = (x, y, 1 - c)
        chips = _other_chips(x, y)
        passed = []
        for j, (cx, cy) in enumerate(chips):
            for i in range(n):
                got = _rows_half(outs[i].at[2 * cx + cy], c)
                _remote(got, got, sems[0], sems[1], 6 * i + j, (cx, cy, c)).wait_recv()
                fw = _remote(got, got, sems[0], sems[1], 6 * i + 3 + j, sib)
                fw.start()
                passed.append(fw)
        for j, (cx, cy) in enumerate(chips):
            for i in range(n):
                got = _rows_half(outs[i].at[2 * cx + cy], 1 - c)
                _remote(got, got, sems[0], sems[1], 6 * i + 3 + j, sib).wait_recv()
        for i in range(n):
            for j, chip in enumerate(chips):
                first_hop(ins, outs, sems, i, j, chip, x, y, c).wait_send()
        for fw in passed:
            fw.wait_send()
        for i in range(n):
            pltpu.make_async_copy(ins[i], outs[i].at[2 * x + y], sems[2].at[i]).wait()

    return Carried(blocks, [jax.ShapeDtypeStruct((N_CHIPS,) + b.shape, b.dtype) for b in blocks],
                   [pltpu.SemaphoreType.DMA((6 * n,)), pltpu.SemaphoreType.DMA((6 * n,)), pltpu.SemaphoreType.DMA((n,))],
                   start, None, finish)


def carried_chips_exchange(parts):
    n = len(parts)

    def copies(ins, outs, sems):
        x, y, c = _me()
        cps = []
        for i in range(n):
            nl = ins[i].shape[0] // N_CHIPS
            for j, (cx, cy) in enumerate(_other_chips(x, y)):
                cps.append(_remote(ins[i].at[pl.ds((2 * cx + cy) * nl, nl)], outs[i].at[j], sems[0], sems[1],
                                   3 * i + j, (cx, cy, c)))
        return cps

    def start(ins, outs, sems):
        for cp in copies(ins, outs, sems):
            cp.start()

    def finish(ins, outs, sems):
        for cp in copies(ins, outs, sems):
            cp.wait()

    return Carried(parts, [jax.ShapeDtypeStruct((3, a.shape[0] // N_CHIPS) + a.shape[1:], a.dtype) for a in parts],
                   [pltpu.SemaphoreType.DMA((3 * n,)), pltpu.SemaphoreType.DMA((3 * n,))], start, None, finish)


BIG = ("ev_w_in", "ev_s5_glu_w", "ev_w_out", "od_w_in", "od_w_out", "xa_w_qg", "xa_w_kv", "xa_w_o")
SHARDED_F32 = (("ev_conv_w", 2), ("od_norm_g", 1))
SMALL = ("mem_norm_g", "ev_norm_g", "ev_s5_lambda_re", "ev_s5_lambda_im", "ev_s5_log_dt", "ev_s5_b_re", "ev_s5_b_im",
         "ev_s5_c_re", "ev_s5_c_im", "ev_s5_d", "ev_s5_glu_b", "ev_conv_b", "ev_conv_ln_g", "ev_conv_ln_b",
         "od_rel_bias", "xa_norm_g", "final_norm_g")
NARROW = ("ev_s5_c_re", "ev_s5_c_im")
DENSE_B = ("ev_s5_b_re", "ev_s5_b_im")
PACK_COLS = 1024
WEIGHTS = ("mem_norm_g", "ev_norm_g", "ev_w_in", "ev_s5_lambda_re", "ev_s5_lambda_im", "ev_s5_log_dt", "ev_s5_b_re",
           "ev_s5_b_im", "ev_s5_c_re", "ev_s5_c_im", "ev_s5_d", "ev_s5_glu_w", "ev_s5_glu_b", "ev_conv_w", "ev_conv_b",
           "ev_conv_ln_g", "ev_conv_ln_b", "ev_w_out", "od_norm_g", "od_w_in", "od_rel_bias", "od_w_out", "xa_norm_g",
           "xa_w_qg", "xa_w_kv", "xa_w_o", "final_norm_g")


def _as2d(a):
    return a.reshape(1, -1) if a.ndim == 1 else a.reshape(-1, a.shape[-1])


def kernel(x, mem, mem_norm_g, ev_norm_g, ev_w_in, ev_s5_lambda_re, ev_s5_lambda_im, ev_s5_log_dt, ev_s5_b_re, ev_s5_b_im, ev_s5_c_re, ev_s5_c_im, ev_s5_d, ev_s5_glu_w, ev_s5_glu_b, ev_conv_w, ev_conv_b, ev_conv_ln_g, ev_conv_ln_b, ev_w_out, od_norm_g, od_w_in, od_rel_bias, od_w_out, xa_norm_g, xa_w_qg, xa_w_kv, xa_w_o, final_norm_g, loss_target, m_mem_norm_g, m_ev_norm_g, m_ev_w_in, m_ev_s5_lambda_re, m_ev_s5_lambda_im, m_ev_s5_log_dt, m_ev_s5_b_re, m_ev_s5_b_im, m_ev_s5_c_re, m_ev_s5_c_im, m_ev_s5_d, m_ev_s5_glu_w, m_ev_s5_glu_b, m_ev_conv_w, m_ev_conv_b, m_ev_conv_ln_g, m_ev_conv_ln_b, m_ev_w_out, m_od_norm_g, m_od_w_in, m_od_rel_bias, m_od_w_out, m_xa_norm_g, m_xa_w_qg, m_xa_w_kv, m_xa_w_o, m_final_norm_g, v_mem_norm_g, v_ev_norm_g, v_ev_w_in, v_ev_s5_lambda_re, v_ev_s5_lambda_im, v_ev_s5_log_dt, v_ev_s5_b_re, v_ev_s5_b_im, v_ev_s5_c_re, v_ev_s5_c_im, v_ev_s5_d, v_ev_s5_glu_w, v_ev_s5_glu_b, v_ev_conv_w, v_ev_conv_b, v_ev_conv_ln_g, v_ev_conv_ln_b, v_ev_w_out, v_od_norm_g, v_od_w_in, v_od_rel_bias, v_od_w_out, v_xa_norm_g, v_xa_w_qg, v_xa_w_kv, v_xa_w_o, v_final_norm_g):
    a = dict(locals())
    w = {n: a[n] for n in WEIGHTS}
    shard = (2 * lax.axis_index("x") + lax.axis_index("y")).reshape(1).astype(jnp.int32)
    core = lax.axis_index("c").reshape(1).astype(jnp.int32)

    place = jnp.concatenate([shard, core])

    blocks = {n: w[n].astype(BF16).reshape(-1, w[n].shape[-1]) for n in BIG}
    early = [n for n in BIG if n not in LATE]
    gathered = allgather_chip_blocks([blocks[n] for n in early], [_as2d(w[n]) for n, _ in SHARDED_F32])
    gw = dict(zip(early, gathered))
    p = {n: w[n] for n in SMALL}
    conv_g, odn_g = gathered[len(early):]
    p["ev_conv_w"] = jnp.concatenate([conv_g[s] for s in range(N_CHIPS)], axis=1)[None]
    p["od_norm_g"] = odn_g.reshape(1, D_MODEL)

    loss, grad_x, g, reduced = local_step(x[0], mem[0], loss_target[0], p, gw, {n: blocks[n] for n in LATE},
                                          place, core)
    loss = lax.psum(loss[0, 0], ("x", "y", "c"))
    g_big = dict(zip(BIG, sibling_share([reduced[n] for n in BIG])))

    out = {tag: {} for tag in ("grad", "delta", "m", "v")}
    for n in BIG:
        sh = w[n].shape
        to2d = lambda t: t.reshape(-1, sh[-1])
        gn = to2d(g_big[n])
        d, mn, vn = adamw(to2d(w[n]), gn, to2d(a["m_" + n]), to2d(a["v_" + n]), "adamw_" + n)
        for tag, val in zip(("grad", "delta", "m", "v"), (gn, d, mn, vn)):
            out[tag][n] = val.reshape(sh)

    packed_names = [n for n in SMALL if n not in NARROW]
    single_names = list(NARROW) + [n for n, _ in SHARDED_F32]
    packed, slots = pack_rows([_as2d(g[n]) for n in packed_names], "pack_small_grads")
    gath = allgather_devices([packed] + [_as2d(g[n]) for n in single_names])
    jobs = [(n, gath[0], s) for n, s in zip(packed_names, slots)]
    jobs += [(n, gt, None) for n, gt in zip(single_names, gath[1:])]
    for n, gt, slot in jobs:
        sh = w[n].shape
        w2, m2, v2 = _as2d(w[n]), _as2d(a["m_" + n]), _as2d(a["v_" + n])
        if n in DENSE_B:
            gn = _as2d(s5_b_from_dense(sum_slot(gt, slot, g[n].shape[-2:], "sum_" + n)))
            d, mn, vn = adamw(w2, gn, m2, v2, "adamw_" + n)
        else:
            gn, d, mn, vn = adamw_allreduce(gt, w2, m2, v2, shard, "adamw_" + n, slot=slot)
        for tag, val in zip(("grad", "delta", "m", "v"), (gn, d, mn, vn)):
            out[tag][n] = val.reshape(sh)

    res = [loss, grad_x[None]]
    for tag in ("grad", "delta", "m", "v"):
        res += [out[tag][n] for n in WEIGHTS]
    return tuple(res)
```

```python
import math

import jax
import jax.numpy as jnp
import numpy as np
from jax import lax
from jax.experimental import pallas as pl
from jax.experimental.pallas import tpu as pltpu

F32 = jnp.float32
BF16 = jnp.bfloat16

D_MODEL = 1024
CHUNK = 64
LEFT_CHUNKS = 8
S5_WIDTH = 512
S5_GROUP = 16
S5_GROUPS = 32
S5_STATE = 64
S5_COLS = S5_GROUPS * S5_STATE
S5_SPLIT = 4
S5_CC = S5_COLS // S5_SPLIT
S5_UC = S5_WIDTH // S5_SPLIT
CONV_WIDTH = 512
CONV_KERNEL = 31
CONV_HALO = 32
ATT_HEADS = 16
ATT_HEAD_DIM = 64
MAX_REL = 128
MEM_LEN = 256
XA_HEADS = 4
XA_HEAD_DIM = 256
EPS = 1e-6
EVEN_IN = 2560
ODD_IN = 4096

ADAM_LR = 0.001
ADAM_B1 = 0.9
ADAM_B2 = 0.999
ADAM_EPS = 1e-08
ADAM_WD = 0.01
ADAM_STEP = 10

ROW_TILE = 256
MM_TILE = 512
S5_TILE = 512
ATT_QB = 256
ATT_PAD = LEFT_CHUNKS * CHUNK
ATT_WIN = ATT_PAD + ATT_QB
VMEM_LIMIT_V7X = 56 * 1024 * 1024
NEG = -1e30
LANES = 128
N_CHIPS = 4
N_DEV = 8

MESH = pl.DeviceIdType.MESH
ANY = pl.BlockSpec(memory_space=pl.ANY)


def _cp(*sem, vmem=VMEM_LIMIT_V7X):
    return pltpu.CompilerParams(dimension_semantics=sem if sem else None, vmem_limit_bytes=vmem)


def _full(shape):
    n = len(shape)
    return pl.BlockSpec(shape, lambda *_: (0,) * n)


def _wspec(w, layer=None):
    if layer is None:
        return _full(w.shape)
    s, _, r, c = w.shape
    return pl.BlockSpec((s, None, r, c), lambda *_: (0, layer, 0, 0))


def _lane_tile(n, cap):
    return max(t for t in range(LANES, min(n, cap) + 1, LANES) if n % t == 0)


def _sigmoid(x):
    return 1.0 / (1.0 + jnp.exp(-x))


def _silu(x):
    return x * _sigmoid(x)


def _silu_pair(x):
    s = _sigmoid(x)
    return x * s, s * (1.0 + x * (1.0 - s))


_GELU_C = math.sqrt(2.0 / math.pi)


def _gelu(x):
    return 0.5 * x * (1.0 + jnp.tanh(_GELU_C * (x + 0.044715 * x * x * x)))


def _dgelu(x):
    t = jnp.tanh(_GELU_C * (x + 0.044715 * x * x * x))
    return 0.5 * (1.0 + t) + 0.5 * x * (1.0 - t * t) * _GELU_C * (1.0 + 3.0 * 0.044715 * x * x)


def _dot(a, b):
    return jnp.dot(a, b, preferred_element_type=F32)


def _dot_nt(a, b):
    return lax.dot_general(a, b, (((1,), (1,)), ((), ())), preferred_element_type=F32)


def _dot_tn(a, b):
    return lax.dot_general(a, b, (((0,), (0,)), ((), ())), preferred_element_type=F32)


def _dot_cols(a, w4, shards=range(N_CHIPS)):
    return jnp.concatenate([_dot(a, w4[s]) for s in shards], axis=1)


def _dot_rows(a, w4):
    r = w4.shape[1]
    acc = _dot(a[:, 0:r], w4[0])
    for s in range(1, N_CHIPS):
        acc = acc + _dot(a[:, s * r:(s + 1) * r], w4[s])
    return acc


def _dot_nt_cols(dys, w4):
    acc = _dot_nt(dys[0], w4[0])
    for s in range(1, N_CHIPS):
        acc = acc + _dot_nt(dys[s], w4[s])
    return acc


def _dot_nt_rows(dy, w4):
    return jnp.concatenate([_dot_nt(dy, w4[s]) for s in range(N_CHIPS)], axis=1)


def _col_pieces(v, n):
    return [v[:, s * n:(s + 1) * n] for s in range(N_CHIPS)]


def _rms_parts(xv):
    inv = lax.rsqrt(jnp.mean(xv * xv, axis=-1, keepdims=True) + EPS)
    return inv, xv * inv


def _rms_bwd(xv, g, dh):
    inv, xhat = _rms_parts(xv)
    dg = jnp.sum(dh * xhat, axis=0, keepdims=True)
    dxh = dh * g
    dx = inv * (dxh - xhat * jnp.mean(dxh * xhat, axis=-1, keepdims=True))
    return dx, dg


def norm_mm(x, g, w4, groups, name, tm=MM_TILE):
    M, D = x.shape
    n = w4.shape[2]
    tm = min(tm, M)

    def body(x_ref, g_ref, w_ref, *outs):
        _, xhat = _rms_parts(x_ref[...])
        hb = (xhat * g_ref[...]).astype(BF16)
        for o, (shards, dt, _) in zip(outs, groups):
            o[...] = _dot_cols(hb, w_ref, shards).astype(dt)
        outs[-1][...] = hb

    out_shape = [jax.ShapeDtypeStruct((M + pad, len(sh) * n), dt) for (sh, dt, pad) in groups]
    out_specs = [pl.BlockSpec((tm, len(sh) * n), lambda i, p=pad // tm: (i + p, 0)) for (sh, _, pad) in groups]
    out_shape.append(jax.ShapeDtypeStruct((M, D), BF16))
    out_specs.append(pl.BlockSpec((tm, D), lambda i: (i, 0)))
    return pl.pallas_call(
        body, name=name, grid=(M // tm,),
        in_specs=[pl.BlockSpec((tm, D), lambda i: (i, 0)), _full(g.shape), _full(w4.shape)],
        out_specs=out_specs, out_shape=out_shape, compiler_params=_cp("parallel"),
    )(x, g, w4)


def zero_rows(buf, rows, name, tm=ROW_TILE):
    C = buf.shape[1]

    def body(b_ref, o_ref):
        o_ref[...] = jnp.zeros_like(o_ref)

    return pl.pallas_call(
        body, name=name, grid=(rows // tm,), in_specs=[ANY],
        out_specs=pl.BlockSpec((tm, C), lambda i: (i, 0)),
        out_shape=jax.ShapeDtypeStruct(buf.shape, buf.dtype), input_output_aliases={0: 0},
        compiler_params=_cp("parallel"),
    )(buf)


def mm_res(a, w4, res, name, tm=MM_TILE):
    M, K = a.shape
    N = w4.shape[2]
    tm = min(tm, M)

    def body(a_ref, w_ref, r_ref, o_ref):
        o_ref[...] = r_ref[...] + _dot_rows(a_ref[...], w_ref)

    return pl.pallas_call(
        body, name=name, grid=(M // tm,),
        in_specs=[pl.BlockSpec((tm, K), lambda i: (i, 0)), _full(w4.shape), pl.BlockSpec((tm, N), lambda i: (i, 0))],
        out_specs=pl.BlockSpec((tm, N), lambda i: (i, 0)),
        out_shape=jax.ShapeDtypeStruct((M, N), F32), compiler_params=_cp("parallel"),
    )(a, w4, res)


def mm_cols(a, w, layer, name, out_dtype):
    M = a.shape[0]
    n = w.shape[3]

    def body(a_ref, w_ref, o_ref):
        o_ref[...] = _dot_cols(a_ref[...], w_ref).astype(out_dtype)

    return pl.pallas_call(
        body, name=name, grid=(1,), in_specs=[_full(a.shape), _wspec(w, layer)],
        out_specs=_full((M, N_CHIPS * n)), out_shape=jax.ShapeDtypeStruct((M, N_CHIPS * n), out_dtype),
        compiler_params=_cp("arbitrary"),
    )(a, w)


def mm_nt_cols(dy, w, layer, name):
    M = dy.shape[0]
    K, n = w.shape[2], w.shape[3]

    def body(d_ref, w_ref, o_ref):
        o_ref[...] = _dot_nt_cols(_col_pieces(d_ref[...].astype(BF16), n), w_ref)

    return pl.pallas_call(
        body, name=name, grid=(1,), in_specs=[_full(dy.shape), _wspec(w, layer)],
        out_specs=_full((M, K)), out_shape=jax.ShapeDtypeStruct((M, K), F32), compiler_params=_cp("arbitrary"),
    )(dy, w)


def mm_nt_rows(dy, w4, name, tm=MM_TILE):
    M, N = dy.shape
    K = N_CHIPS * w4.shape[1]
    tm = min(tm, M)

    def body(d_ref, w_ref, o_ref):
        o_ref[...] = _dot_nt_rows(d_ref[...].astype(BF16), w_ref)

    return pl.pallas_call(
        body, name=name, grid=(M // tm,),
        in_specs=[pl.BlockSpec((tm, N), lambda i: (i, 0)), _full(w4.shape)],
        out_specs=pl.BlockSpec((tm, K), lambda i: (i, 0)),
        out_shape=jax.ShapeDtypeStruct((M, K), F32), compiler_params=_cp("parallel"),
    )(dy, w4)


def mm_nt_normbwd(dys, offs, w4, x, g, dx_out, name, tm=MM_TILE):
    M, D = x.shape
    n = w4.shape[2]
    tm = min(tm, M)
    nd = len(dys)

    def body(*refs):
        d_refs = refs[:nd]
        w_ref, x_ref, g_ref, dxo_ref, dx_ref, dg_ref = refs[nd:]
        if nd == 1:
            pieces = _col_pieces(d_refs[0][...].astype(BF16), n)
        else:
            pieces = [r[...].astype(BF16) for r in d_refs]
        dh = _dot_nt_cols(pieces, w_ref)
        dx, dg = _rms_bwd(x_ref[...], g_ref[...], dh)
        dx_ref[...] = dxo_ref[...] + dx

        @pl.when(pl.program_id(0) == 0)
        def _():
            dg_ref[...] = jnp.zeros_like(dg_ref)

        dg_ref[...] += dg

    row = lambda c, off=0: pl.BlockSpec((tm, c), lambda i, p=off // tm: (i + p, 0))
    return pl.pallas_call(
        body, name=name, grid=(M // tm,),
        in_specs=[row(d.shape[1], off) for d, off in zip(dys, offs)] + [_full(w4.shape), row(D), _full(g.shape), row(D)],
        out_specs=[row(D), _full((1, D))],
        out_shape=[jax.ShapeDtypeStruct((M, D), F32), jax.ShapeDtypeStruct((1, D), F32)],
        compiler_params=_cp("arbitrary"),
    )(*dys, w4, x, g, dx_out)


def mm_tn(a, b, name, layout, into=None, b_off=0, out_dtype=BF16, bm=1024, bn=1280, bl=1024):
    L, K = a.shape
    N = b.shape[1]
    kind = layout[0]
    arg = layout[1] if len(layout) > 1 else None
    bm, bn, bl = _lane_tile(K, bm), _lane_tile(N, bn), min(bl, L)
    assert L % bl == 0 and b_off % bl == 0, (L, bl, b_off)
    nl = L // bl
    n_sh, r_sh = N // N_CHIPS, K // N_CHIPS
    lay = (None,) if arg is None else (None, None)
    mid = () if arg is None else (arg,)
    gs = 1
    if kind == "plain":
        oshape, oblock, oidx = (K, N), (bm, bn), lambda i, j, l: (i, j)
    elif kind == "slab":
        oshape, oblock, oidx = (N_CHIPS, K, N), (None, bm, bn), lambda i, j, l: (arg, i, j)
    elif kind == "cols":
        bn = max(bn - bn % n_sh, n_sh) if bn >= n_sh else _lane_tile(n_sh, bn)
        gs = max(bn // n_sh, 1)
        per = n_sh // bn if gs == 1 else 1
        oshape = (N_CHIPS,) + ((2,) if arg is not None else ()) + (K, n_sh)
        oblock = ((gs,) if gs > 1 else (None,)) + lay[1:] + (bm, min(bn, n_sh))
        oidx = lambda i, j, l: (j // per,) + mid + (i, j % per)
    else:
        bm = max(bm - bm % r_sh, r_sh) if bm >= r_sh else _lane_tile(r_sh, bm)
        gs = max(bm // r_sh, 1)
        per = r_sh // bm if gs == 1 else 1
        oshape = (N_CHIPS,) + ((2,) if arg is not None else ()) + (r_sh, N)
        oblock = ((gs,) if gs > 1 else (None,)) + lay[1:] + (min(bm, r_sh), bn)
        oidx = lambda i, j, l: (i // per,) + mid + (i % per, j)
    assert K % bm == 0 and N % bn == 0, (K, bm, N, bn)

    def body(a_ref, b_ref, *rest):
        o_ref, acc = rest[-2], rest[-1]
        l = pl.program_id(2)

        @pl.when(l == 0)
        def _():
            acc[...] = jnp.zeros_like(acc)

        acc[...] += _dot_tn(a_ref[...].astype(BF16), b_ref[...].astype(BF16))

        @pl.when(l == nl - 1)
        def _():
            if gs == 1:
                o_ref[...] = acc[...].astype(out_dtype)
            elif kind == "cols":
                for t in range(gs):
                    o_ref[t] = acc[:, t * n_sh:(t + 1) * n_sh].astype(out_dtype)
            else:
                for t in range(gs):
                    o_ref[t] = acc[t * r_sh:(t + 1) * r_sh, :].astype(out_dtype)

    in_specs = [pl.BlockSpec((bl, bm), lambda i, j, l: (l, i)),
                pl.BlockSpec((bl, bn), lambda i, j, l, p=b_off // bl: (l + p, j))]
    args = [a, b]
    alias = {}
    if into is not None:
        in_specs.append(ANY)
        args.append(into)
        alias = {2: 0}
    return pl.pallas_call(
        body, name=name, grid=(K // bm, N // bn, nl), in_specs=in_specs,
        out_specs=pl.BlockSpec(oblock, oidx), out_shape=jax.ShapeDtypeStruct(oshape, out_dtype),
        scratch_shapes=[pltpu.VMEM((bm, bn), F32)], input_output_aliases=alias,
        compiler_params=_cp("parallel", "parallel", "arbitrary"),
    )(*args)


def rms_fwd(x, g, name):
    def body(x_ref, g_ref, ob_ref):
        _, xhat = _rms_parts(x_ref[...])
        ob_ref[...] = (xhat * g_ref[...]).astype(BF16)

    return pl.pallas_call(body, name=name, out_shape=jax.ShapeDtypeStruct(x.shape, BF16))(x, g)


def rms_dgain(x, dy0, dy1, name):
    def body(x_ref, d0_ref, d1_ref, o_ref):
        _, xhat = _rms_parts(x_ref[...])
        o_ref[...] = jnp.sum((d0_ref[...] + d1_ref[...]) * xhat, axis=0, keepdims=True)

    return pl.pallas_call(body, name=name, out_shape=jax.ShapeDtypeStruct((1, x.shape[1]), F32))(x, dy0, dy1)


def _s5_discretise(lr, li, logdt, bt_re, bt_im):
    dt = jnp.exp(logdt)
    mag = jnp.exp(lr * dt)
    ab_re = mag * jnp.cos(li * dt)
    ab_im = mag * jnp.sin(li * dt)
    den = lr * lr + li * li
    nr = ab_re - 1.0
    coef_re = (nr * lr + ab_im * li) / den
    coef_im = (ab_im * lr - nr * li) / den
    cr = coef_re[:, None, :]
    ci = coef_im[:, None, :]
    bb_re = cr * bt_re - ci * bt_im
    bb_im = cr * bt_im + ci * bt_re
    return ab_re, ab_im, bb_re, bb_im


def s5_param_fwd(lr, li, logdt, bt_re, bt_im):
    def body(lr_ref, li_ref, ld_ref, br_ref, bi_ref, bbr_ref, bbi_ref):
        _, _, bb_re, bb_im = _s5_discretise(lr_ref[...], li_ref[...], ld_ref[...], br_ref[...], bi_ref[...])
        bbr_ref[...] = bb_re
        bbi_ref[...] = bb_im

    sh = jax.ShapeDtypeStruct(bt_re.shape, F32)
    return pl.pallas_call(body, name="s5_param_fwd", out_shape=[sh, sh])(lr, li, logdt, bt_re, bt_im)


def s5_param_bwd(lr, li, logdt, bt_re, bt_im, d_ab_re, d_ab_im, d_bb_re, d_bb_im):
    def body(lr_ref, li_ref, ld_ref, br_ref, bi_ref, dar_ref, dai_ref, dbr_ref, dbi_ref,
             o_lr, o_li, o_ld, o_br, o_bi):
        _, vjp = jax.vjp(_s5_discretise, lr_ref[...], li_ref[...], ld_ref[...], br_ref[...], bi_ref[...])
        g = vjp((dar_ref[...], dai_ref[...], dbr_ref[...], dbi_ref[...]))
        for o, v in zip((o_lr, o_li, o_ld), g[:3]):
            o[...] = v
        for o, v in zip((o_br, o_bi), g[3:]):
            for c in range(S5_GROUP):
                o[:, c * S5_STATE:(c + 1) * S5_STATE] = v[:, c, :]

    dense = jax.ShapeDtypeStruct((S5_GROUPS, S5_GROUP * S5_STATE), F32)
    shapes = [jax.ShapeDtypeStruct(a.shape, F32) for a in (lr, li, logdt)] + [dense, dense]
    return pl.pallas_call(body, name="s5_param_bwd", out_shape=shapes)(
        lr, li, logdt, bt_re, bt_im, d_ab_re, d_ab_im, d_bb_re, d_bb_im)


def s5_tables(lr_flat, li_flat, logdt_flat):
    def body(lr_ref, li_ref, ld_ref, tab_ref):
        dt = jnp.exp(ld_ref[...])
        a = lr_ref[...] * dt
        th = li_ref[...] * dt
        row = lax.broadcasted_iota(jnp.int32, (8, 1), 0)
        rowf = row.astype(F32)

        def power(e, sign):
            m = jnp.exp(e * a)
            return m * jnp.cos(e * th), sign * m * jnp.sin(e * th)

        k = 0
        for sign, fwd in ((1.0, True), (-1.0, False)):
            for s in (1, 2, 4):
                pr, pi = power(jnp.full((8, 1), float(s), F32), sign)
                keep = (row >= s) if fwd else (row + s < 8)
                tab_ref[k] = jnp.where(keep, pr, 0.0)
                tab_ref[k + 1] = jnp.where(keep, pi, 0.0)
                k += 2
            e = rowf + 1.0 if fwd else 8.0 - rowf
            pr, pi = power(e, sign)
            tab_ref[k] = pr
            tab_ref[k + 1] = pi
            k += 2

    return pl.pallas_call(body, name="s5_tables",
                          out_shape=jax.ShapeDtypeStruct((16, 8, S5_COLS), F32))(lr_flat, li_flat, logdt_flat)


def _scan_block(a, b, tabs, base, cr, ci, reverse):
    for n, s in enumerate((1, 2, 4)):
        mr = tabs[base + 2 * n]
        mi = tabs[base + 2 * n + 1]
        sh = (8 - s) if reverse else s
        ar = pltpu.roll(a, sh, 0)
        br = pltpu.roll(b, sh, 0)
        a, b = a + mr * ar - mi * br, b + mr * br + mi * ar
    pr = tabs[base + 6]
    pi = tabs[base + 7]
    a, b = a + pr * cr - pi * ci, b + pr * ci + pi * cr
    return a, b


class Carried:
    def __init__(self, arrays, out_shapes, sems, start, middle, finish):
        self.arrays, self.out_shapes, self.sems = list(arrays), list(out_shapes), list(sems)
        self.start, self.middle, self.finish = start, middle, finish

    def split(self, refs, n_in, n_out, n_scratch):
        a, o, s = len(self.arrays), len(self.out_shapes), len(self.sems)
        own_in, car_in = refs[:n_in], refs[n_in:n_in + a]
        own_out, car_out = refs[n_in + a:n_in + a + n_out], refs[n_in + a + n_out:n_in + a + n_out + o]
        rest = refs[n_in + a + n_out + o:]
        return own_in + own_out + rest[:n_scratch], (car_in, car_out, rest[n_scratch:n_scratch + s])

    def hooks(self, parts, n_chunks, nt):
        j, t = pl.program_id(0), pl.program_id(1)

        def top():
            pl.when((j == 0) & (t == 0))(lambda: self.start(*parts))
            if self.middle is not None:
                pl.when((j == n_chunks // 2) & (t == 0))(lambda: self.middle(*parts))

        def end():
            pl.when((j == n_chunks - 1) & (t == nt - 1))(lambda: self.finish(*parts))

        return top, end


def s5_fwd(z, bbd_re, bbd_im, ccd_re, ccd_im, tab, dskip, tm=S5_TILE, carried=None):
    L = z.shape[0]
    tm = min(tm, L)
    nt = L // tm

    def body(*refs):
        top = end = None
        if carried is not None:
            refs, parts = carried.split(refs, 7, 4, 3)
            top, end = carried.hooks(parts, S5_SPLIT, nt)
            top()
        u_ref, bbr_ref, bbi_ref, ccr_ref, cci_ref, tab_ref, d_ref, y_ref, ck_ref, hr_ref, hi_ref, xr, xi, car = refs
        t = pl.program_id(1)

        @pl.when(t == 0)
        def _():
            car[...] = jnp.zeros_like(car)

        u = u_ref[...]
        ub = u.astype(BF16)
        xr[...] = _dot(ub, bbr_ref[...])
        xi[...] = _dot(ub, bbi_ref[...])
        tabs = [tab_ref[k] for k in range(8)]

        def blk(i, c):
            r0 = pl.multiple_of(i * 8, 8)
            a, b = _scan_block(xr[pl.ds(r0, 8), :], xi[pl.ds(r0, 8), :], tabs, 0, c[0], c[1], False)
            xr[pl.ds(r0, 8), :] = a
            xi[pl.ds(r0, 8), :] = b
            return a[7:8, :], b[7:8, :]

        cr, ci = lax.fori_loop(0, tm // 8, blk, (car[0:1, :], car[1:2, :]))
        car[0:1, :] = cr
        car[1:2, :] = ci
        ck_ref[0:1, :] = cr
        ck_ref[1:2, :] = ci
        hrb = xr[...].astype(BF16)
        hib = xi[...].astype(BF16)
        hr_ref[...] = hrb
        hi_ref[...] = hib
        y_ref[...] = _dot(hrb, ccr_ref[...]) - _dot(hib, cci_ref[...]) + d_ref[...] * u
        if end is not None:
            end()

    extra = carried.arrays if carried is not None else []
    extra_out = carried.out_shapes if carried is not None else []
    extra_sems = carried.sems if carried is not None else []
    return pl.pallas_call(
        body, name="s5_fwd", grid=(S5_SPLIT, nt),
        in_specs=[pl.BlockSpec((tm, S5_UC), lambda j, t: (t, j)),
                  pl.BlockSpec((None, S5_UC, S5_CC), lambda j, t: (j, 0, 0)),
                  pl.BlockSpec((None, S5_UC, S5_CC), lambda j, t: (j, 0, 0)),
                  pl.BlockSpec((None, S5_CC, S5_UC), lambda j, t: (j, 0, 0)),
                  pl.BlockSpec((None, S5_CC, S5_UC), lambda j, t: (j, 0, 0)),
                  pl.BlockSpec((8, 8, S5_CC), lambda j, t: (0, 0, j)),
                  pl.BlockSpec((1, S5_UC), lambda j, t: (0, j))] + [ANY] * len(extra),
        out_specs=[pl.BlockSpec((tm, S5_UC), lambda j, t: (t, j)),
                   pl.BlockSpec((None, 2, S5_CC), lambda j, t: (t, 0, j)),
                   pl.BlockSpec((tm, S5_CC), lambda j, t: (t, j)),
                   pl.BlockSpec((tm, S5_CC), lambda j, t: (t, j))] + [ANY] * len(extra_out),
        out_shape=[jax.ShapeDtypeStruct((L, S5_WIDTH), F32), jax.ShapeDtypeStruct((nt, 2, S5_COLS), F32),
                   jax.ShapeDtypeStruct((L, S5_COLS), BF16), jax.ShapeDtypeStruct((L, S5_COLS), BF16)] + extra_out,
        scratch_shapes=[pltpu.VMEM((tm, S5_CC), F32), pltpu.VMEM((tm, S5_CC), F32), pltpu.VMEM((2, S5_CC), F32)]
        + extra_sems,
        compiler_params=_cp("arbitrary" if carried is not None else "parallel", "arbitrary"),
    )(z, bbd_re, bbd_im, ccd_re, ccd_im, tab, dskip, *extra)


def s5_bwd(z, dy, dz, ckpt, hrb, hib, bbd_re, bbd_im, ccd_re, ccd_im, tab, dskip, tm=S5_TILE, carried=None):
    L = z.shape[0]
    tm = min(tm, L)
    nt = L // tm

    def body(*refs):
        top = end = None
        if carried is not None:
            refs, parts = carried.split(refs, 12, 7, 7)
            top, end = carried.hooks(parts, S5_SPLIT, nt)
            top()
        (u_ref, dy_ref, dz_ref, ck_ref, hrb_ref, hib_ref, bbr_ref, bbi_ref, ccr_ref, cci_ref, tab_ref, d_ref,
         du_ref, da_ref, dbr_ref, dbi_ref, dcr_ref, dci_ref, dd_ref, hr, hi, gr, gi, car, acr, aci) = refs
        t = pl.program_id(1)
        tt = nt - 1 - t

        @pl.when(t == 0)
        def _():
            for r in (car, acr, aci, dbr_ref, dbi_ref, dcr_ref, dci_ref, dd_ref):
                r[...] = jnp.zeros_like(r)

        u = u_ref[...]
        ub = u.astype(BF16)
        dyv = dy_ref[...]
        dyb = dyv.astype(BF16)
        tabs = [None] * 8 + [tab_ref[k] for k in range(8, 16)]

        live = (tt > 0).astype(F32)
        hr[0:8, :] = jnp.broadcast_to(ck_ref[0:1, :] * live, (8, S5_CC))
        hi[0:8, :] = jnp.broadcast_to(ck_ref[1:2, :] * live, (8, S5_CC))
        hrb = hrb_ref[...]
        hib = hib_ref[...]
        hr[8:, :] = hrb.astype(F32)
        hi[8:, :] = hib.astype(F32)
        dcr_ref[...] += _dot_tn(hrb, dyb)
        dci_ref[...] -= _dot_tn(hib, dyb)

        gr[...] = _dot_nt(dyb, ccr_ref[...])
        gi[...] = -_dot_nt(dyb, cci_ref[...])
        row0 = lax.broadcasted_iota(jnp.int32, (8, S5_CC), 0) == 0

        def rblk(k, c):
            i = tm // 8 - 1 - k
            r0 = pl.multiple_of(i * 8, 8)
            a, b = _scan_block(gr[pl.ds(r0, 8), :], gi[pl.ds(r0, 8), :], tabs, 8, c[0], c[1], True)
            gr[pl.ds(r0, 8), :] = a
            gi[pl.ds(r0, 8), :] = b
            r1 = pl.multiple_of(i * 8 + 8, 8)
            hpr = jnp.where(row0, pltpu.roll(hr[pl.ds(r0, 8), :], 1, 0), pltpu.roll(hr[pl.ds(r1, 8), :], 1, 0))
            hpi = jnp.where(row0, pltpu.roll(hi[pl.ds(r0, 8), :], 1, 0), pltpu.roll(hi[pl.ds(r1, 8), :], 1, 0))
            acr[...] += a * hpr + b * hpi
            aci[...] += b * hpr - a * hpi
            return a[0:1, :], b[0:1, :]

        cr, ci = lax.fori_loop(0, tm // 8, rblk, (car[0:1, :], car[1:2, :]))
        car[0:1, :] = cr
        car[1:2, :] = ci

        grb = gr[...].astype(BF16)
        gib = gi[...].astype(BF16)
        du_ref[...] = (_dot_nt(grb, bbr_ref[...]) + _dot_nt(gib, bbi_ref[...]) + d_ref[...] * dyv).astype(BF16)
        dbr_ref[...] += _dot_tn(ub, grb)
        dbi_ref[...] += _dot_tn(ub, gib)
        dd_ref[...] += jnp.sum(dyv * u, axis=0, keepdims=True)

        @pl.when(t == nt - 1)
        def _():
            da_ref[0:1, :] = jnp.sum(acr[...], axis=0, keepdims=True)
            da_ref[1:2, :] = jnp.sum(aci[...], axis=0, keepdims=True)

        if end is not None:
            end()

    extra = carried.arrays if carried is not None else []
    extra_out = carried.out_shapes if carried is not None else []
    extra_sems = carried.sems if carried is not None else []
    chunk = lambda a, b: pl.BlockSpec((None, a, b), lambda j, t: (j, 0, 0))
    return pl.pallas_call(
        body, name="s5_bwd", grid=(S5_SPLIT, nt),
        in_specs=[pl.BlockSpec((tm, S5_UC), lambda j, t: (nt - 1 - t, j)),
                  pl.BlockSpec((tm, S5_UC), lambda j, t: (nt - 1 - t, j)),
                  ANY,
                  pl.BlockSpec((None, 2, S5_CC), lambda j, t: (jnp.maximum(nt - 2 - t, 0), 0, j)),
                  pl.BlockSpec((tm, S5_CC), lambda j, t: (nt - 1 - t, j)),
                  pl.BlockSpec((tm, S5_CC), lambda j, t: (nt - 1 - t, j)),
                  chunk(S5_UC, S5_CC), chunk(S5_UC, S5_CC), chunk(S5_CC, S5_UC), chunk(S5_CC, S5_UC),
                  pl.BlockSpec((16, 8, S5_CC), lambda j, t: (0, 0, j)),
                  pl.BlockSpec((1, S5_UC), lambda j, t: (0, j))] + [ANY] * len(extra),
        out_specs=[pl.BlockSpec((tm, S5_UC), lambda j, t: (nt - 1 - t, j)),
                   pl.BlockSpec((None, 2, S5_CC), lambda j, t: (j, 0, 0)),
                   chunk(S5_UC, S5_CC), chunk(S5_UC, S5_CC), chunk(S5_CC, S5_UC), chunk(S5_CC, S5_UC),
                   pl.BlockSpec((1, S5_UC), lambda j, t: (0, j))] + [ANY] * len(extra_out),
        out_shape=[jax.ShapeDtypeStruct(dz.shape, dz.dtype),
                   jax.ShapeDtypeStruct((S5_SPLIT, 2, S5_CC), F32),
                   jax.ShapeDtypeStruct((S5_SPLIT, S5_UC, S5_CC), F32),
                   jax.ShapeDtypeStruct((S5_SPLIT, S5_UC, S5_CC), F32),
                   jax.ShapeDtypeStruct((S5_SPLIT, S5_CC, S5_UC), F32),
                   jax.ShapeDtypeStruct((S5_SPLIT, S5_CC, S5_UC), F32),
                   jax.ShapeDtypeStruct((1, S5_WIDTH), F32)] + extra_out,
        scratch_shapes=[pltpu.VMEM((tm + 8, S5_CC), F32), pltpu.VMEM((tm + 8, S5_CC), F32),
                        pltpu.VMEM((tm, S5_CC), F32), pltpu.VMEM((tm, S5_CC), F32),
                        pltpu.VMEM((2, S5_CC), F32), pltpu.VMEM((8, S5_CC), F32), pltpu.VMEM((8, S5_CC), F32)]
        + extra_sems,
        input_output_aliases={2: 0},
        compiler_params=_cp("arbitrary" if carried is not None else "parallel", "arbitrary"),
    )(z, dy, dz, ckpt, hrb, hib, bbd_re, bbd_im, ccd_re, ccd_im, tab, dskip, *extra)


_EYE8 = np.eye(S5_GROUPS // S5_SPLIT, dtype=np.float32)


def _blockdiag(a):
    g, r, c = a.shape
    a = a.reshape(S5_SPLIT, g // S5_SPLIT, r, c)
    out = a[:, :, :, None, :] * _EYE8[None, :, None, :, None].astype(a.dtype)
    return out.reshape(S5_SPLIT, (g // S5_SPLIT) * r, (g // S5_SPLIT) * c)


def _blockdiag_extract(a, r, c):
    n = S5_GROUPS // S5_SPLIT
    a = a.reshape(S5_SPLIT, n, r, n, c)
    d = jnp.stack([a[:, k, :, k, :] for k in range(n)], axis=1)
    return d.reshape(S5_GROUPS, r, c)


def s5_mixer_core_fwd(z, lam_re, lam_im, log_dt, b_re, b_im, c_re, c_im, d_skip, carried=None):
    bt_re = jnp.swapaxes(b_re, 1, 2)
    bt_im = jnp.swapaxes(b_im, 1, 2)
    logdt = log_dt.reshape(S5_GROUPS, 1)
    bb_re, bb_im = s5_param_fwd(lam_re, lam_im, logdt, bt_re, bt_im)
    flat = lambda a: a.reshape(1, S5_COLS)
    tab = s5_tables(flat(lam_re), flat(lam_im), flat(jnp.broadcast_to(logdt, (S5_GROUPS, S5_STATE))))
    bbd_re = _blockdiag(bb_re).astype(BF16)
    bbd_im = _blockdiag(bb_im).astype(BF16)
    ccd_re = _blockdiag(jnp.swapaxes(c_re, 1, 2)).astype(BF16)
    ccd_im = _blockdiag(jnp.swapaxes(c_im, 1, 2)).astype(BF16)
    dsk = d_skip.reshape(1, S5_WIDTH)
    y, ckpt, hrb, hib, *landed = s5_fwd(z, bbd_re, bbd_im, ccd_re, ccd_im, tab, dsk, carried=carried)
    saved = (logdt, bt_re, bt_im, bbd_re, bbd_im, ccd_re, ccd_im, tab, dsk, ckpt, hrb, hib)
    return y, saved, landed


def s5_b_from_dense(dense):
    return jnp.swapaxes(dense.reshape(S5_GROUPS, S5_GROUP, S5_STATE), 1, 2)


def s5_mixer_core_bwd(z, dy, dz, lam_re, lam_im, saved, carried=None):
    logdt, bt_re, bt_im, bbd_re, bbd_im, ccd_re, ccd_im, tab, dsk, ckpt, hrb, hib = saved
    dz, da, dbr, dbi, dcr, dci, dd, *landed = s5_bwd(z, dy, dz, ckpt, hrb, hib, bbd_re, bbd_im, ccd_re, ccd_im, tab,
                                                     dsk, carried=carried)
    d_ab_re = da[:, 0, :].reshape(S5_GROUPS, S5_STATE)
    d_ab_im = da[:, 1, :].reshape(S5_GROUPS, S5_STATE)
    d_bb_re = _blockdiag_extract(dbr, S5_GROUP, S5_STATE)
    d_bb_im = _blockdiag_extract(dbi, S5_GROUP, S5_STATE)
    g_lr, g_li, g_ld, g_btr, g_bti = s5_param_bwd(lam_re, lam_im, logdt, bt_re, bt_im,
                                                  d_ab_re, d_ab_im, d_bb_re, d_bb_im)
    g_cre = jnp.swapaxes(_blockdiag_extract(dcr, S5_STATE, S5_GROUP), 1, 2)
    g_cim = jnp.swapaxes(_blockdiag_extract(dci, S5_STATE, S5_GROUP), 1, 2)
    grads = dict(lambda_re=g_lr, lambda_im=g_li, log_dt=g_ld.reshape(S5_GROUPS), b_re=g_btr, b_im=g_bti,
                 c_re=g_cre, c_im=g_cim, d=dd.reshape(S5_WIDTH))
    return dz, grads, landed


Z_U, Z_GA, Z_VAL, Z_GLU, Z_GB = range(5)
SUBLANES = 8


def _shifted_copies(buf, tm):
    n = tm + CONV_HALO - SUBLANES
    for r in range(1, SUBLANES):
        buf[r, 0:n, :] = buf[0, pl.ds(r, n), :]


CONV_ROWS = 32


def _shifted_rows(buf, start, rows, base=0):
    return buf[start % SUBLANES, pl.ds(base + (start - start % SUBLANES), rows), :]


def conv_fwd(z, conv_w, conv_b, tm=ROW_TILE):
    L = z.shape[0]
    tm = min(tm, L)
    nt = L // tm
    hb = tm // CONV_HALO
    C = CONV_WIDTH

    def body(val_ref, glu_ref, valh_ref, gluh_ref, w_ref, b_ref, c_ref, vsh):
        live = (pl.program_id(0) > 0).astype(F32)
        vsh[0, 0:CONV_HALO, :] = valh_ref[...] * _sigmoid(gluh_ref[...]) * live
        vsh[0, CONV_HALO:, :] = val_ref[...] * _sigmoid(glu_ref[...])
        _shifted_copies(vsh, tm)

        def rows(i, carry):
            base = pl.multiple_of(i * CONV_ROWS, CONV_ROWS)
            acc = jnp.broadcast_to(b_ref[...], (CONV_ROWS, C))
            for k in range(CONV_KERNEL):
                acc = acc + w_ref[k:k + 1, :] * _shifted_rows(vsh, CONV_HALO - CONV_KERNEL + 1 + k, CONV_ROWS, base)
            c_ref[pl.ds(base, CONV_ROWS), :] = acc
            return carry

        lax.fori_loop(0, tm // CONV_ROWS, rows, 0)

    cur = lambda col: pl.BlockSpec((tm, C), lambda t: (t, col))
    prev = lambda col: pl.BlockSpec((CONV_HALO, C), lambda t: (jnp.maximum(t * hb - 1, 0), col))
    return pl.pallas_call(
        body, name="conv_fwd", grid=(nt,),
        in_specs=[cur(Z_VAL), cur(Z_GLU), prev(Z_VAL), prev(Z_GLU), _full(conv_w.shape), _full(conv_b.shape)],
        out_specs=pl.BlockSpec((tm, C), lambda t: (t, 0)),
        out_shape=jax.ShapeDtypeStruct((L, C), F32),
        scratch_shapes=[pltpu.VMEM((8, tm + CONV_HALO, C), F32)],
        compiler_params=_cp("parallel"),
    )(z, z, z, z, conv_w, conv_b)


def conv_bwd(z, dc, dz, conv_w, tm=ROW_TILE):
    L = z.shape[0]
    tm = min(tm, L)
    nt = L // tm
    hb = tm // CONV_HALO
    nh = L // CONV_HALO
    C = CONV_WIDTH
    off = CONV_HALO - CONV_KERNEL + 1

    def body(val_ref, glu_ref, valh_ref, gluh_ref, dc_ref, dcn_ref, dz_ref, w_ref, dvg_ref, dw_ref, db_ref,
             vsh, dsh, wacc):
        t = pl.program_id(0)

        @pl.when(t == 0)
        def _():
            wacc[...] = jnp.zeros_like(wacc)
            db_ref[...] = jnp.zeros_like(db_ref)

        val = val_ref[...]
        sg = _sigmoid(glu_ref[...])
        vsh[0, 0:CONV_HALO, :] = valh_ref[...] * _sigmoid(gluh_ref[...]) * (t > 0).astype(F32)
        vsh[0, CONV_HALO:, :] = val * sg
        dcv = dc_ref[...]
        dsh[0, 0:tm, :] = dcv
        dsh[0, tm:, :] = dcn_ref[...] * (t < nt - 1).astype(F32)
        _shifted_copies(vsh, tm)
        _shifted_copies(dsh, tm)

        def rows(i, carry):
            base = pl.multiple_of(i * CONV_ROWS, CONV_ROWS)
            dcr = dc_ref[pl.ds(base, CONV_ROWS), :]
            dv = jnp.zeros((CONV_ROWS, C), F32)
            for k in range(CONV_KERNEL):
                dv = dv + w_ref[k:k + 1, :] * _shifted_rows(dsh, CONV_KERNEL - 1 - k, CONV_ROWS, base)
                prod = dcr * _shifted_rows(vsh, off + k, CONV_ROWS, base)
                wacc[k] += jnp.sum(prod.reshape(CONV_ROWS // SUBLANES, SUBLANES, C), axis=0)
            valr = val_ref[pl.ds(base, CONV_ROWS), :]
            sgr = _sigmoid(glu_ref[pl.ds(base, CONV_ROWS), :])
            dvg_ref[pl.ds(base, CONV_ROWS), 0:C] = (dv * sgr).astype(BF16)
            dvg_ref[pl.ds(base, CONV_ROWS), C:] = (dv * valr * sgr * (1.0 - sgr)).astype(BF16)
            return carry

        lax.fori_loop(0, tm // CONV_ROWS, rows, 0)
        db_ref[...] += jnp.sum(dcv, axis=0, keepdims=True)

        @pl.when(t == nt - 1)
        def _():
            dw_ref[...] = jnp.sum(wacc[...], axis=1)

    cur = lambda col: pl.BlockSpec((tm, C), lambda t: (t, col))
    prev = lambda col: pl.BlockSpec((CONV_HALO, C), lambda t: (jnp.maximum(t * hb - 1, 0), col))
    nxt = pl.BlockSpec((CONV_HALO, C), lambda t: (jnp.minimum((t + 1) * hb, nh - 1), 0))
    row = pl.BlockSpec((tm, C), lambda t: (t, 0))
    return pl.pallas_call(
        body, name="conv_bwd", grid=(nt,),
        in_specs=[cur(Z_VAL), cur(Z_GLU), prev(Z_VAL), prev(Z_GLU), row, nxt, ANY, _full(conv_w.shape)],
        out_specs=[pl.BlockSpec((tm, 2 * C), lambda t: (t, 1)), _full((CONV_HALO, C)), _full((1, C))],
        out_shape=[jax.ShapeDtypeStruct(dz.shape, dz.dtype),
                   jax.ShapeDtypeStruct((CONV_HALO, C), F32), jax.ShapeDtypeStruct((1, C), F32)],
        scratch_shapes=[pltpu.VMEM((8, tm + CONV_HALO, C), F32), pltpu.VMEM((8, tm + CONV_HALO, C), F32),
                        pltpu.VMEM((CONV_HALO, SUBLANES, C), F32)],
        input_output_aliases={6: 0},
        compiler_params=_cp("arbitrary"),
    )(z, z, z, z, dc, dc, dz, conv_w)


def _ln_parts(c):
    mu = jnp.mean(c, axis=-1, keepdims=True)
    cc = c - mu
    rstd = lax.rsqrt(jnp.mean(cc * cc, axis=-1, keepdims=True) + EPS)
    return rstd, cc * rstd


def _ev_tail_branches(ys, c, wglu, bglu, lng, lnb):
    z1 = _gelu(ys)
    z1b = z1.astype(BF16)
    sg = _sigmoid(_dot_rows(z1b, wglu) + bglu)
    out = z1 * sg
    rstd, chat = _ln_parts(c)
    cn = chat * lng + lnb
    return z1, z1b, sg, out, rstd, chat, cn


def ev_tail_fwd(ys, z, c, x0, wglu, bglu, lng, lnb, wout, tm=ROW_TILE):
    L, D = x0.shape
    tm = min(tm, L)
    W = S5_WIDTH

    def body(ys_ref, ga_ref, c_ref, gb_ref, x_ref, wglu_ref, bglu_ref, lng_ref, lnb_ref, wout_ref, o_ref):
        _, _, _, out, _, _, cn = _ev_tail_branches(ys_ref[...], c_ref[...], wglu_ref, bglu_ref[...],
                                                   lng_ref[...], lnb_ref[...])
        ya = (out * _silu(ga_ref[...])).astype(BF16)
        yb = (_silu(cn) * _silu(gb_ref[...])).astype(BF16)
        o_ref[...] = x_ref[...] + _dot_rows(jnp.concatenate([ya, yb], axis=1), wout_ref)

    row = lambda n, col=0: pl.BlockSpec((tm, n), lambda t: (t, col))
    return pl.pallas_call(
        body, name="ev_tail_fwd", grid=(L // tm,),
        in_specs=[row(W), row(W, Z_GA), row(W), row(W, Z_GB), row(D), _full(wglu.shape), _full(bglu.shape),
                  _full(lng.shape), _full(lnb.shape), _full(wout.shape)],
        out_specs=row(D), out_shape=jax.ShapeDtypeStruct((L, D), F32), compiler_params=_cp("parallel"),
    )(ys, z, c, z, x0, wglu, bglu, lng, lnb, wout)


def ev_tail_bwd(ys, z, c, dx1, wglu, bglu, lng, lnb, wout, tm=ROW_TILE):
    L, D = dx1.shape
    tm = min(tm, L)
    W = S5_WIDTH

    def body(ys_ref, ga_ref, c_ref, gb_ref, dx_ref, wglu_ref, bglu_ref, lng_ref, lnb_ref, wout_ref,
             dys_ref, dc_ref, dz_ref, r_ref, z1_ref, dt_ref, dbg_ref, dlg_ref, dlb_ref):
        @pl.when(pl.program_id(0) == 0)
        def _():
            for r in (dbg_ref, dlg_ref, dlb_ref):
                r[...] = jnp.zeros_like(r)

        ys, ga, gb = ys_ref[...], ga_ref[...], gb_ref[...]
        z1, z1b, sg, out, rstd, chat, cn = _ev_tail_branches(ys, c_ref[...], wglu_ref, bglu_ref[...],
                                                             lng_ref[...], lnb_ref[...])
        (sga, dsga), (sgb, dsgb), (scn, dscn) = _silu_pair(ga), _silu_pair(gb), _silu_pair(cn)
        r_ref[:, 0:W] = (out * sga).astype(BF16)
        r_ref[:, W:] = (scn * sgb).astype(BF16)
        dr = _dot_nt_rows(dx_ref[...].astype(BF16), wout_ref)
        dra, drb = dr[:, 0:W], dr[:, W:]
        dz_ref[...] = jnp.zeros_like(dz_ref)
        dz_ref[:, Z_GA * W:(Z_GA + 1) * W] = (dra * out * dsga).astype(BF16)
        dout = dra * sga
        dt = dout * z1 * sg * (1.0 - sg)
        dtb = dt.astype(BF16)
        dz1 = dout * sg + _dot_nt_rows(dtb, wglu_ref)
        dys_ref[...] = dz1 * _dgelu(ys)
        z1_ref[...] = z1b
        dt_ref[...] = dtb
        dbg_ref[...] += jnp.sum(dt, axis=0, keepdims=True)
        dz_ref[:, Z_GB * W:(Z_GB + 1) * W] = (drb * scn * dsgb).astype(BF16)
        dcn = drb * sgb * dscn
        dlg_ref[...] += jnp.sum(dcn * chat, axis=0, keepdims=True)
        dlb_ref[...] += jnp.sum(dcn, axis=0, keepdims=True)
        dch = dcn * lng_ref[...]
        dc_ref[...] = rstd * (dch - jnp.mean(dch, axis=-1, keepdims=True)
                              - chat * jnp.mean(dch * chat, axis=-1, keepdims=True))

    row = lambda n, col=0: pl.BlockSpec((tm, n), lambda t: (t, col))
    f = lambda n, dt: jax.ShapeDtypeStruct((L, n), dt)
    vec = jax.ShapeDtypeStruct((1, W), F32)
    return pl.pallas_call(
        body, name="ev_tail_bwd", grid=(L // tm,),
        in_specs=[row(W), row(W, Z_GA), row(W), row(W, Z_GB), row(D), _full(wglu.shape), _full(bglu.shape),
                  _full(lng.shape), _full(lnb.shape), _full(wout.shape)],
        out_specs=[row(W), row(W), row(EVEN_IN), row(D), row(W), row(W), _full((1, W)), _full((1, W)), _full((1, W))],
        out_shape=[f(W, F32), f(W, F32), f(EVEN_IN, BF16), f(D, BF16), f(W, BF16), f(W, BF16), vec, vec, vec],
        compiler_params=_cp("arbitrary"),
    )(ys, z, c, z, dx1, wglu, bglu, lng, lnb, wout)


XA_SCALE = XA_HEAD_DIM ** -0.5


def _xa_forward(xv, g, wqg, kv):
    D = D_MODEL
    _, xhat = _rms_parts(xv)
    hb = (xhat * g).astype(BF16)
    qb = _dot_cols(hb, wqg, (0, 1)).astype(BF16)
    gate = _dot_cols(hb, wqg, (2, 3))
    ps, os_ = [], []
    for h in range(XA_HEADS):
        lo, hi = h * XA_HEAD_DIM, (h + 1) * XA_HEAD_DIM
        s = _dot_nt(qb[:, lo:hi], kv[:, lo:hi]) * XA_SCALE
        e = jnp.exp(s - jnp.max(s, axis=-1, keepdims=True))
        p = e / jnp.sum(e, axis=-1, keepdims=True)
        ps.append(p)
        os_.append(_dot(p.astype(BF16), kv[:, D + lo:D + hi]))
    return hb, qb, gate, ps, jnp.concatenate(os_, axis=1)


def xa_fwd(x, g, wqg, kv, wo, layer, name, tm=MM_TILE):
    L, D = x.shape
    tm = min(tm, L)

    def body(x_ref, g_ref, wqg_ref, kv_ref, wo_ref, o_ref):
        xv = x_ref[...]
        _, _, gate, _, o = _xa_forward(xv, g_ref[...], wqg_ref, kv_ref[...])
        o_ref[...] = xv + _dot_rows((o * _silu(gate)).astype(BF16), wo_ref)

    row = pl.BlockSpec((tm, D), lambda t: (t, 0))
    return pl.pallas_call(
        body, name=name, grid=(L // tm,),
        in_specs=[row, _full(g.shape), _wspec(wqg, layer), _full(kv.shape), _wspec(wo, layer)],
        out_specs=row, out_shape=jax.ShapeDtypeStruct((L, D), F32), compiler_params=_cp("parallel"),
    )(x, g, wqg, kv, wo)


def xa_bwd(x, dxo, g, wqg, kv, wo, layer, name, tm=MM_TILE):
    L, D = x.shape
    tm = min(tm, L)

    def body(x_ref, dxo_ref, g_ref, wqg_ref, kv_ref, wo_ref, dx_ref, dqg_ref, h_ref, r_ref, dkv_ref, dg_ref):
        @pl.when(pl.program_id(0) == 0)
        def _():
            dkv_ref[...] = jnp.zeros_like(dkv_ref)
            dg_ref[...] = jnp.zeros_like(dg_ref)

        xv = x_ref[...]
        kv = kv_ref[...]
        hb, qb, gate, ps, o = _xa_forward(xv, g_ref[...], wqg_ref, kv)
        sgate, dsgate = _silu_pair(gate)
        h_ref[...] = hb
        r_ref[...] = (o * sgate).astype(BF16)
        dxo = dxo_ref[...]
        dr = _dot_nt_rows(dxo.astype(BF16), wo_ref)
        do = dr * sgate
        dqg_ref[:, D:] = (dr * o * dsgate).astype(BF16)
        dob = do.astype(BF16)
        for h in range(XA_HEADS):
            lo, hi = h * XA_HEAD_DIM, (h + 1) * XA_HEAD_DIM
            p = ps[h]
            pb = p.astype(BF16)
            dp = _dot_nt(dob[:, lo:hi], kv[:, D + lo:D + hi])
            dkv_ref[:, D + lo:D + hi] += _dot_tn(pb, dob[:, lo:hi])
            ds = p * (dp - jnp.sum(dp * p, axis=-1, keepdims=True))
            dsb = (ds * XA_SCALE).astype(BF16)
            dqg_ref[:, lo:hi] = _dot(dsb, kv[:, lo:hi]).astype(BF16)
            dkv_ref[:, lo:hi] += _dot_tn(dsb, qb[:, lo:hi])
        dh = _dot_nt_cols(_col_pieces(dqg_ref[...], D // 2), wqg_ref)
        dx, dg = _rms_bwd(xv, g_ref[...], dh)
        dx_ref[...] = dxo + dx
        dg_ref[...] += dg

    row = lambda n: pl.BlockSpec((tm, n), lambda t: (t, 0))
    return pl.pallas_call(
        body, name=name, grid=(L // tm,),
        in_specs=[row(D), row(D), _full(g.shape), _wspec(wqg, layer), _full(kv.shape), _wspec(wo, layer)],
        out_specs=[row(D), row(2 * D), row(D), row(D), _full(kv.shape), _full((1, D))],
        out_shape=[jax.ShapeDtypeStruct((L, D), F32), jax.ShapeDtypeStruct((L, 2 * D), BF16),
                   jax.ShapeDtypeStruct((L, D), BF16), jax.ShapeDtypeStruct((L, D), BF16),
                   jax.ShapeDtypeStruct(kv.shape, F32), jax.ShapeDtypeStruct((1, D), F32)],
        compiler_params=_cp("arbitrary"),
    )(x, dxo, g, wqg, kv, wo)


ATT_SCALE = ATT_HEAD_DIM ** -0.5
ATT_PAIRS = ATT_HEADS // 2
SKEW_LANES = 1024
REL_LANES = 384


def _skew(x, left):
    amt = (ATT_QB - 1) - lax.broadcasted_iota(jnp.int32, (ATT_QB, 1), 0)
    for bit in range(8):
        sh = (SKEW_LANES - (1 << bit)) if left else (1 << bit)
        x = jnp.where(((amt >> bit) & 1) == 1, pltpu.roll(x, sh, 1), x)
    return x


def _dist_onehot(shape, dist_axis):
    j = lax.broadcasted_iota(jnp.int32, shape, dist_axis)
    r = lax.broadcasted_iota(jnp.int32, shape, 1 - dist_axis)
    return (jnp.clip((ATT_WIN - 1) - j, -MAX_REL, MAX_REL) + MAX_REL == r).astype(BF16)


def _dot_exact(v, onehot):
    acc = jnp.zeros((v.shape[0], onehot.shape[1]), F32)
    rem = v
    for _ in range(3):
        part = rem.astype(BF16)
        acc = acc + _dot(part, onehot)
        rem = rem - part.astype(F32)
    return acc


ATT_EDGE = ATT_PAD // ATT_QB


def att_bias(rel_bias):
    H = rel_bias.shape[0]
    rb = jnp.pad(rel_bias, ((0, 0), (0, REL_LANES - rel_bias.shape[1]))).reshape(H, 1, REL_LANES)

    def body(rb_ref, o_ref):
        by_col = _dot_exact(jnp.broadcast_to(rb_ref[...], (8, REL_LANES)), _dist_onehot((REL_LANES, SKEW_LANES), 1))
        x = _skew(jnp.broadcast_to(by_col[0:1, :], (ATT_QB, SKEW_LANES)), left=True)[:, 0:ATT_WIN]
        qc = lax.broadcasted_iota(jnp.int32, (ATT_QB, 1), 0) // CHUNK + LEFT_CHUNKS
        col = lax.broadcasted_iota(jnp.int32, (1, ATT_WIN), 1)
        dc = qc - col // CHUNK
        band = (dc >= 0) & (dc <= LEFT_CHUNKS)
        for blk in range(ATT_EDGE + 1):
            o_ref[blk] = jnp.where(band & (col >= ATT_PAD - blk * ATT_QB), x, NEG)

    return pl.pallas_call(
        body, name="att_bias", grid=(H,),
        in_specs=[pl.BlockSpec((None, 1, REL_LANES), lambda h: (h, 0, 0))],
        out_specs=pl.BlockSpec((ATT_EDGE + 1, None, ATT_QB, ATT_WIN), lambda h: (0, h, 0, 0)),
        out_shape=jax.ShapeDtypeStruct((ATT_EDGE + 1, H, ATT_QB, ATT_WIN), F32), compiler_params=_cp("parallel"),
    )(rb)


def relbias_bwd(dbias):
    H = dbias.shape[0]

    def body(x_ref, o_ref):
        x = jnp.concatenate([x_ref[...], jnp.zeros((ATT_QB, SKEW_LANES - ATT_WIN), F32)], axis=1)
        col = jnp.sum(_skew(x, left=False), axis=0, keepdims=True)
        o_ref[...] = _dot_exact(jnp.broadcast_to(col, (8, SKEW_LANES)), _dist_onehot((SKEW_LANES, REL_LANES), 0))

    out = pl.pallas_call(
        body, name="relbias_bwd", grid=(H,),
        in_specs=[pl.BlockSpec((None, ATT_QB, ATT_WIN), lambda h: (h, 0, 0))],
        out_specs=pl.BlockSpec((None, 8, REL_LANES), lambda h: (h, 0, 0)),
        out_shape=jax.ShapeDtypeStruct((H, 8, REL_LANES), F32), compiler_params=_cp("parallel"),
    )(dbias)
    return out[:, 0, :2 * MAX_REL + 1]


def _ca_scores(qh, kw, bias):
    s = _dot_nt(qh, kw) + bias
    e = jnp.exp(s - jnp.max(s, axis=-1, keepdims=True))
    return e, 1.0 / jnp.sum(e, axis=-1, keepdims=True)


def _ca_head(qv, m):
    return jnp.where(m, qv, jnp.zeros_like(qv)) * ATT_SCALE


def _ca_bias_spec():
    return pl.BlockSpec((None, 2, ATT_QB, ATT_WIN), lambda hp, b: (jnp.minimum(b, ATT_EDGE), hp, 0, 0))


def ca_fwd(q, kvp, gate, bias):
    L, D = q.shape
    Lp = kvp.shape[0]
    nb = L // ATT_QB

    def body(q_ref, k_ref, v_ref, g_ref, b_ref, r_ref, o_ref):
        w = pl.multiple_of(pl.program_id(1) * ATT_QB, ATT_QB)
        kw = k_ref[pl.ds(w, ATT_WIN), :]
        vw = v_ref[pl.ds(w, ATT_WIN), :]
        qv = q_ref[...]
        first = lax.broadcasted_iota(jnp.int32, (1, 128), 1) < ATT_HEAD_DIM
        outs = []
        for hh, m in enumerate((first, jnp.logical_not(first))):
            e, inv = _ca_scores(_ca_head(qv, m), kw, b_ref[hh])
            outs.append(_dot(e.astype(BF16), vw) * inv)
        o = jnp.where(first, outs[0], outs[1])
        r_ref[...] = (o * _silu(g_ref[...])).astype(BF16)
        o_ref[...] = o.astype(BF16)

    blk = pl.BlockSpec((ATT_QB, 128), lambda hp, b: (b, hp))
    return pl.pallas_call(
        body, name="ca_fwd", grid=(ATT_PAIRS, nb),
        in_specs=[blk, pl.BlockSpec((Lp, 128), lambda hp, b: (0, hp)),
                  pl.BlockSpec((Lp, 128), lambda hp, b: (0, ATT_PAIRS + hp)), blk, _ca_bias_spec()],
        out_specs=[blk, blk], out_shape=[jax.ShapeDtypeStruct((L, D), BF16), jax.ShapeDtypeStruct((L, D), BF16)],
        compiler_params=_cp("parallel", "arbitrary"),
    )(q, kvp, kvp, gate, bias)


def ca_bwd(q, kvp, gate, bias, dr, o):
    L, D = q.shape
    Lp = kvp.shape[0]
    nb = L // ATT_QB

    def body(q_ref, k_ref, v_ref, g_ref, b_ref, dr_ref, o_ref, dq_ref, dg_ref, dk_ref, dv_ref, db_ref):
        b = pl.program_id(1)

        @pl.when(b == 0)
        def _():
            for r in (dk_ref, dv_ref, db_ref):
                r[...] = jnp.zeros_like(r)

        w = pl.multiple_of(b * ATT_QB, ATT_QB)
        kw = k_ref[pl.ds(w, ATT_WIN), :]
        vw = v_ref[pl.ds(w, ATT_WIN), :]
        qv = q_ref[...]
        gate_v = g_ref[...]
        drv = dr_ref[...]
        o = o_ref[...].astype(F32)
        sgate, dsgate = _silu_pair(gate_v)
        do = drv * sgate
        doo = do * o
        first = lax.broadcasted_iota(jnp.int32, (1, 128), 1) < ATT_HEAD_DIM
        dqs = []
        dkw = jnp.zeros((ATT_WIN, 128), F32)
        dvw = jnp.zeros((ATT_WIN, 128), F32)
        for hh, m in enumerate((first, jnp.logical_not(first))):
            qh = _ca_head(qv, m)
            e, inv = _ca_scores(qh, kw, b_ref[hh])
            eb = e.astype(BF16)
            doh = jnp.where(m, do, 0.0)
            dp = _dot_nt(doh.astype(BF16), vw)
            dvw = dvw + _dot_tn(eb, (doh * inv).astype(BF16))
            rs = jnp.sum(jnp.where(m, doo, 0.0), axis=-1, keepdims=True)
            ds = e * ((dp - rs) * inv)
            db_ref[hh] += ds
            dsb = ds.astype(BF16)
            dqs.append(_dot(dsb, kw))
            dkw = dkw + _dot_tn(dsb, qh)
        dg_ref[...] = (drv * o * dsgate).astype(BF16)
        dq_ref[...] = (jnp.where(first, dqs[0], dqs[1]) * ATT_SCALE).astype(BF16)
        dk_ref[pl.ds(w, ATT_WIN), :] += dkw
        dv_ref[pl.ds(w, ATT_WIN), :] += dvw

    blk = pl.BlockSpec((ATT_QB, 128), lambda hp, b: (b, hp))
    kblk = pl.BlockSpec((Lp, 128), lambda hp, b: (0, hp))
    vblk = pl.BlockSpec((Lp, 128), lambda hp, b: (0, ATT_PAIRS + hp))
    bblk = pl.BlockSpec((2, ATT_QB, ATT_WIN), lambda hp, b: (hp, 0, 0))
    return pl.pallas_call(
        body, name="ca_bwd", grid=(ATT_PAIRS, nb),
        in_specs=[blk, kblk, vblk, blk, _ca_bias_spec(), blk, blk],
        out_specs=[blk, blk, kblk, kblk, bblk],
        out_shape=[jax.ShapeDtypeStruct((L, D), BF16), jax.ShapeDtypeStruct((L, D), BF16),
                   jax.ShapeDtypeStruct((Lp, D), F32), jax.ShapeDtypeStruct((Lp, D), F32),
                   jax.ShapeDtypeStruct(bias.shape[1:], F32)],
        compiler_params=_cp("parallel", "arbitrary"),
    )(q, kvp, kvp, gate, bias, dr, o)


def loss_bwd(x, target, g, tm=ROW_TILE):
    L, D = x.shape
    tm = min(tm, L)

    def body(x_ref, t_ref, g_ref, loss_ref, dx_ref, dg_ref):
        @pl.when(pl.program_id(0) == 0)
        def _():
            loss_ref[...] = jnp.zeros_like(loss_ref)
            dg_ref[...] = jnp.zeros_like(dg_ref)

        xv = x_ref[...]
        gv = g_ref[...]
        _, xhat = _rms_parts(xv)
        err = xhat * gv - t_ref[...]
        loss_ref[...] += 0.5 * jnp.sum(jnp.sum(err * err, axis=-1, keepdims=True), axis=0, keepdims=True) / D
        dx, dg = _rms_bwd(xv, gv, err / D)
        dx_ref[...] = dx
        dg_ref[...] += dg

    row = pl.BlockSpec((tm, D), lambda t: (t, 0))
    return pl.pallas_call(
        body, name="loss_bwd", grid=(L // tm,),
        in_specs=[row, row, _full(g.shape)],
        out_specs=[_full((1, 128)), row, _full((1, D))],
        out_shape=[jax.ShapeDtypeStruct((1, 128), F32), jax.ShapeDtypeStruct((L, D), F32),
                   jax.ShapeDtypeStruct((1, D), F32)],
        compiler_params=_cp("arbitrary"),
    )(x, target, g)


_ADAM_C1 = 1.0 / (1.0 - ADAM_B1 ** ADAM_STEP)
_ADAM_C2 = 1.0 / (1.0 - ADAM_B2 ** ADAM_STEP)


def _adam_update(w, g, m, v):
    mn = ADAM_B1 * m + (1.0 - ADAM_B1) * g
    vn = ADAM_B2 * v + (1.0 - ADAM_B2) * g * g
    delta = -ADAM_LR * ((mn * _ADAM_C1) / (jnp.sqrt(vn * _ADAM_C2) + ADAM_EPS) + ADAM_WD * w)
    return delta, mn, vn


def adamw(w, g, m, v, name, tr=512):
    R, C = w.shape
    tr = min(tr, R)

    def body(w_ref, g_ref, m_ref, v_ref, d_ref, mo_ref, vo_ref):
        d_ref[...], mo_ref[...], vo_ref[...] = _adam_update(w_ref[...], g_ref[...], m_ref[...], v_ref[...])

    blk = pl.BlockSpec((tr, C), lambda i: (i, 0))
    sh = jax.ShapeDtypeStruct((R, C), F32)
    return pl.pallas_call(
        body, name=name, grid=(R // tr,), in_specs=[blk] * 4, out_specs=[blk] * 3,
        out_shape=[sh] * 3, compiler_params=_cp("parallel"),
    )(w, g, m, v)


def adamw_allreduce(gathered, w, m, v, shard, name, slot=None):
    R, C = w.shape
    sharded = slot is None and gathered.shape[2] != C

    def body(s_ref, ga_ref, w_ref, m_ref, v_ref, g_ref, d_ref, mo_ref, vo_ref):
        take = (lambda d: ga_ref[d]) if slot is None else (lambda d: ga_ref[d, slot:slot + R, 0:C])
        g = take(0)
        for d in range(1, N_DEV):
            g = g + take(d)
        g_ref[...] = g
        d_ref[...], mo_ref[...], vo_ref[...] = _adam_update(w_ref[...], g, m_ref[...], v_ref[...])

    blk = pl.BlockSpec((R, C), lambda i, s_ref: (0, 0))
    if slot is not None:
        gblk = pl.BlockSpec(gathered.shape, lambda i, s_ref: (0, 0, 0))
    else:
        gblk = pl.BlockSpec((N_DEV, R, C),
                            (lambda i, s_ref: (0, 0, s_ref[0])) if sharded else (lambda i, s_ref: (0, 0, 0)))
    sh = jax.ShapeDtypeStruct((R, C), F32)
    return pl.pallas_call(
        body, name=name,
        grid_spec=pltpu.PrefetchScalarGridSpec(num_scalar_prefetch=1, grid=(1,), in_specs=[gblk, blk, blk, blk],
                                               out_specs=[blk] * 4),
        out_shape=[sh] * 4, compiler_params=_cp("arbitrary"),
    )(shard, gathered, w, m, v)


LATE = ("ev_s5_glu_w", "ev_w_out", "od_w_in", "od_w_out", "xa_w_qg", "xa_w_kv", "xa_w_o")
EARLY_GRADS = ("od_w_in", "od_w_out", "xa_w_qg", "xa_w_kv", "xa_w_o", "ev_w_out", "ev_s5_glu_w")


def _reduce_to_chip(gs, names, core, tag):
    from_sibling = sibling_send_other_half(gs, "sibling_send_" + tag)
    return [sum_with_sibling(gi, ri, core, "sum_sibling_" + n) for n, gi, ri in zip(names, gs, from_sibling)]


def local_step(x, mem, target, p, gw, late, place, core):
    row = lambda a: a.reshape(1, -1)
    D = D_MODEL
    L = x.shape[0]
    g, big = {}, {}
    gw = dict(gw)

    z, h0b = norm_mm(x, p["ev_norm_g"], gw["ev_w_in"], [((0, 1, 2, 3), F32, 0)], "ev_in")
    ys, s5_saved, landed = s5_mixer_core_fwd(
        z, p["ev_s5_lambda_re"][0], p["ev_s5_lambda_im"][0], p["ev_s5_log_dt"][0], p["ev_s5_b_re"][0],
        p["ev_s5_b_im"][0], p["ev_s5_c_re"][0], p["ev_s5_c_im"][0], p["ev_s5_d"][0],
        carried=carried_allgather([late[n] for n in LATE]))
    for n, gth in zip(LATE, landed):
        rows = gth.shape[1]
        gw[n] = gth.reshape(N_CHIPS, 2, rows // 2, gth.shape[2]) if n.startswith("xa_") else gth
    memn_b = rms_fwd(mem, row(p["mem_norm_g"]), "mem_norm")
    kvs = [mm_cols(memn_b, gw["xa_w_kv"], l, f"xa_kv{l}", BF16) for l in range(2)]
    conv_w = p["ev_conv_w"][0]
    c = conv_fwd(z, conv_w, p["ev_conv_b"])
    tail = (gw["ev_s5_glu_w"], p["ev_s5_glu_b"], p["ev_conv_ln_g"], p["ev_conv_ln_b"], gw["ev_w_out"])
    x1 = ev_tail_fwd(ys, z, c, x, *tail)
    xa0 = (row(p["xa_norm_g"][0]), gw["xa_w_qg"], kvs[0], gw["xa_w_o"], 0)
    x2 = xa_fwd(x1, *xa0, "xa_fwd0")

    q, kvp, gate, h1b = norm_mm(x2, p["od_norm_g"], gw["od_w_in"],
                                [((0,), BF16, 0), ((1, 2), BF16, ATT_PAD), ((3,), F32, 0)], "od_in")
    kvp = zero_rows(kvp, ATT_PAD, "od_kv_pad")
    bias = att_bias(p["od_rel_bias"][0])
    r, att_o = ca_fwd(q, kvp, gate, bias)
    x3 = mm_res(r, gw["od_w_out"], x2, "od_out")
    xa1 = (row(p["xa_norm_g"][1]), gw["xa_w_qg"], kvs[1], gw["xa_w_o"], 1)
    x4 = xa_fwd(x3, *xa1, "xa_fwd1")

    loss, dx4, dgf = loss_bwd(x4, target, row(p["final_norm_g"]))
    g["final_norm_g"] = dgf.reshape(D)

    dx3, dqg1, hx1, rx1, dkv1, dgxa1 = xa_bwd(x3, dx4, *xa1, "xa_bwd1")
    dwqg = mm_tn(hx1, dqg1, "xa_dwqg1", ("cols", 1))
    dwo = mm_tn(rx1, dx4, "xa_dwo1", ("rows", 1))

    big["od_w_out"] = mm_tn(r, dx3, "od_dwout", ("rows",))
    dr = mm_nt_rows(dx3, gw["od_w_out"], "od_out_bwd")
    dq, dgate, dkp, dvp, dbias = ca_bwd(q, kvp, gate, bias, dr, att_o)
    pieces, offs = (dq, dkp, dvp, dgate), (0, ATT_PAD, ATT_PAD, 0)
    dwin = None
    for s in range(N_CHIPS):
        dwin = mm_tn(h1b, pieces[s], f"od_dwin{s}", ("slab", s), into=dwin, b_off=offs[s],
                     bl=ATT_PAD if offs[s] else 1024)
    big["od_w_in"] = dwin
    dx2, dgod = mm_nt_normbwd(pieces, offs, gw["od_w_in"], x2, p["od_norm_g"], dx3, "od_in_bwd")
    g["od_norm_g"] = dgod
    g["od_rel_bias"] = relbias_bwd(dbias)[None]

    dx1, dqg0, hx0, rx0, dkv0, dgxa0 = xa_bwd(x1, dx2, *xa0, "xa_bwd0")
    big["xa_w_qg"] = mm_tn(hx0, dqg0, "xa_dwqg0", ("cols", 0), into=dwqg)
    big["xa_w_o"] = mm_tn(rx0, dx2, "xa_dwo0", ("rows", 0), into=dwo)
    g["xa_norm_g"] = jnp.concatenate([dgxa0, dgxa1], axis=0)

    dys, dc, dz, ra, z1b, dtb, dbglu, dlng, dlnb = ev_tail_bwd(ys, z, c, dx1, *tail)
    big["ev_w_out"] = mm_tn(ra, dx1, "ev_dwout", ("rows",))
    big["ev_s5_glu_w"] = mm_tn(z1b, dtb, "ev_dwglu", ("rows",))
    g["ev_s5_glu_b"], g["ev_conv_ln_g"], g["ev_conv_ln_b"] = dbglu, dlng, dlnb
    dz, dconvw, dconvb = conv_bwd(z, dc, dz, conv_w)
    g["ev_conv_w"] = dconvw[None, :CONV_KERNEL]
    g["ev_conv_b"] = dconvb

    dwkv = mm_tn(memn_b, dkv1, "xa_dwkv1", ("cols", 1), bl=MEM_LEN)
    big["xa_w_kv"] = mm_tn(memn_b, dkv0, "xa_dwkv0", ("cols", 0), into=dwkv, bl=MEM_LEN)
    dmem0 = mm_nt_cols(dkv0, gw["xa_w_kv"], 0, "xa_kv_bwd0")
    dmem1 = mm_nt_cols(dkv1, gw["xa_w_kv"], 1, "xa_kv_bwd1")
    g["mem_norm_g"] = rms_dgain(mem, dmem0, dmem1, "mem_norm_bwd").reshape(D)

    shard_major = lambda t: t.reshape((-1,) + t.shape[-2:])
    chip_sums = _reduce_to_chip([shard_major(big[n]) for n in EARLY_GRADS], EARLY_GRADS, core, "early")
    dz, s5g, from_chips = s5_mixer_core_bwd(z, dys, dz, p["ev_s5_lambda_re"][0], p["ev_s5_lambda_im"][0], s5_saved,
                                            carried=carried_chips_exchange(chip_sums))
    reduced = {n: sum_chips(ci, ri, place, "sum_chips_" + n) for n, ci, ri in zip(EARLY_GRADS, chip_sums, from_chips)}
    for n, v in s5g.items():
        g["ev_s5_" + n] = v[None]
    dwin_ev = mm_tn(h0b, dz, "ev_dwin", ("cols",))
    grad_x, dgev = mm_nt_normbwd((dz,), (0,), gw["ev_w_in"], x, p["ev_norm_g"], dx1, "ev_in_bwd")
    g["ev_norm_g"] = dgev
    chip_sum = _reduce_to_chip([dwin_ev], ["ev_w_in"], core, "last")
    reduced["ev_w_in"] = sum_chips(chip_sum[0], chips_exchange(chip_sum)[0], place, "sum_chips_ev_w_in")
    return loss, grad_x, g, reduced


def _me():
    return lax.axis_index("x"), lax.axis_index("y"), lax.axis_index("c")


def _other_chips(x, y):
    return [(1 - x, y), (x, 1 - y), (1 - x, 1 - y)]


def _remote(src, dst, send_sems, recv_sems, k, to):
    return pltpu.make_async_remote_copy(src_ref=src, dst_ref=dst, send_sem=send_sems.at[k], recv_sem=recv_sems.at[k],
                                        device_id=to, device_id_type=MESH)


def _rows_half(ref, h):
    H = ref.shape[-2] // 2
    return ref.at[(slice(None),) * (len(ref.shape) - 2) + (pl.ds(h * H, H), slice(None))]


def allgather_chip_blocks(halved, whole):
    nh, nw = len(halved), len(whole)
    n = nh + nw

    def body(*refs):
        ins, outs = refs[:n], refs[n:2 * n]
        send_sems, recv_sems, local_sems = refs[2 * n:]
        x, y, c = _me()
        sib = (x, y, 1 - c)
        chips = _other_chips(x, y)
        me = 2 * x + y
        local = [pltpu.make_async_copy(ins[i], outs[i].at[me], local_sems.at[i]) for i in range(n)]
        for cp in local:
            cp.start()
        first, passed = [], []
        for i in range(n):
            for j, (cx, cy) in enumerate(chips):
                if i < nh:
                    src, dst = _rows_half(ins[i], c), _rows_half(outs[i].at[me], c)
                    k = 6 * i + j
                else:
                    src, dst = ins[i], outs[i].at[me]
                    k = 6 * nh + 3 * (i - nh) + j
                first.append(_remote(src, dst, send_sems, recv_sems, k, (cx, cy, c)))
        for cp in first:
            cp.start()
        for j, (cx, cy) in enumerate(chips):
            for i in range(nh):
                got = _rows_half(outs[i].at[2 * cx + cy], c)
                _remote(got, got, send_sems, recv_sems, 6 * i + j, (cx, cy, c)).wait_recv()
                fw = _remote(got, got, send_sems, recv_sems, 6 * i + 3 + j, sib)
                fw.start()
                passed.append(fw)
        for j, (cx, cy) in enumerate(chips):
            for i in range(nh):
                got = _rows_half(outs[i].at[2 * cx + cy], 1 - c)
                _remote(got, got, send_sems, recv_sems, 6 * i + 3 + j, sib).wait_recv()
            for i in range(nh, n):
                got = outs[i].at[2 * cx + cy]
                _remote(got, got, send_sems, recv_sems, 6 * nh + 3 * (i - nh) + j, (cx, cy, c)).wait_recv()
        for cp in first + passed:
            cp.wait_send()
        for cp in local:
            cp.wait()

    arrays = list(halved) + list(whole)
    nsem = 6 * nh + 3 * nw
    return pl.pallas_call(
        body, name="allgather_chip_blocks", in_specs=[ANY] * n, out_specs=[ANY] * n,
        out_shape=[jax.ShapeDtypeStruct((N_CHIPS,) + a.shape, a.dtype) for a in arrays],
        scratch_shapes=[pltpu.SemaphoreType.DMA((nsem,)), pltpu.SemaphoreType.DMA((nsem,)),
                        pltpu.SemaphoreType.DMA((n,))],
    )(*arrays)


def allgather_devices(vs):
    n = len(vs)

    def body(*refs):
        ins, outs = refs[:n], refs[n:2 * n]
        send_sems, recv_sems, local_sems = refs[2 * n:]
        x, y, c = _me()
        sib = (x, y, 1 - c)
        chips = _other_chips(x, y)
        me = 4 * x + 2 * y + c
        local = [pltpu.make_async_copy(ins[i], outs[i].at[me], local_sems.at[i]) for i in range(n)]
        for cp in local:
            cp.start()
        first, passed = [], []
        for i in range(n):
            first.append(_remote(ins[i], outs[i].at[me], send_sems, recv_sems, 7 * i, sib))
            for j, (cx, cy) in enumerate(chips):
                first.append(_remote(ins[i], outs[i].at[me], send_sems, recv_sems, 7 * i + 1 + j, (cx, cy, c)))
        for cp in first:
            cp.start()
        for j, (cx, cy) in enumerate(chips):
            for i in range(n):
                got = outs[i].at[4 * cx + 2 * cy + c]
                _remote(got, got, send_sems, recv_sems, 7 * i + 1 + j, (cx, cy, c)).wait_recv()
                fw = _remote(got, got, send_sems, recv_sems, 7 * i + 4 + j, sib)
                fw.start()
                passed.append(fw)
        for i in range(n):
            got = outs[i].at[4 * x + 2 * y + (1 - c)]
            _remote(got, got, send_sems, recv_sems, 7 * i, sib).wait_recv()
            for j, (cx, cy) in enumerate(chips):
                got = outs[i].at[4 * cx + 2 * cy + (1 - c)]
                _remote(got, got, send_sems, recv_sems, 7 * i + 4 + j, sib).wait_recv()
        for cp in first + passed:
            cp.wait_send()
        for cp in local:
            cp.wait()

    return pl.pallas_call(
        body, name="allgather_devices", in_specs=[ANY] * n, out_specs=[ANY] * n,
        out_shape=[jax.ShapeDtypeStruct((N_DEV,) + v.shape, v.dtype) for v in vs],
        scratch_shapes=[pltpu.SemaphoreType.DMA((7 * n,)), pltpu.SemaphoreType.DMA((7 * n,)),
                        pltpu.SemaphoreType.DMA((n,))],
    )(*vs)


def sibling_send_other_half(gs, name):
    n = len(gs)

    def body(*refs):
        ins, outs = refs[:n], refs[n:2 * n]
        send_sems, recv_sems = refs[2 * n:]
        x, y, c = _me()
        cps = [_remote(_rows_half(ins[i], 1 - c), outs[i], send_sems, recv_sems, i, (x, y, 1 - c)) for i in range(n)]
        for cp in cps:
            cp.start()
        for cp in cps:
            cp.wait()

    return pl.pallas_call(
        body, name=name, in_specs=[ANY] * n, out_specs=[ANY] * n,
        out_shape=[jax.ShapeDtypeStruct((g.shape[0], g.shape[1] // 2, g.shape[2]), g.dtype) for g in gs],
        scratch_shapes=[pltpu.SemaphoreType.DMA((n,)), pltpu.SemaphoreType.DMA((n,))],
    )(*gs)


def chips_exchange(parts):
    n = len(parts)

    def body(*refs):
        ins, outs = refs[:n], refs[n:2 * n]
        send_sems, recv_sems = refs[2 * n:]
        x, y, c = _me()
        cps = []
        for i in range(n):
            nl = ins[i].shape[0] // N_CHIPS
            for j, (cx, cy) in enumerate(_other_chips(x, y)):
                cps.append(_remote(ins[i].at[pl.ds((2 * cx + cy) * nl, nl)], outs[i].at[j], send_sems, recv_sems,
                                   3 * i + j, (cx, cy, c)))
        for cp in cps:
            cp.start()
        for cp in cps:
            cp.wait()

    return pl.pallas_call(
        body, name="chips_exchange", in_specs=[ANY] * n, out_specs=[ANY] * n,
        out_shape=[jax.ShapeDtypeStruct((3, a.shape[0] // N_CHIPS) + a.shape[1:], a.dtype) for a in parts],
        scratch_shapes=[pltpu.SemaphoreType.DMA((3 * n,)), pltpu.SemaphoreType.DMA((3 * n,))],
    )(*parts)


def sibling_share(fulls):
    n = len(fulls)

    def body(*refs):
        outs = refs[n:2 * n]
        send_sems, recv_sems = refs[2 * n:]
        x, y, c = _me()
        cps = [_remote(_rows_half(outs[i], c), _rows_half(outs[i], c), send_sems, recv_sems, i, (x, y, 1 - c))
               for i in range(n)]
        for cp in cps:
            cp.start()
        for i in range(n):
            got = _rows_half(outs[i], 1 - c)
            _remote(got, got, send_sems, recv_sems, i, (x, y, 1 - c)).wait_recv()
        for cp in cps:
            cp.wait_send()

    return pl.pallas_call(
        body, name="sibling_share", in_specs=[ANY] * n, out_specs=[ANY] * n,
        out_shape=[jax.ShapeDtypeStruct(f.shape, f.dtype) for f in fulls],
        input_output_aliases={i: i for i in range(n)},
        scratch_shapes=[pltpu.SemaphoreType.DMA((n,)), pltpu.SemaphoreType.DMA((n,))],
    )(*fulls)


def sum_with_sibling(g, recv, core, name):
    S, H, C = recv.shape
    tr = min(512, H)

    def body(c_ref, g_ref, r_ref, o_ref):
        o_ref[...] = (g_ref[...].astype(F32) + r_ref[...].astype(F32)).astype(o_ref.dtype)

    nb = H // tr
    return pl.pallas_call(
        body, name=name,
        grid_spec=pltpu.PrefetchScalarGridSpec(
            num_scalar_prefetch=1, grid=(S, nb),
            in_specs=[pl.BlockSpec((None, tr, C), lambda s, i, c_ref: (s, c_ref[0] * nb + i, 0)),
                      pl.BlockSpec((None, tr, C), lambda s, i, c_ref: (s, i, 0))],
            out_specs=pl.BlockSpec((None, tr, C), lambda s, i, c_ref: (s, i, 0))),
        out_shape=jax.ShapeDtypeStruct((S, H, C), g.dtype), compiler_params=_cp("parallel", "parallel"),
    )(core, g, recv)


def sum_chips(a, recv, place, name):
    _, nl, H, C = recv.shape
    tr = min(512, H)
    nb = H // tr

    def body(p_ref, a_ref, r_ref, o_ref):
        acc = a_ref[...].astype(F32)
        for j in range(3):
            acc = acc + r_ref[j].astype(F32)
        o_ref[...] = acc

    return pl.pallas_call(
        body, name=name,
        grid_spec=pltpu.PrefetchScalarGridSpec(
            num_scalar_prefetch=1, grid=(nl, nb),
            in_specs=[pl.BlockSpec((None, tr, C), lambda l, i, p_ref: (p_ref[0] * nl + l, i, 0)),
                      pl.BlockSpec((3, None, tr, C), lambda l, i, p_ref: (0, l, i, 0))],
            out_specs=pl.BlockSpec((None, tr, C), lambda l, i, p_ref: (l, p_ref[1] * nb + i, 0))),
        out_shape=jax.ShapeDtypeStruct((nl, 2 * H, C), F32), compiler_params=_cp("parallel", "parallel"),
    )(place, a, recv)


def pack_rows(arrays, name):
    starts, r0 = [], 0
    for a in arrays:
        if a.shape[0] >= SUBLANES:
            r0 = -(-r0 // SUBLANES) * SUBLANES
        starts.append(r0)
        r0 += a.shape[0]
    r0 = -(-r0 // SUBLANES) * SUBLANES
    n = len(arrays)

    def body(*refs):
        o_ref = refs[n]
        o_ref[...] = jnp.zeros_like(o_ref)
        for a_ref, s in zip(refs[:n], starts):
            r, c = a_ref.shape
            o_ref[s:s + r, 0:c] = a_ref[...]

    out = pl.pallas_call(body, name=name, out_shape=jax.ShapeDtypeStruct((r0, PACK_COLS), F32))(*arrays)
    return out, starts


def sum_slot(gathered, slot, shape, name):
    r, c = shape

    def body(ga_ref, o_ref):
        acc = ga_ref[0, slot:slot + r, 0:c]
        for d in range(1, N_DEV):
            acc = acc + ga_ref[d, slot:slot + r, 0:c]
        o_ref[...] = acc

    return pl.pallas_call(body, name=name, out_shape=jax.ShapeDtypeStruct((r, c), F32))(gathered)


def carried_allgather(blocks):
    n = len(blocks)

    def first_hop(ins, outs, sems, i, j, chip, x, y, c):
        me = 2 * x + y
        return _remote(_rows_half(ins[i], c), _rows_half(outs[i].at[me], c), sems[0], sems[1], 6 * i + j, (*chip, c))

    def start(ins, outs, sems):
        x, y, c = _me()
        for i in range(n):
            pltpu.make_async_copy(ins[i], outs[i].at[2 * x + y], sems[2].at[i]).start()
        for i in range(n):
            for j, chip in enumerate(_other_chips(x, y)):
                first_hop(ins, outs, sems, i, j, chip, x, y, c).start()

    def finish(ins, outs, sems):
        x, y, c = _me()
        sib = (x, y, 1 - c)
        chips = _other_chips(x, y)
        passed = []
        for j, (cx, cy) in enumerate(chips):
            for i in range(n):
                got = _rows_half(outs[i].at[2 * cx + cy], c)
                _remote(got, got, sems[0], sems[1], 6 * i + j, (cx, cy, c)).wait_recv()
                fw = _remote(got, got, sems[0], sems[1], 6 * i + 3 + j, sib)
                fw.start()
                passed.append(fw)
        for j, (cx, cy) in enumerate(chips):
            for i in range(n):
                got = _rows_half(outs[i].at[2 * cx + cy], 1 - c)
                _remote(got, got, sems[0], sems[1], 6 * i + 3 + j, sib).wait_recv()
        for i in range(n):
            for j, chip in enumerate(chips):
                first_hop(ins, outs, sems, i, j, chip, x, y, c).wait_send()
        for fw in passed:
            fw.wait_send()
        for i in range(n):
            pltpu.make_async_copy(ins[i], outs[i].at[2 * x + y], sems[2].at[i]).wait()

    return Carried(blocks, [jax.ShapeDtypeStruct((N_CHIPS,) + b.shape, b.dtype) for b in blocks],
                   [pltpu.SemaphoreType.DMA((6 * n,)), pltpu.SemaphoreType.DMA((6 * n,)), pltpu.SemaphoreType.DMA((n,))],
                   start, None, finish)


def carried_chips_exchange(parts):
    n = len(parts)

    def copies(ins, outs, sems):
        x, y, c = _me()
        cps = []
        for i in range(n):
            nl = ins[i].shape[0] // N_CHIPS
            for j, (cx, cy) in enumerate(_other_chips(x, y)):
                cps.append(_remote(ins[i].at[pl.ds((2 * cx + cy) * nl, nl)], outs[i].at[j], sems[0], sems[1],
                                   3 * i + j, (cx, cy, c)))
        return cps

    def start(ins, outs, sems):
        for cp in copies(ins, outs, sems):
            cp.start()

    def finish(ins, outs, sems):
        for cp in copies(ins, outs, sems):
            cp.wait()

    return Carried(parts, [jax.ShapeDtypeStruct((3, a.shape[0] // N_CHIPS) + a.shape[1:], a.dtype) for a in parts],
                   [pltpu.SemaphoreType.DMA((3 * n,)), pltpu.SemaphoreType.DMA((3 * n,))], start, None, finish)


BIG = ("ev_w_in", "ev_s5_glu_w", "ev_w_out", "od_w_in", "od_w_out", "xa_w_qg", "xa_w_kv", "xa_w_o")
SHARDED_F32 = (("ev_conv_w", 2), ("od_norm_g", 1))
SMALL = ("mem_norm_g", "ev_norm_g", "ev_s5_lambda_re", "ev_s5_lambda_im", "ev_s5_log_dt", "ev_s5_b_re", "ev_s5_b_im",
         "ev_s5_c_re", "ev_s5_c_im", "ev_s5_d", "ev_s5_glu_b", "ev_conv_b", "ev_conv_ln_g", "ev_conv_ln_b",
         "od_rel_bias", "xa_norm_g", "final_norm_g")
NARROW = ("ev_s5_c_re", "ev_s5_c_im")
DENSE_B = ("ev_s5_b_re", "ev_s5_b_im")
PACK_COLS = 1024
WEIGHTS = ("mem_norm_g", "ev_norm_g", "ev_w_in", "ev_s5_lambda_re", "ev_s5_lambda_im", "ev_s5_log_dt", "ev_s5_b_re",
           "ev_s5_b_im", "ev_s5_c_re", "ev_s5_c_im", "ev_s5_d", "ev_s5_glu_w", "ev_s5_glu_b", "ev_conv_w", "ev_conv_b",
           "ev_conv_ln_g", "ev_conv_ln_b", "ev_w_out", "od_norm_g", "od_w_in", "od_rel_bias", "od_w_out", "xa_norm_g",
           "xa_w_qg", "xa_w_kv", "xa_w_o", "final_norm_g")


def _as2d(a):
    return a.reshape(1, -1) if a.ndim == 1 else a.reshape(-1, a.shape[-1])


def kernel(x, mem, mem_norm_g, ev_norm_g, ev_w_in, ev_s5_lambda_re, ev_s5_lambda_im, ev_s5_log_dt, ev_s5_b_re, ev_s5_b_im, ev_s5_c_re, ev_s5_c_im, ev_s5_d, ev_s5_glu_w, ev_s5_glu_b, ev_conv_w, ev_conv_b, ev_conv_ln_g, ev_conv_ln_b, ev_w_out, od_norm_g, od_w_in, od_rel_bias, od_w_out, xa_norm_g, xa_w_qg, xa_w_kv, xa_w_o, final_norm_g, loss_target, m_mem_norm_g, m_ev_norm_g, m_ev_w_in, m_ev_s5_lambda_re, m_ev_s5_lambda_im, m_ev_s5_log_dt, m_ev_s5_b_re, m_ev_s5_b_im, m_ev_s5_c_re, m_ev_s5_c_im, m_ev_s5_d, m_ev_s5_glu_w, m_ev_s5_glu_b, m_ev_conv_w, m_ev_conv_b, m_ev_conv_ln_g, m_ev_conv_ln_b, m_ev_w_out, m_od_norm_g, m_od_w_in, m_od_rel_bias, m_od_w_out, m_xa_norm_g, m_xa_w_qg, m_xa_w_kv, m_xa_w_o, m_final_norm_g, v_mem_norm_g, v_ev_norm_g, v_ev_w_in, v_ev_s5_lambda_re, v_ev_s5_lambda_im, v_ev_s5_log_dt, v_ev_s5_b_re, v_ev_s5_b_im, v_ev_s5_c_re, v_ev_s5_c_im, v_ev_s5_d, v_ev_s5_glu_w, v_ev_s5_glu_b, v_ev_conv_w, v_ev_conv_b, v_ev_conv_ln_g, v_ev_conv_ln_b, v_ev_w_out, v_od_norm_g, v_od_w_in, v_od_rel_bias, v_od_w_out, v_xa_norm_g, v_xa_w_qg, v_xa_w_kv, v_xa_w_o, v_final_norm_g):
    a = dict(locals())
    w = {n: a[n] for n in WEIGHTS}
    shard = (2 * lax.axis_index("x") + lax.axis_index("y")).reshape(1).astype(jnp.int32)
    core = lax.axis_index("c").reshape(1).astype(jnp.int32)

    place = jnp.concatenate([shard, core])

    blocks = {n: w[n].astype(BF16).reshape(-1, w[n].shape[-1]) for n in BIG}
    early = [n for n in BIG if n not in LATE]
    gathered = allgather_chip_blocks([blocks[n] for n in early], [_as2d(w[n]) for n, _ in SHARDED_F32])
    gw = dict(zip(early, gathered))
    p = {n: w[n] for n in SMALL}
    conv_g, odn_g = gathered[len(early):]
    p["ev_conv_w"] = jnp.concatenate([conv_g[s] for s in range(N_CHIPS)], axis=1)[None]
    p["od_norm_g"] = odn_g.reshape(1, D_MODEL)

    loss, grad_x, g, reduced = local_step(x[0], mem[0], loss_target[0], p, gw, {n: blocks[n] for n in LATE},
                                          place, core)
    loss = lax.psum(loss[0, 0], ("x", "y", "c"))
    g_big = dict(zip(BIG, sibling_share([reduced[n] for n in BIG])))

    out = {tag: {} for tag in ("grad", "delta", "m", "v")}
    for n in BIG:
        sh = w[n].shape
        to2d = lambda t: t.reshape(-1, sh[-1])
        gn = to2d(g_big[n])
        d, mn, vn = adamw(to2d(w[n]), gn, to2d(a["m_" + n]), to2d(a["v_" + n]), "adamw_" + n)
        for tag, val in zip(("grad", "delta", "m", "v"), (gn, d, mn, vn)):
            out[tag][n] = val.reshape(sh)

    packed_names = [n for n in SMALL if n not in NARROW]
    single_names = list(NARROW) + [n for n, _ in SHARDED_F32]
    packed, slots = pack_rows([_as2d(g[n]) for n in packed_names], "pack_small_grads")
    gath = allgather_devices([packed] + [_as2d(g[n]) for n in single_names])
    jobs = [(n, gath[0], s) for n, s in zip(packed_names, slots)]
    jobs += [(n, gt, None) for n, gt in zip(single_names, gath[1:])]
    for n, gt, slot in jobs:
        sh = w[n].shape
        w2, m2, v2 = _as2d(w[n]), _as2d(a["m_" + n]), _as2d(a["v_" + n])
        if n in DENSE_B:
            gn = _as2d(s5_b_from_dense(sum_slot(gt, slot, g[n].shape[-2:], "sum_" + n)))
            d, mn, vn = adamw(w2, gn, m2, v2, "adamw_" + n)
        else:
            gn, d, mn, vn = adamw_allreduce(gt, w2, m2, v2, shard, "adamw_" + n, slot=slot)
        for tag, val in zip(("grad", "delta", "m", "v"), (gn, d, mn, vn)):
            out[tag][n] = val.reshape(sh)

    res = [loss, grad_x[None]]
    for tag in ("grad", "delta", "m", "v"):
        res += [out[tag][n] for n in WEIGHTS]
    return tuple(res)
```

```python
import math

import jax
import jax.numpy as jnp
import numpy as np
from jax import lax
from jax.experimental import pallas as pl
from jax.experimental.pallas import tpu as pltpu

F32 = jnp.float32
BF16 = jnp.bfloat16

D_MODEL = 1024
CHUNK = 64
LEFT_CHUNKS = 8
S5_WIDTH = 512
S5_GROUP = 16
S5_GROUPS = 32
S5_STATE = 64
S5_COLS = S5_GROUPS * S5_STATE
S5_SPLIT = 4
S5_CC = S5_COLS // S5_SPLIT
S5_UC = S5_WIDTH // S5_SPLIT
CONV_WIDTH = 512
CONV_KERNEL = 31
CONV_HALO = 32
ATT_HEADS = 16
ATT_HEAD_DIM = 64
MAX_REL = 128
MEM_LEN = 256
XA_HEADS = 4
XA_HEAD_DIM = 256
EPS = 1e-6
EVEN_IN = 2560
ODD_IN = 4096

ADAM_LR = 0.001
ADAM_B1 = 0.9
ADAM_B2 = 0.999
ADAM_EPS = 1e-08
ADAM_WD = 0.01
ADAM_STEP = 10

ROW_TILE = 256
MM_TILE = 512
S5_TILE = 512
ATT_QB = 256
ATT_PAD = LEFT_CHUNKS * CHUNK
ATT_WIN = ATT_PAD + ATT_QB
VMEM_LIMIT_V7X = 56 * 1024 * 1024
NEG = -1e30
LANES = 128
N_CHIPS = 4
N_DEV = 8

MESH = pl.DeviceIdType.MESH
ANY = pl.BlockSpec(memory_space=pl.ANY)


def _cp(*sem, vmem=VMEM_LIMIT_V7X):
    return pltpu.CompilerParams(dimension_semantics=sem if sem else None, vmem_limit_bytes=vmem)


def _full(shape):
    n = len(shape)
    return pl.BlockSpec(shape, lambda *_: (0,) * n)


def _wspec(w, layer=None):
    if layer is None:
        return _full(w.shape)
    s, _, r, c = w.shape
    return pl.BlockSpec((s, None, r, c), lambda *_: (0, layer, 0, 0))


def _lane_tile(n, cap):
    return max(t for t in range(LANES, min(n, cap) + 1, LANES) if n % t == 0)


def _sigmoid(x):
    return 1.0 / (1.0 + jnp.exp(-x))


def _silu(x):
    return x * _sigmoid(x)


def _silu_pair(x):
    s = _sigmoid(x)
    return x * s, s * (1.0 + x * (1.0 - s))


_GELU_C = math.sqrt(2.0 / math.pi)


def _gelu(x):
    return 0.5 * x * (1.0 + jnp.tanh(_GELU_C * (x + 0.044715 * x * x * x)))


def _dgelu(x):
    t = jnp.tanh(_GELU_C * (x + 0.044715 * x * x * x))
    return 0.5 * (1.0 + t) + 0.5 * x * (1.0 - t * t) * _GELU_C * (1.0 + 3.0 * 0.044715 * x * x)


def _dot(a, b):
    return jnp.dot(a, b, preferred_element_type=F32)


def _dot_nt(a, b):
    return lax.dot_general(a, b, (((1,), (1,)), ((), ())), preferred_element_type=F32)


def _dot_tn(a, b):
    return lax.dot_general(a, b, (((0,), (0,)), ((), ())), preferred_element_type=F32)


def _dot_cols(a, w4, shards=range(N_CHIPS)):
    return jnp.concatenate([_dot(a, w4[s]) for s in shards], axis=1)


def _dot_rows(a, w4):
    r = w4.shape[1]
    acc = _dot(a[:, 0:r], w4[0])
    for s in range(1, N_CHIPS):
        acc = acc + _dot(a[:, s * r:(s + 1) * r], w4[s])
    return acc


def _dot_nt_cols(dys, w4):
    acc = _dot_nt(dys[0], w4[0])
    for s in range(1, N_CHIPS):
        acc = acc + _dot_nt(dys[s], w4[s])
    return acc


def _dot_nt_rows(dy, w4):
    return jnp.concatenate([_dot_nt(dy, w4[s]) for s in range(N_CHIPS)], axis=1)


def _col_pieces(v, n):
    return [v[:, s * n:(s + 1) * n] for s in range(N_CHIPS)]


def _rms_parts(xv):
    inv = lax.rsqrt(jnp.mean(xv * xv, axis=-1, keepdims=True) + EPS)
    return inv, xv * inv


def _rms_bwd(xv, g, dh):
    inv, xhat = _rms_parts(xv)
    dg = jnp.sum(dh * xhat, axis=0, keepdims=True)
    dxh = dh * g
    dx = inv * (dxh - xhat * jnp.mean(dxh * xhat, axis=-1, keepdims=True))
    return dx, dg


def norm_mm(x, g, w4, groups, name, tm=MM_TILE):
    M, D = x.shape
    n = w4.shape[2]
    tm = min(tm, M)

    def body(x_ref, g_ref, w_ref, *outs):
        _, xhat = _rms_parts(x_ref[...])
        hb = (xhat * g_ref[...]).astype(BF16)
        for o, (shards, dt, _) in zip(outs, groups):
            o[...] = _dot_cols(hb, w_ref, shards).astype(dt)
        outs[-1][...] = hb

    out_shape = [jax.ShapeDtypeStruct((M + pad, len(sh) * n), dt) for (sh, dt, pad) in groups]
    out_specs = [pl.BlockSpec((tm, len(sh) * n), lambda i, p=pad // tm: (i + p, 0)) for (sh, _, pad) in groups]
    out_shape.append(jax.ShapeDtypeStruct((M, D), BF16))
    out_specs.append(pl.BlockSpec((tm, D), lambda i: (i, 0)))
    return pl.pallas_call(
        body, name=name, grid=(M // tm,),
        in_specs=[pl.BlockSpec((tm, D), lambda i: (i, 0)), _full(g.shape), _full(w4.shape)],
        out_specs=out_specs, out_shape=out_shape, compiler_params=_cp("parallel"),
    )(x, g, w4)


def zero_rows(buf, rows, name, tm=ROW_TILE):
    C = buf.shape[1]

    def body(b_ref, o_ref):
        o_ref[...] = jnp.zeros_like(o_ref)

    return pl.pallas_call(
        body, name=name, grid=(rows // tm,), in_specs=[ANY],
        out_specs=pl.BlockSpec((tm, C), lambda i: (i, 0)),
        out_shape=jax.ShapeDtypeStruct(buf.shape, buf.dtype), input_output_aliases={0: 0},
        compiler_params=_cp("parallel"),
    )(buf)


def mm_res(a, w4, res, name, tm=MM_TILE):
    M, K = a.shape
    N = w4.shape[2]
    tm = min(tm, M)

    def body(a_ref, w_ref, r_ref, o_ref):
        o_ref[...] = r_ref[...] + _dot_rows(a_ref[...], w_ref)

    return pl.pallas_call(
        body, name=name, grid=(M // tm,),
        in_specs=[pl.BlockSpec((tm, K), lambda i: (i, 0)), _full(w4.shape), pl.BlockSpec((tm, N), lambda i: (i, 0))],
        out_specs=pl.BlockSpec((tm, N), lambda i: (i, 0)),
        out_shape=jax.ShapeDtypeStruct((M, N), F32), compiler_params=_cp("parallel"),
    )(a, w4, res)


def mm_cols(a, w, layer, name, out_dtype):
    M = a.shape[0]
    n = w.shape[3]

    def body(a_ref, w_ref, o_ref):
        o_ref[...] = _dot_cols(a_ref[...], w_ref).astype(out_dtype)

    return pl.pallas_call(
        body, name=name, grid=(1,), in_specs=[_full(a.shape), _wspec(w, layer)],
        out_specs=_full((M, N_CHIPS * n)), out_shape=jax.ShapeDtypeStruct((M, N_CHIPS * n), out_dtype),
        compiler_params=_cp("arbitrary"),
    )(a, w)


def mm_nt_cols(dy, w, layer, name):
    M = dy.shape[0]
    K, n = w.shape[2], w.shape[3]

    def body(d_ref, w_ref, o_ref):
        o_ref[...] = _dot_nt_cols(_col_pieces(d_ref[...].astype(BF16), n), w_ref)

    return pl.pallas_call(
        body, name=name, grid=(1,), in_specs=[_full(dy.shape), _wspec(w, layer)],
        out_specs=_full((M, K)), out_shape=jax.ShapeDtypeStruct((M, K), F32), compiler_params=_cp("arbitrary"),
    )(dy, w)


def mm_nt_rows(dy, w4, name, tm=MM_TILE):
    M, N = dy.shape
    K = N_CHIPS * w4.shape[1]
    tm = min(tm, M)

    def body(d_ref, w_ref, o_ref):
        o_ref[...] = _dot_nt_rows(d_ref[...].astype(BF16), w_ref)

    return pl.pallas_call(
        body, name=name, grid=(M // tm,),
        in_specs=[pl.BlockSpec((tm, N), lambda i: (i, 0)), _full(w4.shape)],
        out_specs=pl.BlockSpec((tm, K), lambda i: (i, 0)),
        out_shape=jax.ShapeDtypeStruct((M, K), F32), compiler_params=_cp("parallel"),
    )(dy, w4)


def mm_nt_normbwd(dys, offs, w4, x, g, dx_out, name, tm=MM_TILE):
    M, D = x.shape
    n = w4.shape[2]
    tm = min(tm, M)
    nd = len(dys)

    def body(*refs):
        d_refs = refs[:nd]
        w_ref, x_ref, g_ref, dxo_ref, dx_ref, dg_ref = refs[nd:]
        if nd == 1:
            pieces = _col_pieces(d_refs[0][...].astype(BF16), n)
        else:
            pieces = [r[...].astype(BF16) for r in d_refs]
        dh = _dot_nt_cols(pieces, w_ref)
        dx, dg = _rms_bwd(x_ref[...], g_ref[...], dh)
        dx_ref[...] = dxo_ref[...] + dx

        @pl.when(pl.program_id(0) == 0)
        def _():
            dg_ref[...] = jnp.zeros_like(dg_ref)

        dg_ref[...] += dg

    row = lambda c, off=0: pl.BlockSpec((tm, c), lambda i, p=off // tm: (i + p, 0))
    return pl.pallas_call(
        body, name=name, grid=(M // tm,),
        in_specs=[row(d.shape[1], off) for d, off in zip(dys, offs)] + [_full(w4.shape), row(D), _full(g.shape), row(D)],
        out_specs=[row(D), _full((1, D))],
        out_shape=[jax.ShapeDtypeStruct((M, D), F32), jax.ShapeDtypeStruct((1, D), F32)],
        compiler_params=_cp("arbitrary"),
    )(*dys, w4, x, g, dx_out)


def mm_tn(a, b, name, layout, into=None, b_off=0, out_dtype=BF16, bm=1024, bn=1280, bl=1024):
    L, K = a.shape
    N = b.shape[1]
    kind = layout[0]
    arg = layout[1] if len(layout) > 1 else None
    bm, bn, bl = _lane_tile(K, bm), _lane_tile(N, bn), min(bl, L)
    assert L % bl == 0 and b_off % bl == 0, (L, bl, b_off)
    nl = L // bl
    n_sh, r_sh = N // N_CHIPS, K // N_CHIPS
    lay = (None,) if arg is None else (None, None)
    mid = () if arg is None else (arg,)
    gs = 1
    if kind == "plain":
        oshape, oblock, oidx = (K, N), (bm, bn), lambda i, j, l: (i, j)
    elif kind == "slab":
        oshape, oblock, oidx = (N_CHIPS, K, N), (None, bm, bn), lambda i, j, l: (arg, i, j)
    elif kind == "cols":
        bn = max(bn - bn % n_sh, n_sh) if bn >= n_sh else _lane_tile(n_sh, bn)
        gs = max(bn // n_sh, 1)
        per = n_sh // bn if gs == 1 else 1
        oshape = (N_CHIPS,) + ((2,) if arg is not None else ()) + (K, n_sh)
        oblock = ((gs,) if gs > 1 else (None,)) + lay[1:] + (bm, min(bn, n_sh))
        oidx = lambda i, j, l: (j // per,) + mid + (i, j % per)
    else:
        bm = max(bm - bm % r_sh, r_sh) if bm >= r_sh else _lane_tile(r_sh, bm)
        gs = max(bm // r_sh, 1)
        per = r_sh // bm if gs == 1 else 1
        oshape = (N_CHIPS,) + ((2,) if arg is not None else ()) + (r_sh, N)
        oblock = ((gs,) if gs > 1 else (None,)) + lay[1:] + (min(bm, r_sh), bn)
        oidx = lambda i, j, l: (i // per,) + mid + (i % per, j)
    assert K % bm == 0 and N % bn == 0, (K, bm, N, bn)

    def body(a_ref, b_ref, *rest):
        o_ref, acc = rest[-2], rest[-1]
        l = pl.program_id(2)

        @pl.when(l == 0)
        def _():
            acc[...] = jnp.zeros_like(acc)

        acc[...] += _dot_tn(a_ref[...].astype(BF16), b_ref[...].astype(BF16))

        @pl.when(l == nl - 1)
        def _():
            if gs == 1:
                o_ref[...] = acc[...].astype(out_dtype)
            elif kind == "cols":
                for t in range(gs):
                    o_ref[t] = acc[:, t * n_sh:(t + 1) * n_sh].astype(out_dtype)
            else:
                for t in range(gs):
                    o_ref[t] = acc[t * r_sh:(t + 1) * r_sh, :].astype(out_dtype)

    in_specs = [pl.BlockSpec((bl, bm), lambda i, j, l: (l, i)),
                pl.BlockSpec((bl, bn), lambda i, j, l, p=b_off // bl: (l + p, j))]
    args = [a, b]
    alias = {}
    if into is not None:
        in_specs.append(ANY)
        args.append(into)
        alias = {2: 0}
    return pl.pallas_call(
        body, name=name, grid=(K // bm, N // bn, nl), in_specs=in_specs,
        out_specs=pl.BlockSpec(oblock, oidx), out_shape=jax.ShapeDtypeStruct(oshape, out_dtype),
        scratch_shapes=[pltpu.VMEM((bm, bn), F32)], input_output_aliases=alias,
        compiler_params=_cp("parallel", "parallel", "arbitrary"),
    )(*args)


def rms_fwd(x, g, name):
    def body(x_ref, g_ref, ob_ref):
        _, xhat = _rms_parts(x_ref[...])
        ob_ref[...] = (xhat * g_ref[...]).astype(BF16)

    return pl.pallas_call(body, name=name, out_shape=jax.ShapeDtypeStruct(x.shape, BF16))(x, g)


def rms_dgain(x, dy0, dy1, name):
    def body(x_ref, d0_ref, d1_ref, o_ref):
        _, xhat = _rms_parts(x_ref[...])
        o_ref[...] = jnp.sum((d0_ref[...] + d1_ref[...]) * xhat, axis=0, keepdims=True)

    return pl.pallas_call(body, name=name, out_shape=jax.ShapeDtypeStruct((1, x.shape[1]), F32))(x, dy0, dy1)


def _s5_discretise(lr, li, logdt, bt_re, bt_im):
    dt = jnp.exp(logdt)
    mag = jnp.exp(lr * dt)
    ab_re = mag * jnp.cos(li * dt)
    ab_im = mag * jnp.sin(li * dt)
    den = lr * lr + li * li
    nr = ab_re - 1.0
    coef_re = (nr * lr + ab_im * li) / den
    coef_im = (ab_im * lr - nr * li) / den
    cr = coef_re[:, None, :]
    ci = coef_im[:, None, :]
    bb_re = cr * bt_re - ci * bt_im
    bb_im = cr * bt_im + ci * bt_re
    return ab_re, ab_im, bb_re, bb_im


def s5_param_fwd(lr, li, logdt, bt_re, bt_im):
    def body(lr_ref, li_ref, ld_ref, br_ref, bi_ref, bbr_ref, bbi_ref):
        _, _, bb_re, bb_im = _s5_discretise(lr_ref[...], li_ref[...], ld_ref[...], br_ref[...], bi_ref[...])
        bbr_ref[...] = bb_re
        bbi_ref[...] = bb_im

    sh = jax.ShapeDtypeStruct(bt_re.shape, F32)
    return pl.pallas_call(body, name="s5_param_fwd", out_shape=[sh, sh])(lr, li, logdt, bt_re, bt_im)


def s5_param_bwd(lr, li, logdt, bt_re, bt_im, d_ab_re, d_ab_im, d_bb_re, d_bb_im):
    def body(lr_ref, li_ref, ld_ref, br_ref, bi_ref, dar_ref, dai_ref, dbr_ref, dbi_ref,
             o_lr, o_li, o_ld, o_br, o_bi):
        _, vjp = jax.vjp(_s5_discretise, lr_ref[...], li_ref[...], ld_ref[...], br_ref[...], bi_ref[...])
        g = vjp((dar_ref[...], dai_ref[...], dbr_ref[...], dbi_ref[...]))
        for o, v in zip((o_lr, o_li, o_ld), g[:3]):
            o[...] = v
        for o, v in zip((o_br, o_bi), g[3:]):
            for c in range(S5_GROUP):
                o[:, c * S5_STATE:(c + 1) * S5_STATE] = v[:, c, :]

    dense = jax.ShapeDtypeStruct((S5_GROUPS, S5_GROUP * S5_STATE), F32)
    shapes = [jax.ShapeDtypeStruct(a.shape, F32) for a in (lr, li, logdt)] + [dense, dense]
    return pl.pallas_call(body, name="s5_param_bwd", out_shape=shapes)(
        lr, li, logdt, bt_re, bt_im, d_ab_re, d_ab_im, d_bb_re, d_bb_im)


def s5_tables(lr_flat, li_flat, logdt_flat):
    def body(lr_ref, li_ref, ld_ref, tab_ref):
        dt = jnp.exp(ld_ref[...])
        a = lr_ref[...] * dt
        th = li_ref[...] * dt
        row = lax.broadcasted_iota(jnp.int32, (8, 1), 0)
        rowf = row.astype(F32)

        def power(e, sign):
            m = jnp.exp(e * a)
            return m * jnp.cos(e * th), sign * m * jnp.sin(e * th)

        k = 0
        for sign, fwd in ((1.0, True), (-1.0, False)):
            for s in (1, 2, 4):
                pr, pi = power(jnp.full((8, 1), float(s), F32), sign)
                keep = (row >= s) if fwd else (row + s < 8)
                tab_ref[k] = jnp.where(keep, pr, 0.0)
                tab_ref[k + 1] = jnp.where(keep, pi, 0.0)
                k += 2
            e = rowf + 1.0 if fwd else 8.0 - rowf
            pr, pi = power(e, sign)
            tab_ref[k] = pr
            tab_ref[k + 1] = pi
            k += 2

    return pl.pallas_call(body, name="s5_tables",
                          out_shape=jax.ShapeDtypeStruct((16, 8, S5_COLS), F32))(lr_flat, li_flat, logdt_flat)


def _scan_block(a, b, tabs, base, cr, ci, reverse):
    for n, s in enumerate((1, 2, 4)):
        mr = tabs[base + 2 * n]
        mi = tabs[base + 2 * n + 1]
        sh = (8 - s) if reverse else s
        ar = pltpu.roll(a, sh, 0)
        br = pltpu.roll(b, sh, 0)
        a, b = a + mr * ar - mi * br, b + mr * br + mi * ar
    pr = tabs[base + 6]
    pi = tabs[base + 7]
    a, b = a + pr * cr - pi * ci, b + pr * ci + pi * cr
    return a, b


class Carried:
    def __init__(self, arrays, out_shapes, sems, start, middle, finish):
        self.arrays, self.out_shapes, self.sems = list(arrays), list(out_shapes), list(sems)
        self.start, self.middle, self.finish = start, middle, finish

    def split(self, refs, n_in, n_out, n_scratch):
        a, o, s = len(self.arrays), len(self.out_shapes), len(self.sems)
        own_in, car_in = refs[:n_in], refs[n_in:n_in + a]
        own_out, car_out = refs[n_in + a:n_in + a + n_out], refs[n_in + a + n_out:n_in + a + n_out + o]
        rest = refs[n_in + a + n_out + o:]
        return own_in + own_out + rest[:n_scratch], (car_in, car_out, rest[n_scratch:n_scratch + s])

    def hooks(self, parts, n_chunks, nt):
        if n_chunks is None:
            t = pl.program_id(0)
            first, last, mid = t == 0, t == nt - 1, None
        else:
            j, t = pl.program_id(0), pl.program_id(1)
            first, last = (j == 0) & (t == 0), (j == n_chunks - 1) & (t == nt - 1)
            mid = (j == n_chunks // 2) & (t == 0)

        def top():
            pl.when(first)(lambda: self.start(*parts))
            if self.middle is not None and mid is not None:
                pl.when(mid)(lambda: self.middle(*parts))

        def end():
            pl.when(last)(lambda: self.finish(*parts))

        return top, end


def s5_fwd(z, bbd_re, bbd_im, ccd_re, ccd_im, tab, dskip, tm=S5_TILE, carried=None):
    L = z.shape[0]
    tm = min(tm, L)
    nt = L // tm

    def body(*refs):
        top = end = None
        if carried is not None:
            refs, parts = carried.split(refs, 7, 4, 3)
            top, end = carried.hooks(parts, S5_SPLIT, nt)
            top()
        u_ref, bbr_ref, bbi_ref, ccr_ref, cci_ref, tab_ref, d_ref, y_ref, ck_ref, hr_ref, hi_ref, xr, xi, car = refs
        t = pl.program_id(1)

        @pl.when(t == 0)
        def _():
            car[...] = jnp.zeros_like(car)

        u = u_ref[...]
        ub = u.astype(BF16)
        xr[...] = _dot(ub, bbr_ref[...])
        xi[...] = _dot(ub, bbi_ref[...])
        tabs = [tab_ref[k] for k in range(8)]

        def blk(i, c):
            r0 = pl.multiple_of(i * 8, 8)
            a, b = _scan_block(xr[pl.ds(r0, 8), :], xi[pl.ds(r0, 8), :], tabs, 0, c[0], c[1], False)
            xr[pl.ds(r0, 8), :] = a
            xi[pl.ds(r0, 8), :] = b
            return a[7:8, :], b[7:8, :]

        cr, ci = lax.fori_loop(0, tm // 8, blk, (car[0:1, :], car[1:2, :]))
        car[0:1, :] = cr
        car[1:2, :] = ci
        ck_ref[0:1, :] = cr
        ck_ref[1:2, :] = ci
        hrb = xr[...].astype(BF16)
        hib = xi[...].astype(BF16)
        hr_ref[...] = hrb
        hi_ref[...] = hib
        y_ref[...] = _dot(hrb, ccr_ref[...]) - _dot(hib, cci_ref[...]) + d_ref[...] * u
        if end is not None:
            end()

    extra = carried.arrays if carried is not None else []
    extra_out = carried.out_shapes if carried is not None else []
    extra_sems = carried.sems if carried is not None else []
    return pl.pallas_call(
        body, name="s5_fwd", grid=(S5_SPLIT, nt),
        in_specs=[pl.BlockSpec((tm, S5_UC), lambda j, t: (t, j)),
                  pl.BlockSpec((None, S5_UC, S5_CC), lambda j, t: (j, 0, 0)),
                  pl.BlockSpec((None, S5_UC, S5_CC), lambda j, t: (j, 0, 0)),
                  pl.BlockSpec((None, S5_CC, S5_UC), lambda j, t: (j, 0, 0)),
                  pl.BlockSpec((None, S5_CC, S5_UC), lambda j, t: (j, 0, 0)),
                  pl.BlockSpec((8, 8, S5_CC), lambda j, t: (0, 0, j)),
                  pl.BlockSpec((1, S5_UC), lambda j, t: (0, j))] + [ANY] * len(extra),
        out_specs=[pl.BlockSpec((tm, S5_UC), lambda j, t: (t, j)),
                   pl.BlockSpec((None, 2, S5_CC), lambda j, t: (t, 0, j)),
                   pl.BlockSpec((tm, S5_CC), lambda j, t: (t, j)),
                   pl.BlockSpec((tm, S5_CC), lambda j, t: (t, j))] + [ANY] * len(extra_out),
        out_shape=[jax.ShapeDtypeStruct((L, S5_WIDTH), F32), jax.ShapeDtypeStruct((nt, 2, S5_COLS), F32),
                   jax.ShapeDtypeStruct((L, S5_COLS), BF16), jax.ShapeDtypeStruct((L, S5_COLS), BF16)] + extra_out,
        scratch_shapes=[pltpu.VMEM((tm, S5_CC), F32), pltpu.VMEM((tm, S5_CC), F32), pltpu.VMEM((2, S5_CC), F32)]
        + extra_sems,
        compiler_params=_cp("arbitrary" if carried is not None else "parallel", "arbitrary"),
    )(z, bbd_re, bbd_im, ccd_re, ccd_im, tab, dskip, *extra)


def s5_bwd(z, dy, dz, ckpt, hrb, hib, bbd_re, bbd_im, ccd_re, ccd_im, tab, dskip, tm=S5_TILE, carried=None):
    L = z.shape[0]
    tm = min(tm, L)
    nt = L // tm

    def body(*refs):
        top = end = None
        if carried is not None:
            refs, parts = carried.split(refs, 12, 7, 7)
            top, end = carried.hooks(parts, S5_SPLIT, nt)
            top()
        (u_ref, dy_ref, dz_ref, ck_ref, hrb_ref, hib_ref, bbr_ref, bbi_ref, ccr_ref, cci_ref, tab_ref, d_ref,
         du_ref, da_ref, dbr_ref, dbi_ref, dcr_ref, dci_ref, dd_ref, hr, hi, gr, gi, car, acr, aci) = refs
        t = pl.program_id(1)
        tt = nt - 1 - t

        @pl.when(t == 0)
        def _():
            for r in (car, acr, aci, dbr_ref, dbi_ref, dcr_ref, dci_ref, dd_ref):
                r[...] = jnp.zeros_like(r)

        u = u_ref[...]
        ub = u.astype(BF16)
        dyv = dy_ref[...]
        dyb = dyv.astype(BF16)
        tabs = [None] * 8 + [tab_ref[k] for k in range(8, 16)]

        live = (tt > 0).astype(F32)
        hr[0:8, :] = jnp.broadcast_to(ck_ref[0:1, :] * live, (8, S5_CC))
        hi[0:8, :] = jnp.broadcast_to(ck_ref[1:2, :] * live, (8, S5_CC))
        hrb = hrb_ref[...]
        hib = hib_ref[...]
        hr[8:, :] = hrb.astype(F32)
        hi[8:, :] = hib.astype(F32)
        dcr_ref[...] += _dot_tn(hrb, dyb)
        dci_ref[...] -= _dot_tn(hib, dyb)

        gr[...] = _dot_nt(dyb, ccr_ref[...])
        gi[...] = -_dot_nt(dyb, cci_ref[...])
        row0 = lax.broadcasted_iota(jnp.int32, (8, S5_CC), 0) == 0

        def rblk(k, c):
            i = tm // 8 - 1 - k
            r0 = pl.multiple_of(i * 8, 8)
            a, b = _scan_block(gr[pl.ds(r0, 8), :], gi[pl.ds(r0, 8), :], tabs, 8, c[0], c[1], True)
            gr[pl.ds(r0, 8), :] = a
            gi[pl.ds(r0, 8), :] = b
            r1 = pl.multiple_of(i * 8 + 8, 8)
            hpr = jnp.where(row0, pltpu.roll(hr[pl.ds(r0, 8), :], 1, 0), pltpu.roll(hr[pl.ds(r1, 8), :], 1, 0))
            hpi = jnp.where(row0, pltpu.roll(hi[pl.ds(r0, 8), :], 1, 0), pltpu.roll(hi[pl.ds(r1, 8), :], 1, 0))
            acr[...] += a * hpr + b * hpi
            aci[...] += b * hpr - a * hpi
            return a[0:1, :], b[0:1, :]

        cr, ci = lax.fori_loop(0, tm // 8, rblk, (car[0:1, :], car[1:2, :]))
        car[0:1, :] = cr
        car[1:2, :] = ci

        grb = gr[...].astype(BF16)
        gib = gi[...].astype(BF16)
        du_ref[...] = (_dot_nt(grb, bbr_ref[...]) + _dot_nt(gib, bbi_ref[...]) + d_ref[...] * dyv).astype(BF16)
        dbr_ref[...] += _dot_tn(ub, grb)
        dbi_ref[...] += _dot_tn(ub, gib)
        dd_ref[...] += jnp.sum(dyv * u, axis=0, keepdims=True)

        @pl.when(t == nt - 1)
        def _():
            da_ref[0:1, :] = jnp.sum(acr[...], axis=0, keepdims=True)
            da_ref[1:2, :] = jnp.sum(aci[...], axis=0, keepdims=True)

        if end is not None:
            end()

    extra = carried.arrays if carried is not None else []
    extra_out = carried.out_shapes if carried is not None else []
    extra_sems = carried.sems if carried is not None else []
    chunk = lambda a, b: pl.BlockSpec((None, a, b), lambda j, t: (j, 0, 0))
    return pl.pallas_call(
        body, name="s5_bwd", grid=(S5_SPLIT, nt),
        in_specs=[pl.BlockSpec((tm, S5_UC), lambda j, t: (nt - 1 - t, j)),
                  pl.BlockSpec((tm, S5_UC), lambda j, t: (nt - 1 - t, j)),
                  ANY,
                  pl.BlockSpec((None, 2, S5_CC), lambda j, t: (jnp.maximum(nt - 2 - t, 0), 0, j)),
                  pl.BlockSpec((tm, S5_CC), lambda j, t: (nt - 1 - t, j)),
                  pl.BlockSpec((tm, S5_CC), lambda j, t: (nt - 1 - t, j)),
                  chunk(S5_UC, S5_CC), chunk(S5_UC, S5_CC), chunk(S5_CC, S5_UC), chunk(S5_CC, S5_UC),
                  pl.BlockSpec((16, 8, S5_CC), lambda j, t: (0, 0, j)),
                  pl.BlockSpec((1, S5_UC), lambda j, t: (0, j))] + [ANY] * len(extra),
        out_specs=[pl.BlockSpec((tm, S5_UC), lambda j, t: (nt - 1 - t, j)),
                   pl.BlockSpec((None, 2, S5_CC), lambda j, t: (j, 0, 0)),
                   chunk(S5_UC, S5_CC), chunk(S5_UC, S5_CC), chunk(S5_CC, S5_UC), chunk(S5_CC, S5_UC),
                   pl.BlockSpec((1, S5_UC), lambda j, t: (0, j))] + [ANY] * len(extra_out),
        out_shape=[jax.ShapeDtypeStruct(dz.shape, dz.dtype),
                   jax.ShapeDtypeStruct((S5_SPLIT, 2, S5_CC), F32),
                   jax.ShapeDtypeStruct((S5_SPLIT, S5_UC, S5_CC), F32),
                   jax.ShapeDtypeStruct((S5_SPLIT, S5_UC, S5_CC), F32),
                   jax.ShapeDtypeStruct((S5_SPLIT, S5_CC, S5_UC), F32),
                   jax.ShapeDtypeStruct((S5_SPLIT, S5_CC, S5_UC), F32),
                   jax.ShapeDtypeStruct((1, S5_WIDTH), F32)] + extra_out,
        scratch_shapes=[pltpu.VMEM((tm + 8, S5_CC), F32), pltpu.VMEM((tm + 8, S5_CC), F32),
                        pltpu.VMEM((tm, S5_CC), F32), pltpu.VMEM((tm, S5_CC), F32),
                        pltpu.VMEM((2, S5_CC), F32), pltpu.VMEM((8, S5_CC), F32), pltpu.VMEM((8, S5_CC), F32)]
        + extra_sems,
        input_output_aliases={2: 0},
        compiler_params=_cp("arbitrary" if carried is not None else "parallel", "arbitrary"),
    )(z, dy, dz, ckpt, hrb, hib, bbd_re, bbd_im, ccd_re, ccd_im, tab, dskip, *extra)


_EYE8 = np.eye(S5_GROUPS // S5_SPLIT, dtype=np.float32)


def _blockdiag(a):
    g, r, c = a.shape
    a = a.reshape(S5_SPLIT, g // S5_SPLIT, r, c)
    out = a[:, :, :, None, :] * _EYE8[None, :, None, :, None].astype(a.dtype)
    return out.reshape(S5_SPLIT, (g // S5_SPLIT) * r, (g // S5_SPLIT) * c)


def _blockdiag_extract(a, r, c):
    n = S5_GROUPS // S5_SPLIT
    a = a.reshape(S5_SPLIT, n, r, n, c)
    d = jnp.stack([a[:, k, :, k, :] for k in range(n)], axis=1)
    return d.reshape(S5_GROUPS, r, c)


def s5_mixer_core_fwd(z, lam_re, lam_im, log_dt, b_re, b_im, c_re, c_im, d_skip, carried=None):
    bt_re = jnp.swapaxes(b_re, 1, 2)
    bt_im = jnp.swapaxes(b_im, 1, 2)
    logdt = log_dt.reshape(S5_GROUPS, 1)
    bb_re, bb_im = s5_param_fwd(lam_re, lam_im, logdt, bt_re, bt_im)
    flat = lambda a: a.reshape(1, S5_COLS)
    tab = s5_tables(flat(lam_re), flat(lam_im), flat(jnp.broadcast_to(logdt, (S5_GROUPS, S5_STATE))))
    bbd_re = _blockdiag(bb_re).astype(BF16)
    bbd_im = _blockdiag(bb_im).astype(BF16)
    ccd_re = _blockdiag(jnp.swapaxes(c_re, 1, 2)).astype(BF16)
    ccd_im = _blockdiag(jnp.swapaxes(c_im, 1, 2)).astype(BF16)
    dsk = d_skip.reshape(1, S5_WIDTH)
    y, ckpt, hrb, hib, *landed = s5_fwd(z, bbd_re, bbd_im, ccd_re, ccd_im, tab, dsk, carried=carried)
    saved = (logdt, bt_re, bt_im, bbd_re, bbd_im, ccd_re, ccd_im, tab, dsk, ckpt, hrb, hib)
    return y, saved, landed


def s5_b_from_dense(dense):
    return jnp.swapaxes(dense.reshape(S5_GROUPS, S5_GROUP, S5_STATE), 1, 2)


def s5_mixer_core_bwd(z, dy, dz, lam_re, lam_im, saved, carried=None):
    logdt, bt_re, bt_im, bbd_re, bbd_im, ccd_re, ccd_im, tab, dsk, ckpt, hrb, hib = saved
    dz, da, dbr, dbi, dcr, dci, dd, *landed = s5_bwd(z, dy, dz, ckpt, hrb, hib, bbd_re, bbd_im, ccd_re, ccd_im, tab,
                                                     dsk, carried=carried)
    d_ab_re = da[:, 0, :].reshape(S5_GROUPS, S5_STATE)
    d_ab_im = da[:, 1, :].reshape(S5_GROUPS, S5_STATE)
    d_bb_re = _blockdiag_extract(dbr, S5_GROUP, S5_STATE)
    d_bb_im = _blockdiag_extract(dbi, S5_GROUP, S5_STATE)
    g_lr, g_li, g_ld, g_btr, g_bti = s5_param_bwd(lam_re, lam_im, logdt, bt_re, bt_im,
                                                  d_ab_re, d_ab_im, d_bb_re, d_bb_im)
    g_cre = jnp.swapaxes(_blockdiag_extract(dcr, S5_STATE, S5_GROUP), 1, 2)
    g_cim = jnp.swapaxes(_blockdiag_extract(dci, S5_STATE, S5_GROUP), 1, 2)
    grads = dict(lambda_re=g_lr, lambda_im=g_li, log_dt=g_ld.reshape(S5_GROUPS), b_re=g_btr, b_im=g_bti,
                 c_re=g_cre, c_im=g_cim, d=dd.reshape(S5_WIDTH))
    return dz, grads, landed


Z_U, Z_GA, Z_VAL, Z_GLU, Z_GB = range(5)
SUBLANES = 8


def _shifted_copies(buf, tm):
    n = tm + CONV_HALO - SUBLANES
    for r in range(1, SUBLANES):
        buf[r, 0:n, :] = buf[0, pl.ds(r, n), :]


CONV_ROWS = 32


def _shifted_rows(buf, start, rows, base=0):
    return buf[start % SUBLANES, pl.ds(base + (start - start % SUBLANES), rows), :]


def conv_fwd(z, conv_w, conv_b, tm=ROW_TILE):
    L = z.shape[0]
    tm = min(tm, L)
    nt = L // tm
    hb = tm // CONV_HALO
    C = CONV_WIDTH

    def body(val_ref, glu_ref, valh_ref, gluh_ref, w_ref, b_ref, c_ref, vsh):
        live = (pl.program_id(0) > 0).astype(F32)
        vsh[0, 0:CONV_HALO, :] = valh_ref[...] * _sigmoid(gluh_ref[...]) * live
        vsh[0, CONV_HALO:, :] = val_ref[...] * _sigmoid(glu_ref[...])
        _shifted_copies(vsh, tm)

        def rows(i, carry):
            base = pl.multiple_of(i * CONV_ROWS, CONV_ROWS)
            acc = jnp.broadcast_to(b_ref[...], (CONV_ROWS, C))
            for k in range(CONV_KERNEL):
                acc = acc + w_ref[k:k + 1, :] * _shifted_rows(vsh, CONV_HALO - CONV_KERNEL + 1 + k, CONV_ROWS, base)
            c_ref[pl.ds(base, CONV_ROWS), :] = acc
            return carry

        lax.fori_loop(0, tm // CONV_ROWS, rows, 0)

    cur = lambda col: pl.BlockSpec((tm, C), lambda t: (t, col))
    prev = lambda col: pl.BlockSpec((CONV_HALO, C), lambda t: (jnp.maximum(t * hb - 1, 0), col))
    return pl.pallas_call(
        body, name="conv_fwd", grid=(nt,),
        in_specs=[cur(Z_VAL), cur(Z_GLU), prev(Z_VAL), prev(Z_GLU), _full(conv_w.shape), _full(conv_b.shape)],
        out_specs=pl.BlockSpec((tm, C), lambda t: (t, 0)),
        out_shape=jax.ShapeDtypeStruct((L, C), F32),
        scratch_shapes=[pltpu.VMEM((8, tm + CONV_HALO, C), F32)],
        compiler_params=_cp("parallel"),
    )(z, z, z, z, conv_w, conv_b)


def conv_bwd(z, dc, dz, conv_w, tm=ROW_TILE, carried=None):
    L = z.shape[0]
    tm = min(tm, L)
    nt = L // tm
    hb = tm // CONV_HALO
    nh = L // CONV_HALO
    C = CONV_WIDTH
    off = CONV_HALO - CONV_KERNEL + 1

    def body(*refs):
        top = end = None
        if carried is not None:
            refs, parts = carried.split(refs, 8, 3, 3)
            top, end = carried.hooks(parts, None, nt)
            top()
        val_ref, glu_ref, valh_ref, gluh_ref, dc_ref, dcn_ref, dz_ref, w_ref, dvg_ref, dw_ref, db_ref, vsh, dsh, wacc = refs
        t = pl.program_id(0)

        @pl.when(t == 0)
        def _():
            wacc[...] = jnp.zeros_like(wacc)
            db_ref[...] = jnp.zeros_like(db_ref)

        val = val_ref[...]
        sg = _sigmoid(glu_ref[...])
        vsh[0, 0:CONV_HALO, :] = valh_ref[...] * _sigmoid(gluh_ref[...]) * (t > 0).astype(F32)
        vsh[0, CONV_HALO:, :] = val * sg
        dcv = dc_ref[...]
        dsh[0, 0:tm, :] = dcv
        dsh[0, tm:, :] = dcn_ref[...] * (t < nt - 1).astype(F32)
        _shifted_copies(vsh, tm)
        _shifted_copies(dsh, tm)

        def rows(i, carry):
            base = pl.multiple_of(i * CONV_ROWS, CONV_ROWS)
            dcr = dc_ref[pl.ds(base, CONV_ROWS), :]
            dv = jnp.zeros((CONV_ROWS, C), F32)
            for k in range(CONV_KERNEL):
                dv = dv + w_ref[k:k + 1, :] * _shifted_rows(dsh, CONV_KERNEL - 1 - k, CONV_ROWS, base)
                prod = dcr * _shifted_rows(vsh, off + k, CONV_ROWS, base)
                wacc[k] += jnp.sum(prod.reshape(CONV_ROWS // SUBLANES, SUBLANES, C), axis=0)
            valr = val_ref[pl.ds(base, CONV_ROWS), :]
            sgr = _sigmoid(glu_ref[pl.ds(base, CONV_ROWS), :])
            dvg_ref[pl.ds(base, CONV_ROWS), 0:C] = (dv * sgr).astype(BF16)
            dvg_ref[pl.ds(base, CONV_ROWS), C:] = (dv * valr * sgr * (1.0 - sgr)).astype(BF16)
            return carry

        lax.fori_loop(0, tm // CONV_ROWS, rows, 0)
        db_ref[...] += jnp.sum(dcv, axis=0, keepdims=True)

        @pl.when(t == nt - 1)
        def _():
            dw_ref[...] = jnp.sum(wacc[...], axis=1)

        if end is not None:
            end()

    extra = carried.arrays if carried is not None else []
    extra_out = carried.out_shapes if carried is not None else []
    extra_sems = carried.sems if carried is not None else []
    cur = lambda col: pl.BlockSpec((tm, C), lambda t: (t, col))
    prev = lambda col: pl.BlockSpec((CONV_HALO, C), lambda t: (jnp.maximum(t * hb - 1, 0), col))
    nxt = pl.BlockSpec((CONV_HALO, C), lambda t: (jnp.minimum((t + 1) * hb, nh - 1), 0))
    row = pl.BlockSpec((tm, C), lambda t: (t, 0))
    return pl.pallas_call(
        body, name="conv_bwd", grid=(nt,),
        in_specs=[cur(Z_VAL), cur(Z_GLU), prev(Z_VAL), prev(Z_GLU), row, nxt, ANY, _full(conv_w.shape)]
        + [ANY] * len(extra),
        out_specs=[pl.BlockSpec((tm, 2 * C), lambda t: (t, 1)), _full((CONV_HALO, C)), _full((1, C))]
        + [ANY] * len(extra_out),
        out_shape=[jax.ShapeDtypeStruct(dz.shape, dz.dtype),
                   jax.ShapeDtypeStruct((CONV_HALO, C), F32), jax.ShapeDtypeStruct((1, C), F32)] + extra_out,
        scratch_shapes=[pltpu.VMEM((8, tm + CONV_HALO, C), F32), pltpu.VMEM((8, tm + CONV_HALO, C), F32),
                        pltpu.VMEM((CONV_HALO, SUBLANES, C), F32)] + extra_sems,
        input_output_aliases={6: 0},
        compiler_params=_cp("arbitrary"),
    )(z, z, z, z, dc, dc, dz, conv_w, *extra)


def _ln_parts(c):
    mu = jnp.mean(c, axis=-1, keepdims=True)
    cc = c - mu
    rstd = lax.rsqrt(jnp.mean(cc * cc, axis=-1, keepdims=True) + EPS)
    return rstd, cc * rstd


def _ev_tail_branches(ys, c, wglu, bglu, lng, lnb):
    z1 = _gelu(ys)
    z1b = z1.astype(BF16)
    sg = _sigmoid(_dot_rows(z1b, wglu) + bglu)
    out = z1 * sg
    rstd, chat = _ln_parts(c)
    cn = chat * lng + lnb
    return z1, z1b, sg, out, rstd, chat, cn


def ev_tail_fwd(ys, z, c, x0, wglu, bglu, lng, lnb, wout, tm=ROW_TILE):
    L, D = x0.shape
    tm = min(tm, L)
    W = S5_WIDTH

    def body(ys_ref, ga_ref, c_ref, gb_ref, x_ref, wglu_ref, bglu_ref, lng_ref, lnb_ref, wout_ref, o_ref):
        _, _, _, out, _, _, cn = _ev_tail_branches(ys_ref[...], c_ref[...], wglu_ref, bglu_ref[...],
                                                   lng_ref[...], lnb_ref[...])
        ya = (out * _silu(ga_ref[...])).astype(BF16)
        yb = (_silu(cn) * _silu(gb_ref[...])).astype(BF16)
        o_ref[...] = x_ref[...] + _dot_rows(jnp.concatenate([ya, yb], axis=1), wout_ref)

    row = lambda n, col=0: pl.BlockSpec((tm, n), lambda t: (t, col))
    return pl.pallas_call(
        body, name="ev_tail_fwd", grid=(L // tm,),
        in_specs=[row(W), row(W, Z_GA), row(W), row(W, Z_GB), row(D), _full(wglu.shape), _full(bglu.shape),
                  _full(lng.shape), _full(lnb.shape), _full(wout.shape)],
        out_specs=row(D), out_shape=jax.ShapeDtypeStruct((L, D), F32), compiler_params=_cp("parallel"),
    )(ys, z, c, z, x0, wglu, bglu, lng, lnb, wout)


def ev_tail_bwd(ys, z, c, dx1, wglu, bglu, lng, lnb, wout, tm=ROW_TILE):
    L, D = dx1.shape
    tm = min(tm, L)
    W = S5_WIDTH

    def body(ys_ref, ga_ref, c_ref, gb_ref, dx_ref, wglu_ref, bglu_ref, lng_ref, lnb_ref, wout_ref,
             dys_ref, dc_ref, dz_ref, r_ref, z1_ref, dt_ref, dbg_ref, dlg_ref, dlb_ref):
        @pl.when(pl.program_id(0) == 0)
        def _():
            for r in (dbg_ref, dlg_ref, dlb_ref):
                r[...] = jnp.zeros_like(r)

        ys, ga, gb = ys_ref[...], ga_ref[...], gb_ref[...]
        z1, z1b, sg, out, rstd, chat, cn = _ev_tail_branches(ys, c_ref[...], wglu_ref, bglu_ref[...],
                                                             lng_ref[...], lnb_ref[...])
        (sga, dsga), (sgb, dsgb), (scn, dscn) = _silu_pair(ga), _silu_pair(gb), _silu_pair(cn)
        r_ref[:, 0:W] = (out * sga).astype(BF16)
        r_ref[:, W:] = (scn * sgb).astype(BF16)
        dr = _dot_nt_rows(dx_ref[...].astype(BF16), wout_ref)
        dra, drb = dr[:, 0:W], dr[:, W:]
        dz_ref[...] = jnp.zeros_like(dz_ref)
        dz_ref[:, Z_GA * W:(Z_GA + 1) * W] = (dra * out * dsga).astype(BF16)
        dout = dra * sga
        dt = dout * z1 * sg * (1.0 - sg)
        dtb = dt.astype(BF16)
        dz1 = dout * sg + _dot_nt_rows(dtb, wglu_ref)
        dys_ref[...] = dz1 * _dgelu(ys)
        z1_ref[...] = z1b
        dt_ref[...] = dtb
        dbg_ref[...] += jnp.sum(dt, axis=0, keepdims=True)
        dz_ref[:, Z_GB * W:(Z_GB + 1) * W] = (drb * scn * dsgb).astype(BF16)
        dcn = drb * sgb * dscn
        dlg_ref[...] += jnp.sum(dcn * chat, axis=0, keepdims=True)
        dlb_ref[...] += jnp.sum(dcn, axis=0, keepdims=True)
        dch = dcn * lng_ref[...]
        dc_ref[...] = rstd * (dch - jnp.mean(dch, axis=-1, keepdims=True)
                              - chat * jnp.mean(dch * chat, axis=-1, keepdims=True))

    row = lambda n, col=0: pl.BlockSpec((tm, n), lambda t: (t, col))
    f = lambda n, dt: jax.ShapeDtypeStruct((L, n), dt)
    vec = jax.ShapeDtypeStruct((1, W), F32)
    return pl.pallas_call(
        body, name="ev_tail_bwd", grid=(L // tm,),
        in_specs=[row(W), row(W, Z_GA), row(W), row(W, Z_GB), row(D), _full(wglu.shape), _full(bglu.shape),
                  _full(lng.shape), _full(lnb.shape), _full(wout.shape)],
        out_specs=[row(W), row(W), row(EVEN_IN), row(D), row(W), row(W), _full((1, W)), _full((1, W)), _full((1, W))],
        out_shape=[f(W, F32), f(W, F32), f(EVEN_IN, BF16), f(D, BF16), f(W, BF16), f(W, BF16), vec, vec, vec],
        compiler_params=_cp("arbitrary"),
    )(ys, z, c, z, dx1, wglu, bglu, lng, lnb, wout)


XA_SCALE = XA_HEAD_DIM ** -0.5


def _xa_forward(xv, g, wqg, kv):
    D = D_MODEL
    _, xhat = _rms_parts(xv)
    hb = (xhat * g).astype(BF16)
    qb = _dot_cols(hb, wqg, (0, 1)).astype(BF16)
    gate = _dot_cols(hb, wqg, (2, 3))
    ps, os_ = [], []
    for h in range(XA_HEADS):
        lo, hi = h * XA_HEAD_DIM, (h + 1) * XA_HEAD_DIM
        s = _dot_nt(qb[:, lo:hi], kv[:, lo:hi]) * XA_SCALE
        e = jnp.exp(s - jnp.max(s, axis=-1, keepdims=True))
        p = e / jnp.sum(e, axis=-1, keepdims=True)
        ps.append(p)
        os_.append(_dot(p.astype(BF16), kv[:, D + lo:D + hi]))
    return hb, qb, gate, ps, jnp.concatenate(os_, axis=1)


def xa_fwd(x, g, wqg, kv, wo, layer, name, tm=MM_TILE):
    L, D = x.shape
    tm = min(tm, L)

    def body(x_ref, g_ref, wqg_ref, kv_ref, wo_ref, o_ref):
        xv = x_ref[...]
        _, _, gate, _, o = _xa_forward(xv, g_ref[...], wqg_ref, kv_ref[...])
        o_ref[...] = xv + _dot_rows((o * _silu(gate)).astype(BF16), wo_ref)

    row = pl.BlockSpec((tm, D), lambda t: (t, 0))
    return pl.pallas_call(
        body, name=name, grid=(L // tm,),
        in_specs=[row, _full(g.shape), _wspec(wqg, layer), _full(kv.shape), _wspec(wo, layer)],
        out_specs=row, out_shape=jax.ShapeDtypeStruct((L, D), F32), compiler_params=_cp("parallel"),
    )(x, g, wqg, kv, wo)


def xa_bwd(x, dxo, g, wqg, kv, wo, layer, name, tm=MM_TILE):
    L, D = x.shape
    tm = min(tm, L)

    def body(x_ref, dxo_ref, g_ref, wqg_ref, kv_ref, wo_ref, dx_ref, dqg_ref, h_ref, r_ref, dkv_ref, dg_ref):
        @pl.when(pl.program_id(0) == 0)
        def _():
            dkv_ref[...] = jnp.zeros_like(dkv_ref)
            dg_ref[...] = jnp.zeros_like(dg_ref)

        xv = x_ref[...]
        kv = kv_ref[...]
        hb, qb, gate, ps, o = _xa_forward(xv, g_ref[...], wqg_ref, kv)
        sgate, dsgate = _silu_pair(gate)
        h_ref[...] = hb
        r_ref[...] = (o * sgate).astype(BF16)
        dxo = dxo_ref[...]
        dr = _dot_nt_rows(dxo.astype(BF16), wo_ref)
        do = dr * sgate
        dqg_ref[:, D:] = (dr * o * dsgate).astype(BF16)
        dob = do.astype(BF16)
        for h in range(XA_HEADS):
            lo, hi = h * XA_HEAD_DIM, (h + 1) * XA_HEAD_DIM
            p = ps[h]
            pb = p.astype(BF16)
            dp = _dot_nt(dob[:, lo:hi], kv[:, D + lo:D + hi])
            dkv_ref[:, D + lo:D + hi] += _dot_tn(pb, dob[:, lo:hi])
            ds = p * (dp - jnp.sum(dp * p, axis=-1, keepdims=True))
            dsb = (ds * XA_SCALE).astype(BF16)
            dqg_ref[:, lo:hi] = _dot(dsb, kv[:, lo:hi]).astype(BF16)
            dkv_ref[:, lo:hi] += _dot_tn(dsb, qb[:, lo:hi])
        dh = _dot_nt_cols(_col_pieces(dqg_ref[...], D // 2), wqg_ref)
        dx, dg = _rms_bwd(xv, g_ref[...], dh)
        dx_ref[...] = dxo + dx
        dg_ref[...] += dg

    row = lambda n: pl.BlockSpec((tm, n), lambda t: (t, 0))
    return pl.pallas_call(
        body, name=name, grid=(L // tm,),
        in_specs=[row(D), row(D), _full(g.shape), _wspec(wqg, layer), _full(kv.shape), _wspec(wo, layer)],
        out_specs=[row(D), row(2 * D), row(D), row(D), _full(kv.shape), _full((1, D))],
        out_shape=[jax.ShapeDtypeStruct((L, D), F32), jax.ShapeDtypeStruct((L, 2 * D), BF16),
                   jax.ShapeDtypeStruct((L, D), BF16), jax.ShapeDtypeStruct((L, D), BF16),
                   jax.ShapeDtypeStruct(kv.shape, F32), jax.ShapeDtypeStruct((1, D), F32)],
        compiler_params=_cp("arbitrary"),
    )(x, dxo, g, wqg, kv, wo)


ATT_SCALE = ATT_HEAD_DIM ** -0.5
ATT_PAIRS = ATT_HEADS // 2
SKEW_LANES = 1024
REL_LANES = 384


def _skew(x, left):
    amt = (ATT_QB - 1) - lax.broadcasted_iota(jnp.int32, (ATT_QB, 1), 0)
    for bit in range(8):
        sh = (SKEW_LANES - (1 << bit)) if left else (1 << bit)
        x = jnp.where(((amt >> bit) & 1) == 1, pltpu.roll(x, sh, 1), x)
    return x


def _dist_onehot(shape, dist_axis):
    j = lax.broadcasted_iota(jnp.int32, shape, dist_axis)
    r = lax.broadcasted_iota(jnp.int32, shape, 1 - dist_axis)
    return (jnp.clip((ATT_WIN - 1) - j, -MAX_REL, MAX_REL) + MAX_REL == r).astype(BF16)


def _dot_exact(v, onehot):
    acc = jnp.zeros((v.shape[0], onehot.shape[1]), F32)
    rem = v
    for _ in range(3):
        part = rem.astype(BF16)
        acc = acc + _dot(part, onehot)
        rem = rem - part.astype(F32)
    return acc


ATT_EDGE = ATT_PAD // ATT_QB


def att_bias(rel_bias):
    H = rel_bias.shape[0]
    rb = jnp.pad(rel_bias, ((0, 0), (0, REL_LANES - rel_bias.shape[1]))).reshape(H, 1, REL_LANES)

    def body(rb_ref, o_ref):
        by_col = _dot_exact(jnp.broadcast_to(rb_ref[...], (8, REL_LANES)), _dist_onehot((REL_LANES, SKEW_LANES), 1))
        x = _skew(jnp.broadcast_to(by_col[0:1, :], (ATT_QB, SKEW_LANES)), left=True)[:, 0:ATT_WIN]
        qc = lax.broadcasted_iota(jnp.int32, (ATT_QB, 1), 0) // CHUNK + LEFT_CHUNKS
        col = lax.broadcasted_iota(jnp.int32, (1, ATT_WIN), 1)
        dc = qc - col // CHUNK
        band = (dc >= 0) & (dc <= LEFT_CHUNKS)
        for blk in range(ATT_EDGE + 1):
            o_ref[blk] = jnp.where(band & (col >= ATT_PAD - blk * ATT_QB), x, NEG)

    return pl.pallas_call(
        body, name="att_bias", grid=(H,),
        in_specs=[pl.BlockSpec((None, 1, REL_LANES), lambda h: (h, 0, 0))],
        out_specs=pl.BlockSpec((ATT_EDGE + 1, None, ATT_QB, ATT_WIN), lambda h: (0, h, 0, 0)),
        out_shape=jax.ShapeDtypeStruct((ATT_EDGE + 1, H, ATT_QB, ATT_WIN), F32), compiler_params=_cp("parallel"),
    )(rb)


def relbias_bwd(dbias):
    H = dbias.shape[0]

    def body(x_ref, o_ref):
        x = jnp.concatenate([x_ref[...], jnp.zeros((ATT_QB, SKEW_LANES - ATT_WIN), F32)], axis=1)
        col = jnp.sum(_skew(x, left=False), axis=0, keepdims=True)
        o_ref[...] = _dot_exact(jnp.broadcast_to(col, (8, SKEW_LANES)), _dist_onehot((SKEW_LANES, REL_LANES), 0))

    out = pl.pallas_call(
        body, name="relbias_bwd", grid=(H,),
        in_specs=[pl.BlockSpec((None, ATT_QB, ATT_WIN), lambda h: (h, 0, 0))],
        out_specs=pl.BlockSpec((None, 8, REL_LANES), lambda h: (h, 0, 0)),
        out_shape=jax.ShapeDtypeStruct((H, 8, REL_LANES), F32), compiler_params=_cp("parallel"),
    )(dbias)
    return out[:, 0, :2 * MAX_REL + 1]


def _ca_scores(qh, kw, bias):
    s = _dot_nt(qh, kw) + bias
    e = jnp.exp(s - jnp.max(s, axis=-1, keepdims=True))
    return e, 1.0 / jnp.sum(e, axis=-1, keepdims=True)


def _ca_head(qv, m):
    return jnp.where(m, qv, jnp.zeros_like(qv)) * ATT_SCALE


def _ca_bias_spec():
    return pl.BlockSpec((None, 2, ATT_QB, ATT_WIN), lambda hp, b: (jnp.minimum(b, ATT_EDGE), hp, 0, 0))


def ca_fwd(q, kvp, gate, bias):
    L, D = q.shape
    Lp = kvp.shape[0]
    nb = L // ATT_QB

    def body(q_ref, k_ref, v_ref, g_ref, b_ref, r_ref, o_ref):
        w = pl.multiple_of(pl.program_id(1) * ATT_QB, ATT_QB)
        kw = k_ref[pl.ds(w, ATT_WIN), :]
        vw = v_ref[pl.ds(w, ATT_WIN), :]
        qv = q_ref[...]
        first = lax.broadcasted_iota(jnp.int32, (1, 128), 1) < ATT_HEAD_DIM
        outs = []
        for hh, m in enumerate((first, jnp.logical_not(first))):
            e, inv = _ca_scores(_ca_head(qv, m), kw, b_ref[hh])
            outs.append(_dot(e.astype(BF16), vw) * inv)
        o = jnp.where(first, outs[0], outs[1])
        r_ref[...] = (o * _silu(g_ref[...])).astype(BF16)
        o_ref[...] = o.astype(BF16)

    blk = pl.BlockSpec((ATT_QB, 128), lambda hp, b: (b, hp))
    return pl.pallas_call(
        body, name="ca_fwd", grid=(ATT_PAIRS, nb),
        in_specs=[blk, pl.BlockSpec((Lp, 128), lambda hp, b: (0, hp)),
                  pl.BlockSpec((Lp, 128), lambda hp, b: (0, ATT_PAIRS + hp)), blk, _ca_bias_spec()],
        out_specs=[blk, blk], out_shape=[jax.ShapeDtypeStruct((L, D), BF16), jax.ShapeDtypeStruct((L, D), BF16)],
        compiler_params=_cp("parallel", "arbitrary"),
    )(q, kvp, kvp, gate, bias)


def ca_bwd(q, kvp, gate, bias, dr, o):
    L, D = q.shape
    Lp = kvp.shape[0]
    nb = L // ATT_QB

    def body(q_ref, k_ref, v_ref, g_ref, b_ref, dr_ref, o_ref, dq_ref, dg_ref, dk_ref, dv_ref, db_ref):
        b = pl.program_id(1)

        @pl.when(b == 0)
        def _():
            for r in (dk_ref, dv_ref, db_ref):
                r[...] = jnp.zeros_like(r)

        w = pl.multiple_of(b * ATT_QB, ATT_QB)
        kw = k_ref[pl.ds(w, ATT_WIN), :]
        vw = v_ref[pl.ds(w, ATT_WIN), :]
        qv = q_ref[...]
        gate_v = g_ref[...]
        drv = dr_ref[...]
        o = o_ref[...].astype(F32)
        sgate, dsgate = _silu_pair(gate_v)
        do = drv * sgate
        doo = do * o
        first = lax.broadcasted_iota(jnp.int32, (1, 128), 1) < ATT_HEAD_DIM
        dqs = []
        dkw = jnp.zeros((ATT_WIN, 128), F32)
        dvw = jnp.zeros((ATT_WIN, 128), F32)
        for hh, m in enumerate((first, jnp.logical_not(first))):
            qh = _ca_head(qv, m)
            e, inv = _ca_scores(qh, kw, b_ref[hh])
            eb = e.astype(BF16)
            doh = jnp.where(m, do, 0.0)
            dp = _dot_nt(doh.astype(BF16), vw)
            dvw = dvw + _dot_tn(eb, (doh * inv).astype(BF16))
            rs = jnp.sum(jnp.where(m, doo, 0.0), axis=-1, keepdims=True)
            ds = e * ((dp - rs) * inv)
            db_ref[hh] += ds
            dsb = ds.astype(BF16)
            dqs.append(_dot(dsb, kw))
            dkw = dkw + _dot_tn(dsb, qh)
        dg_ref[...] = (drv * o * dsgate).astype(BF16)
        dq_ref[...] = (jnp.where(first, dqs[0], dqs[1]) * ATT_SCALE).astype(BF16)
        dk_ref[pl.ds(w, ATT_WIN), :] += dkw
        dv_ref[pl.ds(w, ATT_WIN), :] += dvw

    blk = pl.BlockSpec((ATT_QB, 128), lambda hp, b: (b, hp))
    kblk = pl.BlockSpec((Lp, 128), lambda hp, b: (0, hp))
    vblk = pl.BlockSpec((Lp, 128), lambda hp, b: (0, ATT_PAIRS + hp))
    bblk = pl.BlockSpec((2, ATT_QB, ATT_WIN), lambda hp, b: (hp, 0, 0))
    return pl.pallas_call(
        body, name="ca_bwd", grid=(ATT_PAIRS, nb),
        in_specs=[blk, kblk, vblk, blk, _ca_bias_spec(), blk, blk],
        out_specs=[blk, blk, kblk, kblk, bblk],
        out_shape=[jax.ShapeDtypeStruct((L, D), BF16), jax.ShapeDtypeStruct((L, D), BF16),
                   jax.ShapeDtypeStruct((Lp, D), F32), jax.ShapeDtypeStruct((Lp, D), F32),
                   jax.ShapeDtypeStruct(bias.shape[1:], F32)],
        compiler_params=_cp("parallel", "arbitrary"),
    )(q, kvp, kvp, gate, bias, dr, o)


def loss_bwd(x, target, g, tm=ROW_TILE):
    L, D = x.shape
    tm = min(tm, L)

    def body(x_ref, t_ref, g_ref, loss_ref, dx_ref, dg_ref):
        @pl.when(pl.program_id(0) == 0)
        def _():
            loss_ref[...] = jnp.zeros_like(loss_ref)
            dg_ref[...] = jnp.zeros_like(dg_ref)

        xv = x_ref[...]
        gv = g_ref[...]
        _, xhat = _rms_parts(xv)
        err = xhat * gv - t_ref[...]
        loss_ref[...] += 0.5 * jnp.sum(jnp.sum(err * err, axis=-1, keepdims=True), axis=0, keepdims=True) / D
        dx, dg = _rms_bwd(xv, gv, err / D)
        dx_ref[...] = dx
        dg_ref[...] += dg

    row = pl.BlockSpec((tm, D), lambda t: (t, 0))
    return pl.pallas_call(
        body, name="loss_bwd", grid=(L // tm,),
        in_specs=[row, row, _full(g.shape)],
        out_specs=[_full((1, 128)), row, _full((1, D))],
        out_shape=[jax.ShapeDtypeStruct((1, 128), F32), jax.ShapeDtypeStruct((L, D), F32),
                   jax.ShapeDtypeStruct((1, D), F32)],
        compiler_params=_cp("arbitrary"),
    )(x, target, g)


_ADAM_C1 = 1.0 / (1.0 - ADAM_B1 ** ADAM_STEP)
_ADAM_C2 = 1.0 / (1.0 - ADAM_B2 ** ADAM_STEP)


def _adam_update(w, g, m, v):
    mn = ADAM_B1 * m + (1.0 - ADAM_B1) * g
    vn = ADAM_B2 * v + (1.0 - ADAM_B2) * g * g
    delta = -ADAM_LR * ((mn * _ADAM_C1) / (jnp.sqrt(vn * _ADAM_C2) + ADAM_EPS) + ADAM_WD * w)
    return delta, mn, vn


def adamw(w, g, m, v, name, tr=512):
    R, C = w.shape
    tr = min(tr, R)

    def body(w_ref, g_ref, m_ref, v_ref, d_ref, mo_ref, vo_ref):
        d_ref[...], mo_ref[...], vo_ref[...] = _adam_update(w_ref[...], g_ref[...], m_ref[...], v_ref[...])

    blk = pl.BlockSpec((tr, C), lambda i: (i, 0))
    sh = jax.ShapeDtypeStruct((R, C), F32)
    return pl.pallas_call(
        body, name=name, grid=(R // tr,), in_specs=[blk] * 4, out_specs=[blk] * 3,
        out_shape=[sh] * 3, compiler_params=_cp("parallel"),
    )(w, g, m, v)


def adamw_allreduce(gathered, w, m, v, shard, name, slot=None):
    R, C = w.shape
    sharded = slot is None and gathered.shape[2] != C

    def body(s_ref, ga_ref, w_ref, m_ref, v_ref, g_ref, d_ref, mo_ref, vo_ref):
        take = (lambda d: ga_ref[d]) if slot is None else (lambda d: ga_ref[d, slot:slot + R, 0:C])
        g = take(0)
        for d in range(1, N_DEV):
            g = g + take(d)
        g_ref[...] = g
        d_ref[...], mo_ref[...], vo_ref[...] = _adam_update(w_ref[...], g, m_ref[...], v_ref[...])

    blk = pl.BlockSpec((R, C), lambda i, s_ref: (0, 0))
    if slot is not None:
        gblk = pl.BlockSpec(gathered.shape, lambda i, s_ref: (0, 0, 0))
    else:
        gblk = pl.BlockSpec((N_DEV, R, C),
                            (lambda i, s_ref: (0, 0, s_ref[0])) if sharded else (lambda i, s_ref: (0, 0, 0)))
    sh = jax.ShapeDtypeStruct((R, C), F32)
    return pl.pallas_call(
        body, name=name,
        grid_spec=pltpu.PrefetchScalarGridSpec(num_scalar_prefetch=1, grid=(1,), in_specs=[gblk, blk, blk, blk],
                                               out_specs=[blk] * 4),
        out_shape=[sh] * 4, compiler_params=_cp("arbitrary"),
    )(shard, gathered, w, m, v)


LATE = ("ev_s5_glu_w", "ev_w_out", "od_w_in", "od_w_out", "xa_w_qg", "xa_w_kv", "xa_w_o")
EARLY_GRADS = ("od_w_in", "od_w_out", "xa_w_qg", "xa_w_kv", "xa_w_o", "ev_w_out", "ev_s5_glu_w")


def _reduce_to_chip(gs, names, core, tag):
    from_sibling = sibling_send_other_half(gs, "sibling_send_" + tag)
    return [sum_with_sibling(gi, ri, core, "sum_sibling_" + n) for n, gi, ri in zip(names, gs, from_sibling)]


def local_step(x, mem, target, p, gw, late, place, core):
    row = lambda a: a.reshape(1, -1)
    D = D_MODEL
    L = x.shape[0]
    g, big = {}, {}
    gw = dict(gw)

    z, h0b = norm_mm(x, p["ev_norm_g"], gw["ev_w_in"], [((0, 1, 2, 3), F32, 0)], "ev_in")
    ys, s5_saved, landed = s5_mixer_core_fwd(
        z, p["ev_s5_lambda_re"][0], p["ev_s5_lambda_im"][0], p["ev_s5_log_dt"][0], p["ev_s5_b_re"][0],
        p["ev_s5_b_im"][0], p["ev_s5_c_re"][0], p["ev_s5_c_im"][0], p["ev_s5_d"][0],
        carried=carried_allgather([late[n] for n in LATE]))
    for n, gth in zip(LATE, landed):
        rows = gth.shape[1]
        gw[n] = gth.reshape(N_CHIPS, 2, rows // 2, gth.shape[2]) if n.startswith("xa_") else gth
    memn_b = rms_fwd(mem, row(p["mem_norm_g"]), "mem_norm")
    kvs = [mm_cols(memn_b, gw["xa_w_kv"], l, f"xa_kv{l}", BF16) for l in range(2)]
    conv_w = p["ev_conv_w"][0]
    c = conv_fwd(z, conv_w, p["ev_conv_b"])
    tail = (gw["ev_s5_glu_w"], p["ev_s5_glu_b"], p["ev_conv_ln_g"], p["ev_conv_ln_b"], gw["ev_w_out"])
    x1 = ev_tail_fwd(ys, z, c, x, *tail)
    xa0 = (row(p["xa_norm_g"][0]), gw["xa_w_qg"], kvs[0], gw["xa_w_o"], 0)
    x2 = xa_fwd(x1, *xa0, "xa_fwd0")

    q, kvp, gate, h1b = norm_mm(x2, p["od_norm_g"], gw["od_w_in"],
                                [((0,), BF16, 0), ((1, 2), BF16, ATT_PAD), ((3,), F32, 0)], "od_in")
    kvp = zero_rows(kvp, ATT_PAD, "od_kv_pad")
    bias = att_bias(p["od_rel_bias"][0])
    r, att_o = ca_fwd(q, kvp, gate, bias)
    x3 = mm_res(r, gw["od_w_out"], x2, "od_out")
    xa1 = (row(p["xa_norm_g"][1]), gw["xa_w_qg"], kvs[1], gw["xa_w_o"], 1)
    x4 = xa_fwd(x3, *xa1, "xa_fwd1")

    loss, dx4, dgf = loss_bwd(x4, target, row(p["final_norm_g"]))
    g["final_norm_g"] = dgf.reshape(D)

    dx3, dqg1, hx1, rx1, dkv1, dgxa1 = xa_bwd(x3, dx4, *xa1, "xa_bwd1")
    dwqg = mm_tn(hx1, dqg1, "xa_dwqg1", ("cols", 1))
    dwo = mm_tn(rx1, dx4, "xa_dwo1", ("rows", 1))

    big["od_w_out"] = mm_tn(r, dx3, "od_dwout", ("rows",))
    dr = mm_nt_rows(dx3, gw["od_w_out"], "od_out_bwd")
    dq, dgate, dkp, dvp, dbias = ca_bwd(q, kvp, gate, bias, dr, att_o)
    pieces, offs = (dq, dkp, dvp, dgate), (0, ATT_PAD, ATT_PAD, 0)
    dwin = None
    for s in range(N_CHIPS):
        dwin = mm_tn(h1b, pieces[s], f"od_dwin{s}", ("slab", s), into=dwin, b_off=offs[s],
                     bl=ATT_PAD if offs[s] else 1024)
    big["od_w_in"] = dwin
    dx2, dgod = mm_nt_normbwd(pieces, offs, gw["od_w_in"], x2, p["od_norm_g"], dx3, "od_in_bwd")
    g["od_norm_g"] = dgod
    g["od_rel_bias"] = relbias_bwd(dbias)[None]

    dx1, dqg0, hx0, rx0, dkv0, dgxa0 = xa_bwd(x1, dx2, *xa0, "xa_bwd0")
    big["xa_w_qg"] = mm_tn(hx0, dqg0, "xa_dwqg0", ("cols", 0), into=dwqg)
    big["xa_w_o"] = mm_tn(rx0, dx2, "xa_dwo0", ("rows", 0), into=dwo)
    g["xa_norm_g"] = jnp.concatenate([dgxa0, dgxa1], axis=0)

    dys, dc, dz, ra, z1b, dtb, dbglu, dlng, dlnb = ev_tail_bwd(ys, z, c, dx1, *tail)
    big["ev_w_out"] = mm_tn(ra, dx1, "ev_dwout", ("rows",))
    big["ev_s5_glu_w"] = mm_tn(z1b, dtb, "ev_dwglu", ("rows",))
    g["ev_s5_glu_b"], g["ev_conv_ln_g"], g["ev_conv_ln_b"] = dbglu, dlng, dlnb
    dwkv = mm_tn(memn_b, dkv1, "xa_dwkv1", ("cols", 1), bl=MEM_LEN)
    big["xa_w_kv"] = mm_tn(memn_b, dkv0, "xa_dwkv0", ("cols", 0), into=dwkv, bl=MEM_LEN)
    dmem0 = mm_nt_cols(dkv0, gw["xa_w_kv"], 0, "xa_kv_bwd0")
    dmem1 = mm_nt_cols(dkv1, gw["xa_w_kv"], 1, "xa_kv_bwd1")
    g["mem_norm_g"] = rms_dgain(mem, dmem0, dmem1, "mem_norm_bwd").reshape(D)

    shard_major = lambda t: t.reshape((-1,) + t.shape[-2:])
    gs = [shard_major(big[n]) for n in EARLY_GRADS]
    dz, dconvw, dconvb, *from_sibling = conv_bwd(z, dc, dz, conv_w, carried=carried_sibling_send(gs))
    g["ev_conv_w"] = dconvw[None, :CONV_KERNEL]
    g["ev_conv_b"] = dconvb
    chip_sums = [sum_with_sibling(gi, ri, core, "sum_sibling_" + n) for n, gi, ri in zip(EARLY_GRADS, gs, from_sibling)]
    dz, s5g, from_chips = s5_mixer_core_bwd(z, dys, dz, p["ev_s5_lambda_re"][0], p["ev_s5_lambda_im"][0], s5_saved,
                                            carried=carried_chips_exchange(chip_sums))
    reduced = {n: sum_chips(ci, ri, place, "sum_chips_" + n) for n, ci, ri in zip(EARLY_GRADS, chip_sums, from_chips)}
    for n, v in s5g.items():
        g["ev_s5_" + n] = v[None]
    dwin_ev = mm_tn(h0b, dz, "ev_dwin", ("cols",))
    grad_x, dgev = mm_nt_normbwd((dz,), (0,), gw["ev_w_in"], x, p["ev_norm_g"], dx1, "ev_in_bwd")
    g["ev_norm_g"] = dgev
    chip_sum = _reduce_to_chip([dwin_ev], ["ev_w_in"], core, "last")
    reduced["ev_w_in"] = sum_chips(chip_sum[0], chips_exchange(chip_sum)[0], place, "sum_chips_ev_w_in")
    return loss, grad_x, g, reduced


def _me():
    return lax.axis_index("x"), lax.axis_index("y"), lax.axis_index("c")


def _other_chips(x, y):
    return [(1 - x, y), (x, 1 - y), (1 - x, 1 - y)]


def _remote(src, dst, send_sems, recv_sems, k, to):
    return pltpu.make_async_remote_copy(src_ref=src, dst_ref=dst, send_sem=send_sems.at[k], recv_sem=recv_sems.at[k],
                                        device_id=to, device_id_type=MESH)


def _rows_half(ref, h):
    H = ref.shape[-2] // 2
    return ref.at[(slice(None),) * (len(ref.shape) - 2) + (pl.ds(h * H, H), slice(None))]


def allgather_chip_blocks(halved, whole):
    nh, nw = len(halved), len(whole)
    n = nh + nw

    def body(*refs):
        ins, outs = refs[:n], refs[n:2 * n]
        send_sems, recv_sems, local_sems = refs[2 * n:]
        x, y, c = _me()
        sib = (x, y, 1 - c)
        chips = _other_chips(x, y)
        me = 2 * x + y
        local = [pltpu.make_async_copy(ins[i], outs[i].at[me], local_sems.at[i]) for i in range(n)]
        for cp in local:
            cp.start()
        first, passed = [], []
        for i in range(n):
            for j, (cx, cy) in enumerate(chips):
                if i < nh:
                    src, dst = _rows_half(ins[i], c), _rows_half(outs[i].at[me], c)
                    k = 6 * i + j
                else:
                    src, dst = ins[i], outs[i].at[me]
                    k = 6 * nh + 3 * (i - nh) + j
                first.append(_remote(src, dst, send_sems, recv_sems, k, (cx, cy, c)))
        for cp in first:
            cp.start()
        for j, (cx, cy) in enumerate(chips):
            for i in range(nh):
                got = _rows_half(outs[i].at[2 * cx + cy], c)
                _remote(got, got, send_sems, recv_sems, 6 * i + j, (cx, cy, c)).wait_recv()
                fw = _remote(got, got, send_sems, recv_sems, 6 * i + 3 + j, sib)
                fw.start()
                passed.append(fw)
        for j, (cx, cy) in enumerate(chips):
            for i in range(nh):
                got = _rows_half(outs[i].at[2 * cx + cy], 1 - c)
                _remote(got, got, send_sems, recv_sems, 6 * i + 3 + j, sib).wait_recv()
            for i in range(nh, n):
                got = outs[i].at[2 * cx + cy]
                _remote(got, got, send_sems, recv_sems, 6 * nh + 3 * (i - nh) + j, (cx, cy, c)).wait_recv()
        for cp in first + passed:
            cp.wait_send()
        for cp in local:
            cp.wait()

    arrays = list(halved) + list(whole)
    nsem = 6 * nh + 3 * nw
    return pl.pallas_call(
        body, name="allgather_chip_blocks", in_specs=[ANY] * n, out_specs=[ANY] * n,
        out_shape=[jax.ShapeDtypeStruct((N_CHIPS,) + a.shape, a.dtype) for a in arrays],
        scratch_shapes=[pltpu.SemaphoreType.DMA((nsem,)), pltpu.SemaphoreType.DMA((nsem,)),
                        pltpu.SemaphoreType.DMA((n,))],
    )(*arrays)


def allgather_devices(vs):
    n = len(vs)

    def body(*refs):
        ins, outs = refs[:n], refs[n:2 * n]
        send_sems, recv_sems, local_sems = refs[2 * n:]
        x, y, c = _me()
        sib = (x, y, 1 - c)
        chips = _other_chips(x, y)
        me = 4 * x + 2 * y + c
        local = [pltpu.make_async_copy(ins[i], outs[i].at[me], local_sems.at[i]) for i in range(n)]
        for cp in local:
            cp.start()
        first, passed = [], []
        for i in range(n):
            first.append(_remote(ins[i], outs[i].at[me], send_sems, recv_sems, 7 * i, sib))
            for j, (cx, cy) in enumerate(chips):
                first.append(_remote(ins[i], outs[i].at[me], send_sems, recv_sems, 7 * i + 1 + j, (cx, cy, c)))
        for cp in first:
            cp.start()
        for j, (cx, cy) in enumerate(chips):
            for i in range(n):
                got = outs[i].at[4 * cx + 2 * cy + c]
                _remote(got, got, send_sems, recv_sems, 7 * i + 1 + j, (cx, cy, c)).wait_recv()
                fw = _remote(got, got, send_sems, recv_sems, 7 * i + 4 + j, sib)
                fw.start()
                passed.append(fw)
        for i in range(n):
            got = outs[i].at[4 * x + 2 * y + (1 - c)]
            _remote(got, got, send_sems, recv_sems, 7 * i, sib).wait_recv()
            for j, (cx, cy) in enumerate(chips):
                got = outs[i].at[4 * cx + 2 * cy + (1 - c)]
                _remote(got, got, send_sems, recv_sems, 7 * i + 4 + j, sib).wait_recv()
        for cp in first + passed:
            cp.wait_send()
        for cp in local:
            cp.wait()

    return pl.pallas_call(
        body, name="allgather_devices", in_specs=[ANY] * n, out_specs=[ANY] * n,
        out_shape=[jax.ShapeDtypeStruct((N_DEV,) + v.shape, v.dtype) for v in vs],
        scratch_shapes=[pltpu.SemaphoreType.DMA((7 * n,)), pltpu.SemaphoreType.DMA((7 * n,)),
                        pltpu.SemaphoreType.DMA((n,))],
    )(*vs)


def sibling_send_other_half(gs, name):
    n = len(gs)

    def body(*refs):
        ins, outs = refs[:n], refs[n:2 * n]
        send_sems, recv_sems = refs[2 * n:]
        x, y, c = _me()
        cps = [_remote(_rows_half(ins[i], 1 - c), outs[i], send_sems, recv_sems, i, (x, y, 1 - c)) for i in range(n)]
        for cp in cps:
            cp.start()
        for cp in cps:
            cp.wait()

    return pl.pallas_call(
        body, name=name, in_specs=[ANY] * n, out_specs=[ANY] * n,
        out_shape=[jax.ShapeDtypeStruct((g.shape[0], g.shape[1] // 2, g.shape[2]), g.dtype) for g in gs],
        scratch_shapes=[pltpu.SemaphoreType.DMA((n,)), pltpu.SemaphoreType.DMA((n,))],
    )(*gs)


def chips_exchange(parts):
    n = len(parts)

    def body(*refs):
        ins, outs = refs[:n], refs[n:2 * n]
        send_sems, recv_sems = refs[2 * n:]
        x, y, c = _me()
        cps = []
        for i in range(n):
            nl = ins[i].shape[0] // N_CHIPS
            for j, (cx, cy) in enumerate(_other_chips(x, y)):
                cps.append(_remote(ins[i].at[pl.ds((2 * cx + cy) * nl, nl)], outs[i].at[j], send_sems, recv_sems,
                                   3 * i + j, (cx, cy, c)))
        for cp in cps:
            cp.start()
        for cp in cps:
            cp.wait()

    return pl.pallas_call(
        body, name="chips_exchange", in_specs=[ANY] * n, out_specs=[ANY] * n,
        out_shape=[jax.ShapeDtypeStruct((3, a.shape[0] // N_CHIPS) + a.shape[1:], a.dtype) for a in parts],
        scratch_shapes=[pltpu.SemaphoreType.DMA((3 * n,)), pltpu.SemaphoreType.DMA((3 * n,))],
    )(*parts)


def sibling_share(fulls):
    n = len(fulls)

    def body(*refs):
        outs = refs[n:2 * n]
        send_sems, recv_sems = refs[2 * n:]
        x, y, c = _me()
        cps = [_remote(_rows_half(outs[i], c), _rows_half(outs[i], c), send_sems, recv_sems, i, (x, y, 1 - c))
               for i in range(n)]
        for cp in cps:
            cp.start()
        for i in range(n):
            got = _rows_half(outs[i], 1 - c)
            _remote(got, got, send_sems, recv_sems, i, (x, y, 1 - c)).wait_recv()
        for cp in cps:
            cp.wait_send()

    return pl.pallas_call(
        body, name="sibling_share", in_specs=[ANY] * n, out_specs=[ANY] * n,
        out_shape=[jax.ShapeDtypeStruct(f.shape, f.dtype) for f in fulls],
        input_output_aliases={i: i for i in range(n)},
        scratch_shapes=[pltpu.SemaphoreType.DMA((n,)), pltpu.SemaphoreType.DMA((n,))],
    )(*fulls)


def sum_with_sibling(g, recv, core, name):
    S, H, C = recv.shape
    tr = min(512, H)

    def body(c_ref, g_ref, r_ref, o_ref):
        o_ref[...] = (g_ref[...].astype(F32) + r_ref[...].astype(F32)).astype(o_ref.dtype)

    nb = H // tr
    return pl.pallas_call(
        body, name=name,
        grid_spec=pltpu.PrefetchScalarGridSpec(
            num_scalar_prefetch=1, grid=(S, nb),
            in_specs=[pl.BlockSpec((None, tr, C), lambda s, i, c_ref: (s, c_ref[0] * nb + i, 0)),
                      pl.BlockSpec((None, tr, C), lambda s, i, c_ref: (s, i, 0))],
            out_specs=pl.BlockSpec((None, tr, C), lambda s, i, c_ref: (s, i, 0))),
        out_shape=jax.ShapeDtypeStruct((S, H, C), g.dtype), compiler_params=_cp("parallel", "parallel"),
    )(core, g, recv)


def sum_chips(a, recv, place, name):
    _, nl, H, C = recv.shape
    tr = min(512, H)
    nb = H // tr

    def body(p_ref, a_ref, r_ref, o_ref):
        acc = a_ref[...].astype(F32)
        for j in range(3):
            acc = acc + r_ref[j].astype(F32)
        o_ref[...] = acc

    return pl.pallas_call(
        body, name=name,
        grid_spec=pltpu.PrefetchScalarGridSpec(
            num_scalar_prefetch=1, grid=(nl, nb),
            in_specs=[pl.BlockSpec((None, tr, C), lambda l, i, p_ref: (p_ref[0] * nl + l, i, 0)),
                      pl.BlockSpec((3, None, tr, C), lambda l, i, p_ref: (0, l, i, 0))],
            out_specs=pl.BlockSpec((None, tr, C), lambda l, i, p_ref: (l, p_ref[1] * nb + i, 0))),
        out_shape=jax.ShapeDtypeStruct((nl, 2 * H, C), F32), compiler_params=_cp("parallel", "parallel"),
    )(place, a, recv)


def pack_rows(arrays, name):
    starts, r0 = [], 0
    for a in arrays:
        if a.shape[0] >= SUBLANES:
            r0 = -(-r0 // SUBLANES) * SUBLANES
        starts.append(r0)
        r0 += a.shape[0]
    r0 = -(-r0 // SUBLANES) * SUBLANES
    n = len(arrays)

    def body(*refs):
        o_ref = refs[n]
        o_ref[...] = jnp.zeros_like(o_ref)
        for a_ref, s in zip(refs[:n], starts):
            r, c = a_ref.shape
            o_ref[s:s + r, 0:c] = a_ref[...]

    out = pl.pallas_call(body, name=name, out_shape=jax.ShapeDtypeStruct((r0, PACK_COLS), F32))(*arrays)
    return out, starts


def sum_slot(gathered, slot, shape, name):
    r, c = shape

    def body(ga_ref, o_ref):
        acc = ga_ref[0, slot:slot + r, 0:c]
        for d in range(1, N_DEV):
            acc = acc + ga_ref[d, slot:slot + r, 0:c]
        o_ref[...] = acc

    return pl.pallas_call(body, name=name, out_shape=jax.ShapeDtypeStruct((r, c), F32))(gathered)


def carried_allgather(blocks):
    n = len(blocks)

    def first_hop(ins, outs, sems, i, j, chip, x, y, c):
        me = 2 * x + y
        return _remote(_rows_half(ins[i], c), _rows_half(outs[i].at[me], c), sems[0], sems[1], 6 * i + j, (*chip, c))

    def start(ins, outs, sems):
        x, y, c = _me()
        for i in range(n):
            pltpu.make_async_copy(ins[i], outs[i].at[2 * x + y], sems[2].at[i]).start()
        for i in range(n):
            for j, chip in enumerate(_other_chips(x, y)):
                first_hop(ins, outs, sems, i, j, chip, x, y, c).start()

    def finish(ins, outs, sems):
        x, y, c = _me()
        sib = (x, y, 1 - c)
        chips = _other_chips(x, y)
        passed = []
        for j, (cx, cy) in enumerate(chips):
            for i in range(n):
                got = _rows_half(outs[i].at[2 * cx + cy], c)
                _remote(got, got, sems[0], sems[1], 6 * i + j, (cx, cy, c)).wait_recv()
                fw = _remote(got, got, sems[0], sems[1], 6 * i + 3 + j, sib)
                fw.start()
                passed.append(fw)
        for j, (cx, cy) in enumerate(chips):
            for i in range(n):
                got = _rows_half(outs[i].at[2 * cx + cy], 1 - c)
                _remote(got, got, sems[0], sems[1], 6 * i + 3 + j, sib).wait_recv()
        for i in range(n):
            for j, chip in enumerate(chips):
                first_hop(ins, outs, sems, i, j, chip, x, y, c).wait_send()
        for fw in passed:
            fw.wait_send()
        for i in range(n):
            pltpu.make_async_copy(ins[i], outs[i].at[2 * x + y], sems[2].at[i]).wait()

    return Carried(blocks, [jax.ShapeDtypeStruct((N_CHIPS,) + b.shape, b.dtype) for b in blocks],
                   [pltpu.SemaphoreType.DMA((6 * n,)), pltpu.SemaphoreType.DMA((6 * n,)), pltpu.SemaphoreType.DMA((n,))],
                   start, None, finish)


def carried_sibling_send(gs):
    n = len(gs)

    def copies(ins, outs, sems):
        x, y, c = _me()
        return [_remote(_rows_half(ins[i], 1 - c), outs[i], sems[0], sems[1], i, (x, y, 1 - c)) for i in range(n)]

    def start(ins, outs, sems):
        for cp in copies(ins, outs, sems):
            cp.start()

    def finish(ins, outs, sems):
        for cp in copies(ins, outs, sems):
            cp.wait()

    return Carried(gs, [jax.ShapeDtypeStruct((g.shape[0], g.shape[1] // 2, g.shape[2]), g.dtype) for g in gs],
                   [pltpu.SemaphoreType.DMA((n,)), pltpu.SemaphoreType.DMA((n,))], start, None, finish)


def carried_chips_exchange(parts):
    n = len(parts)

    def copies(ins, outs, sems):
        x, y, c = _me()
        cps = []
        for i in range(n):
            nl = ins[i].shape[0] // N_CHIPS
            for j, (cx, cy) in enumerate(_other_chips(x, y)):
                cps.append(_remote(ins[i].at[pl.ds((2 * cx + cy) * nl, nl)], outs[i].at[j], sems[0], sems[1],
                                   3 * i + j, (cx, cy, c)))
        return cps

    def start(ins, outs, sems):
        for cp in copies(ins, outs, sems):
            cp.start()

    def finish(ins, outs, sems):
        for cp in copies(ins, outs, sems):
            cp.wait()

    return Carried(parts, [jax.ShapeDtypeStruct((3, a.shape[0] // N_CHIPS) + a.shape[1:], a.dtype) for a in parts],
                   [pltpu.SemaphoreType.DMA((3 * n,)), pltpu.SemaphoreType.DMA((3 * n,))], start, None, finish)


BIG = ("ev_w_in", "ev_s5_glu_w", "ev_w_out", "od_w_in", "od_w_out", "xa_w_qg", "xa_w_kv", "xa_w_o")
SHARDED_F32 = (("ev_conv_w", 2), ("od_norm_g", 1))
SMALL = ("mem_norm_g", "ev_norm_g", "ev_s5_lambda_re", "ev_s5_lambda_im", "ev_s5_log_dt", "ev_s5_b_re", "ev_s5_b_im",
         "ev_s5_c_re", "ev_s5_c_im", "ev_s5_d", "ev_s5_glu_b", "ev_conv_b", "ev_conv_ln_g", "ev_conv_ln_b",
         "od_rel_bias", "xa_norm_g", "final_norm_g")
NARROW = ("ev_s5_c_re", "ev_s5_c_im")
DENSE_B = ("ev_s5_b_re", "ev_s5_b_im")
PACK_COLS = 1024
WEIGHTS = ("mem_norm_g", "ev_norm_g", "ev_w_in", "ev_s5_lambda_re", "ev_s5_lambda_im", "ev_s5_log_dt", "ev_s5_b_re",
           "ev_s5_b_im", "ev_s5_c_re", "ev_s5_c_im", "ev_s5_d", "ev_s5_glu_w", "ev_s5_glu_b", "ev_conv_w", "ev_conv_b",
           "ev_conv_ln_g", "ev_conv_ln_b", "ev_w_out", "od_norm_g", "od_w_in", "od_rel_bias", "od_w_out", "xa_norm_g",
           "xa_w_qg", "xa_w_kv", "xa_w_o", "final_norm_g")


def _as2d(a):
    return a.reshape(1, -1) if a.ndim == 1 else a.reshape(-1, a.shape[-1])


def kernel(x, mem, mem_norm_g, ev_norm_g, ev_w_in, ev_s5_lambda_re, ev_s5_lambda_im, ev_s5_log_dt, ev_s5_b_re, ev_s5_b_im, ev_s5_c_re, ev_s5_c_im, ev_s5_d, ev_s5_glu_w, ev_s5_glu_b, ev_conv_w, ev_conv_b, ev_conv_ln_g, ev_conv_ln_b, ev_w_out, od_norm_g, od_w_in, od_rel_bias, od_w_out, xa_norm_g, xa_w_qg, xa_w_kv, xa_w_o, final_norm_g, loss_target, m_mem_norm_g, m_ev_norm_g, m_ev_w_in, m_ev_s5_lambda_re, m_ev_s5_lambda_im, m_ev_s5_log_dt, m_ev_s5_b_re, m_ev_s5_b_im, m_ev_s5_c_re, m_ev_s5_c_im, m_ev_s5_d, m_ev_s5_glu_w, m_ev_s5_glu_b, m_ev_conv_w, m_ev_conv_b, m_ev_conv_ln_g, m_ev_conv_ln_b, m_ev_w_out, m_od_norm_g, m_od_w_in, m_od_rel_bias, m_od_w_out, m_xa_norm_g, m_xa_w_qg, m_xa_w_kv, m_xa_w_o, m_final_norm_g, v_mem_norm_g, v_ev_norm_g, v_ev_w_in, v_ev_s5_lambda_re, v_ev_s5_lambda_im, v_ev_s5_log_dt, v_ev_s5_b_re, v_ev_s5_b_im, v_ev_s5_c_re, v_ev_s5_c_im, v_ev_s5_d, v_ev_s5_glu_w, v_ev_s5_glu_b, v_ev_conv_w, v_ev_conv_b, v_ev_conv_ln_g, v_ev_conv_ln_b, v_ev_w_out, v_od_norm_g, v_od_w_in, v_od_rel_bias, v_od_w_out, v_xa_norm_g, v_xa_w_qg, v_xa_w_kv, v_xa_w_o, v_final_norm_g):
    a = dict(locals())
    w = {n: a[n] for n in WEIGHTS}
    shard = (2 * lax.axis_index("x") + lax.axis_index("y")).reshape(1).astype(jnp.int32)
    core = lax.axis_index("c").reshape(1).astype(jnp.int32)

    place = jnp.concatenate([shard, core])

    blocks = {n: w[n].astype(BF16).reshape(-1, w[n].shape[-1]) for n in BIG}
    early = [n for n in BIG if n not in LATE]
    gathered = allgather_chip_blocks([blocks[n] for n in early], [_as2d(w[n]) for n, _ in SHARDED_F32])
    gw = dict(zip(early, gathered))
    p = {n: w[n] for n in SMALL}
    conv_g, odn_g = gathered[len(early):]
    p["ev_conv_w"] = jnp.concatenate([conv_g[s] for s in range(N_CHIPS)], axis=1)[None]
    p["od_norm_g"] = odn_g.reshape(1, D_MODEL)

    loss, grad_x, g, reduced = local_step(x[0], mem[0], loss_target[0], p, gw, {n: blocks[n] for n in LATE},
                                          place, core)
    loss = lax.psum(loss[0, 0], ("x", "y", "c"))
    g_big = dict(zip(BIG, sibling_share([reduced[n] for n in BIG])))

    out = {tag: {} for tag in ("grad", "delta", "m", "v")}
    for n in BIG:
        sh = w[n].shape
        to2d = lambda t: t.reshape(-1, sh[-1])
        gn = to2d(g_big[n])
        d, mn, vn = adamw(to2d(w[n]), gn, to2d(a["m_" + n]), to2d(a["v_" + n]), "adamw_" + n)
        for tag, val in zip(("grad", "delta", "m", "v"), (gn, d, mn, vn)):
            out[tag][n] = val.reshape(sh)

    packed_names = [n for n in SMALL if n not in NARROW]
    single_names = list(NARROW) + [n for n, _ in SHARDED_F32]
    packed, slots = pack_rows([_as2d(g[n]) for n in packed_names], "pack_small_grads")
    gath = allgather_devices([packed] + [_as2d(g[n]) for n in single_names])
    jobs = [(n, gath[0], s) for n, s in zip(packed_names, slots)]
    jobs += [(n, gt, None) for n, gt in zip(single_names, gath[1:])]
    for n, gt, slot in jobs:
        sh = w[n].shape
        w2, m2, v2 = _as2d(w[n]), _as2d(a["m_" + n]), _as2d(a["v_" + n])
        if n in DENSE_B:
            gn = _as2d(s5_b_from_dense(sum_slot(gt, slot, g[n].shape[-2:], "sum_" + n)))
            d, mn, vn = adamw(w2, gn, m2, v2, "adamw_" + n)
        else:
            gn, d, mn, vn = adamw_allreduce(gt, w2, m2, v2, shard, "adamw_" + n, slot=slot)
        for tag, val in zip(("grad", "delta", "m", "v"), (gn, d, mn, vn)):
            out[tag][n] = val.reshape(sh)

    res = [loss, grad_x[None]]
    for tag in ("grad", "delta", "m", "v"):
        res += [out[tag][n] for n in WEIGHTS]
    return tuple(res)
```

```python
import math

import jax
import jax.numpy as jnp
import numpy as np
from jax import lax
from jax.experimental import pallas as pl
from jax.experimental.pallas import tpu as pltpu

F32 = jnp.float32
BF16 = jnp.bfloat16

D_MODEL = 1024
CHUNK = 64
LEFT_CHUNKS = 8
S5_WIDTH = 512
S5_GROUP = 16
S5_GROUPS = 32
S5_STATE = 64
S5_COLS = S5_GROUPS * S5_STATE
S5_SPLIT = 4
S5_CC = S5_COLS // S5_SPLIT
S5_UC = S5_WIDTH // S5_SPLIT
CONV_WIDTH = 512
CONV_KERNEL = 31
CONV_HALO = 32
ATT_HEADS = 16
ATT_HEAD_DIM = 64
MAX_REL = 128
MEM_LEN = 256
XA_HEADS = 4
XA_HEAD_DIM = 256
EPS = 1e-6
EVEN_IN = 2560
ODD_IN = 4096

ADAM_LR = 0.001
ADAM_B1 = 0.9
ADAM_B2 = 0.999
ADAM_EPS = 1e-08
ADAM_WD = 0.01
ADAM_STEP = 10

ROW_TILE = 256
MM_TILE = 512
S5_TILE = 512
ATT_QB = 256
ATT_PAD = LEFT_CHUNKS * CHUNK
ATT_WIN = ATT_PAD + ATT_QB
VMEM_LIMIT_V7X = 56 * 1024 * 1024
NEG = -1e30
LANES = 128
N_CHIPS = 4
N_DEV = 8

MESH = pl.DeviceIdType.MESH
ANY = pl.BlockSpec(memory_space=pl.ANY)


def _cp(*sem, vmem=VMEM_LIMIT_V7X):
    return pltpu.CompilerParams(dimension_semantics=sem if sem else None, vmem_limit_bytes=vmem)


def _full(shape):
    n = len(shape)
    return pl.BlockSpec(shape, lambda *_: (0,) * n)


def _wspec(w, layer=None):
    if layer is None:
        return _full(w.shape)
    s, _, r, c = w.shape
    return pl.BlockSpec((s, None, r, c), lambda *_: (0, layer, 0, 0))


def _lane_tile(n, cap):
    return max(t for t in range(LANES, min(n, cap) + 1, LANES) if n % t == 0)


def _sigmoid(x):
    return 0.5 * jnp.tanh(0.5 * x) + 0.5


def _silu(x):
    return x * _sigmoid(x)


def _silu_pair(x):
    s = _sigmoid(x)
    return x * s, s * (1.0 + x * (1.0 - s))


_GELU_C = math.sqrt(2.0 / math.pi)


def _gelu(x):
    return 0.5 * x * (1.0 + jnp.tanh(_GELU_C * (x + 0.044715 * x * x * x)))


def _dgelu(x):
    t = jnp.tanh(_GELU_C * (x + 0.044715 * x * x * x))
    return 0.5 * (1.0 + t) + 0.5 * x * (1.0 - t * t) * _GELU_C * (1.0 + 3.0 * 0.044715 * x * x)


def _dot(a, b):
    return jnp.dot(a, b, preferred_element_type=F32)


def _dot_nt(a, b):
    return lax.dot_general(a, b, (((1,), (1,)), ((), ())), preferred_element_type=F32)


def _dot_tn(a, b):
    return lax.dot_general(a, b, (((0,), (0,)), ((), ())), preferred_element_type=F32)


def _dot_cols(a, w4, shards=range(N_CHIPS)):
    return jnp.concatenate([_dot(a, w4[s]) for s in shards], axis=1)


def _dot_rows(a, w4):
    r = w4.shape[1]
    acc = _dot(a[:, 0:r], w4[0])
    for s in range(1, N_CHIPS):
        acc = acc + _dot(a[:, s * r:(s + 1) * r], w4[s])
    return acc


def _dot_nt_cols(dys, w4):
    acc = _dot_nt(dys[0], w4[0])
    for s in range(1, N_CHIPS):
        acc = acc + _dot_nt(dys[s], w4[s])
    return acc


def _dot_nt_rows(dy, w4):
    return jnp.concatenate([_dot_nt(dy, w4[s]) for s in range(N_CHIPS)], axis=1)


def _col_pieces(v, n):
    return [v[:, s * n:(s + 1) * n] for s in range(N_CHIPS)]


def _rms_parts(xv):
    inv = lax.rsqrt(jnp.mean(xv * xv, axis=-1, keepdims=True) + EPS)
    return inv, xv * inv


def _rms_bwd(xv, g, dh):
    inv, xhat = _rms_parts(xv)
    dg = jnp.sum(dh * xhat, axis=0, keepdims=True)
    dxh = dh * g
    dx = inv * (dxh - xhat * jnp.mean(dxh * xhat, axis=-1, keepdims=True))
    return dx, dg


def norm_mm(x, g, w4, groups, name, tm=MM_TILE):
    M, D = x.shape
    n = w4.shape[2]
    tm = min(tm, M)

    def body(x_ref, g_ref, w_ref, *outs):
        _, xhat = _rms_parts(x_ref[...])
        hb = (xhat * g_ref[...]).astype(BF16)
        for o, (shards, dt, _) in zip(outs, groups):
            o[...] = _dot_cols(hb, w_ref, shards).astype(dt)
        outs[-1][...] = hb

    out_shape = [jax.ShapeDtypeStruct((M + pad, len(sh) * n), dt) for (sh, dt, pad) in groups]
    out_specs = [pl.BlockSpec((tm, len(sh) * n), lambda i, p=pad // tm: (i + p, 0)) for (sh, _, pad) in groups]
    out_shape.append(jax.ShapeDtypeStruct((M, D), BF16))
    out_specs.append(pl.BlockSpec((tm, D), lambda i: (i, 0)))
    return pl.pallas_call(
        body, name=name, grid=(M // tm,),
        in_specs=[pl.BlockSpec((tm, D), lambda i: (i, 0)), _full(g.shape), _full(w4.shape)],
        out_specs=out_specs, out_shape=out_shape, compiler_params=_cp("parallel"),
    )(x, g, w4)


def zero_rows(buf, rows, name, tm=ROW_TILE):
    C = buf.shape[1]

    def body(b_ref, o_ref):
        o_ref[...] = jnp.zeros_like(o_ref)

    return pl.pallas_call(
        body, name=name, grid=(rows // tm,), in_specs=[ANY],
        out_specs=pl.BlockSpec((tm, C), lambda i: (i, 0)),
        out_shape=jax.ShapeDtypeStruct(buf.shape, buf.dtype), input_output_aliases={0: 0},
        compiler_params=_cp("parallel"),
    )(buf)


def mm_res(a, w4, res, name, tm=MM_TILE):
    M, K = a.shape
    N = w4.shape[2]
    tm = min(tm, M)

    def body(a_ref, w_ref, r_ref, o_ref):
        o_ref[...] = r_ref[...] + _dot_rows(a_ref[...], w_ref)

    return pl.pallas_call(
        body, name=name, grid=(M // tm,),
        in_specs=[pl.BlockSpec((tm, K), lambda i: (i, 0)), _full(w4.shape), pl.BlockSpec((tm, N), lambda i: (i, 0))],
        out_specs=pl.BlockSpec((tm, N), lambda i: (i, 0)),
        out_shape=jax.ShapeDtypeStruct((M, N), F32), compiler_params=_cp("parallel"),
    )(a, w4, res)


def mm_cols(a, w, layer, name, out_dtype):
    M = a.shape[0]
    n = w.shape[3]

    def body(a_ref, w_ref, o_ref):
        o_ref[...] = _dot_cols(a_ref[...], w_ref).astype(out_dtype)

    return pl.pallas_call(
        body, name=name, grid=(1,), in_specs=[_full(a.shape), _wspec(w, layer)],
        out_specs=_full((M, N_CHIPS * n)), out_shape=jax.ShapeDtypeStruct((M, N_CHIPS * n), out_dtype),
        compiler_params=_cp("arbitrary"),
    )(a, w)


def mm_nt_cols(dy, w, layer, name):
    M = dy.shape[0]
    K, n = w.shape[2], w.shape[3]

    def body(d_ref, w_ref, o_ref):
        o_ref[...] = _dot_nt_cols(_col_pieces(d_ref[...].astype(BF16), n), w_ref)

    return pl.pallas_call(
        body, name=name, grid=(1,), in_specs=[_full(dy.shape), _wspec(w, layer)],
        out_specs=_full((M, K)), out_shape=jax.ShapeDtypeStruct((M, K), F32), compiler_params=_cp("arbitrary"),
    )(dy, w)


def mm_nt_rows(dy, w4, name, tm=MM_TILE):
    M, N = dy.shape
    K = N_CHIPS * w4.shape[1]
    tm = min(tm, M)

    def body(d_ref, w_ref, o_ref):
        o_ref[...] = _dot_nt_rows(d_ref[...].astype(BF16), w_ref)

    return pl.pallas_call(
        body, name=name, grid=(M // tm,),
        in_specs=[pl.BlockSpec((tm, N), lambda i: (i, 0)), _full(w4.shape)],
        out_specs=pl.BlockSpec((tm, K), lambda i: (i, 0)),
        out_shape=jax.ShapeDtypeStruct((M, K), F32), compiler_params=_cp("parallel"),
    )(dy, w4)


def mm_nt_normbwd(dys, offs, w4, x, g, dx_out, name, tm=MM_TILE):
    M, D = x.shape
    n = w4.shape[2]
    tm = min(tm, M)
    nd = len(dys)

    def body(*refs):
        d_refs = refs[:nd]
        w_ref, x_ref, g_ref, dxo_ref, dx_ref, dg_ref = refs[nd:]
        if nd == 1:
            pieces = _col_pieces(d_refs[0][...].astype(BF16), n)
        else:
            pieces = [r[...].astype(BF16) for r in d_refs]
        dh = _dot_nt_cols(pieces, w_ref)
        dx, dg = _rms_bwd(x_ref[...], g_ref[...], dh)
        dx_ref[...] = dxo_ref[...] + dx

        @pl.when(pl.program_id(0) == 0)
        def _():
            dg_ref[...] = jnp.zeros_like(dg_ref)

        dg_ref[...] += dg

    row = lambda c, off=0: pl.BlockSpec((tm, c), lambda i, p=off // tm: (i + p, 0))
    return pl.pallas_call(
        body, name=name, grid=(M // tm,),
        in_specs=[row(d.shape[1], off) for d, off in zip(dys, offs)] + [_full(w4.shape), row(D), _full(g.shape), row(D)],
        out_specs=[row(D), _full((1, D))],
        out_shape=[jax.ShapeDtypeStruct((M, D), F32), jax.ShapeDtypeStruct((1, D), F32)],
        compiler_params=_cp("arbitrary"),
    )(*dys, w4, x, g, dx_out)


def mm_tn(a, b, name, layout, into=None, b_off=0, out_dtype=BF16, bm=1024, bn=1280, bl=1024):
    L, K = a.shape
    N = b.shape[1]
    kind = layout[0]
    arg = layout[1] if len(layout) > 1 else None
    bm, bn, bl = _lane_tile(K, bm), _lane_tile(N, bn), min(bl, L)
    assert L % bl == 0 and b_off % bl == 0, (L, bl, b_off)
    nl = L // bl
    n_sh, r_sh = N // N_CHIPS, K // N_CHIPS
    lay = (None,) if arg is None else (None, None)
    mid = () if arg is None else (arg,)
    gs = 1
    if kind == "plain":
        oshape, oblock, oidx = (K, N), (bm, bn), lambda i, j, l: (i, j)
    elif kind == "slab":
        oshape, oblock, oidx = (N_CHIPS, K, N), (None, bm, bn), lambda i, j, l: (arg, i, j)
    elif kind == "cols":
        bn = max(bn - bn % n_sh, n_sh) if bn >= n_sh else _lane_tile(n_sh, bn)
        gs = max(bn // n_sh, 1)
        per = n_sh // bn if gs == 1 else 1
        oshape = (N_CHIPS,) + ((2,) if arg is not None else ()) + (K, n_sh)
        oblock = ((gs,) if gs > 1 else (None,)) + lay[1:] + (bm, min(bn, n_sh))
        oidx = lambda i, j, l: (j // per,) + mid + (i, j % per)
    else:
        bm = max(bm - bm % r_sh, r_sh) if bm >= r_sh else _lane_tile(r_sh, bm)
        gs = max(bm // r_sh, 1)
        per = r_sh // bm if gs == 1 else 1
        oshape = (N_CHIPS,) + ((2,) if arg is not None else ()) + (r_sh, N)
        oblock = ((gs,) if gs > 1 else (None,)) + lay[1:] + (min(bm, r_sh), bn)
        oidx = lambda i, j, l: (i // per,) + mid + (i % per, j)
    assert K % bm == 0 and N % bn == 0, (K, bm, N, bn)

    def body(a_ref, b_ref, *rest):
        o_ref, acc = rest[-2], rest[-1]
        l = pl.program_id(2)

        @pl.when(l == 0)
        def _():
            acc[...] = jnp.zeros_like(acc)

        acc[...] += _dot_tn(a_ref[...].astype(BF16), b_ref[...].astype(BF16))

        @pl.when(l == nl - 1)
        def _():
            if gs == 1:
                o_ref[...] = acc[...].astype(out_dtype)
            elif kind == "cols":
                for t in range(gs):
                    o_ref[t] = acc[:, t * n_sh:(t + 1) * n_sh].astype(out_dtype)
            else:
                for t in range(gs):
                    o_ref[t] = acc[t * r_sh:(t + 1) * r_sh, :].astype(out_dtype)

    in_specs = [pl.BlockSpec((bl, bm), lambda i, j, l: (l, i)),
                pl.BlockSpec((bl, bn), lambda i, j, l, p=b_off // bl: (l + p, j))]
    args = [a, b]
    alias = {}
    if into is not None:
        in_specs.append(ANY)
        args.append(into)
        alias = {2: 0}
    return pl.pallas_call(
        body, name=name, grid=(K // bm, N // bn, nl), in_specs=in_specs,
        out_specs=pl.BlockSpec(oblock, oidx), out_shape=jax.ShapeDtypeStruct(oshape, out_dtype),
        scratch_shapes=[pltpu.VMEM((bm, bn), F32)], input_output_aliases=alias,
        compiler_params=_cp("parallel", "parallel", "arbitrary"),
    )(*args)


def rms_fwd(x, g, name):
    def body(x_ref, g_ref, ob_ref):
        _, xhat = _rms_parts(x_ref[...])
        ob_ref[...] = (xhat * g_ref[...]).astype(BF16)

    return pl.pallas_call(body, name=name, out_shape=jax.ShapeDtypeStruct(x.shape, BF16))(x, g)


def rms_dgain(x, dy0, dy1, name):
    def body(x_ref, d0_ref, d1_ref, o_ref):
        _, xhat = _rms_parts(x_ref[...])
        o_ref[...] = jnp.sum((d0_ref[...] + d1_ref[...]) * xhat, axis=0, keepdims=True)

    return pl.pallas_call(body, name=name, out_shape=jax.ShapeDtypeStruct((1, x.shape[1]), F32))(x, dy0, dy1)


def _s5_discretise(lr, li, logdt, bt_re, bt_im):
    dt = jnp.exp(logdt)
    mag = jnp.exp(lr * dt)
    ab_re = mag * jnp.cos(li * dt)
    ab_im = mag * jnp.sin(li * dt)
    den = lr * lr + li * li
    nr = ab_re - 1.0
    coef_re = (nr * lr + ab_im * li) / den
    coef_im = (ab_im * lr - nr * li) / den
    cr = coef_re[:, None, :]
    ci = coef_im[:, None, :]
    bb_re = cr * bt_re - ci * bt_im
    bb_im = cr * bt_im + ci * bt_re
    return ab_re, ab_im, bb_re, bb_im


def s5_param_fwd(lr, li, logdt, bt_re, bt_im):
    def body(lr_ref, li_ref, ld_ref, br_ref, bi_ref, bbr_ref, bbi_ref):
        _, _, bb_re, bb_im = _s5_discretise(lr_ref[...], li_ref[...], ld_ref[...], br_ref[...], bi_ref[...])
        bbr_ref[...] = bb_re
        bbi_ref[...] = bb_im

    sh = jax.ShapeDtypeStruct(bt_re.shape, F32)
    return pl.pallas_call(body, name="s5_param_fwd", out_shape=[sh, sh])(lr, li, logdt, bt_re, bt_im)


def s5_param_bwd(lr, li, logdt, bt_re, bt_im, d_ab_re, d_ab_im, d_bb_re, d_bb_im):
    def body(lr_ref, li_ref, ld_ref, br_ref, bi_ref, dar_ref, dai_ref, dbr_ref, dbi_ref,
             o_lr, o_li, o_ld, o_br, o_bi):
        _, vjp = jax.vjp(_s5_discretise, lr_ref[...], li_ref[...], ld_ref[...], br_ref[...], bi_ref[...])
        g = vjp((dar_ref[...], dai_ref[...], dbr_ref[...], dbi_ref[...]))
        for o, v in zip((o_lr, o_li, o_ld), g[:3]):
            o[...] = v
        for o, v in zip((o_br, o_bi), g[3:]):
            for c in range(S5_GROUP):
                o[:, c * S5_STATE:(c + 1) * S5_STATE] = v[:, c, :]

    dense = jax.ShapeDtypeStruct((S5_GROUPS, S5_GROUP * S5_STATE), F32)
    shapes = [jax.ShapeDtypeStruct(a.shape, F32) for a in (lr, li, logdt)] + [dense, dense]
    return pl.pallas_call(body, name="s5_param_bwd", out_shape=shapes)(
        lr, li, logdt, bt_re, bt_im, d_ab_re, d_ab_im, d_bb_re, d_bb_im)


def s5_tables(lr_flat, li_flat, logdt_flat):
    def body(lr_ref, li_ref, ld_ref, tab_ref):
        dt = jnp.exp(ld_ref[...])
        a = lr_ref[...] * dt
        th = li_ref[...] * dt
        row = lax.broadcasted_iota(jnp.int32, (8, 1), 0)
        rowf = row.astype(F32)

        def power(e, sign):
            m = jnp.exp(e * a)
            return m * jnp.cos(e * th), sign * m * jnp.sin(e * th)

        k = 0
        for sign, fwd in ((1.0, True), (-1.0, False)):
            for s in (1, 2, 4):
                pr, pi = power(jnp.full((8, 1), float(s), F32), sign)
                keep = (row >= s) if fwd else (row + s < 8)
                tab_ref[k] = jnp.where(keep, pr, 0.0)
                tab_ref[k + 1] = jnp.where(keep, pi, 0.0)
                k += 2
            e = rowf + 1.0 if fwd else 8.0 - rowf
            pr, pi = power(e, sign)
            tab_ref[k] = pr
            tab_ref[k + 1] = pi
            k += 2

    return pl.pallas_call(body, name="s5_tables",
                          out_shape=jax.ShapeDtypeStruct((16, 8, S5_COLS), F32))(lr_flat, li_flat, logdt_flat)


def _scan_block(a, b, tabs, base, cr, ci, reverse):
    for n, s in enumerate((1, 2, 4)):
        mr = tabs[base + 2 * n]
        mi = tabs[base + 2 * n + 1]
        sh = (8 - s) if reverse else s
        ar = pltpu.roll(a, sh, 0)
        br = pltpu.roll(b, sh, 0)
        a, b = a + mr * ar - mi * br, b + mr * br + mi * ar
    pr = tabs[base + 6]
    pi = tabs[base + 7]
    a, b = a + pr * cr - pi * ci, b + pr * ci + pi * cr
    return a, b


class Carried:
    def __init__(self, arrays, out_shapes, sems, start, middle, finish):
        self.arrays, self.out_shapes, self.sems = list(arrays), list(out_shapes), list(sems)
        self.start, self.middle, self.finish = start, middle, finish

    def split(self, refs, n_in, n_out, n_scratch):
        a, o, s = len(self.arrays), len(self.out_shapes), len(self.sems)
        own_in, car_in = refs[:n_in], refs[n_in:n_in + a]
        own_out, car_out = refs[n_in + a:n_in + a + n_out], refs[n_in + a + n_out:n_in + a + n_out + o]
        rest = refs[n_in + a + n_out + o:]
        return own_in + own_out + rest[:n_scratch], (car_in, car_out, rest[n_scratch:n_scratch + s])

    def hooks(self, parts, n_chunks, nt):
        if n_chunks is None:
            t = pl.program_id(0)
            first, last, mid = t == 0, t == nt - 1, None
        else:
            j, t = pl.program_id(0), pl.program_id(1)
            first, last = (j == 0) & (t == 0), (j == n_chunks - 1) & (t == nt - 1)
            mid = (j == n_chunks // 2) & (t == 0)

        def top():
            pl.when(first)(lambda: self.start(*parts))
            if self.middle is not None and mid is not None:
                pl.when(mid)(lambda: self.middle(*parts))

        def end():
            pl.when(last)(lambda: self.finish(*parts))

        return top, end


def s5_fwd(z, bbd_re, bbd_im, ccd_re, ccd_im, tab, dskip, tm=S5_TILE, carried=None):
    L = z.shape[0]
    tm = min(tm, L)
    nt = L // tm

    def body(*refs):
        top = end = None
        if carried is not None:
            refs, parts = carried.split(refs, 7, 4, 3)
            top, end = carried.hooks(parts, S5_SPLIT, nt)
            top()
        u_ref, bbr_ref, bbi_ref, ccr_ref, cci_ref, tab_ref, d_ref, y_ref, ck_ref, hr_ref, hi_ref, xr, xi, car = refs
        t = pl.program_id(1)

        @pl.when(t == 0)
        def _():
            car[...] = jnp.zeros_like(car)

        u = u_ref[...]
        ub = u.astype(BF16)
        xr[...] = _dot(ub, bbr_ref[...])
        xi[...] = _dot(ub, bbi_ref[...])
        tabs = [tab_ref[k] for k in range(8)]

        def blk(i, c):
            r0 = pl.multiple_of(i * 8, 8)
            a, b = _scan_block(xr[pl.ds(r0, 8), :], xi[pl.ds(r0, 8), :], tabs, 0, c[0], c[1], False)
            xr[pl.ds(r0, 8), :] = a
            xi[pl.ds(r0, 8), :] = b
            return a[7:8, :], b[7:8, :]

        cr, ci = lax.fori_loop(0, tm // 8, blk, (car[0:1, :], car[1:2, :]))
        car[0:1, :] = cr
        car[1:2, :] = ci
        ck_ref[0:1, :] = cr
        ck_ref[1:2, :] = ci
        hrb = xr[...].astype(BF16)
        hib = xi[...].astype(BF16)
        hr_ref[...] = hrb
        hi_ref[...] = hib
        y_ref[...] = _dot(hrb, ccr_ref[...]) - _dot(hib, cci_ref[...]) + d_ref[...] * u
        if end is not None:
            end()

    extra = carried.arrays if carried is not None else []
    extra_out = carried.out_shapes if carried is not None else []
    extra_sems = carried.sems if carried is not None else []
    return pl.pallas_call(
        body, name="s5_fwd", grid=(S5_SPLIT, nt),
        in_specs=[pl.BlockSpec((tm, S5_UC), lambda j, t: (t, j)),
                  pl.BlockSpec((None, S5_UC, S5_CC), lambda j, t: (j, 0, 0)),
                  pl.BlockSpec((None, S5_UC, S5_CC), lambda j, t: (j, 0, 0)),
                  pl.BlockSpec((None, S5_CC, S5_UC), lambda j, t: (j, 0, 0)),
                  pl.BlockSpec((None, S5_CC, S5_UC), lambda j, t: (j, 0, 0)),
                  pl.BlockSpec((8, 8, S5_CC), lambda j, t: (0, 0, j)),
                  pl.BlockSpec((1, S5_UC), lambda j, t: (0, j))] + [ANY] * len(extra),
        out_specs=[pl.BlockSpec((tm, S5_UC), lambda j, t: (t, j)),
                   pl.BlockSpec((None, 2, S5_CC), lambda j, t: (t, 0, j)),
                   pl.BlockSpec((tm, S5_CC), lambda j, t: (t, j)),
                   pl.BlockSpec((tm, S5_CC), lambda j, t: (t, j))] + [ANY] * len(extra_out),
        out_shape=[jax.ShapeDtypeStruct((L, S5_WIDTH), F32), jax.ShapeDtypeStruct((nt, 2, S5_COLS), F32),
                   jax.ShapeDtypeStruct((L, S5_COLS), BF16), jax.ShapeDtypeStruct((L, S5_COLS), BF16)] + extra_out,
        scratch_shapes=[pltpu.VMEM((tm, S5_CC), F32), pltpu.VMEM((tm, S5_CC), F32), pltpu.VMEM((2, S5_CC), F32)]
        + extra_sems,
        compiler_params=_cp("arbitrary" if carried is not None else "parallel", "arbitrary"),
    )(z, bbd_re, bbd_im, ccd_re, ccd_im, tab, dskip, *extra)


def s5_bwd(z, dy, dz, ckpt, hrb, hib, bbd_re, bbd_im, ccd_re, ccd_im, tab, dskip, tm=S5_TILE, carried=None):
    L = z.shape[0]
    tm = min(tm, L)
    nt = L // tm

    def body(*refs):
        top = end = None
        if carried is not None:
            refs, parts = carried.split(refs, 12, 7, 7)
            top, end = carried.hooks(parts, S5_SPLIT, nt)
            top()
        (u_ref, dy_ref, dz_ref, ck_ref, hrb_ref, hib_ref, bbr_ref, bbi_ref, ccr_ref, cci_ref, tab_ref, d_ref,
         du_ref, da_ref, dbr_ref, dbi_ref, dcr_ref, dci_ref, dd_ref, hr, hi, gr, gi, car, acr, aci) = refs
        t = pl.program_id(1)
        tt = nt - 1 - t

        @pl.when(t == 0)
        def _():
            for r in (car, acr, aci, dbr_ref, dbi_ref, dcr_ref, dci_ref, dd_ref):
                r[...] = jnp.zeros_like(r)

        u = u_ref[...]
        ub = u.astype(BF16)
        dyv = dy_ref[...]
        dyb = dyv.astype(BF16)
        tabs = [None] * 8 + [tab_ref[k] for k in range(8, 16)]

        live = (tt > 0).astype(F32)
        hr[0:8, :] = jnp.broadcast_to(ck_ref[0:1, :] * live, (8, S5_CC))
        hi[0:8, :] = jnp.broadcast_to(ck_ref[1:2, :] * live, (8, S5_CC))
        hrb = hrb_ref[...]
        hib = hib_ref[...]
        hr[8:, :] = hrb.astype(F32)
        hi[8:, :] = hib.astype(F32)
        dcr_ref[...] += _dot_tn(hrb, dyb)
        dci_ref[...] -= _dot_tn(hib, dyb)

        gr[...] = _dot_nt(dyb, ccr_ref[...])
        gi[...] = -_dot_nt(dyb, cci_ref[...])
        row0 = lax.broadcasted_iota(jnp.int32, (8, S5_CC), 0) == 0

        def rblk(k, c):
            i = tm // 8 - 1 - k
            r0 = pl.multiple_of(i * 8, 8)
            a, b = _scan_block(gr[pl.ds(r0, 8), :], gi[pl.ds(r0, 8), :], tabs, 8, c[0], c[1], True)
            gr[pl.ds(r0, 8), :] = a
            gi[pl.ds(r0, 8), :] = b
            r1 = pl.multiple_of(i * 8 + 8, 8)
            hpr = jnp.where(row0, pltpu.roll(hr[pl.ds(r0, 8), :], 1, 0), pltpu.roll(hr[pl.ds(r1, 8), :], 1, 0))
            hpi = jnp.where(row0, pltpu.roll(hi[pl.ds(r0, 8), :], 1, 0), pltpu.roll(hi[pl.ds(r1, 8), :], 1, 0))
            acr[...] += a * hpr + b * hpi
            aci[...] += b * hpr - a * hpi
            return a[0:1, :], b[0:1, :]

        cr, ci = lax.fori_loop(0, tm // 8, rblk, (car[0:1, :], car[1:2, :]))
        car[0:1, :] = cr
        car[1:2, :] = ci

        grb = gr[...].astype(BF16)
        gib = gi[...].astype(BF16)
        du_ref[...] = (_dot_nt(grb, bbr_ref[...]) + _dot_nt(gib, bbi_ref[...]) + d_ref[...] * dyv).astype(BF16)
        dbr_ref[...] += _dot_tn(ub, grb)
        dbi_ref[...] += _dot_tn(ub, gib)
        dd_ref[...] += jnp.sum(dyv * u, axis=0, keepdims=True)

        @pl.when(t == nt - 1)
        def _():
            da_ref[0:1, :] = jnp.sum(acr[...], axis=0, keepdims=True)
            da_ref[1:2, :] = jnp.sum(aci[...], axis=0, keepdims=True)

        if end is not None:
            end()

    extra = carried.arrays if carried is not None else []
    extra_out = carried.out_shapes if carried is not None else []
    extra_sems = carried.sems if carried is not None else []
    chunk = lambda a, b: pl.BlockSpec((None, a, b), lambda j, t: (j, 0, 0))
    return pl.pallas_call(
        body, name="s5_bwd", grid=(S5_SPLIT, nt),
        in_specs=[pl.BlockSpec((tm, S5_UC), lambda j, t: (nt - 1 - t, j)),
                  pl.BlockSpec((tm, S5_UC), lambda j, t: (nt - 1 - t, j)),
                  ANY,
                  pl.BlockSpec((None, 2, S5_CC), lambda j, t: (jnp.maximum(nt - 2 - t, 0), 0, j)),
                  pl.BlockSpec((tm, S5_CC), lambda j, t: (nt - 1 - t, j)),
                  pl.BlockSpec((tm, S5_CC), lambda j, t: (nt - 1 - t, j)),
                  chunk(S5_UC, S5_CC), chunk(S5_UC, S5_CC), chunk(S5_CC, S5_UC), chunk(S5_CC, S5_UC),
                  pl.BlockSpec((16, 8, S5_CC), lambda j, t: (0, 0, j)),
                  pl.BlockSpec((1, S5_UC), lambda j, t: (0, j))] + [ANY] * len(extra),
        out_specs=[pl.BlockSpec((tm, S5_UC), lambda j, t: (nt - 1 - t, j)),
                   pl.BlockSpec((None, 2, S5_CC), lambda j, t: (j, 0, 0)),
                   chunk(S5_UC, S5_CC), chunk(S5_UC, S5_CC), chunk(S5_CC, S5_UC), chunk(S5_CC, S5_UC),
                   pl.BlockSpec((1, S5_UC), lambda j, t: (0, j))] + [ANY] * len(extra_out),
        out_shape=[jax.ShapeDtypeStruct(dz.shape, dz.dtype),
                   jax.ShapeDtypeStruct((S5_SPLIT, 2, S5_CC), F32),
                   jax.ShapeDtypeStruct((S5_SPLIT, S5_UC, S5_CC), F32),
                   jax.ShapeDtypeStruct((S5_SPLIT, S5_UC, S5_CC), F32),
                   jax.ShapeDtypeStruct((S5_SPLIT, S5_CC, S5_UC), F32),
                   jax.ShapeDtypeStruct((S5_SPLIT, S5_CC, S5_UC), F32),
                   jax.ShapeDtypeStruct((1, S5_WIDTH), F32)] + extra_out,
        scratch_shapes=[pltpu.VMEM((tm + 8, S5_CC), F32), pltpu.VMEM((tm + 8, S5_CC), F32),
                        pltpu.VMEM((tm, S5_CC), F32), pltpu.VMEM((tm, S5_CC), F32),
                        pltpu.VMEM((2, S5_CC), F32), pltpu.VMEM((8, S5_CC), F32), pltpu.VMEM((8, S5_CC), F32)]
        + extra_sems,
        input_output_aliases={2: 0},
        compiler_params=_cp("arbitrary" if carried is not None else "parallel", "arbitrary"),
    )(z, dy, dz, ckpt, hrb, hib, bbd_re, bbd_im, ccd_re, ccd_im, tab, dskip, *extra)


_EYE8 = np.eye(S5_GROUPS // S5_SPLIT, dtype=np.float32)


def _blockdiag(a):
    g, r, c = a.shape
    a = a.reshape(S5_SPLIT, g // S5_SPLIT, r, c)
    out = a[:, :, :, None, :] * _EYE8[None, :, None, :, None].astype(a.dtype)
    return out.reshape(S5_SPLIT, (g // S5_SPLIT) * r, (g // S5_SPLIT) * c)


def _blockdiag_extract(a, r, c):
    n = S5_GROUPS // S5_SPLIT
    a = a.reshape(S5_SPLIT, n, r, n, c)
    d = jnp.stack([a[:, k, :, k, :] for k in range(n)], axis=1)
    return d.reshape(S5_GROUPS, r, c)


def s5_mixer_core_fwd(z, lam_re, lam_im, log_dt, b_re, b_im, c_re, c_im, d_skip, carried=None):
    bt_re = jnp.swapaxes(b_re, 1, 2)
    bt_im = jnp.swapaxes(b_im, 1, 2)
    logdt = log_dt.reshape(S5_GROUPS, 1)
    bb_re, bb_im = s5_param_fwd(lam_re, lam_im, logdt, bt_re, bt_im)
    flat = lambda a: a.reshape(1, S5_COLS)
    tab = s5_tables(flat(lam_re), flat(lam_im), flat(jnp.broadcast_to(logdt, (S5_GROUPS, S5_STATE))))
    bbd_re = _blockdiag(bb_re).astype(BF16)
    bbd_im = _blockdiag(bb_im).astype(BF16)
    ccd_re = _blockdiag(jnp.swapaxes(c_re, 1, 2)).astype(BF16)
    ccd_im = _blockdiag(jnp.swapaxes(c_im, 1, 2)).astype(BF16)
    dsk = d_skip.reshape(1, S5_WIDTH)
    y, ckpt, hrb, hib, *landed = s5_fwd(z, bbd_re, bbd_im, ccd_re, ccd_im, tab, dsk, carried=carried)
    saved = (logdt, bt_re, bt_im, bbd_re, bbd_im, ccd_re, ccd_im, tab, dsk, ckpt, hrb, hib)
    return y, saved, landed


def s5_b_from_dense(dense):
    return jnp.swapaxes(dense.reshape(S5_GROUPS, S5_GROUP, S5_STATE), 1, 2)


def s5_mixer_core_bwd(z, dy, dz, lam_re, lam_im, saved, carried=None):
    logdt, bt_re, bt_im, bbd_re, bbd_im, ccd_re, ccd_im, tab, dsk, ckpt, hrb, hib = saved
    dz, da, dbr, dbi, dcr, dci, dd, *landed = s5_bwd(z, dy, dz, ckpt, hrb, hib, bbd_re, bbd_im, ccd_re, ccd_im, tab,
                                                     dsk, carried=carried)
    d_ab_re = da[:, 0, :].reshape(S5_GROUPS, S5_STATE)
    d_ab_im = da[:, 1, :].reshape(S5_GROUPS, S5_STATE)
    d_bb_re = _blockdiag_extract(dbr, S5_GROUP, S5_STATE)
    d_bb_im = _blockdiag_extract(dbi, S5_GROUP, S5_STATE)
    g_lr, g_li, g_ld, g_btr, g_bti = s5_param_bwd(lam_re, lam_im, logdt, bt_re, bt_im,
                                                  d_ab_re, d_ab_im, d_bb_re, d_bb_im)
    g_cre = jnp.swapaxes(_blockdiag_extract(dcr, S5_STATE, S5_GROUP), 1, 2)
    g_cim = jnp.swapaxes(_blockdiag_extract(dci, S5_STATE, S5_GROUP), 1, 2)
    grads = dict(lambda_re=g_lr, lambda_im=g_li, log_dt=g_ld.reshape(S5_GROUPS), b_re=g_btr, b_im=g_bti,
                 c_re=g_cre, c_im=g_cim, d=dd.reshape(S5_WIDTH))
    return dz, grads, landed


Z_U, Z_GA, Z_VAL, Z_GLU, Z_GB = range(5)
SUBLANES = 8


def _shifted_copies(buf, tm):
    n = tm + CONV_HALO - SUBLANES
    for r in range(1, SUBLANES):
        buf[r, 0:n, :] = buf[0, pl.ds(r, n), :]


CONV_ROWS = 32


def _shifted_rows(buf, start, rows, base=0):
    return buf[start % SUBLANES, pl.ds(base + (start - start % SUBLANES), rows), :]


def conv_fwd(z, conv_w, conv_b, tm=ROW_TILE):
    L = z.shape[0]
    tm = min(tm, L)
    nt = L // tm
    hb = tm // CONV_HALO
    C = CONV_WIDTH

    def body(val_ref, glu_ref, valh_ref, gluh_ref, w_ref, b_ref, c_ref, vsh):
        live = (pl.program_id(0) > 0).astype(F32)
        vsh[0, 0:CONV_HALO, :] = valh_ref[...] * _sigmoid(gluh_ref[...]) * live
        vsh[0, CONV_HALO:, :] = val_ref[...] * _sigmoid(glu_ref[...])
        _shifted_copies(vsh, tm)

        def rows(i, carry):
            base = pl.multiple_of(i * CONV_ROWS, CONV_ROWS)
            acc = jnp.broadcast_to(b_ref[...], (CONV_ROWS, C))
            for k in range(CONV_KERNEL):
                acc = acc + w_ref[k:k + 1, :] * _shifted_rows(vsh, CONV_HALO - CONV_KERNEL + 1 + k, CONV_ROWS, base)
            c_ref[pl.ds(base, CONV_ROWS), :] = acc
            return carry

        lax.fori_loop(0, tm // CONV_ROWS, rows, 0)

    cur = lambda col: pl.BlockSpec((tm, C), lambda t: (t, col))
    prev = lambda col: pl.BlockSpec((CONV_HALO, C), lambda t: (jnp.maximum(t * hb - 1, 0), col))
    return pl.pallas_call(
        body, name="conv_fwd", grid=(nt,),
        in_specs=[cur(Z_VAL), cur(Z_GLU), prev(Z_VAL), prev(Z_GLU), _full(conv_w.shape), _full(conv_b.shape)],
        out_specs=pl.BlockSpec((tm, C), lambda t: (t, 0)),
        out_shape=jax.ShapeDtypeStruct((L, C), F32),
        scratch_shapes=[pltpu.VMEM((8, tm + CONV_HALO, C), F32)],
        compiler_params=_cp("parallel"),
    )(z, z, z, z, conv_w, conv_b)


def conv_bwd(z, dc, dz, conv_w, tm=ROW_TILE, carried=None):
    L = z.shape[0]
    tm = min(tm, L)
    nt = L // tm
    hb = tm // CONV_HALO
    nh = L // CONV_HALO
    C = CONV_WIDTH
    off = CONV_HALO - CONV_KERNEL + 1

    def body(*refs):
        top = end = None
        if carried is not None:
            refs, parts = carried.split(refs, 8, 3, 3)
            top, end = carried.hooks(parts, None, nt)
            top()
        val_ref, glu_ref, valh_ref, gluh_ref, dc_ref, dcn_ref, dz_ref, w_ref, dvg_ref, dw_ref, db_ref, vsh, dsh, wacc = refs
        t = pl.program_id(0)

        @pl.when(t == 0)
        def _():
            wacc[...] = jnp.zeros_like(wacc)
            db_ref[...] = jnp.zeros_like(db_ref)

        val = val_ref[...]
        sg = _sigmoid(glu_ref[...])
        vsh[0, 0:CONV_HALO, :] = valh_ref[...] * _sigmoid(gluh_ref[...]) * (t > 0).astype(F32)
        vsh[0, CONV_HALO:, :] = val * sg
        dcv = dc_ref[...]
        dsh[0, 0:tm, :] = dcv
        dsh[0, tm:, :] = dcn_ref[...] * (t < nt - 1).astype(F32)
        _shifted_copies(vsh, tm)
        _shifted_copies(dsh, tm)

        def rows(i, carry):
            base = pl.multiple_of(i * CONV_ROWS, CONV_ROWS)
            dcr = dc_ref[pl.ds(base, CONV_ROWS), :]
            dv = jnp.zeros((CONV_ROWS, C), F32)
            for k in range(CONV_KERNEL):
                dv = dv + w_ref[k:k + 1, :] * _shifted_rows(dsh, CONV_KERNEL - 1 - k, CONV_ROWS, base)
                prod = dcr * _shifted_rows(vsh, off + k, CONV_ROWS, base)
                wacc[k] += jnp.sum(prod.reshape(CONV_ROWS // SUBLANES, SUBLANES, C), axis=0)
            valr = val_ref[pl.ds(base, CONV_ROWS), :]
            sgr = _sigmoid(glu_ref[pl.ds(base, CONV_ROWS), :])
            dvg_ref[pl.ds(base, CONV_ROWS), 0:C] = (dv * sgr).astype(BF16)
            dvg_ref[pl.ds(base, CONV_ROWS), C:] = (dv * valr * sgr * (1.0 - sgr)).astype(BF16)
            return carry

        lax.fori_loop(0, tm // CONV_ROWS, rows, 0)
        db_ref[...] += jnp.sum(dcv, axis=0, keepdims=True)

        @pl.when(t == nt - 1)
        def _():
            dw_ref[...] = jnp.sum(wacc[...], axis=1)

        if end is not None:
            end()

    extra = carried.arrays if carried is not None else []
    extra_out = carried.out_shapes if carried is not None else []
    extra_sems = carried.sems if carried is not None else []
    cur = lambda col: pl.BlockSpec((tm, C), lambda t: (t, col))
    prev = lambda col: pl.BlockSpec((CONV_HALO, C), lambda t: (jnp.maximum(t * hb - 1, 0), col))
    nxt = pl.BlockSpec((CONV_HALO, C), lambda t: (jnp.minimum((t + 1) * hb, nh - 1), 0))
    row = pl.BlockSpec((tm, C), lambda t: (t, 0))
    return pl.pallas_call(
        body, name="conv_bwd", grid=(nt,),
        in_specs=[cur(Z_VAL), cur(Z_GLU), prev(Z_VAL), prev(Z_GLU), row, nxt, ANY, _full(conv_w.shape)]
        + [ANY] * len(extra),
        out_specs=[pl.BlockSpec((tm, 2 * C), lambda t: (t, 1)), _full((CONV_HALO, C)), _full((1, C))]
        + [ANY] * len(extra_out),
        out_shape=[jax.ShapeDtypeStruct(dz.shape, dz.dtype),
                   jax.ShapeDtypeStruct((CONV_HALO, C), F32), jax.ShapeDtypeStruct((1, C), F32)] + extra_out,
        scratch_shapes=[pltpu.VMEM((8, tm + CONV_HALO, C), F32), pltpu.VMEM((8, tm + CONV_HALO, C), F32),
                        pltpu.VMEM((CONV_HALO, SUBLANES, C), F32)] + extra_sems,
        input_output_aliases={6: 0},
        compiler_params=_cp("arbitrary"),
    )(z, z, z, z, dc, dc, dz, conv_w, *extra)


def _ln_parts(c):
    mu = jnp.mean(c, axis=-1, keepdims=True)
    cc = c - mu
    rstd = lax.rsqrt(jnp.mean(cc * cc, axis=-1, keepdims=True) + EPS)
    return rstd, cc * rstd


def _ev_tail_branches(ys, c, wglu, bglu, lng, lnb):
    z1 = _gelu(ys)
    z1b = z1.astype(BF16)
    sg = _sigmoid(_dot_rows(z1b, wglu) + bglu)
    out = z1 * sg
    rstd, chat = _ln_parts(c)
    cn = chat * lng + lnb
    return z1, z1b, sg, out, rstd, chat, cn


def ev_tail_fwd(ys, z, c, x0, wglu, bglu, lng, lnb, wout, tm=ROW_TILE):
    L, D = x0.shape
    tm = min(tm, L)
    W = S5_WIDTH

    def body(ys_ref, ga_ref, c_ref, gb_ref, x_ref, wglu_ref, bglu_ref, lng_ref, lnb_ref, wout_ref, o_ref):
        _, _, _, out, _, _, cn = _ev_tail_branches(ys_ref[...], c_ref[...], wglu_ref, bglu_ref[...],
                                                   lng_ref[...], lnb_ref[...])
        ya = (out * _silu(ga_ref[...])).astype(BF16)
        yb = (_silu(cn) * _silu(gb_ref[...])).astype(BF16)
        o_ref[...] = x_ref[...] + _dot_rows(jnp.concatenate([ya, yb], axis=1), wout_ref)

    row = lambda n, col=0: pl.BlockSpec((tm, n), lambda t: (t, col))
    return pl.pallas_call(
        body, name="ev_tail_fwd", grid=(L // tm,),
        in_specs=[row(W), row(W, Z_GA), row(W), row(W, Z_GB), row(D), _full(wglu.shape), _full(bglu.shape),
                  _full(lng.shape), _full(lnb.shape), _full(wout.shape)],
        out_specs=row(D), out_shape=jax.ShapeDtypeStruct((L, D), F32), compiler_params=_cp("parallel"),
    )(ys, z, c, z, x0, wglu, bglu, lng, lnb, wout)


def ev_tail_bwd(ys, z, c, dx1, wglu, bglu, lng, lnb, wout, tm=ROW_TILE):
    L, D = dx1.shape
    tm = min(tm, L)
    W = S5_WIDTH

    def body(ys_ref, ga_ref, c_ref, gb_ref, dx_ref, wglu_ref, bglu_ref, lng_ref, lnb_ref, wout_ref,
             dys_ref, dc_ref, dz_ref, r_ref, z1_ref, dt_ref, dbg_ref, dlg_ref, dlb_ref):
        @pl.when(pl.program_id(0) == 0)
        def _():
            for r in (dbg_ref, dlg_ref, dlb_ref):
                r[...] = jnp.zeros_like(r)

        ys, ga, gb = ys_ref[...], ga_ref[...], gb_ref[...]
        z1, z1b, sg, out, rstd, chat, cn = _ev_tail_branches(ys, c_ref[...], wglu_ref, bglu_ref[...],
                                                             lng_ref[...], lnb_ref[...])
        (sga, dsga), (sgb, dsgb), (scn, dscn) = _silu_pair(ga), _silu_pair(gb), _silu_pair(cn)
        r_ref[:, 0:W] = (out * sga).astype(BF16)
        r_ref[:, W:] = (scn * sgb).astype(BF16)
        dr = _dot_nt_rows(dx_ref[...].astype(BF16), wout_ref)
        dra, drb = dr[:, 0:W], dr[:, W:]
        dz_ref[...] = jnp.zeros_like(dz_ref)
        dz_ref[:, Z_GA * W:(Z_GA + 1) * W] = (dra * out * dsga).astype(BF16)
        dout = dra * sga
        dt = dout * z1 * sg * (1.0 - sg)
        dtb = dt.astype(BF16)
        dz1 = dout * sg + _dot_nt_rows(dtb, wglu_ref)
        dys_ref[...] = dz1 * _dgelu(ys)
        z1_ref[...] = z1b
        dt_ref[...] = dtb
        dbg_ref[...] += jnp.sum(dt, axis=0, keepdims=True)
        dz_ref[:, Z_GB * W:(Z_GB + 1) * W] = (drb * scn * dsgb).astype(BF16)
        dcn = drb * sgb * dscn
        dlg_ref[...] += jnp.sum(dcn * chat, axis=0, keepdims=True)
        dlb_ref[...] += jnp.sum(dcn, axis=0, keepdims=True)
        dch = dcn * lng_ref[...]
        dc_ref[...] = rstd * (dch - jnp.mean(dch, axis=-1, keepdims=True)
                              - chat * jnp.mean(dch * chat, axis=-1, keepdims=True))

    row = lambda n, col=0: pl.BlockSpec((tm, n), lambda t: (t, col))
    f = lambda n, dt: jax.ShapeDtypeStruct((L, n), dt)
    vec = jax.ShapeDtypeStruct((1, W), F32)
    return pl.pallas_call(
        body, name="ev_tail_bwd", grid=(L // tm,),
        in_specs=[row(W), row(W, Z_GA), row(W), row(W, Z_GB), row(D), _full(wglu.shape), _full(bglu.shape),
                  _full(lng.shape), _full(lnb.shape), _full(wout.shape)],
        out_specs=[row(W), row(W), row(EVEN_IN), row(D), row(W), row(W), _full((1, W)), _full((1, W)), _full((1, W))],
        out_shape=[f(W, F32), f(W, F32), f(EVEN_IN, BF16), f(D, BF16), f(W, BF16), f(W, BF16), vec, vec, vec],
        compiler_params=_cp("arbitrary"),
    )(ys, z, c, z, dx1, wglu, bglu, lng, lnb, wout)


XA_SCALE = XA_HEAD_DIM ** -0.5


def _xa_forward(xv, g, wqg, kv):
    D = D_MODEL
    _, xhat = _rms_parts(xv)
    hb = (xhat * g).astype(BF16)
    qb = (_dot_cols(hb, wqg, (0, 1)) * XA_SCALE).astype(BF16)
    gate = _dot_cols(hb, wqg, (2, 3))
    ps, os_ = [], []
    for h in range(XA_HEADS):
        lo, hi = h * XA_HEAD_DIM, (h + 1) * XA_HEAD_DIM
        s = _dot_nt(qb[:, lo:hi], kv[:, lo:hi])
        e = jnp.exp(s - jnp.max(s, axis=-1, keepdims=True))
        inv = 1.0 / jnp.sum(e, axis=-1, keepdims=True)
        ps.append((e, inv))
        os_.append(_dot(e.astype(BF16), kv[:, D + lo:D + hi]) * inv)
    return hb, qb, gate, ps, jnp.concatenate(os_, axis=1)


def xa_fwd(x, g, wqg, kv, wo, layer, name, tm=MM_TILE):
    L, D = x.shape
    tm = min(tm, L)

    def body(x_ref, g_ref, wqg_ref, kv_ref, wo_ref, o_ref):
        xv = x_ref[...]
        _, _, gate, _, o = _xa_forward(xv, g_ref[...], wqg_ref, kv_ref[...])
        o_ref[...] = xv + _dot_rows((o * _silu(gate)).astype(BF16), wo_ref)

    row = pl.BlockSpec((tm, D), lambda t: (t, 0))
    return pl.pallas_call(
        body, name=name, grid=(L // tm,),
        in_specs=[row, _full(g.shape), _wspec(wqg, layer), _full(kv.shape), _wspec(wo, layer)],
        out_specs=row, out_shape=jax.ShapeDtypeStruct((L, D), F32), compiler_params=_cp("parallel"),
    )(x, g, wqg, kv, wo)


def xa_bwd(x, dxo, g, wqg, kv, wo, layer, name, tm=MM_TILE):
    L, D = x.shape
    tm = min(tm, L)

    def body(x_ref, dxo_ref, g_ref, wqg_ref, kv_ref, wo_ref, dx_ref, dqg_ref, h_ref, r_ref, dkv_ref, dg_ref):
        @pl.when(pl.program_id(0) == 0)
        def _():
            dkv_ref[...] = jnp.zeros_like(dkv_ref)
            dg_ref[...] = jnp.zeros_like(dg_ref)

        xv = x_ref[...]
        kv = kv_ref[...]
        hb, qb, gate, ps, o = _xa_forward(xv, g_ref[...], wqg_ref, kv)
        sgate, dsgate = _silu_pair(gate)
        h_ref[...] = hb
        r_ref[...] = (o * sgate).astype(BF16)
        dxo = dxo_ref[...]
        dr = _dot_nt_rows(dxo.astype(BF16), wo_ref)
        do = dr * sgate
        dqg_ref[:, D:] = (dr * o * dsgate).astype(BF16)
        dob = do.astype(BF16)
        doo = do * o
        for h in range(XA_HEADS):
            lo, hi = h * XA_HEAD_DIM, (h + 1) * XA_HEAD_DIM
            e, inv = ps[h]
            dp = _dot_nt(dob[:, lo:hi], kv[:, D + lo:D + hi])
            dkv_ref[:, D + lo:D + hi] += _dot_tn(e.astype(BF16), (do[:, lo:hi] * inv).astype(BF16))
            rs = jnp.sum(doo[:, lo:hi], axis=-1, keepdims=True)
            dsb = (e * ((dp - rs) * inv)).astype(BF16)
            dqg_ref[:, lo:hi] = (_dot(dsb, kv[:, lo:hi]) * XA_SCALE).astype(BF16)
            dkv_ref[:, lo:hi] += _dot_tn(dsb, qb[:, lo:hi])
        dh = _dot_nt_cols(_col_pieces(dqg_ref[...], D // 2), wqg_ref)
        dx, dg = _rms_bwd(xv, g_ref[...], dh)
        dx_ref[...] = dxo + dx
        dg_ref[...] += dg

    row = lambda n: pl.BlockSpec((tm, n), lambda t: (t, 0))
    return pl.pallas_call(
        body, name=name, grid=(L // tm,),
        in_specs=[row(D), row(D), _full(g.shape), _wspec(wqg, layer), _full(kv.shape), _wspec(wo, layer)],
        out_specs=[row(D), row(2 * D), row(D), row(D), _full(kv.shape), _full((1, D))],
        out_shape=[jax.ShapeDtypeStruct((L, D), F32), jax.ShapeDtypeStruct((L, 2 * D), BF16),
                   jax.ShapeDtypeStruct((L, D), BF16), jax.ShapeDtypeStruct((L, D), BF16),
                   jax.ShapeDtypeStruct(kv.shape, F32), jax.ShapeDtypeStruct((1, D), F32)],
        compiler_params=_cp("arbitrary"),
    )(x, dxo, g, wqg, kv, wo)


ATT_SCALE = ATT_HEAD_DIM ** -0.5
ATT_PAIRS = ATT_HEADS // 2
SKEW_LANES = 1024
REL_LANES = 384


def _skew(x, left):
    amt = (ATT_QB - 1) - lax.broadcasted_iota(jnp.int32, (ATT_QB, 1), 0)
    for bit in range(8):
        sh = (SKEW_LANES - (1 << bit)) if left else (1 << bit)
        x = jnp.where(((amt >> bit) & 1) == 1, pltpu.roll(x, sh, 1), x)
    return x


def _dist_onehot(shape, dist_axis):
    j = lax.broadcasted_iota(jnp.int32, shape, dist_axis)
    r = lax.broadcasted_iota(jnp.int32, shape, 1 - dist_axis)
    return (jnp.clip((ATT_WIN - 1) - j, -MAX_REL, MAX_REL) + MAX_REL == r).astype(BF16)


def _dot_exact(v, onehot):
    acc = jnp.zeros((v.shape[0], onehot.shape[1]), F32)
    rem = v
    for _ in range(3):
        part = rem.astype(BF16)
        acc = acc + _dot(part, onehot)
        rem = rem - part.astype(F32)
    return acc


ATT_EDGE = ATT_PAD // ATT_QB


def att_bias(rel_bias):
    H = rel_bias.shape[0]
    rb = jnp.pad(rel_bias, ((0, 0), (0, REL_LANES - rel_bias.shape[1]))).reshape(H, 1, REL_LANES)

    def body(rb_ref, o_ref):
        by_col = _dot_exact(jnp.broadcast_to(rb_ref[...], (8, REL_LANES)), _dist_onehot((REL_LANES, SKEW_LANES), 1))
        x = _skew(jnp.broadcast_to(by_col[0:1, :], (ATT_QB, SKEW_LANES)), left=True)[:, 0:ATT_WIN]
        qc = lax.broadcasted_iota(jnp.int32, (ATT_QB, 1), 0) // CHUNK + LEFT_CHUNKS
        col = lax.broadcasted_iota(jnp.int32, (1, ATT_WIN), 1)
        dc = qc - col // CHUNK
        band = (dc >= 0) & (dc <= LEFT_CHUNKS)
        for blk in range(ATT_EDGE + 1):
            o_ref[blk] = jnp.where(band & (col >= ATT_PAD - blk * ATT_QB), x, NEG)

    return pl.pallas_call(
        body, name="att_bias", grid=(H,),
        in_specs=[pl.BlockSpec((None, 1, REL_LANES), lambda h: (h, 0, 0))],
        out_specs=pl.BlockSpec((ATT_EDGE + 1, None, ATT_QB, ATT_WIN), lambda h: (0, h, 0, 0)),
        out_shape=jax.ShapeDtypeStruct((ATT_EDGE + 1, H, ATT_QB, ATT_WIN), F32), compiler_params=_cp("parallel"),
    )(rb)


def relbias_bwd(dbias):
    H = dbias.shape[0]

    def body(x_ref, o_ref):
        x = jnp.concatenate([x_ref[...], jnp.zeros((ATT_QB, SKEW_LANES - ATT_WIN), F32)], axis=1)
        col = jnp.sum(_skew(x, left=False), axis=0, keepdims=True)
        o_ref[...] = _dot_exact(jnp.broadcast_to(col, (8, SKEW_LANES)), _dist_onehot((SKEW_LANES, REL_LANES), 0))

    out = pl.pallas_call(
        body, name="relbias_bwd", grid=(H,),
        in_specs=[pl.BlockSpec((None, ATT_QB, ATT_WIN), lambda h: (h, 0, 0))],
        out_specs=pl.BlockSpec((None, 8, REL_LANES), lambda h: (h, 0, 0)),
        out_shape=jax.ShapeDtypeStruct((H, 8, REL_LANES), F32), compiler_params=_cp("parallel"),
    )(dbias)
    return out[:, 0, :2 * MAX_REL + 1]


def _ca_scores(qh, kw, bias):
    s = _dot_nt(qh, kw) + bias
    e = jnp.exp(s - jnp.max(s, axis=-1, keepdims=True))
    return e, 1.0 / jnp.sum(e, axis=-1, keepdims=True)


def _ca_head(qv, m):
    return jnp.where(m, qv, jnp.zeros_like(qv)) * ATT_SCALE


def _ca_bias_spec():
    return pl.BlockSpec((None, 2, ATT_QB, ATT_WIN), lambda hp, b: (jnp.minimum(b, ATT_EDGE), hp, 0, 0))


def ca_fwd(q, kvp, gate, bias):
    L, D = q.shape
    Lp = kvp.shape[0]
    nb = L // ATT_QB

    def body(q_ref, k_ref, v_ref, g_ref, b_ref, r_ref, o_ref):
        w = pl.multiple_of(pl.program_id(1) * ATT_QB, ATT_QB)
        kw = k_ref[pl.ds(w, ATT_WIN), :]
        vw = v_ref[pl.ds(w, ATT_WIN), :]
        qv = q_ref[...]
        first = lax.broadcasted_iota(jnp.int32, (1, 128), 1) < ATT_HEAD_DIM
        outs = []
        for hh, m in enumerate((first, jnp.logical_not(first))):
            e, inv = _ca_scores(_ca_head(qv, m), kw, b_ref[hh])
            outs.append(_dot(e.astype(BF16), vw) * inv)
        o = jnp.where(first, outs[0], outs[1])
        r_ref[...] = (o * _silu(g_ref[...])).astype(BF16)
        o_ref[...] = o.astype(BF16)

    blk = pl.BlockSpec((ATT_QB, 128), lambda hp, b: (b, hp))
    return pl.pallas_call(
        body, name="ca_fwd", grid=(ATT_PAIRS, nb),
        in_specs=[blk, pl.BlockSpec((Lp, 128), lambda hp, b: (0, hp)),
                  pl.BlockSpec((Lp, 128), lambda hp, b: (0, ATT_PAIRS + hp)), blk, _ca_bias_spec()],
        out_specs=[blk, blk], out_shape=[jax.ShapeDtypeStruct((L, D), BF16), jax.ShapeDtypeStruct((L, D), BF16)],
        compiler_params=_cp("parallel", "arbitrary"),
    )(q, kvp, kvp, gate, bias)


def ca_bwd(q, kvp, gate, bias, dr, o):
    L, D = q.shape
    Lp = kvp.shape[0]
    nb = L // ATT_QB

    def body(q_ref, k_ref, v_ref, g_ref, b_ref, dr_ref, o_ref, dq_ref, dg_ref, dk_ref, dv_ref, db_ref):
        b = pl.program_id(1)

        @pl.when(b == 0)
        def _():
            for r in (dk_ref, dv_ref, db_ref):
                r[...] = jnp.zeros_like(r)

        w = pl.multiple_of(b * ATT_QB, ATT_QB)
        kw = k_ref[pl.ds(w, ATT_WIN), :]
        vw = v_ref[pl.ds(w, ATT_WIN), :]
        qv = q_ref[...]
        gate_v = g_ref[...]
        drv = dr_ref[...]
        o = o_ref[...].astype(F32)
        sgate, dsgate = _silu_pair(gate_v)
        do = drv * sgate
        doo = do * o
        first = lax.broadcasted_iota(jnp.int32, (1, 128), 1) < ATT_HEAD_DIM
        dqs = []
        dkw = jnp.zeros((ATT_WIN, 128), F32)
        dvw = jnp.zeros((ATT_WIN, 128), F32)
        for hh, m in enumerate((first, jnp.logical_not(first))):
            qh = _ca_head(qv, m)
            e, inv = _ca_scores(qh, kw, b_ref[hh])
            eb = e.astype(BF16)
            doh = jnp.where(m, do, 0.0)
            dp = _dot_nt(doh.astype(BF16), vw)
            dvw = dvw + _dot_tn(eb, (doh * inv).astype(BF16))
            rs = jnp.sum(jnp.where(m, doo, 0.0), axis=-1, keepdims=True)
            ds = e * ((dp - rs) * inv)
            db_ref[hh] += ds
            dsb = ds.astype(BF16)
            dqs.append(_dot(dsb, kw))
            dkw = dkw + _dot_tn(dsb, qh)
        dg_ref[...] = (drv * o * dsgate).astype(BF16)
        dq_ref[...] = (jnp.where(first, dqs[0], dqs[1]) * ATT_SCALE).astype(BF16)
        dk_ref[pl.ds(w, ATT_WIN), :] += dkw
        dv_ref[pl.ds(w, ATT_WIN), :] += dvw

    blk = pl.BlockSpec((ATT_QB, 128), lambda hp, b: (b, hp))
    kblk = pl.BlockSpec((Lp, 128), lambda hp, b: (0, hp))
    vblk = pl.BlockSpec((Lp, 128), lambda hp, b: (0, ATT_PAIRS + hp))
    bblk = pl.BlockSpec((2, ATT_QB, ATT_WIN), lambda hp, b: (hp, 0, 0))
    return pl.pallas_call(
        body, name="ca_bwd", grid=(ATT_PAIRS, nb),
        in_specs=[blk, kblk, vblk, blk, _ca_bias_spec(), blk, blk],
        out_specs=[blk, blk, kblk, kblk, bblk],
        out_shape=[jax.ShapeDtypeStruct((L, D), BF16), jax.ShapeDtypeStruct((L, D), BF16),
                   jax.ShapeDtypeStruct((Lp, D), F32), jax.ShapeDtypeStruct((Lp, D), F32),
                   jax.ShapeDtypeStruct(bias.shape[1:], F32)],
        compiler_params=_cp("parallel", "arbitrary"),
    )(q, kvp, kvp, gate, bias, dr, o)


def loss_bwd(x, target, g, tm=ROW_TILE):
    L, D = x.shape
    tm = min(tm, L)

    def body(x_ref, t_ref, g_ref, loss_ref, dx_ref, dg_ref):
        @pl.when(pl.program_id(0) == 0)
        def _():
            loss_ref[...] = jnp.zeros_like(loss_ref)
            dg_ref[...] = jnp.zeros_like(dg_ref)

        xv = x_ref[...]
        gv = g_ref[...]
        _, xhat = _rms_parts(xv)
        err = xhat * gv - t_ref[...]
        loss_ref[...] += 0.5 * jnp.sum(jnp.sum(err * err, axis=-1, keepdims=True), axis=0, keepdims=True) / D
        dx, dg = _rms_bwd(xv, gv, err * (1.0 / D))
        dx_ref[...] = dx
        dg_ref[...] += dg

    row = pl.BlockSpec((tm, D), lambda t: (t, 0))
    return pl.pallas_call(
        body, name="loss_bwd", grid=(L // tm,),
        in_specs=[row, row, _full(g.shape)],
        out_specs=[_full((1, 128)), row, _full((1, D))],
        out_shape=[jax.ShapeDtypeStruct((1, 128), F32), jax.ShapeDtypeStruct((L, D), F32),
                   jax.ShapeDtypeStruct((1, D), F32)],
        compiler_params=_cp("arbitrary"),
    )(x, target, g)


_ADAM_C1 = 1.0 / (1.0 - ADAM_B1 ** ADAM_STEP)
_ADAM_C2 = 1.0 / (1.0 - ADAM_B2 ** ADAM_STEP)


def _adam_update(w, g, m, v):
    mn = ADAM_B1 * m + (1.0 - ADAM_B1) * g
    vn = ADAM_B2 * v + (1.0 - ADAM_B2) * g * g
    delta = -ADAM_LR * ((mn * _ADAM_C1) / (jnp.sqrt(vn * _ADAM_C2) + ADAM_EPS) + ADAM_WD * w)
    return delta, mn, vn


def adamw(w, g, m, v, name, tr=512):
    R, C = w.shape
    tr = min(tr, R)

    def body(w_ref, g_ref, m_ref, v_ref, d_ref, mo_ref, vo_ref):
        d_ref[...], mo_ref[...], vo_ref[...] = _adam_update(w_ref[...], g_ref[...], m_ref[...], v_ref[...])

    blk = pl.BlockSpec((tr, C), lambda i: (i, 0))
    sh = jax.ShapeDtypeStruct((R, C), F32)
    return pl.pallas_call(
        body, name=name, grid=(R // tr,), in_specs=[blk] * 4, out_specs=[blk] * 3,
        out_shape=[sh] * 3, compiler_params=_cp("parallel"),
    )(w, g, m, v)


def adamw_allreduce(gathered, w, m, v, shard, name, slot=None):
    R, C = w.shape
    sharded = slot is None and gathered.shape[2] != C

    def body(s_ref, ga_ref, w_ref, m_ref, v_ref, g_ref, d_ref, mo_ref, vo_ref):
        take = (lambda d: ga_ref[d]) if slot is None else (lambda d: ga_ref[d, slot:slot + R, 0:C])
        g = take(0)
        for d in range(1, N_DEV):
            g = g + take(d)
        g_ref[...] = g
        d_ref[...], mo_ref[...], vo_ref[...] = _adam_update(w_ref[...], g, m_ref[...], v_ref[...])

    blk = pl.BlockSpec((R, C), lambda i, s_ref: (0, 0))
    if slot is not None:
        gblk = pl.BlockSpec(gathered.shape, lambda i, s_ref: (0, 0, 0))
    else:
        gblk = pl.BlockSpec((N_DEV, R, C),
                            (lambda i, s_ref: (0, 0, s_ref[0])) if sharded else (lambda i, s_ref: (0, 0, 0)))
    sh = jax.ShapeDtypeStruct((R, C), F32)
    return pl.pallas_call(
        body, name=name,
        grid_spec=pltpu.PrefetchScalarGridSpec(num_scalar_prefetch=1, grid=(1,), in_specs=[gblk, blk, blk, blk],
                                               out_specs=[blk] * 4),
        out_shape=[sh] * 4, compiler_params=_cp("arbitrary"),
    )(shard, gathered, w, m, v)


LATE = ("ev_s5_glu_w", "ev_w_out", "od_w_in", "od_w_out", "xa_w_qg", "xa_w_kv", "xa_w_o")
EARLY_GRADS = ("od_w_in", "od_w_out", "xa_w_qg", "xa_w_kv", "xa_w_o", "ev_w_out", "ev_s5_glu_w")


def _reduce_to_chip(gs, names, core, tag):
    from_sibling = sibling_send_other_half(gs, "sibling_send_" + tag)
    return [sum_with_sibling(gi, ri, core, "sum_sibling_" + n) for n, gi, ri in zip(names, gs, from_sibling)]


def local_step(x, mem, target, p, gw, late, place, core):
    row = lambda a: a.reshape(1, -1)
    D = D_MODEL
    L = x.shape[0]
    g, big = {}, {}
    gw = dict(gw)

    z, h0b = norm_mm(x, p["ev_norm_g"], gw["ev_w_in"], [((0, 1, 2, 3), F32, 0)], "ev_in")
    ys, s5_saved, landed = s5_mixer_core_fwd(
        z, p["ev_s5_lambda_re"][0], p["ev_s5_lambda_im"][0], p["ev_s5_log_dt"][0], p["ev_s5_b_re"][0],
        p["ev_s5_b_im"][0], p["ev_s5_c_re"][0], p["ev_s5_c_im"][0], p["ev_s5_d"][0],
        carried=carried_allgather([late[n] for n in LATE]))
    for n, gth in zip(LATE, landed):
        rows = gth.shape[1]
        gw[n] = gth.reshape(N_CHIPS, 2, rows // 2, gth.shape[2]) if n.startswith("xa_") else gth
    memn_b = rms_fwd(mem, row(p["mem_norm_g"]), "mem_norm")
    kvs = [mm_cols(memn_b, gw["xa_w_kv"], l, f"xa_kv{l}", BF16) for l in range(2)]
    conv_w = p["ev_conv_w"][0]
    c = conv_fwd(z, conv_w, p["ev_conv_b"])
    tail = (gw["ev_s5_glu_w"], p["ev_s5_glu_b"], p["ev_conv_ln_g"], p["ev_conv_ln_b"], gw["ev_w_out"])
    x1 = ev_tail_fwd(ys, z, c, x, *tail)
    xa0 = (row(p["xa_norm_g"][0]), gw["xa_w_qg"], kvs[0], gw["xa_w_o"], 0)
    x2 = xa_fwd(x1, *xa0, "xa_fwd0")

    q, kvp, gate, h1b = norm_mm(x2, p["od_norm_g"], gw["od_w_in"],
                                [((0,), BF16, 0), ((1, 2), BF16, ATT_PAD), ((3,), F32, 0)], "od_in")
    kvp = zero_rows(kvp, ATT_PAD, "od_kv_pad")
    bias = att_bias(p["od_rel_bias"][0])
    r, att_o = ca_fwd(q, kvp, gate, bias)
    x3 = mm_res(r, gw["od_w_out"], x2, "od_out")
    xa1 = (row(p["xa_norm_g"][1]), gw["xa_w_qg"], kvs[1], gw["xa_w_o"], 1)
    x4 = xa_fwd(x3, *xa1, "xa_fwd1")

    loss, dx4, dgf = loss_bwd(x4, target, row(p["final_norm_g"]))
    g["final_norm_g"] = dgf.reshape(D)

    dx3, dqg1, hx1, rx1, dkv1, dgxa1 = xa_bwd(x3, dx4, *xa1, "xa_bwd1")
    dwqg = mm_tn(hx1, dqg1, "xa_dwqg1", ("cols", 1))
    dwo = mm_tn(rx1, dx4, "xa_dwo1", ("rows", 1))

    big["od_w_out"] = mm_tn(r, dx3, "od_dwout", ("rows",))
    dr = mm_nt_rows(dx3, gw["od_w_out"], "od_out_bwd")
    dq, dgate, dkp, dvp, dbias = ca_bwd(q, kvp, gate, bias, dr, att_o)
    pieces, offs = (dq, dkp, dvp, dgate), (0, ATT_PAD, ATT_PAD, 0)
    dwin = None
    for s in range(N_CHIPS):
        dwin = mm_tn(h1b, pieces[s], f"od_dwin{s}", ("slab", s), into=dwin, b_off=offs[s],
                     bl=ATT_PAD if offs[s] else 1024)
    big["od_w_in"] = dwin
    dx2, dgod = mm_nt_normbwd(pieces, offs, gw["od_w_in"], x2, p["od_norm_g"], dx3, "od_in_bwd")
    g["od_norm_g"] = dgod
    g["od_rel_bias"] = relbias_bwd(dbias)[None]

    dx1, dqg0, hx0, rx0, dkv0, dgxa0 = xa_bwd(x1, dx2, *xa0, "xa_bwd0")
    big["xa_w_qg"] = mm_tn(hx0, dqg0, "xa_dwqg0", ("cols", 0), into=dwqg)
    big["xa_w_o"] = mm_tn(rx0, dx2, "xa_dwo0", ("rows", 0), into=dwo)
    g["xa_norm_g"] = jnp.concatenate([dgxa0, dgxa1], axis=0)

    dys, dc, dz, ra, z1b, dtb, dbglu, dlng, dlnb = ev_tail_bwd(ys, z, c, dx1, *tail)
    big["ev_w_out"] = mm_tn(ra, dx1, "ev_dwout", ("rows",))
    big["ev_s5_glu_w"] = mm_tn(z1b, dtb, "ev_dwglu", ("rows",))
    g["ev_s5_glu_b"], g["ev_conv_ln_g"], g["ev_conv_ln_b"] = dbglu, dlng, dlnb
    dwkv = mm_tn(memn_b, dkv1, "xa_dwkv1", ("cols", 1), bl=MEM_LEN)
    big["xa_w_kv"] = mm_tn(memn_b, dkv0, "xa_dwkv0", ("cols", 0), into=dwkv, bl=MEM_LEN)
    dmem0 = mm_nt_cols(dkv0, gw["xa_w_kv"], 0, "xa_kv_bwd0")
    dmem1 = mm_nt_cols(dkv1, gw["xa_w_kv"], 1, "xa_kv_bwd1")
    g["mem_norm_g"] = rms_dgain(mem, dmem0, dmem1, "mem_norm_bwd").reshape(D)

    shard_major = lambda t: t.reshape((-1,) + t.shape[-2:])
    gs = [shard_major(big[n]) for n in EARLY_GRADS]
    dz, dconvw, dconvb, *from_sibling = conv_bwd(z, dc, dz, conv_w, carried=carried_sibling_send(gs))
    g["ev_conv_w"] = dconvw[None, :CONV_KERNEL]
    g["ev_conv_b"] = dconvb
    chip_sums = [sum_with_sibling(gi, ri, core, "sum_sibling_" + n) for n, gi, ri in zip(EARLY_GRADS, gs, from_sibling)]
    dz, s5g, from_chips = s5_mixer_core_bwd(z, dys, dz, p["ev_s5_lambda_re"][0], p["ev_s5_lambda_im"][0], s5_saved,
                                            carried=carried_chips_exchange(chip_sums))
    reduced = {n: sum_chips(ci, ri, place, "sum_chips_" + n) for n, ci, ri in zip(EARLY_GRADS, chip_sums, from_chips)}
    for n, v in s5g.items():
        g["ev_s5_" + n] = v[None]
    dwin_ev = mm_tn(h0b, dz, "ev_dwin", ("cols",))
    grad_x, dgev = mm_nt_normbwd((dz,), (0,), gw["ev_w_in"], x, p["ev_norm_g"], dx1, "ev_in_bwd")
    g["ev_norm_g"] = dgev
    chip_sum = _reduce_to_chip([dwin_ev], ["ev_w_in"], core, "last")
    reduced["ev_w_in"] = sum_chips(chip_sum[0], chips_exchange(chip_sum)[0], place, "sum_chips_ev_w_in")
    return loss, grad_x, g, reduced


def _me():
    return lax.axis_index("x"), lax.axis_index("y"), lax.axis_index("c")


def _other_chips(x, y):
    return [(1 - x, y), (x, 1 - y), (1 - x, 1 - y)]


def _remote(src, dst, send_sems, recv_sems, k, to):
    return pltpu.make_async_remote_copy(src_ref=src, dst_ref=dst, send_sem=send_sems.at[k], recv_sem=recv_sems.at[k],
                                        device_id=to, device_id_type=MESH)


def _rows_half(ref, h):
    H = ref.shape[-2] // 2
    return ref.at[(slice(None),) * (len(ref.shape) - 2) + (pl.ds(h * H, H), slice(None))]


def allgather_chip_blocks(halved, whole):
    nh, nw = len(halved), len(whole)
    n = nh + nw

    def body(*refs):
        ins, outs = refs[:n], refs[n:2 * n]
        send_sems, recv_sems, local_sems = refs[2 * n:]
        x, y, c = _me()
        sib = (x, y, 1 - c)
        chips = _other_chips(x, y)
        me = 2 * x + y
        local = [pltpu.make_async_copy(ins[i], outs[i].at[me], local_sems.at[i]) for i in range(n)]
        for cp in local:
            cp.start()
        first, passed = [], []
        for i in range(n):
            for j, (cx, cy) in enumerate(chips):
                if i < nh:
                    src, dst = _rows_half(ins[i], c), _rows_half(outs[i].at[me], c)
                    k = 6 * i + j
                else:
                    src, dst = ins[i], outs[i].at[me]
                    k = 6 * nh + 3 * (i - nh) + j
                first.append(_remote(src, dst, send_sems, recv_sems, k, (cx, cy, c)))
        for cp in first:
            cp.start()
        for j, (cx, cy) in enumerate(chips):
            for i in range(nh):
                got = _rows_half(outs[i].at[2 * cx + cy], c)
                _remote(got, got, send_sems, recv_sems, 6 * i + j, (cx, cy, c)).wait_recv()
                fw = _remote(got, got, send_sems, recv_sems, 6 * i + 3 + j, sib)
                fw.start()
                passed.append(fw)
        for j, (cx, cy) in enumerate(chips):
            for i in range(nh):
                got = _rows_half(outs[i].at[2 * cx + cy], 1 - c)
                _remote(got, got, send_sems, recv_sems, 6 * i + 3 + j, sib).wait_recv()
            for i in range(nh, n):
                got = outs[i].at[2 * cx + cy]
                _remote(got, got, send_sems, recv_sems, 6 * nh + 3 * (i - nh) + j, (cx, cy, c)).wait_recv()
        for cp in first + passed:
            cp.wait_send()
        for cp in local:
            cp.wait()

    arrays = list(halved) + list(whole)
    nsem = 6 * nh + 3 * nw
    return pl.pallas_call(
        body, name="allgather_chip_blocks", in_specs=[ANY] * n, out_specs=[ANY] * n,
        out_shape=[jax.ShapeDtypeStruct((N_CHIPS,) + a.shape, a.dtype) for a in arrays],
        scratch_shapes=[pltpu.SemaphoreType.DMA((nsem,)), pltpu.SemaphoreType.DMA((nsem,)),
                        pltpu.SemaphoreType.DMA((n,))],
    )(*arrays)


def allgather_devices(vs):
    n = len(vs)

    def body(*refs):
        ins, outs = refs[:n], refs[n:2 * n]
        send_sems, recv_sems, local_sems = refs[2 * n:]
        x, y, c = _me()
        sib = (x, y, 1 - c)
        chips = _other_chips(x, y)
        me = 4 * x + 2 * y + c
        local = [pltpu.make_async_copy(ins[i], outs[i].at[me], local_sems.at[i]) for i in range(n)]
        for cp in local:
            cp.start()
        first, passed = [], []
        for i in range(n):
            first.append(_remote(ins[i], outs[i].at[me], send_sems, recv_sems, 7 * i, sib))
            for j, (cx, cy) in enumerate(chips):
                first.append(_remote(ins[i], outs[i].at[me], send_sems, recv_sems, 7 * i + 1 + j, (cx, cy, c)))
        for cp in first:
            cp.start()
        for j, (cx, cy) in enumerate(chips):
            for i in range(n):
                got = outs[i].at[4 * cx + 2 * cy + c]
                _remote(got, got, send_sems, recv_sems, 7 * i + 1 + j, (cx, cy, c)).wait_recv()
                fw = _remote(got, got, send_sems, recv_sems, 7 * i + 4 + j, sib)
                fw.start()
                passed.append(fw)
        for i in range(n):
            got = outs[i].at[4 * x + 2 * y + (1 - c)]
            _remote(got, got, send_sems, recv_sems, 7 * i, sib).wait_recv()
            for j, (cx, cy) in enumerate(chips):
                got = outs[i].at[4 * cx + 2 * cy + (1 - c)]
                _remote(got, got, send_sems, recv_sems, 7 * i + 4 + j, sib).wait_recv()
        for cp in first + passed:
            cp.wait_send()
        for cp in local:
            cp.wait()

    return pl.pallas_call(
        body, name="allgather_devices", in_specs=[ANY] * n, out_specs=[ANY] * n,
        out_shape=[jax.ShapeDtypeStruct((N_DEV,) + v.shape, v.dtype) for v in vs],
        scratch_shapes=[pltpu.SemaphoreType.DMA((7 * n,)), pltpu.SemaphoreType.DMA((7 * n,)),
                        pltpu.SemaphoreType.DMA((n,))],
    )(*vs)


def sibling_send_other_half(gs, name):
    n = len(gs)

    def body(*refs):
        ins, outs = refs[:n], refs[n:2 * n]
        send_sems, recv_sems = refs[2 * n:]
        x, y, c = _me()
        cps = [_remote(_rows_half(ins[i], 1 - c), outs[i], send_sems, recv_sems, i, (x, y, 1 - c)) for i in range(n)]
        for cp in cps:
            cp.start()
        for cp in cps:
            cp.wait()

    return pl.pallas_call(
        body, name=name, in_specs=[ANY] * n, out_specs=[ANY] * n,
        out_shape=[jax.ShapeDtypeStruct((g.shape[0], g.shape[1] // 2, g.shape[2]), g.dtype) for g in gs],
        scratch_shapes=[pltpu.SemaphoreType.DMA((n,)), pltpu.SemaphoreType.DMA((n,))],
    )(*gs)


def chips_exchange(parts):
    n = len(parts)

    def body(*refs):
        ins, outs = refs[:n], refs[n:2 * n]
        send_sems, recv_sems = refs[2 * n:]
        x, y, c = _me()
        cps = []
        for i in range(n):
            nl = ins[i].shape[0] // N_CHIPS
            for j, (cx, cy) in enumerate(_other_chips(x, y)):
                cps.append(_remote(ins[i].at[pl.ds((2 * cx + cy) * nl, nl)], outs[i].at[j], send_sems, recv_sems,
                                   3 * i + j, (cx, cy, c)))
        for cp in cps:
            cp.start()
        for cp in cps:
            cp.wait()

    return pl.pallas_call(
        body, name="chips_exchange", in_specs=[ANY] * n, out_specs=[ANY] * n,
        out_shape=[jax.ShapeDtypeStruct((3, a.shape[0] // N_CHIPS) + a.shape[1:], a.dtype) for a in parts],
        scratch_shapes=[pltpu.SemaphoreType.DMA((3 * n,)), pltpu.SemaphoreType.DMA((3 * n,))],
    )(*parts)


def sibling_share(fulls):
    n = len(fulls)

    def body(*refs):
        outs = refs[n:2 * n]
        send_sems, recv_sems = refs[2 * n:]
        x, y, c = _me()
        cps = [_remote(_rows_half(outs[i], c), _rows_half(outs[i], c), send_sems, recv_sems, i, (x, y, 1 - c))
               for i in range(n)]
        for cp in cps:
            cp.start()
        for i in range(n):
            got = _rows_half(outs[i], 1 - c)
            _remote(got, got, send_sems, recv_sems, i, (x, y, 1 - c)).wait_recv()
        for cp in cps:
            cp.wait_send()

    return pl.pallas_call(
        body, name="sibling_share", in_specs=[ANY] * n, out_specs=[ANY] * n,
        out_shape=[jax.ShapeDtypeStruct(f.shape, f.dtype) for f in fulls],
        input_output_aliases={i: i for i in range(n)},
        scratch_shapes=[pltpu.SemaphoreType.DMA((n,)), pltpu.SemaphoreType.DMA((n,))],
    )(*fulls)


def sum_with_sibling(g, recv, core, name):
    S, H, C = recv.shape
    tr = min(512, H)

    def body(c_ref, g_ref, r_ref, o_ref):
        o_ref[...] = (g_ref[...].astype(F32) + r_ref[...].astype(F32)).astype(o_ref.dtype)

    nb = H // tr
    return pl.pallas_call(
        body, name=name,
        grid_spec=pltpu.PrefetchScalarGridSpec(
            num_scalar_prefetch=1, grid=(S, nb),
            in_specs=[pl.BlockSpec((None, tr, C), lambda s, i, c_ref: (s, c_ref[0] * nb + i, 0)),
                      pl.BlockSpec((None, tr, C), lambda s, i, c_ref: (s, i, 0))],
            out_specs=pl.BlockSpec((None, tr, C), lambda s, i, c_ref: (s, i, 0))),
        out_shape=jax.ShapeDtypeStruct((S, H, C), g.dtype), compiler_params=_cp("parallel", "parallel"),
    )(core, g, recv)


def sum_chips(a, recv, place, name):
    _, nl, H, C = recv.shape
    tr = min(512, H)
    nb = H // tr

    def body(p_ref, a_ref, r_ref, o_ref):
        acc = a_ref[...].astype(F32)
        for j in range(3):
            acc = acc + r_ref[j].astype(F32)
        o_ref[...] = acc

    return pl.pallas_call(
        body, name=name,
        grid_spec=pltpu.PrefetchScalarGridSpec(
            num_scalar_prefetch=1, grid=(nl, nb),
            in_specs=[pl.BlockSpec((None, tr, C), lambda l, i, p_ref: (p_ref[0] * nl + l, i, 0)),
                      pl.BlockSpec((3, None, tr, C), lambda l, i, p_ref: (0, l, i, 0))],
            out_specs=pl.BlockSpec((None, tr, C), lambda l, i, p_ref: (l, p_ref[1] * nb + i, 0))),
        out_shape=jax.ShapeDtypeStruct((nl, 2 * H, C), F32), compiler_params=_cp("parallel", "parallel"),
    )(place, a, recv)


def pack_rows(arrays, name):
    starts, r0 = [], 0
    for a in arrays:
        if a.shape[0] >= SUBLANES:
            r0 = -(-r0 // SUBLANES) * SUBLANES
        starts.append(r0)
        r0 += a.shape[0]
    r0 = -(-r0 // SUBLANES) * SUBLANES
    n = len(arrays)

    def body(*refs):
        o_ref = refs[n]
        o_ref[...] = jnp.zeros_like(o_ref)
        for a_ref, s in zip(refs[:n], starts):
            r, c = a_ref.shape
            o_ref[s:s + r, 0:c] = a_ref[...]

    out = pl.pallas_call(body, name=name, out_shape=jax.ShapeDtypeStruct((r0, PACK_COLS), F32))(*arrays)
    return out, starts


def sum_slot(gathered, slot, shape, name):
    r, c = shape

    def body(ga_ref, o_ref):
        acc = ga_ref[0, slot:slot + r, 0:c]
        for d in range(1, N_DEV):
            acc = acc + ga_ref[d, slot:slot + r, 0:c]
        o_ref[...] = acc

    return pl.pallas_call(body, name=name, out_shape=jax.ShapeDtypeStruct((r, c), F32))(gathered)


def carried_allgather(blocks):
    n = len(blocks)

    def first_hop(ins, outs, sems, i, j, chip, x, y, c):
        me = 2 * x + y
        return _remote(_rows_half(ins[i], c), _rows_half(outs[i].at[me], c), sems[0], sems[1], 6 * i + j, (*chip, c))

    def start(ins, outs, sems):
        x, y, c = _me()
        for i in range(n):
            pltpu.make_async_copy(ins[i], outs[i].at[2 * x + y], sems[2].at[i]).start()
        for i in range(n):
            for j, chip in enumerate(_other_chips(x, y)):
                first_hop(ins, outs, sems, i, j, chip, x, y, c).start()

    def finish(ins, outs, sems):
        x, y, c = _me()
        sib = (x, y, 1 - c)
        chips = _other_chips(x, y)
        passed = []
        for j, (cx, cy) in enumerate(chips):
            for i in range(n):
                got = _rows_half(outs[i].at[2 * cx + cy], c)
                _remote(got, got, sems[0], sems[1], 6 * i + j, (cx, cy, c)).wait_recv()
                fw = _remote(got, got, sems[0], sems[1], 6 * i + 3 + j, sib)
                fw.start()
                passed.append(fw)
        for j, (cx, cy) in enumerate(chips):
            for i in range(n):
                got = _rows_half(outs[i].at[2 * cx + cy], 1 - c)
                _remote(got, got, sems[0], sems[1], 6 * i + 3 + j, sib).wait_recv()
        for i in range(n):
            for j, chip in enumerate(chips):
                first_hop(ins, outs, sems, i, j, chip, x, y, c).wait_send()
        for fw in passed:
            fw.wait_send()
        for i in range(n):
            pltpu.make_async_copy(ins[i], outs[i].at[2 * x + y], sems[2].at[i]).wait()

    return Carried(blocks, [jax.ShapeDtypeStruct((N_CHIPS,) + b.shape, b.dtype) for b in blocks],
                   [pltpu.SemaphoreType.DMA((6 * n,)), pltpu.SemaphoreType.DMA((6 * n,)), pltpu.SemaphoreType.DMA((n,))],
                   start, None, finish)


def carried_sibling_send(gs):
    n = len(gs)

    def copies(ins, outs, sems):
        x, y, c = _me()
        return [_remote(_rows_half(ins[i], 1 - c), outs[i], sems[0], sems[1], i, (x, y, 1 - c)) for i in range(n)]

    def start(ins, outs, sems):
        for cp in copies(ins, outs, sems):
            cp.start()

    def finish(ins, outs, sems):
        for cp in copies(ins, outs, sems):
            cp.wait()

    return Carried(gs, [jax.ShapeDtypeStruct((g.shape[0], g.shape[1] // 2, g.shape[2]), g.dtype) for g in gs],
                   [pltpu.SemaphoreType.DMA((n,)), pltpu.SemaphoreType.DMA((n,))], start, None, finish)


def carried_chips_exchange(parts):
    n = len(parts)

    def copies(ins, outs, sems):
        x, y, c = _me()
        cps = []
        for i in range(n):
            nl = ins[i].shape[0] // N_CHIPS
            for j, (cx, cy) in enumerate(_other_chips(x, y)):
                cps.append(_remote(ins[i].at[pl.ds((2 * cx + cy) * nl, nl)], outs[i].at[j], sems[0], sems[1],
                                   3 * i + j, (cx, cy, c)))
        return cps

    def start(ins, outs, sems):
        for cp in copies(ins, outs, sems):
            cp.start()

    def finish(ins, outs, sems):
        for cp in copies(ins, outs, sems):
            cp.wait()

    return Carried(parts, [jax.ShapeDtypeStruct((3, a.shape[0] // N_CHIPS) + a.shape[1:], a.dtype) for a in parts],
                   [pltpu.SemaphoreType.DMA((3 * n,)), pltpu.SemaphoreType.DMA((3 * n,))], start, None, finish)


BIG = ("ev_w_in", "ev_s5_glu_w", "ev_w_out", "od_w_in", "od_w_out", "xa_w_qg", "xa_w_kv", "xa_w_o")
SHARDED_F32 = (("ev_conv_w", 2), ("od_norm_g", 1))
SMALL = ("mem_norm_g", "ev_norm_g", "ev_s5_lambda_re", "ev_s5_lambda_im", "ev_s5_log_dt", "ev_s5_b_re", "ev_s5_b_im",
         "ev_s5_c_re", "ev_s5_c_im", "ev_s5_d", "ev_s5_glu_b", "ev_conv_b", "ev_conv_ln_g", "ev_conv_ln_b",
         "od_rel_bias", "xa_norm_g", "final_norm_g")
NARROW = ("ev_s5_c_re", "ev_s5_c_im")
DENSE_B = ("ev_s5_b_re", "ev_s5_b_im")
PACK_COLS = 1024
WEIGHTS = ("mem_norm_g", "ev_norm_g", "ev_w_in", "ev_s5_lambda_re", "ev_s5_lambda_im", "ev_s5_log_dt", "ev_s5_b_re",
           "ev_s5_b_im", "ev_s5_c_re", "ev_s5_c_im", "ev_s5_d", "ev_s5_glu_w", "ev_s5_glu_b", "ev_conv_w", "ev_conv_b",
           "ev_conv_ln_g", "ev_conv_ln_b", "ev_w_out", "od_norm_g", "od_w_in", "od_rel_bias", "od_w_out", "xa_norm_g",
           "xa_w_qg", "xa_w_kv", "xa_w_o", "final_norm_g")


def _as2d(a):
    return a.reshape(1, -1) if a.ndim == 1 else a.reshape(-1, a.shape[-1])


def kernel(x, mem, mem_norm_g, ev_norm_g, ev_w_in, ev_s5_lambda_re, ev_s5_lambda_im, ev_s5_log_dt, ev_s5_b_re, ev_s5_b_im, ev_s5_c_re, ev_s5_c_im, ev_s5_d, ev_s5_glu_w, ev_s5_glu_b, ev_conv_w, ev_conv_b, ev_conv_ln_g, ev_conv_ln_b, ev_w_out, od_norm_g, od_w_in, od_rel_bias, od_w_out, xa_norm_g, xa_w_qg, xa_w_kv, xa_w_o, final_norm_g, loss_target, m_mem_norm_g, m_ev_norm_g, m_ev_w_in, m_ev_s5_lambda_re, m_ev_s5_lambda_im, m_ev_s5_log_dt, m_ev_s5_b_re, m_ev_s5_b_im, m_ev_s5_c_re, m_ev_s5_c_im, m_ev_s5_d, m_ev_s5_glu_w, m_ev_s5_glu_b, m_ev_conv_w, m_ev_conv_b, m_ev_conv_ln_g, m_ev_conv_ln_b, m_ev_w_out, m_od_norm_g, m_od_w_in, m_od_rel_bias, m_od_w_out, m_xa_norm_g, m_xa_w_qg, m_xa_w_kv, m_xa_w_o, m_final_norm_g, v_mem_norm_g, v_ev_norm_g, v_ev_w_in, v_ev_s5_lambda_re, v_ev_s5_lambda_im, v_ev_s5_log_dt, v_ev_s5_b_re, v_ev_s5_b_im, v_ev_s5_c_re, v_ev_s5_c_im, v_ev_s5_d, v_ev_s5_glu_w, v_ev_s5_glu_b, v_ev_conv_w, v_ev_conv_b, v_ev_conv_ln_g, v_ev_conv_ln_b, v_ev_w_out, v_od_norm_g, v_od_w_in, v_od_rel_bias, v_od_w_out, v_xa_norm_g, v_xa_w_qg, v_xa_w_kv, v_xa_w_o, v_final_norm_g):
    a = dict(locals())
    w = {n: a[n] for n in WEIGHTS}
    shard = (2 * lax.axis_index("x") + lax.axis_index("y")).reshape(1).astype(jnp.int32)
    core = lax.axis_index("c").reshape(1).astype(jnp.int32)

    place = jnp.concatenate([shard, core])

    blocks = {n: w[n].astype(BF16).reshape(-1, w[n].shape[-1]) for n in BIG}
    early = [n for n in BIG if n not in LATE]
    gathered = allgather_chip_blocks([blocks[n] for n in early], [_as2d(w[n]) for n, _ in SHARDED_F32])
    gw = dict(zip(early, gathered))
    p = {n: w[n] for n in SMALL}
    conv_g, odn_g = gathered[len(early):]
    p["ev_conv_w"] = jnp.concatenate([conv_g[s] for s in range(N_CHIPS)], axis=1)[None]
    p["od_norm_g"] = odn_g.reshape(1, D_MODEL)

    loss, grad_x, g, reduced = local_step(x[0], mem[0], loss_target[0], p, gw, {n: blocks[n] for n in LATE},
                                          place, core)
    loss = lax.psum(loss[0, 0], ("x", "y", "c"))
    g_big = dict(zip(BIG, sibling_share([reduced[n] for n in BIG])))

    out = {tag: {} for tag in ("grad", "delta", "m", "v")}
    for n in BIG:
        sh = w[n].shape
        to2d = lambda t: t.reshape(-1, sh[-1])
        gn = to2d(g_big[n])
        d, mn, vn = adamw(to2d(w[n]), gn, to2d(a["m_" + n]), to2d(a["v_" + n]), "adamw_" + n)
        for tag, val in zip(("grad", "delta", "m", "v"), (gn, d, mn, vn)):
            out[tag][n] = val.reshape(sh)

    packed_names = [n for n in SMALL if n not in NARROW]
    single_names = list(NARROW) + [n for n, _ in SHARDED_F32]
    packed, slots = pack_rows([_as2d(g[n]) for n in packed_names], "pack_small_grads")
    gath = allgather_devices([packed] + [_as2d(g[n]) for n in single_names])
    jobs = [(n, gath[0], s) for n, s in zip(packed_names, slots)]
    jobs += [(n, gt, None) for n, gt in zip(single_names, gath[1:])]
    for n, gt, slot in jobs:
        sh = w[n].shape
        w2, m2, v2 = _as2d(w[n]), _as2d(a["m_" + n]), _as2d(a["v_" + n])
        if n in DENSE_B:
            gn = _as2d(s5_b_from_dense(sum_slot(gt, slot, g[n].shape[-2:], "sum_" + n)))
            d, mn, vn = adamw(w2, gn, m2, v2, "adamw_" + n)
        else:
            gn, d, mn, vn = adamw_allreduce(gt, w2, m2, v2, shard, "adamw_" + n, slot=slot)
        for tag, val in zip(("grad", "delta", "m", "v"), (gn, d, mn, vn)):
            out[tag][n] = val.reshape(sh)

    res = [loss, grad_x[None]]
    for tag in ("grad", "delta", "m", "v"):
        res += [out[tag][n] for n in WEIGHTS]
    return tuple(res)
```

```python
import math

import jax
import jax.numpy as jnp
import numpy as np
from jax import lax
from jax.experimental import pallas as pl
from jax.experimental.pallas import tpu as pltpu

F32 = jnp.float32
BF16 = jnp.bfloat16

D_MODEL = 1024
CHUNK = 64
LEFT_CHUNKS = 8
S5_WIDTH = 512
S5_GROUP = 16
S5_GROUPS = 32
S5_STATE = 64
S5_COLS = S5_GROUPS * S5_STATE
S5_SPLIT = 4
S5_CC = S5_COLS // S5_SPLIT
S5_UC = S5_WIDTH // S5_SPLIT
CONV_WIDTH = 512
CONV_KERNEL = 31
CONV_HALO = 32
ATT_HEADS = 16
ATT_HEAD_DIM = 64
MAX_REL = 128
MEM_LEN = 256
XA_HEADS = 4
XA_HEAD_DIM = 256
EPS = 1e-6
EVEN_IN = 2560
ODD_IN = 4096

ADAM_LR = 0.001
ADAM_B1 = 0.9
ADAM_B2 = 0.999
ADAM_EPS = 1e-08
ADAM_WD = 0.01
ADAM_STEP = 10

ROW_TILE = 256
MM_TILE = 512
S5_TILE = 512
ATT_QB = 256
ATT_PAD = LEFT_CHUNKS * CHUNK
ATT_WIN = ATT_PAD + ATT_QB
VMEM_LIMIT_V7X = 56 * 1024 * 1024
NEG = -1e30
LANES = 128
N_CHIPS = 4
N_DEV = 8

MESH = pl.DeviceIdType.MESH
ANY = pl.BlockSpec(memory_space=pl.ANY)


def _cp(*sem, vmem=VMEM_LIMIT_V7X):
    return pltpu.CompilerParams(dimension_semantics=sem if sem else None, vmem_limit_bytes=vmem)


def _full(shape):
    n = len(shape)
    return pl.BlockSpec(shape, lambda *_: (0,) * n)


def _wspec(w, layer=None):
    if layer is None:
        return _full(w.shape)
    s, _, r, c = w.shape
    return pl.BlockSpec((s, None, r, c), lambda *_: (0, layer, 0, 0))


def _lane_tile(n, cap):
    return max(t for t in range(LANES, min(n, cap) + 1, LANES) if n % t == 0)


def _sigmoid(x):
    return 1.0 / (1.0 + jnp.exp(-x))


def _silu(x):
    return x * _sigmoid(x)


def _silu_pair(x):
    s = _sigmoid(x)
    return x * s, s * (1.0 + x * (1.0 - s))


_GELU_C = math.sqrt(2.0 / math.pi)


def _gelu(x):
    return 0.5 * x * (1.0 + jnp.tanh(_GELU_C * (x + 0.044715 * x * x * x)))


def _dgelu(x):
    t = jnp.tanh(_GELU_C * (x + 0.044715 * x * x * x))
    return 0.5 * (1.0 + t) + 0.5 * x * (1.0 - t * t) * _GELU_C * (1.0 + 3.0 * 0.044715 * x * x)


def _dot(a, b):
    return jnp.dot(a, b, preferred_element_type=F32)


def _dot_nt(a, b):
    return lax.dot_general(a, b, (((1,), (1,)), ((), ())), preferred_element_type=F32)


def _dot_tn(a, b):
    return lax.dot_general(a, b, (((0,), (0,)), ((), ())), preferred_element_type=F32)


def _dot_cols(a, w4, shards=range(N_CHIPS)):
    return jnp.concatenate([_dot(a, w4[s]) for s in shards], axis=1)


def _dot_rows(a, w4):
    r = w4.shape[1]
    acc = _dot(a[:, 0:r], w4[0])
    for s in range(1, N_CHIPS):
        acc = acc + _dot(a[:, s * r:(s + 1) * r], w4[s])
    return acc


def _dot_nt_cols(dys, w4):
    acc = _dot_nt(dys[0], w4[0])
    for s in range(1, N_CHIPS):
        acc = acc + _dot_nt(dys[s], w4[s])
    return acc


def _dot_nt_rows(dy, w4):
    return jnp.concatenate([_dot_nt(dy, w4[s]) for s in range(N_CHIPS)], axis=1)


def _col_pieces(v, n):
    return [v[:, s * n:(s + 1) * n] for s in range(N_CHIPS)]


def _rms_parts(xv):
    inv = lax.rsqrt(jnp.mean(xv * xv, axis=-1, keepdims=True) + EPS)
    return inv, xv * inv


def _rms_bwd(xv, g, dh):
    inv, xhat = _rms_parts(xv)
    dg = jnp.sum(dh * xhat, axis=0, keepdims=True)
    dxh = dh * g
    dx = inv * (dxh - xhat * jnp.mean(dxh * xhat, axis=-1, keepdims=True))
    return dx, dg


def norm_mm(x, g, w4, groups, name, tm=MM_TILE):
    M, D = x.shape
    n = w4.shape[2]
    tm = min(tm, M)

    def body(x_ref, g_ref, w_ref, *outs):
        _, xhat = _rms_parts(x_ref[...])
        hb = (xhat * g_ref[...]).astype(BF16)
        for o, (shards, dt, _) in zip(outs, groups):
            o[...] = _dot_cols(hb, w_ref, shards).astype(dt)
        outs[-1][...] = hb

    out_shape = [jax.ShapeDtypeStruct((M + pad, len(sh) * n), dt) for (sh, dt, pad) in groups]
    out_specs = [pl.BlockSpec((tm, len(sh) * n), lambda i, p=pad // tm: (i + p, 0)) for (sh, _, pad) in groups]
    out_shape.append(jax.ShapeDtypeStruct((M, D), BF16))
    out_specs.append(pl.BlockSpec((tm, D), lambda i: (i, 0)))
    return pl.pallas_call(
        body, name=name, grid=(M // tm,),
        in_specs=[pl.BlockSpec((tm, D), lambda i: (i, 0)), _full(g.shape), _full(w4.shape)],
        out_specs=out_specs, out_shape=out_shape, compiler_params=_cp("parallel"),
    )(x, g, w4)


def zero_rows(buf, rows, name, tm=ROW_TILE):
    C = buf.shape[1]

    def body(b_ref, o_ref):
        o_ref[...] = jnp.zeros_like(o_ref)

    return pl.pallas_call(
        body, name=name, grid=(rows // tm,), in_specs=[ANY],
        out_specs=pl.BlockSpec((tm, C), lambda i: (i, 0)),
        out_shape=jax.ShapeDtypeStruct(buf.shape, buf.dtype), input_output_aliases={0: 0},
        compiler_params=_cp("parallel"),
    )(buf)


def mm_res(a, w4, res, name, tm=MM_TILE):
    M, K = a.shape
    N = w4.shape[2]
    tm = min(tm, M)

    def body(a_ref, w_ref, r_ref, o_ref):
        o_ref[...] = r_ref[...] + _dot_rows(a_ref[...], w_ref)

    return pl.pallas_call(
        body, name=name, grid=(M // tm,),
        in_specs=[pl.BlockSpec((tm, K), lambda i: (i, 0)), _full(w4.shape), pl.BlockSpec((tm, N), lambda i: (i, 0))],
        out_specs=pl.BlockSpec((tm, N), lambda i: (i, 0)),
        out_shape=jax.ShapeDtypeStruct((M, N), F32), compiler_params=_cp("parallel"),
    )(a, w4, res)


def mm_cols(a, w, layer, name, out_dtype):
    M = a.shape[0]
    n = w.shape[3]

    def body(a_ref, w_ref, o_ref):
        o_ref[...] = _dot_cols(a_ref[...], w_ref).astype(out_dtype)

    return pl.pallas_call(
        body, name=name, grid=(1,), in_specs=[_full(a.shape), _wspec(w, layer)],
        out_specs=_full((M, N_CHIPS * n)), out_shape=jax.ShapeDtypeStruct((M, N_CHIPS * n), out_dtype),
        compiler_params=_cp("arbitrary"),
    )(a, w)


def mm_nt_cols(dy, w, layer, name):
    M = dy.shape[0]
    K, n = w.shape[2], w.shape[3]

    def body(d_ref, w_ref, o_ref):
        o_ref[...] = _dot_nt_cols(_col_pieces(d_ref[...].astype(BF16), n), w_ref)

    return pl.pallas_call(
        body, name=name, grid=(1,), in_specs=[_full(dy.shape), _wspec(w, layer)],
        out_specs=_full((M, K)), out_shape=jax.ShapeDtypeStruct((M, K), F32), compiler_params=_cp("arbitrary"),
    )(dy, w)


def mm_nt_rows(dy, w4, name, tm=MM_TILE):
    M, N = dy.shape
    K = N_CHIPS * w4.shape[1]
    tm = min(tm, M)

    def body(d_ref, w_ref, o_ref):
        o_ref[...] = _dot_nt_rows(d_ref[...].astype(BF16), w_ref)

    return pl.pallas_call(
        body, name=name, grid=(M // tm,),
        in_specs=[pl.BlockSpec((tm, N), lambda i: (i, 0)), _full(w4.shape)],
        out_specs=pl.BlockSpec((tm, K), lambda i: (i, 0)),
        out_shape=jax.ShapeDtypeStruct((M, K), F32), compiler_params=_cp("parallel"),
    )(dy, w4)


def mm_nt_normbwd(dys, offs, w4, x, g, dx_out, name, tm=MM_TILE):
    M, D = x.shape
    n = w4.shape[2]
    tm = min(tm, M)
    nd = len(dys)

    def body(*refs):
        d_refs = refs[:nd]
        w_ref, x_ref, g_ref, dxo_ref, dx_ref, dg_ref = refs[nd:]
        if nd == 1:
            pieces = _col_pieces(d_refs[0][...].astype(BF16), n)
        else:
            pieces = [r[...].astype(BF16) for r in d_refs]
        dh = _dot_nt_cols(pieces, w_ref)
        dx, dg = _rms_bwd(x_ref[...], g_ref[...], dh)
        dx_ref[...] = dxo_ref[...] + dx

        @pl.when(pl.program_id(0) == 0)
        def _():
            dg_ref[...] = jnp.zeros_like(dg_ref)

        dg_ref[...] += dg

    row = lambda c, off=0: pl.BlockSpec((tm, c), lambda i, p=off // tm: (i + p, 0))
    return pl.pallas_call(
        body, name=name, grid=(M // tm,),
        in_specs=[row(d.shape[1], off) for d, off in zip(dys, offs)] + [_full(w4.shape), row(D), _full(g.shape), row(D)],
        out_specs=[row(D), _full((1, D))],
        out_shape=[jax.ShapeDtypeStruct((M, D), F32), jax.ShapeDtypeStruct((1, D), F32)],
        compiler_params=_cp("arbitrary"),
    )(*dys, w4, x, g, dx_out)


def mm_tn(a, b, name, layout, into=None, b_off=0, out_dtype=BF16, bm=1024, bn=1280, bl=1024, carried=None):
    L, K = a.shape
    N = b.shape[1]
    kind = layout[0]
    arg = layout[1] if len(layout) > 1 else None
    bm, bn, bl = _lane_tile(K, bm), _lane_tile(N, bn), min(bl, L)
    assert L % bl == 0 and b_off % bl == 0, (L, bl, b_off)
    nl = L // bl
    n_sh, r_sh = N // N_CHIPS, K // N_CHIPS
    lay = (None,) if arg is None else (None, None)
    mid = () if arg is None else (arg,)
    gs = 1
    if kind == "plain":
        oshape, oblock, oidx = (K, N), (bm, bn), lambda i, j, l: (i, j)
    elif kind == "slab":
        oshape, oblock, oidx = (N_CHIPS, K, N), (None, bm, bn), lambda i, j, l: (arg, i, j)
    elif kind == "cols":
        bn = max(bn - bn % n_sh, n_sh) if bn >= n_sh else _lane_tile(n_sh, bn)
        gs = max(bn // n_sh, 1)
        per = n_sh // bn if gs == 1 else 1
        oshape = (N_CHIPS,) + ((2,) if arg is not None else ()) + (K, n_sh)
        oblock = ((gs,) if gs > 1 else (None,)) + lay[1:] + (bm, min(bn, n_sh))
        oidx = lambda i, j, l: (j // per,) + mid + (i, j % per)
    else:
        bm = max(bm - bm % r_sh, r_sh) if bm >= r_sh else _lane_tile(r_sh, bm)
        gs = max(bm // r_sh, 1)
        per = r_sh // bm if gs == 1 else 1
        oshape = (N_CHIPS,) + ((2,) if arg is not None else ()) + (r_sh, N)
        oblock = ((gs,) if gs > 1 else (None,)) + lay[1:] + (min(bm, r_sh), bn)
        oidx = lambda i, j, l: (i // per,) + mid + (i % per, j)
    assert K % bm == 0 and N % bn == 0, (K, bm, N, bn)

    grid = (K // bm, N // bn, nl)

    def body(*refs):
        top = end = None
        if carried is not None:
            refs, parts = carried.split(refs, 2 if into is None else 3, 1, 1)
            top, end = carried.hooks(parts, grid)
            top()
        a_ref, b_ref, o_ref, acc = refs[0], refs[1], refs[-2], refs[-1]
        l = pl.program_id(2)

        @pl.when(l == 0)
        def _():
            acc[...] = jnp.zeros_like(acc)

        acc[...] += _dot_tn(a_ref[...].astype(BF16), b_ref[...].astype(BF16))

        @pl.when(l == nl - 1)
        def _():
            if gs == 1:
                o_ref[...] = acc[...].astype(out_dtype)
            elif kind == "cols":
                for t in range(gs):
                    o_ref[t] = acc[:, t * n_sh:(t + 1) * n_sh].astype(out_dtype)
            else:
                for t in range(gs):
                    o_ref[t] = acc[t * r_sh:(t + 1) * r_sh, :].astype(out_dtype)

        if end is not None:
            end()

    in_specs = [pl.BlockSpec((bl, bm), lambda i, j, l: (l, i)),
                pl.BlockSpec((bl, bn), lambda i, j, l, p=b_off // bl: (l + p, j))]
    args = [a, b]
    alias = {}
    if into is not None:
        in_specs.append(ANY)
        args.append(into)
        alias = {2: 0}
    out_specs, out_shape = pl.BlockSpec(oblock, oidx), jax.ShapeDtypeStruct(oshape, out_dtype)
    scratch = [pltpu.VMEM((bm, bn), F32)]
    if carried is None:
        sem = ("parallel", "parallel", "arbitrary")
    else:
        in_specs += [ANY] * len(carried.arrays)
        args += carried.arrays
        out_specs, out_shape = [out_specs] + [ANY] * len(carried.out_shapes), [out_shape] + carried.out_shapes
        scratch += carried.sems
        sem = ("arbitrary",) * 3
    return pl.pallas_call(
        body, name=name, grid=grid, in_specs=in_specs, out_specs=out_specs, out_shape=out_shape,
        scratch_shapes=scratch, input_output_aliases=alias, compiler_params=_cp(*sem),
    )(*args)


def rms_fwd(x, g, name):
    def body(x_ref, g_ref, ob_ref):
        _, xhat = _rms_parts(x_ref[...])
        ob_ref[...] = (xhat * g_ref[...]).astype(BF16)

    return pl.pallas_call(body, name=name, out_shape=jax.ShapeDtypeStruct(x.shape, BF16))(x, g)


def rms_dgain(x, dy0, dy1, name):
    def body(x_ref, d0_ref, d1_ref, o_ref):
        _, xhat = _rms_parts(x_ref[...])
        o_ref[...] = jnp.sum((d0_ref[...] + d1_ref[...]) * xhat, axis=0, keepdims=True)

    return pl.pallas_call(body, name=name, out_shape=jax.ShapeDtypeStruct((1, x.shape[1]), F32))(x, dy0, dy1)


def _s5_discretise(lr, li, logdt, bt_re, bt_im):
    dt = jnp.exp(logdt)
    mag = jnp.exp(lr * dt)
    ab_re = mag * jnp.cos(li * dt)
    ab_im = mag * jnp.sin(li * dt)
    den = lr * lr + li * li
    nr = ab_re - 1.0
    coef_re = (nr * lr + ab_im * li) / den
    coef_im = (ab_im * lr - nr * li) / den
    cr = coef_re[:, None, :]
    ci = coef_im[:, None, :]
    bb_re = cr * bt_re - ci * bt_im
    bb_im = cr * bt_im + ci * bt_re
    return ab_re, ab_im, bb_re, bb_im


def s5_param_fwd(lr, li, logdt, bt_re, bt_im):
    def body(lr_ref, li_ref, ld_ref, br_ref, bi_ref, bbr_ref, bbi_ref):
        _, _, bb_re, bb_im = _s5_discretise(lr_ref[...], li_ref[...], ld_ref[...], br_ref[...], bi_ref[...])
        bbr_ref[...] = bb_re
        bbi_ref[...] = bb_im

    sh = jax.ShapeDtypeStruct(bt_re.shape, F32)
    return pl.pallas_call(body, name="s5_param_fwd", out_shape=[sh, sh])(lr, li, logdt, bt_re, bt_im)


def s5_param_bwd(lr, li, logdt, bt_re, bt_im, d_ab_re, d_ab_im, d_bb_re, d_bb_im):
    def body(lr_ref, li_ref, ld_ref, br_ref, bi_ref, dar_ref, dai_ref, dbr_ref, dbi_ref,
             o_lr, o_li, o_ld, o_br, o_bi):
        _, vjp = jax.vjp(_s5_discretise, lr_ref[...], li_ref[...], ld_ref[...], br_ref[...], bi_ref[...])
        g = vjp((dar_ref[...], dai_ref[...], dbr_ref[...], dbi_ref[...]))
        for o, v in zip((o_lr, o_li, o_ld), g[:3]):
            o[...] = v
        for o, v in zip((o_br, o_bi), g[3:]):
            for c in range(S5_GROUP):
                o[:, c * S5_STATE:(c + 1) * S5_STATE] = v[:, c, :]

    dense = jax.ShapeDtypeStruct((S5_GROUPS, S5_GROUP * S5_STATE), F32)
    shapes = [jax.ShapeDtypeStruct(a.shape, F32) for a in (lr, li, logdt)] + [dense, dense]
    return pl.pallas_call(body, name="s5_param_bwd", out_shape=shapes)(
        lr, li, logdt, bt_re, bt_im, d_ab_re, d_ab_im, d_bb_re, d_bb_im)


def s5_tables(lr_flat, li_flat, logdt_flat):
    def body(lr_ref, li_ref, ld_ref, tab_ref):
        dt = jnp.exp(ld_ref[...])
        a = lr_ref[...] * dt
        th = li_ref[...] * dt
        row = lax.broadcasted_iota(jnp.int32, (8, 1), 0)
        rowf = row.astype(F32)

        def power(e, sign):
            m = jnp.exp(e * a)
            return m * jnp.cos(e * th), sign * m * jnp.sin(e * th)

        k = 0
        for sign, fwd in ((1.0, True), (-1.0, False)):
            for s in (1, 2, 4):
                pr, pi = power(jnp.full((8, 1), float(s), F32), sign)
                keep = (row >= s) if fwd else (row + s < 8)
                tab_ref[k] = jnp.where(keep, pr, 0.0)
                tab_ref[k + 1] = jnp.where(keep, pi, 0.0)
                k += 2
            e = rowf + 1.0 if fwd else 8.0 - rowf
            pr, pi = power(e, sign)
            tab_ref[k] = pr
            tab_ref[k + 1] = pi
            k += 2

    return pl.pallas_call(body, name="s5_tables",
                          out_shape=jax.ShapeDtypeStruct((16, 8, S5_COLS), F32))(lr_flat, li_flat, logdt_flat)


def _scan_block(a, b, tabs, base, cr, ci, reverse):
    for n, s in enumerate((1, 2, 4)):
        mr = tabs[base + 2 * n]
        mi = tabs[base + 2 * n + 1]
        sh = (8 - s) if reverse else s
        ar = pltpu.roll(a, sh, 0)
        br = pltpu.roll(b, sh, 0)
        a, b = a + mr * ar - mi * br, b + mr * br + mi * ar
    pr = tabs[base + 6]
    pi = tabs[base + 7]
    a, b = a + pr * cr - pi * ci, b + pr * ci + pi * cr
    return a, b


class Carried:
    def __init__(self, arrays, out_shapes, sems, start, finish):
        self.arrays, self.out_shapes, self.sems = list(arrays), list(out_shapes), list(sems)
        self.start, self.finish = start, finish

    def split(self, refs, n_in, n_out, n_scratch):
        a, o, s = len(self.arrays), len(self.out_shapes), len(self.sems)
        own_in, car_in = refs[:n_in], refs[n_in:n_in + a]
        own_out, car_out = refs[n_in + a:n_in + a + n_out], refs[n_in + a + n_out:n_in + a + n_out + o]
        rest = refs[n_in + a + n_out + o:]
        return own_in + own_out + rest[:n_scratch], (car_in, car_out, rest[n_scratch:n_scratch + s])

    def hooks(self, parts, grid):
        first = last = None
        for k, n in enumerate(grid):
            i = pl.program_id(k)
            first = (i == 0) if first is None else first & (i == 0)
            last = (i == n - 1) if last is None else last & (i == n - 1)

        def top():
            pl.when(first)(lambda: self.start(*parts))

        def end():
            pl.when(last)(lambda: self.finish(*parts))

        return top, end


def s5_fwd(z, bbd_re, bbd_im, ccd_re, ccd_im, tab, dskip, tm=S5_TILE, carried=None):
    L = z.shape[0]
    tm = min(tm, L)
    nt = L // tm

    def body(*refs):
        top = end = None
        if carried is not None:
            refs, parts = carried.split(refs, 7, 4, 3)
            top, end = carried.hooks(parts, (S5_SPLIT, nt))
            top()
        u_ref, bbr_ref, bbi_ref, ccr_ref, cci_ref, tab_ref, d_ref, y_ref, ck_ref, hr_ref, hi_ref, xr, xi, car = refs
        t = pl.program_id(1)

        @pl.when(t == 0)
        def _():
            car[...] = jnp.zeros_like(car)

        u = u_ref[...]
        ub = u.astype(BF16)
        xr[...] = _dot(ub, bbr_ref[...])
        xi[...] = _dot(ub, bbi_ref[...])
        tabs = [tab_ref[k] for k in range(8)]

        def blk(i, c):
            r0 = pl.multiple_of(i * 8, 8)
            a, b = _scan_block(xr[pl.ds(r0, 8), :], xi[pl.ds(r0, 8), :], tabs, 0, c[0], c[1], False)
            xr[pl.ds(r0, 8), :] = a
            xi[pl.ds(r0, 8), :] = b
            return a[7:8, :], b[7:8, :]

        cr, ci = lax.fori_loop(0, tm // 8, blk, (car[0:1, :], car[1:2, :]))
        car[0:1, :] = cr
        car[1:2, :] = ci
        ck_ref[0:1, :] = cr
        ck_ref[1:2, :] = ci
        hrb = xr[...].astype(BF16)
        hib = xi[...].astype(BF16)
        hr_ref[...] = hrb
        hi_ref[...] = hib
        y_ref[...] = _dot(hrb, ccr_ref[...]) - _dot(hib, cci_ref[...]) + d_ref[...] * u
        if end is not None:
            end()

    extra = carried.arrays if carried is not None else []
    extra_out = carried.out_shapes if carried is not None else []
    extra_sems = carried.sems if carried is not None else []
    return pl.pallas_call(
        body, name="s5_fwd", grid=(S5_SPLIT, nt),
        in_specs=[pl.BlockSpec((tm, S5_UC), lambda j, t: (t, j)),
                  pl.BlockSpec((None, S5_UC, S5_CC), lambda j, t: (j, 0, 0)),
                  pl.BlockSpec((None, S5_UC, S5_CC), lambda j, t: (j, 0, 0)),
                  pl.BlockSpec((None, S5_CC, S5_UC), lambda j, t: (j, 0, 0)),
                  pl.BlockSpec((None, S5_CC, S5_UC), lambda j, t: (j, 0, 0)),
                  pl.BlockSpec((8, 8, S5_CC), lambda j, t: (0, 0, j)),
                  pl.BlockSpec((1, S5_UC), lambda j, t: (0, j))] + [ANY] * len(extra),
        out_specs=[pl.BlockSpec((tm, S5_UC), lambda j, t: (t, j)),
                   pl.BlockSpec((None, 2, S5_CC), lambda j, t: (t, 0, j)),
                   pl.BlockSpec((tm, S5_CC), lambda j, t: (t, j)),
                   pl.BlockSpec((tm, S5_CC), lambda j, t: (t, j))] + [ANY] * len(extra_out),
        out_shape=[jax.ShapeDtypeStruct((L, S5_WIDTH), F32), jax.ShapeDtypeStruct((nt, 2, S5_COLS), F32),
                   jax.ShapeDtypeStruct((L, S5_COLS), BF16), jax.ShapeDtypeStruct((L, S5_COLS), BF16)] + extra_out,
        scratch_shapes=[pltpu.VMEM((tm, S5_CC), F32), pltpu.VMEM((tm, S5_CC), F32), pltpu.VMEM((2, S5_CC), F32)]
        + extra_sems,
        compiler_params=_cp("arbitrary" if carried is not None else "parallel", "arbitrary"),
    )(z, bbd_re, bbd_im, ccd_re, ccd_im, tab, dskip, *extra)


def s5_bwd(z, dy, dz, ckpt, hrb, hib, bbd_re, bbd_im, ccd_re, ccd_im, tab, dskip, tm=S5_TILE, carried=None):
    L = z.shape[0]
    tm = min(tm, L)
    nt = L // tm

    def body(*refs):
        top = end = None
        if carried is not None:
            refs, parts = carried.split(refs, 12, 7, 7)
            top, end = carried.hooks(parts, (S5_SPLIT, nt))
            top()
        (u_ref, dy_ref, dz_ref, ck_ref, hrb_ref, hib_ref, bbr_ref, bbi_ref, ccr_ref, cci_ref, tab_ref, d_ref,
         du_ref, da_ref, dbr_ref, dbi_ref, dcr_ref, dci_ref, dd_ref, hr, hi, gr, gi, car, acr, aci) = refs
        t = pl.program_id(1)
        tt = nt - 1 - t

        @pl.when(t == 0)
        def _():
            for r in (car, acr, aci, dbr_ref, dbi_ref, dcr_ref, dci_ref, dd_ref):
                r[...] = jnp.zeros_like(r)

        u = u_ref[...]
        ub = u.astype(BF16)
        dyv = dy_ref[...]
        dyb = dyv.astype(BF16)
        tabs = [None] * 8 + [tab_ref[k] for k in range(8, 16)]

        live = (tt > 0).astype(F32)
        hr[0:8, :] = jnp.broadcast_to(ck_ref[0:1, :] * live, (8, S5_CC))
        hi[0:8, :] = jnp.broadcast_to(ck_ref[1:2, :] * live, (8, S5_CC))
        hrb = hrb_ref[...]
        hib = hib_ref[...]
        hr[8:, :] = hrb.astype(F32)
        hi[8:, :] = hib.astype(F32)
        dcr_ref[...] += _dot_tn(hrb, dyb)
        dci_ref[...] -= _dot_tn(hib, dyb)

        gr[...] = _dot_nt(dyb, ccr_ref[...])
        gi[...] = -_dot_nt(dyb, cci_ref[...])
        row0 = lax.broadcasted_iota(jnp.int32, (8, S5_CC), 0) == 0

        def rblk(k, c):
            i = tm // 8 - 1 - k
            r0 = pl.multiple_of(i * 8, 8)
            a, b = _scan_block(gr[pl.ds(r0, 8), :], gi[pl.ds(r0, 8), :], tabs, 8, c[0], c[1], True)
            gr[pl.ds(r0, 8), :] = a
            gi[pl.ds(r0, 8), :] = b
            r1 = pl.multiple_of(i * 8 + 8, 8)
            hpr = jnp.where(row0, pltpu.roll(hr[pl.ds(r0, 8), :], 1, 0), pltpu.roll(hr[pl.ds(r1, 8), :], 1, 0))
            hpi = jnp.where(row0, pltpu.roll(hi[pl.ds(r0, 8), :], 1, 0), pltpu.roll(hi[pl.ds(r1, 8), :], 1, 0))
            acr[...] += a * hpr + b * hpi
            aci[...] += b * hpr - a * hpi
            return a[0:1, :], b[0:1, :]

        cr, ci = lax.fori_loop(0, tm // 8, rblk, (car[0:1, :], car[1:2, :]))
        car[0:1, :] = cr
        car[1:2, :] = ci

        grb = gr[...].astype(BF16)
        gib = gi[...].astype(BF16)
        du_ref[...] = (_dot_nt(grb, bbr_ref[...]) + _dot_nt(gib, bbi_ref[...]) + d_ref[...] * dyv).astype(BF16)
        dbr_ref[...] += _dot_tn(ub, grb)
        dbi_ref[...] += _dot_tn(ub, gib)
        dd_ref[...] += jnp.sum(dyv * u, axis=0, keepdims=True)

        @pl.when(t == nt - 1)
        def _():
            da_ref[0:1, :] = jnp.sum(acr[...], axis=0, keepdims=True)
            da_ref[1:2, :] = jnp.sum(aci[...], axis=0, keepdims=True)

        if end is not None:
            end()

    extra = carried.arrays if carried is not None else []
    extra_out = carried.out_shapes if carried is not None else []
    extra_sems = carried.sems if carried is not None else []
    chunk = lambda a, b: pl.BlockSpec((None, a, b), lambda j, t: (j, 0, 0))
    return pl.pallas_call(
        body, name="s5_bwd", grid=(S5_SPLIT, nt),
        in_specs=[pl.BlockSpec((tm, S5_UC), lambda j, t: (nt - 1 - t, j)),
                  pl.BlockSpec((tm, S5_UC), lambda j, t: (nt - 1 - t, j)),
                  ANY,
                  pl.BlockSpec((None, 2, S5_CC), lambda j, t: (jnp.maximum(nt - 2 - t, 0), 0, j)),
                  pl.BlockSpec((tm, S5_CC), lambda j, t: (nt - 1 - t, j)),
                  pl.BlockSpec((tm, S5_CC), lambda j, t: (nt - 1 - t, j)),
                  chunk(S5_UC, S5_CC), chunk(S5_UC, S5_CC), chunk(S5_CC, S5_UC), chunk(S5_CC, S5_UC),
                  pl.BlockSpec((16, 8, S5_CC), lambda j, t: (0, 0, j)),
                  pl.BlockSpec((1, S5_UC), lambda j, t: (0, j))] + [ANY] * len(extra),
        out_specs=[pl.BlockSpec((tm, S5_UC), lambda j, t: (nt - 1 - t, j)),
                   pl.BlockSpec((None, 2, S5_CC), lambda j, t: (j, 0, 0)),
                   chunk(S5_UC, S5_CC), chunk(S5_UC, S5_CC), chunk(S5_CC, S5_UC), chunk(S5_CC, S5_UC),
                   pl.BlockSpec((1, S5_UC), lambda j, t: (0, j))] + [ANY] * len(extra_out),
        out_shape=[jax.ShapeDtypeStruct(dz.shape, dz.dtype),
                   jax.ShapeDtypeStruct((S5_SPLIT, 2, S5_CC), F32),
                   jax.ShapeDtypeStruct((S5_SPLIT, S5_UC, S5_CC), F32),
                   jax.ShapeDtypeStruct((S5_SPLIT, S5_UC, S5_CC), F32),
                   jax.ShapeDtypeStruct((S5_SPLIT, S5_CC, S5_UC), F32),
                   jax.ShapeDtypeStruct((S5_SPLIT, S5_CC, S5_UC), F32),
                   jax.ShapeDtypeStruct((1, S5_WIDTH), F32)] + extra_out,
        scratch_shapes=[pltpu.VMEM((tm + 8, S5_CC), F32), pltpu.VMEM((tm + 8, S5_CC), F32),
                        pltpu.VMEM((tm, S5_CC), F32), pltpu.VMEM((tm, S5_CC), F32),
                        pltpu.VMEM((2, S5_CC), F32), pltpu.VMEM((8, S5_CC), F32), pltpu.VMEM((8, S5_CC), F32)]
        + extra_sems,
        input_output_aliases={2: 0},
        compiler_params=_cp("arbitrary" if carried is not None else "parallel", "arbitrary"),
    )(z, dy, dz, ckpt, hrb, hib, bbd_re, bbd_im, ccd_re, ccd_im, tab, dskip, *extra)


_EYE8 = np.eye(S5_GROUPS // S5_SPLIT, dtype=np.float32)


def _blockdiag(a):
    g, r, c = a.shape
    a = a.reshape(S5_SPLIT, g // S5_SPLIT, r, c)
    out = a[:, :, :, None, :] * _EYE8[None, :, None, :, None].astype(a.dtype)
    return out.reshape(S5_SPLIT, (g // S5_SPLIT) * r, (g // S5_SPLIT) * c)


def _blockdiag_extract(a, r, c):
    n = S5_GROUPS // S5_SPLIT
    a = a.reshape(S5_SPLIT, n, r, n, c)
    d = jnp.stack([a[:, k, :, k, :] for k in range(n)], axis=1)
    return d.reshape(S5_GROUPS, r, c)


def s5_mixer_core_fwd(z, lam_re, lam_im, log_dt, b_re, b_im, c_re, c_im, d_skip, carried=None):
    bt_re = jnp.swapaxes(b_re, 1, 2)
    bt_im = jnp.swapaxes(b_im, 1, 2)
    logdt = log_dt.reshape(S5_GROUPS, 1)
    bb_re, bb_im = s5_param_fwd(lam_re, lam_im, logdt, bt_re, bt_im)
    flat = lambda a: a.reshape(1, S5_COLS)
    tab = s5_tables(flat(lam_re), flat(lam_im), flat(jnp.broadcast_to(logdt, (S5_GROUPS, S5_STATE))))
    bbd_re = _blockdiag(bb_re).astype(BF16)
    bbd_im = _blockdiag(bb_im).astype(BF16)
    ccd_re = _blockdiag(jnp.swapaxes(c_re, 1, 2)).astype(BF16)
    ccd_im = _blockdiag(jnp.swapaxes(c_im, 1, 2)).astype(BF16)
    dsk = d_skip.reshape(1, S5_WIDTH)
    y, ckpt, hrb, hib, *landed = s5_fwd(z, bbd_re, bbd_im, ccd_re, ccd_im, tab, dsk, carried=carried)
    saved = (logdt, bt_re, bt_im, bbd_re, bbd_im, ccd_re, ccd_im, tab, dsk, ckpt, hrb, hib)
    return y, saved, landed


def s5_b_from_dense(dense):
    return jnp.swapaxes(dense.reshape(S5_GROUPS, S5_GROUP, S5_STATE), 1, 2)


def s5_mixer_core_bwd(z, dy, dz, lam_re, lam_im, saved, carried=None):
    logdt, bt_re, bt_im, bbd_re, bbd_im, ccd_re, ccd_im, tab, dsk, ckpt, hrb, hib = saved
    dz, da, dbr, dbi, dcr, dci, dd, *landed = s5_bwd(z, dy, dz, ckpt, hrb, hib, bbd_re, bbd_im, ccd_re, ccd_im, tab,
                                                     dsk, carried=carried)
    d_ab_re = da[:, 0, :].reshape(S5_GROUPS, S5_STATE)
    d_ab_im = da[:, 1, :].reshape(S5_GROUPS, S5_STATE)
    d_bb_re = _blockdiag_extract(dbr, S5_GROUP, S5_STATE)
    d_bb_im = _blockdiag_extract(dbi, S5_GROUP, S5_STATE)
    g_lr, g_li, g_ld, g_btr, g_bti = s5_param_bwd(lam_re, lam_im, logdt, bt_re, bt_im,
                                                  d_ab_re, d_ab_im, d_bb_re, d_bb_im)
    g_cre = jnp.swapaxes(_blockdiag_extract(dcr, S5_STATE, S5_GROUP), 1, 2)
    g_cim = jnp.swapaxes(_blockdiag_extract(dci, S5_STATE, S5_GROUP), 1, 2)
    grads = dict(lambda_re=g_lr, lambda_im=g_li, log_dt=g_ld.reshape(S5_GROUPS), b_re=g_btr, b_im=g_bti,
                 c_re=g_cre, c_im=g_cim, d=dd.reshape(S5_WIDTH))
    return dz, grads, landed


Z_U, Z_GA, Z_VAL, Z_GLU, Z_GB = range(5)
SUBLANES = 8


def _shifted_copies(buf, tm):
    n = tm + CONV_HALO - SUBLANES
    for r in range(1, SUBLANES):
        buf[r, 0:n, :] = buf[0, pl.ds(r, n), :]


CONV_ROWS = 32


def _shifted_rows(buf, start, rows, base=0):
    return buf[start % SUBLANES, pl.ds(base + (start - start % SUBLANES), rows), :]


def conv_fwd(z, conv_w, conv_b, tm=ROW_TILE):
    L = z.shape[0]
    tm = min(tm, L)
    nt = L // tm
    hb = tm // CONV_HALO
    C = CONV_WIDTH

    def body(val_ref, glu_ref, valh_ref, gluh_ref, w_ref, b_ref, c_ref, vsh):
        live = (pl.program_id(0) > 0).astype(F32)
        vsh[0, 0:CONV_HALO, :] = valh_ref[...] * _sigmoid(gluh_ref[...]) * live
        vsh[0, CONV_HALO:, :] = val_ref[...] * _sigmoid(glu_ref[...])
        _shifted_copies(vsh, tm)

        def rows(i, carry):
            base = pl.multiple_of(i * CONV_ROWS, CONV_ROWS)
            acc = jnp.broadcast_to(b_ref[...], (CONV_ROWS, C))
            for k in range(CONV_KERNEL):
                acc = acc + w_ref[k:k + 1, :] * _shifted_rows(vsh, CONV_HALO - CONV_KERNEL + 1 + k, CONV_ROWS, base)
            c_ref[pl.ds(base, CONV_ROWS), :] = acc
            return carry

        lax.fori_loop(0, tm // CONV_ROWS, rows, 0)

    cur = lambda col: pl.BlockSpec((tm, C), lambda t: (t, col))
    prev = lambda col: pl.BlockSpec((CONV_HALO, C), lambda t: (jnp.maximum(t * hb - 1, 0), col))
    return pl.pallas_call(
        body, name="conv_fwd", grid=(nt,),
        in_specs=[cur(Z_VAL), cur(Z_GLU), prev(Z_VAL), prev(Z_GLU), _full(conv_w.shape), _full(conv_b.shape)],
        out_specs=pl.BlockSpec((tm, C), lambda t: (t, 0)),
        out_shape=jax.ShapeDtypeStruct((L, C), F32),
        scratch_shapes=[pltpu.VMEM((8, tm + CONV_HALO, C), F32)],
        compiler_params=_cp("parallel"),
    )(z, z, z, z, conv_w, conv_b)


def conv_bwd(z, dc, dz, conv_w, tm=ROW_TILE, carried=None):
    L = z.shape[0]
    tm = min(tm, L)
    nt = L // tm
    hb = tm // CONV_HALO
    nh = L // CONV_HALO
    C = CONV_WIDTH
    off = CONV_HALO - CONV_KERNEL + 1

    def body(*refs):
        top = end = None
        if carried is not None:
            refs, parts = carried.split(refs, 8, 3, 3)
            top, end = carried.hooks(parts, (nt,))
            top()
        val_ref, glu_ref, valh_ref, gluh_ref, dc_ref, dcn_ref, dz_ref, w_ref, dvg_ref, dw_ref, db_ref, vsh, dsh, wacc = refs
        t = pl.program_id(0)

        @pl.when(t == 0)
        def _():
            wacc[...] = jnp.zeros_like(wacc)
            db_ref[...] = jnp.zeros_like(db_ref)

        val = val_ref[...]
        sg = _sigmoid(glu_ref[...])
        vsh[0, 0:CONV_HALO, :] = valh_ref[...] * _sigmoid(gluh_ref[...]) * (t > 0).astype(F32)
        vsh[0, CONV_HALO:, :] = val * sg
        dcv = dc_ref[...]
        dsh[0, 0:tm, :] = dcv
        dsh[0, tm:, :] = dcn_ref[...] * (t < nt - 1).astype(F32)
        _shifted_copies(vsh, tm)
        _shifted_copies(dsh, tm)

        def rows(i, carry):
            base = pl.multiple_of(i * CONV_ROWS, CONV_ROWS)
            dcr = dc_ref[pl.ds(base, CONV_ROWS), :]
            dv = jnp.zeros((CONV_ROWS, C), F32)
            for k in range(CONV_KERNEL):
                dv = dv + w_ref[k:k + 1, :] * _shifted_rows(dsh, CONV_KERNEL - 1 - k, CONV_ROWS, base)
                prod = dcr * _shifted_rows(vsh, off + k, CONV_ROWS, base)
                wacc[k] += jnp.sum(prod.reshape(CONV_ROWS // SUBLANES, SUBLANES, C), axis=0)
            valr = val_ref[pl.ds(base, CONV_ROWS), :]
            sgr = _sigmoid(glu_ref[pl.ds(base, CONV_ROWS), :])
            dvg_ref[pl.ds(base, CONV_ROWS), 0:C] = (dv * sgr).astype(BF16)
            dvg_ref[pl.ds(base, CONV_ROWS), C:] = (dv * valr * sgr * (1.0 - sgr)).astype(BF16)
            return carry

        lax.fori_loop(0, tm // CONV_ROWS, rows, 0)
        db_ref[...] += jnp.sum(dcv, axis=0, keepdims=True)

        @pl.when(t == nt - 1)
        def _():
            dw_ref[...] = jnp.sum(wacc[...], axis=1)

        if end is not None:
            end()

    extra = carried.arrays if carried is not None else []
    extra_out = carried.out_shapes if carried is not None else []
    extra_sems = carried.sems if carried is not None else []
    cur = lambda col: pl.BlockSpec((tm, C), lambda t: (t, col))
    prev = lambda col: pl.BlockSpec((CONV_HALO, C), lambda t: (jnp.maximum(t * hb - 1, 0), col))
    nxt = pl.BlockSpec((CONV_HALO, C), lambda t: (jnp.minimum((t + 1) * hb, nh - 1), 0))
    row = pl.BlockSpec((tm, C), lambda t: (t, 0))
    return pl.pallas_call(
        body, name="conv_bwd", grid=(nt,),
        in_specs=[cur(Z_VAL), cur(Z_GLU), prev(Z_VAL), prev(Z_GLU), row, nxt, ANY, _full(conv_w.shape)]
        + [ANY] * len(extra),
        out_specs=[pl.BlockSpec((tm, 2 * C), lambda t: (t, 1)), _full((CONV_HALO, C)), _full((1, C))]
        + [ANY] * len(extra_out),
        out_shape=[jax.ShapeDtypeStruct(dz.shape, dz.dtype),
                   jax.ShapeDtypeStruct((CONV_HALO, C), F32), jax.ShapeDtypeStruct((1, C), F32)] + extra_out,
        scratch_shapes=[pltpu.VMEM((8, tm + CONV_HALO, C), F32), pltpu.VMEM((8, tm + CONV_HALO, C), F32),
                        pltpu.VMEM((CONV_HALO, SUBLANES, C), F32)] + extra_sems,
        input_output_aliases={6: 0},
        compiler_params=_cp("arbitrary"),
    )(z, z, z, z, dc, dc, dz, conv_w, *extra)


def _ln_parts(c):
    mu = jnp.mean(c, axis=-1, keepdims=True)
    cc = c - mu
    rstd = lax.rsqrt(jnp.mean(cc * cc, axis=-1, keepdims=True) + EPS)
    return rstd, cc * rstd


def _ev_tail_branches(ys, c, wglu, bglu, lng, lnb):
    z1 = _gelu(ys)
    z1b = z1.astype(BF16)
    sg = _sigmoid(_dot_rows(z1b, wglu) + bglu)
    out = z1 * sg
    rstd, chat = _ln_parts(c)
    cn = chat * lng + lnb
    return z1, z1b, sg, out, rstd, chat, cn


def ev_tail_fwd(ys, z, c, x0, wglu, bglu, lng, lnb, wout, tm=ROW_TILE):
    L, D = x0.shape
    tm = min(tm, L)
    W = S5_WIDTH

    def body(ys_ref, ga_ref, c_ref, gb_ref, x_ref, wglu_ref, bglu_ref, lng_ref, lnb_ref, wout_ref, o_ref):
        _, _, _, out, _, _, cn = _ev_tail_branches(ys_ref[...], c_ref[...], wglu_ref, bglu_ref[...],
                                                   lng_ref[...], lnb_ref[...])
        ya = (out * _silu(ga_ref[...])).astype(BF16)
        yb = (_silu(cn) * _silu(gb_ref[...])).astype(BF16)
        o_ref[...] = x_ref[...] + _dot_rows(jnp.concatenate([ya, yb], axis=1), wout_ref)

    row = lambda n, col=0: pl.BlockSpec((tm, n), lambda t: (t, col))
    return pl.pallas_call(
        body, name="ev_tail_fwd", grid=(L // tm,),
        in_specs=[row(W), row(W, Z_GA), row(W), row(W, Z_GB), row(D), _full(wglu.shape), _full(bglu.shape),
                  _full(lng.shape), _full(lnb.shape), _full(wout.shape)],
        out_specs=row(D), out_shape=jax.ShapeDtypeStruct((L, D), F32), compiler_params=_cp("parallel"),
    )(ys, z, c, z, x0, wglu, bglu, lng, lnb, wout)


def ev_tail_bwd(ys, z, c, dx1, wglu, bglu, lng, lnb, wout, tm=ROW_TILE):
    L, D = dx1.shape
    tm = min(tm, L)
    W = S5_WIDTH

    def body(ys_ref, ga_ref, c_ref, gb_ref, dx_ref, wglu_ref, bglu_ref, lng_ref, lnb_ref, wout_ref,
             dys_ref, dc_ref, dz_ref, r_ref, z1_ref, dt_ref, dbg_ref, dlg_ref, dlb_ref):
        @pl.when(pl.program_id(0) == 0)
        def _():
            for r in (dbg_ref, dlg_ref, dlb_ref):
                r[...] = jnp.zeros_like(r)

        ys, ga, gb = ys_ref[...], ga_ref[...], gb_ref[...]
        z1, z1b, sg, out, rstd, chat, cn = _ev_tail_branches(ys, c_ref[...], wglu_ref, bglu_ref[...],
                                                             lng_ref[...], lnb_ref[...])
        (sga, dsga), (sgb, dsgb), (scn, dscn) = _silu_pair(ga), _silu_pair(gb), _silu_pair(cn)
        r_ref[:, 0:W] = (out * sga).astype(BF16)
        r_ref[:, W:] = (scn * sgb).astype(BF16)
        dr = _dot_nt_rows(dx_ref[...].astype(BF16), wout_ref)
        dra, drb = dr[:, 0:W], dr[:, W:]
        dz_ref[...] = jnp.zeros_like(dz_ref)
        dz_ref[:, Z_GA * W:(Z_GA + 1) * W] = (dra * out * dsga).astype(BF16)
        dout = dra * sga
        dt = dout * z1 * sg * (1.0 - sg)
        dtb = dt.astype(BF16)
        dz1 = dout * sg + _dot_nt_rows(dtb, wglu_ref)
        dys_ref[...] = dz1 * _dgelu(ys)
        z1_ref[...] = z1b
        dt_ref[...] = dtb
        dbg_ref[...] += jnp.sum(dt, axis=0, keepdims=True)
        dz_ref[:, Z_GB * W:(Z_GB + 1) * W] = (drb * scn * dsgb).astype(BF16)
        dcn = drb * sgb * dscn
        dlg_ref[...] += jnp.sum(dcn * chat, axis=0, keepdims=True)
        dlb_ref[...] += jnp.sum(dcn, axis=0, keepdims=True)
        dch = dcn * lng_ref[...]
        dc_ref[...] = rstd * (dch - jnp.mean(dch, axis=-1, keepdims=True)
                              - chat * jnp.mean(dch * chat, axis=-1, keepdims=True))

    row = lambda n, col=0: pl.BlockSpec((tm, n), lambda t: (t, col))
    f = lambda n, dt: jax.ShapeDtypeStruct((L, n), dt)
    vec = jax.ShapeDtypeStruct((1, W), F32)
    return pl.pallas_call(
        body, name="ev_tail_bwd", grid=(L // tm,),
        in_specs=[row(W), row(W, Z_GA), row(W), row(W, Z_GB), row(D), _full(wglu.shape), _full(bglu.shape),
                  _full(lng.shape), _full(lnb.shape), _full(wout.shape)],
        out_specs=[row(W), row(W), row(EVEN_IN), row(D), row(W), row(W), _full((1, W)), _full((1, W)), _full((1, W))],
        out_shape=[f(W, F32), f(W, F32), f(EVEN_IN, BF16), f(D, BF16), f(W, BF16), f(W, BF16), vec, vec, vec],
        compiler_params=_cp("arbitrary"),
    )(ys, z, c, z, dx1, wglu, bglu, lng, lnb, wout)


XA_SCALE = XA_HEAD_DIM ** -0.5


def _xa_forward(xv, g, wqg, kv):
    D = D_MODEL
    _, xhat = _rms_parts(xv)
    hb = (xhat * g).astype(BF16)
    qb = (_dot_cols(hb, wqg, (0, 1)) * XA_SCALE).astype(BF16)
    gate = _dot_cols(hb, wqg, (2, 3))
    ps, os_ = [], []
    for h in range(XA_HEADS):
        lo, hi = h * XA_HEAD_DIM, (h + 1) * XA_HEAD_DIM
        s = _dot_nt(qb[:, lo:hi], kv[:, lo:hi])
        e = jnp.exp(s - jnp.max(s, axis=-1, keepdims=True))
        inv = 1.0 / jnp.sum(e, axis=-1, keepdims=True)
        ps.append((e, inv))
        os_.append(_dot(e.astype(BF16), kv[:, D + lo:D + hi]) * inv)
    return hb, qb, gate, ps, jnp.concatenate(os_, axis=1)


def xa_fwd(x, g, wqg, kv, wo, layer, name, tm=MM_TILE):
    L, D = x.shape
    tm = min(tm, L)

    def body(x_ref, g_ref, wqg_ref, kv_ref, wo_ref, o_ref):
        xv = x_ref[...]
        _, _, gate, _, o = _xa_forward(xv, g_ref[...], wqg_ref, kv_ref[...])
        o_ref[...] = xv + _dot_rows((o * _silu(gate)).astype(BF16), wo_ref)

    row = pl.BlockSpec((tm, D), lambda t: (t, 0))
    return pl.pallas_call(
        body, name=name, grid=(L // tm,),
        in_specs=[row, _full(g.shape), _wspec(wqg, layer), _full(kv.shape), _wspec(wo, layer)],
        out_specs=row, out_shape=jax.ShapeDtypeStruct((L, D), F32), compiler_params=_cp("parallel"),
    )(x, g, wqg, kv, wo)


def xa_bwd(x, dxo, g, wqg, kv, wo, layer, name, tm=MM_TILE):
    L, D = x.shape
    tm = min(tm, L)

    def body(x_ref, dxo_ref, g_ref, wqg_ref, kv_ref, wo_ref, dx_ref, dqg_ref, h_ref, r_ref, dkv_ref, dg_ref):
        @pl.when(pl.program_id(0) == 0)
        def _():
            dkv_ref[...] = jnp.zeros_like(dkv_ref)
            dg_ref[...] = jnp.zeros_like(dg_ref)

        xv = x_ref[...]
        kv = kv_ref[...]
        hb, qb, gate, ps, o = _xa_forward(xv, g_ref[...], wqg_ref, kv)
        sgate, dsgate = _silu_pair(gate)
        h_ref[...] = hb
        r_ref[...] = (o * sgate).astype(BF16)
        dxo = dxo_ref[...]
        dr = _dot_nt_rows(dxo.astype(BF16), wo_ref)
        do = dr * sgate
        dqg_ref[:, D:] = (dr * o * dsgate).astype(BF16)
        dob = do.astype(BF16)
        doo = do * o
        for h in range(XA_HEADS):
            lo, hi = h * XA_HEAD_DIM, (h + 1) * XA_HEAD_DIM
            e, inv = ps[h]
            dp = _dot_nt(dob[:, lo:hi], kv[:, D + lo:D + hi])
            dkv_ref[:, D + lo:D + hi] += _dot_tn(e.astype(BF16), (do[:, lo:hi] * inv).astype(BF16))
            rs = jnp.sum(doo[:, lo:hi], axis=-1, keepdims=True)
            dsb = (e * ((dp - rs) * inv)).astype(BF16)
            dqg_ref[:, lo:hi] = (_dot(dsb, kv[:, lo:hi]) * XA_SCALE).astype(BF16)
            dkv_ref[:, lo:hi] += _dot_tn(dsb, qb[:, lo:hi])
        dh = _dot_nt_cols(_col_pieces(dqg_ref[...], D // 2), wqg_ref)
        dx, dg = _rms_bwd(xv, g_ref[...], dh)
        dx_ref[...] = dxo + dx
        dg_ref[...] += dg

    row = lambda n: pl.BlockSpec((tm, n), lambda t: (t, 0))
    return pl.pallas_call(
        body, name=name, grid=(L // tm,),
        in_specs=[row(D), row(D), _full(g.shape), _wspec(wqg, layer), _full(kv.shape), _wspec(wo, layer)],
        out_specs=[row(D), row(2 * D), row(D), row(D), _full(kv.shape), _full((1, D))],
        out_shape=[jax.ShapeDtypeStruct((L, D), F32), jax.ShapeDtypeStruct((L, 2 * D), BF16),
                   jax.ShapeDtypeStruct((L, D), BF16), jax.ShapeDtypeStruct((L, D), BF16),
                   jax.ShapeDtypeStruct(kv.shape, F32), jax.ShapeDtypeStruct((1, D), F32)],
        compiler_params=_cp("arbitrary"),
    )(x, dxo, g, wqg, kv, wo)


ATT_SCALE = ATT_HEAD_DIM ** -0.5
ATT_PAIRS = ATT_HEADS // 2
SKEW_LANES = 1024
REL_LANES = 384


def _skew(x, left):
    amt = (ATT_QB - 1) - lax.broadcasted_iota(jnp.int32, (ATT_QB, 1), 0)
    for bit in range(8):
        sh = (SKEW_LANES - (1 << bit)) if left else (1 << bit)
        x = jnp.where(((amt >> bit) & 1) == 1, pltpu.roll(x, sh, 1), x)
    return x


def _dist_onehot(shape, dist_axis):
    j = lax.broadcasted_iota(jnp.int32, shape, dist_axis)
    r = lax.broadcasted_iota(jnp.int32, shape, 1 - dist_axis)
    return (jnp.clip((ATT_WIN - 1) - j, -MAX_REL, MAX_REL) + MAX_REL == r).astype(BF16)


def _dot_exact(v, onehot):
    acc = jnp.zeros((v.shape[0], onehot.shape[1]), F32)
    rem = v
    for _ in range(3):
        part = rem.astype(BF16)
        acc = acc + _dot(part, onehot)
        rem = rem - part.astype(F32)
    return acc


ATT_EDGE = ATT_PAD // ATT_QB


def att_bias(rel_bias, carried=None):
    H = rel_bias.shape[0]
    rb = jnp.pad(rel_bias, ((0, 0), (0, REL_LANES - rel_bias.shape[1]))).reshape(H, 1, REL_LANES)

    def body(*refs):
        top = end = None
        if carried is not None:
            refs, parts = carried.split(refs, 1, 1, 0)
            top, end = carried.hooks(parts, (H,))
            top()
        rb_ref, o_ref = refs
        by_col = _dot_exact(jnp.broadcast_to(rb_ref[...], (8, REL_LANES)), _dist_onehot((REL_LANES, SKEW_LANES), 1))
        x = _skew(jnp.broadcast_to(by_col[0:1, :], (ATT_QB, SKEW_LANES)), left=True)[:, 0:ATT_WIN]
        qc = lax.broadcasted_iota(jnp.int32, (ATT_QB, 1), 0) // CHUNK + LEFT_CHUNKS
        col = lax.broadcasted_iota(jnp.int32, (1, ATT_WIN), 1)
        dc = qc - col // CHUNK
        band = (dc >= 0) & (dc <= LEFT_CHUNKS)
        for blk in range(ATT_EDGE + 1):
            o_ref[blk] = jnp.where(band & (col >= ATT_PAD - blk * ATT_QB), x, NEG)
        if end is not None:
            end()

    extra = carried.arrays if carried is not None else []
    extra_out = carried.out_shapes if carried is not None else []
    extra_sems = carried.sems if carried is not None else []
    return pl.pallas_call(
        body, name="att_bias", grid=(H,),
        in_specs=[pl.BlockSpec((None, 1, REL_LANES), lambda h: (h, 0, 0))] + [ANY] * len(extra),
        out_specs=[pl.BlockSpec((ATT_EDGE + 1, None, ATT_QB, ATT_WIN), lambda h: (0, h, 0, 0))] + [ANY] * len(extra_out),
        out_shape=[jax.ShapeDtypeStruct((ATT_EDGE + 1, H, ATT_QB, ATT_WIN), F32)] + extra_out,
        scratch_shapes=extra_sems,
        compiler_params=_cp("arbitrary" if carried is not None else "parallel"),
    )(rb, *extra)


def relbias_bwd(dbias):
    H = dbias.shape[0]

    def body(x_ref, o_ref):
        x = jnp.concatenate([x_ref[...], jnp.zeros((ATT_QB, SKEW_LANES - ATT_WIN), F32)], axis=1)
        col = jnp.sum(_skew(x, left=False), axis=0, keepdims=True)
        o_ref[...] = _dot_exact(jnp.broadcast_to(col, (8, SKEW_LANES)), _dist_onehot((SKEW_LANES, REL_LANES), 0))

    out = pl.pallas_call(
        body, name="relbias_bwd", grid=(H,),
        in_specs=[pl.BlockSpec((None, ATT_QB, ATT_WIN), lambda h: (h, 0, 0))],
        out_specs=pl.BlockSpec((None, 8, REL_LANES), lambda h: (h, 0, 0)),
        out_shape=jax.ShapeDtypeStruct((H, 8, REL_LANES), F32), compiler_params=_cp("parallel"),
    )(dbias)
    return out[:, 0, :2 * MAX_REL + 1]


def _ca_scores(qh, kw, bias):
    s = _dot_nt(qh, kw) + bias
    e = jnp.exp(s - jnp.max(s, axis=-1, keepdims=True))
    return e, 1.0 / jnp.sum(e, axis=-1, keepdims=True)


def _ca_head(qv, m):
    return jnp.where(m, qv, jnp.zeros_like(qv)) * ATT_SCALE


def _ca_bias_spec():
    return pl.BlockSpec((None, 2, ATT_QB, ATT_WIN), lambda hp, b: (jnp.minimum(b, ATT_EDGE), hp, 0, 0))


def ca_fwd(q, kvp, gate, bias):
    L, D = q.shape
    Lp = kvp.shape[0]
    nb = L // ATT_QB

    PP = 2
    W = PP * 128

    def body(q_ref, k_ref, v_ref, g_ref, b_ref, r_ref, o_ref):
        w = pl.multiple_of(pl.program_id(1) * ATT_QB, ATT_QB)
        first = lax.broadcasted_iota(jnp.int32, (1, 128), 1) < ATT_HEAD_DIM
        for pp in range(PP):
            sl = slice(pp * 128, (pp + 1) * 128)
            kw = k_ref[pl.ds(w, ATT_WIN), sl]
            vw = v_ref[pl.ds(w, ATT_WIN), sl]
            qv = q_ref[:, sl]
            outs = []
            for hh, m in enumerate((first, jnp.logical_not(first))):
                e, inv = _ca_scores(_ca_head(qv, m), kw, b_ref[2 * pp + hh])
                outs.append(_dot(e.astype(BF16), vw) * inv)
            o = jnp.where(first, outs[0], outs[1])
            r_ref[:, sl] = (o * _silu(g_ref[:, sl])).astype(BF16)
            o_ref[:, sl] = o.astype(BF16)

    blk = pl.BlockSpec((ATT_QB, W), lambda hp, b: (b, hp))
    bias_blk = pl.BlockSpec((None, 2 * PP, ATT_QB, ATT_WIN), lambda hp, b: (jnp.minimum(b, ATT_EDGE), hp, 0, 0))
    return pl.pallas_call(
        body, name="ca_fwd", grid=(ATT_PAIRS // PP, nb),
        in_specs=[blk, pl.BlockSpec((Lp, W), lambda hp, b: (0, hp)),
                  pl.BlockSpec((Lp, W), lambda hp, b: (0, ATT_PAIRS // PP + hp)), blk, bias_blk],
        out_specs=[blk, blk], out_shape=[jax.ShapeDtypeStruct((L, D), BF16), jax.ShapeDtypeStruct((L, D), BF16)],
        compiler_params=_cp("parallel", "arbitrary"),
    )(q, kvp, kvp, gate, bias)


def ca_bwd(q, kvp, gate, bias, dr, o):
    L, D = q.shape
    Lp = kvp.shape[0]
    nb = L // ATT_QB

    def body(q_ref, k_ref, v_ref, g_ref, b_ref, dr_ref, o_ref, dq_ref, dg_ref, dk_ref, dv_ref, db_ref):
        b = pl.program_id(1)

        @pl.when(b == 0)
        def _():
            for r in (dk_ref, dv_ref, db_ref):
                r[...] = jnp.zeros_like(r)

        w = pl.multiple_of(b * ATT_QB, ATT_QB)
        kw = k_ref[pl.ds(w, ATT_WIN), :]
        vw = v_ref[pl.ds(w, ATT_WIN), :]
        qv = q_ref[...]
        gate_v = g_ref[...]
        drv = dr_ref[...]
        o = o_ref[...].astype(F32)
        sgate, dsgate = _silu_pair(gate_v)
        do = drv * sgate
        doo = do * o
        first = lax.broadcasted_iota(jnp.int32, (1, 128), 1) < ATT_HEAD_DIM
        dqs = []
        dkw = jnp.zeros((ATT_WIN, 128), F32)
        dvw = jnp.zeros((ATT_WIN, 128), F32)
        for hh, m in enumerate((first, jnp.logical_not(first))):
            qh = _ca_head(qv, m)
            e, inv = _ca_scores(qh, kw, b_ref[hh])
            eb = e.astype(BF16)
            doh = jnp.where(m, do, 0.0)
            dp = _dot_nt(doh.astype(BF16), vw)
            dvw = dvw + _dot_tn(eb, (doh * inv).astype(BF16))
            rs = jnp.sum(jnp.where(m, doo, 0.0), axis=-1, keepdims=True)
            ds = e * ((dp - rs) * inv)
            db_ref[hh] += ds
            dsb = ds.astype(BF16)
            dqs.append(_dot(dsb, kw))
            dkw = dkw + _dot_tn(dsb, qh)
        dg_ref[...] = (drv * o * dsgate).astype(BF16)
        dq_ref[...] = (jnp.where(first, dqs[0], dqs[1]) * ATT_SCALE).astype(BF16)
        dk_ref[pl.ds(w, ATT_WIN), :] += dkw
        dv_ref[pl.ds(w, ATT_WIN), :] += dvw

    blk = pl.BlockSpec((ATT_QB, 128), lambda hp, b: (b, hp))
    kblk = pl.BlockSpec((Lp, 128), lambda hp, b: (0, hp))
    vblk = pl.BlockSpec((Lp, 128), lambda hp, b: (0, ATT_PAIRS + hp))
    bblk = pl.BlockSpec((2, ATT_QB, ATT_WIN), lambda hp, b: (hp, 0, 0))
    return pl.pallas_call(
        body, name="ca_bwd", grid=(ATT_PAIRS, nb),
        in_specs=[blk, kblk, vblk, blk, _ca_bias_spec(), blk, blk],
        out_specs=[blk, blk, kblk, kblk, bblk],
        out_shape=[jax.ShapeDtypeStruct((L, D), BF16), jax.ShapeDtypeStruct((L, D), BF16),
                   jax.ShapeDtypeStruct((Lp, D), F32), jax.ShapeDtypeStruct((Lp, D), F32),
                   jax.ShapeDtypeStruct(bias.shape[1:], F32)],
        compiler_params=_cp("parallel", "arbitrary"),
    )(q, kvp, kvp, gate, bias, dr, o)


def loss_bwd(x, target, g, tm=ROW_TILE):
    L, D = x.shape
    tm = min(tm, L)

    def body(x_ref, t_ref, g_ref, loss_ref, dx_ref, dg_ref):
        @pl.when(pl.program_id(0) == 0)
        def _():
            loss_ref[...] = jnp.zeros_like(loss_ref)
            dg_ref[...] = jnp.zeros_like(dg_ref)

        xv = x_ref[...]
        gv = g_ref[...]
        _, xhat = _rms_parts(xv)
        err = xhat * gv - t_ref[...]
        loss_ref[...] += 0.5 * jnp.sum(jnp.sum(err * err, axis=-1, keepdims=True), axis=0, keepdims=True) / D
        dx, dg = _rms_bwd(xv, gv, err * (1.0 / D))
        dx_ref[...] = dx
        dg_ref[...] += dg

    row = pl.BlockSpec((tm, D), lambda t: (t, 0))
    return pl.pallas_call(
        body, name="loss_bwd", grid=(L // tm,),
        in_specs=[row, row, _full(g.shape)],
        out_specs=[_full((1, 128)), row, _full((1, D))],
        out_shape=[jax.ShapeDtypeStruct((1, 128), F32), jax.ShapeDtypeStruct((L, D), F32),
                   jax.ShapeDtypeStruct((1, D), F32)],
        compiler_params=_cp("arbitrary"),
    )(x, target, g)


_ADAM_C1 = 1.0 / (1.0 - ADAM_B1 ** ADAM_STEP)
_ADAM_C2 = 1.0 / (1.0 - ADAM_B2 ** ADAM_STEP)


def _adam_update(w, g, m, v):
    mn = ADAM_B1 * m + (1.0 - ADAM_B1) * g
    vn = ADAM_B2 * v + (1.0 - ADAM_B2) * g * g
    delta = -ADAM_LR * ((mn * _ADAM_C1) / (jnp.sqrt(vn * _ADAM_C2) + ADAM_EPS) + ADAM_WD * w)
    return delta, mn, vn


def adamw(w, g, m, v, name, tr=512):
    R, C = w.shape
    tr = min(tr, R)

    def body(w_ref, g_ref, m_ref, v_ref, d_ref, mo_ref, vo_ref):
        d_ref[...], mo_ref[...], vo_ref[...] = _adam_update(w_ref[...], g_ref[...], m_ref[...], v_ref[...])

    blk = pl.BlockSpec((tr, C), lambda i: (i, 0))
    sh = jax.ShapeDtypeStruct((R, C), F32)
    return pl.pallas_call(
        body, name=name, grid=(R // tr,), in_specs=[blk] * 4, out_specs=[blk] * 3,
        out_shape=[sh] * 3, compiler_params=_cp("parallel"),
    )(w, g, m, v)


def adamw_allreduce(gathered, w, m, v, shard, name, slot=None):
    R, C = w.shape
    sharded = slot is None and gathered.shape[2] != C

    def body(s_ref, ga_ref, w_ref, m_ref, v_ref, g_ref, d_ref, mo_ref, vo_ref):
        take = (lambda d: ga_ref[d]) if slot is None else (lambda d: ga_ref[d, slot:slot + R, 0:C])
        g = take(0)
        for d in range(1, N_DEV):
            g = g + take(d)
        g_ref[...] = g
        d_ref[...], mo_ref[...], vo_ref[...] = _adam_update(w_ref[...], g, m_ref[...], v_ref[...])

    blk = pl.BlockSpec((R, C), lambda i, s_ref: (0, 0))
    if slot is not None:
        gblk = pl.BlockSpec(gathered.shape, lambda i, s_ref: (0, 0, 0))
    else:
        gblk = pl.BlockSpec((N_DEV, R, C),
                            (lambda i, s_ref: (0, 0, s_ref[0])) if sharded else (lambda i, s_ref: (0, 0, 0)))
    sh = jax.ShapeDtypeStruct((R, C), F32)
    return pl.pallas_call(
        body, name=name,
        grid_spec=pltpu.PrefetchScalarGridSpec(num_scalar_prefetch=1, grid=(1,), in_specs=[gblk, blk, blk, blk],
                                               out_specs=[blk] * 4),
        out_shape=[sh] * 4, compiler_params=_cp("arbitrary"),
    )(shard, gathered, w, m, v)


LATE = ("ev_s5_glu_w", "ev_w_out", "od_w_in", "od_w_out", "xa_w_qg", "xa_w_kv", "xa_w_o")
EARLY_GRADS = ("od_w_in", "od_w_out", "xa_w_qg", "xa_w_kv", "xa_w_o", "ev_w_out", "ev_s5_glu_w")


def _reduce_to_chip(gs, names, core, tag):
    from_sibling = sibling_send_other_half(gs, "sibling_send_" + tag)
    return [sum_with_sibling(gi, ri, core, "sum_sibling_" + n) for n, gi, ri in zip(names, gs, from_sibling)]


def local_step(x, mem, target, p, gw, late, bias, place, core):
    row = lambda a: a.reshape(1, -1)
    D = D_MODEL
    L = x.shape[0]
    g, big = {}, {}
    gw = dict(gw)

    z, h0b = norm_mm(x, p["ev_norm_g"], gw["ev_w_in"], [((0, 1, 2, 3), F32, 0)], "ev_in")
    ys, s5_saved, landed = s5_mixer_core_fwd(
        z, p["ev_s5_lambda_re"][0], p["ev_s5_lambda_im"][0], p["ev_s5_log_dt"][0], p["ev_s5_b_re"][0],
        p["ev_s5_b_im"][0], p["ev_s5_c_re"][0], p["ev_s5_c_im"][0], p["ev_s5_d"][0],
        carried=carried_allgather([late[n] for n in LATE]))
    for n, gth in zip(LATE, landed):
        rows = gth.shape[1]
        gw[n] = gth.reshape(N_CHIPS, 2, rows // 2, gth.shape[2]) if n.startswith("xa_") else gth
    memn_b = rms_fwd(mem, row(p["mem_norm_g"]), "mem_norm")
    kvs = [mm_cols(memn_b, gw["xa_w_kv"], l, f"xa_kv{l}", BF16) for l in range(2)]
    conv_w = p["ev_conv_w"][0]
    c = conv_fwd(z, conv_w, p["ev_conv_b"])
    tail = (gw["ev_s5_glu_w"], p["ev_s5_glu_b"], p["ev_conv_ln_g"], p["ev_conv_ln_b"], gw["ev_w_out"])
    x1 = ev_tail_fwd(ys, z, c, x, *tail)
    xa0 = (row(p["xa_norm_g"][0]), gw["xa_w_qg"], kvs[0], gw["xa_w_o"], 0)
    x2 = xa_fwd(x1, *xa0, "xa_fwd0")

    q, kvp, gate, h1b = norm_mm(x2, p["od_norm_g"], gw["od_w_in"],
                                [((0,), BF16, 0), ((1, 2), BF16, ATT_PAD), ((3,), F32, 0)], "od_in")
    kvp = zero_rows(kvp, ATT_PAD, "od_kv_pad")
    r, att_o = ca_fwd(q, kvp, gate, bias)
    x3 = mm_res(r, gw["od_w_out"], x2, "od_out")
    xa1 = (row(p["xa_norm_g"][1]), gw["xa_w_qg"], kvs[1], gw["xa_w_o"], 1)
    x4 = xa_fwd(x3, *xa1, "xa_fwd1")

    loss, dx4, dgf = loss_bwd(x4, target, row(p["final_norm_g"]))
    g["final_norm_g"] = dgf.reshape(D)

    dx3, dqg1, hx1, rx1, dkv1, dgxa1 = xa_bwd(x3, dx4, *xa1, "xa_bwd1")
    dwqg = mm_tn(hx1, dqg1, "xa_dwqg1", ("cols", 1))
    dwo = mm_tn(rx1, dx4, "xa_dwo1", ("rows", 1))

    big["od_w_out"] = mm_tn(r, dx3, "od_dwout", ("rows",))
    dr = mm_nt_rows(dx3, gw["od_w_out"], "od_out_bwd")
    dq, dgate, dkp, dvp, dbias = ca_bwd(q, kvp, gate, bias, dr, att_o)
    pieces, offs = (dq, dkp, dvp, dgate), (0, ATT_PAD, ATT_PAD, 0)
    dwin = None
    for s in range(N_CHIPS):
        dwin = mm_tn(h1b, pieces[s], f"od_dwin{s}", ("slab", s), into=dwin, b_off=offs[s],
                     bl=ATT_PAD if offs[s] else 1024)
    big["od_w_in"] = dwin
    dx2, dgod = mm_nt_normbwd(pieces, offs, gw["od_w_in"], x2, p["od_norm_g"], dx3, "od_in_bwd")
    g["od_norm_g"] = dgod
    g["od_rel_bias"] = relbias_bwd(dbias)[None]

    dx1, dqg0, hx0, rx0, dkv0, dgxa0 = xa_bwd(x1, dx2, *xa0, "xa_bwd0")
    big["xa_w_qg"] = mm_tn(hx0, dqg0, "xa_dwqg0", ("cols", 0), into=dwqg)
    big["xa_w_o"] = mm_tn(rx0, dx2, "xa_dwo0", ("rows", 0), into=dwo)
    g["xa_norm_g"] = jnp.concatenate([dgxa0, dgxa1], axis=0)

    dys, dc, dz, ra, z1b, dtb, dbglu, dlng, dlnb = ev_tail_bwd(ys, z, c, dx1, *tail)
    big["ev_w_out"] = mm_tn(ra, dx1, "ev_dwout", ("rows",))
    big["ev_s5_glu_w"] = mm_tn(z1b, dtb, "ev_dwglu", ("rows",))
    g["ev_s5_glu_b"], g["ev_conv_ln_g"], g["ev_conv_ln_b"] = dbglu, dlng, dlnb
    dwkv = mm_tn(memn_b, dkv1, "xa_dwkv1", ("cols", 1), bl=MEM_LEN)
    big["xa_w_kv"] = mm_tn(memn_b, dkv0, "xa_dwkv0", ("cols", 0), into=dwkv, bl=MEM_LEN)
    dmem0 = mm_nt_cols(dkv0, gw["xa_w_kv"], 0, "xa_kv_bwd0")
    dmem1 = mm_nt_cols(dkv1, gw["xa_w_kv"], 1, "xa_kv_bwd1")
    g["mem_norm_g"] = rms_dgain(mem, dmem0, dmem1, "mem_norm_bwd").reshape(D)

    shard_major = lambda t: t.reshape((-1,) + t.shape[-2:])
    gs = [shard_major(big[n]) for n in EARLY_GRADS]
    dz, dconvw, dconvb, *from_sibling = conv_bwd(z, dc, dz, conv_w, carried=carried_sibling_send(gs))
    g["ev_conv_w"] = dconvw[None, :CONV_KERNEL]
    g["ev_conv_b"] = dconvb
    chip_sums = [sum_with_sibling(gi, ri, core, "sum_sibling_" + n) for n, gi, ri in zip(EARLY_GRADS, gs, from_sibling)]
    dz, s5g, from_chips = s5_mixer_core_bwd(z, dys, dz, p["ev_s5_lambda_re"][0], p["ev_s5_lambda_im"][0], s5_saved,
                                            carried=carried_chips_exchange(chip_sums))
    reduced = {n: sum_chips(ci, ri, place, "sum_chips_" + n) for n, ci, ri in zip(EARLY_GRADS, chip_sums, from_chips)}
    for n, v in s5g.items():
        g["ev_s5_" + n] = v[None]
    packed, slots = pack_rows([_as2d(g[n]) for n in PACKED_SMALL], "pack_small_grads")
    dwin_ev, *gathered = mm_tn(h0b, dz, "ev_dwin", ("cols",),
                               carried=carried_allgather_devices([packed] + [_as2d(g[n]) for n in SINGLE_SMALL]))
    grad_x, dgev = mm_nt_normbwd((dz,), (0,), gw["ev_w_in"], x, p["ev_norm_g"], dx1, "ev_in_bwd")
    chip_sum = _reduce_to_chip([dwin_ev], ["ev_w_in"], core, "last")
    reduced["ev_w_in"] = sum_chips(chip_sum[0], chips_exchange(chip_sum)[0], place, "sum_chips_ev_w_in")
    return loss, grad_x, g, reduced, dgev, gathered, slots


def _me():
    return lax.axis_index("x"), lax.axis_index("y"), lax.axis_index("c")


def _other_chips(x, y):
    return [(1 - x, y), (x, 1 - y), (1 - x, 1 - y)]


def _remote(src, dst, send_sems, recv_sems, k, to):
    return pltpu.make_async_remote_copy(src_ref=src, dst_ref=dst, send_sem=send_sems.at[k], recv_sem=recv_sems.at[k],
                                        device_id=to, device_id_type=MESH)


def _rows_half(ref, h):
    H = ref.shape[-2] // 2
    return ref.at[(slice(None),) * (len(ref.shape) - 2) + (pl.ds(h * H, H), slice(None))]


def allgather_devices(vs):
    n = len(vs)

    def body(*refs):
        ins, outs = refs[:n], refs[n:2 * n]
        send_sems, recv_sems, local_sems = refs[2 * n:]
        x, y, c = _me()
        sib = (x, y, 1 - c)
        chips = _other_chips(x, y)
        me = 4 * x + 2 * y + c
        local = [pltpu.make_async_copy(ins[i], outs[i].at[me], local_sems.at[i]) for i in range(n)]
        for cp in local:
            cp.start()
        first, passed = [], []
        for i in range(n):
            first.append(_remote(ins[i], outs[i].at[me], send_sems, recv_sems, 7 * i, sib))
            for j, (cx, cy) in enumerate(chips):
                first.append(_remote(ins[i], outs[i].at[me], send_sems, recv_sems, 7 * i + 1 + j, (cx, cy, c)))
        for cp in first:
            cp.start()
        for j, (cx, cy) in enumerate(chips):
            for i in range(n):
                got = outs[i].at[4 * cx + 2 * cy + c]
                _remote(got, got, send_sems, recv_sems, 7 * i + 1 + j, (cx, cy, c)).wait_recv()
                fw = _remote(got, got, send_sems, recv_sems, 7 * i + 4 + j, sib)
                fw.start()
                passed.append(fw)
        for i in range(n):
            got = outs[i].at[4 * x + 2 * y + (1 - c)]
            _remote(got, got, send_sems, recv_sems, 7 * i, sib).wait_recv()
            for j, (cx, cy) in enumerate(chips):
                got = outs[i].at[4 * cx + 2 * cy + (1 - c)]
                _remote(got, got, send_sems, recv_sems, 7 * i + 4 + j, sib).wait_recv()
        for cp in first + passed:
            cp.wait_send()
        for cp in local:
            cp.wait()

    return pl.pallas_call(
        body, name="allgather_devices", in_specs=[ANY] * n, out_specs=[ANY] * n,
        out_shape=[jax.ShapeDtypeStruct((N_DEV,) + v.shape, v.dtype) for v in vs],
        scratch_shapes=[pltpu.SemaphoreType.DMA((7 * n,)), pltpu.SemaphoreType.DMA((7 * n,)),
                        pltpu.SemaphoreType.DMA((n,))],
    )(*vs)


def sibling_send_other_half(gs, name):
    n = len(gs)

    def body(*refs):
        ins, outs = refs[:n], refs[n:2 * n]
        send_sems, recv_sems = refs[2 * n:]
        x, y, c = _me()
        cps = [_remote(_rows_half(ins[i], 1 - c), outs[i], send_sems, recv_sems, i, (x, y, 1 - c)) for i in range(n)]
        for cp in cps:
            cp.start()
        for cp in cps:
            cp.wait()

    return pl.pallas_call(
        body, name=name, in_specs=[ANY] * n, out_specs=[ANY] * n,
        out_shape=[jax.ShapeDtypeStruct((g.shape[0], g.shape[1] // 2, g.shape[2]), g.dtype) for g in gs],
        scratch_shapes=[pltpu.SemaphoreType.DMA((n,)), pltpu.SemaphoreType.DMA((n,))],
    )(*gs)


def chips_exchange(parts):
    n = len(parts)

    def body(*refs):
        ins, outs = refs[:n], refs[n:2 * n]
        send_sems, recv_sems = refs[2 * n:]
        x, y, c = _me()
        cps = []
        for i in range(n):
            nl = ins[i].shape[0] // N_CHIPS
            for j, (cx, cy) in enumerate(_other_chips(x, y)):
                cps.append(_remote(ins[i].at[pl.ds((2 * cx + cy) * nl, nl)], outs[i].at[j], send_sems, recv_sems,
                                   3 * i + j, (cx, cy, c)))
        for cp in cps:
            cp.start()
        for cp in cps:
            cp.wait()

    return pl.pallas_call(
        body, name="chips_exchange", in_specs=[ANY] * n, out_specs=[ANY] * n,
        out_shape=[jax.ShapeDtypeStruct((3, a.shape[0] // N_CHIPS) + a.shape[1:], a.dtype) for a in parts],
        scratch_shapes=[pltpu.SemaphoreType.DMA((3 * n,)), pltpu.SemaphoreType.DMA((3 * n,))],
    )(*parts)


def sibling_share(fulls):
    n = len(fulls)

    def body(*refs):
        outs = refs[n:2 * n]
        send_sems, recv_sems = refs[2 * n:]
        x, y, c = _me()
        cps = [_remote(_rows_half(outs[i], c), _rows_half(outs[i], c), send_sems, recv_sems, i, (x, y, 1 - c))
               for i in range(n)]
        for cp in cps:
            cp.start()
        for i in range(n):
            got = _rows_half(outs[i], 1 - c)
            _remote(got, got, send_sems, recv_sems, i, (x, y, 1 - c)).wait_recv()
        for cp in cps:
            cp.wait_send()

    return pl.pallas_call(
        body, name="sibling_share", in_specs=[ANY] * n, out_specs=[ANY] * n,
        out_shape=[jax.ShapeDtypeStruct(f.shape, f.dtype) for f in fulls],
        input_output_aliases={i: i for i in range(n)},
        scratch_shapes=[pltpu.SemaphoreType.DMA((n,)), pltpu.SemaphoreType.DMA((n,))],
    )(*fulls)


def sum_with_sibling(g, recv, core, name):
    S, H, C = recv.shape
    tr = min(512, H)

    def body(c_ref, g_ref, r_ref, o_ref):
        o_ref[...] = (g_ref[...].astype(F32) + r_ref[...].astype(F32)).astype(o_ref.dtype)

    nb = H // tr
    return pl.pallas_call(
        body, name=name,
        grid_spec=pltpu.PrefetchScalarGridSpec(
            num_scalar_prefetch=1, grid=(S, nb),
            in_specs=[pl.BlockSpec((None, tr, C), lambda s, i, c_ref: (s, c_ref[0] * nb + i, 0)),
                      pl.BlockSpec((None, tr, C), lambda s, i, c_ref: (s, i, 0))],
            out_specs=pl.BlockSpec((None, tr, C), lambda s, i, c_ref: (s, i, 0))),
        out_shape=jax.ShapeDtypeStruct((S, H, C), g.dtype), compiler_params=_cp("parallel", "parallel"),
    )(core, g, recv)


def sum_chips(a, recv, place, name):
    _, nl, H, C = recv.shape
    tr = min(512, H)
    nb = H // tr

    def body(p_ref, a_ref, r_ref, o_ref):
        acc = a_ref[...].astype(F32)
        for j in range(3):
            acc = acc + r_ref[j].astype(F32)
        o_ref[...] = acc

    return pl.pallas_call(
        body, name=name,
        grid_spec=pltpu.PrefetchScalarGridSpec(
            num_scalar_prefetch=1, grid=(nl, nb),
            in_specs=[pl.BlockSpec((None, tr, C), lambda l, i, p_ref: (p_ref[0] * nl + l, i, 0)),
                      pl.BlockSpec((3, None, tr, C), lambda l, i, p_ref: (0, l, i, 0))],
            out_specs=pl.BlockSpec((None, tr, C), lambda l, i, p_ref: (l, p_ref[1] * nb + i, 0))),
        out_shape=jax.ShapeDtypeStruct((nl, 2 * H, C), F32), compiler_params=_cp("parallel", "parallel"),
    )(place, a, recv)


def pack_rows(arrays, name):
    starts, r0 = [], 0
    for a in arrays:
        if a.shape[0] >= SUBLANES:
            r0 = -(-r0 // SUBLANES) * SUBLANES
        starts.append(r0)
        r0 += a.shape[0]
    r0 = -(-r0 // SUBLANES) * SUBLANES
    n = len(arrays)

    def body(*refs):
        o_ref = refs[n]
        o_ref[...] = jnp.zeros_like(o_ref)
        for a_ref, s in zip(refs[:n], starts):
            r, c = a_ref.shape
            o_ref[s:s + r, 0:c] = a_ref[...]

    out = pl.pallas_call(body, name=name, out_shape=jax.ShapeDtypeStruct((r0, PACK_COLS), F32))(*arrays)
    return out, starts


def sum_slot(gathered, slot, shape, name):
    r, c = shape

    def body(ga_ref, o_ref):
        acc = ga_ref[0, slot:slot + r, 0:c]
        for d in range(1, N_DEV):
            acc = acc + ga_ref[d, slot:slot + r, 0:c]
        o_ref[...] = acc

    return pl.pallas_call(body, name=name, out_shape=jax.ShapeDtypeStruct((r, c), F32))(gathered)


def carried_allgather(blocks):
    n = len(blocks)

    def first_hop(ins, outs, sems, i, j, chip, x, y, c):
        me = 2 * x + y
        return _remote(_rows_half(ins[i], c), _rows_half(outs[i].at[me], c), sems[0], sems[1], 6 * i + j, (*chip, c))

    def start(ins, outs, sems):
        x, y, c = _me()
        for i in range(n):
            pltpu.make_async_copy(ins[i], outs[i].at[2 * x + y], sems[2].at[i]).start()
        for i in range(n):
            for j, chip in enumerate(_other_chips(x, y)):
                first_hop(ins, outs, sems, i, j, chip, x, y, c).start()

    def finish(ins, outs, sems):
        x, y, c = _me()
        sib = (x, y, 1 - c)
        chips = _other_chips(x, y)
        passed = []
        for j, (cx, cy) in enumerate(chips):
            for i in range(n):
                got = _rows_half(outs[i].at[2 * cx + cy], c)
                _remote(got, got, sems[0], sems[1], 6 * i + j, (cx, cy, c)).wait_recv()
                fw = _remote(got, got, sems[0], sems[1], 6 * i + 3 + j, sib)
                fw.start()
                passed.append(fw)
        for j, (cx, cy) in enumerate(chips):
            for i in range(n):
                got = _rows_half(outs[i].at[2 * cx + cy], 1 - c)
                _remote(got, got, sems[0], sems[1], 6 * i + 3 + j, sib).wait_recv()
        for i in range(n):
            for j, chip in enumerate(chips):
                first_hop(ins, outs, sems, i, j, chip, x, y, c).wait_send()
        for fw in passed:
            fw.wait_send()
        for i in range(n):
            pltpu.make_async_copy(ins[i], outs[i].at[2 * x + y], sems[2].at[i]).wait()

    return Carried(blocks, [jax.ShapeDtypeStruct((N_CHIPS,) + b.shape, b.dtype) for b in blocks],
                   [pltpu.SemaphoreType.DMA((6 * n,)), pltpu.SemaphoreType.DMA((6 * n,)), pltpu.SemaphoreType.DMA((n,))],
                   start, finish)


def carried_allgather_devices(vs):
    n = len(vs)

    def first_copies(ins, outs, sems):
        x, y, c = _me()
        me = 4 * x + 2 * y + c
        cps = []
        for i in range(n):
            cps.append(_remote(ins[i], outs[i].at[me], sems[0], sems[1], 7 * i, (x, y, 1 - c)))
            for j, (cx, cy) in enumerate(_other_chips(x, y)):
                cps.append(_remote(ins[i], outs[i].at[me], sems[0], sems[1], 7 * i + 1 + j, (cx, cy, c)))
        return cps

    def local_copies(ins, outs, sems):
        x, y, c = _me()
        return [pltpu.make_async_copy(ins[i], outs[i].at[4 * x + 2 * y + c], sems[2].at[i]) for i in range(n)]

    def start(ins, outs, sems):
        for cp in local_copies(ins, outs, sems) + first_copies(ins, outs, sems):
            cp.start()

    def finish(ins, outs, sems):
        x, y, c = _me()
        sib = (x, y, 1 - c)
        chips = _other_chips(x, y)
        passed = []
        for j, (cx, cy) in enumerate(chips):
            for i in range(n):
                got = outs[i].at[4 * cx + 2 * cy + c]
                _remote(got, got, sems[0], sems[1], 7 * i + 1 + j, (cx, cy, c)).wait_recv()
                fw = _remote(got, got, sems[0], sems[1], 7 * i + 4 + j, sib)
                fw.start()
                passed.append(fw)
        for i in range(n):
            got = outs[i].at[4 * x + 2 * y + (1 - c)]
            _remote(got, got, sems[0], sems[1], 7 * i, sib).wait_recv()
            for j, (cx, cy) in enumerate(chips):
                got = outs[i].at[4 * cx + 2 * cy + (1 - c)]
                _remote(got, got, sems[0], sems[1], 7 * i + 4 + j, sib).wait_recv()
        for cp in first_copies(ins, outs, sems) + passed:
            cp.wait_send()
        for cp in local_copies(ins, outs, sems):
            cp.wait()

    return Carried(vs, [jax.ShapeDtypeStruct((N_DEV,) + v.shape, v.dtype) for v in vs],
                   [pltpu.SemaphoreType.DMA((7 * n,)), pltpu.SemaphoreType.DMA((7 * n,)), pltpu.SemaphoreType.DMA((n,))],
                   start, finish)


def carried_sibling_send(gs):
    n = len(gs)

    def copies(ins, outs, sems):
        x, y, c = _me()
        return [_remote(_rows_half(ins[i], 1 - c), outs[i], sems[0], sems[1], i, (x, y, 1 - c)) for i in range(n)]

    def start(ins, outs, sems):
        for cp in copies(ins, outs, sems):
            cp.start()

    def finish(ins, outs, sems):
        for cp in copies(ins, outs, sems):
            cp.wait()

    return Carried(gs, [jax.ShapeDtypeStruct((g.shape[0], g.shape[1] // 2, g.shape[2]), g.dtype) for g in gs],
                   [pltpu.SemaphoreType.DMA((n,)), pltpu.SemaphoreType.DMA((n,))], start, finish)


def carried_chips_exchange(parts):
    n = len(parts)

    def copies(ins, outs, sems):
        x, y, c = _me()
        cps = []
        for i in range(n):
            nl = ins[i].shape[0] // N_CHIPS
            for j, (cx, cy) in enumerate(_other_chips(x, y)):
                cps.append(_remote(ins[i].at[pl.ds((2 * cx + cy) * nl, nl)], outs[i].at[j], sems[0], sems[1],
                                   3 * i + j, (cx, cy, c)))
        return cps

    def start(ins, outs, sems):
        for cp in copies(ins, outs, sems):
            cp.start()

    def finish(ins, outs, sems):
        for cp in copies(ins, outs, sems):
            cp.wait()

    return Carried(parts, [jax.ShapeDtypeStruct((3, a.shape[0] // N_CHIPS) + a.shape[1:], a.dtype) for a in parts],
                   [pltpu.SemaphoreType.DMA((3 * n,)), pltpu.SemaphoreType.DMA((3 * n,))], start, finish)


BIG = ("ev_w_in", "ev_s5_glu_w", "ev_w_out", "od_w_in", "od_w_out", "xa_w_qg", "xa_w_kv", "xa_w_o")
SHARDED_F32 = (("ev_conv_w", 2), ("od_norm_g", 1))
SMALL = ("mem_norm_g", "ev_norm_g", "ev_s5_lambda_re", "ev_s5_lambda_im", "ev_s5_log_dt", "ev_s5_b_re", "ev_s5_b_im",
         "ev_s5_c_re", "ev_s5_c_im", "ev_s5_d", "ev_s5_glu_b", "ev_conv_b", "ev_conv_ln_g", "ev_conv_ln_b",
         "od_rel_bias", "xa_norm_g", "final_norm_g")
NARROW = ("ev_s5_c_re", "ev_s5_c_im")
DENSE_B = ("ev_s5_b_re", "ev_s5_b_im")
PACK_COLS = 1024
PACKED_SMALL = tuple(n for n in SMALL if n not in NARROW and n != "ev_norm_g")
SINGLE_SMALL = NARROW + tuple(n for n, _ in SHARDED_F32)
WEIGHTS = ("mem_norm_g", "ev_norm_g", "ev_w_in", "ev_s5_lambda_re", "ev_s5_lambda_im", "ev_s5_log_dt", "ev_s5_b_re",
           "ev_s5_b_im", "ev_s5_c_re", "ev_s5_c_im", "ev_s5_d", "ev_s5_glu_w", "ev_s5_glu_b", "ev_conv_w", "ev_conv_b",
           "ev_conv_ln_g", "ev_conv_ln_b", "ev_w_out", "od_norm_g", "od_w_in", "od_rel_bias", "od_w_out", "xa_norm_g",
           "xa_w_qg", "xa_w_kv", "xa_w_o", "final_norm_g")


def _as2d(a):
    return a.reshape(1, -1) if a.ndim == 1 else a.reshape(-1, a.shape[-1])


def kernel(x, mem, mem_norm_g, ev_norm_g, ev_w_in, ev_s5_lambda_re, ev_s5_lambda_im, ev_s5_log_dt, ev_s5_b_re, ev_s5_b_im, ev_s5_c_re, ev_s5_c_im, ev_s5_d, ev_s5_glu_w, ev_s5_glu_b, ev_conv_w, ev_conv_b, ev_conv_ln_g, ev_conv_ln_b, ev_w_out, od_norm_g, od_w_in, od_rel_bias, od_w_out, xa_norm_g, xa_w_qg, xa_w_kv, xa_w_o, final_norm_g, loss_target, m_mem_norm_g, m_ev_norm_g, m_ev_w_in, m_ev_s5_lambda_re, m_ev_s5_lambda_im, m_ev_s5_log_dt, m_ev_s5_b_re, m_ev_s5_b_im, m_ev_s5_c_re, m_ev_s5_c_im, m_ev_s5_d, m_ev_s5_glu_w, m_ev_s5_glu_b, m_ev_conv_w, m_ev_conv_b, m_ev_conv_ln_g, m_ev_conv_ln_b, m_ev_w_out, m_od_norm_g, m_od_w_in, m_od_rel_bias, m_od_w_out, m_xa_norm_g, m_xa_w_qg, m_xa_w_kv, m_xa_w_o, m_final_norm_g, v_mem_norm_g, v_ev_norm_g, v_ev_w_in, v_ev_s5_lambda_re, v_ev_s5_lambda_im, v_ev_s5_log_dt, v_ev_s5_b_re, v_ev_s5_b_im, v_ev_s5_c_re, v_ev_s5_c_im, v_ev_s5_d, v_ev_s5_glu_w, v_ev_s5_glu_b, v_ev_conv_w, v_ev_conv_b, v_ev_conv_ln_g, v_ev_conv_ln_b, v_ev_w_out, v_od_norm_g, v_od_w_in, v_od_rel_bias, v_od_w_out, v_xa_norm_g, v_xa_w_qg, v_xa_w_kv, v_xa_w_o, v_final_norm_g):
    a = dict(locals())
    w = {n: a[n] for n in WEIGHTS}
    shard = (2 * lax.axis_index("x") + lax.axis_index("y")).reshape(1).astype(jnp.int32)
    core = lax.axis_index("c").reshape(1).astype(jnp.int32)

    place = jnp.concatenate([shard, core])

    blocks = {n: w[n].astype(BF16).reshape(-1, w[n].shape[-1]) for n in BIG}
    conv_blk = jnp.pad(_as2d(w["ev_conv_w"]), ((0, 1), (0, 0)))
    odn_blk = w["od_norm_g"].reshape(2, -1)
    bias, evin_g, conv_g, odn_g = att_bias(w["od_rel_bias"][0],
                                           carried=carried_allgather([blocks["ev_w_in"], conv_blk, odn_blk]))
    gw = {"ev_w_in": evin_g}
    p = {n: w[n] for n in SMALL}
    p["ev_conv_w"] = jnp.concatenate([conv_g[s, :CONV_KERNEL] for s in range(N_CHIPS)], axis=1)[None]
    p["od_norm_g"] = odn_g.reshape(1, D_MODEL)

    loss, grad_x, g, reduced, dgev, gath, slots = local_step(x[0], mem[0], loss_target[0], p, gw,
                                                             {n: blocks[n] for n in LATE}, bias, place, core)
    loss = lax.psum(loss[0, 0], ("x", "y", "c"))
    g_big = dict(zip(BIG, sibling_share([reduced[n] for n in BIG])))

    out = {tag: {} for tag in ("grad", "delta", "m", "v")}
    for n in BIG:
        sh = w[n].shape
        to2d = lambda t: t.reshape(-1, sh[-1])
        gn = to2d(g_big[n])
        d, mn, vn = adamw(to2d(w[n]), gn, to2d(a["m_" + n]), to2d(a["v_" + n]), "adamw_" + n)
        for tag, val in zip(("grad", "delta", "m", "v"), (gn, d, mn, vn)):
            out[tag][n] = val.reshape(sh)

    jobs = [(n, gath[0], s) for n, s in zip(PACKED_SMALL, slots)]
    jobs += [(n, gt, None) for n, gt in zip(SINGLE_SMALL, gath[1:])]
    jobs += [("ev_norm_g", allgather_devices([dgev])[0], None)]
    for n, gt, slot in jobs:
        sh = w[n].shape
        w2, m2, v2 = _as2d(w[n]), _as2d(a["m_" + n]), _as2d(a["v_" + n])
        if n in DENSE_B:
            gn = _as2d(s5_b_from_dense(sum_slot(gt, slot, g[n].shape[-2:], "sum_" + n)))
            d, mn, vn = adamw(w2, gn, m2, v2, "adamw_" + n)
        else:
            gn, d, mn, vn = adamw_allreduce(gt, w2, m2, v2, shard, "adamw_" + n, slot=slot)
        for tag, val in zip(("grad", "delta", "m", "v"), (gn, d, mn, vn)):
            out[tag][n] = val.reshape(sh)

    res = [loss, grad_x[None]]
    for tag in ("grad", "delta", "m", "v"):
        res += [out[tag][n] for n in WEIGHTS]
    return tuple(res)
```

```python
import math

import jax
import jax.numpy as jnp
import numpy as np
from jax import lax
from jax.experimental import pallas as pl
from jax.experimental.pallas import tpu as pltpu

F32 = jnp.float32
BF16 = jnp.bfloat16

D_MODEL = 1024
CHUNK = 64
LEFT_CHUNKS = 8
S5_WIDTH = 512
S5_GROUP = 16
S5_GROUPS = 32
S5_STATE = 64
S5_COLS = S5_GROUPS * S5_STATE
S5_SPLIT = 4
S5_CC = S5_COLS // S5_SPLIT
S5_UC = S5_WIDTH // S5_SPLIT
CONV_WIDTH = 512
CONV_KERNEL = 31
CONV_HALO = 32
ATT_HEADS = 16
ATT_HEAD_DIM = 64
MAX_REL = 128
MEM_LEN = 256
XA_HEADS = 4
XA_HEAD_DIM = 256
EPS = 1e-6
EVEN_IN = 2560
ODD_IN = 4096

ADAM_LR = 0.001
ADAM_B1 = 0.9
ADAM_B2 = 0.999
ADAM_EPS = 1e-08
ADAM_WD = 0.01
ADAM_STEP = 10

ROW_TILE = 256
MM_TILE = 512
S5_TILE = 512
ATT_QB = 256
ATT_PAD = LEFT_CHUNKS * CHUNK
ATT_WIN = ATT_PAD + ATT_QB
VMEM_LIMIT_V7X = 56 * 1024 * 1024
NEG = -1e30
LANES = 128
N_CHIPS = 4
N_DEV = 8

MESH = pl.DeviceIdType.MESH
ANY = pl.BlockSpec(memory_space=pl.ANY)


def _cp(*sem, vmem=VMEM_LIMIT_V7X):
    return pltpu.CompilerParams(dimension_semantics=sem if sem else None, vmem_limit_bytes=vmem)


def _full(shape):
    n = len(shape)
    return pl.BlockSpec(shape, lambda *_: (0,) * n)


def _wspec(w, layer=None):
    if layer is None:
        return _full(w.shape)
    s, _, r, c = w.shape
    return pl.BlockSpec((s, None, r, c), lambda *_: (0, layer, 0, 0))


def _lane_tile(n, cap):
    return max(t for t in range(LANES, min(n, cap) + 1, LANES) if n % t == 0)


def _sigmoid(x):
    return 1.0 / (1.0 + jnp.exp(-x))


def _silu(x):
    return x * _sigmoid(x)


def _silu_pair(x):
    s = _sigmoid(x)
    return x * s, s * (1.0 + x * (1.0 - s))


_GELU_C = math.sqrt(2.0 / math.pi)


def _gelu(x):
    return 0.5 * x * (1.0 + jnp.tanh(_GELU_C * (x + 0.044715 * x * x * x)))


def _dgelu(x):
    t = jnp.tanh(_GELU_C * (x + 0.044715 * x * x * x))
    return 0.5 * (1.0 + t) + 0.5 * x * (1.0 - t * t) * _GELU_C * (1.0 + 3.0 * 0.044715 * x * x)


def _dot(a, b):
    return jnp.dot(a, b, preferred_element_type=F32)


def _dot_nt(a, b):
    return lax.dot_general(a, b, (((1,), (1,)), ((), ())), preferred_element_type=F32)


def _dot_tn(a, b):
    return lax.dot_general(a, b, (((0,), (0,)), ((), ())), preferred_element_type=F32)


def _dot_cols(a, w4, shards=range(N_CHIPS)):
    return jnp.concatenate([_dot(a, w4[s]) for s in shards], axis=1)


def _dot_rows(a, w4):
    r = w4.shape[1]
    acc = _dot(a[:, 0:r], w4[0])
    for s in range(1, N_CHIPS):
        acc = acc + _dot(a[:, s * r:(s + 1) * r], w4[s])
    return acc


def _dot_nt_cols(dys, w4):
    acc = _dot_nt(dys[0], w4[0])
    for s in range(1, N_CHIPS):
        acc = acc + _dot_nt(dys[s], w4[s])
    return acc


def _dot_nt_rows(dy, w4):
    return jnp.concatenate([_dot_nt(dy, w4[s]) for s in range(N_CHIPS)], axis=1)


def _col_pieces(v, n):
    return [v[:, s * n:(s + 1) * n] for s in range(N_CHIPS)]


def _rms_parts(xv):
    inv = lax.rsqrt(jnp.mean(xv * xv, axis=-1, keepdims=True) + EPS)
    return inv, xv * inv


def _rms_bwd(xv, g, dh):
    inv, xhat = _rms_parts(xv)
    dg = jnp.sum(dh * xhat, axis=0, keepdims=True)
    dxh = dh * g
    dx = inv * (dxh - xhat * jnp.mean(dxh * xhat, axis=-1, keepdims=True))
    return dx, dg


def norm_mm(x, g, w4, groups, name, tm=MM_TILE):
    M, D = x.shape
    n = w4.shape[2]
    tm = min(tm, M)

    def body(x_ref, g_ref, w_ref, *outs):
        _, xhat = _rms_parts(x_ref[...])
        hb = (xhat * g_ref[...]).astype(BF16)
        for o, (shards, dt, _) in zip(outs, groups):
            o[...] = _dot_cols(hb, w_ref, shards).astype(dt)
        outs[-1][...] = hb

    out_shape = [jax.ShapeDtypeStruct((M + pad, len(sh) * n), dt) for (sh, dt, pad) in groups]
    out_specs = [pl.BlockSpec((tm, len(sh) * n), lambda i, p=pad // tm: (i + p, 0)) for (sh, _, pad) in groups]
    out_shape.append(jax.ShapeDtypeStruct((M, D), BF16))
    out_specs.append(pl.BlockSpec((tm, D), lambda i: (i, 0)))
    return pl.pallas_call(
        body, name=name, grid=(M // tm,),
        in_specs=[pl.BlockSpec((tm, D), lambda i: (i, 0)), _full(g.shape), _full(w4.shape)],
        out_specs=out_specs, out_shape=out_shape, compiler_params=_cp("parallel"),
    )(x, g, w4)


def zero_rows(buf, rows, name, tm=ROW_TILE):
    C = buf.shape[1]

    def body(b_ref, o_ref):
        o_ref[...] = jnp.zeros_like(o_ref)

    return pl.pallas_call(
        body, name=name, grid=(rows // tm,), in_specs=[ANY],
        out_specs=pl.BlockSpec((tm, C), lambda i: (i, 0)),
        out_shape=jax.ShapeDtypeStruct(buf.shape, buf.dtype), input_output_aliases={0: 0},
        compiler_params=_cp("parallel"),
    )(buf)


def mm_res(a, w4, res, name, tm=MM_TILE):
    M, K = a.shape
    N = w4.shape[2]
    tm = min(tm, M)

    def body(a_ref, w_ref, r_ref, o_ref):
        o_ref[...] = r_ref[...] + _dot_rows(a_ref[...], w_ref)

    return pl.pallas_call(
        body, name=name, grid=(M // tm,),
        in_specs=[pl.BlockSpec((tm, K), lambda i: (i, 0)), _full(w4.shape), pl.BlockSpec((tm, N), lambda i: (i, 0))],
        out_specs=pl.BlockSpec((tm, N), lambda i: (i, 0)),
        out_shape=jax.ShapeDtypeStruct((M, N), F32), compiler_params=_cp("parallel"),
    )(a, w4, res)


def mm_cols(a, w, layer, name, out_dtype):
    M = a.shape[0]
    n = w.shape[3]

    def body(a_ref, w_ref, o_ref):
        o_ref[...] = _dot_cols(a_ref[...], w_ref).astype(out_dtype)

    return pl.pallas_call(
        body, name=name, grid=(1,), in_specs=[_full(a.shape), _wspec(w, layer)],
        out_specs=_full((M, N_CHIPS * n)), out_shape=jax.ShapeDtypeStruct((M, N_CHIPS * n), out_dtype),
        compiler_params=_cp("arbitrary"),
    )(a, w)


def mm_nt_cols(dy, w, layer, name):
    M = dy.shape[0]
    K, n = w.shape[2], w.shape[3]

    def body(d_ref, w_ref, o_ref):
        o_ref[...] = _dot_nt_cols(_col_pieces(d_ref[...].astype(BF16), n), w_ref)

    return pl.pallas_call(
        body, name=name, grid=(1,), in_specs=[_full(dy.shape), _wspec(w, layer)],
        out_specs=_full((M, K)), out_shape=jax.ShapeDtypeStruct((M, K), F32), compiler_params=_cp("arbitrary"),
    )(dy, w)


def mm_nt_rows(dy, w4, name, tm=MM_TILE):
    M, N = dy.shape
    K = N_CHIPS * w4.shape[1]
    tm = min(tm, M)

    def body(d_ref, w_ref, o_ref):
        o_ref[...] = _dot_nt_rows(d_ref[...].astype(BF16), w_ref)

    return pl.pallas_call(
        body, name=name, grid=(M // tm,),
        in_specs=[pl.BlockSpec((tm, N), lambda i: (i, 0)), _full(w4.shape)],
        out_specs=pl.BlockSpec((tm, K), lambda i: (i, 0)),
        out_shape=jax.ShapeDtypeStruct((M, K), F32), compiler_params=_cp("parallel"),
    )(dy, w4)


def mm_nt_normbwd(dys, offs, w4, x, g, dx_out, name, tm=MM_TILE):
    M, D = x.shape
    n = w4.shape[2]
    tm = min(tm, M)
    nd = len(dys)

    def body(*refs):
        d_refs = refs[:nd]
        w_ref, x_ref, g_ref, dxo_ref, dx_ref, dg_ref = refs[nd:]
        if nd == 1:
            pieces = _col_pieces(d_refs[0][...].astype(BF16), n)
        else:
            pieces = [r[...].astype(BF16) for r in d_refs]
        dh = _dot_nt_cols(pieces, w_ref)
        dx, dg = _rms_bwd(x_ref[...], g_ref[...], dh)
        dx_ref[...] = dxo_ref[...] + dx

        @pl.when(pl.program_id(0) == 0)
        def _():
            dg_ref[...] = jnp.zeros_like(dg_ref)

        dg_ref[...] += dg

    row = lambda c, off=0: pl.BlockSpec((tm, c), lambda i, p=off // tm: (i + p, 0))
    return pl.pallas_call(
        body, name=name, grid=(M // tm,),
        in_specs=[row(d.shape[1], off) for d, off in zip(dys, offs)] + [_full(w4.shape), row(D), _full(g.shape), row(D)],
        out_specs=[row(D), _full((1, D))],
        out_shape=[jax.ShapeDtypeStruct((M, D), F32), jax.ShapeDtypeStruct((1, D), F32)],
        compiler_params=_cp("arbitrary"),
    )(*dys, w4, x, g, dx_out)


def mm_tn(a, b, name, layout, into=None, b_off=0, out_dtype=BF16, bm=1024, bn=1280, bl=1024, carried=None):
    L, K = a.shape
    N = b.shape[1]
    kind = layout[0]
    arg = layout[1] if len(layout) > 1 else None
    bm, bn, bl = _lane_tile(K, bm), _lane_tile(N, bn), min(bl, L)
    assert L % bl == 0 and b_off % bl == 0, (L, bl, b_off)
    nl = L // bl
    n_sh, r_sh = N // N_CHIPS, K // N_CHIPS
    lay = (None,) if arg is None else (None, None)
    mid = () if arg is None else (arg,)
    gs = 1
    if kind == "plain":
        oshape, oblock, oidx = (K, N), (bm, bn), lambda i, j, l: (i, j)
    elif kind == "slab":
        oshape, oblock, oidx = (N_CHIPS, K, N), (None, bm, bn), lambda i, j, l: (arg, i, j)
    elif kind == "cols":
        bn = max(bn - bn % n_sh, n_sh) if bn >= n_sh else _lane_tile(n_sh, bn)
        gs = max(bn // n_sh, 1)
        per = n_sh // bn if gs == 1 else 1
        oshape = (N_CHIPS,) + ((2,) if arg is not None else ()) + (K, n_sh)
        oblock = ((gs,) if gs > 1 else (None,)) + lay[1:] + (bm, min(bn, n_sh))
        oidx = lambda i, j, l: (j // per,) + mid + (i, j % per)
    else:
        bm = max(bm - bm % r_sh, r_sh) if bm >= r_sh else _lane_tile(r_sh, bm)
        gs = max(bm // r_sh, 1)
        per = r_sh // bm if gs == 1 else 1
        oshape = (N_CHIPS,) + ((2,) if arg is not None else ()) + (r_sh, N)
        oblock = ((gs,) if gs > 1 else (None,)) + lay[1:] + (min(bm, r_sh), bn)
        oidx = lambda i, j, l: (i // per,) + mid + (i % per, j)
    assert K % bm == 0 and N % bn == 0, (K, bm, N, bn)

    grid = (K // bm, N // bn, nl)

    def body(*refs):
        top = end = None
        if carried is not None:
            refs, parts = carried.split(refs, 2 if into is None else 3, 1, 1)
            top, end = carried.hooks(parts, grid)
            top()
        a_ref, b_ref, o_ref, acc = refs[0], refs[1], refs[-2], refs[-1]
        l = pl.program_id(2)

        @pl.when(l == 0)
        def _():
            acc[...] = jnp.zeros_like(acc)

        acc[...] += _dot_tn(a_ref[...].astype(BF16), b_ref[...].astype(BF16))

        @pl.when(l == nl - 1)
        def _():
            if gs == 1:
                o_ref[...] = acc[...].astype(out_dtype)
            elif kind == "cols":
                for t in range(gs):
                    o_ref[t] = acc[:, t * n_sh:(t + 1) * n_sh].astype(out_dtype)
            else:
                for t in range(gs):
                    o_ref[t] = acc[t * r_sh:(t + 1) * r_sh, :].astype(out_dtype)

        if end is not None:
            end()

    in_specs = [pl.BlockSpec((bl, bm), lambda i, j, l: (l, i)),
                pl.BlockSpec((bl, bn), lambda i, j, l, p=b_off // bl: (l + p, j))]
    args = [a, b]
    alias = {}
    if into is not None:
        in_specs.append(ANY)
        args.append(into)
        alias = {2: 0}
    out_specs, out_shape = pl.BlockSpec(oblock, oidx), jax.ShapeDtypeStruct(oshape, out_dtype)
    scratch = [pltpu.VMEM((bm, bn), F32)]
    if carried is None:
        sem = ("parallel", "parallel", "arbitrary")
    else:
        in_specs += [ANY] * len(carried.arrays)
        args += carried.arrays
        out_specs, out_shape = [out_specs] + [ANY] * len(carried.out_shapes), [out_shape] + carried.out_shapes
        scratch += carried.sems
        sem = ("arbitrary",) * 3
    return pl.pallas_call(
        body, name=name, grid=grid, in_specs=in_specs, out_specs=out_specs, out_shape=out_shape,
        scratch_shapes=scratch, input_output_aliases=alias, compiler_params=_cp(*sem),
    )(*args)


def rms_fwd(x, g, name):
    def body(x_ref, g_ref, ob_ref):
        _, xhat = _rms_parts(x_ref[...])
        ob_ref[...] = (xhat * g_ref[...]).astype(BF16)

    return pl.pallas_call(body, name=name, out_shape=jax.ShapeDtypeStruct(x.shape, BF16))(x, g)


def rms_dgain(x, dy0, dy1, name):
    def body(x_ref, d0_ref, d1_ref, o_ref):
        _, xhat = _rms_parts(x_ref[...])
        o_ref[...] = jnp.sum((d0_ref[...] + d1_ref[...]) * xhat, axis=0, keepdims=True)

    return pl.pallas_call(body, name=name, out_shape=jax.ShapeDtypeStruct((1, x.shape[1]), F32))(x, dy0, dy1)


def _s5_discretise(lr, li, logdt, bt_re, bt_im):
    dt = jnp.exp(logdt)
    mag = jnp.exp(lr * dt)
    ab_re = mag * jnp.cos(li * dt)
    ab_im = mag * jnp.sin(li * dt)
    den = lr * lr + li * li
    nr = ab_re - 1.0
    coef_re = (nr * lr + ab_im * li) / den
    coef_im = (ab_im * lr - nr * li) / den
    cr = coef_re[:, None, :]
    ci = coef_im[:, None, :]
    bb_re = cr * bt_re - ci * bt_im
    bb_im = cr * bt_im + ci * bt_re
    return ab_re, ab_im, bb_re, bb_im


def s5_param_fwd(lr, li, logdt, bt_re, bt_im):
    def body(lr_ref, li_ref, ld_ref, br_ref, bi_ref, bbr_ref, bbi_ref):
        _, _, bb_re, bb_im = _s5_discretise(lr_ref[...], li_ref[...], ld_ref[...], br_ref[...], bi_ref[...])
        bbr_ref[...] = bb_re
        bbi_ref[...] = bb_im

    sh = jax.ShapeDtypeStruct(bt_re.shape, F32)
    return pl.pallas_call(body, name="s5_param_fwd", out_shape=[sh, sh])(lr, li, logdt, bt_re, bt_im)


def s5_param_bwd(lr, li, logdt, bt_re, bt_im, d_ab_re, d_ab_im, d_bb_re, d_bb_im):
    def body(lr_ref, li_ref, ld_ref, br_ref, bi_ref, dar_ref, dai_ref, dbr_ref, dbi_ref,
             o_lr, o_li, o_ld, o_br, o_bi):
        _, vjp = jax.vjp(_s5_discretise, lr_ref[...], li_ref[...], ld_ref[...], br_ref[...], bi_ref[...])
        g = vjp((dar_ref[...], dai_ref[...], dbr_ref[...], dbi_ref[...]))
        for o, v in zip((o_lr, o_li, o_ld), g[:3]):
            o[...] = v
        for o, v in zip((o_br, o_bi), g[3:]):
            for c in range(S5_GROUP):
                o[:, c * S5_STATE:(c + 1) * S5_STATE] = v[:, c, :]

    dense = jax.ShapeDtypeStruct((S5_GROUPS, S5_GROUP * S5_STATE), F32)
    shapes = [jax.ShapeDtypeStruct(a.shape, F32) for a in (lr, li, logdt)] + [dense, dense]
    return pl.pallas_call(body, name="s5_param_bwd", out_shape=shapes)(
        lr, li, logdt, bt_re, bt_im, d_ab_re, d_ab_im, d_bb_re, d_bb_im)


def s5_tables(lr_flat, li_flat, logdt_flat):
    def body(lr_ref, li_ref, ld_ref, tab_ref):
        dt = jnp.exp(ld_ref[...])
        a = lr_ref[...] * dt
        th = li_ref[...] * dt
        row = lax.broadcasted_iota(jnp.int32, (8, 1), 0)
        rowf = row.astype(F32)

        def power(e, sign):
            m = jnp.exp(e * a)
            return m * jnp.cos(e * th), sign * m * jnp.sin(e * th)

        k = 0
        for sign, fwd in ((1.0, True), (-1.0, False)):
            for s in (1, 2, 4):
                pr, pi = power(jnp.full((8, 1), float(s), F32), sign)
                keep = (row >= s) if fwd else (row + s < 8)
                tab_ref[k] = jnp.where(keep, pr, 0.0)
                tab_ref[k + 1] = jnp.where(keep, pi, 0.0)
                k += 2
            e = rowf + 1.0 if fwd else 8.0 - rowf
            pr, pi = power(e, sign)
            tab_ref[k] = pr
            tab_ref[k + 1] = pi
            k += 2

    return pl.pallas_call(body, name="s5_tables",
                          out_shape=jax.ShapeDtypeStruct((16, 8, S5_COLS), F32))(lr_flat, li_flat, logdt_flat)


def _scan_block(a, b, tabs, base, cr, ci, reverse):
    for n, s in enumerate((1, 2, 4)):
        mr = tabs[base + 2 * n]
        mi = tabs[base + 2 * n + 1]
        sh = (8 - s) if reverse else s
        ar = pltpu.roll(a, sh, 0)
        br = pltpu.roll(b, sh, 0)
        a, b = a + mr * ar - mi * br, b + mr * br + mi * ar
    pr = tabs[base + 6]
    pi = tabs[base + 7]
    a, b = a + pr * cr - pi * ci, b + pr * ci + pi * cr
    return a, b


class Carried:
    def __init__(self, arrays, out_shapes, sems, start, finish):
        self.arrays, self.out_shapes, self.sems = list(arrays), list(out_shapes), list(sems)
        self.start, self.finish = start, finish

    def split(self, refs, n_in, n_out, n_scratch):
        a, o, s = len(self.arrays), len(self.out_shapes), len(self.sems)
        own_in, car_in = refs[:n_in], refs[n_in:n_in + a]
        own_out, car_out = refs[n_in + a:n_in + a + n_out], refs[n_in + a + n_out:n_in + a + n_out + o]
        rest = refs[n_in + a + n_out + o:]
        return own_in + own_out + rest[:n_scratch], (car_in, car_out, rest[n_scratch:n_scratch + s])

    def hooks(self, parts, grid):
        first = last = None
        for k, n in enumerate(grid):
            i = pl.program_id(k)
            first = (i == 0) if first is None else first & (i == 0)
            last = (i == n - 1) if last is None else last & (i == n - 1)

        def top():
            pl.when(first)(lambda: self.start(*parts))

        def end():
            pl.when(last)(lambda: self.finish(*parts))

        return top, end


def s5_fwd(z, bbd_re, bbd_im, ccd_re, ccd_im, tab, dskip, tm=S5_TILE, carried=None):
    L = z.shape[0]
    tm = min(tm, L)
    nt = L // tm

    def body(*refs):
        top = end = None
        if carried is not None:
            refs, parts = carried.split(refs, 7, 4, 3)
            top, end = carried.hooks(parts, (S5_SPLIT, nt))
            top()
        u_ref, bbr_ref, bbi_ref, ccr_ref, cci_ref, tab_ref, d_ref, y_ref, ck_ref, hr_ref, hi_ref, xr, xi, car = refs
        t = pl.program_id(1)

        @pl.when(t == 0)
        def _():
            car[...] = jnp.zeros_like(car)

        u = u_ref[...]
        ub = u.astype(BF16)
        xr[...] = _dot(ub, bbr_ref[...])
        xi[...] = _dot(ub, bbi_ref[...])
        tabs = [tab_ref[k] for k in range(8)]

        def blk(i, c):
            r0 = pl.multiple_of(i * 8, 8)
            a, b = _scan_block(xr[pl.ds(r0, 8), :], xi[pl.ds(r0, 8), :], tabs, 0, c[0], c[1], False)
            xr[pl.ds(r0, 8), :] = a
            xi[pl.ds(r0, 8), :] = b
            return a[7:8, :], b[7:8, :]

        cr, ci = lax.fori_loop(0, tm // 8, blk, (car[0:1, :], car[1:2, :]))
        car[0:1, :] = cr
        car[1:2, :] = ci
        ck_ref[0:1, :] = cr
        ck_ref[1:2, :] = ci
        hrb = xr[...].astype(BF16)
        hib = xi[...].astype(BF16)
        hr_ref[...] = hrb
        hi_ref[...] = hib
        y_ref[...] = _dot(hrb, ccr_ref[...]) - _dot(hib, cci_ref[...]) + d_ref[...] * u
        if end is not None:
            end()

    extra = carried.arrays if carried is not None else []
    extra_out = carried.out_shapes if carried is not None else []
    extra_sems = carried.sems if carried is not None else []
    return pl.pallas_call(
        body, name="s5_fwd", grid=(S5_SPLIT, nt),
        in_specs=[pl.BlockSpec((tm, S5_UC), lambda j, t: (t, j)),
                  pl.BlockSpec((None, S5_UC, S5_CC), lambda j, t: (j, 0, 0)),
                  pl.BlockSpec((None, S5_UC, S5_CC), lambda j, t: (j, 0, 0)),
                  pl.BlockSpec((None, S5_CC, S5_UC), lambda j, t: (j, 0, 0)),
                  pl.BlockSpec((None, S5_CC, S5_UC), lambda j, t: (j, 0, 0)),
                  pl.BlockSpec((8, 8, S5_CC), lambda j, t: (0, 0, j)),
                  pl.BlockSpec((1, S5_UC), lambda j, t: (0, j))] + [ANY] * len(extra),
        out_specs=[pl.BlockSpec((tm, S5_UC), lambda j, t: (t, j)),
                   pl.BlockSpec((None, 2, S5_CC), lambda j, t: (t, 0, j)),
                   pl.BlockSpec((tm, S5_CC), lambda j, t: (t, j)),
                   pl.BlockSpec((tm, S5_CC), lambda j, t: (t, j))] + [ANY] * len(extra_out),
        out_shape=[jax.ShapeDtypeStruct((L, S5_WIDTH), F32), jax.ShapeDtypeStruct((nt, 2, S5_COLS), F32),
                   jax.ShapeDtypeStruct((L, S5_COLS), BF16), jax.ShapeDtypeStruct((L, S5_COLS), BF16)] + extra_out,
        scratch_shapes=[pltpu.VMEM((tm, S5_CC), F32), pltpu.VMEM((tm, S5_CC), F32), pltpu.VMEM((2, S5_CC), F32)]
        + extra_sems,
        compiler_params=_cp("arbitrary" if carried is not None else "parallel", "arbitrary"),
    )(z, bbd_re, bbd_im, ccd_re, ccd_im, tab, dskip, *extra)


def s5_bwd(z, dy, dz, ckpt, hrb, hib, bbd_re, bbd_im, ccd_re, ccd_im, tab, dskip, tm=S5_TILE, carried=None):
    L = z.shape[0]
    tm = min(tm, L)
    nt = L // tm

    def body(*refs):
        top = end = None
        if carried is not None:
            refs, parts = carried.split(refs, 12, 7, 7)
            top, end = carried.hooks(parts, (S5_SPLIT, nt))
            top()
        (u_ref, dy_ref, dz_ref, ck_ref, hrb_ref, hib_ref, bbr_ref, bbi_ref, ccr_ref, cci_ref, tab_ref, d_ref,
         du_ref, da_ref, dbr_ref, dbi_ref, dcr_ref, dci_ref, dd_ref, hr, hi, gr, gi, car, acr, aci) = refs
        t = pl.program_id(1)
        tt = nt - 1 - t

        @pl.when(t == 0)
        def _():
            for r in (car, acr, aci, dbr_ref, dbi_ref, dcr_ref, dci_ref, dd_ref):
                r[...] = jnp.zeros_like(r)

        u = u_ref[...]
        ub = u.astype(BF16)
        dyv = dy_ref[...]
        dyb = dyv.astype(BF16)
        tabs = [None] * 8 + [tab_ref[k] for k in range(8, 16)]

        live = (tt > 0).astype(F32)
        hr[0:8, :] = jnp.broadcast_to(ck_ref[0:1, :] * live, (8, S5_CC))
        hi[0:8, :] = jnp.broadcast_to(ck_ref[1:2, :] * live, (8, S5_CC))
        hrb = hrb_ref[...]
        hib = hib_ref[...]
        hr[8:, :] = hrb.astype(F32)
        hi[8:, :] = hib.astype(F32)
        dcr_ref[...] += _dot_tn(hrb, dyb)
        dci_ref[...] -= _dot_tn(hib, dyb)

        gr[...] = _dot_nt(dyb, ccr_ref[...])
        gi[...] = -_dot_nt(dyb, cci_ref[...])
        row0 = lax.broadcasted_iota(jnp.int32, (8, S5_CC), 0) == 0

        def rblk(k, c):
            i = tm // 8 - 1 - k
            r0 = pl.multiple_of(i * 8, 8)
            a, b = _scan_block(gr[pl.ds(r0, 8), :], gi[pl.ds(r0, 8), :], tabs, 8, c[0], c[1], True)
            gr[pl.ds(r0, 8), :] = a
            gi[pl.ds(r0, 8), :] = b
            r1 = pl.multiple_of(i * 8 + 8, 8)
            hpr = jnp.where(row0, pltpu.roll(hr[pl.ds(r0, 8), :], 1, 0), pltpu.roll(hr[pl.ds(r1, 8), :], 1, 0))
            hpi = jnp.where(row0, pltpu.roll(hi[pl.ds(r0, 8), :], 1, 0), pltpu.roll(hi[pl.ds(r1, 8), :], 1, 0))
            acr[...] += a * hpr + b * hpi
            aci[...] += b * hpr - a * hpi
            return a[0:1, :], b[0:1, :]

        cr, ci = lax.fori_loop(0, tm // 8, rblk, (car[0:1, :], car[1:2, :]))
        car[0:1, :] = cr
        car[1:2, :] = ci

        grb = gr[...].astype(BF16)
        gib = gi[...].astype(BF16)
        du_ref[...] = (_dot_nt(grb, bbr_ref[...]) + _dot_nt(gib, bbi_ref[...]) + d_ref[...] * dyv).astype(BF16)
        dbr_ref[...] += _dot_tn(ub, grb)
        dbi_ref[...] += _dot_tn(ub, gib)
        dd_ref[...] += jnp.sum(dyv * u, axis=0, keepdims=True)

        @pl.when(t == nt - 1)
        def _():
            da_ref[0:1, :] = jnp.sum(acr[...], axis=0, keepdims=True)
            da_ref[1:2, :] = jnp.sum(aci[...], axis=0, keepdims=True)

        if end is not None:
            end()

    extra = carried.arrays if carried is not None else []
    extra_out = carried.out_shapes if carried is not None else []
    extra_sems = carried.sems if carried is not None else []
    chunk = lambda a, b: pl.BlockSpec((None, a, b), lambda j, t: (j, 0, 0))
    return pl.pallas_call(
        body, name="s5_bwd", grid=(S5_SPLIT, nt),
        in_specs=[pl.BlockSpec((tm, S5_UC), lambda j, t: (nt - 1 - t, j)),
                  pl.BlockSpec((tm, S5_UC), lambda j, t: (nt - 1 - t, j)),
                  ANY,
                  pl.BlockSpec((None, 2, S5_CC), lambda j, t: (jnp.maximum(nt - 2 - t, 0), 0, j)),
                  pl.BlockSpec((tm, S5_CC), lambda j, t: (nt - 1 - t, j)),
                  pl.BlockSpec((tm, S5_CC), lambda j, t: (nt - 1 - t, j)),
                  chunk(S5_UC, S5_CC), chunk(S5_UC, S5_CC), chunk(S5_CC, S5_UC), chunk(S5_CC, S5_UC),
                  pl.BlockSpec((16, 8, S5_CC), lambda j, t: (0, 0, j)),
                  pl.BlockSpec((1, S5_UC), lambda j, t: (0, j))] + [ANY] * len(extra),
        out_specs=[pl.BlockSpec((tm, S5_UC), lambda j, t: (nt - 1 - t, j)),
                   pl.BlockSpec((None, 2, S5_CC), lambda j, t: (j, 0, 0)),
                   chunk(S5_UC, S5_CC), chunk(S5_UC, S5_CC), chunk(S5_CC, S5_UC), chunk(S5_CC, S5_UC),
                   pl.BlockSpec((1, S5_UC), lambda j, t: (0, j))] + [ANY] * len(extra_out),
        out_shape=[jax.ShapeDtypeStruct(dz.shape, dz.dtype),
                   jax.ShapeDtypeStruct((S5_SPLIT, 2, S5_CC), F32),
                   jax.ShapeDtypeStruct((S5_SPLIT, S5_UC, S5_CC), F32),
                   jax.ShapeDtypeStruct((S5_SPLIT, S5_UC, S5_CC), F32),
                   jax.ShapeDtypeStruct((S5_SPLIT, S5_CC, S5_UC), F32),
                   jax.ShapeDtypeStruct((S5_SPLIT, S5_CC, S5_UC), F32),
                   jax.ShapeDtypeStruct((1, S5_WIDTH), F32)] + extra_out,
        scratch_shapes=[pltpu.VMEM((tm + 8, S5_CC), F32), pltpu.VMEM((tm + 8, S5_CC), F32),
                        pltpu.VMEM((tm, S5_CC), F32), pltpu.VMEM((tm, S5_CC), F32),
                        pltpu.VMEM((2, S5_CC), F32), pltpu.VMEM((8, S5_CC), F32), pltpu.VMEM((8, S5_CC), F32)]
        + extra_sems,
        input_output_aliases={2: 0},
        compiler_params=_cp("arbitrary" if carried is not None else "parallel", "arbitrary"),
    )(z, dy, dz, ckpt, hrb, hib, bbd_re, bbd_im, ccd_re, ccd_im, tab, dskip, *extra)


_EYE8 = np.eye(S5_GROUPS // S5_SPLIT, dtype=np.float32)


def _blockdiag(a):
    g, r, c = a.shape
    a = a.reshape(S5_SPLIT, g // S5_SPLIT, r, c)
    out = a[:, :, :, None, :] * _EYE8[None, :, None, :, None].astype(a.dtype)
    return out.reshape(S5_SPLIT, (g // S5_SPLIT) * r, (g // S5_SPLIT) * c)


def _blockdiag_extract(a, r, c):
    n = S5_GROUPS // S5_SPLIT
    a = a.reshape(S5_SPLIT, n, r, n, c)
    d = jnp.stack([a[:, k, :, k, :] for k in range(n)], axis=1)
    return d.reshape(S5_GROUPS, r, c)


def s5_mixer_core_fwd(z, lam_re, lam_im, log_dt, b_re, b_im, c_re, c_im, d_skip, carried=None):
    bt_re = jnp.swapaxes(b_re, 1, 2)
    bt_im = jnp.swapaxes(b_im, 1, 2)
    logdt = log_dt.reshape(S5_GROUPS, 1)
    bb_re, bb_im = s5_param_fwd(lam_re, lam_im, logdt, bt_re, bt_im)
    flat = lambda a: a.reshape(1, S5_COLS)
    tab = s5_tables(flat(lam_re), flat(lam_im), flat(jnp.broadcast_to(logdt, (S5_GROUPS, S5_STATE))))
    bbd_re = _blockdiag(bb_re).astype(BF16)
    bbd_im = _blockdiag(bb_im).astype(BF16)
    ccd_re = _blockdiag(jnp.swapaxes(c_re, 1, 2)).astype(BF16)
    ccd_im = _blockdiag(jnp.swapaxes(c_im, 1, 2)).astype(BF16)
    dsk = d_skip.reshape(1, S5_WIDTH)
    y, ckpt, hrb, hib, *landed = s5_fwd(z, bbd_re, bbd_im, ccd_re, ccd_im, tab, dsk, carried=carried)
    saved = (logdt, bt_re, bt_im, bbd_re, bbd_im, ccd_re, ccd_im, tab, dsk, ckpt, hrb, hib)
    return y, saved, landed


def s5_b_from_dense(dense):
    return jnp.swapaxes(dense.reshape(S5_GROUPS, S5_GROUP, S5_STATE), 1, 2)


def s5_mixer_core_bwd(z, dy, dz, lam_re, lam_im, saved, carried=None):
    logdt, bt_re, bt_im, bbd_re, bbd_im, ccd_re, ccd_im, tab, dsk, ckpt, hrb, hib = saved
    dz, da, dbr, dbi, dcr, dci, dd, *landed = s5_bwd(z, dy, dz, ckpt, hrb, hib, bbd_re, bbd_im, ccd_re, ccd_im, tab,
                                                     dsk, carried=carried)
    d_ab_re = da[:, 0, :].reshape(S5_GROUPS, S5_STATE)
    d_ab_im = da[:, 1, :].reshape(S5_GROUPS, S5_STATE)
    d_bb_re = _blockdiag_extract(dbr, S5_GROUP, S5_STATE)
    d_bb_im = _blockdiag_extract(dbi, S5_GROUP, S5_STATE)
    g_lr, g_li, g_ld, g_btr, g_bti = s5_param_bwd(lam_re, lam_im, logdt, bt_re, bt_im,
                                                  d_ab_re, d_ab_im, d_bb_re, d_bb_im)
    g_cre = jnp.swapaxes(_blockdiag_extract(dcr, S5_STATE, S5_GROUP), 1, 2)
    g_cim = jnp.swapaxes(_blockdiag_extract(dci, S5_STATE, S5_GROUP), 1, 2)
    grads = dict(lambda_re=g_lr, lambda_im=g_li, log_dt=g_ld.reshape(S5_GROUPS), b_re=g_btr, b_im=g_bti,
                 c_re=g_cre, c_im=g_cim, d=dd.reshape(S5_WIDTH))
    return dz, grads, landed


Z_U, Z_GA, Z_VAL, Z_GLU, Z_GB = range(5)
SUBLANES = 8


def _shifted_copies(buf, tm):
    n = tm + CONV_HALO - SUBLANES
    for r in range(1, SUBLANES):
        buf[r, 0:n, :] = buf[0, pl.ds(r, n), :]


CONV_ROWS = 32


def _shifted_rows(buf, start, rows, base=0):
    return buf[start % SUBLANES, pl.ds(base + (start - start % SUBLANES), rows), :]


def conv_fwd(z, conv_w, conv_b, tm=ROW_TILE):
    L = z.shape[0]
    tm = min(tm, L)
    nt = L // tm
    hb = tm // CONV_HALO
    C = CONV_WIDTH

    def body(val_ref, glu_ref, valh_ref, gluh_ref, w_ref, b_ref, c_ref, vsh):
        live = (pl.program_id(0) > 0).astype(F32)
        vsh[0, 0:CONV_HALO, :] = valh_ref[...] * _sigmoid(gluh_ref[...]) * live
        vsh[0, CONV_HALO:, :] = val_ref[...] * _sigmoid(glu_ref[...])
        _shifted_copies(vsh, tm)

        def rows(i, carry):
            base = pl.multiple_of(i * CONV_ROWS, CONV_ROWS)
            acc = jnp.broadcast_to(b_ref[...], (CONV_ROWS, C))
            for k in range(CONV_KERNEL):
                acc = acc + w_ref[k:k + 1, :] * _shifted_rows(vsh, CONV_HALO - CONV_KERNEL + 1 + k, CONV_ROWS, base)
            c_ref[pl.ds(base, CONV_ROWS), :] = acc
            return carry

        lax.fori_loop(0, tm // CONV_ROWS, rows, 0)

    cur = lambda col: pl.BlockSpec((tm, C), lambda t: (t, col))
    prev = lambda col: pl.BlockSpec((CONV_HALO, C), lambda t: (jnp.maximum(t * hb - 1, 0), col))
    return pl.pallas_call(
        body, name="conv_fwd", grid=(nt,),
        in_specs=[cur(Z_VAL), cur(Z_GLU), prev(Z_VAL), prev(Z_GLU), _full(conv_w.shape), _full(conv_b.shape)],
        out_specs=pl.BlockSpec((tm, C), lambda t: (t, 0)),
        out_shape=jax.ShapeDtypeStruct((L, C), F32),
        scratch_shapes=[pltpu.VMEM((8, tm + CONV_HALO, C), F32)],
        compiler_params=_cp("parallel"),
    )(z, z, z, z, conv_w, conv_b)


def conv_bwd(z, dc, dz, conv_w, tm=ROW_TILE, carried=None):
    L = z.shape[0]
    tm = min(tm, L)
    nt = L // tm
    hb = tm // CONV_HALO
    nh = L // CONV_HALO
    C = CONV_WIDTH
    off = CONV_HALO - CONV_KERNEL + 1

    def body(*refs):
        top = end = None
        if carried is not None:
            refs, parts = carried.split(refs, 8, 3, 3)
            top, end = carried.hooks(parts, (nt,))
            top()
        val_ref, glu_ref, valh_ref, gluh_ref, dc_ref, dcn_ref, dz_ref, w_ref, dvg_ref, dw_ref, db_ref, vsh, dsh, wacc = refs
        t = pl.program_id(0)

        @pl.when(t == 0)
        def _():
            wacc[...] = jnp.zeros_like(wacc)
            db_ref[...] = jnp.zeros_like(db_ref)

        val = val_ref[...]
        sg = _sigmoid(glu_ref[...])
        vsh[0, 0:CONV_HALO, :] = valh_ref[...] * _sigmoid(gluh_ref[...]) * (t > 0).astype(F32)
        vsh[0, CONV_HALO:, :] = val * sg
        dcv = dc_ref[...]
        dsh[0, 0:tm, :] = dcv
        dsh[0, tm:, :] = dcn_ref[...] * (t < nt - 1).astype(F32)
        _shifted_copies(vsh, tm)
        _shifted_copies(dsh, tm)

        def rows(i, carry):
            base = pl.multiple_of(i * CONV_ROWS, CONV_ROWS)
            dcr = dc_ref[pl.ds(base, CONV_ROWS), :]
            dv = jnp.zeros((CONV_ROWS, C), F32)
            for k in range(CONV_KERNEL):
                dv = dv + w_ref[k:k + 1, :] * _shifted_rows(dsh, CONV_KERNEL - 1 - k, CONV_ROWS, base)
                prod = dcr * _shifted_rows(vsh, off + k, CONV_ROWS, base)
                wacc[k] += jnp.sum(prod.reshape(CONV_ROWS // SUBLANES, SUBLANES, C), axis=0)
            valr = val_ref[pl.ds(base, CONV_ROWS), :]
            sgr = _sigmoid(glu_ref[pl.ds(base, CONV_ROWS), :])
            dvg_ref[pl.ds(base, CONV_ROWS), 0:C] = (dv * sgr).astype(BF16)
            dvg_ref[pl.ds(base, CONV_ROWS), C:] = (dv * valr * sgr * (1.0 - sgr)).astype(BF16)
            return carry

        lax.fori_loop(0, tm // CONV_ROWS, rows, 0)
        db_ref[...] += jnp.sum(dcv, axis=0, keepdims=True)

        @pl.when(t == nt - 1)
        def _():
            dw_ref[...] = jnp.sum(wacc[...], axis=1)

        if end is not None:
            end()

    extra = carried.arrays if carried is not None else []
    extra_out = carried.out_shapes if carried is not None else []
    extra_sems = carried.sems if carried is not None else []
    cur = lambda col: pl.BlockSpec((tm, C), lambda t: (t, col))
    prev = lambda col: pl.BlockSpec((CONV_HALO, C), lambda t: (jnp.maximum(t * hb - 1, 0), col))
    nxt = pl.BlockSpec((CONV_HALO, C), lambda t: (jnp.minimum((t + 1) * hb, nh - 1), 0))
    row = pl.BlockSpec((tm, C), lambda t: (t, 0))
    return pl.pallas_call(
        body, name="conv_bwd", grid=(nt,),
        in_specs=[cur(Z_VAL), cur(Z_GLU), prev(Z_VAL), prev(Z_GLU), row, nxt, ANY, _full(conv_w.shape)]
        + [ANY] * len(extra),
        out_specs=[pl.BlockSpec((tm, 2 * C), lambda t: (t, 1)), _full((CONV_HALO, C)), _full((1, C))]
        + [ANY] * len(extra_out),
        out_shape=[jax.ShapeDtypeStruct(dz.shape, dz.dtype),
                   jax.ShapeDtypeStruct((CONV_HALO, C), F32), jax.ShapeDtypeStruct((1, C), F32)] + extra_out,
        scratch_shapes=[pltpu.VMEM((8, tm + CONV_HALO, C), F32), pltpu.VMEM((8, tm + CONV_HALO, C), F32),
                        pltpu.VMEM((CONV_HALO, SUBLANES, C), F32)] + extra_sems,
        input_output_aliases={6: 0},
        compiler_params=_cp("arbitrary"),
    )(z, z, z, z, dc, dc, dz, conv_w, *extra)


def _ln_parts(c):
    mu = jnp.mean(c, axis=-1, keepdims=True)
    cc = c - mu
    rstd = lax.rsqrt(jnp.mean(cc * cc, axis=-1, keepdims=True) + EPS)
    return rstd, cc * rstd


def _ev_tail_branches(ys, c, wglu, bglu, lng, lnb):
    z1 = _gelu(ys)
    z1b = z1.astype(BF16)
    sg = _sigmoid(_dot_rows(z1b, wglu) + bglu)
    out = z1 * sg
    rstd, chat = _ln_parts(c)
    cn = chat * lng + lnb
    return z1, z1b, sg, out, rstd, chat, cn


def ev_tail_fwd(ys, z, c, x0, wglu, bglu, lng, lnb, wout, tm=ROW_TILE):
    L, D = x0.shape
    tm = min(tm, L)
    W = S5_WIDTH

    def body(ys_ref, ga_ref, c_ref, gb_ref, x_ref, wglu_ref, bglu_ref, lng_ref, lnb_ref, wout_ref, o_ref):
        _, _, _, out, _, _, cn = _ev_tail_branches(ys_ref[...], c_ref[...], wglu_ref, bglu_ref[...],
                                                   lng_ref[...], lnb_ref[...])
        ya = (out * _silu(ga_ref[...])).astype(BF16)
        yb = (_silu(cn) * _silu(gb_ref[...])).astype(BF16)
        o_ref[...] = x_ref[...] + _dot_rows(jnp.concatenate([ya, yb], axis=1), wout_ref)

    row = lambda n, col=0: pl.BlockSpec((tm, n), lambda t: (t, col))
    return pl.pallas_call(
        body, name="ev_tail_fwd", grid=(L // tm,),
        in_specs=[row(W), row(W, Z_GA), row(W), row(W, Z_GB), row(D), _full(wglu.shape), _full(bglu.shape),
                  _full(lng.shape), _full(lnb.shape), _full(wout.shape)],
        out_specs=row(D), out_shape=jax.ShapeDtypeStruct((L, D), F32), compiler_params=_cp("parallel"),
    )(ys, z, c, z, x0, wglu, bglu, lng, lnb, wout)


def ev_tail_bwd(ys, z, c, dx1, wglu, bglu, lng, lnb, wout, tm=ROW_TILE):
    L, D = dx1.shape
    tm = min(tm, L)
    W = S5_WIDTH

    def body(ys_ref, ga_ref, c_ref, gb_ref, dx_ref, wglu_ref, bglu_ref, lng_ref, lnb_ref, wout_ref,
             dys_ref, dc_ref, dz_ref, r_ref, z1_ref, dt_ref, dbg_ref, dlg_ref, dlb_ref):
        @pl.when(pl.program_id(0) == 0)
        def _():
            for r in (dbg_ref, dlg_ref, dlb_ref):
                r[...] = jnp.zeros_like(r)

        ys, ga, gb = ys_ref[...], ga_ref[...], gb_ref[...]
        z1, z1b, sg, out, rstd, chat, cn = _ev_tail_branches(ys, c_ref[...], wglu_ref, bglu_ref[...],
                                                             lng_ref[...], lnb_ref[...])
        (sga, dsga), (sgb, dsgb), (scn, dscn) = _silu_pair(ga), _silu_pair(gb), _silu_pair(cn)
        r_ref[:, 0:W] = (out * sga).astype(BF16)
        r_ref[:, W:] = (scn * sgb).astype(BF16)
        dr = _dot_nt_rows(dx_ref[...].astype(BF16), wout_ref)
        dra, drb = dr[:, 0:W], dr[:, W:]
        dz_ref[...] = jnp.zeros_like(dz_ref)
        dz_ref[:, Z_GA * W:(Z_GA + 1) * W] = (dra * out * dsga).astype(BF16)
        dout = dra * sga
        dt = dout * z1 * sg * (1.0 - sg)
        dtb = dt.astype(BF16)
        dz1 = dout * sg + _dot_nt_rows(dtb, wglu_ref)
        dys_ref[...] = dz1 * _dgelu(ys)
        z1_ref[...] = z1b
        dt_ref[...] = dtb
        dbg_ref[...] += jnp.sum(dt, axis=0, keepdims=True)
        dz_ref[:, Z_GB * W:(Z_GB + 1) * W] = (drb * scn * dsgb).astype(BF16)
        dcn = drb * sgb * dscn
        dlg_ref[...] += jnp.sum(dcn * chat, axis=0, keepdims=True)
        dlb_ref[...] += jnp.sum(dcn, axis=0, keepdims=True)
        dch = dcn * lng_ref[...]
        dc_ref[...] = rstd * (dch - jnp.mean(dch, axis=-1, keepdims=True)
                              - chat * jnp.mean(dch * chat, axis=-1, keepdims=True))

    row = lambda n, col=0: pl.BlockSpec((tm, n), lambda t: (t, col))
    f = lambda n, dt: jax.ShapeDtypeStruct((L, n), dt)
    vec = jax.ShapeDtypeStruct((1, W), F32)
    return pl.pallas_call(
        body, name="ev_tail_bwd", grid=(L // tm,),
        in_specs=[row(W), row(W, Z_GA), row(W), row(W, Z_GB), row(D), _full(wglu.shape), _full(bglu.shape),
                  _full(lng.shape), _full(lnb.shape), _full(wout.shape)],
        out_specs=[row(W), row(W), row(EVEN_IN), row(D), row(W), row(W), _full((1, W)), _full((1, W)), _full((1, W))],
        out_shape=[f(W, F32), f(W, F32), f(EVEN_IN, BF16), f(D, BF16), f(W, BF16), f(W, BF16), vec, vec, vec],
        compiler_params=_cp("arbitrary"),
    )(ys, z, c, z, dx1, wglu, bglu, lng, lnb, wout)


XA_SCALE = XA_HEAD_DIM ** -0.5


def _xa_forward(xv, g, wqg, kv):
    D = D_MODEL
    _, xhat = _rms_parts(xv)
    hb = (xhat * g).astype(BF16)
    qb = (_dot_cols(hb, wqg, (0, 1)) * XA_SCALE).astype(BF16)
    gate = _dot_cols(hb, wqg, (2, 3))
    ps, os_ = [], []
    for h in range(XA_HEADS):
        lo, hi = h * XA_HEAD_DIM, (h + 1) * XA_HEAD_DIM
        s = _dot_nt(qb[:, lo:hi], kv[:, lo:hi])
        e = jnp.exp(s - jnp.max(s, axis=-1, keepdims=True))
        inv = 1.0 / jnp.sum(e, axis=-1, keepdims=True)
        ps.append((e, inv))
        os_.append(_dot(e.astype(BF16), kv[:, D + lo:D + hi]) * inv)
    return hb, qb, gate, ps, jnp.concatenate(os_, axis=1)


def xa_fwd(x, g, wqg, kv, wo, layer, name, tm=MM_TILE):
    L, D = x.shape
    tm = min(tm, L)

    def body(x_ref, g_ref, wqg_ref, kv_ref, wo_ref, o_ref):
        xv = x_ref[...]
        _, _, gate, _, o = _xa_forward(xv, g_ref[...], wqg_ref, kv_ref[...])
        o_ref[...] = xv + _dot_rows((o * _silu(gate)).astype(BF16), wo_ref)

    row = pl.BlockSpec((tm, D), lambda t: (t, 0))
    return pl.pallas_call(
        body, name=name, grid=(L // tm,),
        in_specs=[row, _full(g.shape), _wspec(wqg, layer), _full(kv.shape), _wspec(wo, layer)],
        out_specs=row, out_shape=jax.ShapeDtypeStruct((L, D), F32), compiler_params=_cp("parallel"),
    )(x, g, wqg, kv, wo)


def _loss_head(xv, gv, tv):
    D = xv.shape[-1]
    _, xhat = _rms_parts(xv)
    err = xhat * gv - tv
    loss = 0.5 * jnp.sum(jnp.sum(err * err, axis=-1, keepdims=True), axis=0, keepdims=True) / D
    dx, dg = _rms_bwd(xv, gv, err * (1.0 / D))
    return loss, dx, dg


def xa_fwd_loss(x, g, wqg, kv, wo, layer, target, gf, name, tm=MM_TILE):
    L, D = x.shape
    tm = min(tm, L)

    def body(x_ref, g_ref, wqg_ref, kv_ref, wo_ref, t_ref, gf_ref, loss_ref, dx_ref, dg_ref):
        @pl.when(pl.program_id(0) == 0)
        def _():
            loss_ref[...] = jnp.zeros_like(loss_ref)
            dg_ref[...] = jnp.zeros_like(dg_ref)

        xv = x_ref[...]
        _, _, gate, _, o = _xa_forward(xv, g_ref[...], wqg_ref, kv_ref[...])
        y = xv + _dot_rows((o * _silu(gate)).astype(BF16), wo_ref)
        loss, dx, dg = _loss_head(y, gf_ref[...], t_ref[...])
        loss_ref[...] += loss
        dx_ref[...] = dx
        dg_ref[...] += dg

    row = pl.BlockSpec((tm, D), lambda t: (t, 0))
    return pl.pallas_call(
        body, name=name, grid=(L // tm,),
        in_specs=[row, _full(g.shape), _wspec(wqg, layer), _full(kv.shape), _wspec(wo, layer), row, _full(gf.shape)],
        out_specs=[_full((1, 128)), row, _full((1, D))],
        out_shape=[jax.ShapeDtypeStruct((1, 128), F32), jax.ShapeDtypeStruct((L, D), F32),
                   jax.ShapeDtypeStruct((1, D), F32)],
        compiler_params=_cp("arbitrary"),
    )(x, g, wqg, kv, wo, target, gf)


def xa_bwd(x, dxo, g, wqg, kv, wo, layer, name, tm=MM_TILE):
    L, D = x.shape
    tm = min(tm, L)

    def body(x_ref, dxo_ref, g_ref, wqg_ref, kv_ref, wo_ref, dx_ref, dqg_ref, h_ref, r_ref, dkv_ref, dg_ref):
        @pl.when(pl.program_id(0) == 0)
        def _():
            dkv_ref[...] = jnp.zeros_like(dkv_ref)
            dg_ref[...] = jnp.zeros_like(dg_ref)

        xv = x_ref[...]
        kv = kv_ref[...]
        hb, qb, gate, ps, o = _xa_forward(xv, g_ref[...], wqg_ref, kv)
        sgate, dsgate = _silu_pair(gate)
        h_ref[...] = hb
        r_ref[...] = (o * sgate).astype(BF16)
        dxo = dxo_ref[...]
        dr = _dot_nt_rows(dxo.astype(BF16), wo_ref)
        do = dr * sgate
        dqg_ref[:, D:] = (dr * o * dsgate).astype(BF16)
        dob = do.astype(BF16)
        doo = do * o
        for h in range(XA_HEADS):
            lo, hi = h * XA_HEAD_DIM, (h + 1) * XA_HEAD_DIM
            e, inv = ps[h]
            dp = _dot_nt(dob[:, lo:hi], kv[:, D + lo:D + hi])
            dkv_ref[:, D + lo:D + hi] += _dot_tn(e.astype(BF16), (do[:, lo:hi] * inv).astype(BF16))
            rs = jnp.sum(doo[:, lo:hi], axis=-1, keepdims=True)
            dsb = (e * ((dp - rs) * inv)).astype(BF16)
            dqg_ref[:, lo:hi] = (_dot(dsb, kv[:, lo:hi]) * XA_SCALE).astype(BF16)
            dkv_ref[:, lo:hi] += _dot_tn(dsb, qb[:, lo:hi])
        dh = _dot_nt_cols(_col_pieces(dqg_ref[...], D // 2), wqg_ref)
        dx, dg = _rms_bwd(xv, g_ref[...], dh)
        dx_ref[...] = dxo + dx
        dg_ref[...] += dg

    row = lambda n: pl.BlockSpec((tm, n), lambda t: (t, 0))
    return pl.pallas_call(
        body, name=name, grid=(L // tm,),
        in_specs=[row(D), row(D), _full(g.shape), _wspec(wqg, layer), _full(kv.shape), _wspec(wo, layer)],
        out_specs=[row(D), row(2 * D), row(D), row(D), _full(kv.shape), _full((1, D))],
        out_shape=[jax.ShapeDtypeStruct((L, D), F32), jax.ShapeDtypeStruct((L, 2 * D), BF16),
                   jax.ShapeDtypeStruct((L, D), BF16), jax.ShapeDtypeStruct((L, D), BF16),
                   jax.ShapeDtypeStruct(kv.shape, F32), jax.ShapeDtypeStruct((1, D), F32)],
        compiler_params=_cp("arbitrary"),
    )(x, dxo, g, wqg, kv, wo)


ATT_SCALE = ATT_HEAD_DIM ** -0.5
ATT_PAIRS = ATT_HEADS // 2
SKEW_LANES = 1024
REL_LANES = 384


def _skew(x, left):
    amt = (ATT_QB - 1) - lax.broadcasted_iota(jnp.int32, (ATT_QB, 1), 0)
    for bit in range(8):
        sh = (SKEW_LANES - (1 << bit)) if left else (1 << bit)
        x = jnp.where(((amt >> bit) & 1) == 1, pltpu.roll(x, sh, 1), x)
    return x


def _dist_onehot(shape, dist_axis):
    j = lax.broadcasted_iota(jnp.int32, shape, dist_axis)
    r = lax.broadcasted_iota(jnp.int32, shape, 1 - dist_axis)
    return (jnp.clip((ATT_WIN - 1) - j, -MAX_REL, MAX_REL) + MAX_REL == r).astype(BF16)


def _dot_exact(v, onehot):
    acc = jnp.zeros((v.shape[0], onehot.shape[1]), F32)
    rem = v
    for _ in range(3):
        part = rem.astype(BF16)
        acc = acc + _dot(part, onehot)
        rem = rem - part.astype(F32)
    return acc


ATT_EDGE = ATT_PAD // ATT_QB


def att_bias(rel_bias, carried=None):
    H = rel_bias.shape[0]
    rb = jnp.pad(rel_bias, ((0, 0), (0, REL_LANES - rel_bias.shape[1]))).reshape(H, 1, REL_LANES)

    def body(*refs):
        top = end = None
        if carried is not None:
            refs, parts = carried.split(refs, 1, 1, 0)
            top, end = carried.hooks(parts, (H,))
            top()
        rb_ref, o_ref = refs
        by_col = _dot_exact(jnp.broadcast_to(rb_ref[...], (8, REL_LANES)), _dist_onehot((REL_LANES, SKEW_LANES), 1))
        x = _skew(jnp.broadcast_to(by_col[0:1, :], (ATT_QB, SKEW_LANES)), left=True)[:, 0:ATT_WIN]
        qc = lax.broadcasted_iota(jnp.int32, (ATT_QB, 1), 0) // CHUNK + LEFT_CHUNKS
        col = lax.broadcasted_iota(jnp.int32, (1, ATT_WIN), 1)
        dc = qc - col // CHUNK
        band = (dc >= 0) & (dc <= LEFT_CHUNKS)
        for blk in range(ATT_EDGE + 1):
            o_ref[blk] = jnp.where(band & (col >= ATT_PAD - blk * ATT_QB), x, NEG)
        if end is not None:
            end()

    extra = carried.arrays if carried is not None else []
    extra_out = carried.out_shapes if carried is not None else []
    extra_sems = carried.sems if carried is not None else []
    return pl.pallas_call(
        body, name="att_bias", grid=(H,),
        in_specs=[pl.BlockSpec((None, 1, REL_LANES), lambda h: (h, 0, 0))] + [ANY] * len(extra),
        out_specs=[pl.BlockSpec((ATT_EDGE + 1, None, ATT_QB, ATT_WIN), lambda h: (0, h, 0, 0))] + [ANY] * len(extra_out),
        out_shape=[jax.ShapeDtypeStruct((ATT_EDGE + 1, H, ATT_QB, ATT_WIN), F32)] + extra_out,
        scratch_shapes=extra_sems,
        compiler_params=_cp("arbitrary" if carried is not None else "parallel"),
    )(rb, *extra)


def relbias_bwd(dbias):
    H = dbias.shape[0]

    def body(x_ref, o_ref):
        x = jnp.concatenate([x_ref[...], jnp.zeros((ATT_QB, SKEW_LANES - ATT_WIN), F32)], axis=1)
        col = jnp.sum(_skew(x, left=False), axis=0, keepdims=True)
        o_ref[...] = _dot_exact(jnp.broadcast_to(col, (8, SKEW_LANES)), _dist_onehot((SKEW_LANES, REL_LANES), 0))

    out = pl.pallas_call(
        body, name="relbias_bwd", grid=(H,),
        in_specs=[pl.BlockSpec((None, ATT_QB, ATT_WIN), lambda h: (h, 0, 0))],
        out_specs=pl.BlockSpec((None, 8, REL_LANES), lambda h: (h, 0, 0)),
        out_shape=jax.ShapeDtypeStruct((H, 8, REL_LANES), F32), compiler_params=_cp("parallel"),
    )(dbias)
    return out[:, 0, :2 * MAX_REL + 1]


def _ca_scores(qh, kw, bias):
    s = _dot_nt(qh, kw) + bias
    e = jnp.exp(s - jnp.max(s, axis=-1, keepdims=True))
    return e, 1.0 / jnp.sum(e, axis=-1, keepdims=True)


def _ca_head(qv, m):
    return jnp.where(m, qv, jnp.zeros_like(qv)) * ATT_SCALE


def _ca_bias_spec():
    return pl.BlockSpec((None, 2, ATT_QB, ATT_WIN), lambda hp, b: (jnp.minimum(b, ATT_EDGE), hp, 0, 0))


def ca_fwd(q, kvp, gate, bias):
    L, D = q.shape
    Lp = kvp.shape[0]
    nb = L // ATT_QB

    PP = 2
    W = PP * 128

    def body(q_ref, k_ref, v_ref, g_ref, b_ref, r_ref, o_ref):
        w = pl.multiple_of(pl.program_id(1) * ATT_QB, ATT_QB)
        first = lax.broadcasted_iota(jnp.int32, (1, 128), 1) < ATT_HEAD_DIM
        for pp in range(PP):
            sl = slice(pp * 128, (pp + 1) * 128)
            kw = k_ref[pl.ds(w, ATT_WIN), sl]
            vw = v_ref[pl.ds(w, ATT_WIN), sl]
            qv = q_ref[:, sl]
            outs = []
            for hh, m in enumerate((first, jnp.logical_not(first))):
                e, inv = _ca_scores(_ca_head(qv, m), kw, b_ref[2 * pp + hh])
                outs.append(_dot(e.astype(BF16), vw) * inv)
            o = jnp.where(first, outs[0], outs[1])
            r_ref[:, sl] = (o * _silu(g_ref[:, sl])).astype(BF16)
            o_ref[:, sl] = o.astype(BF16)

    blk = pl.BlockSpec((ATT_QB, W), lambda hp, b: (b, hp))
    bias_blk = pl.BlockSpec((None, 2 * PP, ATT_QB, ATT_WIN), lambda hp, b: (jnp.minimum(b, ATT_EDGE), hp, 0, 0))
    return pl.pallas_call(
        body, name="ca_fwd", grid=(ATT_PAIRS // PP, nb),
        in_specs=[blk, pl.BlockSpec((Lp, W), lambda hp, b: (0, hp)),
                  pl.BlockSpec((Lp, W), lambda hp, b: (0, ATT_PAIRS // PP + hp)), blk, bias_blk],
        out_specs=[blk, blk], out_shape=[jax.ShapeDtypeStruct((L, D), BF16), jax.ShapeDtypeStruct((L, D), BF16)],
        compiler_params=_cp("parallel", "arbitrary"),
    )(q, kvp, kvp, gate, bias)


def ca_bwd(q, kvp, gate, bias, dr, o):
    L, D = q.shape
    Lp = kvp.shape[0]
    nb = L // ATT_QB

    def body(q_ref, k_ref, v_ref, g_ref, b_ref, dr_ref, o_ref, dq_ref, dg_ref, dkb_ref, dvb_ref, db_ref,
             dk_ref, dv_ref):
        b = pl.program_id(1)

        @pl.when(b == 0)
        def _():
            for r in (dk_ref, dv_ref, db_ref):
                r[...] = jnp.zeros_like(r)

        w = pl.multiple_of(b * ATT_QB, ATT_QB)
        kw = k_ref[pl.ds(w, ATT_WIN), :]
        vw = v_ref[pl.ds(w, ATT_WIN), :]
        qv = q_ref[...]
        gate_v = g_ref[...]
        drv = dr_ref[...]
        o = o_ref[...].astype(F32)
        sgate, dsgate = _silu_pair(gate_v)
        do = drv * sgate
        doo = do * o
        first = lax.broadcasted_iota(jnp.int32, (1, 128), 1) < ATT_HEAD_DIM
        dqs = []
        dkw = jnp.zeros((ATT_WIN, 128), F32)
        dvw = jnp.zeros((ATT_WIN, 128), F32)
        for hh, m in enumerate((first, jnp.logical_not(first))):
            qh = _ca_head(qv, m)
            e, inv = _ca_scores(qh, kw, b_ref[hh])
            eb = e.astype(BF16)
            doh = jnp.where(m, do, 0.0)
            dp = _dot_nt(doh.astype(BF16), vw)
            dvw = dvw + _dot_tn(eb, (doh * inv).astype(BF16))
            rs = jnp.sum(jnp.where(m, doo, 0.0), axis=-1, keepdims=True)
            ds = e * ((dp - rs) * inv)
            db_ref[hh] += ds
            dsb = ds.astype(BF16)
            dqs.append(_dot(dsb, kw))
            dkw = dkw + _dot_tn(dsb, qh)
        dg_ref[...] = (drv * o * dsgate).astype(BF16)
        dq_ref[...] = (jnp.where(first, dqs[0], dqs[1]) * ATT_SCALE).astype(BF16)
        dk_ref[pl.ds(w, ATT_WIN), :] += dkw
        dv_ref[pl.ds(w, ATT_WIN), :] += dvw

        @pl.when(b == nb - 1)
        def _():
            dkb_ref[...] = dk_ref[...].astype(BF16)
            dvb_ref[...] = dv_ref[...].astype(BF16)

    blk = pl.BlockSpec((ATT_QB, 128), lambda hp, b: (b, hp))
    kblk = pl.BlockSpec((Lp, 128), lambda hp, b: (0, hp))
    vblk = pl.BlockSpec((Lp, 128), lambda hp, b: (0, ATT_PAIRS + hp))
    bblk = pl.BlockSpec((2, ATT_QB, ATT_WIN), lambda hp, b: (hp, 0, 0))
    return pl.pallas_call(
        body, name="ca_bwd", grid=(ATT_PAIRS, nb),
        in_specs=[blk, kblk, vblk, blk, _ca_bias_spec(), blk, blk],
        out_specs=[blk, blk, kblk, kblk, bblk],
        out_shape=[jax.ShapeDtypeStruct((L, D), BF16), jax.ShapeDtypeStruct((L, D), BF16),
                   jax.ShapeDtypeStruct((Lp, D), BF16), jax.ShapeDtypeStruct((Lp, D), BF16),
                   jax.ShapeDtypeStruct(bias.shape[1:], F32)],
        scratch_shapes=[pltpu.VMEM((Lp, 128), F32), pltpu.VMEM((Lp, 128), F32)],
        compiler_params=_cp("parallel", "arbitrary"),
    )(q, kvp, kvp, gate, bias, dr, o)


_ADAM_C1 = 1.0 / (1.0 - ADAM_B1 ** ADAM_STEP)
_ADAM_C2 = 1.0 / (1.0 - ADAM_B2 ** ADAM_STEP)


def _adam_update(w, g, m, v):
    mn = ADAM_B1 * m + (1.0 - ADAM_B1) * g
    vn = ADAM_B2 * v + (1.0 - ADAM_B2) * g * g
    delta = -ADAM_LR * ((mn * _ADAM_C1) / (jnp.sqrt(vn * _ADAM_C2) + ADAM_EPS) + ADAM_WD * w)
    return delta, mn, vn


def adamw(w, g, m, v, name, tr=512):
    R, C = w.shape
    tr = min(tr, R)

    def body(w_ref, g_ref, m_ref, v_ref, d_ref, mo_ref, vo_ref):
        d_ref[...], mo_ref[...], vo_ref[...] = _adam_update(w_ref[...], g_ref[...], m_ref[...], v_ref[...])

    blk = pl.BlockSpec((tr, C), lambda i: (i, 0))
    sh = jax.ShapeDtypeStruct((R, C), F32)
    return pl.pallas_call(
        body, name=name, grid=(R // tr,), in_specs=[blk] * 4, out_specs=[blk] * 3,
        out_shape=[sh] * 3, compiler_params=_cp("parallel"),
    )(w, g, m, v)


def adamw_allreduce(gathered, w, m, v, shard, name, slot=None):
    R, C = w.shape
    sharded = slot is None and gathered.shape[2] != C

    def body(s_ref, ga_ref, w_ref, m_ref, v_ref, g_ref, d_ref, mo_ref, vo_ref):
        take = (lambda d: ga_ref[d]) if slot is None else (lambda d: ga_ref[d, slot:slot + R, 0:C])
        g = take(0)
        for d in range(1, N_DEV):
            g = g + take(d)
        g_ref[...] = g
        d_ref[...], mo_ref[...], vo_ref[...] = _adam_update(w_ref[...], g, m_ref[...], v_ref[...])

    blk = pl.BlockSpec((R, C), lambda i, s_ref: (0, 0))
    if slot is not None:
        gblk = pl.BlockSpec(gathered.shape, lambda i, s_ref: (0, 0, 0))
    else:
        gblk = pl.BlockSpec((N_DEV, R, C),
                            (lambda i, s_ref: (0, 0, s_ref[0])) if sharded else (lambda i, s_ref: (0, 0, 0)))
    sh = jax.ShapeDtypeStruct((R, C), F32)
    return pl.pallas_call(
        body, name=name,
        grid_spec=pltpu.PrefetchScalarGridSpec(num_scalar_prefetch=1, grid=(1,), in_specs=[gblk, blk, blk, blk],
                                               out_specs=[blk] * 4),
        out_shape=[sh] * 4, compiler_params=_cp("arbitrary"),
    )(shard, gathered, w, m, v)


LATE = ("ev_s5_glu_w", "ev_w_out", "od_w_in", "od_w_out", "xa_w_qg", "xa_w_kv", "xa_w_o")
EARLY_GRADS = ("od_w_in", "od_w_out", "xa_w_qg", "xa_w_kv", "xa_w_o", "ev_w_out", "ev_s5_glu_w")


def _reduce_to_chip(gs, names, core, tag):
    from_sibling = sibling_send_other_half(gs, "sibling_send_" + tag)
    return [sum_with_sibling(gi, ri, core, "sum_sibling_" + n) for n, gi, ri in zip(names, gs, from_sibling)]


def local_step(x, mem, target, p, gw, late, bias, place, core):
    row = lambda a: a.reshape(1, -1)
    D = D_MODEL
    L = x.shape[0]
    g, big = {}, {}
    gw = dict(gw)

    z, h0b = norm_mm(x, p["ev_norm_g"], gw["ev_w_in"], [((0, 1, 2, 3), F32, 0)], "ev_in")
    ys, s5_saved, landed = s5_mixer_core_fwd(
        z, p["ev_s5_lambda_re"][0], p["ev_s5_lambda_im"][0], p["ev_s5_log_dt"][0], p["ev_s5_b_re"][0],
        p["ev_s5_b_im"][0], p["ev_s5_c_re"][0], p["ev_s5_c_im"][0], p["ev_s5_d"][0],
        carried=carried_allgather([late[n] for n in LATE]))
    for n, gth in zip(LATE, landed):
        rows = gth.shape[1]
        gw[n] = gth.reshape(N_CHIPS, 2, rows // 2, gth.shape[2]) if n.startswith("xa_") else gth
    memn_b = rms_fwd(mem, row(p["mem_norm_g"]), "mem_norm")
    kvs = [mm_cols(memn_b, gw["xa_w_kv"], l, f"xa_kv{l}", BF16) for l in range(2)]
    conv_w = p["ev_conv_w"][0]
    c = conv_fwd(z, conv_w, p["ev_conv_b"])
    tail = (gw["ev_s5_glu_w"], p["ev_s5_glu_b"], p["ev_conv_ln_g"], p["ev_conv_ln_b"], gw["ev_w_out"])
    x1 = ev_tail_fwd(ys, z, c, x, *tail)
    xa0 = (row(p["xa_norm_g"][0]), gw["xa_w_qg"], kvs[0], gw["xa_w_o"], 0)
    x2 = xa_fwd(x1, *xa0, "xa_fwd0")

    q, kvp, gate, h1b = norm_mm(x2, p["od_norm_g"], gw["od_w_in"],
                                [((0,), BF16, 0), ((1, 2), BF16, ATT_PAD), ((3,), F32, 0)], "od_in")
    kvp = zero_rows(kvp, ATT_PAD, "od_kv_pad")
    r, att_o = ca_fwd(q, kvp, gate, bias)
    x3 = mm_res(r, gw["od_w_out"], x2, "od_out")
    xa1 = (row(p["xa_norm_g"][1]), gw["xa_w_qg"], kvs[1], gw["xa_w_o"], 1)
    loss, dx4, dgf = xa_fwd_loss(x3, *xa1, target, row(p["final_norm_g"]), "xa_fwd1_loss")
    g["final_norm_g"] = dgf.reshape(D)

    dx3, dqg1, hx1, rx1, dkv1, dgxa1 = xa_bwd(x3, dx4, *xa1, "xa_bwd1")
    dwqg = mm_tn(hx1, dqg1, "xa_dwqg1", ("cols", 1))
    dwo = mm_tn(rx1, dx4, "xa_dwo1", ("rows", 1))

    big["od_w_out"] = mm_tn(r, dx3, "od_dwout", ("rows",))
    dr = mm_nt_rows(dx3, gw["od_w_out"], "od_out_bwd")
    dq, dgate, dkp, dvp, dbias = ca_bwd(q, kvp, gate, bias, dr, att_o)
    pieces, offs = (dq, dkp, dvp, dgate), (0, ATT_PAD, ATT_PAD, 0)
    dwin = None
    for s in range(N_CHIPS):
        dwin = mm_tn(h1b, pieces[s], f"od_dwin{s}", ("slab", s), into=dwin, b_off=offs[s],
                     bl=ATT_PAD if offs[s] else 1024)
    big["od_w_in"] = dwin
    dx2, dgod = mm_nt_normbwd(pieces, offs, gw["od_w_in"], x2, p["od_norm_g"], dx3, "od_in_bwd")
    g["od_norm_g"] = dgod
    g["od_rel_bias"] = relbias_bwd(dbias)[None]

    dx1, dqg0, hx0, rx0, dkv0, dgxa0 = xa_bwd(x1, dx2, *xa0, "xa_bwd0")
    big["xa_w_qg"] = mm_tn(hx0, dqg0, "xa_dwqg0", ("cols", 0), into=dwqg)
    big["xa_w_o"] = mm_tn(rx0, dx2, "xa_dwo0", ("rows", 0), into=dwo)
    g["xa_norm_g"] = jnp.concatenate([dgxa0, dgxa1], axis=0)

    dys, dc, dz, ra, z1b, dtb, dbglu, dlng, dlnb = ev_tail_bwd(ys, z, c, dx1, *tail)
    big["ev_w_out"] = mm_tn(ra, dx1, "ev_dwout", ("rows",))
    big["ev_s5_glu_w"] = mm_tn(z1b, dtb, "ev_dwglu", ("rows",))
    g["ev_s5_glu_b"], g["ev_conv_ln_g"], g["ev_conv_ln_b"] = dbglu, dlng, dlnb
    dwkv = mm_tn(memn_b, dkv1, "xa_dwkv1", ("cols", 1), bl=MEM_LEN)
    big["xa_w_kv"] = mm_tn(memn_b, dkv0, "xa_dwkv0", ("cols", 0), into=dwkv, bl=MEM_LEN)
    dmem0 = mm_nt_cols(dkv0, gw["xa_w_kv"], 0, "xa_kv_bwd0")
    dmem1 = mm_nt_cols(dkv1, gw["xa_w_kv"], 1, "xa_kv_bwd1")
    g["mem_norm_g"] = rms_dgain(mem, dmem0, dmem1, "mem_norm_bwd").reshape(D)

    shard_major = lambda t: t.reshape((-1,) + t.shape[-2:])
    gs = [shard_major(big[n]) for n in EARLY_GRADS]
    dz, dconvw, dconvb, *from_sibling = conv_bwd(z, dc, dz, conv_w, carried=carried_sibling_send(gs))
    g["ev_conv_w"] = dconvw[None, :CONV_KERNEL]
    g["ev_conv_b"] = dconvb
    chip_sums = [sum_with_sibling(gi, ri, core, "sum_sibling_" + n) for n, gi, ri in zip(EARLY_GRADS, gs, from_sibling)]
    dz, s5g, from_chips = s5_mixer_core_bwd(z, dys, dz, p["ev_s5_lambda_re"][0], p["ev_s5_lambda_im"][0], s5_saved,
                                            carried=carried_chips_exchange(chip_sums))
    reduced = {n: sum_chips(ci, ri, place, "sum_chips_" + n) for n, ci, ri in zip(EARLY_GRADS, chip_sums, from_chips)}
    for n, v in s5g.items():
        g["ev_s5_" + n] = v[None]
    packed, slots = pack_rows([_as2d(g[n]) for n in PACKED_SMALL], "pack_small_grads")
    dwin_ev, *gathered = mm_tn(h0b, dz, "ev_dwin", ("cols",),
                               carried=carried_allgather_devices([packed] + [_as2d(g[n]) for n in SINGLE_SMALL]))
    grad_x, dgev = mm_nt_normbwd((dz,), (0,), gw["ev_w_in"], x, p["ev_norm_g"], dx1, "ev_in_bwd")
    chip_sum = _reduce_to_chip([dwin_ev], ["ev_w_in"], core, "last")
    reduced["ev_w_in"] = sum_chips(chip_sum[0], chips_exchange(chip_sum)[0], place, "sum_chips_ev_w_in")
    return loss, grad_x, g, reduced, dgev, gathered, slots


def _me():
    return lax.axis_index("x"), lax.axis_index("y"), lax.axis_index("c")


def _other_chips(x, y):
    return [(1 - x, y), (x, 1 - y), (1 - x, 1 - y)]


def _remote(src, dst, send_sems, recv_sems, k, to):
    return pltpu.make_async_remote_copy(src_ref=src, dst_ref=dst, send_sem=send_sems.at[k], recv_sem=recv_sems.at[k],
                                        device_id=to, device_id_type=MESH)


def _rows_half(ref, h):
    H = ref.shape[-2] // 2
    return ref.at[(slice(None),) * (len(ref.shape) - 2) + (pl.ds(h * H, H), slice(None))]


def allgather_devices(vs):
    n = len(vs)

    def body(*refs):
        ins, outs = refs[:n], refs[n:2 * n]
        send_sems, recv_sems, local_sems = refs[2 * n:]
        x, y, c = _me()
        sib = (x, y, 1 - c)
        chips = _other_chips(x, y)
        me = 4 * x + 2 * y + c
        local = [pltpu.make_async_copy(ins[i], outs[i].at[me], local_sems.at[i]) for i in range(n)]
        for cp in local:
            cp.start()
        first, passed = [], []
        for i in range(n):
            first.append(_remote(ins[i], outs[i].at[me], send_sems, recv_sems, 7 * i, sib))
            for j, (cx, cy) in enumerate(chips):
                first.append(_remote(ins[i], outs[i].at[me], send_sems, recv_sems, 7 * i + 1 + j, (cx, cy, c)))
        for cp in first:
            cp.start()
        for j, (cx, cy) in enumerate(chips):
            for i in range(n):
                got = outs[i].at[4 * cx + 2 * cy + c]
                _remote(got, got, send_sems, recv_sems, 7 * i + 1 + j, (cx, cy, c)).wait_recv()
                fw = _remote(got, got, send_sems, recv_sems, 7 * i + 4 + j, sib)
                fw.start()
                passed.append(fw)
        for i in range(n):
            got = outs[i].at[4 * x + 2 * y + (1 - c)]
            _remote(got, got, send_sems, recv_sems, 7 * i, sib).wait_recv()
            for j, (cx, cy) in enumerate(chips):
                got = outs[i].at[4 * cx + 2 * cy + (1 - c)]
                _remote(got, got, send_sems, recv_sems, 7 * i + 4 + j, sib).wait_recv()
        for cp in first + passed:
            cp.wait_send()
        for cp in local:
            cp.wait()

    return pl.pallas_call(
        body, name="allgather_devices", in_specs=[ANY] * n, out_specs=[ANY] * n,
        out_shape=[jax.ShapeDtypeStruct((N_DEV,) + v.shape, v.dtype) for v in vs],
        scratch_shapes=[pltpu.SemaphoreType.DMA((7 * n,)), pltpu.SemaphoreType.DMA((7 * n,)),
                        pltpu.SemaphoreType.DMA((n,))],
    )(*vs)


def sibling_send_other_half(gs, name):
    n = len(gs)

    def body(*refs):
        ins, outs = refs[:n], refs[n:2 * n]
        send_sems, recv_sems = refs[2 * n:]
        x, y, c = _me()
        cps = [_remote(_rows_half(ins[i], 1 - c), outs[i], send_sems, recv_sems, i, (x, y, 1 - c)) for i in range(n)]
        for cp in cps:
            cp.start()
        for cp in cps:
            cp.wait()

    return pl.pallas_call(
        body, name=name, in_specs=[ANY] * n, out_specs=[ANY] * n,
        out_shape=[jax.ShapeDtypeStruct((g.shape[0], g.shape[1] // 2, g.shape[2]), g.dtype) for g in gs],
        scratch_shapes=[pltpu.SemaphoreType.DMA((n,)), pltpu.SemaphoreType.DMA((n,))],
    )(*gs)


def chips_exchange(parts):
    n = len(parts)

    def body(*refs):
        ins, outs = refs[:n], refs[n:2 * n]
        send_sems, recv_sems = refs[2 * n:]
        x, y, c = _me()
        cps = []
        for i in range(n):
            nl = ins[i].shape[0] // N_CHIPS
            for j, (cx, cy) in enumerate(_other_chips(x, y)):
                cps.append(_remote(ins[i].at[pl.ds((2 * cx + cy) * nl, nl)], outs[i].at[j], send_sems, recv_sems,
                                   3 * i + j, (cx, cy, c)))
        for cp in cps:
            cp.start()
        for cp in cps:
            cp.wait()

    return pl.pallas_call(
        body, name="chips_exchange", in_specs=[ANY] * n, out_specs=[ANY] * n,
        out_shape=[jax.ShapeDtypeStruct((3, a.shape[0] // N_CHIPS) + a.shape[1:], a.dtype) for a in parts],
        scratch_shapes=[pltpu.SemaphoreType.DMA((3 * n,)), pltpu.SemaphoreType.DMA((3 * n,))],
    )(*parts)


def sibling_share(fulls):
    n = len(fulls)

    def body(*refs):
        outs = refs[n:2 * n]
        send_sems, recv_sems = refs[2 * n:]
        x, y, c = _me()
        cps = [_remote(_rows_half(outs[i], c), _rows_half(outs[i], c), send_sems, recv_sems, i, (x, y, 1 - c))
               for i in range(n)]
        for cp in cps:
            cp.start()
        for i in range(n):
            got = _rows_half(outs[i], 1 - c)
            _remote(got, got, send_sems, recv_sems, i, (x, y, 1 - c)).wait_recv()
        for cp in cps:
            cp.wait_send()

    return pl.pallas_call(
        body, name="sibling_share", in_specs=[ANY] * n, out_specs=[ANY] * n,
        out_shape=[jax.ShapeDtypeStruct(f.shape, f.dtype) for f in fulls],
        input_output_aliases={i: i for i in range(n)},
        scratch_shapes=[pltpu.SemaphoreType.DMA((n,)), pltpu.SemaphoreType.DMA((n,))],
    )(*fulls)


def sum_with_sibling(g, recv, core, name):
    S, H, C = recv.shape
    tr = min(512, H)

    def body(c_ref, g_ref, r_ref, o_ref):
        o_ref[...] = (g_ref[...].astype(F32) + r_ref[...].astype(F32)).astype(o_ref.dtype)

    nb = H // tr
    return pl.pallas_call(
        body, name=name,
        grid_spec=pltpu.PrefetchScalarGridSpec(
            num_scalar_prefetch=1, grid=(S, nb),
            in_specs=[pl.BlockSpec((None, tr, C), lambda s, i, c_ref: (s, c_ref[0] * nb + i, 0)),
                      pl.BlockSpec((None, tr, C), lambda s, i, c_ref: (s, i, 0))],
            out_specs=pl.BlockSpec((None, tr, C), lambda s, i, c_ref: (s, i, 0))),
        out_shape=jax.ShapeDtypeStruct((S, H, C), g.dtype), compiler_params=_cp("parallel", "parallel"),
    )(core, g, recv)


def sum_chips(a, recv, place, name):
    _, nl, H, C = recv.shape
    tr = min(512, H)
    nb = H // tr

    def body(p_ref, a_ref, r_ref, o_ref):
        acc = a_ref[...].astype(F32)
        for j in range(3):
            acc = acc + r_ref[j].astype(F32)
        o_ref[...] = acc

    return pl.pallas_call(
        body, name=name,
        grid_spec=pltpu.PrefetchScalarGridSpec(
            num_scalar_prefetch=1, grid=(nl, nb),
            in_specs=[pl.BlockSpec((None, tr, C), lambda l, i, p_ref: (p_ref[0] * nl + l, i, 0)),
                      pl.BlockSpec((3, None, tr, C), lambda l, i, p_ref: (0, l, i, 0))],
            out_specs=pl.BlockSpec((None, tr, C), lambda l, i, p_ref: (l, p_ref[1] * nb + i, 0))),
        out_shape=jax.ShapeDtypeStruct((nl, 2 * H, C), F32), compiler_params=_cp("parallel", "parallel"),
    )(place, a, recv)


def pack_rows(arrays, name):
    starts, r0 = [], 0
    for a in arrays:
        if a.shape[0] >= SUBLANES:
            r0 = -(-r0 // SUBLANES) * SUBLANES
        starts.append(r0)
        r0 += a.shape[0]
    r0 = -(-r0 // SUBLANES) * SUBLANES
    n = len(arrays)

    def body(*refs):
        o_ref = refs[n]
        o_ref[...] = jnp.zeros_like(o_ref)
        for a_ref, s in zip(refs[:n], starts):
            r, c = a_ref.shape
            o_ref[s:s + r, 0:c] = a_ref[...]

    out = pl.pallas_call(body, name=name, out_shape=jax.ShapeDtypeStruct((r0, PACK_COLS), F32))(*arrays)
    return out, starts


def sum_slot(gathered, slot, shape, name):
    r, c = shape

    def body(ga_ref, o_ref):
        acc = ga_ref[0, slot:slot + r, 0:c]
        for d in range(1, N_DEV):
            acc = acc + ga_ref[d, slot:slot + r, 0:c]
        o_ref[...] = acc

    return pl.pallas_call(body, name=name, out_shape=jax.ShapeDtypeStruct((r, c), F32))(gathered)


def carried_allgather(blocks):
    n = len(blocks)

    def first_hop(ins, outs, sems, i, j, chip, x, y, c):
        me = 2 * x + y
        return _remote(_rows_half(ins[i], c), _rows_half(outs[i].at[me], c), sems[0], sems[1], 6 * i + j, (*chip, c))

    def start(ins, outs, sems):
        x, y, c = _me()
        for i in range(n):
            pltpu.make_async_copy(ins[i], outs[i].at[2 * x + y], sems[2].at[i]).start()
        for i in range(n):
            for j, chip in enumerate(_other_chips(x, y)):
                first_hop(ins, outs, sems, i, j, chip, x, y, c).start()

    def finish(ins, outs, sems):
        x, y, c = _me()
        sib = (x, y, 1 - c)
        chips = _other_chips(x, y)
        passed = []
        for j, (cx, cy) in enumerate(chips):
            for i in range(n):
                got = _rows_half(outs[i].at[2 * cx + cy], c)
                _remote(got, got, sems[0], sems[1], 6 * i + j, (cx, cy, c)).wait_recv()
                fw = _remote(got, got, sems[0], sems[1], 6 * i + 3 + j, sib)
                fw.start()
                passed.append(fw)
        for j, (cx, cy) in enumerate(chips):
            for i in range(n):
                got = _rows_half(outs[i].at[2 * cx + cy], 1 - c)
                _remote(got, got, sems[0], sems[1], 6 * i + 3 + j, sib).wait_recv()
        for i in range(n):
            for j, chip in enumerate(chips):
                first_hop(ins, outs, sems, i, j, chip, x, y, c).wait_send()
        for fw in passed:
            fw.wait_send()
        for i in range(n):
            pltpu.make_async_copy(ins[i], outs[i].at[2 * x + y], sems[2].at[i]).wait()

    return Carried(blocks, [jax.ShapeDtypeStruct((N_CHIPS,) + b.shape, b.dtype) for b in blocks],
                   [pltpu.SemaphoreType.DMA((6 * n,)), pltpu.SemaphoreType.DMA((6 * n,)), pltpu.SemaphoreType.DMA((n,))],
                   start, finish)


def carried_allgather_devices(vs):
    n = len(vs)

    def first_copies(ins, outs, sems):
        x, y, c = _me()
        me = 4 * x + 2 * y + c
        cps = []
        for i in range(n):
            cps.append(_remote(ins[i], outs[i].at[me], sems[0], sems[1], 7 * i, (x, y, 1 - c)))
            for j, (cx, cy) in enumerate(_other_chips(x, y)):
                cps.append(_remote(ins[i], outs[i].at[me], sems[0], sems[1], 7 * i + 1 + j, (cx, cy, c)))
        return cps

    def local_copies(ins, outs, sems):
        x, y, c = _me()
        return [pltpu.make_async_copy(ins[i], outs[i].at[4 * x + 2 * y + c], sems[2].at[i]) for i in range(n)]

    def start(ins, outs, sems):
        for cp in local_copies(ins, outs, sems) + first_copies(ins, outs, sems):
            cp.start()

    def finish(ins, outs, sems):
        x, y, c = _me()
        sib = (x, y, 1 - c)
        chips = _other_chips(x, y)
        passed = []
        for j, (cx, cy) in enumerate(chips):
            for i in range(n):
                got = outs[i].at[4 * cx + 2 * cy + c]
                _remote(got, got, sems[0], sems[1], 7 * i + 1 + j, (cx, cy, c)).wait_recv()
                fw = _remote(got, got, sems[0], sems[1], 7 * i + 4 + j, sib)
                fw.start()
                passed.append(fw)
        for i in range(n):
            got = outs[i].at[4 * x + 2 * y + (1 - c)]
            _remote(got, got, sems[0], sems[1], 7 * i, sib).wait_recv()
            for j, (cx, cy) in enumerate(chips):
                got = outs[i].at[4 * cx + 2 * cy + (1 - c)]
                _remote(got, got, sems[0], sems[1], 7 * i + 4 + j, sib).wait_recv()
        for cp in first_copies(ins, outs, sems) + passed:
            cp.wait_send()
        for cp in local_copies(ins, outs, sems):
            cp.wait()

    return Carried(vs, [jax.ShapeDtypeStruct((N_DEV,) + v.shape, v.dtype) for v in vs],
                   [pltpu.SemaphoreType.DMA((7 * n,)), pltpu.SemaphoreType.DMA((7 * n,)), pltpu.SemaphoreType.DMA((n,))],
                   start, finish)


def carried_sibling_send(gs):
    n = len(gs)

    def copies(ins, outs, sems):
        x, y, c = _me()
        return [_remote(_rows_half(ins[i], 1 - c), outs[i], sems[0], sems[1], i, (x, y, 1 - c)) for i in range(n)]

    def start(ins, outs, sems):
        for cp in copies(ins, outs, sems):
            cp.start()

    def finish(ins, outs, sems):
        for cp in copies(ins, outs, sems):
            cp.wait()

    return Carried(gs, [jax.ShapeDtypeStruct((g.shape[0], g.shape[1] // 2, g.shape[2]), g.dtype) for g in gs],
                   [pltpu.SemaphoreType.DMA((n,)), pltpu.SemaphoreType.DMA((n,))], start, finish)


def carried_chips_exchange(parts):
    n = len(parts)

    def copies(ins, outs, sems):
        x, y, c = _me()
        cps = []
        for i in range(n):
            nl = ins[i].shape[0] // N_CHIPS
            for j, (cx, cy) in enumerate(_other_chips(x, y)):
                cps.append(_remote(ins[i].at[pl.ds((2 * cx + cy) * nl, nl)], outs[i].at[j], sems[0], sems[1],
                                   3 * i + j, (cx, cy, c)))
        return cps

    def start(ins, outs, sems):
        for cp in copies(ins, outs, sems):
            cp.start()

    def finish(ins, outs, sems):
        for cp in copies(ins, outs, sems):
            cp.wait()

    return Carried(parts, [jax.ShapeDtypeStruct((3, a.shape[0] // N_CHIPS) + a.shape[1:], a.dtype) for a in parts],
                   [pltpu.SemaphoreType.DMA((3 * n,)), pltpu.SemaphoreType.DMA((3 * n,))], start, finish)


BIG = ("ev_w_in", "ev_s5_glu_w", "ev_w_out", "od_w_in", "od_w_out", "xa_w_qg", "xa_w_kv", "xa_w_o")
SHARDED_F32 = (("ev_conv_w", 2), ("od_norm_g", 1))
SMALL = ("mem_norm_g", "ev_norm_g", "ev_s5_lambda_re", "ev_s5_lambda_im", "ev_s5_log_dt", "ev_s5_b_re", "ev_s5_b_im",
         "ev_s5_c_re", "ev_s5_c_im", "ev_s5_d", "ev_s5_glu_b", "ev_conv_b", "ev_conv_ln_g", "ev_conv_ln_b",
         "od_rel_bias", "xa_norm_g", "final_norm_g")
NARROW = ("ev_s5_c_re", "ev_s5_c_im")
DENSE_B = ("ev_s5_b_re", "ev_s5_b_im")
PACK_COLS = 1024
PACKED_SMALL = tuple(n for n in SMALL if n not in NARROW and n != "ev_norm_g")
SINGLE_SMALL = NARROW + tuple(n for n, _ in SHARDED_F32)
WEIGHTS = ("mem_norm_g", "ev_norm_g", "ev_w_in", "ev_s5_lambda_re", "ev_s5_lambda_im", "ev_s5_log_dt", "ev_s5_b_re",
           "ev_s5_b_im", "ev_s5_c_re", "ev_s5_c_im", "ev_s5_d", "ev_s5_glu_w", "ev_s5_glu_b", "ev_conv_w", "ev_conv_b",
           "ev_conv_ln_g", "ev_conv_ln_b", "ev_w_out", "od_norm_g", "od_w_in", "od_rel_bias", "od_w_out", "xa_norm_g",
           "xa_w_qg", "xa_w_kv", "xa_w_o", "final_norm_g")


def _as2d(a):
    return a.reshape(1, -1) if a.ndim == 1 else a.reshape(-1, a.shape[-1])


def kernel(x, mem, mem_norm_g, ev_norm_g, ev_w_in, ev_s5_lambda_re, ev_s5_lambda_im, ev_s5_log_dt, ev_s5_b_re, ev_s5_b_im, ev_s5_c_re, ev_s5_c_im, ev_s5_d, ev_s5_glu_w, ev_s5_glu_b, ev_conv_w, ev_conv_b, ev_conv_ln_g, ev_conv_ln_b, ev_w_out, od_norm_g, od_w_in, od_rel_bias, od_w_out, xa_norm_g, xa_w_qg, xa_w_kv, xa_w_o, final_norm_g, loss_target, m_mem_norm_g, m_ev_norm_g, m_ev_w_in, m_ev_s5_lambda_re, m_ev_s5_lambda_im, m_ev_s5_log_dt, m_ev_s5_b_re, m_ev_s5_b_im, m_ev_s5_c_re, m_ev_s5_c_im, m_ev_s5_d, m_ev_s5_glu_w, m_ev_s5_glu_b, m_ev_conv_w, m_ev_conv_b, m_ev_conv_ln_g, m_ev_conv_ln_b, m_ev_w_out, m_od_norm_g, m_od_w_in, m_od_rel_bias, m_od_w_out, m_xa_norm_g, m_xa_w_qg, m_xa_w_kv, m_xa_w_o, m_final_norm_g, v_mem_norm_g, v_ev_norm_g, v_ev_w_in, v_ev_s5_lambda_re, v_ev_s5_lambda_im, v_ev_s5_log_dt, v_ev_s5_b_re, v_ev_s5_b_im, v_ev_s5_c_re, v_ev_s5_c_im, v_ev_s5_d, v_ev_s5_glu_w, v_ev_s5_glu_b, v_ev_conv_w, v_ev_conv_b, v_ev_conv_ln_g, v_ev_conv_ln_b, v_ev_w_out, v_od_norm_g, v_od_w_in, v_od_rel_bias, v_od_w_out, v_xa_norm_g, v_xa_w_qg, v_xa_w_kv, v_xa_w_o, v_final_norm_g):
    a = dict(locals())
    w = {n: a[n] for n in WEIGHTS}
    shard = (2 * lax.axis_index("x") + lax.axis_index("y")).reshape(1).astype(jnp.int32)
    core = lax.axis_index("c").reshape(1).astype(jnp.int32)

    place = jnp.concatenate([shard, core])

    blocks = {n: w[n].astype(BF16).reshape(-1, w[n].shape[-1]) for n in BIG}
    conv_blk = jnp.pad(_as2d(w["ev_conv_w"]), ((0, 1), (0, 0)))
    odn_blk = w["od_norm_g"].reshape(2, -1)
    bias, evin_g, conv_g, odn_g = att_bias(w["od_rel_bias"][0],
                                           carried=carried_allgather([blocks["ev_w_in"], conv_blk, odn_blk]))
    gw = {"ev_w_in": evin_g}
    p = {n: w[n] for n in SMALL}
    p["ev_conv_w"] = jnp.concatenate([conv_g[s, :CONV_KERNEL] for s in range(N_CHIPS)], axis=1)[None]
    p["od_norm_g"] = odn_g.reshape(1, D_MODEL)

    loss, grad_x, g, reduced, dgev, gath, slots = local_step(x[0], mem[0], loss_target[0], p, gw,
                                                             {n: blocks[n] for n in LATE}, bias, place, core)
    loss = lax.psum(loss[0, 0], ("x", "y", "c"))
    g_big = dict(zip(BIG, sibling_share([reduced[n] for n in BIG])))

    out = {tag: {} for tag in ("grad", "delta", "m", "v")}
    for n in BIG:
        sh = w[n].shape
        to2d = lambda t: t.reshape(-1, sh[-1])
        gn = to2d(g_big[n])
        d, mn, vn = adamw(to2d(w[n]), gn, to2d(a["m_" + n]), to2d(a["v_" + n]), "adamw_" + n)
        for tag, val in zip(("grad", "delta", "m", "v"), (gn, d, mn, vn)):
            out[tag][n] = val.reshape(sh)

    jobs = [(n, gath[0], s) for n, s in zip(PACKED_SMALL, slots)]
    jobs += [(n, gt, None) for n, gt in zip(SINGLE_SMALL, gath[1:])]
    jobs += [("ev_norm_g", allgather_devices([dgev])[0], None)]
    for n, gt, slot in jobs:
        sh = w[n].shape
        w2, m2, v2 = _as2d(w[n]), _as2d(a["m_" + n]), _as2d(a["v_" + n])
        if n in DENSE_B:
            gn = _as2d(s5_b_from_dense(sum_slot(gt, slot, g[n].shape[-2:], "sum_" + n)))
            d, mn, vn = adamw(w2, gn, m2, v2, "adamw_" + n)
        else:
            gn, d, mn, vn = adamw_allreduce(gt, w2, m2, v2, shard, "adamw_" + n, slot=slot)
        for tag, val in zip(("grad", "delta", "m", "v"), (gn, d, mn, vn)):
            out[tag][n] = val.reshape(sh)

    res = [loss, grad_x[None]]
    for tag in ("grad", "delta", "m", "v"):
        res += [out[tag][n] for n in WEIGHTS]
    return tuple(res)
```

```python
import math

import jax
import jax.numpy as jnp
import numpy as np
from jax import lax
from jax.experimental import pallas as pl
from jax.experimental.pallas import tpu as pltpu

F32 = jnp.float32
BF16 = jnp.bfloat16

D_MODEL = 1024
CHUNK = 64
LEFT_CHUNKS = 8
S5_WIDTH = 512
S5_GROUP = 16
S5_GROUPS = 32
S5_STATE = 64
S5_COLS = S5_GROUPS * S5_STATE
S5_SPLIT = 4
S5_CC = S5_COLS // S5_SPLIT
S5_UC = S5_WIDTH // S5_SPLIT
CONV_WIDTH = 512
CONV_KERNEL = 31
CONV_HALO = 32
ATT_HEADS = 16
ATT_HEAD_DIM = 64
MAX_REL = 128
MEM_LEN = 256
XA_HEADS = 4
XA_HEAD_DIM = 256
EPS = 1e-6
EVEN_IN = 2560
ODD_IN = 4096

ADAM_LR = 0.001
ADAM_B1 = 0.9
ADAM_B2 = 0.999
ADAM_EPS = 1e-08
ADAM_WD = 0.01
ADAM_STEP = 10

ROW_TILE = 256
MM_TILE = 512
S5_TILE = 512
ATT_QB = 256
ATT_PAD = LEFT_CHUNKS * CHUNK
ATT_WIN = ATT_PAD + ATT_QB
VMEM_LIMIT_V7X = 56 * 1024 * 1024
NEG = -1e30
LANES = 128
N_CHIPS = 4
N_DEV = 8

MESH = pl.DeviceIdType.MESH
ANY = pl.BlockSpec(memory_space=pl.ANY)


def _cp(*sem, vmem=VMEM_LIMIT_V7X):
    return pltpu.CompilerParams(dimension_semantics=sem if sem else None, vmem_limit_bytes=vmem)


def _full(shape):
    n = len(shape)
    return pl.BlockSpec(shape, lambda *_: (0,) * n)


def _wspec(w, layer=None):
    if layer is None:
        return _full(w.shape)
    s, _, r, c = w.shape
    return pl.BlockSpec((s, None, r, c), lambda *_: (0, layer, 0, 0))


def _lane_tile(n, cap):
    return max(t for t in range(LANES, min(n, cap) + 1, LANES) if n % t == 0)


def _sigmoid(x):
    return 1.0 / (1.0 + jnp.exp(-x))


def _silu(x):
    return x * _sigmoid(x)


def _silu_pair(x):
    s = _sigmoid(x)
    return x * s, s * (1.0 + x * (1.0 - s))


_GELU_C = math.sqrt(2.0 / math.pi)


def _gelu(x):
    return 0.5 * x * (1.0 + jnp.tanh(_GELU_C * (x + 0.044715 * x * x * x)))


def _dgelu(x):
    t = jnp.tanh(_GELU_C * (x + 0.044715 * x * x * x))
    return 0.5 * (1.0 + t) + 0.5 * x * (1.0 - t * t) * _GELU_C * (1.0 + 3.0 * 0.044715 * x * x)


def _dot(a, b):
    return jnp.dot(a, b, preferred_element_type=F32)


def _dot_nt(a, b):
    return lax.dot_general(a, b, (((1,), (1,)), ((), ())), preferred_element_type=F32)


def _dot_tn(a, b):
    return lax.dot_general(a, b, (((0,), (0,)), ((), ())), preferred_element_type=F32)


def _dot_cols(a, w4, shards=range(N_CHIPS)):
    return jnp.concatenate([_dot(a, w4[s]) for s in shards], axis=1)


def _dot_rows(a, w4):
    r = w4.shape[1]
    acc = _dot(a[:, 0:r], w4[0])
    for s in range(1, N_CHIPS):
        acc = acc + _dot(a[:, s * r:(s + 1) * r], w4[s])
    return acc


def _dot_nt_cols(dys, w4):
    acc = _dot_nt(dys[0], w4[0])
    for s in range(1, N_CHIPS):
        acc = acc + _dot_nt(dys[s], w4[s])
    return acc


def _dot_nt_rows(dy, w4):
    return jnp.concatenate([_dot_nt(dy, w4[s]) for s in range(N_CHIPS)], axis=1)


def _col_pieces(v, n):
    return [v[:, s * n:(s + 1) * n] for s in range(N_CHIPS)]


def _rms_parts(xv):
    inv = lax.rsqrt(jnp.mean(xv * xv, axis=-1, keepdims=True) + EPS)
    return inv, xv * inv


def _rms_bwd(xv, g, dh):
    inv, xhat = _rms_parts(xv)
    dg = jnp.sum(dh * xhat, axis=0, keepdims=True)
    dxh = dh * g
    dx = inv * (dxh - xhat * jnp.mean(dxh * xhat, axis=-1, keepdims=True))
    return dx, dg


def norm_mm(x, g, w4, groups, name, tm=MM_TILE):
    M, D = x.shape
    n = w4.shape[2]
    tm = min(tm, M)

    def body(x_ref, g_ref, w_ref, *outs):
        _, xhat = _rms_parts(x_ref[...])
        hb = (xhat * g_ref[...]).astype(BF16)
        for o, (shards, dt, _) in zip(outs, groups):
            o[...] = _dot_cols(hb, w_ref, shards).astype(dt)
        outs[-1][...] = hb

    out_shape = [jax.ShapeDtypeStruct((M + pad, len(sh) * n), dt) for (sh, dt, pad) in groups]
    out_specs = [pl.BlockSpec((tm, len(sh) * n), lambda i, p=pad // tm: (i + p, 0)) for (sh, _, pad) in groups]
    out_shape.append(jax.ShapeDtypeStruct((M, D), BF16))
    out_specs.append(pl.BlockSpec((tm, D), lambda i: (i, 0)))
    return pl.pallas_call(
        body, name=name, grid=(M // tm,),
        in_specs=[pl.BlockSpec((tm, D), lambda i: (i, 0)), _full(g.shape), _full(w4.shape)],
        out_specs=out_specs, out_shape=out_shape, compiler_params=_cp("parallel"),
    )(x, g, w4)


def zero_rows(buf, rows, name, tm=ROW_TILE):
    C = buf.shape[1]

    def body(b_ref, o_ref):
        o_ref[...] = jnp.zeros_like(o_ref)

    return pl.pallas_call(
        body, name=name, grid=(rows // tm,), in_specs=[ANY],
        out_specs=pl.BlockSpec((tm, C), lambda i: (i, 0)),
        out_shape=jax.ShapeDtypeStruct(buf.shape, buf.dtype), input_output_aliases={0: 0},
        compiler_params=_cp("parallel"),
    )(buf)


def mm_res(a, w4, res, name, tm=MM_TILE):
    M, K = a.shape
    N = w4.shape[2]
    tm = min(tm, M)

    def body(a_ref, w_ref, r_ref, o_ref):
        o_ref[...] = r_ref[...] + _dot_rows(a_ref[...], w_ref)

    return pl.pallas_call(
        body, name=name, grid=(M // tm,),
        in_specs=[pl.BlockSpec((tm, K), lambda i: (i, 0)), _full(w4.shape), pl.BlockSpec((tm, N), lambda i: (i, 0))],
        out_specs=pl.BlockSpec((tm, N), lambda i: (i, 0)),
        out_shape=jax.ShapeDtypeStruct((M, N), F32), compiler_params=_cp("parallel"),
    )(a, w4, res)


def mm_cols(a, w, layer, name, out_dtype):
    M = a.shape[0]
    n = w.shape[3]

    def body(a_ref, w_ref, o_ref):
        o_ref[...] = _dot_cols(a_ref[...], w_ref).astype(out_dtype)

    return pl.pallas_call(
        body, name=name, grid=(1,), in_specs=[_full(a.shape), _wspec(w, layer)],
        out_specs=_full((M, N_CHIPS * n)), out_shape=jax.ShapeDtypeStruct((M, N_CHIPS * n), out_dtype),
        compiler_params=_cp("arbitrary"),
    )(a, w)


def mm_nt_cols(dy, w, layer, name):
    M = dy.shape[0]
    K, n = w.shape[2], w.shape[3]

    def body(d_ref, w_ref, o_ref):
        o_ref[...] = _dot_nt_cols(_col_pieces(d_ref[...].astype(BF16), n), w_ref)

    return pl.pallas_call(
        body, name=name, grid=(1,), in_specs=[_full(dy.shape), _wspec(w, layer)],
        out_specs=_full((M, K)), out_shape=jax.ShapeDtypeStruct((M, K), F32), compiler_params=_cp("arbitrary"),
    )(dy, w)


def mm_nt_rows(dy, w4, name, tm=MM_TILE):
    M, N = dy.shape
    K = N_CHIPS * w4.shape[1]
    tm = min(tm, M)

    def body(d_ref, w_ref, o_ref):
        o_ref[...] = _dot_nt_rows(d_ref[...].astype(BF16), w_ref)

    return pl.pallas_call(
        body, name=name, grid=(M // tm,),
        in_specs=[pl.BlockSpec((tm, N), lambda i: (i, 0)), _full(w4.shape)],
        out_specs=pl.BlockSpec((tm, K), lambda i: (i, 0)),
        out_shape=jax.ShapeDtypeStruct((M, K), F32), compiler_params=_cp("parallel"),
    )(dy, w4)


def mm_nt_normbwd(dys, offs, w4, x, g, dx_out, name, tm=MM_TILE):
    M, D = x.shape
    n = w4.shape[2]
    tm = min(tm, M)
    nd = len(dys)

    def body(*refs):
        d_refs = refs[:nd]
        w_ref, x_ref, g_ref, dxo_ref, dx_ref, dg_ref = refs[nd:]
        if nd == 1:
            pieces = _col_pieces(d_refs[0][...].astype(BF16), n)
        else:
            pieces = [r[...].astype(BF16) for r in d_refs]
        dh = _dot_nt_cols(pieces, w_ref)
        dx, dg = _rms_bwd(x_ref[...], g_ref[...], dh)
        dx_ref[...] = dxo_ref[...] + dx

        @pl.when(pl.program_id(0) == 0)
        def _():
            dg_ref[...] = jnp.zeros_like(dg_ref)

        dg_ref[...] += dg

    row = lambda c, off=0: pl.BlockSpec((tm, c), lambda i, p=off // tm: (i + p, 0))
    return pl.pallas_call(
        body, name=name, grid=(M // tm,),
        in_specs=[row(d.shape[1], off) for d, off in zip(dys, offs)] + [_full(w4.shape), row(D), _full(g.shape), row(D)],
        out_specs=[row(D), _full((1, D))],
        out_shape=[jax.ShapeDtypeStruct((M, D), F32), jax.ShapeDtypeStruct((1, D), F32)],
        compiler_params=_cp("arbitrary"),
    )(*dys, w4, x, g, dx_out)


def mm_tn(a, b, name, layout, into=None, b_off=0, out_dtype=BF16, bm=1024, bn=1280, bl=1024, carried=None):
    L, K = a.shape
    N = b.shape[1]
    kind = layout[0]
    arg = layout[1] if len(layout) > 1 else None
    bm, bn, bl = _lane_tile(K, bm), _lane_tile(N, bn), min(bl, L)
    assert L % bl == 0 and b_off % bl == 0, (L, bl, b_off)
    nl = L // bl
    n_sh, r_sh = N // N_CHIPS, K // N_CHIPS
    lay = (None,) if arg is None else (None, None)
    mid = () if arg is None else (arg,)
    gs = 1
    if kind == "plain":
        oshape, oblock, oidx = (K, N), (bm, bn), lambda i, j, l: (i, j)
    elif kind == "slab":
        oshape, oblock, oidx = (N_CHIPS, K, N), (None, bm, bn), lambda i, j, l: (arg, i, j)
    elif kind == "cols":
        bn = max(bn - bn % n_sh, n_sh) if bn >= n_sh else _lane_tile(n_sh, bn)
        gs = max(bn // n_sh, 1)
        per = n_sh // bn if gs == 1 else 1
        oshape = (N_CHIPS,) + ((2,) if arg is not None else ()) + (K, n_sh)
        oblock = ((gs,) if gs > 1 else (None,)) + lay[1:] + (bm, min(bn, n_sh))
        oidx = lambda i, j, l: (j // per,) + mid + (i, j % per)
    else:
        bm = max(bm - bm % r_sh, r_sh) if bm >= r_sh else _lane_tile(r_sh, bm)
        gs = max(bm // r_sh, 1)
        per = r_sh // bm if gs == 1 else 1
        oshape = (N_CHIPS,) + ((2,) if arg is not None else ()) + (r_sh, N)
        oblock = ((gs,) if gs > 1 else (None,)) + lay[1:] + (min(bm, r_sh), bn)
        oidx = lambda i, j, l: (i // per,) + mid + (i % per, j)
    assert K % bm == 0 and N % bn == 0, (K, bm, N, bn)

    grid = (K // bm, N // bn, nl)

    def body(*refs):
        top = end = None
        if carried is not None:
            refs, parts = carried.split(refs, 2 if into is None else 3, 1, 1)
            top, end = carried.hooks(parts, grid)
            top()
        a_ref, b_ref, o_ref, acc = refs[0], refs[1], refs[-2], refs[-1]
        l = pl.program_id(2)

        @pl.when(l == 0)
        def _():
            acc[...] = jnp.zeros_like(acc)

        acc[...] += _dot_tn(a_ref[...].astype(BF16), b_ref[...].astype(BF16))

        @pl.when(l == nl - 1)
        def _():
            if gs == 1:
                o_ref[...] = acc[...].astype(out_dtype)
            elif kind == "cols":
                for t in range(gs):
                    o_ref[t] = acc[:, t * n_sh:(t + 1) * n_sh].astype(out_dtype)
            else:
                for t in range(gs):
                    o_ref[t] = acc[t * r_sh:(t + 1) * r_sh, :].astype(out_dtype)

        if end is not None:
            end()

    in_specs = [pl.BlockSpec((bl, bm), lambda i, j, l: (l, i)),
                pl.BlockSpec((bl, bn), lambda i, j, l, p=b_off // bl: (l + p, j))]
    args = [a, b]
    alias = {}
    if into is not None:
        in_specs.append(ANY)
        args.append(into)
        alias = {2: 0}
    out_specs, out_shape = pl.BlockSpec(oblock, oidx), jax.ShapeDtypeStruct(oshape, out_dtype)
    scratch = [pltpu.VMEM((bm, bn), F32)]
    if carried is None:
        sem = ("parallel", "parallel", "arbitrary")
    else:
        in_specs += [ANY] * len(carried.arrays)
        args += carried.arrays
        out_specs, out_shape = [out_specs] + [ANY] * len(carried.out_shapes), [out_shape] + carried.out_shapes
        scratch += carried.sems
        sem = ("arbitrary",) * 3
    return pl.pallas_call(
        body, name=name, grid=grid, in_specs=in_specs, out_specs=out_specs, out_shape=out_shape,
        scratch_shapes=scratch, input_output_aliases=alias, compiler_params=_cp(*sem),
    )(*args)


def rms_fwd(x, g, name):
    def body(x_ref, g_ref, ob_ref):
        _, xhat = _rms_parts(x_ref[...])
        ob_ref[...] = (xhat * g_ref[...]).astype(BF16)

    return pl.pallas_call(body, name=name, out_shape=jax.ShapeDtypeStruct(x.shape, BF16))(x, g)


def rms_dgain(x, dy0, dy1, name):
    def body(x_ref, d0_ref, d1_ref, o_ref):
        _, xhat = _rms_parts(x_ref[...])
        o_ref[...] = jnp.sum((d0_ref[...] + d1_ref[...]) * xhat, axis=0, keepdims=True)

    return pl.pallas_call(body, name=name, out_shape=jax.ShapeDtypeStruct((1, x.shape[1]), F32))(x, dy0, dy1)


def _s5_discretise(lr, li, logdt, bt_re, bt_im):
    dt = jnp.exp(logdt)
    mag = jnp.exp(lr * dt)
    ab_re = mag * jnp.cos(li * dt)
    ab_im = mag * jnp.sin(li * dt)
    den = lr * lr + li * li
    nr = ab_re - 1.0
    coef_re = (nr * lr + ab_im * li) / den
    coef_im = (ab_im * lr - nr * li) / den
    cr = coef_re[:, None, :]
    ci = coef_im[:, None, :]
    bb_re = cr * bt_re - ci * bt_im
    bb_im = cr * bt_im + ci * bt_re
    return ab_re, ab_im, bb_re, bb_im


def s5_param_fwd(lr, li, logdt, bt_re, bt_im):
    def body(lr_ref, li_ref, ld_ref, br_ref, bi_ref, bbr_ref, bbi_ref):
        _, _, bb_re, bb_im = _s5_discretise(lr_ref[...], li_ref[...], ld_ref[...], br_ref[...], bi_ref[...])
        bbr_ref[...] = bb_re
        bbi_ref[...] = bb_im

    sh = jax.ShapeDtypeStruct(bt_re.shape, F32)
    return pl.pallas_call(body, name="s5_param_fwd", out_shape=[sh, sh])(lr, li, logdt, bt_re, bt_im)


def s5_param_bwd(lr, li, logdt, bt_re, bt_im, d_ab_re, d_ab_im, d_bb_re, d_bb_im):
    def body(lr_ref, li_ref, ld_ref, br_ref, bi_ref, dar_ref, dai_ref, dbr_ref, dbi_ref,
             o_lr, o_li, o_ld, o_br, o_bi):
        _, vjp = jax.vjp(_s5_discretise, lr_ref[...], li_ref[...], ld_ref[...], br_ref[...], bi_ref[...])
        g = vjp((dar_ref[...], dai_ref[...], dbr_ref[...], dbi_ref[...]))
        for o, v in zip((o_lr, o_li, o_ld), g[:3]):
            o[...] = v
        for o, v in zip((o_br, o_bi), g[3:]):
            for c in range(S5_GROUP):
                o[:, c * S5_STATE:(c + 1) * S5_STATE] = v[:, c, :]

    dense = jax.ShapeDtypeStruct((S5_GROUPS, S5_GROUP * S5_STATE), F32)
    shapes = [jax.ShapeDtypeStruct(a.shape, F32) for a in (lr, li, logdt)] + [dense, dense]
    return pl.pallas_call(body, name="s5_param_bwd", out_shape=shapes)(
        lr, li, logdt, bt_re, bt_im, d_ab_re, d_ab_im, d_bb_re, d_bb_im)


def s5_tables(lr_flat, li_flat, logdt_flat):
    def body(lr_ref, li_ref, ld_ref, tab_ref):
        dt = jnp.exp(ld_ref[...])
        a = lr_ref[...] * dt
        th = li_ref[...] * dt
        row = lax.broadcasted_iota(jnp.int32, (8, 1), 0)
        rowf = row.astype(F32)

        def power(e, sign):
            m = jnp.exp(e * a)
            return m * jnp.cos(e * th), sign * m * jnp.sin(e * th)

        k = 0
        for sign, fwd in ((1.0, True), (-1.0, False)):
            for s in (1, 2, 4):
                pr, pi = power(jnp.full((8, 1), float(s), F32), sign)
                keep = (row >= s) if fwd else (row + s < 8)
                tab_ref[k] = jnp.where(keep, pr, 0.0)
                tab_ref[k + 1] = jnp.where(keep, pi, 0.0)
                k += 2
            e = rowf + 1.0 if fwd else 8.0 - rowf
            pr, pi = power(e, sign)
            tab_ref[k] = pr
            tab_ref[k + 1] = pi
            k += 2

    return pl.pallas_call(body, name="s5_tables",
                          out_shape=jax.ShapeDtypeStruct((16, 8, S5_COLS), F32))(lr_flat, li_flat, logdt_flat)


def _scan_block(a, b, tabs, base, cr, ci, reverse):
    for n, s in enumerate((1, 2, 4)):
        mr = tabs[base + 2 * n]
        mi = tabs[base + 2 * n + 1]
        sh = (8 - s) if reverse else s
        ar = pltpu.roll(a, sh, 0)
        br = pltpu.roll(b, sh, 0)
        a, b = a + mr * ar - mi * br, b + mr * br + mi * ar
    pr = tabs[base + 6]
    pi = tabs[base + 7]
    a, b = a + pr * cr - pi * ci, b + pr * ci + pi * cr
    return a, b


class Carried:
    def __init__(self, arrays, out_shapes, sems, start, finish, relay=None):
        self.arrays, self.out_shapes, self.sems = list(arrays), list(out_shapes), list(sems)
        self.start, self.finish, self.relay = start, finish, relay

    def split(self, refs, n_in, n_out, n_scratch):
        a, o, s = len(self.arrays), len(self.out_shapes), len(self.sems)
        own_in, car_in = refs[:n_in], refs[n_in:n_in + a]
        own_out, car_out = refs[n_in + a:n_in + a + n_out], refs[n_in + a + n_out:n_in + a + n_out + o]
        rest = refs[n_in + a + n_out + o:]
        return own_in + own_out + rest[:n_scratch], (car_in, car_out, rest[n_scratch:n_scratch + s])

    def hooks(self, parts, grid):
        first = last = late = None
        for k, n in enumerate(grid):
            i = pl.program_id(k)
            first = (i == 0) if first is None else first & (i == 0)
            last = (i == n - 1) if last is None else last & (i == n - 1)
            late = (i == n - 1) if late is None else late & (i == 0)

        def top():
            pl.when(first)(lambda: self.start(*parts))
            if self.relay is not None:
                pl.when(late)(lambda: self.relay(*parts))

        def end():
            pl.when(last)(lambda: self.finish(*parts))

        return top, end


def s5_fwd(z, bbd_re, bbd_im, ccd_re, ccd_im, tab, dskip, tm=S5_TILE, carried=None):
    L = z.shape[0]
    tm = min(tm, L)
    nt = L // tm

    def body(*refs):
        top = end = None
        if carried is not None:
            refs, parts = carried.split(refs, 7, 4, 3)
            top, end = carried.hooks(parts, (S5_SPLIT, nt))
            top()
        u_ref, bbr_ref, bbi_ref, ccr_ref, cci_ref, tab_ref, d_ref, y_ref, ck_ref, hr_ref, hi_ref, xr, xi, car = refs
        t = pl.program_id(1)

        @pl.when(t == 0)
        def _():
            car[...] = jnp.zeros_like(car)

        u = u_ref[...]
        ub = u.astype(BF16)
        xr[...] = _dot(ub, bbr_ref[...])
        xi[...] = _dot(ub, bbi_ref[...])
        tabs = [tab_ref[k] for k in range(8)]

        def blk(i, c):
            r0 = pl.multiple_of(i * 8, 8)
            a, b = _scan_block(xr[pl.ds(r0, 8), :], xi[pl.ds(r0, 8), :], tabs, 0, c[0], c[1], False)
            xr[pl.ds(r0, 8), :] = a
            xi[pl.ds(r0, 8), :] = b
            return a[7:8, :], b[7:8, :]

        cr, ci = lax.fori_loop(0, tm // 8, blk, (car[0:1, :], car[1:2, :]))
        car[0:1, :] = cr
        car[1:2, :] = ci
        ck_ref[0:1, :] = cr
        ck_ref[1:2, :] = ci
        hrb = xr[...].astype(BF16)
        hib = xi[...].astype(BF16)
        hr_ref[...] = hrb
        hi_ref[...] = hib
        y_ref[...] = _dot(hrb, ccr_ref[...]) - _dot(hib, cci_ref[...]) + d_ref[...] * u
        if end is not None:
            end()

    extra = carried.arrays if carried is not None else []
    extra_out = carried.out_shapes if carried is not None else []
    extra_sems = carried.sems if carried is not None else []
    return pl.pallas_call(
        body, name="s5_fwd", grid=(S5_SPLIT, nt),
        in_specs=[pl.BlockSpec((tm, S5_UC), lambda j, t: (t, j)),
                  pl.BlockSpec((None, S5_UC, S5_CC), lambda j, t: (j, 0, 0)),
                  pl.BlockSpec((None, S5_UC, S5_CC), lambda j, t: (j, 0, 0)),
                  pl.BlockSpec((None, S5_CC, S5_UC), lambda j, t: (j, 0, 0)),
                  pl.BlockSpec((None, S5_CC, S5_UC), lambda j, t: (j, 0, 0)),
                  pl.BlockSpec((8, 8, S5_CC), lambda j, t: (0, 0, j)),
                  pl.BlockSpec((1, S5_UC), lambda j, t: (0, j))] + [ANY] * len(extra),
        out_specs=[pl.BlockSpec((tm, S5_UC), lambda j, t: (t, j)),
                   pl.BlockSpec((None, 2, S5_CC), lambda j, t: (t, 0, j)),
                   pl.BlockSpec((tm, S5_CC), lambda j, t: (t, j)),
                   pl.BlockSpec((tm, S5_CC), lambda j, t: (t, j))] + [ANY] * len(extra_out),
        out_shape=[jax.ShapeDtypeStruct((L, S5_WIDTH), F32), jax.ShapeDtypeStruct((nt, 2, S5_COLS), F32),
                   jax.ShapeDtypeStruct((L, S5_COLS), BF16), jax.ShapeDtypeStruct((L, S5_COLS), BF16)] + extra_out,
        scratch_shapes=[pltpu.VMEM((tm, S5_CC), F32), pltpu.VMEM((tm, S5_CC), F32), pltpu.VMEM((2, S5_CC), F32)]
        + extra_sems,
        compiler_params=_cp("arbitrary" if carried is not None else "parallel", "arbitrary"),
    )(z, bbd_re, bbd_im, ccd_re, ccd_im, tab, dskip, *extra)


def s5_bwd(z, dy, dz, ckpt, hrb, hib, bbd_re, bbd_im, ccd_re, ccd_im, tab, dskip, tm=S5_TILE, carried=None):
    L = z.shape[0]
    tm = min(tm, L)
    nt = L // tm

    def body(*refs):
        top = end = None
        if carried is not None:
            refs, parts = carried.split(refs, 12, 7, 7)
            top, end = carried.hooks(parts, (S5_SPLIT, nt))
            top()
        (u_ref, dy_ref, dz_ref, ck_ref, hrb_ref, hib_ref, bbr_ref, bbi_ref, ccr_ref, cci_ref, tab_ref, d_ref,
         du_ref, da_ref, dbr_ref, dbi_ref, dcr_ref, dci_ref, dd_ref, hr, hi, gr, gi, car, acr, aci) = refs
        t = pl.program_id(1)
        tt = nt - 1 - t

        @pl.when(t == 0)
        def _():
            for r in (car, acr, aci, dbr_ref, dbi_ref, dcr_ref, dci_ref, dd_ref):
                r[...] = jnp.zeros_like(r)

        u = u_ref[...]
        ub = u.astype(BF16)
        dyv = dy_ref[...]
        dyb = dyv.astype(BF16)
        tabs = [None] * 8 + [tab_ref[k] for k in range(8, 16)]

        live = (tt > 0).astype(F32)
        hr[0:8, :] = jnp.broadcast_to(ck_ref[0:1, :] * live, (8, S5_CC))
        hi[0:8, :] = jnp.broadcast_to(ck_ref[1:2, :] * live, (8, S5_CC))
        hrb = hrb_ref[...]
        hib = hib_ref[...]
        hr[8:, :] = hrb.astype(F32)
        hi[8:, :] = hib.astype(F32)
        dcr_ref[...] += _dot_tn(hrb, dyb)
        dci_ref[...] -= _dot_tn(hib, dyb)

        gr[...] = _dot_nt(dyb, ccr_ref[...])
        gi[...] = -_dot_nt(dyb, cci_ref[...])
        row0 = lax.broadcasted_iota(jnp.int32, (8, S5_CC), 0) == 0

        def rblk(k, c):
            i = tm // 8 - 1 - k
            r0 = pl.multiple_of(i * 8, 8)
            a, b = _scan_block(gr[pl.ds(r0, 8), :], gi[pl.ds(r0, 8), :], tabs, 8, c[0], c[1], True)
            gr[pl.ds(r0, 8), :] = a
            gi[pl.ds(r0, 8), :] = b
            r1 = pl.multiple_of(i * 8 + 8, 8)
            hpr = jnp.where(row0, pltpu.roll(hr[pl.ds(r0, 8), :], 1, 0), pltpu.roll(hr[pl.ds(r1, 8), :], 1, 0))
            hpi = jnp.where(row0, pltpu.roll(hi[pl.ds(r0, 8), :], 1, 0), pltpu.roll(hi[pl.ds(r1, 8), :], 1, 0))
            acr[...] += a * hpr + b * hpi
            aci[...] += b * hpr - a * hpi
            return a[0:1, :], b[0:1, :]

        cr, ci = lax.fori_loop(0, tm // 8, rblk, (car[0:1, :], car[1:2, :]))
        car[0:1, :] = cr
        car[1:2, :] = ci

        grb = gr[...].astype(BF16)
        gib = gi[...].astype(BF16)
        du_ref[...] = (_dot_nt(grb, bbr_ref[...]) + _dot_nt(gib, bbi_ref[...]) + d_ref[...] * dyv).astype(BF16)
        dbr_ref[...] += _dot_tn(ub, grb)
        dbi_ref[...] += _dot_tn(ub, gib)
        dd_ref[...] += jnp.sum(dyv * u, axis=0, keepdims=True)

        @pl.when(t == nt - 1)
        def _():
            da_ref[0:1, :] = jnp.sum(acr[...], axis=0, keepdims=True)
            da_ref[1:2, :] = jnp.sum(aci[...], axis=0, keepdims=True)

        if end is not None:
            end()

    extra = carried.arrays if carried is not None else []
    extra_out = carried.out_shapes if carried is not None else []
    extra_sems = carried.sems if carried is not None else []
    chunk = lambda a, b: pl.BlockSpec((None, a, b), lambda j, t: (j, 0, 0))
    return pl.pallas_call(
        body, name="s5_bwd", grid=(S5_SPLIT, nt),
        in_specs=[pl.BlockSpec((tm, S5_UC), lambda j, t: (nt - 1 - t, j)),
                  pl.BlockSpec((tm, S5_UC), lambda j, t: (nt - 1 - t, j)),
                  ANY,
                  pl.BlockSpec((None, 2, S5_CC), lambda j, t: (jnp.maximum(nt - 2 - t, 0), 0, j)),
                  pl.BlockSpec((tm, S5_CC), lambda j, t: (nt - 1 - t, j)),
                  pl.BlockSpec((tm, S5_CC), lambda j, t: (nt - 1 - t, j)),
                  chunk(S5_UC, S5_CC), chunk(S5_UC, S5_CC), chunk(S5_CC, S5_UC), chunk(S5_CC, S5_UC),
                  pl.BlockSpec((16, 8, S5_CC), lambda j, t: (0, 0, j)),
                  pl.BlockSpec((1, S5_UC), lambda j, t: (0, j))] + [ANY] * len(extra),
        out_specs=[pl.BlockSpec((tm, S5_UC), lambda j, t: (nt - 1 - t, j)),
                   pl.BlockSpec((None, 2, S5_CC), lambda j, t: (j, 0, 0)),
                   chunk(S5_UC, S5_CC), chunk(S5_UC, S5_CC), chunk(S5_CC, S5_UC), chunk(S5_CC, S5_UC),
                   pl.BlockSpec((1, S5_UC), lambda j, t: (0, j))] + [ANY] * len(extra_out),
        out_shape=[jax.ShapeDtypeStruct(dz.shape, dz.dtype),
                   jax.ShapeDtypeStruct((S5_SPLIT, 2, S5_CC), F32),
                   jax.ShapeDtypeStruct((S5_SPLIT, S5_UC, S5_CC), F32),
                   jax.ShapeDtypeStruct((S5_SPLIT, S5_UC, S5_CC), F32),
                   jax.ShapeDtypeStruct((S5_SPLIT, S5_CC, S5_UC), F32),
                   jax.ShapeDtypeStruct((S5_SPLIT, S5_CC, S5_UC), F32),
                   jax.ShapeDtypeStruct((1, S5_WIDTH), F32)] + extra_out,
        scratch_shapes=[pltpu.VMEM((tm + 8, S5_CC), F32), pltpu.VMEM((tm + 8, S5_CC), F32),
                        pltpu.VMEM((tm, S5_CC), F32), pltpu.VMEM((tm, S5_CC), F32),
                        pltpu.VMEM((2, S5_CC), F32), pltpu.VMEM((8, S5_CC), F32), pltpu.VMEM((8, S5_CC), F32)]
        + extra_sems,
        input_output_aliases={2: 0},
        compiler_params=_cp("arbitrary" if carried is not None else "parallel", "arbitrary"),
    )(z, dy, dz, ckpt, hrb, hib, bbd_re, bbd_im, ccd_re, ccd_im, tab, dskip, *extra)


_EYE8 = np.eye(S5_GROUPS // S5_SPLIT, dtype=np.float32)


def _blockdiag(a):
    g, r, c = a.shape
    a = a.reshape(S5_SPLIT, g // S5_SPLIT, r, c)
    out = a[:, :, :, None, :] * _EYE8[None, :, None, :, None].astype(a.dtype)
    return out.reshape(S5_SPLIT, (g // S5_SPLIT) * r, (g // S5_SPLIT) * c)


def _blockdiag_extract(a, r, c):
    n = S5_GROUPS // S5_SPLIT
    a = a.reshape(S5_SPLIT, n, r, n, c)
    d = jnp.stack([a[:, k, :, k, :] for k in range(n)], axis=1)
    return d.reshape(S5_GROUPS, r, c)


def s5_mixer_core_fwd(z, lam_re, lam_im, log_dt, b_re, b_im, c_re, c_im, d_skip, carried=None):
    bt_re = jnp.swapaxes(b_re, 1, 2)
    bt_im = jnp.swapaxes(b_im, 1, 2)
    logdt = log_dt.reshape(S5_GROUPS, 1)
    bb_re, bb_im = s5_param_fwd(lam_re, lam_im, logdt, bt_re, bt_im)
    flat = lambda a: a.reshape(1, S5_COLS)
    tab = s5_tables(flat(lam_re), flat(lam_im), flat(jnp.broadcast_to(logdt, (S5_GROUPS, S5_STATE))))
    bbd_re = _blockdiag(bb_re).astype(BF16)
    bbd_im = _blockdiag(bb_im).astype(BF16)
    ccd_re = _blockdiag(jnp.swapaxes(c_re, 1, 2)).astype(BF16)
    ccd_im = _blockdiag(jnp.swapaxes(c_im, 1, 2)).astype(BF16)
    dsk = d_skip.reshape(1, S5_WIDTH)
    y, ckpt, hrb, hib, *landed = s5_fwd(z, bbd_re, bbd_im, ccd_re, ccd_im, tab, dsk, carried=carried)
    saved = (logdt, bt_re, bt_im, bbd_re, bbd_im, ccd_re, ccd_im, tab, dsk, ckpt, hrb, hib)
    return y, saved, landed


def s5_b_from_dense(dense):
    return jnp.swapaxes(dense.reshape(S5_GROUPS, S5_GROUP, S5_STATE), 1, 2)


def s5_mixer_core_bwd(z, dy, dz, lam_re, lam_im, saved, carried=None):
    logdt, bt_re, bt_im, bbd_re, bbd_im, ccd_re, ccd_im, tab, dsk, ckpt, hrb, hib = saved
    dz, da, dbr, dbi, dcr, dci, dd, *landed = s5_bwd(z, dy, dz, ckpt, hrb, hib, bbd_re, bbd_im, ccd_re, ccd_im, tab,
                                                     dsk, carried=carried)
    d_ab_re = da[:, 0, :].reshape(S5_GROUPS, S5_STATE)
    d_ab_im = da[:, 1, :].reshape(S5_GROUPS, S5_STATE)
    d_bb_re = _blockdiag_extract(dbr, S5_GROUP, S5_STATE)
    d_bb_im = _blockdiag_extract(dbi, S5_GROUP, S5_STATE)
    g_lr, g_li, g_ld, g_btr, g_bti = s5_param_bwd(lam_re, lam_im, logdt, bt_re, bt_im,
                                                  d_ab_re, d_ab_im, d_bb_re, d_bb_im)
    g_cre = jnp.swapaxes(_blockdiag_extract(dcr, S5_STATE, S5_GROUP), 1, 2)
    g_cim = jnp.swapaxes(_blockdiag_extract(dci, S5_STATE, S5_GROUP), 1, 2)
    grads = dict(lambda_re=g_lr, lambda_im=g_li, log_dt=g_ld.reshape(S5_GROUPS), b_re=g_btr, b_im=g_bti,
                 c_re=g_cre, c_im=g_cim, d=dd.reshape(S5_WIDTH))
    return dz, grads, landed


Z_U, Z_GA, Z_VAL, Z_GLU, Z_GB = range(5)
SUBLANES = 8


def _shifted_copies(buf, tm):
    n = tm + CONV_HALO - SUBLANES
    for r in range(1, SUBLANES):
        buf[r, 0:n, :] = buf[0, pl.ds(r, n), :]


CONV_ROWS = 32


def _shifted_rows(buf, start, rows, base=0):
    return buf[start % SUBLANES, pl.ds(base + (start - start % SUBLANES), rows), :]


def conv_fwd(z, conv_w, conv_b, tm=ROW_TILE):
    L = z.shape[0]
    tm = min(tm, L)
    nt = L // tm
    hb = tm // CONV_HALO
    C = CONV_WIDTH

    def body(val_ref, glu_ref, valh_ref, gluh_ref, w_ref, b_ref, c_ref, vsh):
        live = (pl.program_id(0) > 0).astype(F32)
        vsh[0, 0:CONV_HALO, :] = valh_ref[...] * _sigmoid(gluh_ref[...]) * live
        vsh[0, CONV_HALO:, :] = val_ref[...] * _sigmoid(glu_ref[...])
        _shifted_copies(vsh, tm)

        def rows(i, carry):
            base = pl.multiple_of(i * CONV_ROWS, CONV_ROWS)
            acc = jnp.broadcast_to(b_ref[...], (CONV_ROWS, C))
            for k in range(CONV_KERNEL):
                acc = acc + w_ref[k:k + 1, :] * _shifted_rows(vsh, CONV_HALO - CONV_KERNEL + 1 + k, CONV_ROWS, base)
            c_ref[pl.ds(base, CONV_ROWS), :] = acc
            return carry

        lax.fori_loop(0, tm // CONV_ROWS, rows, 0)

    cur = lambda col: pl.BlockSpec((tm, C), lambda t: (t, col))
    prev = lambda col: pl.BlockSpec((CONV_HALO, C), lambda t: (jnp.maximum(t * hb - 1, 0), col))
    return pl.pallas_call(
        body, name="conv_fwd", grid=(nt,),
        in_specs=[cur(Z_VAL), cur(Z_GLU), prev(Z_VAL), prev(Z_GLU), _full(conv_w.shape), _full(conv_b.shape)],
        out_specs=pl.BlockSpec((tm, C), lambda t: (t, 0)),
        out_shape=jax.ShapeDtypeStruct((L, C), F32),
        scratch_shapes=[pltpu.VMEM((8, tm + CONV_HALO, C), F32)],
        compiler_params=_cp("parallel"),
    )(z, z, z, z, conv_w, conv_b)


def conv_bwd(z, dc, dz, conv_w, tm=ROW_TILE, carried=None):
    L = z.shape[0]
    tm = min(tm, L)
    nt = L // tm
    hb = tm // CONV_HALO
    nh = L // CONV_HALO
    C = CONV_WIDTH
    off = CONV_HALO - CONV_KERNEL + 1

    def body(*refs):
        top = end = None
        if carried is not None:
            refs, parts = carried.split(refs, 8, 3, 3)
            top, end = carried.hooks(parts, (nt,))
            top()
        val_ref, glu_ref, valh_ref, gluh_ref, dc_ref, dcn_ref, dz_ref, w_ref, dvg_ref, dw_ref, db_ref, vsh, dsh, wacc = refs
        t = pl.program_id(0)

        @pl.when(t == 0)
        def _():
            wacc[...] = jnp.zeros_like(wacc)
            db_ref[...] = jnp.zeros_like(db_ref)

        val = val_ref[...]
        sg = _sigmoid(glu_ref[...])
        vsh[0, 0:CONV_HALO, :] = valh_ref[...] * _sigmoid(gluh_ref[...]) * (t > 0).astype(F32)
        vsh[0, CONV_HALO:, :] = val * sg
        dcv = dc_ref[...]
        dsh[0, 0:tm, :] = dcv
        dsh[0, tm:, :] = dcn_ref[...] * (t < nt - 1).astype(F32)
        _shifted_copies(vsh, tm)
        _shifted_copies(dsh, tm)

        def rows(i, carry):
            base = pl.multiple_of(i * CONV_ROWS, CONV_ROWS)
            dcr = dc_ref[pl.ds(base, CONV_ROWS), :]
            dv = jnp.zeros((CONV_ROWS, C), F32)
            for k in range(CONV_KERNEL):
                dv = dv + w_ref[k:k + 1, :] * _shifted_rows(dsh, CONV_KERNEL - 1 - k, CONV_ROWS, base)
                prod = dcr * _shifted_rows(vsh, off + k, CONV_ROWS, base)
                wacc[k] += jnp.sum(prod.reshape(CONV_ROWS // SUBLANES, SUBLANES, C), axis=0)
            valr = val_ref[pl.ds(base, CONV_ROWS), :]
            sgr = _sigmoid(glu_ref[pl.ds(base, CONV_ROWS), :])
            dvg_ref[pl.ds(base, CONV_ROWS), 0:C] = (dv * sgr).astype(BF16)
            dvg_ref[pl.ds(base, CONV_ROWS), C:] = (dv * valr * sgr * (1.0 - sgr)).astype(BF16)
            return carry

        lax.fori_loop(0, tm // CONV_ROWS, rows, 0)
        db_ref[...] += jnp.sum(dcv, axis=0, keepdims=True)

        @pl.when(t == nt - 1)
        def _():
            dw_ref[...] = jnp.sum(wacc[...], axis=1)

        if end is not None:
            end()

    extra = carried.arrays if carried is not None else []
    extra_out = carried.out_shapes if carried is not None else []
    extra_sems = carried.sems if carried is not None else []
    cur = lambda col: pl.BlockSpec((tm, C), lambda t: (t, col))
    prev = lambda col: pl.BlockSpec((CONV_HALO, C), lambda t: (jnp.maximum(t * hb - 1, 0), col))
    nxt = pl.BlockSpec((CONV_HALO, C), lambda t: (jnp.minimum((t + 1) * hb, nh - 1), 0))
    row = pl.BlockSpec((tm, C), lambda t: (t, 0))
    return pl.pallas_call(
        body, name="conv_bwd", grid=(nt,),
        in_specs=[cur(Z_VAL), cur(Z_GLU), prev(Z_VAL), prev(Z_GLU), row, nxt, ANY, _full(conv_w.shape)]
        + [ANY] * len(extra),
        out_specs=[pl.BlockSpec((tm, 2 * C), lambda t: (t, 1)), _full((CONV_HALO, C)), _full((1, C))]
        + [ANY] * len(extra_out),
        out_shape=[jax.ShapeDtypeStruct(dz.shape, dz.dtype),
                   jax.ShapeDtypeStruct((CONV_HALO, C), F32), jax.ShapeDtypeStruct((1, C), F32)] + extra_out,
        scratch_shapes=[pltpu.VMEM((8, tm + CONV_HALO, C), F32), pltpu.VMEM((8, tm + CONV_HALO, C), F32),
                        pltpu.VMEM((CONV_HALO, SUBLANES, C), F32)] + extra_sems,
        input_output_aliases={6: 0},
        compiler_params=_cp("arbitrary"),
    )(z, z, z, z, dc, dc, dz, conv_w, *extra)


def _ln_parts(c):
    mu = jnp.mean(c, axis=-1, keepdims=True)
    cc = c - mu
    rstd = lax.rsqrt(jnp.mean(cc * cc, axis=-1, keepdims=True) + EPS)
    return rstd, cc * rstd


def _ev_tail_branches(ys, c, wglu, bglu, lng, lnb):
    z1 = _gelu(ys)
    z1b = z1.astype(BF16)
    sg = _sigmoid(_dot_rows(z1b, wglu) + bglu)
    out = z1 * sg
    rstd, chat = _ln_parts(c)
    cn = chat * lng + lnb
    return z1, z1b, sg, out, rstd, chat, cn


def ev_tail_fwd(ys, z, c, x0, wglu, bglu, lng, lnb, wout, tm=ROW_TILE):
    L, D = x0.shape
    tm = min(tm, L)
    W = S5_WIDTH

    def body(ys_ref, ga_ref, c_ref, gb_ref, x_ref, wglu_ref, bglu_ref, lng_ref, lnb_ref, wout_ref, o_ref):
        _, _, _, out, _, _, cn = _ev_tail_branches(ys_ref[...], c_ref[...], wglu_ref, bglu_ref[...],
                                                   lng_ref[...], lnb_ref[...])
        ya = (out * _silu(ga_ref[...])).astype(BF16)
        yb = (_silu(cn) * _silu(gb_ref[...])).astype(BF16)
        o_ref[...] = x_ref[...] + _dot_rows(jnp.concatenate([ya, yb], axis=1), wout_ref)

    row = lambda n, col=0: pl.BlockSpec((tm, n), lambda t: (t, col))
    return pl.pallas_call(
        body, name="ev_tail_fwd", grid=(L // tm,),
        in_specs=[row(W), row(W, Z_GA), row(W), row(W, Z_GB), row(D), _full(wglu.shape), _full(bglu.shape),
                  _full(lng.shape), _full(lnb.shape), _full(wout.shape)],
        out_specs=row(D), out_shape=jax.ShapeDtypeStruct((L, D), F32), compiler_params=_cp("parallel"),
    )(ys, z, c, z, x0, wglu, bglu, lng, lnb, wout)


def ev_tail_bwd(ys, z, c, dx1, wglu, bglu, lng, lnb, wout, tm=ROW_TILE):
    L, D = dx1.shape
    tm = min(tm, L)
    W = S5_WIDTH

    def body(ys_ref, ga_ref, c_ref, gb_ref, dx_ref, wglu_ref, bglu_ref, lng_ref, lnb_ref, wout_ref,
             dys_ref, dc_ref, dz_ref, r_ref, z1_ref, dt_ref, dbg_ref, dlg_ref, dlb_ref):
        @pl.when(pl.program_id(0) == 0)
        def _():
            for r in (dbg_ref, dlg_ref, dlb_ref):
                r[...] = jnp.zeros_like(r)

        ys, ga, gb = ys_ref[...], ga_ref[...], gb_ref[...]
        z1, z1b, sg, out, rstd, chat, cn = _ev_tail_branches(ys, c_ref[...], wglu_ref, bglu_ref[...],
                                                             lng_ref[...], lnb_ref[...])
        (sga, dsga), (sgb, dsgb), (scn, dscn) = _silu_pair(ga), _silu_pair(gb), _silu_pair(cn)
        r_ref[:, 0:W] = (out * sga).astype(BF16)
        r_ref[:, W:] = (scn * sgb).astype(BF16)
        dr = _dot_nt_rows(dx_ref[...].astype(BF16), wout_ref)
        dra, drb = dr[:, 0:W], dr[:, W:]
        dz_ref[...] = jnp.zeros_like(dz_ref)
        dz_ref[:, Z_GA * W:(Z_GA + 1) * W] = (dra * out * dsga).astype(BF16)
        dout = dra * sga
        dt = dout * z1 * sg * (1.0 - sg)
        dtb = dt.astype(BF16)
        dz1 = dout * sg + _dot_nt_rows(dtb, wglu_ref)
        dys_ref[...] = dz1 * _dgelu(ys)
        z1_ref[...] = z1b
        dt_ref[...] = dtb
        dbg_ref[...] += jnp.sum(dt, axis=0, keepdims=True)
        dz_ref[:, Z_GB * W:(Z_GB + 1) * W] = (drb * scn * dsgb).astype(BF16)
        dcn = drb * sgb * dscn
        dlg_ref[...] += jnp.sum(dcn * chat, axis=0, keepdims=True)
        dlb_ref[...] += jnp.sum(dcn, axis=0, keepdims=True)
        dch = dcn * lng_ref[...]
        dc_ref[...] = rstd * (dch - jnp.mean(dch, axis=-1, keepdims=True)
                              - chat * jnp.mean(dch * chat, axis=-1, keepdims=True))

    row = lambda n, col=0: pl.BlockSpec((tm, n), lambda t: (t, col))
    f = lambda n, dt: jax.ShapeDtypeStruct((L, n), dt)
    vec = jax.ShapeDtypeStruct((1, W), F32)
    return pl.pallas_call(
        body, name="ev_tail_bwd", grid=(L // tm,),
        in_specs=[row(W), row(W, Z_GA), row(W), row(W, Z_GB), row(D), _full(wglu.shape), _full(bglu.shape),
                  _full(lng.shape), _full(lnb.shape), _full(wout.shape)],
        out_specs=[row(W), row(W), row(EVEN_IN), row(D), row(W), row(W), _full((1, W)), _full((1, W)), _full((1, W))],
        out_shape=[f(W, F32), f(W, F32), f(EVEN_IN, BF16), f(D, BF16), f(W, BF16), f(W, BF16), vec, vec, vec],
        compiler_params=_cp("arbitrary"),
    )(ys, z, c, z, dx1, wglu, bglu, lng, lnb, wout)


XA_SCALE = XA_HEAD_DIM ** -0.5


def _xa_forward(xv, g, wqg, kv):
    D = D_MODEL
    _, xhat = _rms_parts(xv)
    hb = (xhat * g).astype(BF16)
    qb = (_dot_cols(hb, wqg, (0, 1)) * XA_SCALE).astype(BF16)
    gate = _dot_cols(hb, wqg, (2, 3))
    ps, os_ = [], []
    for h in range(XA_HEADS):
        lo, hi = h * XA_HEAD_DIM, (h + 1) * XA_HEAD_DIM
        s = _dot_nt(qb[:, lo:hi], kv[:, lo:hi])
        e = jnp.exp(s - jnp.max(s, axis=-1, keepdims=True))
        inv = 1.0 / jnp.sum(e, axis=-1, keepdims=True)
        ps.append((e, inv))
        os_.append(_dot(e.astype(BF16), kv[:, D + lo:D + hi]) * inv)
    return hb, qb, gate, ps, jnp.concatenate(os_, axis=1)


def xa_fwd(x, g, wqg, kv, wo, layer, name, tm=MM_TILE):
    L, D = x.shape
    tm = min(tm, L)

    def body(x_ref, g_ref, wqg_ref, kv_ref, wo_ref, o_ref):
        xv = x_ref[...]
        _, _, gate, _, o = _xa_forward(xv, g_ref[...], wqg_ref, kv_ref[...])
        o_ref[...] = xv + _dot_rows((o * _silu(gate)).astype(BF16), wo_ref)

    row = pl.BlockSpec((tm, D), lambda t: (t, 0))
    return pl.pallas_call(
        body, name=name, grid=(L // tm,),
        in_specs=[row, _full(g.shape), _wspec(wqg, layer), _full(kv.shape), _wspec(wo, layer)],
        out_specs=row, out_shape=jax.ShapeDtypeStruct((L, D), F32), compiler_params=_cp("parallel"),
    )(x, g, wqg, kv, wo)


def _loss_head(xv, gv, tv):
    D = xv.shape[-1]
    _, xhat = _rms_parts(xv)
    err = xhat * gv - tv
    loss = 0.5 * jnp.sum(jnp.sum(err * err, axis=-1, keepdims=True), axis=0, keepdims=True) / D
    dx, dg = _rms_bwd(xv, gv, err * (1.0 / D))
    return loss, dx, dg


def xa_fwd_loss(x, g, wqg, kv, wo, layer, target, gf, name, tm=MM_TILE):
    L, D = x.shape
    tm = min(tm, L)

    def body(x_ref, g_ref, wqg_ref, kv_ref, wo_ref, t_ref, gf_ref, loss_ref, dx_ref, dg_ref):
        @pl.when(pl.program_id(0) == 0)
        def _():
            loss_ref[...] = jnp.zeros_like(loss_ref)
            dg_ref[...] = jnp.zeros_like(dg_ref)

        xv = x_ref[...]
        _, _, gate, _, o = _xa_forward(xv, g_ref[...], wqg_ref, kv_ref[...])
        y = xv + _dot_rows((o * _silu(gate)).astype(BF16), wo_ref)
        loss, dx, dg = _loss_head(y, gf_ref[...], t_ref[...])
        loss_ref[...] += loss
        dx_ref[...] = dx
        dg_ref[...] += dg

    row = pl.BlockSpec((tm, D), lambda t: (t, 0))
    return pl.pallas_call(
        body, name=name, grid=(L // tm,),
        in_specs=[row, _full(g.shape), _wspec(wqg, layer), _full(kv.shape), _wspec(wo, layer), row, _full(gf.shape)],
        out_specs=[_full((1, 128)), row, _full((1, D))],
        out_shape=[jax.ShapeDtypeStruct((1, 128), F32), jax.ShapeDtypeStruct((L, D), F32),
                   jax.ShapeDtypeStruct((1, D), F32)],
        compiler_params=_cp("arbitrary"),
    )(x, g, wqg, kv, wo, target, gf)


def xa_bwd(x, dxo, g, wqg, kv, wo, layer, name, tm=MM_TILE):
    L, D = x.shape
    tm = min(tm, L)

    def body(x_ref, dxo_ref, g_ref, wqg_ref, kv_ref, wo_ref, dx_ref, dqg_ref, h_ref, r_ref, dkv_ref, dg_ref):
        @pl.when(pl.program_id(0) == 0)
        def _():
            dkv_ref[...] = jnp.zeros_like(dkv_ref)
            dg_ref[...] = jnp.zeros_like(dg_ref)

        xv = x_ref[...]
        kv = kv_ref[...]
        hb, qb, gate, ps, o = _xa_forward(xv, g_ref[...], wqg_ref, kv)
        sgate, dsgate = _silu_pair(gate)
        h_ref[...] = hb
        r_ref[...] = (o * sgate).astype(BF16)
        dxo = dxo_ref[...]
        dr = _dot_nt_rows(dxo.astype(BF16), wo_ref)
        do = dr * sgate
        dqg_ref[:, D:] = (dr * o * dsgate).astype(BF16)
        dob = do.astype(BF16)
        doo = do * o
        for h in range(XA_HEADS):
            lo, hi = h * XA_HEAD_DIM, (h + 1) * XA_HEAD_DIM
            e, inv = ps[h]
            dp = _dot_nt(dob[:, lo:hi], kv[:, D + lo:D + hi])
            dkv_ref[:, D + lo:D + hi] += _dot_tn(e.astype(BF16), (do[:, lo:hi] * inv).astype(BF16))
            rs = jnp.sum(doo[:, lo:hi], axis=-1, keepdims=True)
            dsb = (e * ((dp - rs) * inv)).astype(BF16)
            dqg_ref[:, lo:hi] = (_dot(dsb, kv[:, lo:hi]) * XA_SCALE).astype(BF16)
            dkv_ref[:, lo:hi] += _dot_tn(dsb, qb[:, lo:hi])
        dh = _dot_nt_cols(_col_pieces(dqg_ref[...], D // 2), wqg_ref)
        dx, dg = _rms_bwd(xv, g_ref[...], dh)
        dx_ref[...] = dxo + dx
        dg_ref[...] += dg

    row = lambda n: pl.BlockSpec((tm, n), lambda t: (t, 0))
    return pl.pallas_call(
        body, name=name, grid=(L // tm,),
        in_specs=[row(D), row(D), _full(g.shape), _wspec(wqg, layer), _full(kv.shape), _wspec(wo, layer)],
        out_specs=[row(D), row(2 * D), row(D), row(D), _full(kv.shape), _full((1, D))],
        out_shape=[jax.ShapeDtypeStruct((L, D), F32), jax.ShapeDtypeStruct((L, 2 * D), BF16),
                   jax.ShapeDtypeStruct((L, D), BF16), jax.ShapeDtypeStruct((L, D), BF16),
                   jax.ShapeDtypeStruct(kv.shape, F32), jax.ShapeDtypeStruct((1, D), F32)],
        compiler_params=_cp("arbitrary"),
    )(x, dxo, g, wqg, kv, wo)


ATT_SCALE = ATT_HEAD_DIM ** -0.5
ATT_PAIRS = ATT_HEADS // 2
SKEW_LANES = 1024
REL_LANES = 384


def _skew(x, left):
    amt = (ATT_QB - 1) - lax.broadcasted_iota(jnp.int32, (ATT_QB, 1), 0)
    for bit in range(8):
        sh = (SKEW_LANES - (1 << bit)) if left else (1 << bit)
        x = jnp.where(((amt >> bit) & 1) == 1, pltpu.roll(x, sh, 1), x)
    return x


def _dist_onehot(shape, dist_axis):
    j = lax.broadcasted_iota(jnp.int32, shape, dist_axis)
    r = lax.broadcasted_iota(jnp.int32, shape, 1 - dist_axis)
    return (jnp.clip((ATT_WIN - 1) - j, -MAX_REL, MAX_REL) + MAX_REL == r).astype(BF16)


def _dot_exact(v, onehot):
    acc = jnp.zeros((v.shape[0], onehot.shape[1]), F32)
    rem = v
    for _ in range(3):
        part = rem.astype(BF16)
        acc = acc + _dot(part, onehot)
        rem = rem - part.astype(F32)
    return acc


ATT_EDGE = ATT_PAD // ATT_QB


def att_bias(rel_bias, carried=None):
    H = rel_bias.shape[0]
    rb = jnp.pad(rel_bias, ((0, 0), (0, REL_LANES - rel_bias.shape[1]))).reshape(H, 1, REL_LANES)

    def body(*refs):
        top = end = None
        if carried is not None:
            refs, parts = carried.split(refs, 1, 1, 0)
            top, end = carried.hooks(parts, (H,))
            top()
        rb_ref, o_ref = refs
        by_col = _dot_exact(jnp.broadcast_to(rb_ref[...], (8, REL_LANES)), _dist_onehot((REL_LANES, SKEW_LANES), 1))
        x = _skew(jnp.broadcast_to(by_col[0:1, :], (ATT_QB, SKEW_LANES)), left=True)[:, 0:ATT_WIN]
        qc = lax.broadcasted_iota(jnp.int32, (ATT_QB, 1), 0) // CHUNK + LEFT_CHUNKS
        col = lax.broadcasted_iota(jnp.int32, (1, ATT_WIN), 1)
        dc = qc - col // CHUNK
        band = (dc >= 0) & (dc <= LEFT_CHUNKS)
        for blk in range(ATT_EDGE + 1):
            o_ref[blk] = jnp.where(band & (col >= ATT_PAD - blk * ATT_QB), x, NEG)
        if end is not None:
            end()

    extra = carried.arrays if carried is not None else []
    extra_out = carried.out_shapes if carried is not None else []
    extra_sems = carried.sems if carried is not None else []
    return pl.pallas_call(
        body, name="att_bias", grid=(H,),
        in_specs=[pl.BlockSpec((None, 1, REL_LANES), lambda h: (h, 0, 0))] + [ANY] * len(extra),
        out_specs=[pl.BlockSpec((ATT_EDGE + 1, None, ATT_QB, ATT_WIN), lambda h: (0, h, 0, 0))] + [ANY] * len(extra_out),
        out_shape=[jax.ShapeDtypeStruct((ATT_EDGE + 1, H, ATT_QB, ATT_WIN), F32)] + extra_out,
        scratch_shapes=extra_sems,
        compiler_params=_cp("arbitrary" if carried is not None else "parallel"),
    )(rb, *extra)


def relbias_bwd(dbias):
    H = dbias.shape[0]

    def body(x_ref, o_ref):
        x = jnp.concatenate([x_ref[...], jnp.zeros((ATT_QB, SKEW_LANES - ATT_WIN), F32)], axis=1)
        col = jnp.sum(_skew(x, left=False), axis=0, keepdims=True)
        o_ref[...] = _dot_exact(jnp.broadcast_to(col, (8, SKEW_LANES)), _dist_onehot((SKEW_LANES, REL_LANES), 0))

    out = pl.pallas_call(
        body, name="relbias_bwd", grid=(H,),
        in_specs=[pl.BlockSpec((None, ATT_QB, ATT_WIN), lambda h: (h, 0, 0))],
        out_specs=pl.BlockSpec((None, 8, REL_LANES), lambda h: (h, 0, 0)),
        out_shape=jax.ShapeDtypeStruct((H, 8, REL_LANES), F32), compiler_params=_cp("parallel"),
    )(dbias)
    return out[:, 0, :2 * MAX_REL + 1]


def _ca_scores(qh, kw, bias):
    s = _dot_nt(qh, kw) + bias
    e = jnp.exp(s - jnp.max(s, axis=-1, keepdims=True))
    return e, 1.0 / jnp.sum(e, axis=-1, keepdims=True)


def _ca_head(qv, m):
    return jnp.where(m, qv, jnp.zeros_like(qv)) * ATT_SCALE


def _ca_bias_spec():
    return pl.BlockSpec((None, 2, ATT_QB, ATT_WIN), lambda hp, b: (jnp.minimum(b, ATT_EDGE), hp, 0, 0))


def ca_fwd(q, kvp, gate, bias):
    L, D = q.shape
    Lp = kvp.shape[0]
    nb = L // ATT_QB

    PP = 2
    W = PP * 128

    def body(q_ref, k_ref, v_ref, g_ref, b_ref, r_ref, o_ref):
        w = pl.multiple_of(pl.program_id(1) * ATT_QB, ATT_QB)
        first = lax.broadcasted_iota(jnp.int32, (1, 128), 1) < ATT_HEAD_DIM
        for pp in range(PP):
            sl = slice(pp * 128, (pp + 1) * 128)
            kw = k_ref[pl.ds(w, ATT_WIN), sl]
            vw = v_ref[pl.ds(w, ATT_WIN), sl]
            qv = q_ref[:, sl]
            outs = []
            for hh, m in enumerate((first, jnp.logical_not(first))):
                e, inv = _ca_scores(_ca_head(qv, m), kw, b_ref[2 * pp + hh])
                outs.append(_dot(e.astype(BF16), vw) * inv)
            o = jnp.where(first, outs[0], outs[1])
            r_ref[:, sl] = (o * _silu(g_ref[:, sl])).astype(BF16)
            o_ref[:, sl] = o.astype(BF16)

    blk = pl.BlockSpec((ATT_QB, W), lambda hp, b: (b, hp))
    bias_blk = pl.BlockSpec((None, 2 * PP, ATT_QB, ATT_WIN), lambda hp, b: (jnp.minimum(b, ATT_EDGE), hp, 0, 0))
    return pl.pallas_call(
        body, name="ca_fwd", grid=(ATT_PAIRS // PP, nb),
        in_specs=[blk, pl.BlockSpec((Lp, W), lambda hp, b: (0, hp)),
                  pl.BlockSpec((Lp, W), lambda hp, b: (0, ATT_PAIRS // PP + hp)), blk, bias_blk],
        out_specs=[blk, blk], out_shape=[jax.ShapeDtypeStruct((L, D), BF16), jax.ShapeDtypeStruct((L, D), BF16)],
        compiler_params=_cp("parallel", "arbitrary"),
    )(q, kvp, kvp, gate, bias)


def ca_bwd(q, kvp, gate, bias, dr, o):
    L, D = q.shape
    Lp = kvp.shape[0]
    nb = L // ATT_QB

    def body(q_ref, k_ref, v_ref, g_ref, b_ref, dr_ref, o_ref, dq_ref, dg_ref, dkb_ref, dvb_ref, db_ref,
             dk_ref, dv_ref):
        b = pl.program_id(1)

        @pl.when(b == 0)
        def _():
            for r in (dk_ref, dv_ref, db_ref):
                r[...] = jnp.zeros_like(r)

        w = pl.multiple_of(b * ATT_QB, ATT_QB)
        kw = k_ref[pl.ds(w, ATT_WIN), :]
        vw = v_ref[pl.ds(w, ATT_WIN), :]
        qv = q_ref[...]
        gate_v = g_ref[...]
        drv = dr_ref[...]
        o = o_ref[...].astype(F32)
        sgate, dsgate = _silu_pair(gate_v)
        do = drv * sgate
        doo = do * o
        first = lax.broadcasted_iota(jnp.int32, (1, 128), 1) < ATT_HEAD_DIM
        dqs = []
        dkw = jnp.zeros((ATT_WIN, 128), F32)
        dvw = jnp.zeros((ATT_WIN, 128), F32)
        for hh, m in enumerate((first, jnp.logical_not(first))):
            qh = _ca_head(qv, m)
            e, inv = _ca_scores(qh, kw, b_ref[hh])
            eb = e.astype(BF16)
            doh = jnp.where(m, do, 0.0)
            dp = _dot_nt(doh.astype(BF16), vw)
            dvw = dvw + _dot_tn(eb, (doh * inv).astype(BF16))
            rs = jnp.sum(jnp.where(m, doo, 0.0), axis=-1, keepdims=True)
            ds = e * ((dp - rs) * inv)
            db_ref[hh] += ds
            dsb = ds.astype(BF16)
            dqs.append(_dot(dsb, kw))
            dkw = dkw + _dot_tn(dsb, qh)
        dg_ref[...] = (drv * o * dsgate).astype(BF16)
        dq_ref[...] = (jnp.where(first, dqs[0], dqs[1]) * ATT_SCALE).astype(BF16)
        dk_ref[pl.ds(w, ATT_WIN), :] += dkw
        dv_ref[pl.ds(w, ATT_WIN), :] += dvw

        @pl.when(b == nb - 1)
        def _():
            dkb_ref[...] = dk_ref[...].astype(BF16)
            dvb_ref[...] = dv_ref[...].astype(BF16)

    blk = pl.BlockSpec((ATT_QB, 128), lambda hp, b: (b, hp))
    kblk = pl.BlockSpec((Lp, 128), lambda hp, b: (0, hp))
    vblk = pl.BlockSpec((Lp, 128), lambda hp, b: (0, ATT_PAIRS + hp))
    bblk = pl.BlockSpec((2, ATT_QB, ATT_WIN), lambda hp, b: (hp, 0, 0))
    return pl.pallas_call(
        body, name="ca_bwd", grid=(ATT_PAIRS, nb),
        in_specs=[blk, kblk, vblk, blk, _ca_bias_spec(), blk, blk],
        out_specs=[blk, blk, kblk, kblk, bblk],
        out_shape=[jax.ShapeDtypeStruct((L, D), BF16), jax.ShapeDtypeStruct((L, D), BF16),
                   jax.ShapeDtypeStruct((Lp, D), BF16), jax.ShapeDtypeStruct((Lp, D), BF16),
                   jax.ShapeDtypeStruct(bias.shape[1:], F32)],
        scratch_shapes=[pltpu.VMEM((Lp, 128), F32), pltpu.VMEM((Lp, 128), F32)],
        compiler_params=_cp("parallel", "arbitrary"),
    )(q, kvp, kvp, gate, bias, dr, o)


_ADAM_C1 = 1.0 / (1.0 - ADAM_B1 ** ADAM_STEP)
_ADAM_C2 = 1.0 / (1.0 - ADAM_B2 ** ADAM_STEP)


def _adam_update(w, g, m, v):
    mn = ADAM_B1 * m + (1.0 - ADAM_B1) * g
    vn = ADAM_B2 * v + (1.0 - ADAM_B2) * g * g
    delta = -ADAM_LR * ((mn * _ADAM_C1) / (jnp.sqrt(vn * _ADAM_C2) + ADAM_EPS) + ADAM_WD * w)
    return delta, mn, vn


def adamw(w, g, m, v, name, tr=512):
    R, C = w.shape
    tr = min(tr, R)

    def body(w_ref, g_ref, m_ref, v_ref, go_ref, d_ref, mo_ref, vo_ref):
        gv = g_ref[...]
        go_ref[...] = gv
        d_ref[...], mo_ref[...], vo_ref[...] = _adam_update(w_ref[...], gv, m_ref[...], v_ref[...])

    blk = pl.BlockSpec((tr, C), lambda i: (i, 0))
    sh = jax.ShapeDtypeStruct((R, C), F32)
    return pl.pallas_call(
        body, name=name, grid=(R // tr,), in_specs=[blk] * 4, out_specs=[blk] * 4,
        out_shape=[sh] * 4, compiler_params=_cp("parallel"),
    )(w, g, m, v)


def adamw_allreduce(gathered, w, m, v, shard, name, slot=None):
    R, C = w.shape
    sharded = slot is None and gathered.shape[2] != C

    def body(s_ref, ga_ref, w_ref, m_ref, v_ref, g_ref, d_ref, mo_ref, vo_ref):
        take = (lambda d: ga_ref[d]) if slot is None else (lambda d: ga_ref[d, slot:slot + R, 0:C])
        g = take(0)
        for d in range(1, N_DEV):
            g = g + take(d)
        g_ref[...] = g
        d_ref[...], mo_ref[...], vo_ref[...] = _adam_update(w_ref[...], g, m_ref[...], v_ref[...])

    blk = pl.BlockSpec((R, C), lambda i, s_ref: (0, 0))
    if slot is not None:
        gblk = pl.BlockSpec(gathered.shape, lambda i, s_ref: (0, 0, 0))
    else:
        gblk = pl.BlockSpec((N_DEV, R, C),
                            (lambda i, s_ref: (0, 0, s_ref[0])) if sharded else (lambda i, s_ref: (0, 0, 0)))
    sh = jax.ShapeDtypeStruct((R, C), F32)
    return pl.pallas_call(
        body, name=name,
        grid_spec=pltpu.PrefetchScalarGridSpec(num_scalar_prefetch=1, grid=(1,), in_specs=[gblk, blk, blk, blk],
                                               out_specs=[blk] * 4),
        out_shape=[sh] * 4, compiler_params=_cp("arbitrary"),
    )(shard, gathered, w, m, v)


LATE = ("ev_s5_glu_w", "ev_w_out", "od_w_in", "od_w_out", "xa_w_qg", "xa_w_kv", "xa_w_o")
EARLY_GRADS = ("od_w_in", "od_w_out", "xa_w_qg", "xa_w_kv", "xa_w_o", "ev_w_out", "ev_s5_glu_w")


def _reduce_to_chip(gs, names, core, tag):
    from_sibling = sibling_send_other_half(gs, "sibling_send_" + tag)
    return [sum_with_sibling(gi, ri, core, "sum_sibling_" + n) for n, gi, ri in zip(names, gs, from_sibling)]


def local_step(x, mem, target, p, gw, late, bias, place, core):
    row = lambda a: a.reshape(1, -1)
    D = D_MODEL
    L = x.shape[0]
    g, big = {}, {}
    gw = dict(gw)

    z, h0b = norm_mm(x, p["ev_norm_g"], gw["ev_w_in"], [((0, 1, 2, 3), F32, 0)], "ev_in")
    ys, s5_saved, landed = s5_mixer_core_fwd(
        z, p["ev_s5_lambda_re"][0], p["ev_s5_lambda_im"][0], p["ev_s5_log_dt"][0], p["ev_s5_b_re"][0],
        p["ev_s5_b_im"][0], p["ev_s5_c_re"][0], p["ev_s5_c_im"][0], p["ev_s5_d"][0],
        carried=carried_allgather([late[n] for n in LATE], early_relay=True))
    for n, gth in zip(LATE, landed):
        rows = gth.shape[1]
        gw[n] = gth.reshape(N_CHIPS, 2, rows // 2, gth.shape[2]) if n.startswith("xa_") else gth
    memn_b = rms_fwd(mem, row(p["mem_norm_g"]), "mem_norm")
    kvs = [mm_cols(memn_b, gw["xa_w_kv"], l, f"xa_kv{l}", BF16) for l in range(2)]
    conv_w = p["ev_conv_w"][0]
    c = conv_fwd(z, conv_w, p["ev_conv_b"])
    tail = (gw["ev_s5_glu_w"], p["ev_s5_glu_b"], p["ev_conv_ln_g"], p["ev_conv_ln_b"], gw["ev_w_out"])
    x1 = ev_tail_fwd(ys, z, c, x, *tail)
    xa0 = (row(p["xa_norm_g"][0]), gw["xa_w_qg"], kvs[0], gw["xa_w_o"], 0)
    x2 = xa_fwd(x1, *xa0, "xa_fwd0")

    q, kvp, gate, h1b = norm_mm(x2, p["od_norm_g"], gw["od_w_in"],
                                [((0,), BF16, 0), ((1, 2), BF16, ATT_PAD), ((3,), F32, 0)], "od_in")
    kvp = zero_rows(kvp, ATT_PAD, "od_kv_pad")
    r, att_o = ca_fwd(q, kvp, gate, bias)
    x3 = mm_res(r, gw["od_w_out"], x2, "od_out")
    xa1 = (row(p["xa_norm_g"][1]), gw["xa_w_qg"], kvs[1], gw["xa_w_o"], 1)
    loss, dx4, dgf = xa_fwd_loss(x3, *xa1, target, row(p["final_norm_g"]), "xa_fwd1_loss")
    g["final_norm_g"] = dgf.reshape(D)

    dx3, dqg1, hx1, rx1, dkv1, dgxa1 = xa_bwd(x3, dx4, *xa1, "xa_bwd1")
    dwqg = mm_tn(hx1, dqg1, "xa_dwqg1", ("cols", 1))
    dwo = mm_tn(rx1, dx4, "xa_dwo1", ("rows", 1))

    big["od_w_out"] = mm_tn(r, dx3, "od_dwout", ("rows",))
    dr = mm_nt_rows(dx3, gw["od_w_out"], "od_out_bwd")
    dq, dgate, dkp, dvp, dbias = ca_bwd(q, kvp, gate, bias, dr, att_o)
    pieces, offs = (dq, dkp, dvp, dgate), (0, ATT_PAD, ATT_PAD, 0)
    dwin = None
    for s in range(N_CHIPS):
        dwin = mm_tn(h1b, pieces[s], f"od_dwin{s}", ("slab", s), into=dwin, b_off=offs[s],
                     bl=ATT_PAD if offs[s] else 1024)
    big["od_w_in"] = dwin
    dx2, dgod = mm_nt_normbwd(pieces, offs, gw["od_w_in"], x2, p["od_norm_g"], dx3, "od_in_bwd")
    g["od_norm_g"] = dgod
    g["od_rel_bias"] = relbias_bwd(dbias)[None]

    dx1, dqg0, hx0, rx0, dkv0, dgxa0 = xa_bwd(x1, dx2, *xa0, "xa_bwd0")
    big["xa_w_qg"] = mm_tn(hx0, dqg0, "xa_dwqg0", ("cols", 0), into=dwqg)
    big["xa_w_o"] = mm_tn(rx0, dx2, "xa_dwo0", ("rows", 0), into=dwo)
    g["xa_norm_g"] = jnp.concatenate([dgxa0, dgxa1], axis=0)

    dys, dc, dz, ra, z1b, dtb, dbglu, dlng, dlnb = ev_tail_bwd(ys, z, c, dx1, *tail)
    big["ev_w_out"] = mm_tn(ra, dx1, "ev_dwout", ("rows",))
    big["ev_s5_glu_w"] = mm_tn(z1b, dtb, "ev_dwglu", ("rows",))
    g["ev_s5_glu_b"], g["ev_conv_ln_g"], g["ev_conv_ln_b"] = dbglu, dlng, dlnb
    dwkv = mm_tn(memn_b, dkv1, "xa_dwkv1", ("cols", 1), bl=MEM_LEN)
    big["xa_w_kv"] = mm_tn(memn_b, dkv0, "xa_dwkv0", ("cols", 0), into=dwkv, bl=MEM_LEN)
    dmem0 = mm_nt_cols(dkv0, gw["xa_w_kv"], 0, "xa_kv_bwd0")
    dmem1 = mm_nt_cols(dkv1, gw["xa_w_kv"], 1, "xa_kv_bwd1")
    g["mem_norm_g"] = rms_dgain(mem, dmem0, dmem1, "mem_norm_bwd").reshape(D)

    shard_major = lambda t: t.reshape((-1,) + t.shape[-2:])
    gs = [shard_major(big[n]) for n in EARLY_GRADS]
    dz, dconvw, dconvb, *from_sibling = conv_bwd(z, dc, dz, conv_w, carried=carried_sibling_send(gs))
    g["ev_conv_w"] = dconvw[None, :CONV_KERNEL]
    g["ev_conv_b"] = dconvb
    chip_sums = [sum_with_sibling(gi, ri, core, "sum_sibling_" + n) for n, gi, ri in zip(EARLY_GRADS, gs, from_sibling)]
    dz, s5g, from_chips = s5_mixer_core_bwd(z, dys, dz, p["ev_s5_lambda_re"][0], p["ev_s5_lambda_im"][0], s5_saved,
                                            carried=carried_chips_exchange(chip_sums))
    reduced = {n: sum_chips(ci, ri, place, "sum_chips_" + n) for n, ci, ri in zip(EARLY_GRADS, chip_sums, from_chips)}
    for n, v in s5g.items():
        g["ev_s5_" + n] = v[None]
    packed, slots = pack_rows([_as2d(g[n]) for n in PACKED_SMALL], "pack_small_grads")
    dwin_ev, *gathered = mm_tn(h0b, dz, "ev_dwin", ("cols",),
                               carried=carried_allgather_devices([packed] + [_as2d(g[n]) for n in SINGLE_SMALL]))
    grad_x, dgev = mm_nt_normbwd((dz,), (0,), gw["ev_w_in"], x, p["ev_norm_g"], dx1, "ev_in_bwd")
    chip_sum = _reduce_to_chip([dwin_ev], ["ev_w_in"], core, "last")
    reduced["ev_w_in"] = sum_chips(chip_sum[0], chips_exchange(chip_sum)[0], place, "sum_chips_ev_w_in")
    return loss, grad_x, g, reduced, dgev, gathered, slots


def _me():
    return lax.axis_index("x"), lax.axis_index("y"), lax.axis_index("c")


def _other_chips(x, y):
    return [(1 - x, y), (x, 1 - y), (1 - x, 1 - y)]


def _remote(src, dst, send_sems, recv_sems, k, to):
    return pltpu.make_async_remote_copy(src_ref=src, dst_ref=dst, send_sem=send_sems.at[k], recv_sem=recv_sems.at[k],
                                        device_id=to, device_id_type=MESH)


def _rows_half(ref, h):
    H = ref.shape[-2] // 2
    return ref.at[(slice(None),) * (len(ref.shape) - 2) + (pl.ds(h * H, H), slice(None))]


def allgather_devices(vs):
    n = len(vs)

    def body(*refs):
        ins, outs = refs[:n], refs[n:2 * n]
        send_sems, recv_sems, local_sems = refs[2 * n:]
        x, y, c = _me()
        sib = (x, y, 1 - c)
        chips = _other_chips(x, y)
        me = 4 * x + 2 * y + c
        local = [pltpu.make_async_copy(ins[i], outs[i].at[me], local_sems.at[i]) for i in range(n)]
        for cp in local:
            cp.start()
        first, passed = [], []
        for i in range(n):
            first.append(_remote(ins[i], outs[i].at[me], send_sems, recv_sems, 7 * i, sib))
            for j, (cx, cy) in enumerate(chips):
                first.append(_remote(ins[i], outs[i].at[me], send_sems, recv_sems, 7 * i + 1 + j, (cx, cy, c)))
        for cp in first:
            cp.start()
        for j, (cx, cy) in enumerate(chips):
            for i in range(n):
                got = outs[i].at[4 * cx + 2 * cy + c]
                _remote(got, got, send_sems, recv_sems, 7 * i + 1 + j, (cx, cy, c)).wait_recv()
                fw = _remote(got, got, send_sems, recv_sems, 7 * i + 4 + j, sib)
                fw.start()
                passed.append(fw)
        for i in range(n):
            got = outs[i].at[4 * x + 2 * y + (1 - c)]
            _remote(got, got, send_sems, recv_sems, 7 * i, sib).wait_recv()
            for j, (cx, cy) in enumerate(chips):
                got = outs[i].at[4 * cx + 2 * cy + (1 - c)]
                _remote(got, got, send_sems, recv_sems, 7 * i + 4 + j, sib).wait_recv()
        for cp in first + passed:
            cp.wait_send()
        for cp in local:
            cp.wait()

    return pl.pallas_call(
        body, name="allgather_devices", in_specs=[ANY] * n, out_specs=[ANY] * n,
        out_shape=[jax.ShapeDtypeStruct((N_DEV,) + v.shape, v.dtype) for v in vs],
        scratch_shapes=[pltpu.SemaphoreType.DMA((7 * n,)), pltpu.SemaphoreType.DMA((7 * n,)),
                        pltpu.SemaphoreType.DMA((n,))],
    )(*vs)


def sibling_send_other_half(gs, name):
    n = len(gs)

    def body(*refs):
        ins, outs = refs[:n], refs[n:2 * n]
        send_sems, recv_sems = refs[2 * n:]
        x, y, c = _me()
        cps = [_remote(_rows_half(ins[i], 1 - c), outs[i], send_sems, recv_sems, i, (x, y, 1 - c)) for i in range(n)]
        for cp in cps:
            cp.start()
        for cp in cps:
            cp.wait()

    return pl.pallas_call(
        body, name=name, in_specs=[ANY] * n, out_specs=[ANY] * n,
        out_shape=[jax.ShapeDtypeStruct((g.shape[0], g.shape[1] // 2, g.shape[2]), g.dtype) for g in gs],
        scratch_shapes=[pltpu.SemaphoreType.DMA((n,)), pltpu.SemaphoreType.DMA((n,))],
    )(*gs)


def chips_exchange(parts):
    n = len(parts)

    def body(*refs):
        ins, outs = refs[:n], refs[n:2 * n]
        send_sems, recv_sems = refs[2 * n:]
        x, y, c = _me()
        cps = []
        for i in range(n):
            nl = ins[i].shape[0] // N_CHIPS
            for j, (cx, cy) in enumerate(_other_chips(x, y)):
                cps.append(_remote(ins[i].at[pl.ds((2 * cx + cy) * nl, nl)], outs[i].at[j], send_sems, recv_sems,
                                   3 * i + j, (cx, cy, c)))
        for cp in cps:
            cp.start()
        for cp in cps:
            cp.wait()

    return pl.pallas_call(
        body, name="chips_exchange", in_specs=[ANY] * n, out_specs=[ANY] * n,
        out_shape=[jax.ShapeDtypeStruct((3, a.shape[0] // N_CHIPS) + a.shape[1:], a.dtype) for a in parts],
        scratch_shapes=[pltpu.SemaphoreType.DMA((3 * n,)), pltpu.SemaphoreType.DMA((3 * n,))],
    )(*parts)


def sibling_share(fulls):
    n = len(fulls)

    def body(*refs):
        outs = refs[n:2 * n]
        send_sems, recv_sems = refs[2 * n:]
        x, y, c = _me()
        cps = [_remote(_rows_half(outs[i], c), _rows_half(outs[i], c), send_sems, recv_sems, i, (x, y, 1 - c))
               for i in range(n)]
        for cp in cps:
            cp.start()
        for i in range(n):
            got = _rows_half(outs[i], 1 - c)
            _remote(got, got, send_sems, recv_sems, i, (x, y, 1 - c)).wait_recv()
        for cp in cps:
            cp.wait_send()

    return pl.pallas_call(
        body, name="sibling_share", in_specs=[ANY] * n, out_specs=[ANY] * n,
        out_shape=[jax.ShapeDtypeStruct(f.shape, f.dtype) for f in fulls],
        input_output_aliases={i: i for i in range(n)},
        scratch_shapes=[pltpu.SemaphoreType.DMA((n,)), pltpu.SemaphoreType.DMA((n,))],
    )(*fulls)


def sum_with_sibling(g, recv, core, name):
    S, H, C = recv.shape
    tr = min(512, H)

    def body(c_ref, g_ref, r_ref, o_ref):
        o_ref[...] = (g_ref[...].astype(F32) + r_ref[...].astype(F32)).astype(o_ref.dtype)

    nb = H // tr
    return pl.pallas_call(
        body, name=name,
        grid_spec=pltpu.PrefetchScalarGridSpec(
            num_scalar_prefetch=1, grid=(S, nb),
            in_specs=[pl.BlockSpec((None, tr, C), lambda s, i, c_ref: (s, c_ref[0] * nb + i, 0)),
                      pl.BlockSpec((None, tr, C), lambda s, i, c_ref: (s, i, 0))],
            out_specs=pl.BlockSpec((None, tr, C), lambda s, i, c_ref: (s, i, 0))),
        out_shape=jax.ShapeDtypeStruct((S, H, C), g.dtype), compiler_params=_cp("parallel", "parallel"),
    )(core, g, recv)


def sum_chips(a, recv, place, name):
    _, nl, H, C = recv.shape
    tr = min(512, H)
    nb = H // tr

    def body(p_ref, a_ref, r_ref, o_ref):
        acc = a_ref[...].astype(F32)
        for j in range(3):
            acc = acc + r_ref[j].astype(F32)
        o_ref[...] = acc

    return pl.pallas_call(
        body, name=name,
        grid_spec=pltpu.PrefetchScalarGridSpec(
            num_scalar_prefetch=1, grid=(nl, nb),
            in_specs=[pl.BlockSpec((None, tr, C), lambda l, i, p_ref: (p_ref[0] * nl + l, i, 0)),
                      pl.BlockSpec((3, None, tr, C), lambda l, i, p_ref: (0, l, i, 0))],
            out_specs=pl.BlockSpec((None, tr, C), lambda l, i, p_ref: (l, p_ref[1] * nb + i, 0))),
        out_shape=jax.ShapeDtypeStruct((nl, 2 * H, C), F32), compiler_params=_cp("parallel", "parallel"),
    )(place, a, recv)


def pack_rows(arrays, name):
    starts, r0 = [], 0
    for a in arrays:
        if a.shape[0] >= SUBLANES:
            r0 = -(-r0 // SUBLANES) * SUBLANES
        starts.append(r0)
        r0 += a.shape[0]
    r0 = -(-r0 // SUBLANES) * SUBLANES
    n = len(arrays)

    def body(*refs):
        o_ref = refs[n]
        o_ref[...] = jnp.zeros_like(o_ref)
        for a_ref, s in zip(refs[:n], starts):
            r, c = a_ref.shape
            o_ref[s:s + r, 0:c] = a_ref[...]

    out = pl.pallas_call(body, name=name, out_shape=jax.ShapeDtypeStruct((r0, PACK_COLS), F32))(*arrays)
    return out, starts


def sum_slot(gathered, slot, shape, name):
    r, c = shape

    def body(ga_ref, o_ref):
        acc = ga_ref[0, slot:slot + r, 0:c]
        for d in range(1, N_DEV):
            acc = acc + ga_ref[d, slot:slot + r, 0:c]
        o_ref[...] = acc

    return pl.pallas_call(body, name=name, out_shape=jax.ShapeDtypeStruct((r, c), F32))(gathered)


def carried_allgather(blocks, early_relay=False):
    n = len(blocks)

    def first_hop(ins, outs, sems, i, j, chip, x, y, c):
        me = 2 * x + y
        return _remote(_rows_half(ins[i], c), _rows_half(outs[i].at[me], c), sems[0], sems[1], 6 * i + j, (*chip, c))

    def start(ins, outs, sems):
        x, y, c = _me()
        for i in range(n):
            pltpu.make_async_copy(ins[i], outs[i].at[2 * x + y], sems[2].at[i]).start()
        for i in range(n):
            for j, chip in enumerate(_other_chips(x, y)):
                first_hop(ins, outs, sems, i, j, chip, x, y, c).start()

    def forwards(outs, sems):
        x, y, c = _me()
        fws = []
        for j, (cx, cy) in enumerate(_other_chips(x, y)):
            for i in range(n):
                got = _rows_half(outs[i].at[2 * cx + cy], c)
                fws.append((_remote(got, got, sems[0], sems[1], 6 * i + j, (cx, cy, c)),
                            _remote(got, got, sems[0], sems[1], 6 * i + 3 + j, (x, y, 1 - c))))
        return fws

    def relay(ins, outs, sems):
        for landed, fw in forwards(outs, sems):
            landed.wait_recv()
            fw.start()

    def settle(ins, outs, sems):
        x, y, c = _me()
        chips = _other_chips(x, y)
        for j, (cx, cy) in enumerate(chips):
            for i in range(n):
                got = _rows_half(outs[i].at[2 * cx + cy], 1 - c)
                _remote(got, got, sems[0], sems[1], 6 * i + 3 + j, (x, y, 1 - c)).wait_recv()
        for i in range(n):
            for j, chip in enumerate(chips):
                first_hop(ins, outs, sems, i, j, chip, x, y, c).wait_send()
        for _, fw in forwards(outs, sems):
            fw.wait_send()
        for i in range(n):
            pltpu.make_async_copy(ins[i], outs[i].at[2 * x + y], sems[2].at[i]).wait()

    def finish(ins, outs, sems):
        relay(ins, outs, sems)
        settle(ins, outs, sems)

    return Carried(blocks, [jax.ShapeDtypeStruct((N_CHIPS,) + b.shape, b.dtype) for b in blocks],
                   [pltpu.SemaphoreType.DMA((6 * n,)), pltpu.SemaphoreType.DMA((6 * n,)), pltpu.SemaphoreType.DMA((n,))],
                   start, settle if early_relay else finish, relay if early_relay else None)


def carried_allgather_devices(vs):
    n = len(vs)

    def first_copies(ins, outs, sems):
        x, y, c = _me()
        me = 4 * x + 2 * y + c
        cps = []
        for i in range(n):
            cps.append(_remote(ins[i], outs[i].at[me], sems[0], sems[1], 7 * i, (x, y, 1 - c)))
            for j, (cx, cy) in enumerate(_other_chips(x, y)):
                cps.append(_remote(ins[i], outs[i].at[me], sems[0], sems[1], 7 * i + 1 + j, (cx, cy, c)))
        return cps

    def local_copies(ins, outs, sems):
        x, y, c = _me()
        return [pltpu.make_async_copy(ins[i], outs[i].at[4 * x + 2 * y + c], sems[2].at[i]) for i in range(n)]

    def start(ins, outs, sems):
        for cp in local_copies(ins, outs, sems) + first_copies(ins, outs, sems):
            cp.start()

    def finish(ins, outs, sems):
        x, y, c = _me()
        sib = (x, y, 1 - c)
        chips = _other_chips(x, y)
        passed = []
        for j, (cx, cy) in enumerate(chips):
            for i in range(n):
                got = outs[i].at[4 * cx + 2 * cy + c]
                _remote(got, got, sems[0], sems[1], 7 * i + 1 + j, (cx, cy, c)).wait_recv()
                fw = _remote(got, got, sems[0], sems[1], 7 * i + 4 + j, sib)
                fw.start()
                passed.append(fw)
        for i in range(n):
            got = outs[i].at[4 * x + 2 * y + (1 - c)]
            _remote(got, got, sems[0], sems[1], 7 * i, sib).wait_recv()
            for j, (cx, cy) in enumerate(chips):
                got = outs[i].at[4 * cx + 2 * cy + (1 - c)]
                _remote(got, got, sems[0], sems[1], 7 * i + 4 + j, sib).wait_recv()
        for cp in first_copies(ins, outs, sems) + passed:
            cp.wait_send()
        for cp in local_copies(ins, outs, sems):
            cp.wait()

    return Carried(vs, [jax.ShapeDtypeStruct((N_DEV,) + v.shape, v.dtype) for v in vs],
                   [pltpu.SemaphoreType.DMA((7 * n,)), pltpu.SemaphoreType.DMA((7 * n,)), pltpu.SemaphoreType.DMA((n,))],
                   start, finish)


def carried_sibling_send(gs):
    n = len(gs)

    def copies(ins, outs, sems):
        x, y, c = _me()
        return [_remote(_rows_half(ins[i], 1 - c), outs[i], sems[0], sems[1], i, (x, y, 1 - c)) for i in range(n)]

    def start(ins, outs, sems):
        for cp in copies(ins, outs, sems):
            cp.start()

    def finish(ins, outs, sems):
        for cp in copies(ins, outs, sems):
            cp.wait()

    return Carried(gs, [jax.ShapeDtypeStruct((g.shape[0], g.shape[1] // 2, g.shape[2]), g.dtype) for g in gs],
                   [pltpu.SemaphoreType.DMA((n,)), pltpu.SemaphoreType.DMA((n,))], start, finish)


def carried_chips_exchange(parts):
    n = len(parts)

    def copies(ins, outs, sems):
        x, y, c = _me()
        cps = []
        for i in range(n):
            nl = ins[i].shape[0] // N_CHIPS
            for j, (cx, cy) in enumerate(_other_chips(x, y)):
                cps.append(_remote(ins[i].at[pl.ds((2 * cx + cy) * nl, nl)], outs[i].at[j], sems[0], sems[1],
                                   3 * i + j, (cx, cy, c)))
        return cps

    def start(ins, outs, sems):
        for cp in copies(ins, outs, sems):
            cp.start()

    def finish(ins, outs, sems):
        for cp in copies(ins, outs, sems):
            cp.wait()

    return Carried(parts, [jax.ShapeDtypeStruct((3, a.shape[0] // N_CHIPS) + a.shape[1:], a.dtype) for a in parts],
                   [pltpu.SemaphoreType.DMA((3 * n,)), pltpu.SemaphoreType.DMA((3 * n,))], start, finish)


BIG = ("ev_w_in", "ev_s5_glu_w", "ev_w_out", "od_w_in", "od_w_out", "xa_w_qg", "xa_w_kv", "xa_w_o")
SHARDED_F32 = (("ev_conv_w", 2), ("od_norm_g", 1))
SMALL = ("mem_norm_g", "ev_norm_g", "ev_s5_lambda_re", "ev_s5_lambda_im", "ev_s5_log_dt", "ev_s5_b_re", "ev_s5_b_im",
         "ev_s5_c_re", "ev_s5_c_im", "ev_s5_d", "ev_s5_glu_b", "ev_conv_b", "ev_conv_ln_g", "ev_conv_ln_b",
         "od_rel_bias", "xa_norm_g", "final_norm_g")
NARROW = ("ev_s5_c_re", "ev_s5_c_im")
DENSE_B = ("ev_s5_b_re", "ev_s5_b_im")
PACK_COLS = 1024
PACKED_SMALL = tuple(n for n in SMALL if n not in NARROW and n != "ev_norm_g")
SINGLE_SMALL = NARROW + tuple(n for n, _ in SHARDED_F32)
WEIGHTS = ("mem_norm_g", "ev_norm_g", "ev_w_in", "ev_s5_lambda_re", "ev_s5_lambda_im", "ev_s5_log_dt", "ev_s5_b_re",
           "ev_s5_b_im", "ev_s5_c_re", "ev_s5_c_im", "ev_s5_d", "ev_s5_glu_w", "ev_s5_glu_b", "ev_conv_w", "ev_conv_b",
           "ev_conv_ln_g", "ev_conv_ln_b", "ev_w_out", "od_norm_g", "od_w_in", "od_rel_bias", "od_w_out", "xa_norm_g",
           "xa_w_qg", "xa_w_kv", "xa_w_o", "final_norm_g")


def _as2d(a):
    return a.reshape(1, -1) if a.ndim == 1 else a.reshape(-1, a.shape[-1])


def kernel(x, mem, mem_norm_g, ev_norm_g, ev_w_in, ev_s5_lambda_re, ev_s5_lambda_im, ev_s5_log_dt, ev_s5_b_re, ev_s5_b_im, ev_s5_c_re, ev_s5_c_im, ev_s5_d, ev_s5_glu_w, ev_s5_glu_b, ev_conv_w, ev_conv_b, ev_conv_ln_g, ev_conv_ln_b, ev_w_out, od_norm_g, od_w_in, od_rel_bias, od_w_out, xa_norm_g, xa_w_qg, xa_w_kv, xa_w_o, final_norm_g, loss_target, m_mem_norm_g, m_ev_norm_g, m_ev_w_in, m_ev_s5_lambda_re, m_ev_s5_lambda_im, m_ev_s5_log_dt, m_ev_s5_b_re, m_ev_s5_b_im, m_ev_s5_c_re, m_ev_s5_c_im, m_ev_s5_d, m_ev_s5_glu_w, m_ev_s5_glu_b, m_ev_conv_w, m_ev_conv_b, m_ev_conv_ln_g, m_ev_conv_ln_b, m_ev_w_out, m_od_norm_g, m_od_w_in, m_od_rel_bias, m_od_w_out, m_xa_norm_g, m_xa_w_qg, m_xa_w_kv, m_xa_w_o, m_final_norm_g, v_mem_norm_g, v_ev_norm_g, v_ev_w_in, v_ev_s5_lambda_re, v_ev_s5_lambda_im, v_ev_s5_log_dt, v_ev_s5_b_re, v_ev_s5_b_im, v_ev_s5_c_re, v_ev_s5_c_im, v_ev_s5_d, v_ev_s5_glu_w, v_ev_s5_glu_b, v_ev_conv_w, v_ev_conv_b, v_ev_conv_ln_g, v_ev_conv_ln_b, v_ev_w_out, v_od_norm_g, v_od_w_in, v_od_rel_bias, v_od_w_out, v_xa_norm_g, v_xa_w_qg, v_xa_w_kv, v_xa_w_o, v_final_norm_g):
    a = dict(locals())
    w = {n: a[n] for n in WEIGHTS}
    shard = (2 * lax.axis_index("x") + lax.axis_index("y")).reshape(1).astype(jnp.int32)
    core = lax.axis_index("c").reshape(1).astype(jnp.int32)

    place = jnp.concatenate([shard, core])

    blocks = {n: w[n].astype(BF16).reshape(-1, w[n].shape[-1]) for n in BIG}
    conv_blk = jnp.pad(_as2d(w["ev_conv_w"]), ((0, 1), (0, 0)))
    odn_blk = w["od_norm_g"].reshape(2, -1)
    bias, evin_g, conv_g, odn_g = att_bias(w["od_rel_bias"][0],
                                           carried=carried_allgather([blocks["ev_w_in"], conv_blk, odn_blk]))
    gw = {"ev_w_in": evin_g}
    p = {n: w[n] for n in SMALL}
    p["ev_conv_w"] = jnp.concatenate([conv_g[s, :CONV_KERNEL] for s in range(N_CHIPS)], axis=1)[None]
    p["od_norm_g"] = odn_g.reshape(1, D_MODEL)

    loss, grad_x, g, reduced, dgev, gath, slots = local_step(x[0], mem[0], loss_target[0], p, gw,
                                                             {n: blocks[n] for n in LATE}, bias, place, core)
    loss = lax.psum(loss[0, 0], ("x", "y", "c"))
    g_big = dict(zip(BIG, sibling_share([reduced[n] for n in BIG])))

    out = {tag: {} for tag in ("grad", "delta", "m", "v")}
    for n in BIG:
        sh = w[n].shape
        to2d = lambda t: t.reshape(-1, sh[-1])
        gn, d, mn, vn = adamw(to2d(w[n]), to2d(g_big[n]), to2d(a["m_" + n]), to2d(a["v_" + n]), "adamw_" + n)
        for tag, val in zip(("grad", "delta", "m", "v"), (gn, d, mn, vn)):
            out[tag][n] = val.reshape(sh)

    jobs = [(n, gath[0], s) for n, s in zip(PACKED_SMALL, slots)]
    jobs += [(n, gt, None) for n, gt in zip(SINGLE_SMALL, gath[1:])]
    jobs += [("ev_norm_g", allgather_devices([dgev])[0], None)]
    for n, gt, slot in jobs:
        sh = w[n].shape
        w2, m2, v2 = _as2d(w[n]), _as2d(a["m_" + n]), _as2d(a["v_" + n])
        if n in DENSE_B:
            gn = _as2d(s5_b_from_dense(sum_slot(gt, slot, g[n].shape[-2:], "sum_" + n)))
            gn, d, mn, vn = adamw(w2, gn, m2, v2, "adamw_" + n)
        else:
            gn, d, mn, vn = adamw_allreduce(gt, w2, m2, v2, shard, "adamw_" + n, slot=slot)
        for tag, val in zip(("grad", "delta", "m", "v"), (gn, d, mn, vn)):
            out[tag][n] = val.reshape(sh)

    res = [loss, grad_x[None]]
    for tag in ("grad", "delta", "m", "v"):
        res += [out[tag][n] for n in WEIGHTS]
    return tuple(res)
```

```python
import math

import jax
import jax.numpy as jnp
import numpy as np
from jax import lax
from jax.experimental import pallas as pl
from jax.experimental.pallas import tpu as pltpu

F32 = jnp.float32
BF16 = jnp.bfloat16

D_MODEL = 1024
CHUNK = 64
LEFT_CHUNKS = 8
S5_WIDTH = 512
S5_GROUP = 16
S5_GROUPS = 32
S5_STATE = 64
S5_COLS = S5_GROUPS * S5_STATE
S5_SPLIT = 4
S5_CC = S5_COLS // S5_SPLIT
S5_UC = S5_WIDTH // S5_SPLIT
CONV_WIDTH = 512
CONV_KERNEL = 31
CONV_HALO = 32
ATT_HEADS = 16
ATT_HEAD_DIM = 64
MAX_REL = 128
MEM_LEN = 256
XA_HEADS = 4
XA_HEAD_DIM = 256
EPS = 1e-6
EVEN_IN = 2560
ODD_IN = 4096

ADAM_LR = 0.001
ADAM_B1 = 0.9
ADAM_B2 = 0.999
ADAM_EPS = 1e-08
ADAM_WD = 0.01
ADAM_STEP = 10

ROW_TILE = 256
MM_TILE = 512
S5_TILE = 512
ATT_QB = 256
ATT_PAD = LEFT_CHUNKS * CHUNK
ATT_WIN = ATT_PAD + ATT_QB
VMEM_LIMIT_V7X = 56 * 1024 * 1024
NEG = -1e30
LANES = 128
N_CHIPS = 4
N_DEV = 8

MESH = pl.DeviceIdType.MESH
ANY = pl.BlockSpec(memory_space=pl.ANY)


def _cp(*sem, vmem=VMEM_LIMIT_V7X):
    return pltpu.CompilerParams(dimension_semantics=sem if sem else None, vmem_limit_bytes=vmem)


def _full(shape):
    n = len(shape)
    return pl.BlockSpec(shape, lambda *_: (0,) * n)


def _wspec(w, layer=None):
    if layer is None:
        return _full(w.shape)
    s, _, r, c = w.shape
    return pl.BlockSpec((s, None, r, c), lambda *_: (0, layer, 0, 0))


def _lane_tile(n, cap):
    return max(t for t in range(LANES, min(n, cap) + 1, LANES) if n % t == 0)


def _sigmoid(x):
    return 1.0 / (1.0 + jnp.exp(-x))


def _silu(x):
    return x * _sigmoid(x)


def _silu_pair(x):
    s = _sigmoid(x)
    return x * s, s * (1.0 + x * (1.0 - s))


_GELU_C = math.sqrt(2.0 / math.pi)


def _gelu(x):
    return 0.5 * x * (1.0 + jnp.tanh(_GELU_C * (x + 0.044715 * x * x * x)))


def _dgelu(x):
    t = jnp.tanh(_GELU_C * (x + 0.044715 * x * x * x))
    return 0.5 * (1.0 + t) + 0.5 * x * (1.0 - t * t) * _GELU_C * (1.0 + 3.0 * 0.044715 * x * x)


def _dot(a, b):
    return jnp.dot(a, b, preferred_element_type=F32)


def _dot_nt(a, b):
    return lax.dot_general(a, b, (((1,), (1,)), ((), ())), preferred_element_type=F32)


def _dot_tn(a, b):
    return lax.dot_general(a, b, (((0,), (0,)), ((), ())), preferred_element_type=F32)


def _dot_cols(a, w4, shards=range(N_CHIPS)):
    return jnp.concatenate([_dot(a, w4[s]) for s in shards], axis=1)


def _dot_rows(a, w4):
    r = w4.shape[1]
    acc = _dot(a[:, 0:r], w4[0])
    for s in range(1, N_CHIPS):
        acc = acc + _dot(a[:, s * r:(s + 1) * r], w4[s])
    return acc


def _dot_nt_cols(dys, w4):
    acc = _dot_nt(dys[0], w4[0])
    for s in range(1, N_CHIPS):
        acc = acc + _dot_nt(dys[s], w4[s])
    return acc


def _dot_nt_rows(dy, w4):
    return jnp.concatenate([_dot_nt(dy, w4[s]) for s in range(N_CHIPS)], axis=1)


def _col_pieces(v, n):
    return [v[:, s * n:(s + 1) * n] for s in range(N_CHIPS)]


def _rms_parts(xv):
    inv = lax.rsqrt(jnp.mean(xv * xv, axis=-1, keepdims=True) + EPS)
    return inv, xv * inv


def _rms_bwd(xv, g, dh):
    inv, xhat = _rms_parts(xv)
    dg = jnp.sum(dh * xhat, axis=0, keepdims=True)
    dxh = dh * g
    dx = inv * (dxh - xhat * jnp.mean(dxh * xhat, axis=-1, keepdims=True))
    return dx, dg


def norm_mm(x, g, w4, groups, name, tm=MM_TILE):
    M, D = x.shape
    n = w4.shape[2]
    tm = min(tm, M)

    def body(x_ref, g_ref, w_ref, *outs):
        _, xhat = _rms_parts(x_ref[...])
        hb = (xhat * g_ref[...]).astype(BF16)
        for o, (shards, dt, _) in zip(outs, groups):
            o[...] = _dot_cols(hb, w_ref, shards).astype(dt)
        outs[-1][...] = hb

    out_shape = [jax.ShapeDtypeStruct((M + pad, len(sh) * n), dt) for (sh, dt, pad) in groups]
    out_specs = [pl.BlockSpec((tm, len(sh) * n), lambda i, p=pad // tm: (i + p, 0)) for (sh, _, pad) in groups]
    out_shape.append(jax.ShapeDtypeStruct((M, D), BF16))
    out_specs.append(pl.BlockSpec((tm, D), lambda i: (i, 0)))
    return pl.pallas_call(
        body, name=name, grid=(M // tm,),
        in_specs=[pl.BlockSpec((tm, D), lambda i: (i, 0)), _full(g.shape), _full(w4.shape)],
        out_specs=out_specs, out_shape=out_shape, compiler_params=_cp("parallel"),
    )(x, g, w4)


def zero_rows(buf, rows, name, tm=ROW_TILE):
    C = buf.shape[1]

    def body(b_ref, o_ref):
        o_ref[...] = jnp.zeros_like(o_ref)

    return pl.pallas_call(
        body, name=name, grid=(rows // tm,), in_specs=[ANY],
        out_specs=pl.BlockSpec((tm, C), lambda i: (i, 0)),
        out_shape=jax.ShapeDtypeStruct(buf.shape, buf.dtype), input_output_aliases={0: 0},
        compiler_params=_cp("parallel"),
    )(buf)


def mm_res(a, w4, res, name, tm=MM_TILE):
    M, K = a.shape
    N = w4.shape[2]
    tm = min(tm, M)

    def body(a_ref, w_ref, r_ref, o_ref):
        o_ref[...] = r_ref[...] + _dot_rows(a_ref[...], w_ref)

    return pl.pallas_call(
        body, name=name, grid=(M // tm,),
        in_specs=[pl.BlockSpec((tm, K), lambda i: (i, 0)), _full(w4.shape), pl.BlockSpec((tm, N), lambda i: (i, 0))],
        out_specs=pl.BlockSpec((tm, N), lambda i: (i, 0)),
        out_shape=jax.ShapeDtypeStruct((M, N), F32), compiler_params=_cp("parallel"),
    )(a, w4, res)


def mm_cols(a, w, layer, name, out_dtype):
    M = a.shape[0]
    n = w.shape[3]

    def body(a_ref, w_ref, o_ref):
        o_ref[...] = _dot_cols(a_ref[...], w_ref).astype(out_dtype)

    return pl.pallas_call(
        body, name=name, grid=(1,), in_specs=[_full(a.shape), _wspec(w, layer)],
        out_specs=_full((M, N_CHIPS * n)), out_shape=jax.ShapeDtypeStruct((M, N_CHIPS * n), out_dtype),
        compiler_params=_cp("arbitrary"),
    )(a, w)


def mm_nt_cols(dy, w, layer, name):
    M = dy.shape[0]
    K, n = w.shape[2], w.shape[3]

    def body(d_ref, w_ref, o_ref):
        o_ref[...] = _dot_nt_cols(_col_pieces(d_ref[...].astype(BF16), n), w_ref)

    return pl.pallas_call(
        body, name=name, grid=(1,), in_specs=[_full(dy.shape), _wspec(w, layer)],
        out_specs=_full((M, K)), out_shape=jax.ShapeDtypeStruct((M, K), F32), compiler_params=_cp("arbitrary"),
    )(dy, w)


def mm_nt_rows(dy, w4, name, out_dtype, tm=MM_TILE):
    M, N = dy.shape
    K = N_CHIPS * w4.shape[1]
    tm = min(tm, M)

    def body(d_ref, w_ref, o_ref):
        o_ref[...] = _dot_nt_rows(d_ref[...].astype(BF16), w_ref).astype(out_dtype)

    return pl.pallas_call(
        body, name=name, grid=(M // tm,),
        in_specs=[pl.BlockSpec((tm, N), lambda i: (i, 0)), _full(w4.shape)],
        out_specs=pl.BlockSpec((tm, K), lambda i: (i, 0)),
        out_shape=jax.ShapeDtypeStruct((M, K), out_dtype), compiler_params=_cp("parallel"),
    )(dy, w4)


def mm_nt_normbwd(dys, offs, w4, x, g, dx_out, name, tm=MM_TILE):
    M, D = x.shape
    n = w4.shape[2]
    tm = min(tm, M)
    nd = len(dys)

    def body(*refs):
        d_refs = refs[:nd]
        w_ref, x_ref, g_ref, dxo_ref, dx_ref, dg_ref = refs[nd:]
        if nd == 1:
            pieces = _col_pieces(d_refs[0][...].astype(BF16), n)
        else:
            pieces = [r[...].astype(BF16) for r in d_refs]
        dh = _dot_nt_cols(pieces, w_ref)
        dx, dg = _rms_bwd(x_ref[...], g_ref[...], dh)
        dx_ref[...] = dxo_ref[...] + dx

        @pl.when(pl.program_id(0) == 0)
        def _():
            dg_ref[...] = jnp.zeros_like(dg_ref)

        dg_ref[...] += dg

    row = lambda c, off=0: pl.BlockSpec((tm, c), lambda i, p=off // tm: (i + p, 0))
    return pl.pallas_call(
        body, name=name, grid=(M // tm,),
        in_specs=[row(d.shape[1], off) for d, off in zip(dys, offs)] + [_full(w4.shape), row(D), _full(g.shape), row(D)],
        out_specs=[row(D), _full((1, D))],
        out_shape=[jax.ShapeDtypeStruct((M, D), F32), jax.ShapeDtypeStruct((1, D), F32)],
        compiler_params=_cp("arbitrary"),
    )(*dys, w4, x, g, dx_out)


def mm_tn(a, b, name, layout, into=None, b_off=0, out_dtype=BF16, bm=1024, bn=1280, bl=1024, carried=None):
    L, K = a.shape
    N = b.shape[1]
    kind = layout[0]
    arg = layout[1] if len(layout) > 1 else None
    bm, bn, bl = _lane_tile(K, bm), _lane_tile(N, bn), min(bl, L)
    assert L % bl == 0 and b_off % bl == 0, (L, bl, b_off)
    nl = L // bl
    n_sh, r_sh = N // N_CHIPS, K // N_CHIPS
    lay = (None,) if arg is None else (None, None)
    mid = () if arg is None else (arg,)
    gs = 1
    if kind == "plain":
        oshape, oblock, oidx = (K, N), (bm, bn), lambda i, j, l: (i, j)
    elif kind == "slab":
        oshape, oblock, oidx = (N_CHIPS, K, N), (None, bm, bn), lambda i, j, l: (arg, i, j)
    elif kind == "cols":
        bn = max(bn - bn % n_sh, n_sh) if bn >= n_sh else _lane_tile(n_sh, bn)
        gs = max(bn // n_sh, 1)
        per = n_sh // bn if gs == 1 else 1
        oshape = (N_CHIPS,) + ((2,) if arg is not None else ()) + (K, n_sh)
        oblock = ((gs,) if gs > 1 else (None,)) + lay[1:] + (bm, min(bn, n_sh))
        oidx = lambda i, j, l: (j // per,) + mid + (i, j % per)
    else:
        bm = max(bm - bm % r_sh, r_sh) if bm >= r_sh else _lane_tile(r_sh, bm)
        gs = max(bm // r_sh, 1)
        per = r_sh // bm if gs == 1 else 1
        oshape = (N_CHIPS,) + ((2,) if arg is not None else ()) + (r_sh, N)
        oblock = ((gs,) if gs > 1 else (None,)) + lay[1:] + (min(bm, r_sh), bn)
        oidx = lambda i, j, l: (i // per,) + mid + (i % per, j)
    assert K % bm == 0 and N % bn == 0, (K, bm, N, bn)

    grid = (K // bm, N // bn, nl)

    def body(*refs):
        top = end = None
        if carried is not None:
            refs, parts = carried.split(refs, 2 if into is None else 3, 1, 1)
            top, end = carried.hooks(parts, grid)
            top()
        a_ref, b_ref, o_ref, acc = refs[0], refs[1], refs[-2], refs[-1]
        l = pl.program_id(2)

        @pl.when(l == 0)
        def _():
            acc[...] = jnp.zeros_like(acc)

        acc[...] += _dot_tn(a_ref[...].astype(BF16), b_ref[...].astype(BF16))

        @pl.when(l == nl - 1)
        def _():
            if gs == 1:
                o_ref[...] = acc[...].astype(out_dtype)
            elif kind == "cols":
                for t in range(gs):
                    o_ref[t] = acc[:, t * n_sh:(t + 1) * n_sh].astype(out_dtype)
            else:
                for t in range(gs):
                    o_ref[t] = acc[t * r_sh:(t + 1) * r_sh, :].astype(out_dtype)

        if end is not None:
            end()

    in_specs = [pl.BlockSpec((bl, bm), lambda i, j, l: (l, i)),
                pl.BlockSpec((bl, bn), lambda i, j, l, p=b_off // bl: (l + p, j))]
    args = [a, b]
    alias = {}
    if into is not None:
        in_specs.append(ANY)
        args.append(into)
        alias = {2: 0}
    out_specs, out_shape = pl.BlockSpec(oblock, oidx), jax.ShapeDtypeStruct(oshape, out_dtype)
    scratch = [pltpu.VMEM((bm, bn), F32)]
    if carried is None:
        sem = ("parallel", "parallel", "arbitrary")
    else:
        in_specs += [ANY] * len(carried.arrays)
        args += carried.arrays
        out_specs, out_shape = [out_specs] + [ANY] * len(carried.out_shapes), [out_shape] + carried.out_shapes
        scratch += carried.sems
        sem = ("arbitrary",) * 3
    return pl.pallas_call(
        body, name=name, grid=grid, in_specs=in_specs, out_specs=out_specs, out_shape=out_shape,
        scratch_shapes=scratch, input_output_aliases=alias, compiler_params=_cp(*sem),
    )(*args)


def rms_fwd(x, g, name):
    def body(x_ref, g_ref, ob_ref):
        _, xhat = _rms_parts(x_ref[...])
        ob_ref[...] = (xhat * g_ref[...]).astype(BF16)

    return pl.pallas_call(body, name=name, out_shape=jax.ShapeDtypeStruct(x.shape, BF16))(x, g)


def rms_dgain(x, dy0, dy1, name):
    def body(x_ref, d0_ref, d1_ref, o_ref):
        _, xhat = _rms_parts(x_ref[...])
        o_ref[...] = jnp.sum((d0_ref[...] + d1_ref[...]) * xhat, axis=0, keepdims=True)

    return pl.pallas_call(body, name=name, out_shape=jax.ShapeDtypeStruct((1, x.shape[1]), F32))(x, dy0, dy1)


def _s5_discretise(lr, li, logdt, bt_re, bt_im):
    dt = jnp.exp(logdt)
    mag = jnp.exp(lr * dt)
    ab_re = mag * jnp.cos(li * dt)
    ab_im = mag * jnp.sin(li * dt)
    den = lr * lr + li * li
    nr = ab_re - 1.0
    coef_re = (nr * lr + ab_im * li) / den
    coef_im = (ab_im * lr - nr * li) / den
    cr = coef_re[:, None, :]
    ci = coef_im[:, None, :]
    bb_re = cr * bt_re - ci * bt_im
    bb_im = cr * bt_im + ci * bt_re
    return ab_re, ab_im, bb_re, bb_im


def s5_param_fwd(lr, li, logdt, bt_re, bt_im):
    def body(lr_ref, li_ref, ld_ref, br_ref, bi_ref, bbr_ref, bbi_ref):
        _, _, bb_re, bb_im = _s5_discretise(lr_ref[...], li_ref[...], ld_ref[...], br_ref[...], bi_ref[...])
        bbr_ref[...] = bb_re
        bbi_ref[...] = bb_im

    sh = jax.ShapeDtypeStruct(bt_re.shape, F32)
    return pl.pallas_call(body, name="s5_param_fwd", out_shape=[sh, sh])(lr, li, logdt, bt_re, bt_im)


def s5_param_bwd(lr, li, logdt, bt_re, bt_im, d_ab_re, d_ab_im, d_bb_re, d_bb_im):
    def body(lr_ref, li_ref, ld_ref, br_ref, bi_ref, dar_ref, dai_ref, dbr_ref, dbi_ref,
             o_lr, o_li, o_ld, o_br, o_bi):
        _, vjp = jax.vjp(_s5_discretise, lr_ref[...], li_ref[...], ld_ref[...], br_ref[...], bi_ref[...])
        g = vjp((dar_ref[...], dai_ref[...], dbr_ref[...], dbi_ref[...]))
        for o, v in zip((o_lr, o_li, o_ld), g[:3]):
            o[...] = v
        for o, v in zip((o_br, o_bi), g[3:]):
            for c in range(S5_GROUP):
                o[:, c * S5_STATE:(c + 1) * S5_STATE] = v[:, c, :]

    dense = jax.ShapeDtypeStruct((S5_GROUPS, S5_GROUP * S5_STATE), F32)
    shapes = [jax.ShapeDtypeStruct(a.shape, F32) for a in (lr, li, logdt)] + [dense, dense]
    return pl.pallas_call(body, name="s5_param_bwd", out_shape=shapes)(
        lr, li, logdt, bt_re, bt_im, d_ab_re, d_ab_im, d_bb_re, d_bb_im)


def s5_tables(lr_flat, li_flat, logdt_flat):
    def body(lr_ref, li_ref, ld_ref, tab_ref):
        dt = jnp.exp(ld_ref[...])
        a = lr_ref[...] * dt
        th = li_ref[...] * dt
        row = lax.broadcasted_iota(jnp.int32, (8, 1), 0)
        rowf = row.astype(F32)

        def power(e, sign):
            m = jnp.exp(e * a)
            return m * jnp.cos(e * th), sign * m * jnp.sin(e * th)

        k = 0
        for sign, fwd in ((1.0, True), (-1.0, False)):
            for s in (1, 2, 4):
                pr, pi = power(jnp.full((8, 1), float(s), F32), sign)
                keep = (row >= s) if fwd else (row + s < 8)
                tab_ref[k] = jnp.where(keep, pr, 0.0)
                tab_ref[k + 1] = jnp.where(keep, pi, 0.0)
                k += 2
            e = rowf + 1.0 if fwd else 8.0 - rowf
            pr, pi = power(e, sign)
            tab_ref[k] = pr
            tab_ref[k + 1] = pi
            k += 2

    return pl.pallas_call(body, name="s5_tables",
                          out_shape=jax.ShapeDtypeStruct((16, 8, S5_COLS), F32))(lr_flat, li_flat, logdt_flat)


def _scan_block(a, b, tabs, base, cr, ci, reverse):
    for n, s in enumerate((1, 2, 4)):
        mr = tabs[base + 2 * n]
        mi = tabs[base + 2 * n + 1]
        sh = (8 - s) if reverse else s
        ar = pltpu.roll(a, sh, 0)
        br = pltpu.roll(b, sh, 0)
        a, b = a + mr * ar - mi * br, b + mr * br + mi * ar
    pr = tabs[base + 6]
    pi = tabs[base + 7]
    a, b = a + pr * cr - pi * ci, b + pr * ci + pi * cr
    return a, b


class Carried:
    def __init__(self, arrays, out_shapes, sems, start, finish):
        self.arrays, self.out_shapes, self.sems = list(arrays), list(out_shapes), list(sems)
        self.start, self.finish = start, finish

    def split(self, refs, n_in, n_out, n_scratch):
        a, o, s = len(self.arrays), len(self.out_shapes), len(self.sems)
        own_in, car_in = refs[:n_in], refs[n_in:n_in + a]
        own_out, car_out = refs[n_in + a:n_in + a + n_out], refs[n_in + a + n_out:n_in + a + n_out + o]
        rest = refs[n_in + a + n_out + o:]
        return own_in + own_out + rest[:n_scratch], (car_in, car_out, rest[n_scratch:n_scratch + s])

    def hooks(self, parts, grid):
        first = last = None
        for k, n in enumerate(grid):
            i = pl.program_id(k)
            first = (i == 0) if first is None else first & (i == 0)
            last = (i == n - 1) if last is None else last & (i == n - 1)

        def top():
            pl.when(first)(lambda: self.start(*parts))

        def end():
            pl.when(last)(lambda: self.finish(*parts))

        return top, end


def s5_fwd(z, bbd_re, bbd_im, ccd_re, ccd_im, tab, dskip, tm=S5_TILE, carried=None):
    L = z.shape[0]
    tm = min(tm, L)
    nt = L // tm

    def body(*refs):
        top = end = None
        if carried is not None:
            refs, parts = carried.split(refs, 7, 4, 3)
            top, end = carried.hooks(parts, (S5_SPLIT, nt))
            top()
        u_ref, bbr_ref, bbi_ref, ccr_ref, cci_ref, tab_ref, d_ref, y_ref, ck_ref, hr_ref, hi_ref, xr, xi, car = refs
        t = pl.program_id(1)

        @pl.when(t == 0)
        def _():
            car[...] = jnp.zeros_like(car)

        ub = u_ref[...]
        u = ub.astype(F32)
        xr[...] = _dot(ub, bbr_ref[...])
        xi[...] = _dot(ub, bbi_ref[...])
        tabs = [tab_ref[k] for k in range(8)]

        def blk(i, c):
            r0 = pl.multiple_of(i * 8, 8)
            a, b = _scan_block(xr[pl.ds(r0, 8), :], xi[pl.ds(r0, 8), :], tabs, 0, c[0], c[1], False)
            xr[pl.ds(r0, 8), :] = a
            xi[pl.ds(r0, 8), :] = b
            return a[7:8, :], b[7:8, :]

        cr, ci = lax.fori_loop(0, tm // 8, blk, (car[0:1, :], car[1:2, :]))
        car[0:1, :] = cr
        car[1:2, :] = ci
        ck_ref[0:1, :] = cr
        ck_ref[1:2, :] = ci
        hrb = xr[...].astype(BF16)
        hib = xi[...].astype(BF16)
        hr_ref[...] = hrb
        hi_ref[...] = hib
        y_ref[...] = _dot(hrb, ccr_ref[...]) - _dot(hib, cci_ref[...]) + d_ref[...] * u
        if end is not None:
            end()

    extra = carried.arrays if carried is not None else []
    extra_out = carried.out_shapes if carried is not None else []
    extra_sems = carried.sems if carried is not None else []
    return pl.pallas_call(
        body, name="s5_fwd", grid=(S5_SPLIT, nt),
        in_specs=[pl.BlockSpec((tm, S5_UC), lambda j, t: (t, j)),
                  pl.BlockSpec((None, S5_UC, S5_CC), lambda j, t: (j, 0, 0)),
                  pl.BlockSpec((None, S5_UC, S5_CC), lambda j, t: (j, 0, 0)),
                  pl.BlockSpec((None, S5_CC, S5_UC), lambda j, t: (j, 0, 0)),
                  pl.BlockSpec((None, S5_CC, S5_UC), lambda j, t: (j, 0, 0)),
                  pl.BlockSpec((8, 8, S5_CC), lambda j, t: (0, 0, j)),
                  pl.BlockSpec((1, S5_UC), lambda j, t: (0, j))] + [ANY] * len(extra),
        out_specs=[pl.BlockSpec((tm, S5_UC), lambda j, t: (t, j)),
                   pl.BlockSpec((None, 2, S5_CC), lambda j, t: (t, 0, j)),
                   pl.BlockSpec((tm, S5_CC), lambda j, t: (t, j)),
                   pl.BlockSpec((tm, S5_CC), lambda j, t: (t, j))] + [ANY] * len(extra_out),
        out_shape=[jax.ShapeDtypeStruct((L, S5_WIDTH), F32), jax.ShapeDtypeStruct((nt, 2, S5_COLS), F32),
                   jax.ShapeDtypeStruct((L, S5_COLS), BF16), jax.ShapeDtypeStruct((L, S5_COLS), BF16)] + extra_out,
        scratch_shapes=[pltpu.VMEM((tm, S5_CC), F32), pltpu.VMEM((tm, S5_CC), F32), pltpu.VMEM((2, S5_CC), F32)]
        + extra_sems,
        compiler_params=_cp("arbitrary" if carried is not None else "parallel", "arbitrary"),
    )(z, bbd_re, bbd_im, ccd_re, ccd_im, tab, dskip, *extra)


def s5_bwd(z, dy, dz, ckpt, hrb, hib, bbd_re, bbd_im, ccd_re, ccd_im, tab, dskip, tm=S5_TILE, carried=None):
    L = z.shape[0]
    tm = min(tm, L)
    nt = L // tm

    def body(*refs):
        top = end = None
        if carried is not None:
            refs, parts = carried.split(refs, 12, 7, 7)
            top, end = carried.hooks(parts, (S5_SPLIT, nt))
            top()
        (u_ref, dy_ref, dz_ref, ck_ref, hrb_ref, hib_ref, bbr_ref, bbi_ref, ccr_ref, cci_ref, tab_ref, d_ref,
         du_ref, da_ref, dbr_ref, dbi_ref, dcr_ref, dci_ref, dd_ref, hr, hi, gr, gi, car, acr, aci) = refs
        t = pl.program_id(1)
        tt = nt - 1 - t

        @pl.when(t == 0)
        def _():
            for r in (car, acr, aci, dbr_ref, dbi_ref, dcr_ref, dci_ref, dd_ref):
                r[...] = jnp.zeros_like(r)

        ub = u_ref[...]
        u = ub.astype(F32)
        dyv = dy_ref[...]
        dyb = dyv.astype(BF16)
        tabs = [None] * 8 + [tab_ref[k] for k in range(8, 16)]

        live = (tt > 0).astype(F32)
        hr[0:8, :] = jnp.broadcast_to(ck_ref[0:1, :] * live, (8, S5_CC))
        hi[0:8, :] = jnp.broadcast_to(ck_ref[1:2, :] * live, (8, S5_CC))
        hrb = hrb_ref[...]
        hib = hib_ref[...]
        hr[8:, :] = hrb.astype(F32)
        hi[8:, :] = hib.astype(F32)
        dcr_ref[...] += _dot_tn(hrb, dyb)
        dci_ref[...] -= _dot_tn(hib, dyb)

        gr[...] = _dot_nt(dyb, ccr_ref[...])
        gi[...] = -_dot_nt(dyb, cci_ref[...])
        row0 = lax.broadcasted_iota(jnp.int32, (8, S5_CC), 0) == 0

        def rblk(k, c):
            i = tm // 8 - 1 - k
            r0 = pl.multiple_of(i * 8, 8)
            a, b = _scan_block(gr[pl.ds(r0, 8), :], gi[pl.ds(r0, 8), :], tabs, 8, c[0], c[1], True)
            gr[pl.ds(r0, 8), :] = a
            gi[pl.ds(r0, 8), :] = b
            r1 = pl.multiple_of(i * 8 + 8, 8)
            hpr = jnp.where(row0, pltpu.roll(hr[pl.ds(r0, 8), :], 1, 0), pltpu.roll(hr[pl.ds(r1, 8), :], 1, 0))
            hpi = jnp.where(row0, pltpu.roll(hi[pl.ds(r0, 8), :], 1, 0), pltpu.roll(hi[pl.ds(r1, 8), :], 1, 0))
            acr[...] += a * hpr + b * hpi
            aci[...] += b * hpr - a * hpi
            return a[0:1, :], b[0:1, :]

        cr, ci = lax.fori_loop(0, tm // 8, rblk, (car[0:1, :], car[1:2, :]))
        car[0:1, :] = cr
        car[1:2, :] = ci

        grb = gr[...].astype(BF16)
        gib = gi[...].astype(BF16)
        du_ref[...] = (_dot_nt(grb, bbr_ref[...]) + _dot_nt(gib, bbi_ref[...]) + d_ref[...] * dyv).astype(BF16)
        dbr_ref[...] += _dot_tn(ub, grb)
        dbi_ref[...] += _dot_tn(ub, gib)
        dd_ref[...] += jnp.sum(dyv * u, axis=0, keepdims=True)

        @pl.when(t == nt - 1)
        def _():
            da_ref[0:1, :] = jnp.sum(acr[...], axis=0, keepdims=True)
            da_ref[1:2, :] = jnp.sum(aci[...], axis=0, keepdims=True)

        if end is not None:
            end()

    extra = carried.arrays if carried is not None else []
    extra_out = carried.out_shapes if carried is not None else []
    extra_sems = carried.sems if carried is not None else []
    chunk = lambda a, b: pl.BlockSpec((None, a, b), lambda j, t: (j, 0, 0))
    return pl.pallas_call(
        body, name="s5_bwd", grid=(S5_SPLIT, nt),
        in_specs=[pl.BlockSpec((tm, S5_UC), lambda j, t: (nt - 1 - t, j)),
                  pl.BlockSpec((tm, S5_UC), lambda j, t: (nt - 1 - t, j)),
                  ANY,
                  pl.BlockSpec((None, 2, S5_CC), lambda j, t: (jnp.maximum(nt - 2 - t, 0), 0, j)),
                  pl.BlockSpec((tm, S5_CC), lambda j, t: (nt - 1 - t, j)),
                  pl.BlockSpec((tm, S5_CC), lambda j, t: (nt - 1 - t, j)),
                  chunk(S5_UC, S5_CC), chunk(S5_UC, S5_CC), chunk(S5_CC, S5_UC), chunk(S5_CC, S5_UC),
                  pl.BlockSpec((16, 8, S5_CC), lambda j, t: (0, 0, j)),
                  pl.BlockSpec((1, S5_UC), lambda j, t: (0, j))] + [ANY] * len(extra),
        out_specs=[pl.BlockSpec((tm, S5_UC), lambda j, t: (nt - 1 - t, j)),
                   pl.BlockSpec((None, 2, S5_CC), lambda j, t: (j, 0, 0)),
                   chunk(S5_UC, S5_CC), chunk(S5_UC, S5_CC), chunk(S5_CC, S5_UC), chunk(S5_CC, S5_UC),
                   pl.BlockSpec((1, S5_UC), lambda j, t: (0, j))] + [ANY] * len(extra_out),
        out_shape=[jax.ShapeDtypeStruct(dz.shape, dz.dtype),
                   jax.ShapeDtypeStruct((S5_SPLIT, 2, S5_CC), F32),
                   jax.ShapeDtypeStruct((S5_SPLIT, S5_UC, S5_CC), F32),
                   jax.ShapeDtypeStruct((S5_SPLIT, S5_UC, S5_CC), F32),
                   jax.ShapeDtypeStruct((S5_SPLIT, S5_CC, S5_UC), F32),
                   jax.ShapeDtypeStruct((S5_SPLIT, S5_CC, S5_UC), F32),
                   jax.ShapeDtypeStruct((1, S5_WIDTH), F32)] + extra_out,
        scratch_shapes=[pltpu.VMEM((tm + 8, S5_CC), F32), pltpu.VMEM((tm + 8, S5_CC), F32),
                        pltpu.VMEM((tm, S5_CC), F32), pltpu.VMEM((tm, S5_CC), F32),
                        pltpu.VMEM((2, S5_CC), F32), pltpu.VMEM((8, S5_CC), F32), pltpu.VMEM((8, S5_CC), F32)]
        + extra_sems,
        input_output_aliases={2: 0},
        compiler_params=_cp("arbitrary" if carried is not None else "parallel", "arbitrary"),
    )(z, dy, dz, ckpt, hrb, hib, bbd_re, bbd_im, ccd_re, ccd_im, tab, dskip, *extra)


_EYE8 = np.eye(S5_GROUPS // S5_SPLIT, dtype=np.float32)


def _blockdiag(a):
    g, r, c = a.shape
    a = a.reshape(S5_SPLIT, g // S5_SPLIT, r, c)
    out = a[:, :, :, None, :] * _EYE8[None, :, None, :, None].astype(a.dtype)
    return out.reshape(S5_SPLIT, (g // S5_SPLIT) * r, (g // S5_SPLIT) * c)


def _blockdiag_extract(a, r, c):
    n = S5_GROUPS // S5_SPLIT
    a = a.reshape(S5_SPLIT, n, r, n, c)
    d = jnp.stack([a[:, k, :, k, :] for k in range(n)], axis=1)
    return d.reshape(S5_GROUPS, r, c)


def s5_mixer_core_fwd(z, lam_re, lam_im, log_dt, b_re, b_im, c_re, c_im, d_skip, carried=None):
    bt_re = jnp.swapaxes(b_re, 1, 2)
    bt_im = jnp.swapaxes(b_im, 1, 2)
    logdt = log_dt.reshape(S5_GROUPS, 1)
    bb_re, bb_im = s5_param_fwd(lam_re, lam_im, logdt, bt_re, bt_im)
    flat = lambda a: a.reshape(1, S5_COLS)
    tab = s5_tables(flat(lam_re), flat(lam_im), flat(jnp.broadcast_to(logdt, (S5_GROUPS, S5_STATE))))
    bbd_re = _blockdiag(bb_re).astype(BF16)
    bbd_im = _blockdiag(bb_im).astype(BF16)
    ccd_re = _blockdiag(jnp.swapaxes(c_re, 1, 2)).astype(BF16)
    ccd_im = _blockdiag(jnp.swapaxes(c_im, 1, 2)).astype(BF16)
    dsk = d_skip.reshape(1, S5_WIDTH)
    y, ckpt, hrb, hib, *landed = s5_fwd(z, bbd_re, bbd_im, ccd_re, ccd_im, tab, dsk, carried=carried)
    saved = (logdt, bt_re, bt_im, bbd_re, bbd_im, ccd_re, ccd_im, tab, dsk, ckpt, hrb, hib)
    return y, saved, landed


def s5_b_from_dense(dense):
    return jnp.swapaxes(dense.reshape(S5_GROUPS, S5_GROUP, S5_STATE), 1, 2)


def s5_mixer_core_bwd(z, dy, dz, lam_re, lam_im, saved, carried=None):
    logdt, bt_re, bt_im, bbd_re, bbd_im, ccd_re, ccd_im, tab, dsk, ckpt, hrb, hib = saved
    dz, da, dbr, dbi, dcr, dci, dd, *landed = s5_bwd(z, dy, dz, ckpt, hrb, hib, bbd_re, bbd_im, ccd_re, ccd_im, tab,
                                                     dsk, carried=carried)
    d_ab_re = da[:, 0, :].reshape(S5_GROUPS, S5_STATE)
    d_ab_im = da[:, 1, :].reshape(S5_GROUPS, S5_STATE)
    d_bb_re = _blockdiag_extract(dbr, S5_GROUP, S5_STATE)
    d_bb_im = _blockdiag_extract(dbi, S5_GROUP, S5_STATE)
    g_lr, g_li, g_ld, g_btr, g_bti = s5_param_bwd(lam_re, lam_im, logdt, bt_re, bt_im,
                                                  d_ab_re, d_ab_im, d_bb_re, d_bb_im)
    g_cre = jnp.swapaxes(_blockdiag_extract(dcr, S5_STATE, S5_GROUP), 1, 2)
    g_cim = jnp.swapaxes(_blockdiag_extract(dci, S5_STATE, S5_GROUP), 1, 2)
    grads = dict(lambda_re=g_lr, lambda_im=g_li, log_dt=g_ld.reshape(S5_GROUPS), b_re=g_btr, b_im=g_bti,
                 c_re=g_cre, c_im=g_cim, d=dd.reshape(S5_WIDTH))
    return dz, grads, landed


Z_U, Z_GA, Z_VAL, Z_GLU, Z_GB = range(5)
SUBLANES = 8


def _shifted_copies(buf, tm):
    n = tm + CONV_HALO - SUBLANES
    for r in range(1, SUBLANES):
        buf[r, 0:n, :] = buf[0, pl.ds(r, n), :]


CONV_ROWS = 32


def _shifted_rows(buf, start, rows, base=0):
    return buf[start % SUBLANES, pl.ds(base + (start - start % SUBLANES), rows), :]


def conv_fwd(z, conv_w, conv_b, tm=ROW_TILE):
    L = z.shape[0]
    tm = min(tm, L)
    nt = L // tm
    hb = tm // CONV_HALO
    C = CONV_WIDTH

    def body(val_ref, glu_ref, valh_ref, gluh_ref, w_ref, b_ref, c_ref, vsh):
        live = (pl.program_id(0) > 0).astype(F32)
        vsh[0, 0:CONV_HALO, :] = valh_ref[...].astype(F32) * _sigmoid(gluh_ref[...].astype(F32)) * live
        vsh[0, CONV_HALO:, :] = val_ref[...].astype(F32) * _sigmoid(glu_ref[...].astype(F32))
        _shifted_copies(vsh, tm)

        def rows(i, carry):
            base = pl.multiple_of(i * CONV_ROWS, CONV_ROWS)
            acc = jnp.broadcast_to(b_ref[...], (CONV_ROWS, C))
            for k in range(CONV_KERNEL):
                acc = acc + w_ref[k:k + 1, :] * _shifted_rows(vsh, CONV_HALO - CONV_KERNEL + 1 + k, CONV_ROWS, base)
            c_ref[pl.ds(base, CONV_ROWS), :] = acc
            return carry

        lax.fori_loop(0, tm // CONV_ROWS, rows, 0)

    cur = lambda col: pl.BlockSpec((tm, C), lambda t: (t, col))
    prev = lambda col: pl.BlockSpec((CONV_HALO, C), lambda t: (jnp.maximum(t * hb - 1, 0), col))
    return pl.pallas_call(
        body, name="conv_fwd", grid=(nt,),
        in_specs=[cur(Z_VAL), cur(Z_GLU), prev(Z_VAL), prev(Z_GLU), _full(conv_w.shape), _full(conv_b.shape)],
        out_specs=pl.BlockSpec((tm, C), lambda t: (t, 0)),
        out_shape=jax.ShapeDtypeStruct((L, C), F32),
        scratch_shapes=[pltpu.VMEM((8, tm + CONV_HALO, C), F32)],
        compiler_params=_cp("parallel"),
    )(z, z, z, z, conv_w, conv_b)


def conv_bwd(z, dc, dz, conv_w, tm=ROW_TILE, carried=None):
    L = z.shape[0]
    tm = min(tm, L)
    nt = L // tm
    hb = tm // CONV_HALO
    nh = L // CONV_HALO
    C = CONV_WIDTH
    off = CONV_HALO - CONV_KERNEL + 1

    def body(*refs):
        top = end = None
        if carried is not None:
            refs, parts = carried.split(refs, 8, 3, 3)
            top, end = carried.hooks(parts, (nt,))
            top()
        val_ref, glu_ref, valh_ref, gluh_ref, dc_ref, dcn_ref, dz_ref, w_ref, dvg_ref, dw_ref, db_ref, vsh, dsh, wacc = refs
        t = pl.program_id(0)

        @pl.when(t == 0)
        def _():
            wacc[...] = jnp.zeros_like(wacc)
            db_ref[...] = jnp.zeros_like(db_ref)

        val = val_ref[...].astype(F32)
        sg = _sigmoid(glu_ref[...].astype(F32))
        vsh[0, 0:CONV_HALO, :] = (valh_ref[...].astype(F32) * _sigmoid(gluh_ref[...].astype(F32))
                                  * (t > 0).astype(F32))
        vsh[0, CONV_HALO:, :] = val * sg
        dcv = dc_ref[...]
        dsh[0, 0:tm, :] = dcv
        dsh[0, tm:, :] = dcn_ref[...] * (t < nt - 1).astype(F32)
        _shifted_copies(vsh, tm)
        _shifted_copies(dsh, tm)

        def rows(i, carry):
            base = pl.multiple_of(i * CONV_ROWS, CONV_ROWS)
            dcr = dc_ref[pl.ds(base, CONV_ROWS), :]
            dv = jnp.zeros((CONV_ROWS, C), F32)
            for k in range(CONV_KERNEL):
                dv = dv + w_ref[k:k + 1, :] * _shifted_rows(dsh, CONV_KERNEL - 1 - k, CONV_ROWS, base)
                prod = dcr * _shifted_rows(vsh, off + k, CONV_ROWS, base)
                wacc[k] += jnp.sum(prod.reshape(CONV_ROWS // SUBLANES, SUBLANES, C), axis=0)
            valr = val_ref[pl.ds(base, CONV_ROWS), :].astype(F32)
            sgr = _sigmoid(glu_ref[pl.ds(base, CONV_ROWS), :].astype(F32))
            dvg_ref[pl.ds(base, CONV_ROWS), 0:C] = (dv * sgr).astype(BF16)
            dvg_ref[pl.ds(base, CONV_ROWS), C:] = (dv * valr * sgr * (1.0 - sgr)).astype(BF16)
            return carry

        lax.fori_loop(0, tm // CONV_ROWS, rows, 0)
        db_ref[...] += jnp.sum(dcv, axis=0, keepdims=True)

        @pl.when(t == nt - 1)
        def _():
            dw_ref[...] = jnp.sum(wacc[...], axis=1)

        if end is not None:
            end()

    extra = carried.arrays if carried is not None else []
    extra_out = carried.out_shapes if carried is not None else []
    extra_sems = carried.sems if carried is not None else []
    cur = lambda col: pl.BlockSpec((tm, C), lambda t: (t, col))
    prev = lambda col: pl.BlockSpec((CONV_HALO, C), lambda t: (jnp.maximum(t * hb - 1, 0), col))
    nxt = pl.BlockSpec((CONV_HALO, C), lambda t: (jnp.minimum((t + 1) * hb, nh - 1), 0))
    row = pl.BlockSpec((tm, C), lambda t: (t, 0))
    return pl.pallas_call(
        body, name="conv_bwd", grid=(nt,),
        in_specs=[cur(Z_VAL), cur(Z_GLU), prev(Z_VAL), prev(Z_GLU), row, nxt, ANY, _full(conv_w.shape)]
        + [ANY] * len(extra),
        out_specs=[pl.BlockSpec((tm, 2 * C), lambda t: (t, 1)), _full((CONV_HALO, C)), _full((1, C))]
        + [ANY] * len(extra_out),
        out_shape=[jax.ShapeDtypeStruct(dz.shape, dz.dtype),
                   jax.ShapeDtypeStruct((CONV_HALO, C), F32), jax.ShapeDtypeStruct((1, C), F32)] + extra_out,
        scratch_shapes=[pltpu.VMEM((8, tm + CONV_HALO, C), F32), pltpu.VMEM((8, tm + CONV_HALO, C), F32),
                        pltpu.VMEM((CONV_HALO, SUBLANES, C), F32)] + extra_sems,
        input_output_aliases={6: 0},
        compiler_params=_cp("arbitrary"),
    )(z, z, z, z, dc, dc, dz, conv_w, *extra)


def _ln_parts(c):
    mu = jnp.mean(c, axis=-1, keepdims=True)
    cc = c - mu
    rstd = lax.rsqrt(jnp.mean(cc * cc, axis=-1, keepdims=True) + EPS)
    return rstd, cc * rstd


def _ev_tail_branches(ys, c, wglu, bglu, lng, lnb):
    z1 = _gelu(ys)
    z1b = z1.astype(BF16)
    sg = _sigmoid(_dot_rows(z1b, wglu) + bglu)
    out = z1 * sg
    rstd, chat = _ln_parts(c)
    cn = chat * lng + lnb
    return z1, z1b, sg, out, rstd, chat, cn


def ev_tail_fwd(ys, z, c, x0, wglu, bglu, lng, lnb, wout, tm=ROW_TILE):
    L, D = x0.shape
    tm = min(tm, L)
    W = S5_WIDTH

    def body(ys_ref, ga_ref, c_ref, gb_ref, x_ref, wglu_ref, bglu_ref, lng_ref, lnb_ref, wout_ref, o_ref):
        _, _, _, out, _, _, cn = _ev_tail_branches(ys_ref[...], c_ref[...], wglu_ref, bglu_ref[...],
                                                   lng_ref[...], lnb_ref[...])
        ya = (out * _silu(ga_ref[...].astype(F32))).astype(BF16)
        yb = (_silu(cn) * _silu(gb_ref[...].astype(F32))).astype(BF16)
        o_ref[...] = x_ref[...] + _dot_rows(jnp.concatenate([ya, yb], axis=1), wout_ref)

    row = lambda n, col=0: pl.BlockSpec((tm, n), lambda t: (t, col))
    return pl.pallas_call(
        body, name="ev_tail_fwd", grid=(L // tm,),
        in_specs=[row(W), row(W, Z_GA), row(W), row(W, Z_GB), row(D), _full(wglu.shape), _full(bglu.shape),
                  _full(lng.shape), _full(lnb.shape), _full(wout.shape)],
        out_specs=row(D), out_shape=jax.ShapeDtypeStruct((L, D), F32), compiler_params=_cp("parallel"),
    )(ys, z, c, z, x0, wglu, bglu, lng, lnb, wout)


def ev_tail_bwd(ys, z, c, dx1, wglu, bglu, lng, lnb, wout, tm=ROW_TILE):
    L, D = dx1.shape
    tm = min(tm, L)
    W = S5_WIDTH

    def body(ys_ref, ga_ref, c_ref, gb_ref, dx_ref, wglu_ref, bglu_ref, lng_ref, lnb_ref, wout_ref,
             dys_ref, dc_ref, dz_ref, r_ref, z1_ref, dt_ref, dbg_ref, dlg_ref, dlb_ref):
        @pl.when(pl.program_id(0) == 0)
        def _():
            for r in (dbg_ref, dlg_ref, dlb_ref):
                r[...] = jnp.zeros_like(r)

        ys, ga, gb = ys_ref[...], ga_ref[...].astype(F32), gb_ref[...].astype(F32)
        z1, z1b, sg, out, rstd, chat, cn = _ev_tail_branches(ys, c_ref[...], wglu_ref, bglu_ref[...],
                                                             lng_ref[...], lnb_ref[...])
        (sga, dsga), (sgb, dsgb), (scn, dscn) = _silu_pair(ga), _silu_pair(gb), _silu_pair(cn)
        r_ref[:, 0:W] = (out * sga).astype(BF16)
        r_ref[:, W:] = (scn * sgb).astype(BF16)
        dr = _dot_nt_rows(dx_ref[...].astype(BF16), wout_ref)
        dra, drb = dr[:, 0:W], dr[:, W:]
        dz_ref[...] = jnp.zeros_like(dz_ref)
        dz_ref[:, Z_GA * W:(Z_GA + 1) * W] = (dra * out * dsga).astype(BF16)
        dout = dra * sga
        dt = dout * z1 * sg * (1.0 - sg)
        dtb = dt.astype(BF16)
        dz1 = dout * sg + _dot_nt_rows(dtb, wglu_ref)
        dys_ref[...] = dz1 * _dgelu(ys)
        z1_ref[...] = z1b
        dt_ref[...] = dtb
        dbg_ref[...] += jnp.sum(dt, axis=0, keepdims=True)
        dz_ref[:, Z_GB * W:(Z_GB + 1) * W] = (drb * scn * dsgb).astype(BF16)
        dcn = drb * sgb * dscn
        dlg_ref[...] += jnp.sum(dcn * chat, axis=0, keepdims=True)
        dlb_ref[...] += jnp.sum(dcn, axis=0, keepdims=True)
        dch = dcn * lng_ref[...]
        dc_ref[...] = rstd * (dch - jnp.mean(dch, axis=-1, keepdims=True)
                              - chat * jnp.mean(dch * chat, axis=-1, keepdims=True))

    row = lambda n, col=0: pl.BlockSpec((tm, n), lambda t: (t, col))
    f = lambda n, dt: jax.ShapeDtypeStruct((L, n), dt)
    vec = jax.ShapeDtypeStruct((1, W), F32)
    return pl.pallas_call(
        body, name="ev_tail_bwd", grid=(L // tm,),
        in_specs=[row(W), row(W, Z_GA), row(W), row(W, Z_GB), row(D), _full(wglu.shape), _full(bglu.shape),
                  _full(lng.shape), _full(lnb.shape), _full(wout.shape)],
        out_specs=[row(W), row(W), row(EVEN_IN), row(D), row(W), row(W), _full((1, W)), _full((1, W)), _full((1, W))],
        out_shape=[f(W, F32), f(W, F32), f(EVEN_IN, BF16), f(D, BF16), f(W, BF16), f(W, BF16), vec, vec, vec],
        compiler_params=_cp("arbitrary"),
    )(ys, z, c, z, dx1, wglu, bglu, lng, lnb, wout)


XA_SCALE = XA_HEAD_DIM ** -0.5


def _xa_forward(xv, g, wqg, kv):
    D = D_MODEL
    _, xhat = _rms_parts(xv)
    hb = (xhat * g).astype(BF16)
    qb = (_dot_cols(hb, wqg, (0, 1)) * XA_SCALE).astype(BF16)
    gate = _dot_cols(hb, wqg, (2, 3))
    ps, os_ = [], []
    for h in range(XA_HEADS):
        lo, hi = h * XA_HEAD_DIM, (h + 1) * XA_HEAD_DIM
        s = _dot_nt(qb[:, lo:hi], kv[:, lo:hi])
        e = jnp.exp(s - jnp.max(s, axis=-1, keepdims=True))
        inv = 1.0 / jnp.sum(e, axis=-1, keepdims=True)
        ps.append((e, inv))
        os_.append(_dot(e.astype(BF16), kv[:, D + lo:D + hi]) * inv)
    return hb, qb, gate, ps, jnp.concatenate(os_, axis=1)


def xa_fwd(x, g, wqg, kv, wo, layer, name, tm=MM_TILE):
    L, D = x.shape
    tm = min(tm, L)

    def body(x_ref, g_ref, wqg_ref, kv_ref, wo_ref, o_ref):
        xv = x_ref[...]
        _, _, gate, _, o = _xa_forward(xv, g_ref[...], wqg_ref, kv_ref[...])
        o_ref[...] = xv + _dot_rows((o * _silu(gate)).astype(BF16), wo_ref)

    row = pl.BlockSpec((tm, D), lambda t: (t, 0))
    return pl.pallas_call(
        body, name=name, grid=(L // tm,),
        in_specs=[row, _full(g.shape), _wspec(wqg, layer), _full(kv.shape), _wspec(wo, layer)],
        out_specs=row, out_shape=jax.ShapeDtypeStruct((L, D), F32), compiler_params=_cp("parallel"),
    )(x, g, wqg, kv, wo)


def _loss_head(xv, gv, tv):
    D = xv.shape[-1]
    _, xhat = _rms_parts(xv)
    err = xhat * gv - tv
    loss = 0.5 * jnp.sum(jnp.sum(err * err, axis=-1, keepdims=True), axis=0, keepdims=True) / D
    dx, dg = _rms_bwd(xv, gv, err * (1.0 / D))
    return loss, dx, dg


def xa_fwd_loss(x, g, wqg, kv, wo, layer, target, gf, name, tm=MM_TILE):
    L, D = x.shape
    tm = min(tm, L)

    def body(x_ref, g_ref, wqg_ref, kv_ref, wo_ref, t_ref, gf_ref, loss_ref, dx_ref, dg_ref):
        @pl.when(pl.program_id(0) == 0)
        def _():
            loss_ref[...] = jnp.zeros_like(loss_ref)
            dg_ref[...] = jnp.zeros_like(dg_ref)

        xv = x_ref[...]
        _, _, gate, _, o = _xa_forward(xv, g_ref[...], wqg_ref, kv_ref[...])
        y = xv + _dot_rows((o * _silu(gate)).astype(BF16), wo_ref)
        loss, dx, dg = _loss_head(y, gf_ref[...], t_ref[...])
        loss_ref[...] += loss
        dx_ref[...] = dx
        dg_ref[...] += dg

    row = pl.BlockSpec((tm, D), lambda t: (t, 0))
    return pl.pallas_call(
        body, name=name, grid=(L // tm,),
        in_specs=[row, _full(g.shape), _wspec(wqg, layer), _full(kv.shape), _wspec(wo, layer), row, _full(gf.shape)],
        out_specs=[_full((1, 128)), row, _full((1, D))],
        out_shape=[jax.ShapeDtypeStruct((1, 128), F32), jax.ShapeDtypeStruct((L, D), F32),
                   jax.ShapeDtypeStruct((1, D), F32)],
        compiler_params=_cp("arbitrary"),
    )(x, g, wqg, kv, wo, target, gf)


def xa_bwd(x, dxo, g, wqg, kv, wo, layer, name, tm=MM_TILE):
    L, D = x.shape
    tm = min(tm, L)

    def body(x_ref, dxo_ref, g_ref, wqg_ref, kv_ref, wo_ref, dx_ref, dqg_ref, h_ref, r_ref, dkv_ref, dg_ref):
        @pl.when(pl.program_id(0) == 0)
        def _():
            dkv_ref[...] = jnp.zeros_like(dkv_ref)
            dg_ref[...] = jnp.zeros_like(dg_ref)

        xv = x_ref[...]
        kv = kv_ref[...]
        hb, qb, gate, ps, o = _xa_forward(xv, g_ref[...], wqg_ref, kv)
        sgate, dsgate = _silu_pair(gate)
        h_ref[...] = hb
        r_ref[...] = (o * sgate).astype(BF16)
        dxo = dxo_ref[...]
        dr = _dot_nt_rows(dxo.astype(BF16), wo_ref)
        do = dr * sgate
        dqg_ref[:, D:] = (dr * o * dsgate).astype(BF16)
        dob = do.astype(BF16)
        doo = do * o
        for h in range(XA_HEADS):
            lo, hi = h * XA_HEAD_DIM, (h + 1) * XA_HEAD_DIM
            e, inv = ps[h]
            dp = _dot_nt(dob[:, lo:hi], kv[:, D + lo:D + hi])
            dkv_ref[:, D + lo:D + hi] += _dot_tn(e.astype(BF16), (do[:, lo:hi] * inv).astype(BF16))
            rs = jnp.sum(doo[:, lo:hi], axis=-1, keepdims=True)
            dsb = (e * ((dp - rs) * inv)).astype(BF16)
            dqg_ref[:, lo:hi] = (_dot(dsb, kv[:, lo:hi]) * XA_SCALE).astype(BF16)
            dkv_ref[:, lo:hi] += _dot_tn(dsb, qb[:, lo:hi])
        dh = _dot_nt_cols(_col_pieces(dqg_ref[...], D // 2), wqg_ref)
        dx, dg = _rms_bwd(xv, g_ref[...], dh)
        dx_ref[...] = dxo + dx
        dg_ref[...] += dg

    row = lambda n: pl.BlockSpec((tm, n), lambda t: (t, 0))
    return pl.pallas_call(
        body, name=name, grid=(L // tm,),
        in_specs=[row(D), row(D), _full(g.shape), _wspec(wqg, layer), _full(kv.shape), _wspec(wo, layer)],
        out_specs=[row(D), row(2 * D), row(D), row(D), _full(kv.shape), _full((1, D))],
        out_shape=[jax.ShapeDtypeStruct((L, D), F32), jax.ShapeDtypeStruct((L, 2 * D), BF16),
                   jax.ShapeDtypeStruct((L, D), BF16), jax.ShapeDtypeStruct((L, D), BF16),
                   jax.ShapeDtypeStruct(kv.shape, F32), jax.ShapeDtypeStruct((1, D), F32)],
        compiler_params=_cp("arbitrary"),
    )(x, dxo, g, wqg, kv, wo)


ATT_SCALE = ATT_HEAD_DIM ** -0.5
ATT_PAIRS = ATT_HEADS // 2
SKEW_LANES = 1024
REL_LANES = 384


def _skew(x, left):
    amt = (ATT_QB - 1) - lax.broadcasted_iota(jnp.int32, (ATT_QB, 1), 0)
    for bit in range(8):
        sh = (SKEW_LANES - (1 << bit)) if left else (1 << bit)
        x = jnp.where(((amt >> bit) & 1) == 1, pltpu.roll(x, sh, 1), x)
    return x


def _dist_onehot(shape, dist_axis):
    j = lax.broadcasted_iota(jnp.int32, shape, dist_axis)
    r = lax.broadcasted_iota(jnp.int32, shape, 1 - dist_axis)
    return (jnp.clip((ATT_WIN - 1) - j, -MAX_REL, MAX_REL) + MAX_REL == r).astype(BF16)


def _dot_exact(v, onehot):
    acc = jnp.zeros((v.shape[0], onehot.shape[1]), F32)
    rem = v
    for _ in range(3):
        part = rem.astype(BF16)
        acc = acc + _dot(part, onehot)
        rem = rem - part.astype(F32)
    return acc


ATT_EDGE = ATT_PAD // ATT_QB


def att_bias(rel_bias, carried=None):
    H = rel_bias.shape[0]
    rb = jnp.pad(rel_bias, ((0, 0), (0, REL_LANES - rel_bias.shape[1]))).reshape(H, 1, REL_LANES)

    def body(*refs):
        top = end = None
        if carried is not None:
            refs, parts = carried.split(refs, 1, 1, 0)
            top, end = carried.hooks(parts, (H,))
            top()
        rb_ref, o_ref = refs
        by_col = _dot_exact(jnp.broadcast_to(rb_ref[...], (8, REL_LANES)), _dist_onehot((REL_LANES, SKEW_LANES), 1))
        x = _skew(jnp.broadcast_to(by_col[0:1, :], (ATT_QB, SKEW_LANES)), left=True)[:, 0:ATT_WIN]
        qc = lax.broadcasted_iota(jnp.int32, (ATT_QB, 1), 0) // CHUNK + LEFT_CHUNKS
        col = lax.broadcasted_iota(jnp.int32, (1, ATT_WIN), 1)
        dc = qc - col // CHUNK
        band = (dc >= 0) & (dc <= LEFT_CHUNKS)
        for blk in range(ATT_EDGE + 1):
            o_ref[blk] = jnp.where(band & (col >= ATT_PAD - blk * ATT_QB), x, NEG)
        if end is not None:
            end()

    extra = carried.arrays if carried is not None else []
    extra_out = carried.out_shapes if carried is not None else []
    extra_sems = carried.sems if carried is not None else []
    return pl.pallas_call(
        body, name="att_bias", grid=(H,),
        in_specs=[pl.BlockSpec((None, 1, REL_LANES), lambda h: (h, 0, 0))] + [ANY] * len(extra),
        out_specs=[pl.BlockSpec((ATT_EDGE + 1, None, ATT_QB, ATT_WIN), lambda h: (0, h, 0, 0))] + [ANY] * len(extra_out),
        out_shape=[jax.ShapeDtypeStruct((ATT_EDGE + 1, H, ATT_QB, ATT_WIN), F32)] + extra_out,
        scratch_shapes=extra_sems,
        compiler_params=_cp("arbitrary" if carried is not None else "parallel"),
    )(rb, *extra)


def relbias_bwd(dbias):
    H = dbias.shape[0]

    def body(x_ref, o_ref):
        x = jnp.concatenate([x_ref[...], jnp.zeros((ATT_QB, SKEW_LANES - ATT_WIN), F32)], axis=1)
        col = jnp.sum(_skew(x, left=False), axis=0, keepdims=True)
        o_ref[...] = _dot_exact(jnp.broadcast_to(col, (8, SKEW_LANES)), _dist_onehot((SKEW_LANES, REL_LANES), 0))

    out = pl.pallas_call(
        body, name="relbias_bwd", grid=(H,),
        in_specs=[pl.BlockSpec((None, ATT_QB, ATT_WIN), lambda h: (h, 0, 0))],
        out_specs=pl.BlockSpec((None, 8, REL_LANES), lambda h: (h, 0, 0)),
        out_shape=jax.ShapeDtypeStruct((H, 8, REL_LANES), F32), compiler_params=_cp("parallel"),
    )(dbias)
    return out[:, 0, :2 * MAX_REL + 1]


def _ca_scores(qh, kw, bias):
    s = _dot_nt(qh, kw) + bias
    e = jnp.exp(s - jnp.max(s, axis=-1, keepdims=True))
    return e, 1.0 / jnp.sum(e, axis=-1, keepdims=True)


def _ca_head(qv, m):
    return jnp.where(m, qv, jnp.zeros_like(qv)) * ATT_SCALE


def _ca_bias_spec():
    return pl.BlockSpec((None, 2, ATT_QB, ATT_WIN), lambda hp, b: (jnp.minimum(b, ATT_EDGE), hp, 0, 0))


def ca_fwd(q, kvp, gate, bias):
    L, D = q.shape
    Lp = kvp.shape[0]
    nb = L // ATT_QB

    PP = 2
    W = PP * 128

    def body(q_ref, k_ref, v_ref, g_ref, b_ref, r_ref, o_ref):
        w = pl.multiple_of(pl.program_id(1) * ATT_QB, ATT_QB)
        first = lax.broadcasted_iota(jnp.int32, (1, 128), 1) < ATT_HEAD_DIM
        for pp in range(PP):
            sl = slice(pp * 128, (pp + 1) * 128)
            kw = k_ref[pl.ds(w, ATT_WIN), sl]
            vw = v_ref[pl.ds(w, ATT_WIN), sl]
            qv = q_ref[:, sl]
            outs = []
            for hh, m in enumerate((first, jnp.logical_not(first))):
                e, inv = _ca_scores(_ca_head(qv, m), kw, b_ref[2 * pp + hh])
                outs.append(_dot(e.astype(BF16), vw) * inv)
            o = jnp.where(first, outs[0], outs[1])
            r_ref[:, sl] = (o * _silu(g_ref[:, sl])).astype(BF16)
            o_ref[:, sl] = o.astype(BF16)

    blk = pl.BlockSpec((ATT_QB, W), lambda hp, b: (b, hp))
    bias_blk = pl.BlockSpec((None, 2 * PP, ATT_QB, ATT_WIN), lambda hp, b: (jnp.minimum(b, ATT_EDGE), hp, 0, 0))
    return pl.pallas_call(
        body, name="ca_fwd", grid=(ATT_PAIRS // PP, nb),
        in_specs=[blk, pl.BlockSpec((Lp, W), lambda hp, b: (0, hp)),
                  pl.BlockSpec((Lp, W), lambda hp, b: (0, ATT_PAIRS // PP + hp)), blk, bias_blk],
        out_specs=[blk, blk], out_shape=[jax.ShapeDtypeStruct((L, D), BF16), jax.ShapeDtypeStruct((L, D), BF16)],
        compiler_params=_cp("parallel", "arbitrary"),
    )(q, kvp, kvp, gate, bias)


def ca_bwd(q, kvp, gate, bias, dr, o):
    L, D = q.shape
    Lp = kvp.shape[0]
    nb = L // ATT_QB

    def body(q_ref, k_ref, v_ref, g_ref, b_ref, dr_ref, o_ref, dq_ref, dg_ref, dkb_ref, dvb_ref, db_ref,
             dk_ref, dv_ref):
        b = pl.program_id(1)

        @pl.when(b == 0)
        def _():
            for r in (dk_ref, dv_ref, db_ref):
                r[...] = jnp.zeros_like(r)

        w = pl.multiple_of(b * ATT_QB, ATT_QB)
        kw = k_ref[pl.ds(w, ATT_WIN), :]
        vw = v_ref[pl.ds(w, ATT_WIN), :]
        qv = q_ref[...]
        gate_v = g_ref[...]
        drv = dr_ref[...].astype(F32)
        o = o_ref[...].astype(F32)
        sgate, dsgate = _silu_pair(gate_v)
        do = drv * sgate
        doo = do * o
        first = lax.broadcasted_iota(jnp.int32, (1, 128), 1) < ATT_HEAD_DIM
        dqs = []
        dkw = jnp.zeros((ATT_WIN, 128), F32)
        dvw = jnp.zeros((ATT_WIN, 128), F32)
        for hh, m in enumerate((first, jnp.logical_not(first))):
            qh = _ca_head(qv, m)
            e, inv = _ca_scores(qh, kw, b_ref[hh])
            eb = e.astype(BF16)
            doh = jnp.where(m, do, 0.0)
            dp = _dot_nt(doh.astype(BF16), vw)
            dvw = dvw + _dot_tn(eb, (doh * inv).astype(BF16))
            rs = jnp.sum(jnp.where(m, doo, 0.0), axis=-1, keepdims=True)
            ds = e * ((dp - rs) * inv)
            db_ref[hh] += ds
            dsb = ds.astype(BF16)
            dqs.append(_dot(dsb, kw))
            dkw = dkw + _dot_tn(dsb, qh)
        dg_ref[...] = (drv * o * dsgate).astype(BF16)
        dq_ref[...] = (jnp.where(first, dqs[0], dqs[1]) * ATT_SCALE).astype(BF16)
        dk_ref[pl.ds(w, ATT_WIN), :] += dkw
        dv_ref[pl.ds(w, ATT_WIN), :] += dvw

        @pl.when(b == nb - 1)
        def _():
            dkb_ref[...] = dk_ref[...].astype(BF16)
            dvb_ref[...] = dv_ref[...].astype(BF16)

    blk = pl.BlockSpec((ATT_QB, 128), lambda hp, b: (b, hp))
    kblk = pl.BlockSpec((Lp, 128), lambda hp, b: (0, hp))
    vblk = pl.BlockSpec((Lp, 128), lambda hp, b: (0, ATT_PAIRS + hp))
    bblk = pl.BlockSpec((2, ATT_QB, ATT_WIN), lambda hp, b: (hp, 0, 0))
    return pl.pallas_call(
        body, name="ca_bwd", grid=(ATT_PAIRS, nb),
        in_specs=[blk, kblk, vblk, blk, _ca_bias_spec(), blk, blk],
        out_specs=[blk, blk, kblk, kblk, bblk],
        out_shape=[jax.ShapeDtypeStruct((L, D), BF16), jax.ShapeDtypeStruct((L, D), BF16),
                   jax.ShapeDtypeStruct((Lp, D), BF16), jax.ShapeDtypeStruct((Lp, D), BF16),
                   jax.ShapeDtypeStruct(bias.shape[1:], F32)],
        scratch_shapes=[pltpu.VMEM((Lp, 128), F32), pltpu.VMEM((Lp, 128), F32)],
        compiler_params=_cp("parallel", "arbitrary"),
    )(q, kvp, kvp, gate, bias, dr, o)


_ADAM_C1 = 1.0 / (1.0 - ADAM_B1 ** ADAM_STEP)
_ADAM_C2 = 1.0 / (1.0 - ADAM_B2 ** ADAM_STEP)


def _adam_update(w, g, m, v):
    mn = ADAM_B1 * m + (1.0 - ADAM_B1) * g
    vn = ADAM_B2 * v + (1.0 - ADAM_B2) * g * g
    delta = -ADAM_LR * ((mn * _ADAM_C1) / (jnp.sqrt(vn * _ADAM_C2) + ADAM_EPS) + ADAM_WD * w)
    return delta, mn, vn


def adamw(w, g, m, v, name, tr=512):
    R, C = w.shape
    tr = min(tr, R)

    def body(w_ref, g_ref, m_ref, v_ref, d_ref, mo_ref, vo_ref):
        d_ref[...], mo_ref[...], vo_ref[...] = _adam_update(w_ref[...], g_ref[...], m_ref[...], v_ref[...])

    blk = pl.BlockSpec((tr, C), lambda i: (i, 0))
    sh = jax.ShapeDtypeStruct((R, C), F32)
    return pl.pallas_call(
        body, name=name, grid=(R // tr,), in_specs=[blk] * 4, out_specs=[blk] * 3,
        out_shape=[sh] * 3, compiler_params=_cp("parallel"),
    )(w, g, m, v)


def adamw_allreduce(gathered, w, m, v, shard, name, slot=None):
    R, C = w.shape
    sharded = slot is None and gathered.shape[2] != C

    def body(s_ref, ga_ref, w_ref, m_ref, v_ref, g_ref, d_ref, mo_ref, vo_ref):
        take = (lambda d: ga_ref[d]) if slot is None else (lambda d: ga_ref[d, slot:slot + R, 0:C])
        g = take(0)
        for d in range(1, N_DEV):
            g = g + take(d)
        g_ref[...] = g
        d_ref[...], mo_ref[...], vo_ref[...] = _adam_update(w_ref[...], g, m_ref[...], v_ref[...])

    blk = pl.BlockSpec((R, C), lambda i, s_ref: (0, 0))
    if slot is not None:
        gblk = pl.BlockSpec(gathered.shape, lambda i, s_ref: (0, 0, 0))
    else:
        gblk = pl.BlockSpec((N_DEV, R, C),
                            (lambda i, s_ref: (0, 0, s_ref[0])) if sharded else (lambda i, s_ref: (0, 0, 0)))
    sh = jax.ShapeDtypeStruct((R, C), F32)
    return pl.pallas_call(
        body, name=name,
        grid_spec=pltpu.PrefetchScalarGridSpec(num_scalar_prefetch=1, grid=(1,), in_specs=[gblk, blk, blk, blk],
                                               out_specs=[blk] * 4),
        out_shape=[sh] * 4, compiler_params=_cp("arbitrary"),
    )(shard, gathered, w, m, v)


LATE = ("ev_s5_glu_w", "ev_w_out", "od_w_in", "od_w_out", "xa_w_qg", "xa_w_kv", "xa_w_o")
EARLY_GRADS = ("od_w_in", "od_w_out", "xa_w_qg", "xa_w_kv", "xa_w_o", "ev_w_out", "ev_s5_glu_w")


def _reduce_to_chip(gs, names, core, tag):
    from_sibling = sibling_send_other_half(gs, "sibling_send_" + tag)
    return [sum_with_sibling(gi, ri, core, "sum_sibling_" + n) for n, gi, ri in zip(names, gs, from_sibling)]


def local_step(x, mem, target, p, gw, late, bias, place, core):
    row = lambda a: a.reshape(1, -1)
    D = D_MODEL
    L = x.shape[0]
    g, big = {}, {}
    gw = dict(gw)

    z, h0b = norm_mm(x, p["ev_norm_g"], gw["ev_w_in"], [((0, 1, 2, 3), BF16, 0)], "ev_in")
    ys, s5_saved, landed = s5_mixer_core_fwd(
        z, p["ev_s5_lambda_re"][0], p["ev_s5_lambda_im"][0], p["ev_s5_log_dt"][0], p["ev_s5_b_re"][0],
        p["ev_s5_b_im"][0], p["ev_s5_c_re"][0], p["ev_s5_c_im"][0], p["ev_s5_d"][0],
        carried=carried_allgather([late[n] for n in LATE]))
    for n, gth in zip(LATE, landed):
        rows = gth.shape[1]
        gw[n] = gth.reshape(N_CHIPS, 2, rows // 2, gth.shape[2]) if n.startswith("xa_") else gth
    memn_b = rms_fwd(mem, row(p["mem_norm_g"]), "mem_norm")
    kvs = [mm_cols(memn_b, gw["xa_w_kv"], l, f"xa_kv{l}", BF16) for l in range(2)]
    conv_w = p["ev_conv_w"][0]
    c = conv_fwd(z, conv_w, p["ev_conv_b"])
    tail = (gw["ev_s5_glu_w"], p["ev_s5_glu_b"], p["ev_conv_ln_g"], p["ev_conv_ln_b"], gw["ev_w_out"])
    x1 = ev_tail_fwd(ys, z, c, x, *tail)
    xa0 = (row(p["xa_norm_g"][0]), gw["xa_w_qg"], kvs[0], gw["xa_w_o"], 0)
    x2 = xa_fwd(x1, *xa0, "xa_fwd0")

    q, kvp, gate, h1b = norm_mm(x2, p["od_norm_g"], gw["od_w_in"],
                                [((0,), BF16, 0), ((1, 2), BF16, ATT_PAD), ((3,), F32, 0)], "od_in")
    kvp = zero_rows(kvp, ATT_PAD, "od_kv_pad")
    r, att_o = ca_fwd(q, kvp, gate, bias)
    x3 = mm_res(r, gw["od_w_out"], x2, "od_out")
    xa1 = (row(p["xa_norm_g"][1]), gw["xa_w_qg"], kvs[1], gw["xa_w_o"], 1)
    loss, dx4, dgf = xa_fwd_loss(x3, *xa1, target, row(p["final_norm_g"]), "xa_fwd1_loss")
    g["final_norm_g"] = dgf.reshape(D)

    dx3, dqg1, hx1, rx1, dkv1, dgxa1 = xa_bwd(x3, dx4, *xa1, "xa_bwd1")
    dwqg = mm_tn(hx1, dqg1, "xa_dwqg1", ("cols", 1), bn=2 * D)
    dwo = mm_tn(rx1, dx4, "xa_dwo1", ("rows", 1))

    big["od_w_out"] = mm_tn(r, dx3, "od_dwout", ("rows",))
    dr = mm_nt_rows(dx3, gw["od_w_out"], "od_out_bwd", BF16)
    dq, dgate, dkp, dvp, dbias = ca_bwd(q, kvp, gate, bias, dr, att_o)
    pieces, offs = (dq, dkp, dvp, dgate), (0, ATT_PAD, ATT_PAD, 0)
    dwin = None
    for s in range(N_CHIPS):
        dwin = mm_tn(h1b, pieces[s], f"od_dwin{s}", ("slab", s), into=dwin, b_off=offs[s],
                     bl=ATT_PAD if offs[s] else 1024)
    big["od_w_in"] = dwin
    dx2, dgod = mm_nt_normbwd(pieces, offs, gw["od_w_in"], x2, p["od_norm_g"], dx3, "od_in_bwd")
    g["od_norm_g"] = dgod
    g["od_rel_bias"] = relbias_bwd(dbias)[None]

    dx1, dqg0, hx0, rx0, dkv0, dgxa0 = xa_bwd(x1, dx2, *xa0, "xa_bwd0")
    big["xa_w_qg"] = mm_tn(hx0, dqg0, "xa_dwqg0", ("cols", 0), into=dwqg, bn=2 * D)
    big["xa_w_o"] = mm_tn(rx0, dx2, "xa_dwo0", ("rows", 0), into=dwo)
    g["xa_norm_g"] = jnp.concatenate([dgxa0, dgxa1], axis=0)

    dys, dc, dz, ra, z1b, dtb, dbglu, dlng, dlnb = ev_tail_bwd(ys, z, c, dx1, *tail)
    big["ev_w_out"] = mm_tn(ra, dx1, "ev_dwout", ("rows",))
    big["ev_s5_glu_w"] = mm_tn(z1b, dtb, "ev_dwglu", ("rows",))
    g["ev_s5_glu_b"], g["ev_conv_ln_g"], g["ev_conv_ln_b"] = dbglu, dlng, dlnb
    dwkv = mm_tn(memn_b, dkv1, "xa_dwkv1", ("cols", 1), bl=MEM_LEN)
    big["xa_w_kv"] = mm_tn(memn_b, dkv0, "xa_dwkv0", ("cols", 0), into=dwkv, bl=MEM_LEN)
    dmem0 = mm_nt_cols(dkv0, gw["xa_w_kv"], 0, "xa_kv_bwd0")
    dmem1 = mm_nt_cols(dkv1, gw["xa_w_kv"], 1, "xa_kv_bwd1")
    g["mem_norm_g"] = rms_dgain(mem, dmem0, dmem1, "mem_norm_bwd").reshape(D)

    shard_major = lambda t: t.reshape((-1,) + t.shape[-2:])
    gs = [shard_major(big[n]) for n in EARLY_GRADS]
    dz, dconvw, dconvb, *from_sibling = conv_bwd(z, dc, dz, conv_w, carried=carried_sibling_send(gs))
    g["ev_conv_w"] = dconvw[None, :CONV_KERNEL]
    g["ev_conv_b"] = dconvb
    chip_sums = [sum_with_sibling(gi, ri, core, "sum_sibling_" + n) for n, gi, ri in zip(EARLY_GRADS, gs, from_sibling)]
    dz, s5g, from_chips = s5_mixer_core_bwd(z, dys, dz, p["ev_s5_lambda_re"][0], p["ev_s5_lambda_im"][0], s5_saved,
                                            carried=carried_chips_exchange(chip_sums))
    reduced = {n: sum_chips(ci, ri, place, "sum_chips_" + n) for n, ci, ri in zip(EARLY_GRADS, chip_sums, from_chips)}
    for n, v in s5g.items():
        g["ev_s5_" + n] = v[None]
    packed, slots = pack_rows([_as2d(g[n]) for n in PACKED_SMALL], "pack_small_grads")
    dwin_ev, *gathered = mm_tn(h0b, dz, "ev_dwin", ("cols",),
                               carried=carried_allgather_devices([packed] + [_as2d(g[n]) for n in SINGLE_SMALL]))
    grad_x, dgev = mm_nt_normbwd((dz,), (0,), gw["ev_w_in"], x, p["ev_norm_g"], dx1, "ev_in_bwd")
    chip_sum = _reduce_to_chip([dwin_ev], ["ev_w_in"], core, "last")
    reduced["ev_w_in"] = sum_chips(chip_sum[0], chips_exchange(chip_sum)[0], place, "sum_chips_ev_w_in")
    return loss, grad_x, g, reduced, dgev, gathered, slots


def _me():
    return lax.axis_index("x"), lax.axis_index("y"), lax.axis_index("c")


def _other_chips(x, y):
    return [(1 - x, y), (x, 1 - y), (1 - x, 1 - y)]


def _remote(src, dst, send_sems, recv_sems, k, to):
    return pltpu.make_async_remote_copy(src_ref=src, dst_ref=dst, send_sem=send_sems.at[k], recv_sem=recv_sems.at[k],
                                        device_id=to, device_id_type=MESH)


def _rows_half(ref, h):
    H = ref.shape[-2] // 2
    return ref.at[(slice(None),) * (len(ref.shape) - 2) + (pl.ds(h * H, H), slice(None))]


def allgather_devices(vs):
    n = len(vs)

    def body(*refs):
        ins, outs = refs[:n], refs[n:2 * n]
        send_sems, recv_sems, local_sems = refs[2 * n:]
        x, y, c = _me()
        sib = (x, y, 1 - c)
        chips = _other_chips(x, y)
        me = 4 * x + 2 * y + c
        local = [pltpu.make_async_copy(ins[i], outs[i].at[me], local_sems.at[i]) for i in range(n)]
        for cp in local:
            cp.start()
        first, passed = [], []
        for i in range(n):
            first.append(_remote(ins[i], outs[i].at[me], send_sems, recv_sems, 7 * i, sib))
            for j, (cx, cy) in enumerate(chips):
                first.append(_remote(ins[i], outs[i].at[me], send_sems, recv_sems, 7 * i + 1 + j, (cx, cy, c)))
        for cp in first:
            cp.start()
        for j, (cx, cy) in enumerate(chips):
            for i in range(n):
                got = outs[i].at[4 * cx + 2 * cy + c]
                _remote(got, got, send_sems, recv_sems, 7 * i + 1 + j, (cx, cy, c)).wait_recv()
                fw = _remote(got, got, send_sems, recv_sems, 7 * i + 4 + j, sib)
                fw.start()
                passed.append(fw)
        for i in range(n):
            got = outs[i].at[4 * x + 2 * y + (1 - c)]
            _remote(got, got, send_sems, recv_sems, 7 * i, sib).wait_recv()
            for j, (cx, cy) in enumerate(chips):
                got = outs[i].at[4 * cx + 2 * cy + (1 - c)]
                _remote(got, got, send_sems, recv_sems, 7 * i + 4 + j, sib).wait_recv()
        for cp in first + passed:
            cp.wait_send()
        for cp in local:
            cp.wait()

    return pl.pallas_call(
        body, name="allgather_devices", in_specs=[ANY] * n, out_specs=[ANY] * n,
        out_shape=[jax.ShapeDtypeStruct((N_DEV,) + v.shape, v.dtype) for v in vs],
        scratch_shapes=[pltpu.SemaphoreType.DMA((7 * n,)), pltpu.SemaphoreType.DMA((7 * n,)),
                        pltpu.SemaphoreType.DMA((n,))],
    )(*vs)


def sibling_send_other_half(gs, name):
    n = len(gs)

    def body(*refs):
        ins, outs = refs[:n], refs[n:2 * n]
        send_sems, recv_sems = refs[2 * n:]
        x, y, c = _me()
        cps = [_remote(_rows_half(ins[i], 1 - c), outs[i], send_sems, recv_sems, i, (x, y, 1 - c)) for i in range(n)]
        for cp in cps:
            cp.start()
        for cp in cps:
            cp.wait()

    return pl.pallas_call(
        body, name=name, in_specs=[ANY] * n, out_specs=[ANY] * n,
        out_shape=[jax.ShapeDtypeStruct((g.shape[0], g.shape[1] // 2, g.shape[2]), g.dtype) for g in gs],
        scratch_shapes=[pltpu.SemaphoreType.DMA((n,)), pltpu.SemaphoreType.DMA((n,))],
    )(*gs)


def chips_exchange(parts):
    n = len(parts)

    def body(*refs):
        ins, outs = refs[:n], refs[n:2 * n]
        send_sems, recv_sems = refs[2 * n:]
        x, y, c = _me()
        cps = []
        for i in range(n):
            nl = ins[i].shape[0] // N_CHIPS
            for j, (cx, cy) in enumerate(_other_chips(x, y)):
                cps.append(_remote(ins[i].at[pl.ds((2 * cx + cy) * nl, nl)], outs[i].at[j], send_sems, recv_sems,
                                   3 * i + j, (cx, cy, c)))
        for cp in cps:
            cp.start()
        for cp in cps:
            cp.wait()

    return pl.pallas_call(
        body, name="chips_exchange", in_specs=[ANY] * n, out_specs=[ANY] * n,
        out_shape=[jax.ShapeDtypeStruct((3, a.shape[0] // N_CHIPS) + a.shape[1:], a.dtype) for a in parts],
        scratch_shapes=[pltpu.SemaphoreType.DMA((3 * n,)), pltpu.SemaphoreType.DMA((3 * n,))],
    )(*parts)


def sibling_share(fulls):
    n = len(fulls)

    def body(*refs):
        outs = refs[n:2 * n]
        send_sems, recv_sems = refs[2 * n:]
        x, y, c = _me()
        cps = [_remote(_rows_half(outs[i], c), _rows_half(outs[i], c), send_sems, recv_sems, i, (x, y, 1 - c))
               for i in range(n)]
        for cp in cps:
            cp.start()
        for i in range(n):
            got = _rows_half(outs[i], 1 - c)
            _remote(got, got, send_sems, recv_sems, i, (x, y, 1 - c)).wait_recv()
        for cp in cps:
            cp.wait_send()

    return pl.pallas_call(
        body, name="sibling_share", in_specs=[ANY] * n, out_specs=[ANY] * n,
        out_shape=[jax.ShapeDtypeStruct(f.shape, f.dtype) for f in fulls],
        input_output_aliases={i: i for i in range(n)},
        scratch_shapes=[pltpu.SemaphoreType.DMA((n,)), pltpu.SemaphoreType.DMA((n,))],
    )(*fulls)


def sum_with_sibling(g, recv, core, name):
    S, H, C = recv.shape
    tr = min(512, H)

    def body(c_ref, g_ref, r_ref, o_ref):
        o_ref[...] = (g_ref[...].astype(F32) + r_ref[...].astype(F32)).astype(o_ref.dtype)

    nb = H // tr
    return pl.pallas_call(
        body, name=name,
        grid_spec=pltpu.PrefetchScalarGridSpec(
            num_scalar_prefetch=1, grid=(S, nb),
            in_specs=[pl.BlockSpec((None, tr, C), lambda s, i, c_ref: (s, c_ref[0] * nb + i, 0)),
                      pl.BlockSpec((None, tr, C), lambda s, i, c_ref: (s, i, 0))],
            out_specs=pl.BlockSpec((None, tr, C), lambda s, i, c_ref: (s, i, 0))),
        out_shape=jax.ShapeDtypeStruct((S, H, C), g.dtype), compiler_params=_cp("parallel", "parallel"),
    )(core, g, recv)


def sum_chips(a, recv, place, name):
    _, nl, H, C = recv.shape
    tr = min(512, H)
    nb = H // tr

    def body(p_ref, a_ref, r_ref, o_ref):
        acc = a_ref[...].astype(F32)
        for j in range(3):
            acc = acc + r_ref[j].astype(F32)
        o_ref[...] = acc

    return pl.pallas_call(
        body, name=name,
        grid_spec=pltpu.PrefetchScalarGridSpec(
            num_scalar_prefetch=1, grid=(nl, nb),
            in_specs=[pl.BlockSpec((None, tr, C), lambda l, i, p_ref: (p_ref[0] * nl + l, i, 0)),
                      pl.BlockSpec((3, None, tr, C), lambda l, i, p_ref: (0, l, i, 0))],
            out_specs=pl.BlockSpec((None, tr, C), lambda l, i, p_ref: (l, p_ref[1] * nb + i, 0))),
        out_shape=jax.ShapeDtypeStruct((nl, 2 * H, C), F32), compiler_params=_cp("parallel", "parallel"),
    )(place, a, recv)


def pack_rows(arrays, name):
    starts, r0 = [], 0
    for a in arrays:
        if a.shape[0] >= SUBLANES:
            r0 = -(-r0 // SUBLANES) * SUBLANES
        starts.append(r0)
        r0 += a.shape[0]
    r0 = -(-r0 // SUBLANES) * SUBLANES
    n = len(arrays)

    def body(*refs):
        o_ref = refs[n]
        o_ref[...] = jnp.zeros_like(o_ref)
        for a_ref, s in zip(refs[:n], starts):
            r, c = a_ref.shape
            o_ref[s:s + r, 0:c] = a_ref[...]

    out = pl.pallas_call(body, name=name, out_shape=jax.ShapeDtypeStruct((r0, PACK_COLS), F32))(*arrays)
    return out, starts


def sum_slot(gathered, slot, shape, name):
    r, c = shape

    def body(ga_ref, o_ref):
        acc = ga_ref[0, slot:slot + r, 0:c]
        for d in range(1, N_DEV):
            acc = acc + ga_ref[d, slot:slot + r, 0:c]
        o_ref[...] = acc

    return pl.pallas_call(body, name=name, out_shape=jax.ShapeDtypeStruct((r, c), F32))(gathered)


def carried_allgather(blocks):
    n = len(blocks)

    def first_hop(ins, outs, sems, i, j, chip, x, y, c):
        me = 2 * x + y
        return _remote(_rows_half(ins[i], c), _rows_half(outs[i].at[me], c), sems[0], sems[1], 6 * i + j, (*chip, c))

    def start(ins, outs, sems):
        x, y, c = _me()
        for i in range(n):
            pltpu.make_async_copy(ins[i], outs[i].at[2 * x + y], sems[2].at[i]).start()
        for i in range(n):
            for j, chip in enumerate(_other_chips(x, y)):
                first_hop(ins, outs, sems, i, j, chip, x, y, c).start()

    def finish(ins, outs, sems):
        x, y, c = _me()
        sib = (x, y, 1 - c)
        chips = _other_chips(x, y)
        passed = []
        for j, (cx, cy) in enumerate(chips):
            for i in range(n):
                got = _rows_half(outs[i].at[2 * cx + cy], c)
                _remote(got, got, sems[0], sems[1], 6 * i + j, (cx, cy, c)).wait_recv()
                fw = _remote(got, got, sems[0], sems[1], 6 * i + 3 + j, sib)
                fw.start()
                passed.append(fw)
        for j, (cx, cy) in enumerate(chips):
            for i in range(n):
                got = _rows_half(outs[i].at[2 * cx + cy], 1 - c)
                _remote(got, got, sems[0], sems[1], 6 * i + 3 + j, sib).wait_recv()
        for i in range(n):
            for j, chip in enumerate(chips):
                first_hop(ins, outs, sems, i, j, chip, x, y, c).wait_send()
        for fw in passed:
            fw.wait_send()
        for i in range(n):
            pltpu.make_async_copy(ins[i], outs[i].at[2 * x + y], sems[2].at[i]).wait()

    return Carried(blocks, [jax.ShapeDtypeStruct((N_CHIPS,) + b.shape, b.dtype) for b in blocks],
                   [pltpu.SemaphoreType.DMA((6 * n,)), pltpu.SemaphoreType.DMA((6 * n,)), pltpu.SemaphoreType.DMA((n,))],
                   start, finish)


def carried_allgather_devices(vs):
    n = len(vs)

    def first_copies(ins, outs, sems):
        x, y, c = _me()
        me = 4 * x + 2 * y + c
        cps = []
        for i in range(n):
            cps.append(_remote(ins[i], outs[i].at[me], sems[0], sems[1], 7 * i, (x, y, 1 - c)))
            for j, (cx, cy) in enumerate(_other_chips(x, y)):
                cps.append(_remote(ins[i], outs[i].at[me], sems[0], sems[1], 7 * i + 1 + j, (cx, cy, c)))
        return cps

    def local_copies(ins, outs, sems):
        x, y, c = _me()
        return [pltpu.make_async_copy(ins[i], outs[i].at[4 * x + 2 * y + c], sems[2].at[i]) for i in range(n)]

    def start(ins, outs, sems):
        for cp in local_copies(ins, outs, sems) + first_copies(ins, outs, sems):
            cp.start()

    def finish(ins, outs, sems):
        x, y, c = _me()
        sib = (x, y, 1 - c)
        chips = _other_chips(x, y)
        passed = []
        for j, (cx, cy) in enumerate(chips):
            for i in range(n):
                got = outs[i].at[4 * cx + 2 * cy + c]
                _remote(got, got, sems[0], sems[1], 7 * i + 1 + j, (cx, cy, c)).wait_recv()
                fw = _remote(got, got, sems[0], sems[1], 7 * i + 4 + j, sib)
                fw.start()
                passed.append(fw)
        for i in range(n):
            got = outs[i].at[4 * x + 2 * y + (1 - c)]
            _remote(got, got, sems[0], sems[1], 7 * i, sib).wait_recv()
            for j, (cx, cy) in enumerate(chips):
                got = outs[i].at[4 * cx + 2 * cy + (1 - c)]
                _remote(got, got, sems[0], sems[1], 7 * i + 4 + j, sib).wait_recv()
        for cp in first_copies(ins, outs, sems) + passed:
            cp.wait_send()
        for cp in local_copies(ins, outs, sems):
            cp.wait()

    return Carried(vs, [jax.ShapeDtypeStruct((N_DEV,) + v.shape, v.dtype) for v in vs],
                   [pltpu.SemaphoreType.DMA((7 * n,)), pltpu.SemaphoreType.DMA((7 * n,)), pltpu.SemaphoreType.DMA((n,))],
                   start, finish)


def carried_sibling_send(gs):
    n = len(gs)

    def copies(ins, outs, sems):
        x, y, c = _me()
        return [_remote(_rows_half(ins[i], 1 - c), outs[i], sems[0], sems[1], i, (x, y, 1 - c)) for i in range(n)]

    def start(ins, outs, sems):
        for cp in copies(ins, outs, sems):
            cp.start()

    def finish(ins, outs, sems):
        for cp in copies(ins, outs, sems):
            cp.wait()

    return Carried(gs, [jax.ShapeDtypeStruct((g.shape[0], g.shape[1] // 2, g.shape[2]), g.dtype) for g in gs],
                   [pltpu.SemaphoreType.DMA((n,)), pltpu.SemaphoreType.DMA((n,))], start, finish)


def carried_chips_exchange(parts):
    n = len(parts)

    def copies(ins, outs, sems):
        x, y, c = _me()
        cps = []
        for i in range(n):
            nl = ins[i].shape[0] // N_CHIPS
            for j, (cx, cy) in enumerate(_other_chips(x, y)):
                cps.append(_remote(ins[i].at[pl.ds((2 * cx + cy) * nl, nl)], outs[i].at[j], sems[0], sems[1],
                                   3 * i + j, (cx, cy, c)))
        return cps

    def start(ins, outs, sems):
        for cp in copies(ins, outs, sems):
            cp.start()

    def finish(ins, outs, sems):
        for cp in copies(ins, outs, sems):
            cp.wait()

    return Carried(parts, [jax.ShapeDtypeStruct((3, a.shape[0] // N_CHIPS) + a.shape[1:], a.dtype) for a in parts],
                   [pltpu.SemaphoreType.DMA((3 * n,)), pltpu.SemaphoreType.DMA((3 * n,))], start, finish)


BIG = ("ev_w_in", "ev_s5_glu_w", "ev_w_out", "od_w_in", "od_w_out", "xa_w_qg", "xa_w_kv", "xa_w_o")
SHARDED_F32 = (("ev_conv_w", 2), ("od_norm_g", 1))
SMALL = ("mem_norm_g", "ev_norm_g", "ev_s5_lambda_re", "ev_s5_lambda_im", "ev_s5_log_dt", "ev_s5_b_re", "ev_s5_b_im",
         "ev_s5_c_re", "ev_s5_c_im", "ev_s5_d", "ev_s5_glu_b", "ev_conv_b", "ev_conv_ln_g", "ev_conv_ln_b",
         "od_rel_bias", "xa_norm_g", "final_norm_g")
NARROW = ("ev_s5_c_re", "ev_s5_c_im")
DENSE_B = ("ev_s5_b_re", "ev_s5_b_im")
PACK_COLS = 1024
PACKED_SMALL = tuple(n for n in SMALL if n not in NARROW and n != "ev_norm_g")
SINGLE_SMALL = NARROW + tuple(n for n, _ in SHARDED_F32)
WEIGHTS = ("mem_norm_g", "ev_norm_g", "ev_w_in", "ev_s5_lambda_re", "ev_s5_lambda_im", "ev_s5_log_dt", "ev_s5_b_re",
           "ev_s5_b_im", "ev_s5_c_re", "ev_s5_c_im", "ev_s5_d", "ev_s5_glu_w", "ev_s5_glu_b", "ev_conv_w", "ev_conv_b",
           "ev_conv_ln_g", "ev_conv_ln_b", "ev_w_out", "od_norm_g", "od_w_in", "od_rel_bias", "od_w_out", "xa_norm_g",
           "xa_w_qg", "xa_w_kv", "xa_w_o", "final_norm_g")


def _as2d(a):
    return a.reshape(1, -1) if a.ndim == 1 else a.reshape(-1, a.shape[-1])


def kernel(x, mem, mem_norm_g, ev_norm_g, ev_w_in, ev_s5_lambda_re, ev_s5_lambda_im, ev_s5_log_dt, ev_s5_b_re, ev_s5_b_im, ev_s5_c_re, ev_s5_c_im, ev_s5_d, ev_s5_glu_w, ev_s5_glu_b, ev_conv_w, ev_conv_b, ev_conv_ln_g, ev_conv_ln_b, ev_w_out, od_norm_g, od_w_in, od_rel_bias, od_w_out, xa_norm_g, xa_w_qg, xa_w_kv, xa_w_o, final_norm_g, loss_target, m_mem_norm_g, m_ev_norm_g, m_ev_w_in, m_ev_s5_lambda_re, m_ev_s5_lambda_im, m_ev_s5_log_dt, m_ev_s5_b_re, m_ev_s5_b_im, m_ev_s5_c_re, m_ev_s5_c_im, m_ev_s5_d, m_ev_s5_glu_w, m_ev_s5_glu_b, m_ev_conv_w, m_ev_conv_b, m_ev_conv_ln_g, m_ev_conv_ln_b, m_ev_w_out, m_od_norm_g, m_od_w_in, m_od_rel_bias, m_od_w_out, m_xa_norm_g, m_xa_w_qg, m_xa_w_kv, m_xa_w_o, m_final_norm_g, v_mem_norm_g, v_ev_norm_g, v_ev_w_in, v_ev_s5_lambda_re, v_ev_s5_lambda_im, v_ev_s5_log_dt, v_ev_s5_b_re, v_ev_s5_b_im, v_ev_s5_c_re, v_ev_s5_c_im, v_ev_s5_d, v_ev_s5_glu_w, v_ev_s5_glu_b, v_ev_conv_w, v_ev_conv_b, v_ev_conv_ln_g, v_ev_conv_ln_b, v_ev_w_out, v_od_norm_g, v_od_w_in, v_od_rel_bias, v_od_w_out, v_xa_norm_g, v_xa_w_qg, v_xa_w_kv, v_xa_w_o, v_final_norm_g):
    a = dict(locals())
    w = {n: a[n] for n in WEIGHTS}
    shard = (2 * lax.axis_index("x") + lax.axis_index("y")).reshape(1).astype(jnp.int32)
    core = lax.axis_index("c").reshape(1).astype(jnp.int32)

    place = jnp.concatenate([shard, core])

    blocks = {n: w[n].astype(BF16).reshape(-1, w[n].shape[-1]) for n in BIG}
    conv_blk = jnp.pad(_as2d(w["ev_conv_w"]), ((0, 1), (0, 0)))
    odn_blk = w["od_norm_g"].reshape(2, -1)
    bias, evin_g, conv_g, odn_g = att_bias(w["od_rel_bias"][0],
                                           carried=carried_allgather([blocks["ev_w_in"], conv_blk, odn_blk]))
    gw = {"ev_w_in": evin_g}
    p = {n: w[n] for n in SMALL}
    p["ev_conv_w"] = jnp.concatenate([conv_g[s, :CONV_KERNEL] for s in range(N_CHIPS)], axis=1)[None]
    p["od_norm_g"] = odn_g.reshape(1, D_MODEL)

    loss, grad_x, g, reduced, dgev, gath, slots = local_step(x[0], mem[0], loss_target[0], p, gw,
                                                             {n: blocks[n] for n in LATE}, bias, place, core)
    loss = lax.psum(loss[0, 0], ("x", "y", "c"))
    g_big = dict(zip(BIG, sibling_share([reduced[n] for n in BIG])))

    out = {tag: {} for tag in ("grad", "delta", "m", "v")}
    for n in BIG:
        sh = w[n].shape
        to2d = lambda t: t.reshape(-1, sh[-1])
        gn = to2d(g_big[n])
        d, mn, vn = adamw(to2d(w[n]), gn, to2d(a["m_" + n]), to2d(a["v_" + n]), "adamw_" + n)
        for tag, val in zip(("grad", "delta", "m", "v"), (gn, d, mn, vn)):
            out[tag][n] = val.reshape(sh)

    jobs = [(n, gath[0], s) for n, s in zip(PACKED_SMALL, slots)]
    jobs += [(n, gt, None) for n, gt in zip(SINGLE_SMALL, gath[1:])]
    jobs += [("ev_norm_g", allgather_devices([dgev])[0], None)]
    for n, gt, slot in jobs:
        sh = w[n].shape
        w2, m2, v2 = _as2d(w[n]), _as2d(a["m_" + n]), _as2d(a["v_" + n])
        if n in DENSE_B:
            gn = _as2d(s5_b_from_dense(sum_slot(gt, slot, g[n].shape[-2:], "sum_" + n)))
            d, mn, vn = adamw(w2, gn, m2, v2, "adamw_" + n)
        else:
            gn, d, mn, vn = adamw_allreduce(gt, w2, m2, v2, shard, "adamw_" + n, slot=slot)
        for tag, val in zip(("grad", "delta", "m", "v"), (gn, d, mn, vn)):
            out[tag][n] = val.reshape(sh)

    res = [loss, grad_x[None]]
    for tag in ("grad", "delta", "m", "v"):
        res += [out[tag][n] for n in WEIGHTS]
    return tuple(res)
```

```python
import math

import jax
import jax.numpy as jnp
import numpy as np
from jax import lax
from jax.experimental import pallas as pl
from jax.experimental.pallas import tpu as pltpu

F32 = jnp.float32
BF16 = jnp.bfloat16

D_MODEL = 1024
CHUNK = 64
LEFT_CHUNKS = 8
S5_WIDTH = 512
S5_GROUP = 16
S5_GROUPS = 32
S5_STATE = 64
S5_COLS = S5_GROUPS * S5_STATE
S5_SPLIT = 4
S5_CC = S5_COLS // S5_SPLIT
S5_UC = S5_WIDTH // S5_SPLIT
CONV_WIDTH = 512
CONV_KERNEL = 31
CONV_HALO = 32
ATT_HEADS = 16
ATT_HEAD_DIM = 64
MAX_REL = 128
MEM_LEN = 256
XA_HEADS = 4
XA_HEAD_DIM = 256
EPS = 1e-6
EVEN_IN = 2560
ODD_IN = 4096

ADAM_LR = 0.001
ADAM_B1 = 0.9
ADAM_B2 = 0.999
ADAM_EPS = 1e-08
ADAM_WD = 0.01
ADAM_STEP = 10

ROW_TILE = 256
MM_TILE = 512
S5_TILE = 512
ATT_QB = 256
ATT_PAD = LEFT_CHUNKS * CHUNK
ATT_WIN = ATT_PAD + ATT_QB
VMEM_LIMIT_V7X = 56 * 1024 * 1024
NEG = -1e30
LANES = 128
N_CHIPS = 4
N_DEV = 8

MESH = pl.DeviceIdType.MESH
ANY = pl.BlockSpec(memory_space=pl.ANY)


def _cp(*sem, vmem=VMEM_LIMIT_V7X):
    return pltpu.CompilerParams(dimension_semantics=sem if sem else None, vmem_limit_bytes=vmem)


def _full(shape):
    n = len(shape)
    return pl.BlockSpec(shape, lambda *_: (0,) * n)


def _wspec(w, layer=None):
    if layer is None:
        return _full(w.shape)
    s, _, r, c = w.shape
    return pl.BlockSpec((s, None, r, c), lambda *_: (0, layer, 0, 0))


def _lane_tile(n, cap):
    return max(t for t in range(LANES, min(n, cap) + 1, LANES) if n % t == 0)


def _sigmoid(x):
    return 1.0 / (1.0 + jnp.exp(-x))


def _silu(x):
    return x * _sigmoid(x)


def _silu_pair(x):
    s = _sigmoid(x)
    return x * s, s * (1.0 + x * (1.0 - s))


_GELU_C = math.sqrt(2.0 / math.pi)


def _gelu(x):
    return 0.5 * x * (1.0 + jnp.tanh(_GELU_C * (x + 0.044715 * x * x * x)))


def _dgelu(x):
    t = jnp.tanh(_GELU_C * (x + 0.044715 * x * x * x))
    return 0.5 * (1.0 + t) + 0.5 * x * (1.0 - t * t) * _GELU_C * (1.0 + 3.0 * 0.044715 * x * x)


def _dot(a, b):
    return jnp.dot(a, b, preferred_element_type=F32)


def _dot_nt(a, b):
    return lax.dot_general(a, b, (((1,), (1,)), ((), ())), preferred_element_type=F32)


def _dot_tn(a, b):
    return lax.dot_general(a, b, (((0,), (0,)), ((), ())), preferred_element_type=F32)


def _dot_cols(a, w4, shards=range(N_CHIPS)):
    return jnp.concatenate([_dot(a, w4[s]) for s in shards], axis=1)


def _dot_rows(a, w4):
    r = w4.shape[1]
    acc = _dot(a[:, 0:r], w4[0])
    for s in range(1, N_CHIPS):
        acc = acc + _dot(a[:, s * r:(s + 1) * r], w4[s])
    return acc


def _dot_nt_cols(dys, w4):
    acc = _dot_nt(dys[0], w4[0])
    for s in range(1, N_CHIPS):
        acc = acc + _dot_nt(dys[s], w4[s])
    return acc


def _dot_nt_rows(dy, w4):
    return jnp.concatenate([_dot_nt(dy, w4[s]) for s in range(N_CHIPS)], axis=1)


def _col_pieces(v, n):
    return [v[:, s * n:(s + 1) * n] for s in range(N_CHIPS)]


def _rms_parts(xv):
    inv = lax.rsqrt(jnp.mean(xv * xv, axis=-1, keepdims=True) + EPS)
    return inv, xv * inv


def _rms_bwd(xv, g, dh):
    inv, xhat = _rms_parts(xv)
    dg = jnp.sum(dh * xhat, axis=0, keepdims=True)
    dxh = dh * g
    dx = inv * (dxh - xhat * jnp.mean(dxh * xhat, axis=-1, keepdims=True))
    return dx, dg


def norm_mm(x, g, w4, groups, name, tm=MM_TILE):
    M, D = x.shape
    n = w4.shape[2]
    tm = min(tm, M)

    def body(x_ref, g_ref, w_ref, *outs):
        _, xhat = _rms_parts(x_ref[...])
        hb = (xhat * g_ref[...]).astype(BF16)
        for o, (shards, dt, _) in zip(outs, groups):
            o[...] = _dot_cols(hb, w_ref, shards).astype(dt)
        outs[-1][...] = hb

    out_shape = [jax.ShapeDtypeStruct((M + pad, len(sh) * n), dt) for (sh, dt, pad) in groups]
    out_specs = [pl.BlockSpec((tm, len(sh) * n), lambda i, p=pad // tm: (i + p, 0)) for (sh, _, pad) in groups]
    out_shape.append(jax.ShapeDtypeStruct((M, D), BF16))
    out_specs.append(pl.BlockSpec((tm, D), lambda i: (i, 0)))
    return pl.pallas_call(
        body, name=name, grid=(M // tm,),
        in_specs=[pl.BlockSpec((tm, D), lambda i: (i, 0)), _full(g.shape), _full(w4.shape)],
        out_specs=out_specs, out_shape=out_shape, compiler_params=_cp("parallel"),
    )(x, g, w4)


def zero_rows(buf, rows, name, tm=ROW_TILE):
    C = buf.shape[1]

    def body(b_ref, o_ref):
        o_ref[...] = jnp.zeros_like(o_ref)

    return pl.pallas_call(
        body, name=name, grid=(rows // tm,), in_specs=[ANY],
        out_specs=pl.BlockSpec((tm, C), lambda i: (i, 0)),
        out_shape=jax.ShapeDtypeStruct(buf.shape, buf.dtype), input_output_aliases={0: 0},
        compiler_params=_cp("parallel"),
    )(buf)


def mm_cols(a, w, layer, name, out_dtype):
    M = a.shape[0]
    n = w.shape[3]

    def body(a_ref, w_ref, o_ref):
        o_ref[...] = _dot_cols(a_ref[...], w_ref).astype(out_dtype)

    return pl.pallas_call(
        body, name=name, grid=(1,), in_specs=[_full(a.shape), _wspec(w, layer)],
        out_specs=_full((M, N_CHIPS * n)), out_shape=jax.ShapeDtypeStruct((M, N_CHIPS * n), out_dtype),
        compiler_params=_cp("arbitrary"),
    )(a, w)


def mm_nt_cols(dy, w, layer, name):
    M = dy.shape[0]
    K, n = w.shape[2], w.shape[3]

    def body(d_ref, w_ref, o_ref):
        o_ref[...] = _dot_nt_cols(_col_pieces(d_ref[...].astype(BF16), n), w_ref)

    return pl.pallas_call(
        body, name=name, grid=(1,), in_specs=[_full(dy.shape), _wspec(w, layer)],
        out_specs=_full((M, K)), out_shape=jax.ShapeDtypeStruct((M, K), F32), compiler_params=_cp("arbitrary"),
    )(dy, w)


def mm_nt_rows(dy, w4, name, tm=MM_TILE):
    M, N = dy.shape
    K = N_CHIPS * w4.shape[1]
    tm = min(tm, M)

    def body(d_ref, w_ref, o_ref):
        o_ref[...] = _dot_nt_rows(d_ref[...].astype(BF16), w_ref)

    return pl.pallas_call(
        body, name=name, grid=(M // tm,),
        in_specs=[pl.BlockSpec((tm, N), lambda i: (i, 0)), _full(w4.shape)],
        out_specs=pl.BlockSpec((tm, K), lambda i: (i, 0)),
        out_shape=jax.ShapeDtypeStruct((M, K), F32), compiler_params=_cp("parallel"),
    )(dy, w4)


def mm_nt_normbwd(dys, offs, w4, x, g, dx_out, name, tm=MM_TILE):
    M, D = x.shape
    n = w4.shape[2]
    tm = min(tm, M)
    nd = len(dys)

    def body(*refs):
        d_refs = refs[:nd]
        w_ref, x_ref, g_ref, dxo_ref, dx_ref, dg_ref = refs[nd:]
        if nd == 1:
            pieces = _col_pieces(d_refs[0][...].astype(BF16), n)
        else:
            pieces = [r[...].astype(BF16) for r in d_refs]
        dh = _dot_nt_cols(pieces, w_ref)
        dx, dg = _rms_bwd(x_ref[...], g_ref[...], dh)
        dx_ref[...] = dxo_ref[...] + dx

        @pl.when(pl.program_id(0) == 0)
        def _():
            dg_ref[...] = jnp.zeros_like(dg_ref)

        dg_ref[...] += dg

    row = lambda c, off=0: pl.BlockSpec((tm, c), lambda i, p=off // tm: (i + p, 0))
    return pl.pallas_call(
        body, name=name, grid=(M // tm,),
        in_specs=[row(d.shape[1], off) for d, off in zip(dys, offs)] + [_full(w4.shape), row(D), _full(g.shape), row(D)],
        out_specs=[row(D), _full((1, D))],
        out_shape=[jax.ShapeDtypeStruct((M, D), F32), jax.ShapeDtypeStruct((1, D), F32)],
        compiler_params=_cp("arbitrary"),
    )(*dys, w4, x, g, dx_out)


def mm_tn(a, b, name, layout, into=None, b_off=0, out_dtype=BF16, bm=1024, bn=1280, bl=1024, carried=None):
    L, K = a.shape
    N = b.shape[1]
    kind = layout[0]
    arg = layout[1] if len(layout) > 1 else None
    bm, bn, bl = _lane_tile(K, bm), _lane_tile(N, bn), min(bl, L)
    assert L % bl == 0 and b_off % bl == 0, (L, bl, b_off)
    nl = L // bl
    n_sh, r_sh = N // N_CHIPS, K // N_CHIPS
    lay = (None,) if arg is None else (None, None)
    mid = () if arg is None else (arg,)
    gs = 1
    if kind == "plain":
        oshape, oblock, oidx = (K, N), (bm, bn), lambda i, j, l: (i, j)
    elif kind == "slab":
        oshape, oblock, oidx = (N_CHIPS, K, N), (None, bm, bn), lambda i, j, l: (arg, i, j)
    elif kind == "cols":
        bn = max(bn - bn % n_sh, n_sh) if bn >= n_sh else _lane_tile(n_sh, bn)
        gs = max(bn // n_sh, 1)
        per = n_sh // bn if gs == 1 else 1
        oshape = (N_CHIPS,) + ((2,) if arg is not None else ()) + (K, n_sh)
        oblock = ((gs,) if gs > 1 else (None,)) + lay[1:] + (bm, min(bn, n_sh))
        oidx = lambda i, j, l: (j // per,) + mid + (i, j % per)
    else:
        bm = max(bm - bm % r_sh, r_sh) if bm >= r_sh else _lane_tile(r_sh, bm)
        gs = max(bm // r_sh, 1)
        per = r_sh // bm if gs == 1 else 1
        oshape = (N_CHIPS,) + ((2,) if arg is not None else ()) + (r_sh, N)
        oblock = ((gs,) if gs > 1 else (None,)) + lay[1:] + (min(bm, r_sh), bn)
        oidx = lambda i, j, l: (i // per,) + mid + (i % per, j)
    assert K % bm == 0 and N % bn == 0, (K, bm, N, bn)

    grid = (K // bm, N // bn, nl)

    def body(*refs):
        top = end = None
        if carried is not None:
            refs, parts = carried.split(refs, 2 if into is None else 3, 1, 1)
            top, end = carried.hooks(parts, grid)
            top()
        a_ref, b_ref, o_ref, acc = refs[0], refs[1], refs[-2], refs[-1]
        l = pl.program_id(2)

        @pl.when(l == 0)
        def _():
            acc[...] = jnp.zeros_like(acc)

        acc[...] += _dot_tn(a_ref[...].astype(BF16), b_ref[...].astype(BF16))

        @pl.when(l == nl - 1)
        def _():
            if gs == 1:
                o_ref[...] = acc[...].astype(out_dtype)
            elif kind == "cols":
                for t in range(gs):
                    o_ref[t] = acc[:, t * n_sh:(t + 1) * n_sh].astype(out_dtype)
            else:
                for t in range(gs):
                    o_ref[t] = acc[t * r_sh:(t + 1) * r_sh, :].astype(out_dtype)

        if end is not None:
            end()

    in_specs = [pl.BlockSpec((bl, bm), lambda i, j, l: (l, i)),
                pl.BlockSpec((bl, bn), lambda i, j, l, p=b_off // bl: (l + p, j))]
    args = [a, b]
    alias = {}
    if into is not None:
        in_specs.append(ANY)
        args.append(into)
        alias = {2: 0}
    out_specs, out_shape = pl.BlockSpec(oblock, oidx), jax.ShapeDtypeStruct(oshape, out_dtype)
    scratch = [pltpu.VMEM((bm, bn), F32)]
    if carried is None:
        sem = ("parallel", "parallel", "arbitrary")
    else:
        in_specs += [ANY] * len(carried.arrays)
        args += carried.arrays
        out_specs, out_shape = [out_specs] + [ANY] * len(carried.out_shapes), [out_shape] + carried.out_shapes
        scratch += carried.sems
        sem = ("arbitrary",) * 3
    return pl.pallas_call(
        body, name=name, grid=grid, in_specs=in_specs, out_specs=out_specs, out_shape=out_shape,
        scratch_shapes=scratch, input_output_aliases=alias, compiler_params=_cp(*sem),
    )(*args)


def rms_fwd(x, g, name):
    def body(x_ref, g_ref, ob_ref):
        _, xhat = _rms_parts(x_ref[...])
        ob_ref[...] = (xhat * g_ref[...]).astype(BF16)

    return pl.pallas_call(body, name=name, out_shape=jax.ShapeDtypeStruct(x.shape, BF16))(x, g)


def rms_dgain(x, dy0, dy1, name):
    def body(x_ref, d0_ref, d1_ref, o_ref):
        _, xhat = _rms_parts(x_ref[...])
        o_ref[...] = jnp.sum((d0_ref[...] + d1_ref[...]) * xhat, axis=0, keepdims=True)

    return pl.pallas_call(body, name=name, out_shape=jax.ShapeDtypeStruct((1, x.shape[1]), F32))(x, dy0, dy1)


def _s5_discretise(lr, li, logdt, bt_re, bt_im):
    dt = jnp.exp(logdt)
    mag = jnp.exp(lr * dt)
    ab_re = mag * jnp.cos(li * dt)
    ab_im = mag * jnp.sin(li * dt)
    den = lr * lr + li * li
    nr = ab_re - 1.0
    coef_re = (nr * lr + ab_im * li) / den
    coef_im = (ab_im * lr - nr * li) / den
    cr = coef_re[:, None, :]
    ci = coef_im[:, None, :]
    bb_re = cr * bt_re - ci * bt_im
    bb_im = cr * bt_im + ci * bt_re
    return ab_re, ab_im, bb_re, bb_im


def s5_param_fwd(lr, li, logdt, bt_re, bt_im):
    def body(lr_ref, li_ref, ld_ref, br_ref, bi_ref, bbr_ref, bbi_ref):
        _, _, bb_re, bb_im = _s5_discretise(lr_ref[...], li_ref[...], ld_ref[...], br_ref[...], bi_ref[...])
        bbr_ref[...] = bb_re
        bbi_ref[...] = bb_im

    sh = jax.ShapeDtypeStruct(bt_re.shape, F32)
    return pl.pallas_call(body, name="s5_param_fwd", out_shape=[sh, sh])(lr, li, logdt, bt_re, bt_im)


def s5_param_bwd(lr, li, logdt, bt_re, bt_im, d_ab_re, d_ab_im, d_bb_re, d_bb_im):
    def body(lr_ref, li_ref, ld_ref, br_ref, bi_ref, dar_ref, dai_ref, dbr_ref, dbi_ref,
             o_lr, o_li, o_ld, o_br, o_bi):
        _, vjp = jax.vjp(_s5_discretise, lr_ref[...], li_ref[...], ld_ref[...], br_ref[...], bi_ref[...])
        g = vjp((dar_ref[...], dai_ref[...], dbr_ref[...], dbi_ref[...]))
        for o, v in zip((o_lr, o_li, o_ld), g[:3]):
            o[...] = v
        for o, v in zip((o_br, o_bi), g[3:]):
            for c in range(S5_GROUP):
                o[:, c * S5_STATE:(c + 1) * S5_STATE] = v[:, c, :]

    dense = jax.ShapeDtypeStruct((S5_GROUPS, S5_GROUP * S5_STATE), F32)
    shapes = [jax.ShapeDtypeStruct(a.shape, F32) for a in (lr, li, logdt)] + [dense, dense]
    return pl.pallas_call(body, name="s5_param_bwd", out_shape=shapes)(
        lr, li, logdt, bt_re, bt_im, d_ab_re, d_ab_im, d_bb_re, d_bb_im)


def s5_tables(lr_flat, li_flat, logdt_flat):
    def body(lr_ref, li_ref, ld_ref, tab_ref):
        dt = jnp.exp(ld_ref[...])
        a = lr_ref[...] * dt
        th = li_ref[...] * dt
        row = lax.broadcasted_iota(jnp.int32, (8, 1), 0)
        rowf = row.astype(F32)

        def power(e, sign):
            m = jnp.exp(e * a)
            return m * jnp.cos(e * th), sign * m * jnp.sin(e * th)

        k = 0
        for sign, fwd in ((1.0, True), (-1.0, False)):
            for s in (1, 2, 4):
                pr, pi = power(jnp.full((8, 1), float(s), F32), sign)
                keep = (row >= s) if fwd else (row + s < 8)
                tab_ref[k] = jnp.where(keep, pr, 0.0)
                tab_ref[k + 1] = jnp.where(keep, pi, 0.0)
                k += 2
            e = rowf + 1.0 if fwd else 8.0 - rowf
            pr, pi = power(e, sign)
            tab_ref[k] = pr
            tab_ref[k + 1] = pi
            k += 2

    return pl.pallas_call(body, name="s5_tables",
                          out_shape=jax.ShapeDtypeStruct((16, 8, S5_COLS), F32))(lr_flat, li_flat, logdt_flat)


def _scan_block(a, b, tabs, base, cr, ci, reverse):
    for n, s in enumerate((1, 2, 4)):
        mr = tabs[base + 2 * n]
        mi = tabs[base + 2 * n + 1]
        sh = (8 - s) if reverse else s
        ar = pltpu.roll(a, sh, 0)
        br = pltpu.roll(b, sh, 0)
        a, b = a + mr * ar - mi * br, b + mr * br + mi * ar
    pr = tabs[base + 6]
    pi = tabs[base + 7]
    a, b = a + pr * cr - pi * ci, b + pr * ci + pi * cr
    return a, b


class Carried:
    def __init__(self, arrays, out_shapes, sems, start, finish):
        self.arrays, self.out_shapes, self.sems = list(arrays), list(out_shapes), list(sems)
        self.start, self.finish = start, finish

    def split(self, refs, n_in, n_out, n_scratch):
        a, o, s = len(self.arrays), len(self.out_shapes), len(self.sems)
        own_in, car_in = refs[:n_in], refs[n_in:n_in + a]
        own_out, car_out = refs[n_in + a:n_in + a + n_out], refs[n_in + a + n_out:n_in + a + n_out + o]
        rest = refs[n_in + a + n_out + o:]
        return own_in + own_out + rest[:n_scratch], (car_in, car_out, rest[n_scratch:n_scratch + s])

    def hooks(self, parts, grid):
        first = last = None
        for k, n in enumerate(grid):
            i = pl.program_id(k)
            first = (i == 0) if first is None else first & (i == 0)
            last = (i == n - 1) if last is None else last & (i == n - 1)

        def top():
            pl.when(first)(lambda: self.start(*parts))

        def end():
            pl.when(last)(lambda: self.finish(*parts))

        return top, end


def s5_fwd(z, bbd_re, bbd_im, ccd_re, ccd_im, tab, dskip, tm=S5_TILE, carried=None):
    L = z.shape[0]
    tm = min(tm, L)
    nt = L // tm

    def body(*refs):
        top = end = None
        if carried is not None:
            refs, parts = carried.split(refs, 7, 4, 3)
            top, end = carried.hooks(parts, (S5_SPLIT, nt))
            top()
        u_ref, bbr_ref, bbi_ref, ccr_ref, cci_ref, tab_ref, d_ref, y_ref, ck_ref, hr_ref, hi_ref, xr, xi, car = refs
        t = pl.program_id(1)

        @pl.when(t == 0)
        def _():
            car[...] = jnp.zeros_like(car)

        u = u_ref[...]
        ub = u.astype(BF16)
        xr[...] = _dot(ub, bbr_ref[...])
        xi[...] = _dot(ub, bbi_ref[...])
        tabs = [tab_ref[k] for k in range(8)]

        def blk(i, c):
            r0 = pl.multiple_of(i * 8, 8)
            a, b = _scan_block(xr[pl.ds(r0, 8), :], xi[pl.ds(r0, 8), :], tabs, 0, c[0], c[1], False)
            xr[pl.ds(r0, 8), :] = a
            xi[pl.ds(r0, 8), :] = b
            return a[7:8, :], b[7:8, :]

        cr, ci = lax.fori_loop(0, tm // 8, blk, (car[0:1, :], car[1:2, :]))
        car[0:1, :] = cr
        car[1:2, :] = ci
        ck_ref[0:1, :] = cr
        ck_ref[1:2, :] = ci
        hrb = xr[...].astype(BF16)
        hib = xi[...].astype(BF16)
        hr_ref[...] = hrb
        hi_ref[...] = hib
        y_ref[...] = _dot(hrb, ccr_ref[...]) - _dot(hib, cci_ref[...]) + d_ref[...] * u
        if end is not None:
            end()

    extra = carried.arrays if carried is not None else []
    extra_out = carried.out_shapes if carried is not None else []
    extra_sems = carried.sems if carried is not None else []
    return pl.pallas_call(
        body, name="s5_fwd", grid=(S5_SPLIT, nt),
        in_specs=[pl.BlockSpec((tm, S5_UC), lambda j, t: (t, j)),
                  pl.BlockSpec((None, S5_UC, S5_CC), lambda j, t: (j, 0, 0)),
                  pl.BlockSpec((None, S5_UC, S5_CC), lambda j, t: (j, 0, 0)),
                  pl.BlockSpec((None, S5_CC, S5_UC), lambda j, t: (j, 0, 0)),
                  pl.BlockSpec((None, S5_CC, S5_UC), lambda j, t: (j, 0, 0)),
                  pl.BlockSpec((8, 8, S5_CC), lambda j, t: (0, 0, j)),
                  pl.BlockSpec((1, S5_UC), lambda j, t: (0, j))] + [ANY] * len(extra),
        out_specs=[pl.BlockSpec((tm, S5_UC), lambda j, t: (t, j)),
                   pl.BlockSpec((None, 2, S5_CC), lambda j, t: (t, 0, j)),
                   pl.BlockSpec((tm, S5_CC), lambda j, t: (t, j)),
                   pl.BlockSpec((tm, S5_CC), lambda j, t: (t, j))] + [ANY] * len(extra_out),
        out_shape=[jax.ShapeDtypeStruct((L, S5_WIDTH), F32), jax.ShapeDtypeStruct((nt, 2, S5_COLS), F32),
                   jax.ShapeDtypeStruct((L, S5_COLS), BF16), jax.ShapeDtypeStruct((L, S5_COLS), BF16)] + extra_out,
        scratch_shapes=[pltpu.VMEM((tm, S5_CC), F32), pltpu.VMEM((tm, S5_CC), F32), pltpu.VMEM((2, S5_CC), F32)]
        + extra_sems,
        compiler_params=_cp("arbitrary" if carried is not None else "parallel", "arbitrary"),
    )(z, bbd_re, bbd_im, ccd_re, ccd_im, tab, dskip, *extra)


def s5_bwd(z, dy, dz, ckpt, hrb, hib, bbd_re, bbd_im, ccd_re, ccd_im, tab, dskip, tm=S5_TILE, carried=None):
    L = z.shape[0]
    tm = min(tm, L)
    nt = L // tm

    def body(*refs):
        top = end = None
        if carried is not None:
            refs, parts = carried.split(refs, 12, 7, 7)
            top, end = carried.hooks(parts, (S5_SPLIT, nt))
            top()
        (u_ref, dy_ref, dz_ref, ck_ref, hrb_ref, hib_ref, bbr_ref, bbi_ref, ccr_ref, cci_ref, tab_ref, d_ref,
         du_ref, da_ref, dbr_ref, dbi_ref, dcr_ref, dci_ref, dd_ref, hr, hi, gr, gi, car, acr, aci) = refs
        t = pl.program_id(1)
        tt = nt - 1 - t

        @pl.when(t == 0)
        def _():
            for r in (car, acr, aci, dbr_ref, dbi_ref, dcr_ref, dci_ref, dd_ref):
                r[...] = jnp.zeros_like(r)

        u = u_ref[...]
        ub = u.astype(BF16)
        dyv = dy_ref[...]
        dyb = dyv.astype(BF16)
        tabs = [None] * 8 + [tab_ref[k] for k in range(8, 16)]

        live = (tt > 0).astype(F32)
        hr[0:8, :] = jnp.broadcast_to(ck_ref[0:1, :] * live, (8, S5_CC))
        hi[0:8, :] = jnp.broadcast_to(ck_ref[1:2, :] * live, (8, S5_CC))
        hrb = hrb_ref[...]
        hib = hib_ref[...]
        hr[8:, :] = hrb.astype(F32)
        hi[8:, :] = hib.astype(F32)
        dcr_ref[...] += _dot_tn(hrb, dyb)
        dci_ref[...] -= _dot_tn(hib, dyb)

        gr[...] = _dot_nt(dyb, ccr_ref[...])
        gi[...] = -_dot_nt(dyb, cci_ref[...])
        row0 = lax.broadcasted_iota(jnp.int32, (8, S5_CC), 0) == 0

        def rblk(k, c):
            i = tm // 8 - 1 - k
            r0 = pl.multiple_of(i * 8, 8)
            a, b = _scan_block(gr[pl.ds(r0, 8), :], gi[pl.ds(r0, 8), :], tabs, 8, c[0], c[1], True)
            gr[pl.ds(r0, 8), :] = a
            gi[pl.ds(r0, 8), :] = b
            r1 = pl.multiple_of(i * 8 + 8, 8)
            hpr = jnp.where(row0, pltpu.roll(hr[pl.ds(r0, 8), :], 1, 0), pltpu.roll(hr[pl.ds(r1, 8), :], 1, 0))
            hpi = jnp.where(row0, pltpu.roll(hi[pl.ds(r0, 8), :], 1, 0), pltpu.roll(hi[pl.ds(r1, 8), :], 1, 0))
            acr[...] += a * hpr + b * hpi
            aci[...] += b * hpr - a * hpi
            return a[0:1, :], b[0:1, :]

        cr, ci = lax.fori_loop(0, tm // 8, rblk, (car[0:1, :], car[1:2, :]))
        car[0:1, :] = cr
        car[1:2, :] = ci

        grb = gr[...].astype(BF16)
        gib = gi[...].astype(BF16)
        du_ref[...] = (_dot_nt(grb, bbr_ref[...]) + _dot_nt(gib, bbi_ref[...]) + d_ref[...] * dyv).astype(BF16)
        dbr_ref[...] += _dot_tn(ub, grb)
        dbi_ref[...] += _dot_tn(ub, gib)
        dd_ref[...] += jnp.sum(dyv * u, axis=0, keepdims=True)

        @pl.when(t == nt - 1)
        def _():
            da_ref[0:1, :] = jnp.sum(acr[...], axis=0, keepdims=True)
            da_ref[1:2, :] = jnp.sum(aci[...], axis=0, keepdims=True)

        if end is not None:
            end()

    extra = carried.arrays if carried is not None else []
    extra_out = carried.out_shapes if carried is not None else []
    extra_sems = carried.sems if carried is not None else []
    chunk = lambda a, b: pl.BlockSpec((None, a, b), lambda j, t: (j, 0, 0))
    return pl.pallas_call(
        body, name="s5_bwd", grid=(S5_SPLIT, nt),
        in_specs=[pl.BlockSpec((tm, S5_UC), lambda j, t: (nt - 1 - t, j)),
                  pl.BlockSpec((tm, S5_UC), lambda j, t: (nt - 1 - t, j)),
                  ANY,
                  pl.BlockSpec((None, 2, S5_CC), lambda j, t: (jnp.maximum(nt - 2 - t, 0), 0, j)),
                  pl.BlockSpec((tm, S5_CC), lambda j, t: (nt - 1 - t, j)),
                  pl.BlockSpec((tm, S5_CC), lambda j, t: (nt - 1 - t, j)),
                  chunk(S5_UC, S5_CC), chunk(S5_UC, S5_CC), chunk(S5_CC, S5_UC), chunk(S5_CC, S5_UC),
                  pl.BlockSpec((16, 8, S5_CC), lambda j, t: (0, 0, j)),
                  pl.BlockSpec((1, S5_UC), lambda j, t: (0, j))] + [ANY] * len(extra),
        out_specs=[pl.BlockSpec((tm, S5_UC), lambda j, t: (nt - 1 - t, j)),
                   pl.BlockSpec((None, 2, S5_CC), lambda j, t: (j, 0, 0)),
                   chunk(S5_UC, S5_CC), chunk(S5_UC, S5_CC), chunk(S5_CC, S5_UC), chunk(S5_CC, S5_UC),
                   pl.BlockSpec((1, S5_UC), lambda j, t: (0, j))] + [ANY] * len(extra_out),
        out_shape=[jax.ShapeDtypeStruct(dz.shape, dz.dtype),
                   jax.ShapeDtypeStruct((S5_SPLIT, 2, S5_CC), F32),
                   jax.ShapeDtypeStruct((S5_SPLIT, S5_UC, S5_CC), F32),
                   jax.ShapeDtypeStruct((S5_SPLIT, S5_UC, S5_CC), F32),
                   jax.ShapeDtypeStruct((S5_SPLIT, S5_CC, S5_UC), F32),
                   jax.ShapeDtypeStruct((S5_SPLIT, S5_CC, S5_UC), F32),
                   jax.ShapeDtypeStruct((1, S5_WIDTH), F32)] + extra_out,
        scratch_shapes=[pltpu.VMEM((tm + 8, S5_CC), F32), pltpu.VMEM((tm + 8, S5_CC), F32),
                        pltpu.VMEM((tm, S5_CC), F32), pltpu.VMEM((tm, S5_CC), F32),
                        pltpu.VMEM((2, S5_CC), F32), pltpu.VMEM((8, S5_CC), F32), pltpu.VMEM((8, S5_CC), F32)]
        + extra_sems,
        input_output_aliases={2: 0},
        compiler_params=_cp("arbitrary" if carried is not None else "parallel", "arbitrary"),
    )(z, dy, dz, ckpt, hrb, hib, bbd_re, bbd_im, ccd_re, ccd_im, tab, dskip, *extra)


_EYE8 = np.eye(S5_GROUPS // S5_SPLIT, dtype=np.float32)


def _blockdiag(a):
    g, r, c = a.shape
    a = a.reshape(S5_SPLIT, g // S5_SPLIT, r, c)
    out = a[:, :, :, None, :] * _EYE8[None, :, None, :, None].astype(a.dtype)
    return out.reshape(S5_SPLIT, (g // S5_SPLIT) * r, (g // S5_SPLIT) * c)


def _blockdiag_extract(a, r, c):
    n = S5_GROUPS // S5_SPLIT
    a = a.reshape(S5_SPLIT, n, r, n, c)
    d = jnp.stack([a[:, k, :, k, :] for k in range(n)], axis=1)
    return d.reshape(S5_GROUPS, r, c)


def s5_mixer_core_fwd(z, lam_re, lam_im, log_dt, b_re, b_im, c_re, c_im, d_skip, carried=None):
    bt_re = jnp.swapaxes(b_re, 1, 2)
    bt_im = jnp.swapaxes(b_im, 1, 2)
    logdt = log_dt.reshape(S5_GROUPS, 1)
    bb_re, bb_im = s5_param_fwd(lam_re, lam_im, logdt, bt_re, bt_im)
    flat = lambda a: a.reshape(1, S5_COLS)
    tab = s5_tables(flat(lam_re), flat(lam_im), flat(jnp.broadcast_to(logdt, (S5_GROUPS, S5_STATE))))
    bbd_re = _blockdiag(bb_re).astype(BF16)
    bbd_im = _blockdiag(bb_im).astype(BF16)
    ccd_re = _blockdiag(jnp.swapaxes(c_re, 1, 2)).astype(BF16)
    ccd_im = _blockdiag(jnp.swapaxes(c_im, 1, 2)).astype(BF16)
    dsk = d_skip.reshape(1, S5_WIDTH)
    y, ckpt, hrb, hib, *landed = s5_fwd(z, bbd_re, bbd_im, ccd_re, ccd_im, tab, dsk, carried=carried)
    saved = (logdt, bt_re, bt_im, bbd_re, bbd_im, ccd_re, ccd_im, tab, dsk, ckpt, hrb, hib)
    return y, saved, landed


def s5_b_from_dense(dense):
    return jnp.swapaxes(dense.reshape(S5_GROUPS, S5_GROUP, S5_STATE), 1, 2)


def s5_mixer_core_bwd(z, dy, dz, lam_re, lam_im, saved, carried=None):
    logdt, bt_re, bt_im, bbd_re, bbd_im, ccd_re, ccd_im, tab, dsk, ckpt, hrb, hib = saved
    dz, da, dbr, dbi, dcr, dci, dd, *landed = s5_bwd(z, dy, dz, ckpt, hrb, hib, bbd_re, bbd_im, ccd_re, ccd_im, tab,
                                                     dsk, carried=carried)
    d_ab_re = da[:, 0, :].reshape(S5_GROUPS, S5_STATE)
    d_ab_im = da[:, 1, :].reshape(S5_GROUPS, S5_STATE)
    d_bb_re = _blockdiag_extract(dbr, S5_GROUP, S5_STATE)
    d_bb_im = _blockdiag_extract(dbi, S5_GROUP, S5_STATE)
    g_lr, g_li, g_ld, g_btr, g_bti = s5_param_bwd(lam_re, lam_im, logdt, bt_re, bt_im,
                                                  d_ab_re, d_ab_im, d_bb_re, d_bb_im)
    g_cre = jnp.swapaxes(_blockdiag_extract(dcr, S5_STATE, S5_GROUP), 1, 2)
    g_cim = jnp.swapaxes(_blockdiag_extract(dci, S5_STATE, S5_GROUP), 1, 2)
    grads = dict(lambda_re=g_lr, lambda_im=g_li, log_dt=g_ld.reshape(S5_GROUPS), b_re=g_btr, b_im=g_bti,
                 c_re=g_cre, c_im=g_cim, d=dd.reshape(S5_WIDTH))
    return dz, grads, landed


Z_U, Z_GA, Z_VAL, Z_GLU, Z_GB = range(5)
SUBLANES = 8


def _shifted_copies(buf, tm):
    n = tm + CONV_HALO - SUBLANES
    for r in range(1, SUBLANES):
        buf[r, 0:n, :] = buf[0, pl.ds(r, n), :]


CONV_ROWS = 32


def _shifted_rows(buf, start, rows, base=0):
    return buf[start % SUBLANES, pl.ds(base + (start - start % SUBLANES), rows), :]


def conv_fwd(z, conv_w, conv_b, tm=ROW_TILE):
    L = z.shape[0]
    tm = min(tm, L)
    nt = L // tm
    hb = tm // CONV_HALO
    C = CONV_WIDTH

    def body(val_ref, glu_ref, valh_ref, gluh_ref, w_ref, b_ref, c_ref, vsh):
        live = (pl.program_id(0) > 0).astype(F32)
        vsh[0, 0:CONV_HALO, :] = valh_ref[...] * _sigmoid(gluh_ref[...]) * live
        vsh[0, CONV_HALO:, :] = val_ref[...] * _sigmoid(glu_ref[...])
        _shifted_copies(vsh, tm)

        def rows(i, carry):
            base = pl.multiple_of(i * CONV_ROWS, CONV_ROWS)
            acc = jnp.broadcast_to(b_ref[...], (CONV_ROWS, C))
            for k in range(CONV_KERNEL):
                acc = acc + w_ref[k:k + 1, :] * _shifted_rows(vsh, CONV_HALO - CONV_KERNEL + 1 + k, CONV_ROWS, base)
            c_ref[pl.ds(base, CONV_ROWS), :] = acc
            return carry

        lax.fori_loop(0, tm // CONV_ROWS, rows, 0)

    cur = lambda col: pl.BlockSpec((tm, C), lambda t: (t, col))
    prev = lambda col: pl.BlockSpec((CONV_HALO, C), lambda t: (jnp.maximum(t * hb - 1, 0), col))
    return pl.pallas_call(
        body, name="conv_fwd", grid=(nt,),
        in_specs=[cur(Z_VAL), cur(Z_GLU), prev(Z_VAL), prev(Z_GLU), _full(conv_w.shape), _full(conv_b.shape)],
        out_specs=pl.BlockSpec((tm, C), lambda t: (t, 0)),
        out_shape=jax.ShapeDtypeStruct((L, C), F32),
        scratch_shapes=[pltpu.VMEM((8, tm + CONV_HALO, C), F32)],
        compiler_params=_cp("parallel"),
    )(z, z, z, z, conv_w, conv_b)


def conv_bwd(z, dc, dz, conv_w, tm=ROW_TILE, carried=None):
    L = z.shape[0]
    tm = min(tm, L)
    nt = L // tm
    hb = tm // CONV_HALO
    nh = L // CONV_HALO
    C = CONV_WIDTH
    off = CONV_HALO - CONV_KERNEL + 1

    def body(*refs):
        top = end = None
        if carried is not None:
            refs, parts = carried.split(refs, 8, 3, 3)
            top, end = carried.hooks(parts, (nt,))
            top()
        val_ref, glu_ref, valh_ref, gluh_ref, dc_ref, dcn_ref, dz_ref, w_ref, dvg_ref, dw_ref, db_ref, vsh, dsh, wacc = refs
        t = pl.program_id(0)

        @pl.when(t == 0)
        def _():
            wacc[...] = jnp.zeros_like(wacc)
            db_ref[...] = jnp.zeros_like(db_ref)

        val = val_ref[...]
        sg = _sigmoid(glu_ref[...])
        vsh[0, 0:CONV_HALO, :] = valh_ref[...] * _sigmoid(gluh_ref[...]) * (t > 0).astype(F32)
        vsh[0, CONV_HALO:, :] = val * sg
        dcv = dc_ref[...]
        dsh[0, 0:tm, :] = dcv
        dsh[0, tm:, :] = dcn_ref[...] * (t < nt - 1).astype(F32)
        _shifted_copies(vsh, tm)
        _shifted_copies(dsh, tm)

        def rows(i, carry):
            base = pl.multiple_of(i * CONV_ROWS, CONV_ROWS)
            dcr = dc_ref[pl.ds(base, CONV_ROWS), :]
            dv = jnp.zeros((CONV_ROWS, C), F32)
            for k in range(CONV_KERNEL):
                dv = dv + w_ref[k:k + 1, :] * _shifted_rows(dsh, CONV_KERNEL - 1 - k, CONV_ROWS, base)
                prod = dcr * _shifted_rows(vsh, off + k, CONV_ROWS, base)
                wacc[k] += jnp.sum(prod.reshape(CONV_ROWS // SUBLANES, SUBLANES, C), axis=0)
            valr = val_ref[pl.ds(base, CONV_ROWS), :]
            sgr = _sigmoid(glu_ref[pl.ds(base, CONV_ROWS), :])
            dvg_ref[pl.ds(base, CONV_ROWS), 0:C] = (dv * sgr).astype(BF16)
            dvg_ref[pl.ds(base, CONV_ROWS), C:] = (dv * valr * sgr * (1.0 - sgr)).astype(BF16)
            return carry

        lax.fori_loop(0, tm // CONV_ROWS, rows, 0)
        db_ref[...] += jnp.sum(dcv, axis=0, keepdims=True)

        @pl.when(t == nt - 1)
        def _():
            dw_ref[...] = jnp.sum(wacc[...], axis=1)

        if end is not None:
            end()

    extra = carried.arrays if carried is not None else []
    extra_out = carried.out_shapes if carried is not None else []
    extra_sems = carried.sems if carried is not None else []
    cur = lambda col: pl.BlockSpec((tm, C), lambda t: (t, col))
    prev = lambda col: pl.BlockSpec((CONV_HALO, C), lambda t: (jnp.maximum(t * hb - 1, 0), col))
    nxt = pl.BlockSpec((CONV_HALO, C), lambda t: (jnp.minimum((t + 1) * hb, nh - 1), 0))
    row = pl.BlockSpec((tm, C), lambda t: (t, 0))
    return pl.pallas_call(
        body, name="conv_bwd", grid=(nt,),
        in_specs=[cur(Z_VAL), cur(Z_GLU), prev(Z_VAL), prev(Z_GLU), row, nxt, ANY, _full(conv_w.shape)]
        + [ANY] * len(extra),
        out_specs=[pl.BlockSpec((tm, 2 * C), lambda t: (t, 1)), _full((CONV_HALO, C)), _full((1, C))]
        + [ANY] * len(extra_out),
        out_shape=[jax.ShapeDtypeStruct(dz.shape, dz.dtype),
                   jax.ShapeDtypeStruct((CONV_HALO, C), F32), jax.ShapeDtypeStruct((1, C), F32)] + extra_out,
        scratch_shapes=[pltpu.VMEM((8, tm + CONV_HALO, C), F32), pltpu.VMEM((8, tm + CONV_HALO, C), F32),
                        pltpu.VMEM((CONV_HALO, SUBLANES, C), F32)] + extra_sems,
        input_output_aliases={6: 0},
        compiler_params=_cp("arbitrary"),
    )(z, z, z, z, dc, dc, dz, conv_w, *extra)


def _ln_parts(c):
    mu = jnp.mean(c, axis=-1, keepdims=True)
    cc = c - mu
    rstd = lax.rsqrt(jnp.mean(cc * cc, axis=-1, keepdims=True) + EPS)
    return rstd, cc * rstd


def _ev_tail_branches(ys, c, wglu, bglu, lng, lnb):
    z1 = _gelu(ys)
    z1b = z1.astype(BF16)
    sg = _sigmoid(_dot_rows(z1b, wglu) + bglu)
    out = z1 * sg
    rstd, chat = _ln_parts(c)
    cn = chat * lng + lnb
    return z1, z1b, sg, out, rstd, chat, cn


def ev_tail_fwd(ys, z, c, x0, wglu, bglu, lng, lnb, wout, tm=ROW_TILE):
    L, D = x0.shape
    tm = min(tm, L)
    W = S5_WIDTH

    def body(ys_ref, ga_ref, c_ref, gb_ref, x_ref, wglu_ref, bglu_ref, lng_ref, lnb_ref, wout_ref, o_ref):
        _, _, _, out, _, _, cn = _ev_tail_branches(ys_ref[...], c_ref[...], wglu_ref, bglu_ref[...],
                                                   lng_ref[...], lnb_ref[...])
        ya = (out * _silu(ga_ref[...])).astype(BF16)
        yb = (_silu(cn) * _silu(gb_ref[...])).astype(BF16)
        o_ref[...] = x_ref[...] + _dot_rows(jnp.concatenate([ya, yb], axis=1), wout_ref)

    row = lambda n, col=0: pl.BlockSpec((tm, n), lambda t: (t, col))
    return pl.pallas_call(
        body, name="ev_tail_fwd", grid=(L // tm,),
        in_specs=[row(W), row(W, Z_GA), row(W), row(W, Z_GB), row(D), _full(wglu.shape), _full(bglu.shape),
                  _full(lng.shape), _full(lnb.shape), _full(wout.shape)],
        out_specs=row(D), out_shape=jax.ShapeDtypeStruct((L, D), F32), compiler_params=_cp("parallel"),
    )(ys, z, c, z, x0, wglu, bglu, lng, lnb, wout)


def ev_tail_bwd(ys, z, c, dx1, wglu, bglu, lng, lnb, wout, tm=ROW_TILE):
    L, D = dx1.shape
    tm = min(tm, L)
    W = S5_WIDTH

    def body(ys_ref, ga_ref, c_ref, gb_ref, dx_ref, wglu_ref, bglu_ref, lng_ref, lnb_ref, wout_ref,
             dys_ref, dc_ref, dz_ref, r_ref, z1_ref, dt_ref, dbg_ref, dlg_ref, dlb_ref):
        @pl.when(pl.program_id(0) == 0)
        def _():
            for r in (dbg_ref, dlg_ref, dlb_ref):
                r[...] = jnp.zeros_like(r)

        ys, ga, gb = ys_ref[...], ga_ref[...], gb_ref[...]
        z1, z1b, sg, out, rstd, chat, cn = _ev_tail_branches(ys, c_ref[...], wglu_ref, bglu_ref[...],
                                                             lng_ref[...], lnb_ref[...])
        (sga, dsga), (sgb, dsgb), (scn, dscn) = _silu_pair(ga), _silu_pair(gb), _silu_pair(cn)
        r_ref[:, 0:W] = (out * sga).astype(BF16)
        r_ref[:, W:] = (scn * sgb).astype(BF16)
        dr = _dot_nt_rows(dx_ref[...].astype(BF16), wout_ref)
        dra, drb = dr[:, 0:W], dr[:, W:]
        dz_ref[...] = jnp.zeros_like(dz_ref)
        dz_ref[:, Z_GA * W:(Z_GA + 1) * W] = (dra * out * dsga).astype(BF16)
        dout = dra * sga
        dt = dout * z1 * sg * (1.0 - sg)
        dtb = dt.astype(BF16)
        dz1 = dout * sg + _dot_nt_rows(dtb, wglu_ref)
        dys_ref[...] = dz1 * _dgelu(ys)
        z1_ref[...] = z1b
        dt_ref[...] = dtb
        dbg_ref[...] += jnp.sum(dt, axis=0, keepdims=True)
        dz_ref[:, Z_GB * W:(Z_GB + 1) * W] = (drb * scn * dsgb).astype(BF16)
        dcn = drb * sgb * dscn
        dlg_ref[...] += jnp.sum(dcn * chat, axis=0, keepdims=True)
        dlb_ref[...] += jnp.sum(dcn, axis=0, keepdims=True)
        dch = dcn * lng_ref[...]
        dc_ref[...] = rstd * (dch - jnp.mean(dch, axis=-1, keepdims=True)
                              - chat * jnp.mean(dch * chat, axis=-1, keepdims=True))

    row = lambda n, col=0: pl.BlockSpec((tm, n), lambda t: (t, col))
    f = lambda n, dt: jax.ShapeDtypeStruct((L, n), dt)
    vec = jax.ShapeDtypeStruct((1, W), F32)
    return pl.pallas_call(
        body, name="ev_tail_bwd", grid=(L // tm,),
        in_specs=[row(W), row(W, Z_GA), row(W), row(W, Z_GB), row(D), _full(wglu.shape), _full(bglu.shape),
                  _full(lng.shape), _full(lnb.shape), _full(wout.shape)],
        out_specs=[row(W), row(W), row(EVEN_IN), row(D), row(W), row(W), _full((1, W)), _full((1, W)), _full((1, W))],
        out_shape=[f(W, F32), f(W, F32), f(EVEN_IN, BF16), f(D, BF16), f(W, BF16), f(W, BF16), vec, vec, vec],
        compiler_params=_cp("arbitrary"),
    )(ys, z, c, z, dx1, wglu, bglu, lng, lnb, wout)


XA_SCALE = XA_HEAD_DIM ** -0.5


def _xa_forward(xv, g, wqg, kv):
    D = D_MODEL
    _, xhat = _rms_parts(xv)
    hb = (xhat * g).astype(BF16)
    qb = (_dot_cols(hb, wqg, (0, 1)) * XA_SCALE).astype(BF16)
    gate = _dot_cols(hb, wqg, (2, 3))
    ps, os_ = [], []
    for h in range(XA_HEADS):
        lo, hi = h * XA_HEAD_DIM, (h + 1) * XA_HEAD_DIM
        s = _dot_nt(qb[:, lo:hi], kv[:, lo:hi])
        e = jnp.exp(s - jnp.max(s, axis=-1, keepdims=True))
        inv = 1.0 / jnp.sum(e, axis=-1, keepdims=True)
        ps.append((e, inv))
        os_.append(_dot(e.astype(BF16), kv[:, D + lo:D + hi]) * inv)
    return hb, qb, gate, ps, jnp.concatenate(os_, axis=1)


def xa_fwd(x, g, wqg, kv, wo, layer, name, tm=MM_TILE):
    L, D = x.shape
    tm = min(tm, L)

    def body(x_ref, g_ref, wqg_ref, kv_ref, wo_ref, o_ref):
        xv = x_ref[...]
        _, _, gate, _, o = _xa_forward(xv, g_ref[...], wqg_ref, kv_ref[...])
        o_ref[...] = xv + _dot_rows((o * _silu(gate)).astype(BF16), wo_ref)

    row = pl.BlockSpec((tm, D), lambda t: (t, 0))
    return pl.pallas_call(
        body, name=name, grid=(L // tm,),
        in_specs=[row, _full(g.shape), _wspec(wqg, layer), _full(kv.shape), _wspec(wo, layer)],
        out_specs=row, out_shape=jax.ShapeDtypeStruct((L, D), F32), compiler_params=_cp("parallel"),
    )(x, g, wqg, kv, wo)


def _loss_head(xv, gv, tv):
    D = xv.shape[-1]
    _, xhat = _rms_parts(xv)
    err = xhat * gv - tv
    loss = 0.5 * jnp.sum(jnp.sum(err * err, axis=-1, keepdims=True), axis=0, keepdims=True) / D
    dx, dg = _rms_bwd(xv, gv, err * (1.0 / D))
    return loss, dx, dg


def xa_fwd_loss(x_in, r, wout, g, wqg, kv, wo, layer, target, gf, name, tm=MM_TILE):
    L, D = x_in.shape
    tm = min(tm, L)

    def body(xi_ref, r_ref, wout_ref, g_ref, wqg_ref, kv_ref, wo_ref, t_ref, gf_ref, x_ref, loss_ref, dx_ref, dg_ref):
        @pl.when(pl.program_id(0) == 0)
        def _():
            loss_ref[...] = jnp.zeros_like(loss_ref)
            dg_ref[...] = jnp.zeros_like(dg_ref)

        xv = xi_ref[...] + _dot_rows(r_ref[...], wout_ref)
        x_ref[...] = xv
        _, _, gate, _, o = _xa_forward(xv, g_ref[...], wqg_ref, kv_ref[...])
        y = xv + _dot_rows((o * _silu(gate)).astype(BF16), wo_ref)
        loss, dx, dg = _loss_head(y, gf_ref[...], t_ref[...])
        loss_ref[...] += loss
        dx_ref[...] = dx
        dg_ref[...] += dg

    row = pl.BlockSpec((tm, D), lambda t: (t, 0))
    return pl.pallas_call(
        body, name=name, grid=(L // tm,),
        in_specs=[row, row, _full(wout.shape), _full(g.shape), _wspec(wqg, layer), _full(kv.shape), _wspec(wo, layer),
                  row, _full(gf.shape)],
        out_specs=[row, _full((1, 128)), row, _full((1, D))],
        out_shape=[jax.ShapeDtypeStruct((L, D), F32), jax.ShapeDtypeStruct((1, 128), F32),
                   jax.ShapeDtypeStruct((L, D), F32), jax.ShapeDtypeStruct((1, D), F32)],
        compiler_params=_cp("arbitrary"),
    )(x_in, r, wout, g, wqg, kv, wo, target, gf)


def xa_bwd(x, dxo, g, wqg, kv, wo, layer, name, tm=MM_TILE):
    L, D = x.shape
    tm = min(tm, L)

    def body(x_ref, dxo_ref, g_ref, wqg_ref, kv_ref, wo_ref, dx_ref, dqg_ref, h_ref, r_ref, dkv_ref, dg_ref):
        @pl.when(pl.program_id(0) == 0)
        def _():
            dkv_ref[...] = jnp.zeros_like(dkv_ref)
            dg_ref[...] = jnp.zeros_like(dg_ref)

        xv = x_ref[...]
        kv = kv_ref[...]
        hb, qb, gate, ps, o = _xa_forward(xv, g_ref[...], wqg_ref, kv)
        sgate, dsgate = _silu_pair(gate)
        h_ref[...] = hb
        r_ref[...] = (o * sgate).astype(BF16)
        dxo = dxo_ref[...]
        dr = _dot_nt_rows(dxo.astype(BF16), wo_ref)
        do = dr * sgate
        dqg_ref[:, D:] = (dr * o * dsgate).astype(BF16)
        dob = do.astype(BF16)
        doo = do * o
        for h in range(XA_HEADS):
            lo, hi = h * XA_HEAD_DIM, (h + 1) * XA_HEAD_DIM
            e, inv = ps[h]
            dp = _dot_nt(dob[:, lo:hi], kv[:, D + lo:D + hi])
            dkv_ref[:, D + lo:D + hi] += _dot_tn(e.astype(BF16), (do[:, lo:hi] * inv).astype(BF16))
            rs = jnp.sum(doo[:, lo:hi], axis=-1, keepdims=True)
            dsb = (e * ((dp - rs) * inv)).astype(BF16)
            dqg_ref[:, lo:hi] = (_dot(dsb, kv[:, lo:hi]) * XA_SCALE).astype(BF16)
            dkv_ref[:, lo:hi] += _dot_tn(dsb, qb[:, lo:hi])
        dh = _dot_nt_cols(_col_pieces(dqg_ref[...], D // 2), wqg_ref)
        dx, dg = _rms_bwd(xv, g_ref[...], dh)
        dx_ref[...] = dxo + dx
        dg_ref[...] += dg

    row = lambda n: pl.BlockSpec((tm, n), lambda t: (t, 0))
    return pl.pallas_call(
        body, name=name, grid=(L // tm,),
        in_specs=[row(D), row(D), _full(g.shape), _wspec(wqg, layer), _full(kv.shape), _wspec(wo, layer)],
        out_specs=[row(D), row(2 * D), row(D), row(D), _full(kv.shape), _full((1, D))],
        out_shape=[jax.ShapeDtypeStruct((L, D), F32), jax.ShapeDtypeStruct((L, 2 * D), BF16),
                   jax.ShapeDtypeStruct((L, D), BF16), jax.ShapeDtypeStruct((L, D), BF16),
                   jax.ShapeDtypeStruct(kv.shape, F32), jax.ShapeDtypeStruct((1, D), F32)],
        compiler_params=_cp("arbitrary"),
    )(x, dxo, g, wqg, kv, wo)


ATT_SCALE = ATT_HEAD_DIM ** -0.5
ATT_PAIRS = ATT_HEADS // 2
SKEW_LANES = 1024
REL_LANES = 384


def _skew(x, left):
    amt = (ATT_QB - 1) - lax.broadcasted_iota(jnp.int32, (ATT_QB, 1), 0)
    for bit in range(8):
        sh = (SKEW_LANES - (1 << bit)) if left else (1 << bit)
        x = jnp.where(((amt >> bit) & 1) == 1, pltpu.roll(x, sh, 1), x)
    return x


def _dist_onehot(shape, dist_axis):
    j = lax.broadcasted_iota(jnp.int32, shape, dist_axis)
    r = lax.broadcasted_iota(jnp.int32, shape, 1 - dist_axis)
    return (jnp.clip((ATT_WIN - 1) - j, -MAX_REL, MAX_REL) + MAX_REL == r).astype(BF16)


def _dot_exact(v, onehot):
    acc = jnp.zeros((v.shape[0], onehot.shape[1]), F32)
    rem = v
    for _ in range(3):
        part = rem.astype(BF16)
        acc = acc + _dot(part, onehot)
        rem = rem - part.astype(F32)
    return acc


ATT_EDGE = ATT_PAD // ATT_QB


def att_bias(rel_bias, carried=None):
    H = rel_bias.shape[0]
    rb = jnp.pad(rel_bias, ((0, 0), (0, REL_LANES - rel_bias.shape[1]))).reshape(H, 1, REL_LANES)

    def body(*refs):
        top = end = None
        if carried is not None:
            refs, parts = carried.split(refs, 1, 1, 0)
            top, end = carried.hooks(parts, (H,))
            top()
        rb_ref, o_ref = refs
        by_col = _dot_exact(jnp.broadcast_to(rb_ref[...], (8, REL_LANES)), _dist_onehot((REL_LANES, SKEW_LANES), 1))
        x = _skew(jnp.broadcast_to(by_col[0:1, :], (ATT_QB, SKEW_LANES)), left=True)[:, 0:ATT_WIN]
        qc = lax.broadcasted_iota(jnp.int32, (ATT_QB, 1), 0) // CHUNK + LEFT_CHUNKS
        col = lax.broadcasted_iota(jnp.int32, (1, ATT_WIN), 1)
        dc = qc - col // CHUNK
        band = (dc >= 0) & (dc <= LEFT_CHUNKS)
        for blk in range(ATT_EDGE + 1):
            o_ref[blk] = jnp.where(band & (col >= ATT_PAD - blk * ATT_QB), x, NEG)
        if end is not None:
            end()

    extra = carried.arrays if carried is not None else []
    extra_out = carried.out_shapes if carried is not None else []
    extra_sems = carried.sems if carried is not None else []
    return pl.pallas_call(
        body, name="att_bias", grid=(H,),
        in_specs=[pl.BlockSpec((None, 1, REL_LANES), lambda h: (h, 0, 0))] + [ANY] * len(extra),
        out_specs=[pl.BlockSpec((ATT_EDGE + 1, None, ATT_QB, ATT_WIN), lambda h: (0, h, 0, 0))] + [ANY] * len(extra_out),
        out_shape=[jax.ShapeDtypeStruct((ATT_EDGE + 1, H, ATT_QB, ATT_WIN), F32)] + extra_out,
        scratch_shapes=extra_sems,
        compiler_params=_cp("arbitrary" if carried is not None else "parallel"),
    )(rb, *extra)


def relbias_bwd(dbias):
    H = dbias.shape[0]

    def body(x_ref, o_ref):
        x = jnp.concatenate([x_ref[...], jnp.zeros((ATT_QB, SKEW_LANES - ATT_WIN), F32)], axis=1)
        col = jnp.sum(_skew(x, left=False), axis=0, keepdims=True)
        o_ref[...] = _dot_exact(jnp.broadcast_to(col, (8, SKEW_LANES)), _dist_onehot((SKEW_LANES, REL_LANES), 0))

    out = pl.pallas_call(
        body, name="relbias_bwd", grid=(H,),
        in_specs=[pl.BlockSpec((None, ATT_QB, ATT_WIN), lambda h: (h, 0, 0))],
        out_specs=pl.BlockSpec((None, 8, REL_LANES), lambda h: (h, 0, 0)),
        out_shape=jax.ShapeDtypeStruct((H, 8, REL_LANES), F32), compiler_params=_cp("parallel"),
    )(dbias)
    return out[:, 0, :2 * MAX_REL + 1]


def _ca_scores(qh, kw, bias):
    s = _dot_nt(qh, kw) + bias
    e = jnp.exp(s - jnp.max(s, axis=-1, keepdims=True))
    return e, 1.0 / jnp.sum(e, axis=-1, keepdims=True)


def _ca_head(qv, m):
    return jnp.where(m, qv, jnp.zeros_like(qv)) * ATT_SCALE


def _ca_bias_spec():
    return pl.BlockSpec((None, 2, ATT_QB, ATT_WIN), lambda hp, b: (jnp.minimum(b, ATT_EDGE), hp, 0, 0))


def ca_fwd(q, kvp, gate, bias):
    L, D = q.shape
    Lp = kvp.shape[0]
    nb = L // ATT_QB

    PP = 2
    W = PP * 128

    def body(q_ref, k_ref, v_ref, g_ref, b_ref, r_ref, o_ref):
        w = pl.multiple_of(pl.program_id(1) * ATT_QB, ATT_QB)
        first = lax.broadcasted_iota(jnp.int32, (1, 128), 1) < ATT_HEAD_DIM
        for pp in range(PP):
            sl = slice(pp * 128, (pp + 1) * 128)
            kw = k_ref[pl.ds(w, ATT_WIN), sl]
            vw = v_ref[pl.ds(w, ATT_WIN), sl]
            qv = q_ref[:, sl]
            outs = []
            for hh, m in enumerate((first, jnp.logical_not(first))):
                e, inv = _ca_scores(_ca_head(qv, m), kw, b_ref[2 * pp + hh])
                outs.append(_dot(e.astype(BF16), vw) * inv)
            o = jnp.where(first, outs[0], outs[1])
            r_ref[:, sl] = (o * _silu(g_ref[:, sl])).astype(BF16)
            o_ref[:, sl] = o.astype(BF16)

    blk = pl.BlockSpec((ATT_QB, W), lambda hp, b: (b, hp))
    bias_blk = pl.BlockSpec((None, 2 * PP, ATT_QB, ATT_WIN), lambda hp, b: (jnp.minimum(b, ATT_EDGE), hp, 0, 0))
    return pl.pallas_call(
        body, name="ca_fwd", grid=(ATT_PAIRS // PP, nb),
        in_specs=[blk, pl.BlockSpec((Lp, W), lambda hp, b: (0, hp)),
                  pl.BlockSpec((Lp, W), lambda hp, b: (0, ATT_PAIRS // PP + hp)), blk, bias_blk],
        out_specs=[blk, blk], out_shape=[jax.ShapeDtypeStruct((L, D), BF16), jax.ShapeDtypeStruct((L, D), BF16)],
        compiler_params=_cp("parallel", "arbitrary"),
    )(q, kvp, kvp, gate, bias)


def ca_bwd(q, kvp, gate, bias, dr, o):
    L, D = q.shape
    Lp = kvp.shape[0]
    nb = L // ATT_QB

    def body(q_ref, k_ref, v_ref, g_ref, b_ref, dr_ref, o_ref, dq_ref, dg_ref, dkb_ref, dvb_ref, db_ref,
             dk_ref, dv_ref):
        b = pl.program_id(1)

        @pl.when(b == 0)
        def _():
            for r in (dk_ref, dv_ref, db_ref):
                r[...] = jnp.zeros_like(r)

        w = pl.multiple_of(b * ATT_QB, ATT_QB)
        kw = k_ref[pl.ds(w, ATT_WIN), :]
        vw = v_ref[pl.ds(w, ATT_WIN), :]
        qv = q_ref[...]
        gate_v = g_ref[...]
        drv = dr_ref[...]
        o = o_ref[...].astype(F32)
        sgate, dsgate = _silu_pair(gate_v)
        do = drv * sgate
        doo = do * o
        first = lax.broadcasted_iota(jnp.int32, (1, 128), 1) < ATT_HEAD_DIM
        dqs = []
        dkw = jnp.zeros((ATT_WIN, 128), F32)
        dvw = jnp.zeros((ATT_WIN, 128), F32)
        for hh, m in enumerate((first, jnp.logical_not(first))):
            qh = _ca_head(qv, m)
            e, inv = _ca_scores(qh, kw, b_ref[hh])
            eb = e.astype(BF16)
            doh = jnp.where(m, do, 0.0)
            dp = _dot_nt(doh.astype(BF16), vw)
            dvw = dvw + _dot_tn(eb, (doh * inv).astype(BF16))
            rs = jnp.sum(jnp.where(m, doo, 0.0), axis=-1, keepdims=True)
            ds = e * ((dp - rs) * inv)
            db_ref[hh] += ds
            dsb = ds.astype(BF16)
            dqs.append(_dot(dsb, kw))
            dkw = dkw + _dot_tn(dsb, qh)
        dg_ref[...] = (drv * o * dsgate).astype(BF16)
        dq_ref[...] = (jnp.where(first, dqs[0], dqs[1]) * ATT_SCALE).astype(BF16)
        dk_ref[pl.ds(w, ATT_WIN), :] += dkw
        dv_ref[pl.ds(w, ATT_WIN), :] += dvw

        @pl.when(b == nb - 1)
        def _():
            dkb_ref[...] = dk_ref[...].astype(BF16)
            dvb_ref[...] = dv_ref[...].astype(BF16)

    blk = pl.BlockSpec((ATT_QB, 128), lambda hp, b: (b, hp))
    kblk = pl.BlockSpec((Lp, 128), lambda hp, b: (0, hp))
    vblk = pl.BlockSpec((Lp, 128), lambda hp, b: (0, ATT_PAIRS + hp))
    bblk = pl.BlockSpec((2, ATT_QB, ATT_WIN), lambda hp, b: (hp, 0, 0))
    return pl.pallas_call(
        body, name="ca_bwd", grid=(ATT_PAIRS, nb),
        in_specs=[blk, kblk, vblk, blk, _ca_bias_spec(), blk, blk],
        out_specs=[blk, blk, kblk, kblk, bblk],
        out_shape=[jax.ShapeDtypeStruct((L, D), BF16), jax.ShapeDtypeStruct((L, D), BF16),
                   jax.ShapeDtypeStruct((Lp, D), BF16), jax.ShapeDtypeStruct((Lp, D), BF16),
                   jax.ShapeDtypeStruct(bias.shape[1:], F32)],
        scratch_shapes=[pltpu.VMEM((Lp, 128), F32), pltpu.VMEM((Lp, 128), F32)],
        compiler_params=_cp("parallel", "arbitrary"),
    )(q, kvp, kvp, gate, bias, dr, o)


_ADAM_C1 = 1.0 / (1.0 - ADAM_B1 ** ADAM_STEP)
_ADAM_C2 = 1.0 / (1.0 - ADAM_B2 ** ADAM_STEP)


def _adam_update(w, g, m, v):
    mn = ADAM_B1 * m + (1.0 - ADAM_B1) * g
    vn = ADAM_B2 * v + (1.0 - ADAM_B2) * g * g
    delta = -ADAM_LR * ((mn * _ADAM_C1) / (jnp.sqrt(vn * _ADAM_C2) + ADAM_EPS) + ADAM_WD * w)
    return delta, mn, vn


def adamw(w, g, m, v, name, tr=512):
    R, C = w.shape
    tr = min(tr, R)

    def body(w_ref, g_ref, m_ref, v_ref, d_ref, mo_ref, vo_ref):
        d_ref[...], mo_ref[...], vo_ref[...] = _adam_update(w_ref[...], g_ref[...], m_ref[...], v_ref[...])

    blk = pl.BlockSpec((tr, C), lambda i: (i, 0))
    sh = jax.ShapeDtypeStruct((R, C), F32)
    return pl.pallas_call(
        body, name=name, grid=(R // tr,), in_specs=[blk] * 4, out_specs=[blk] * 3,
        out_shape=[sh] * 3, compiler_params=_cp("parallel"),
    )(w, g, m, v)


def adamw_allreduce(gathered, w, m, v, shard, name, slot=None):
    R, C = w.shape
    sharded = slot is None and gathered.shape[2] != C

    def body(s_ref, ga_ref, w_ref, m_ref, v_ref, g_ref, d_ref, mo_ref, vo_ref):
        take = (lambda d: ga_ref[d]) if slot is None else (lambda d: ga_ref[d, slot:slot + R, 0:C])
        g = take(0)
        for d in range(1, N_DEV):
            g = g + take(d)
        g_ref[...] = g
        d_ref[...], mo_ref[...], vo_ref[...] = _adam_update(w_ref[...], g, m_ref[...], v_ref[...])

    blk = pl.BlockSpec((R, C), lambda i, s_ref: (0, 0))
    if slot is not None:
        gblk = pl.BlockSpec(gathered.shape, lambda i, s_ref: (0, 0, 0))
    else:
        gblk = pl.BlockSpec((N_DEV, R, C),
                            (lambda i, s_ref: (0, 0, s_ref[0])) if sharded else (lambda i, s_ref: (0, 0, 0)))
    sh = jax.ShapeDtypeStruct((R, C), F32)
    return pl.pallas_call(
        body, name=name,
        grid_spec=pltpu.PrefetchScalarGridSpec(num_scalar_prefetch=1, grid=(1,), in_specs=[gblk, blk, blk, blk],
                                               out_specs=[blk] * 4),
        out_shape=[sh] * 4, compiler_params=_cp("arbitrary"),
    )(shard, gathered, w, m, v)


LATE = ("ev_s5_glu_w", "ev_w_out", "od_w_in", "od_w_out", "xa_w_qg", "xa_w_kv", "xa_w_o")
EARLY_GRADS = ("od_w_in", "od_w_out", "xa_w_qg", "xa_w_kv", "xa_w_o", "ev_w_out", "ev_s5_glu_w")


def _reduce_to_chip(gs, names, core, tag):
    from_sibling = sibling_send_other_half(gs, "sibling_send_" + tag)
    return [sum_with_sibling(gi, ri, core, "sum_sibling_" + n) for n, gi, ri in zip(names, gs, from_sibling)]


def local_step(x, mem, target, p, gw, late, bias, place, core):
    row = lambda a: a.reshape(1, -1)
    D = D_MODEL
    L = x.shape[0]
    g, big = {}, {}
    gw = dict(gw)

    z, h0b = norm_mm(x, p["ev_norm_g"], gw["ev_w_in"], [((0, 1, 2, 3), F32, 0)], "ev_in")
    ys, s5_saved, landed = s5_mixer_core_fwd(
        z, p["ev_s5_lambda_re"][0], p["ev_s5_lambda_im"][0], p["ev_s5_log_dt"][0], p["ev_s5_b_re"][0],
        p["ev_s5_b_im"][0], p["ev_s5_c_re"][0], p["ev_s5_c_im"][0], p["ev_s5_d"][0],
        carried=carried_allgather([late[n] for n in LATE]))
    for n, gth in zip(LATE, landed):
        rows = gth.shape[1]
        gw[n] = gth.reshape(N_CHIPS, 2, rows // 2, gth.shape[2]) if n.startswith("xa_") else gth
    memn_b = rms_fwd(mem, row(p["mem_norm_g"]), "mem_norm")
    kvs = [mm_cols(memn_b, gw["xa_w_kv"], l, f"xa_kv{l}", BF16) for l in range(2)]
    conv_w = p["ev_conv_w"][0]
    c = conv_fwd(z, conv_w, p["ev_conv_b"])
    tail = (gw["ev_s5_glu_w"], p["ev_s5_glu_b"], p["ev_conv_ln_g"], p["ev_conv_ln_b"], gw["ev_w_out"])
    x1 = ev_tail_fwd(ys, z, c, x, *tail)
    xa0 = (row(p["xa_norm_g"][0]), gw["xa_w_qg"], kvs[0], gw["xa_w_o"], 0)
    x2 = xa_fwd(x1, *xa0, "xa_fwd0")

    q, kvp, gate, h1b = norm_mm(x2, p["od_norm_g"], gw["od_w_in"],
                                [((0,), BF16, 0), ((1, 2), BF16, ATT_PAD), ((3,), F32, 0)], "od_in")
    kvp = zero_rows(kvp, ATT_PAD, "od_kv_pad")
    r, att_o = ca_fwd(q, kvp, gate, bias)
    xa1 = (row(p["xa_norm_g"][1]), gw["xa_w_qg"], kvs[1], gw["xa_w_o"], 1)
    x3, loss, dx4, dgf = xa_fwd_loss(x2, r, gw["od_w_out"], *xa1, target, row(p["final_norm_g"]), "od_out_xa_fwd1_loss")
    g["final_norm_g"] = dgf.reshape(D)

    dx3, dqg1, hx1, rx1, dkv1, dgxa1 = xa_bwd(x3, dx4, *xa1, "xa_bwd1")
    dwqg = mm_tn(hx1, dqg1, "xa_dwqg1", ("cols", 1))
    dwo = mm_tn(rx1, dx4, "xa_dwo1", ("rows", 1))

    big["od_w_out"] = mm_tn(r, dx3, "od_dwout", ("rows",))
    dr = mm_nt_rows(dx3, gw["od_w_out"], "od_out_bwd")
    dq, dgate, dkp, dvp, dbias = ca_bwd(q, kvp, gate, bias, dr, att_o)
    pieces, offs = (dq, dkp, dvp, dgate), (0, ATT_PAD, ATT_PAD, 0)
    dwin = None
    for s in range(N_CHIPS):
        dwin = mm_tn(h1b, pieces[s], f"od_dwin{s}", ("slab", s), into=dwin, b_off=offs[s],
                     bl=ATT_PAD if offs[s] else 1024)
    big["od_w_in"] = dwin
    dx2, dgod = mm_nt_normbwd(pieces, offs, gw["od_w_in"], x2, p["od_norm_g"], dx3, "od_in_bwd")
    g["od_norm_g"] = dgod
    g["od_rel_bias"] = relbias_bwd(dbias)[None]

    dx1, dqg0, hx0, rx0, dkv0, dgxa0 = xa_bwd(x1, dx2, *xa0, "xa_bwd0")
    big["xa_w_qg"] = mm_tn(hx0, dqg0, "xa_dwqg0", ("cols", 0), into=dwqg)
    big["xa_w_o"] = mm_tn(rx0, dx2, "xa_dwo0", ("rows", 0), into=dwo)
    g["xa_norm_g"] = jnp.concatenate([dgxa0, dgxa1], axis=0)

    dys, dc, dz, ra, z1b, dtb, dbglu, dlng, dlnb = ev_tail_bwd(ys, z, c, dx1, *tail)
    big["ev_w_out"] = mm_tn(ra, dx1, "ev_dwout", ("rows",))
    big["ev_s5_glu_w"] = mm_tn(z1b, dtb, "ev_dwglu", ("rows",))
    g["ev_s5_glu_b"], g["ev_conv_ln_g"], g["ev_conv_ln_b"] = dbglu, dlng, dlnb
    dwkv = mm_tn(memn_b, dkv1, "xa_dwkv1", ("cols", 1), bl=MEM_LEN)
    big["xa_w_kv"] = mm_tn(memn_b, dkv0, "xa_dwkv0", ("cols", 0), into=dwkv, bl=MEM_LEN)
    dmem0 = mm_nt_cols(dkv0, gw["xa_w_kv"], 0, "xa_kv_bwd0")
    dmem1 = mm_nt_cols(dkv1, gw["xa_w_kv"], 1, "xa_kv_bwd1")
    g["mem_norm_g"] = rms_dgain(mem, dmem0, dmem1, "mem_norm_bwd").reshape(D)

    shard_major = lambda t: t.reshape((-1,) + t.shape[-2:])
    gs = [shard_major(big[n]) for n in EARLY_GRADS]
    dz, dconvw, dconvb, *from_sibling = conv_bwd(z, dc, dz, conv_w, carried=carried_sibling_send(gs))
    g["ev_conv_w"] = dconvw[None, :CONV_KERNEL]
    g["ev_conv_b"] = dconvb
    chip_sums = [sum_with_sibling(gi, ri, core, "sum_sibling_" + n) for n, gi, ri in zip(EARLY_GRADS, gs, from_sibling)]
    dz, s5g, from_chips = s5_mixer_core_bwd(z, dys, dz, p["ev_s5_lambda_re"][0], p["ev_s5_lambda_im"][0], s5_saved,
                                            carried=carried_chips_exchange(chip_sums))
    reduced = {n: sum_chips(ci, ri, place, "sum_chips_" + n) for n, ci, ri in zip(EARLY_GRADS, chip_sums, from_chips)}
    for n, v in s5g.items():
        g["ev_s5_" + n] = v[None]
    packed, slots = pack_rows([_as2d(g[n]) for n in PACKED_SMALL], "pack_small_grads")
    dwin_ev, *gathered = mm_tn(h0b, dz, "ev_dwin", ("cols",),
                               carried=carried_allgather_devices([packed] + [_as2d(g[n]) for n in SINGLE_SMALL]))
    grad_x, dgev = mm_nt_normbwd((dz,), (0,), gw["ev_w_in"], x, p["ev_norm_g"], dx1, "ev_in_bwd")
    chip_sum = _reduce_to_chip([dwin_ev], ["ev_w_in"], core, "last")
    reduced["ev_w_in"] = sum_chips(chip_sum[0], chips_exchange(chip_sum)[0], place, "sum_chips_ev_w_in")
    return loss, grad_x, g, reduced, dgev, gathered, slots


def _me():
    return lax.axis_index("x"), lax.axis_index("y"), lax.axis_index("c")


def _other_chips(x, y):
    return [(1 - x, y), (x, 1 - y), (1 - x, 1 - y)]


def _remote(src, dst, send_sems, recv_sems, k, to):
    return pltpu.make_async_remote_copy(src_ref=src, dst_ref=dst, send_sem=send_sems.at[k], recv_sem=recv_sems.at[k],
                                        device_id=to, device_id_type=MESH)


def _rows_half(ref, h):
    H = ref.shape[-2] // 2
    return ref.at[(slice(None),) * (len(ref.shape) - 2) + (pl.ds(h * H, H), slice(None))]


def allgather_devices(vs):
    n = len(vs)

    def body(*refs):
        ins, outs = refs[:n], refs[n:2 * n]
        send_sems, recv_sems, local_sems = refs[2 * n:]
        x, y, c = _me()
        sib = (x, y, 1 - c)
        chips = _other_chips(x, y)
        me = 4 * x + 2 * y + c
        local = [pltpu.make_async_copy(ins[i], outs[i].at[me], local_sems.at[i]) for i in range(n)]
        for cp in local:
            cp.start()
        first, passed = [], []
        for i in range(n):
            first.append(_remote(ins[i], outs[i].at[me], send_sems, recv_sems, 7 * i, sib))
            for j, (cx, cy) in enumerate(chips):
                first.append(_remote(ins[i], outs[i].at[me], send_sems, recv_sems, 7 * i + 1 + j, (cx, cy, c)))
        for cp in first:
            cp.start()
        for j, (cx, cy) in enumerate(chips):
            for i in range(n):
                got = outs[i].at[4 * cx + 2 * cy + c]
                _remote(got, got, send_sems, recv_sems, 7 * i + 1 + j, (cx, cy, c)).wait_recv()
                fw = _remote(got, got, send_sems, recv_sems, 7 * i + 4 + j, sib)
                fw.start()
                passed.append(fw)
        for i in range(n):
            got = outs[i].at[4 * x + 2 * y + (1 - c)]
            _remote(got, got, send_sems, recv_sems, 7 * i, sib).wait_recv()
            for j, (cx, cy) in enumerate(chips):
                got = outs[i].at[4 * cx + 2 * cy + (1 - c)]
                _remote(got, got, send_sems, recv_sems, 7 * i + 4 + j, sib).wait_recv()
        for cp in first + passed:
            cp.wait_send()
        for cp in local:
            cp.wait()

    return pl.pallas_call(
        body, name="allgather_devices", in_specs=[ANY] * n, out_specs=[ANY] * n,
        out_shape=[jax.ShapeDtypeStruct((N_DEV,) + v.shape, v.dtype) for v in vs],
        scratch_shapes=[pltpu.SemaphoreType.DMA((7 * n,)), pltpu.SemaphoreType.DMA((7 * n,)),
                        pltpu.SemaphoreType.DMA((n,))],
    )(*vs)


def sibling_send_other_half(gs, name):
    n = len(gs)

    def body(*refs):
        ins, outs = refs[:n], refs[n:2 * n]
        send_sems, recv_sems = refs[2 * n:]
        x, y, c = _me()
        cps = [_remote(_rows_half(ins[i], 1 - c), outs[i], send_sems, recv_sems, i, (x, y, 1 - c)) for i in range(n)]
        for cp in cps:
            cp.start()
        for cp in cps:
            cp.wait()

    return pl.pallas_call(
        body, name=name, in_specs=[ANY] * n, out_specs=[ANY] * n,
        out_shape=[jax.ShapeDtypeStruct((g.shape[0], g.shape[1] // 2, g.shape[2]), g.dtype) for g in gs],
        scratch_shapes=[pltpu.SemaphoreType.DMA((n,)), pltpu.SemaphoreType.DMA((n,))],
    )(*gs)


def chips_exchange(parts):
    n = len(parts)

    def body(*refs):
        ins, outs = refs[:n], refs[n:2 * n]
        send_sems, recv_sems = refs[2 * n:]
        x, y, c = _me()
        cps = []
        for i in range(n):
            nl = ins[i].shape[0] // N_CHIPS
            for j, (cx, cy) in enumerate(_other_chips(x, y)):
                cps.append(_remote(ins[i].at[pl.ds((2 * cx + cy) * nl, nl)], outs[i].at[j], send_sems, recv_sems,
                                   3 * i + j, (cx, cy, c)))
        for cp in cps:
            cp.start()
        for cp in cps:
            cp.wait()

    return pl.pallas_call(
        body, name="chips_exchange", in_specs=[ANY] * n, out_specs=[ANY] * n,
        out_shape=[jax.ShapeDtypeStruct((3, a.shape[0] // N_CHIPS) + a.shape[1:], a.dtype) for a in parts],
        scratch_shapes=[pltpu.SemaphoreType.DMA((3 * n,)), pltpu.SemaphoreType.DMA((3 * n,))],
    )(*parts)


def sibling_share(fulls):
    n = len(fulls)

    def body(*refs):
        outs = refs[n:2 * n]
        send_sems, recv_sems = refs[2 * n:]
        x, y, c = _me()
        cps = [_remote(_rows_half(outs[i], c), _rows_half(outs[i], c), send_sems, recv_sems, i, (x, y, 1 - c))
               for i in range(n)]
        for cp in cps:
            cp.start()
        for i in range(n):
            got = _rows_half(outs[i], 1 - c)
            _remote(got, got, send_sems, recv_sems, i, (x, y, 1 - c)).wait_recv()
        for cp in cps:
            cp.wait_send()

    return pl.pallas_call(
        body, name="sibling_share", in_specs=[ANY] * n, out_specs=[ANY] * n,
        out_shape=[jax.ShapeDtypeStruct(f.shape, f.dtype) for f in fulls],
        input_output_aliases={i: i for i in range(n)},
        scratch_shapes=[pltpu.SemaphoreType.DMA((n,)), pltpu.SemaphoreType.DMA((n,))],
    )(*fulls)


def sum_with_sibling(g, recv, core, name):
    S, H, C = recv.shape
    tr = min(512, H)

    def body(c_ref, g_ref, r_ref, o_ref):
        o_ref[...] = (g_ref[...].astype(F32) + r_ref[...].astype(F32)).astype(o_ref.dtype)

    nb = H // tr
    return pl.pallas_call(
        body, name=name,
        grid_spec=pltpu.PrefetchScalarGridSpec(
            num_scalar_prefetch=1, grid=(S, nb),
            in_specs=[pl.BlockSpec((None, tr, C), lambda s, i, c_ref: (s, c_ref[0] * nb + i, 0)),
                      pl.BlockSpec((None, tr, C), lambda s, i, c_ref: (s, i, 0))],
            out_specs=pl.BlockSpec((None, tr, C), lambda s, i, c_ref: (s, i, 0))),
        out_shape=jax.ShapeDtypeStruct((S, H, C), g.dtype), compiler_params=_cp("parallel", "parallel"),
    )(core, g, recv)


def sum_chips(a, recv, place, name):
    _, nl, H, C = recv.shape
    tr = min(512, H)
    nb = H // tr

    def body(p_ref, a_ref, r_ref, o_ref):
        acc = a_ref[...].astype(F32)
        for j in range(3):
            acc = acc + r_ref[j].astype(F32)
        o_ref[...] = acc

    return pl.pallas_call(
        body, name=name,
        grid_spec=pltpu.PrefetchScalarGridSpec(
            num_scalar_prefetch=1, grid=(nl, nb),
            in_specs=[pl.BlockSpec((None, tr, C), lambda l, i, p_ref: (p_ref[0] * nl + l, i, 0)),
                      pl.BlockSpec((3, None, tr, C), lambda l, i, p_ref: (0, l, i, 0))],
            out_specs=pl.BlockSpec((None, tr, C), lambda l, i, p_ref: (l, p_ref[1] * nb + i, 0))),
        out_shape=jax.ShapeDtypeStruct((nl, 2 * H, C), F32), compiler_params=_cp("parallel", "parallel"),
    )(place, a, recv)


def pack_rows(arrays, name):
    starts, r0 = [], 0
    for a in arrays:
        if a.shape[0] >= SUBLANES:
            r0 = -(-r0 // SUBLANES) * SUBLANES
        starts.append(r0)
        r0 += a.shape[0]
    r0 = -(-r0 // SUBLANES) * SUBLANES
    n = len(arrays)

    def body(*refs):
        o_ref = refs[n]
        o_ref[...] = jnp.zeros_like(o_ref)
        for a_ref, s in zip(refs[:n], starts):
            r, c = a_ref.shape
            o_ref[s:s + r, 0:c] = a_ref[...]

    out = pl.pallas_call(body, name=name, out_shape=jax.ShapeDtypeStruct((r0, PACK_COLS), F32))(*arrays)
    return out, starts


def sum_slot(gathered, slot, shape, name):
    r, c = shape

    def body(ga_ref, o_ref):
        acc = ga_ref[0, slot:slot + r, 0:c]
        for d in range(1, N_DEV):
            acc = acc + ga_ref[d, slot:slot + r, 0:c]
        o_ref[...] = acc

    return pl.pallas_call(body, name=name, out_shape=jax.ShapeDtypeStruct((r, c), F32))(gathered)


def carried_allgather(blocks):
    n = len(blocks)

    def first_hop(ins, outs, sems, i, j, chip, x, y, c):
        me = 2 * x + y
        return _remote(_rows_half(ins[i], c), _rows_half(outs[i].at[me], c), sems[0], sems[1], 6 * i + j, (*chip, c))

    def start(ins, outs, sems):
        x, y, c = _me()
        for i in range(n):
            pltpu.make_async_copy(ins[i], outs[i].at[2 * x + y], sems[2].at[i]).start()
        for i in range(n):
            for j, chip in enumerate(_other_chips(x, y)):
                first_hop(ins, outs, sems, i, j, chip, x, y, c).start()

    def finish(ins, outs, sems):
        x, y, c = _me()
        sib = (x, y, 1 - c)
        chips = _other_chips(x, y)
        passed = []
        for j, (cx, cy) in enumerate(chips):
            for i in range(n):
                got = _rows_half(outs[i].at[2 * cx + cy], c)
                _remote(got, got, sems[0], sems[1], 6 * i + j, (cx, cy, c)).wait_recv()
                fw = _remote(got, got, sems[0], sems[1], 6 * i + 3 + j, sib)
                fw.start()
                passed.append(fw)
        for j, (cx, cy) in enumerate(chips):
            for i in range(n):
                got = _rows_half(outs[i].at[2 * cx + cy], 1 - c)
                _remote(got, got, sems[0], sems[1], 6 * i + 3 + j, sib).wait_recv()
        for i in range(n):
            for j, chip in enumerate(chips):
                first_hop(ins, outs, sems, i, j, chip, x, y, c).wait_send()
        for fw in passed:
            fw.wait_send()
        for i in range(n):
            pltpu.make_async_copy(ins[i], outs[i].at[2 * x + y], sems[2].at[i]).wait()

    return Carried(blocks, [jax.ShapeDtypeStruct((N_CHIPS,) + b.shape, b.dtype) for b in blocks],
                   [pltpu.SemaphoreType.DMA((6 * n,)), pltpu.SemaphoreType.DMA((6 * n,)), pltpu.SemaphoreType.DMA((n,))],
                   start, finish)


def carried_allgather_devices(vs):
    n = len(vs)

    def first_copies(ins, outs, sems):
        x, y, c = _me()
        me = 4 * x + 2 * y + c
        cps = []
        for i in range(n):
            cps.append(_remote(ins[i], outs[i].at[me], sems[0], sems[1], 7 * i, (x, y, 1 - c)))
            for j, (cx, cy) in enumerate(_other_chips(x, y)):
                cps.append(_remote(ins[i], outs[i].at[me], sems[0], sems[1], 7 * i + 1 + j, (cx, cy, c)))
        return cps

    def local_copies(ins, outs, sems):
        x, y, c = _me()
        return [pltpu.make_async_copy(ins[i], outs[i].at[4 * x + 2 * y + c], sems[2].at[i]) for i in range(n)]

    def start(ins, outs, sems):
        for cp in local_copies(ins, outs, sems) + first_copies(ins, outs, sems):
            cp.start()

    def finish(ins, outs, sems):
        x, y, c = _me()
        sib = (x, y, 1 - c)
        chips = _other_chips(x, y)
        passed = []
        for j, (cx, cy) in enumerate(chips):
            for i in range(n):
                got = outs[i].at[4 * cx + 2 * cy + c]
                _remote(got, got, sems[0], sems[1], 7 * i + 1 + j, (cx, cy, c)).wait_recv()
                fw = _remote(got, got, sems[0], sems[1], 7 * i + 4 + j, sib)
                fw.start()
                passed.append(fw)
        for i in range(n):
            got = outs[i].at[4 * x + 2 * y + (1 - c)]
            _remote(got, got, sems[0], sems[1], 7 * i, sib).wait_recv()
            for j, (cx, cy) in enumerate(chips):
                got = outs[i].at[4 * cx + 2 * cy + (1 - c)]
                _remote(got, got, sems[0], sems[1], 7 * i + 4 + j, sib).wait_recv()
        for cp in first_copies(ins, outs, sems) + passed:
            cp.wait_send()
        for cp in local_copies(ins, outs, sems):
            cp.wait()

    return Carried(vs, [jax.ShapeDtypeStruct((N_DEV,) + v.shape, v.dtype) for v in vs],
                   [pltpu.SemaphoreType.DMA((7 * n,)), pltpu.SemaphoreType.DMA((7 * n,)), pltpu.SemaphoreType.DMA((n,))],
                   start, finish)


def carried_sibling_send(gs):
    n = len(gs)

    def copies(ins, outs, sems):
        x, y, c = _me()
        return [_remote(_rows_half(ins[i], 1 - c), outs[i], sems[0], sems[1], i, (x, y, 1 - c)) for i in range(n)]

    def start(ins, outs, sems):
        for cp in copies(ins, outs, sems):
            cp.start()

    def finish(ins, outs, sems):
        for cp in copies(ins, outs, sems):
            cp.wait()

    return Carried(gs, [jax.ShapeDtypeStruct((g.shape[0], g.shape[1] // 2, g.shape[2]), g.dtype) for g in gs],
                   [pltpu.SemaphoreType.DMA((n,)), pltpu.SemaphoreType.DMA((n,))], start, finish)


def carried_chips_exchange(parts):
    n = len(parts)

    def copies(ins, outs, sems):
        x, y, c = _me()
        cps = []
        for i in range(n):
            nl = ins[i].shape[0] // N_CHIPS
            for j, (cx, cy) in enumerate(_other_chips(x, y)):
                cps.append(_remote(ins[i].at[pl.ds((2 * cx + cy) * nl, nl)], outs[i].at[j], sems[0], sems[1],
                                   3 * i + j, (cx, cy, c)))
        return cps

    def start(ins, outs, sems):
        for cp in copies(ins, outs, sems):
            cp.start()

    def finish(ins, outs, sems):
        for cp in copies(ins, outs, sems):
            cp.wait()

    return Carried(parts, [jax.ShapeDtypeStruct((3, a.shape[0] // N_CHIPS) + a.shape[1:], a.dtype) for a in parts],
                   [pltpu.SemaphoreType.DMA((3 * n,)), pltpu.SemaphoreType.DMA((3 * n,))], start, finish)


BIG = ("ev_w_in", "ev_s5_glu_w", "ev_w_out", "od_w_in", "od_w_out", "xa_w_qg", "xa_w_kv", "xa_w_o")
SHARDED_F32 = (("ev_conv_w", 2), ("od_norm_g", 1))
SMALL = ("mem_norm_g", "ev_norm_g", "ev_s5_lambda_re", "ev_s5_lambda_im", "ev_s5_log_dt", "ev_s5_b_re", "ev_s5_b_im",
         "ev_s5_c_re", "ev_s5_c_im", "ev_s5_d", "ev_s5_glu_b", "ev_conv_b", "ev_conv_ln_g", "ev_conv_ln_b",
         "od_rel_bias", "xa_norm_g", "final_norm_g")
NARROW = ("ev_s5_c_re", "ev_s5_c_im")
DENSE_B = ("ev_s5_b_re", "ev_s5_b_im")
PACK_COLS = 1024
PACKED_SMALL = tuple(n for n in SMALL if n not in NARROW and n != "ev_norm_g")
SINGLE_SMALL = NARROW + tuple(n for n, _ in SHARDED_F32)
WEIGHTS = ("mem_norm_g", "ev_norm_g", "ev_w_in", "ev_s5_lambda_re", "ev_s5_lambda_im", "ev_s5_log_dt", "ev_s5_b_re",
           "ev_s5_b_im", "ev_s5_c_re", "ev_s5_c_im", "ev_s5_d", "ev_s5_glu_w", "ev_s5_glu_b", "ev_conv_w", "ev_conv_b",
           "ev_conv_ln_g", "ev_conv_ln_b", "ev_w_out", "od_norm_g", "od_w_in", "od_rel_bias", "od_w_out", "xa_norm_g",
           "xa_w_qg", "xa_w_kv", "xa_w_o", "final_norm_g")


def _as2d(a):
    return a.reshape(1, -1) if a.ndim == 1 else a.reshape(-1, a.shape[-1])


def kernel(x, mem, mem_norm_g, ev_norm_g, ev_w_in, ev_s5_lambda_re, ev_s5_lambda_im, ev_s5_log_dt, ev_s5_b_re, ev_s5_b_im, ev_s5_c_re, ev_s5_c_im, ev_s5_d, ev_s5_glu_w, ev_s5_glu_b, ev_conv_w, ev_conv_b, ev_conv_ln_g, ev_conv_ln_b, ev_w_out, od_norm_g, od_w_in, od_rel_bias, od_w_out, xa_norm_g, xa_w_qg, xa_w_kv, xa_w_o, final_norm_g, loss_target, m_mem_norm_g, m_ev_norm_g, m_ev_w_in, m_ev_s5_lambda_re, m_ev_s5_lambda_im, m_ev_s5_log_dt, m_ev_s5_b_re, m_ev_s5_b_im, m_ev_s5_c_re, m_ev_s5_c_im, m_ev_s5_d, m_ev_s5_glu_w, m_ev_s5_glu_b, m_ev_conv_w, m_ev_conv_b, m_ev_conv_ln_g, m_ev_conv_ln_b, m_ev_w_out, m_od_norm_g, m_od_w_in, m_od_rel_bias, m_od_w_out, m_xa_norm_g, m_xa_w_qg, m_xa_w_kv, m_xa_w_o, m_final_norm_g, v_mem_norm_g, v_ev_norm_g, v_ev_w_in, v_ev_s5_lambda_re, v_ev_s5_lambda_im, v_ev_s5_log_dt, v_ev_s5_b_re, v_ev_s5_b_im, v_ev_s5_c_re, v_ev_s5_c_im, v_ev_s5_d, v_ev_s5_glu_w, v_ev_s5_glu_b, v_ev_conv_w, v_ev_conv_b, v_ev_conv_ln_g, v_ev_conv_ln_b, v_ev_w_out, v_od_norm_g, v_od_w_in, v_od_rel_bias, v_od_w_out, v_xa_norm_g, v_xa_w_qg, v_xa_w_kv, v_xa_w_o, v_final_norm_g):
    a = dict(locals())
    w = {n: a[n] for n in WEIGHTS}
    shard = (2 * lax.axis_index("x") + lax.axis_index("y")).reshape(1).astype(jnp.int32)
    core = lax.axis_index("c").reshape(1).astype(jnp.int32)

    place = jnp.concatenate([shard, core])

    blocks = {n: w[n].astype(BF16).reshape(-1, w[n].shape[-1]) for n in BIG}
    conv_blk = jnp.pad(_as2d(w["ev_conv_w"]), ((0, 1), (0, 0)))
    odn_blk = w["od_norm_g"].reshape(2, -1)
    bias, evin_g, conv_g, odn_g = att_bias(w["od_rel_bias"][0],
                                           carried=carried_allgather([blocks["ev_w_in"], conv_blk, odn_blk]))
    gw = {"ev_w_in": evin_g}
    p = {n: w[n] for n in SMALL}
    p["ev_conv_w"] = jnp.concatenate([conv_g[s, :CONV_KERNEL] for s in range(N_CHIPS)], axis=1)[None]
    p["od_norm_g"] = odn_g.reshape(1, D_MODEL)

    loss, grad_x, g, reduced, dgev, gath, slots = local_step(x[0], mem[0], loss_target[0], p, gw,
                                                             {n: blocks[n] for n in LATE}, bias, place, core)
    loss = lax.psum(loss[0, 0], ("x", "y", "c"))
    g_big = dict(zip(BIG, sibling_share([reduced[n] for n in BIG])))

    out = {tag: {} for tag in ("grad", "delta", "m", "v")}
    for n in BIG:
        sh = w[n].shape
        to2d = lambda t: t.reshape(-1, sh[-1])
        gn = to2d(g_big[n])
        d, mn, vn = adamw(to2d(w[n]), gn, to2d(a["m_" + n]), to2d(a["v_" + n]), "adamw_" + n)
        for tag, val in zip(("grad", "delta", "m", "v"), (gn, d, mn, vn)):
            out[tag][n] = val.reshape(sh)

    jobs = [(n, gath[0], s) for n, s in zip(PACKED_SMALL, slots)]
    jobs += [(n, gt, None) for n, gt in zip(SINGLE_SMALL, gath[1:])]
    jobs += [("ev_norm_g", allgather_devices([dgev])[0], None)]
    for n, gt, slot in jobs:
        sh = w[n].shape
        w2, m2, v2 = _as2d(w[n]), _as2d(a["m_" + n]), _as2d(a["v_" + n])
        if n in DENSE_B:
            gn = _as2d(s5_b_from_dense(sum_slot(gt, slot, g[n].shape[-2:], "sum_" + n)))
            d, mn, vn = adamw(w2, gn, m2, v2, "adamw_" + n)
        else:
            gn, d, mn, vn = adamw_allreduce(gt, w2, m2, v2, shard, "adamw_" + n, slot=slot)
        for tag, val in zip(("grad", "delta", "m", "v"), (gn, d, mn, vn)):
            out[tag][n] = val.reshape(sh)

    res = [loss, grad_x[None]]
    for tag in ("grad", "delta", "m", "v"):
        res += [out[tag][n] for n in WEIGHTS]
    return tuple(res)
```

```python
import math

import jax
import jax.numpy as jnp
import numpy as np
from jax import lax
from jax.experimental import pallas as pl
from jax.experimental.pallas import tpu as pltpu

F32 = jnp.float32
BF16 = jnp.bfloat16

D_MODEL = 1024
CHUNK = 64
LEFT_CHUNKS = 8
S5_WIDTH = 512
S5_GROUP = 16
S5_GROUPS = 32
S5_STATE = 64
S5_COLS = S5_GROUPS * S5_STATE
S5_SPLIT = 4
S5_CC = S5_COLS // S5_SPLIT
S5_UC = S5_WIDTH // S5_SPLIT
CONV_WIDTH = 512
CONV_KERNEL = 31
CONV_HALO = 32
ATT_HEADS = 16
ATT_HEAD_DIM = 64
MAX_REL = 128
MEM_LEN = 256
XA_HEADS = 4
XA_HEAD_DIM = 256
EPS = 1e-6
EVEN_IN = 2560
ODD_IN = 4096

ADAM_LR = 0.001
ADAM_B1 = 0.9
ADAM_B2 = 0.999
ADAM_EPS = 1e-08
ADAM_WD = 0.01
ADAM_STEP = 10

ROW_TILE = 512
MM_TILE = 512
S5_TILE = 512
ATT_QB = 256
ATT_PAD = LEFT_CHUNKS * CHUNK
ATT_WIN = ATT_PAD + ATT_QB
VMEM_LIMIT_V7X = 56 * 1024 * 1024
NEG = -1e30
LANES = 128
N_CHIPS = 4
N_DEV = 8

MESH = pl.DeviceIdType.MESH
ANY = pl.BlockSpec(memory_space=pl.ANY)


def _cp(*sem, vmem=VMEM_LIMIT_V7X):
    return pltpu.CompilerParams(dimension_semantics=sem if sem else None, vmem_limit_bytes=vmem)


def _full(shape):
    n = len(shape)
    return pl.BlockSpec(shape, lambda *_: (0,) * n)


def _wspec(w, layer=None):
    if layer is None:
        return _full(w.shape)
    s, _, r, c = w.shape
    return pl.BlockSpec((s, None, r, c), lambda *_: (0, layer, 0, 0))


def _lane_tile(n, cap):
    return max(t for t in range(LANES, min(n, cap) + 1, LANES) if n % t == 0)


def _sigmoid(x):
    return 1.0 / (1.0 + jnp.exp(-x))


def _silu(x):
    return x * _sigmoid(x)


def _silu_pair(x):
    s = _sigmoid(x)
    return x * s, s * (1.0 + x * (1.0 - s))


_GELU_C = math.sqrt(2.0 / math.pi)


def _gelu(x):
    return 0.5 * x * (1.0 + jnp.tanh(_GELU_C * (x + 0.044715 * x * x * x)))


def _dgelu(x):
    t = jnp.tanh(_GELU_C * (x + 0.044715 * x * x * x))
    return 0.5 * (1.0 + t) + 0.5 * x * (1.0 - t * t) * _GELU_C * (1.0 + 3.0 * 0.044715 * x * x)


def _dot(a, b):
    return jnp.dot(a, b, preferred_element_type=F32)


def _dot_nt(a, b):
    return lax.dot_general(a, b, (((1,), (1,)), ((), ())), preferred_element_type=F32)


def _dot_tn(a, b):
    return lax.dot_general(a, b, (((0,), (0,)), ((), ())), preferred_element_type=F32)


def _dot_cols(a, w4, shards=range(N_CHIPS)):
    return jnp.concatenate([_dot(a, w4[s]) for s in shards], axis=1)


def _dot_rows(a, w4):
    r = w4.shape[1]
    acc = _dot(a[:, 0:r], w4[0])
    for s in range(1, N_CHIPS):
        acc = acc + _dot(a[:, s * r:(s + 1) * r], w4[s])
    return acc


def _dot_nt_cols(dys, w4):
    acc = _dot_nt(dys[0], w4[0])
    for s in range(1, N_CHIPS):
        acc = acc + _dot_nt(dys[s], w4[s])
    return acc


def _dot_nt_rows(dy, w4):
    return jnp.concatenate([_dot_nt(dy, w4[s]) for s in range(N_CHIPS)], axis=1)


def _col_pieces(v, n):
    return [v[:, s * n:(s + 1) * n] for s in range(N_CHIPS)]


def _rms_parts(xv):
    inv = lax.rsqrt(jnp.mean(xv * xv, axis=-1, keepdims=True) + EPS)
    return inv, xv * inv


def _rms_bwd(xv, g, dh):
    inv, xhat = _rms_parts(xv)
    dg = jnp.sum(dh * xhat, axis=0, keepdims=True)
    dxh = dh * g
    dx = inv * (dxh - xhat * jnp.mean(dxh * xhat, axis=-1, keepdims=True))
    return dx, dg


def norm_mm(x, g, w4, groups, name, tm=MM_TILE):
    M, D = x.shape
    n = w4.shape[2]
    tm = min(tm, M)

    def body(x_ref, g_ref, w_ref, *outs):
        _, xhat = _rms_parts(x_ref[...])
        hb = (xhat * g_ref[...]).astype(BF16)
        for o, (shards, dt, _) in zip(outs, groups):
            o[...] = _dot_cols(hb, w_ref, shards).astype(dt)
        outs[-1][...] = hb

    out_shape = [jax.ShapeDtypeStruct((M + pad, len(sh) * n), dt) for (sh, dt, pad) in groups]
    out_specs = [pl.BlockSpec((tm, len(sh) * n), lambda i, p=pad // tm: (i + p, 0)) for (sh, _, pad) in groups]
    out_shape.append(jax.ShapeDtypeStruct((M, D), BF16))
    out_specs.append(pl.BlockSpec((tm, D), lambda i: (i, 0)))
    return pl.pallas_call(
        body, name=name, grid=(M // tm,),
        in_specs=[pl.BlockSpec((tm, D), lambda i: (i, 0)), _full(g.shape), _full(w4.shape)],
        out_specs=out_specs, out_shape=out_shape, compiler_params=_cp("parallel"),
    )(x, g, w4)


def zero_rows(buf, rows, name, tm=ROW_TILE):
    C = buf.shape[1]

    def body(b_ref, o_ref):
        o_ref[...] = jnp.zeros_like(o_ref)

    return pl.pallas_call(
        body, name=name, grid=(rows // tm,), in_specs=[ANY],
        out_specs=pl.BlockSpec((tm, C), lambda i: (i, 0)),
        out_shape=jax.ShapeDtypeStruct(buf.shape, buf.dtype), input_output_aliases={0: 0},
        compiler_params=_cp("parallel"),
    )(buf)


def mm_cols(a, w, layer, name, out_dtype):
    M = a.shape[0]
    n = w.shape[3]

    def body(a_ref, w_ref, o_ref):
        o_ref[...] = _dot_cols(a_ref[...], w_ref).astype(out_dtype)

    return pl.pallas_call(
        body, name=name, grid=(1,), in_specs=[_full(a.shape), _wspec(w, layer)],
        out_specs=_full((M, N_CHIPS * n)), out_shape=jax.ShapeDtypeStruct((M, N_CHIPS * n), out_dtype),
        compiler_params=_cp("arbitrary"),
    )(a, w)


def mm_nt_cols(dy, w, layer, name):
    M = dy.shape[0]
    K, n = w.shape[2], w.shape[3]

    def body(d_ref, w_ref, o_ref):
        o_ref[...] = _dot_nt_cols(_col_pieces(d_ref[...].astype(BF16), n), w_ref)

    return pl.pallas_call(
        body, name=name, grid=(1,), in_specs=[_full(dy.shape), _wspec(w, layer)],
        out_specs=_full((M, K)), out_shape=jax.ShapeDtypeStruct((M, K), F32), compiler_params=_cp("arbitrary"),
    )(dy, w)


def mm_nt_rows(dy, w4, name, tm=MM_TILE):
    M, N = dy.shape
    K = N_CHIPS * w4.shape[1]
    tm = min(tm, M)

    def body(d_ref, w_ref, o_ref):
        o_ref[...] = _dot_nt_rows(d_ref[...].astype(BF16), w_ref)

    return pl.pallas_call(
        body, name=name, grid=(M // tm,),
        in_specs=[pl.BlockSpec((tm, N), lambda i: (i, 0)), _full(w4.shape)],
        out_specs=pl.BlockSpec((tm, K), lambda i: (i, 0)),
        out_shape=jax.ShapeDtypeStruct((M, K), F32), compiler_params=_cp("parallel"),
    )(dy, w4)


def mm_nt_normbwd(dys, offs, w4, x, g, dx_out, name, tm=MM_TILE):
    M, D = x.shape
    n = w4.shape[2]
    tm = min(tm, M)
    nd = len(dys)

    def body(*refs):
        d_refs = refs[:nd]
        w_ref, x_ref, g_ref, dxo_ref, dx_ref, dg_ref = refs[nd:]
        if nd == 1:
            pieces = _col_pieces(d_refs[0][...].astype(BF16), n)
        else:
            pieces = [r[...].astype(BF16) for r in d_refs]
        dh = _dot_nt_cols(pieces, w_ref)
        dx, dg = _rms_bwd(x_ref[...], g_ref[...], dh)
        dx_ref[...] = dxo_ref[...] + dx

        @pl.when(pl.program_id(0) == 0)
        def _():
            dg_ref[...] = jnp.zeros_like(dg_ref)

        dg_ref[...] += dg

    row = lambda c, off=0: pl.BlockSpec((tm, c), lambda i, p=off // tm: (i + p, 0))
    return pl.pallas_call(
        body, name=name, grid=(M // tm,),
        in_specs=[row(d.shape[1], off) for d, off in zip(dys, offs)] + [_full(w4.shape), row(D), _full(g.shape), row(D)],
        out_specs=[row(D), _full((1, D))],
        out_shape=[jax.ShapeDtypeStruct((M, D), F32), jax.ShapeDtypeStruct((1, D), F32)],
        compiler_params=_cp("arbitrary"),
    )(*dys, w4, x, g, dx_out)


def mm_tn(a, b, name, layout, into=None, b_off=0, out_dtype=BF16, bm=1024, bn=1280, bl=1024, carried=None):
    L, K = a.shape
    N = b.shape[1]
    kind = layout[0]
    arg = layout[1] if len(layout) > 1 else None
    bm, bn, bl = _lane_tile(K, bm), _lane_tile(N, bn), min(bl, L)
    assert L % bl == 0 and b_off % bl == 0, (L, bl, b_off)
    nl = L // bl
    n_sh, r_sh = N // N_CHIPS, K // N_CHIPS
    lay = (None,) if arg is None else (None, None)
    mid = () if arg is None else (arg,)
    gs = 1
    if kind == "plain":
        oshape, oblock, oidx = (K, N), (bm, bn), lambda i, j, l: (i, j)
    elif kind == "slab":
        oshape, oblock, oidx = (N_CHIPS, K, N), (None, bm, bn), lambda i, j, l: (arg, i, j)
    elif kind == "cols":
        bn = max(bn - bn % n_sh, n_sh) if bn >= n_sh else _lane_tile(n_sh, bn)
        gs = max(bn // n_sh, 1)
        per = n_sh // bn if gs == 1 else 1
        oshape = (N_CHIPS,) + ((2,) if arg is not None else ()) + (K, n_sh)
        oblock = ((gs,) if gs > 1 else (None,)) + lay[1:] + (bm, min(bn, n_sh))
        oidx = lambda i, j, l: (j // per,) + mid + (i, j % per)
    else:
        bm = max(bm - bm % r_sh, r_sh) if bm >= r_sh else _lane_tile(r_sh, bm)
        gs = max(bm // r_sh, 1)
        per = r_sh // bm if gs == 1 else 1
        oshape = (N_CHIPS,) + ((2,) if arg is not None else ()) + (r_sh, N)
        oblock = ((gs,) if gs > 1 else (None,)) + lay[1:] + (min(bm, r_sh), bn)
        oidx = lambda i, j, l: (i // per,) + mid + (i % per, j)
    assert K % bm == 0 and N % bn == 0, (K, bm, N, bn)

    grid = (K // bm, N // bn, nl)

    def body(*refs):
        top = end = None
        if carried is not None:
            refs, parts = carried.split(refs, 2 if into is None else 3, 1, 1)
            top, end = carried.hooks(parts, grid)
            top()
        a_ref, b_ref, o_ref, acc = refs[0], refs[1], refs[-2], refs[-1]
        l = pl.program_id(2)

        @pl.when(l == 0)
        def _():
            acc[...] = jnp.zeros_like(acc)

        acc[...] += _dot_tn(a_ref[...].astype(BF16), b_ref[...].astype(BF16))

        @pl.when(l == nl - 1)
        def _():
            if gs == 1:
                o_ref[...] = acc[...].astype(out_dtype)
            elif kind == "cols":
                for t in range(gs):
                    o_ref[t] = acc[:, t * n_sh:(t + 1) * n_sh].astype(out_dtype)
            else:
                for t in range(gs):
                    o_ref[t] = acc[t * r_sh:(t + 1) * r_sh, :].astype(out_dtype)

        if end is not None:
            end()

    in_specs = [pl.BlockSpec((bl, bm), lambda i, j, l: (l, i)),
                pl.BlockSpec((bl, bn), lambda i, j, l, p=b_off // bl: (l + p, j))]
    args = [a, b]
    alias = {}
    if into is not None:
        in_specs.append(ANY)
        args.append(into)
        alias = {2: 0}
    out_specs, out_shape = pl.BlockSpec(oblock, oidx), jax.ShapeDtypeStruct(oshape, out_dtype)
    scratch = [pltpu.VMEM((bm, bn), F32)]
    if carried is None:
        sem = ("parallel", "parallel", "arbitrary")
    else:
        in_specs += [ANY] * len(carried.arrays)
        args += carried.arrays
        out_specs, out_shape = [out_specs] + [ANY] * len(carried.out_shapes), [out_shape] + carried.out_shapes
        scratch += carried.sems
        sem = ("arbitrary",) * 3
    return pl.pallas_call(
        body, name=name, grid=grid, in_specs=in_specs, out_specs=out_specs, out_shape=out_shape,
        scratch_shapes=scratch, input_output_aliases=alias, compiler_params=_cp(*sem),
    )(*args)


def rms_fwd(x, g, name):
    def body(x_ref, g_ref, ob_ref):
        _, xhat = _rms_parts(x_ref[...])
        ob_ref[...] = (xhat * g_ref[...]).astype(BF16)

    return pl.pallas_call(body, name=name, out_shape=jax.ShapeDtypeStruct(x.shape, BF16))(x, g)


def rms_dgain(x, dy0, dy1, name):
    def body(x_ref, d0_ref, d1_ref, o_ref):
        _, xhat = _rms_parts(x_ref[...])
        o_ref[...] = jnp.sum((d0_ref[...] + d1_ref[...]) * xhat, axis=0, keepdims=True)

    return pl.pallas_call(body, name=name, out_shape=jax.ShapeDtypeStruct((1, x.shape[1]), F32))(x, dy0, dy1)


def _s5_discretise(lr, li, logdt, bt_re, bt_im):
    dt = jnp.exp(logdt)
    mag = jnp.exp(lr * dt)
    ab_re = mag * jnp.cos(li * dt)
    ab_im = mag * jnp.sin(li * dt)
    den = lr * lr + li * li
    nr = ab_re - 1.0
    coef_re = (nr * lr + ab_im * li) / den
    coef_im = (ab_im * lr - nr * li) / den
    cr = coef_re[:, None, :]
    ci = coef_im[:, None, :]
    bb_re = cr * bt_re - ci * bt_im
    bb_im = cr * bt_im + ci * bt_re
    return ab_re, ab_im, bb_re, bb_im


def s5_param_fwd(lr, li, logdt, bt_re, bt_im):
    def body(lr_ref, li_ref, ld_ref, br_ref, bi_ref, bbr_ref, bbi_ref):
        _, _, bb_re, bb_im = _s5_discretise(lr_ref[...], li_ref[...], ld_ref[...], br_ref[...], bi_ref[...])
        bbr_ref[...] = bb_re
        bbi_ref[...] = bb_im

    sh = jax.ShapeDtypeStruct(bt_re.shape, F32)
    return pl.pallas_call(body, name="s5_param_fwd", out_shape=[sh, sh])(lr, li, logdt, bt_re, bt_im)


def s5_param_bwd(lr, li, logdt, bt_re, bt_im, d_ab_re, d_ab_im, d_bb_re, d_bb_im):
    def body(lr_ref, li_ref, ld_ref, br_ref, bi_ref, dar_ref, dai_ref, dbr_ref, dbi_ref,
             o_lr, o_li, o_ld, o_br, o_bi):
        _, vjp = jax.vjp(_s5_discretise, lr_ref[...], li_ref[...], ld_ref[...], br_ref[...], bi_ref[...])
        g = vjp((dar_ref[...], dai_ref[...], dbr_ref[...], dbi_ref[...]))
        for o, v in zip((o_lr, o_li, o_ld), g[:3]):
            o[...] = v
        for o, v in zip((o_br, o_bi), g[3:]):
            for c in range(S5_GROUP):
                o[:, c * S5_STATE:(c + 1) * S5_STATE] = v[:, c, :]

    dense = jax.ShapeDtypeStruct((S5_GROUPS, S5_GROUP * S5_STATE), F32)
    shapes = [jax.ShapeDtypeStruct(a.shape, F32) for a in (lr, li, logdt)] + [dense, dense]
    return pl.pallas_call(body, name="s5_param_bwd", out_shape=shapes)(
        lr, li, logdt, bt_re, bt_im, d_ab_re, d_ab_im, d_bb_re, d_bb_im)


def s5_tables(lr_flat, li_flat, logdt_flat):
    def body(lr_ref, li_ref, ld_ref, tab_ref):
        dt = jnp.exp(ld_ref[...])
        a = lr_ref[...] * dt
        th = li_ref[...] * dt
        row = lax.broadcasted_iota(jnp.int32, (8, 1), 0)
        rowf = row.astype(F32)

        def power(e, sign):
            m = jnp.exp(e * a)
            return m * jnp.cos(e * th), sign * m * jnp.sin(e * th)

        k = 0
        for sign, fwd in ((1.0, True), (-1.0, False)):
            for s in (1, 2, 4):
                pr, pi = power(jnp.full((8, 1), float(s), F32), sign)
                keep = (row >= s) if fwd else (row + s < 8)
                tab_ref[k] = jnp.where(keep, pr, 0.0)
                tab_ref[k + 1] = jnp.where(keep, pi, 0.0)
                k += 2
            e = rowf + 1.0 if fwd else 8.0 - rowf
            pr, pi = power(e, sign)
            tab_ref[k] = pr
            tab_ref[k + 1] = pi
            k += 2

    return pl.pallas_call(body, name="s5_tables",
                          out_shape=jax.ShapeDtypeStruct((16, 8, S5_COLS), F32))(lr_flat, li_flat, logdt_flat)


def _scan_block(a, b, tabs, base, cr, ci, reverse):
    for n, s in enumerate((1, 2, 4)):
        mr = tabs[base + 2 * n]
        mi = tabs[base + 2 * n + 1]
        sh = (8 - s) if reverse else s
        ar = pltpu.roll(a, sh, 0)
        br = pltpu.roll(b, sh, 0)
        a, b = a + mr * ar - mi * br, b + mr * br + mi * ar
    pr = tabs[base + 6]
    pi = tabs[base + 7]
    a, b = a + pr * cr - pi * ci, b + pr * ci + pi * cr
    return a, b


class Carried:
    def __init__(self, arrays, out_shapes, sems, start, finish):
        self.arrays, self.out_shapes, self.sems = list(arrays), list(out_shapes), list(sems)
        self.start, self.finish = start, finish

    def split(self, refs, n_in, n_out, n_scratch):
        a, o, s = len(self.arrays), len(self.out_shapes), len(self.sems)
        own_in, car_in = refs[:n_in], refs[n_in:n_in + a]
        own_out, car_out = refs[n_in + a:n_in + a + n_out], refs[n_in + a + n_out:n_in + a + n_out + o]
        rest = refs[n_in + a + n_out + o:]
        return own_in + own_out + rest[:n_scratch], (car_in, car_out, rest[n_scratch:n_scratch + s])

    def hooks(self, parts, grid):
        first = last = None
        for k, n in enumerate(grid):
            i = pl.program_id(k)
            first = (i == 0) if first is None else first & (i == 0)
            last = (i == n - 1) if last is None else last & (i == n - 1)

        def top():
            pl.when(first)(lambda: self.start(*parts))

        def end():
            pl.when(last)(lambda: self.finish(*parts))

        return top, end


def s5_fwd(z, bbd_re, bbd_im, ccd_re, ccd_im, tab, dskip, tm=S5_TILE, carried=None):
    L = z.shape[0]
    tm = min(tm, L)
    nt = L // tm

    def body(*refs):
        top = end = None
        if carried is not None:
            refs, parts = carried.split(refs, 7, 4, 3)
            top, end = carried.hooks(parts, (S5_SPLIT, nt))
            top()
        u_ref, bbr_ref, bbi_ref, ccr_ref, cci_ref, tab_ref, d_ref, y_ref, ck_ref, hr_ref, hi_ref, xr, xi, car = refs
        t = pl.program_id(1)

        @pl.when(t == 0)
        def _():
            car[...] = jnp.zeros_like(car)

        u = u_ref[...]
        ub = u.astype(BF16)
        xr[...] = _dot(ub, bbr_ref[...])
        xi[...] = _dot(ub, bbi_ref[...])
        tabs = [tab_ref[k] for k in range(8)]

        def blk(i, c):
            r0 = pl.multiple_of(i * 8, 8)
            a, b = _scan_block(xr[pl.ds(r0, 8), :], xi[pl.ds(r0, 8), :], tabs, 0, c[0], c[1], False)
            xr[pl.ds(r0, 8), :] = a
            xi[pl.ds(r0, 8), :] = b
            return a[7:8, :], b[7:8, :]

        cr, ci = lax.fori_loop(0, tm // 8, blk, (car[0:1, :], car[1:2, :]))
        car[0:1, :] = cr
        car[1:2, :] = ci
        ck_ref[0:1, :] = cr
        ck_ref[1:2, :] = ci
        hrb = xr[...].astype(BF16)
        hib = xi[...].astype(BF16)
        hr_ref[...] = hrb
        hi_ref[...] = hib
        y_ref[...] = _dot(hrb, ccr_ref[...]) - _dot(hib, cci_ref[...]) + d_ref[...] * u
        if end is not None:
            end()

    extra = carried.arrays if carried is not None else []
    extra_out = carried.out_shapes if carried is not None else []
    extra_sems = carried.sems if carried is not None else []
    return pl.pallas_call(
        body, name="s5_fwd", grid=(S5_SPLIT, nt),
        in_specs=[pl.BlockSpec((tm, S5_UC), lambda j, t: (t, j)),
                  pl.BlockSpec((None, S5_UC, S5_CC), lambda j, t: (j, 0, 0)),
                  pl.BlockSpec((None, S5_UC, S5_CC), lambda j, t: (j, 0, 0)),
                  pl.BlockSpec((None, S5_CC, S5_UC), lambda j, t: (j, 0, 0)),
                  pl.BlockSpec((None, S5_CC, S5_UC), lambda j, t: (j, 0, 0)),
                  pl.BlockSpec((8, 8, S5_CC), lambda j, t: (0, 0, j)),
                  pl.BlockSpec((1, S5_UC), lambda j, t: (0, j))] + [ANY] * len(extra),
        out_specs=[pl.BlockSpec((tm, S5_UC), lambda j, t: (t, j)),
                   pl.BlockSpec((None, 2, S5_CC), lambda j, t: (t, 0, j)),
                   pl.BlockSpec((tm, S5_CC), lambda j, t: (t, j)),
                   pl.BlockSpec((tm, S5_CC), lambda j, t: (t, j))] + [ANY] * len(extra_out),
        out_shape=[jax.ShapeDtypeStruct((L, S5_WIDTH), F32), jax.ShapeDtypeStruct((nt, 2, S5_COLS), F32),
                   jax.ShapeDtypeStruct((L, S5_COLS), BF16), jax.ShapeDtypeStruct((L, S5_COLS), BF16)] + extra_out,
        scratch_shapes=[pltpu.VMEM((tm, S5_CC), F32), pltpu.VMEM((tm, S5_CC), F32), pltpu.VMEM((2, S5_CC), F32)]
        + extra_sems,
        compiler_params=_cp("arbitrary" if carried is not None else "parallel", "arbitrary"),
    )(z, bbd_re, bbd_im, ccd_re, ccd_im, tab, dskip, *extra)


def s5_bwd(z, dy, dz, ckpt, hrb, hib, bbd_re, bbd_im, ccd_re, ccd_im, tab, dskip, tm=S5_TILE, carried=None):
    L = z.shape[0]
    tm = min(tm, L)
    nt = L // tm

    def body(*refs):
        top = end = None
        if carried is not None:
            refs, parts = carried.split(refs, 12, 7, 7)
            top, end = carried.hooks(parts, (S5_SPLIT, nt))
            top()
        (u_ref, dy_ref, dz_ref, ck_ref, hrb_ref, hib_ref, bbr_ref, bbi_ref, ccr_ref, cci_ref, tab_ref, d_ref,
         du_ref, da_ref, dbr_ref, dbi_ref, dcr_ref, dci_ref, dd_ref, hr, hi, gr, gi, car, acr, aci) = refs
        t = pl.program_id(1)
        tt = nt - 1 - t

        @pl.when(t == 0)
        def _():
            for r in (car, acr, aci, dbr_ref, dbi_ref, dcr_ref, dci_ref, dd_ref):
                r[...] = jnp.zeros_like(r)

        u = u_ref[...]
        ub = u.astype(BF16)
        dyv = dy_ref[...]
        dyb = dyv.astype(BF16)
        tabs = [None] * 8 + [tab_ref[k] for k in range(8, 16)]

        live = (tt > 0).astype(F32)
        hr[0:8, :] = jnp.broadcast_to(ck_ref[0:1, :] * live, (8, S5_CC))
        hi[0:8, :] = jnp.broadcast_to(ck_ref[1:2, :] * live, (8, S5_CC))
        hrb = hrb_ref[...]
        hib = hib_ref[...]
        hr[8:, :] = hrb.astype(F32)
        hi[8:, :] = hib.astype(F32)
        dcr_ref[...] += _dot_tn(hrb, dyb)
        dci_ref[...] -= _dot_tn(hib, dyb)

        gr[...] = _dot_nt(dyb, ccr_ref[...])
        gi[...] = -_dot_nt(dyb, cci_ref[...])
        row0 = lax.broadcasted_iota(jnp.int32, (8, S5_CC), 0) == 0

        def rblk(k, c):
            i = tm // 8 - 1 - k
            r0 = pl.multiple_of(i * 8, 8)
            a, b = _scan_block(gr[pl.ds(r0, 8), :], gi[pl.ds(r0, 8), :], tabs, 8, c[0], c[1], True)
            gr[pl.ds(r0, 8), :] = a
            gi[pl.ds(r0, 8), :] = b
            r1 = pl.multiple_of(i * 8 + 8, 8)
            hpr = jnp.where(row0, pltpu.roll(hr[pl.ds(r0, 8), :], 1, 0), pltpu.roll(hr[pl.ds(r1, 8), :], 1, 0))
            hpi = jnp.where(row0, pltpu.roll(hi[pl.ds(r0, 8), :], 1, 0), pltpu.roll(hi[pl.ds(r1, 8), :], 1, 0))
            acr[...] += a * hpr + b * hpi
            aci[...] += b * hpr - a * hpi
            return a[0:1, :], b[0:1, :]

        cr, ci = lax.fori_loop(0, tm // 8, rblk, (car[0:1, :], car[1:2, :]))
        car[0:1, :] = cr
        car[1:2, :] = ci

        grb = gr[...].astype(BF16)
        gib = gi[...].astype(BF16)
        du_ref[...] = (_dot_nt(grb, bbr_ref[...]) + _dot_nt(gib, bbi_ref[...]) + d_ref[...] * dyv).astype(BF16)
        dbr_ref[...] += _dot_tn(ub, grb)
        dbi_ref[...] += _dot_tn(ub, gib)
        dd_ref[...] += jnp.sum(dyv * u, axis=0, keepdims=True)

        @pl.when(t == nt - 1)
        def _():
            da_ref[0:1, :] = jnp.sum(acr[...], axis=0, keepdims=True)
            da_ref[1:2, :] = jnp.sum(aci[...], axis=0, keepdims=True)

        if end is not None:
            end()

    extra = carried.arrays if carried is not None else []
    extra_out = carried.out_shapes if carried is not None else []
    extra_sems = carried.sems if carried is not None else []
    chunk = lambda a, b: pl.BlockSpec((None, a, b), lambda j, t: (j, 0, 0))
    return pl.pallas_call(
        body, name="s5_bwd", grid=(S5_SPLIT, nt),
        in_specs=[pl.BlockSpec((tm, S5_UC), lambda j, t: (nt - 1 - t, j)),
                  pl.BlockSpec((tm, S5_UC), lambda j, t: (nt - 1 - t, j)),
                  ANY,
                  pl.BlockSpec((None, 2, S5_CC), lambda j, t: (jnp.maximum(nt - 2 - t, 0), 0, j)),
                  pl.BlockSpec((tm, S5_CC), lambda j, t: (nt - 1 - t, j)),
                  pl.BlockSpec((tm, S5_CC), lambda j, t: (nt - 1 - t, j)),
                  chunk(S5_UC, S5_CC), chunk(S5_UC, S5_CC), chunk(S5_CC, S5_UC), chunk(S5_CC, S5_UC),
                  pl.BlockSpec((16, 8, S5_CC), lambda j, t: (0, 0, j)),
                  pl.BlockSpec((1, S5_UC), lambda j, t: (0, j))] + [ANY] * len(extra),
        out_specs=[pl.BlockSpec((tm, S5_UC), lambda j, t: (nt - 1 - t, j)),
                   pl.BlockSpec((None, 2, S5_CC), lambda j, t: (j, 0, 0)),
                   chunk(S5_UC, S5_CC), chunk(S5_UC, S5_CC), chunk(S5_CC, S5_UC), chunk(S5_CC, S5_UC),
                   pl.BlockSpec((1, S5_UC), lambda j, t: (0, j))] + [ANY] * len(extra_out),
        out_shape=[jax.ShapeDtypeStruct(dz.shape, dz.dtype),
                   jax.ShapeDtypeStruct((S5_SPLIT, 2, S5_CC), F32),
                   jax.ShapeDtypeStruct((S5_SPLIT, S5_UC, S5_CC), F32),
                   jax.ShapeDtypeStruct((S5_SPLIT, S5_UC, S5_CC), F32),
                   jax.ShapeDtypeStruct((S5_SPLIT, S5_CC, S5_UC), F32),
                   jax.ShapeDtypeStruct((S5_SPLIT, S5_CC, S5_UC), F32),
                   jax.ShapeDtypeStruct((1, S5_WIDTH), F32)] + extra_out,
        scratch_shapes=[pltpu.VMEM((tm + 8, S5_CC), F32), pltpu.VMEM((tm + 8, S5_CC), F32),
                        pltpu.VMEM((tm, S5_CC), F32), pltpu.VMEM((tm, S5_CC), F32),
                        pltpu.VMEM((2, S5_CC), F32), pltpu.VMEM((8, S5_CC), F32), pltpu.VMEM((8, S5_CC), F32)]
        + extra_sems,
        input_output_aliases={2: 0},
        compiler_params=_cp("arbitrary" if carried is not None else "parallel", "arbitrary"),
    )(z, dy, dz, ckpt, hrb, hib, bbd_re, bbd_im, ccd_re, ccd_im, tab, dskip, *extra)


_EYE8 = np.eye(S5_GROUPS // S5_SPLIT, dtype=np.float32)


def _blockdiag(a):
    g, r, c = a.shape
    a = a.reshape(S5_SPLIT, g // S5_SPLIT, r, c)
    out = a[:, :, :, None, :] * _EYE8[None, :, None, :, None].astype(a.dtype)
    return out.reshape(S5_SPLIT, (g // S5_SPLIT) * r, (g // S5_SPLIT) * c)


def _blockdiag_extract(a, r, c):
    n = S5_GROUPS // S5_SPLIT
    a = a.reshape(S5_SPLIT, n, r, n, c)
    d = jnp.stack([a[:, k, :, k, :] for k in range(n)], axis=1)
    return d.reshape(S5_GROUPS, r, c)


def s5_mixer_core_fwd(z, lam_re, lam_im, log_dt, b_re, b_im, c_re, c_im, d_skip, carried=None):
    bt_re = jnp.swapaxes(b_re, 1, 2)
    bt_im = jnp.swapaxes(b_im, 1, 2)
    logdt = log_dt.reshape(S5_GROUPS, 1)
    bb_re, bb_im = s5_param_fwd(lam_re, lam_im, logdt, bt_re, bt_im)
    flat = lambda a: a.reshape(1, S5_COLS)
    tab = s5_tables(flat(lam_re), flat(lam_im), flat(jnp.broadcast_to(logdt, (S5_GROUPS, S5_STATE))))
    bbd_re = _blockdiag(bb_re).astype(BF16)
    bbd_im = _blockdiag(bb_im).astype(BF16)
    ccd_re = _blockdiag(jnp.swapaxes(c_re, 1, 2)).astype(BF16)
    ccd_im = _blockdiag(jnp.swapaxes(c_im, 1, 2)).astype(BF16)
    dsk = d_skip.reshape(1, S5_WIDTH)
    y, ckpt, hrb, hib, *landed = s5_fwd(z, bbd_re, bbd_im, ccd_re, ccd_im, tab, dsk, carried=carried)
    saved = (logdt, bt_re, bt_im, bbd_re, bbd_im, ccd_re, ccd_im, tab, dsk, ckpt, hrb, hib)
    return y, saved, landed


def s5_b_from_dense(dense):
    return jnp.swapaxes(dense.reshape(S5_GROUPS, S5_GROUP, S5_STATE), 1, 2)


def s5_mixer_core_bwd(z, dy, dz, lam_re, lam_im, saved, carried=None):
    logdt, bt_re, bt_im, bbd_re, bbd_im, ccd_re, ccd_im, tab, dsk, ckpt, hrb, hib = saved
    dz, da, dbr, dbi, dcr, dci, dd, *landed = s5_bwd(z, dy, dz, ckpt, hrb, hib, bbd_re, bbd_im, ccd_re, ccd_im, tab,
                                                     dsk, carried=carried)
    d_ab_re = da[:, 0, :].reshape(S5_GROUPS, S5_STATE)
    d_ab_im = da[:, 1, :].reshape(S5_GROUPS, S5_STATE)
    d_bb_re = _blockdiag_extract(dbr, S5_GROUP, S5_STATE)
    d_bb_im = _blockdiag_extract(dbi, S5_GROUP, S5_STATE)
    g_lr, g_li, g_ld, g_btr, g_bti = s5_param_bwd(lam_re, lam_im, logdt, bt_re, bt_im,
                                                  d_ab_re, d_ab_im, d_bb_re, d_bb_im)
    g_cre = jnp.swapaxes(_blockdiag_extract(dcr, S5_STATE, S5_GROUP), 1, 2)
    g_cim = jnp.swapaxes(_blockdiag_extract(dci, S5_STATE, S5_GROUP), 1, 2)
    grads = dict(lambda_re=g_lr, lambda_im=g_li, log_dt=g_ld.reshape(S5_GROUPS), b_re=g_btr, b_im=g_bti,
                 c_re=g_cre, c_im=g_cim, d=dd.reshape(S5_WIDTH))
    return dz, grads, landed


Z_U, Z_GA, Z_VAL, Z_GLU, Z_GB = range(5)
SUBLANES = 8


def _shifted_copies(buf, tm):
    n = tm + CONV_HALO - SUBLANES
    for r in range(1, SUBLANES):
        buf[r, 0:n, :] = buf[0, pl.ds(r, n), :]


CONV_ROWS = 32


def _shifted_rows(buf, start, rows, base=0):
    return buf[start % SUBLANES, pl.ds(base + (start - start % SUBLANES), rows), :]


def conv_fwd(z, conv_w, conv_b, tm=ROW_TILE):
    L = z.shape[0]
    tm = min(tm, L)
    nt = L // tm
    hb = tm // CONV_HALO
    C = CONV_WIDTH

    def body(val_ref, glu_ref, valh_ref, gluh_ref, w_ref, b_ref, c_ref, vsh):
        live = (pl.program_id(0) > 0).astype(F32)
        vsh[0, 0:CONV_HALO, :] = valh_ref[...] * _sigmoid(gluh_ref[...]) * live
        vsh[0, CONV_HALO:, :] = val_ref[...] * _sigmoid(glu_ref[...])
        _shifted_copies(vsh, tm)

        def rows(i, carry):
            base = pl.multiple_of(i * CONV_ROWS, CONV_ROWS)
            acc = jnp.broadcast_to(b_ref[...], (CONV_ROWS, C))
            for k in range(CONV_KERNEL):
                acc = acc + w_ref[k:k + 1, :] * _shifted_rows(vsh, CONV_HALO - CONV_KERNEL + 1 + k, CONV_ROWS, base)
            c_ref[pl.ds(base, CONV_ROWS), :] = acc
            return carry

        lax.fori_loop(0, tm // CONV_ROWS, rows, 0)

    cur = lambda col: pl.BlockSpec((tm, C), lambda t: (t, col))
    prev = lambda col: pl.BlockSpec((CONV_HALO, C), lambda t: (jnp.maximum(t * hb - 1, 0), col))
    return pl.pallas_call(
        body, name="conv_fwd", grid=(nt,),
        in_specs=[cur(Z_VAL), cur(Z_GLU), prev(Z_VAL), prev(Z_GLU), _full(conv_w.shape), _full(conv_b.shape)],
        out_specs=pl.BlockSpec((tm, C), lambda t: (t, 0)),
        out_shape=jax.ShapeDtypeStruct((L, C), F32),
        scratch_shapes=[pltpu.VMEM((8, tm + CONV_HALO, C), F32)],
        compiler_params=_cp("parallel"),
    )(z, z, z, z, conv_w, conv_b)


def conv_bwd(z, dc, dz, conv_w, tm=ROW_TILE, carried=None):
    L = z.shape[0]
    tm = min(tm, L)
    nt = L // tm
    hb = tm // CONV_HALO
    nh = L // CONV_HALO
    C = CONV_WIDTH
    off = CONV_HALO - CONV_KERNEL + 1

    def body(*refs):
        top = end = None
        if carried is not None:
            refs, parts = carried.split(refs, 8, 3, 3)
            top, end = carried.hooks(parts, (nt,))
            top()
        val_ref, glu_ref, valh_ref, gluh_ref, dc_ref, dcn_ref, dz_ref, w_ref, dvg_ref, dw_ref, db_ref, vsh, dsh, wacc = refs
        t = pl.program_id(0)

        @pl.when(t == 0)
        def _():
            wacc[...] = jnp.zeros_like(wacc)
            db_ref[...] = jnp.zeros_like(db_ref)

        val = val_ref[...]
        sg = _sigmoid(glu_ref[...])
        vsh[0, 0:CONV_HALO, :] = valh_ref[...] * _sigmoid(gluh_ref[...]) * (t > 0).astype(F32)
        vsh[0, CONV_HALO:, :] = val * sg
        dcv = dc_ref[...]
        dsh[0, 0:tm, :] = dcv
        dsh[0, tm:, :] = dcn_ref[...] * (t < nt - 1).astype(F32)
        _shifted_copies(vsh, tm)
        _shifted_copies(dsh, tm)

        def rows(i, carry):
            base = pl.multiple_of(i * CONV_ROWS, CONV_ROWS)
            dcr = dc_ref[pl.ds(base, CONV_ROWS), :]
            dv = jnp.zeros((CONV_ROWS, C), F32)
            for k in range(CONV_KERNEL):
                dv = dv + w_ref[k:k + 1, :] * _shifted_rows(dsh, CONV_KERNEL - 1 - k, CONV_ROWS, base)
                prod = dcr * _shifted_rows(vsh, off + k, CONV_ROWS, base)
                wacc[k] += jnp.sum(prod.reshape(CONV_ROWS // SUBLANES, SUBLANES, C), axis=0)
            valr = val_ref[pl.ds(base, CONV_ROWS), :]
            sgr = _sigmoid(glu_ref[pl.ds(base, CONV_ROWS), :])
            dvg_ref[pl.ds(base, CONV_ROWS), 0:C] = (dv * sgr).astype(BF16)
            dvg_ref[pl.ds(base, CONV_ROWS), C:] = (dv * valr * sgr * (1.0 - sgr)).astype(BF16)
            return carry

        lax.fori_loop(0, tm // CONV_ROWS, rows, 0)
        db_ref[...] += jnp.sum(dcv, axis=0, keepdims=True)

        @pl.when(t == nt - 1)
        def _():
            dw_ref[...] = jnp.sum(wacc[...], axis=1)

        if end is not None:
            end()

    extra = carried.arrays if carried is not None else []
    extra_out = carried.out_shapes if carried is not None else []
    extra_sems = carried.sems if carried is not None else []
    cur = lambda col: pl.BlockSpec((tm, C), lambda t: (t, col))
    prev = lambda col: pl.BlockSpec((CONV_HALO, C), lambda t: (jnp.maximum(t * hb - 1, 0), col))
    nxt = pl.BlockSpec((CONV_HALO, C), lambda t: (jnp.minimum((t + 1) * hb, nh - 1), 0))
    row = pl.BlockSpec((tm, C), lambda t: (t, 0))
    return pl.pallas_call(
        body, name="conv_bwd", grid=(nt,),
        in_specs=[cur(Z_VAL), cur(Z_GLU), prev(Z_VAL), prev(Z_GLU), row, nxt, ANY, _full(conv_w.shape)]
        + [ANY] * len(extra),
        out_specs=[pl.BlockSpec((tm, 2 * C), lambda t: (t, 1)), _full((CONV_HALO, C)), _full((1, C))]
        + [ANY] * len(extra_out),
        out_shape=[jax.ShapeDtypeStruct(dz.shape, dz.dtype),
                   jax.ShapeDtypeStruct((CONV_HALO, C), F32), jax.ShapeDtypeStruct((1, C), F32)] + extra_out,
        scratch_shapes=[pltpu.VMEM((8, tm + CONV_HALO, C), F32), pltpu.VMEM((8, tm + CONV_HALO, C), F32),
                        pltpu.VMEM((CONV_HALO, SUBLANES, C), F32)] + extra_sems,
        input_output_aliases={6: 0},
        compiler_params=_cp("arbitrary"),
    )(z, z, z, z, dc, dc, dz, conv_w, *extra)


def _ln_parts(c):
    mu = jnp.mean(c, axis=-1, keepdims=True)
    cc = c - mu
    rstd = lax.rsqrt(jnp.mean(cc * cc, axis=-1, keepdims=True) + EPS)
    return rstd, cc * rstd


def _ev_tail_branches(ys, c, wglu, bglu, lng, lnb):
    z1 = _gelu(ys)
    z1b = z1.astype(BF16)
    sg = _sigmoid(_dot_rows(z1b, wglu) + bglu)
    out = z1 * sg
    rstd, chat = _ln_parts(c)
    cn = chat * lng + lnb
    return z1, z1b, sg, out, rstd, chat, cn


def ev_tail_fwd(ys, z, c, x0, wglu, bglu, lng, lnb, wout, tm=ROW_TILE):
    L, D = x0.shape
    tm = min(tm, L)
    W = S5_WIDTH

    def body(ys_ref, ga_ref, c_ref, gb_ref, x_ref, wglu_ref, bglu_ref, lng_ref, lnb_ref, wout_ref, o_ref):
        _, _, _, out, _, _, cn = _ev_tail_branches(ys_ref[...], c_ref[...], wglu_ref, bglu_ref[...],
                                                   lng_ref[...], lnb_ref[...])
        ya = (out * _silu(ga_ref[...])).astype(BF16)
        yb = (_silu(cn) * _silu(gb_ref[...])).astype(BF16)
        o_ref[...] = x_ref[...] + _dot_rows(jnp.concatenate([ya, yb], axis=1), wout_ref)

    row = lambda n, col=0: pl.BlockSpec((tm, n), lambda t: (t, col))
    return pl.pallas_call(
        body, name="ev_tail_fwd", grid=(L // tm,),
        in_specs=[row(W), row(W, Z_GA), row(W), row(W, Z_GB), row(D), _full(wglu.shape), _full(bglu.shape),
                  _full(lng.shape), _full(lnb.shape), _full(wout.shape)],
        out_specs=row(D), out_shape=jax.ShapeDtypeStruct((L, D), F32), compiler_params=_cp("parallel"),
    )(ys, z, c, z, x0, wglu, bglu, lng, lnb, wout)


def ev_tail_bwd(ys, z, c, dx1, wglu, bglu, lng, lnb, wout, tm=ROW_TILE):
    L, D = dx1.shape
    tm = min(tm, L)
    W = S5_WIDTH

    def body(ys_ref, ga_ref, c_ref, gb_ref, dx_ref, wglu_ref, bglu_ref, lng_ref, lnb_ref, wout_ref,
             dys_ref, dc_ref, dz_ref, r_ref, z1_ref, dt_ref, dbg_ref, dlg_ref, dlb_ref):
        @pl.when(pl.program_id(0) == 0)
        def _():
            for r in (dbg_ref, dlg_ref, dlb_ref):
                r[...] = jnp.zeros_like(r)

        ys, ga, gb = ys_ref[...], ga_ref[...], gb_ref[...]
        z1, z1b, sg, out, rstd, chat, cn = _ev_tail_branches(ys, c_ref[...], wglu_ref, bglu_ref[...],
                                                             lng_ref[...], lnb_ref[...])
        (sga, dsga), (sgb, dsgb), (scn, dscn) = _silu_pair(ga), _silu_pair(gb), _silu_pair(cn)
        r_ref[:, 0:W] = (out * sga).astype(BF16)
        r_ref[:, W:] = (scn * sgb).astype(BF16)
        dr = _dot_nt_rows(dx_ref[...].astype(BF16), wout_ref)
        dra, drb = dr[:, 0:W], dr[:, W:]
        dz_ref[...] = jnp.zeros_like(dz_ref)
        dz_ref[:, Z_GA * W:(Z_GA + 1) * W] = (dra * out * dsga).astype(BF16)
        dout = dra * sga
        dt = dout * z1 * sg * (1.0 - sg)
        dtb = dt.astype(BF16)
        dz1 = dout * sg + _dot_nt_rows(dtb, wglu_ref)
        dys_ref[...] = dz1 * _dgelu(ys)
        z1_ref[...] = z1b
        dt_ref[...] = dtb
        dbg_ref[...] += jnp.sum(dt, axis=0, keepdims=True)
        dz_ref[:, Z_GB * W:(Z_GB + 1) * W] = (drb * scn * dsgb).astype(BF16)
        dcn = drb * sgb * dscn
        dlg_ref[...] += jnp.sum(dcn * chat, axis=0, keepdims=True)
        dlb_ref[...] += jnp.sum(dcn, axis=0, keepdims=True)
        dch = dcn * lng_ref[...]
        dc_ref[...] = rstd * (dch - jnp.mean(dch, axis=-1, keepdims=True)
                              - chat * jnp.mean(dch * chat, axis=-1, keepdims=True))

    row = lambda n, col=0: pl.BlockSpec((tm, n), lambda t: (t, col))
    f = lambda n, dt: jax.ShapeDtypeStruct((L, n), dt)
    vec = jax.ShapeDtypeStruct((1, W), F32)
    return pl.pallas_call(
        body, name="ev_tail_bwd", grid=(L // tm,),
        in_specs=[row(W), row(W, Z_GA), row(W), row(W, Z_GB), row(D), _full(wglu.shape), _full(bglu.shape),
                  _full(lng.shape), _full(lnb.shape), _full(wout.shape)],
        out_specs=[row(W), row(W), row(EVEN_IN), row(D), row(W), row(W), _full((1, W)), _full((1, W)), _full((1, W))],
        out_shape=[f(W, F32), f(W, F32), f(EVEN_IN, BF16), f(D, BF16), f(W, BF16), f(W, BF16), vec, vec, vec],
        compiler_params=_cp("arbitrary"),
    )(ys, z, c, z, dx1, wglu, bglu, lng, lnb, wout)


XA_SCALE = XA_HEAD_DIM ** -0.5


def _xa_forward(xv, g, wqg, kv):
    D = D_MODEL
    _, xhat = _rms_parts(xv)
    hb = (xhat * g).astype(BF16)
    qb = (_dot_cols(hb, wqg, (0, 1)) * XA_SCALE).astype(BF16)
    gate = _dot_cols(hb, wqg, (2, 3))
    ps, os_ = [], []
    for h in range(XA_HEADS):
        lo, hi = h * XA_HEAD_DIM, (h + 1) * XA_HEAD_DIM
        s = _dot_nt(qb[:, lo:hi], kv[:, lo:hi])
        e = jnp.exp(s - jnp.max(s, axis=-1, keepdims=True))
        inv = 1.0 / jnp.sum(e, axis=-1, keepdims=True)
        ps.append((e, inv))
        os_.append(_dot(e.astype(BF16), kv[:, D + lo:D + hi]) * inv)
    return hb, qb, gate, ps, jnp.concatenate(os_, axis=1)


def xa_fwd(x, g, wqg, kv, wo, layer, name, tm=MM_TILE):
    L, D = x.shape
    tm = min(tm, L)

    def body(x_ref, g_ref, wqg_ref, kv_ref, wo_ref, o_ref):
        xv = x_ref[...]
        _, _, gate, _, o = _xa_forward(xv, g_ref[...], wqg_ref, kv_ref[...])
        o_ref[...] = xv + _dot_rows((o * _silu(gate)).astype(BF16), wo_ref)

    row = pl.BlockSpec((tm, D), lambda t: (t, 0))
    return pl.pallas_call(
        body, name=name, grid=(L // tm,),
        in_specs=[row, _full(g.shape), _wspec(wqg, layer), _full(kv.shape), _wspec(wo, layer)],
        out_specs=row, out_shape=jax.ShapeDtypeStruct((L, D), F32), compiler_params=_cp("parallel"),
    )(x, g, wqg, kv, wo)


def _loss_head(xv, gv, tv):
    D = xv.shape[-1]
    _, xhat = _rms_parts(xv)
    err = xhat * gv - tv
    loss = 0.5 * jnp.sum(jnp.sum(err * err, axis=-1, keepdims=True), axis=0, keepdims=True) / D
    dx, dg = _rms_bwd(xv, gv, err * (1.0 / D))
    return loss, dx, dg


def xa_fwd_loss(x_in, r, wout, g, wqg, kv, wo, layer, target, gf, name, tm=MM_TILE):
    L, D = x_in.shape
    tm = min(tm, L)

    def body(xi_ref, r_ref, wout_ref, g_ref, wqg_ref, kv_ref, wo_ref, t_ref, gf_ref, x_ref, loss_ref, dx_ref, dg_ref):
        @pl.when(pl.program_id(0) == 0)
        def _():
            loss_ref[...] = jnp.zeros_like(loss_ref)
            dg_ref[...] = jnp.zeros_like(dg_ref)

        xv = xi_ref[...] + _dot_rows(r_ref[...], wout_ref)
        x_ref[...] = xv
        _, _, gate, _, o = _xa_forward(xv, g_ref[...], wqg_ref, kv_ref[...])
        y = xv + _dot_rows((o * _silu(gate)).astype(BF16), wo_ref)
        loss, dx, dg = _loss_head(y, gf_ref[...], t_ref[...])
        loss_ref[...] += loss
        dx_ref[...] = dx
        dg_ref[...] += dg

    row = pl.BlockSpec((tm, D), lambda t: (t, 0))
    return pl.pallas_call(
        body, name=name, grid=(L // tm,),
        in_specs=[row, row, _full(wout.shape), _full(g.shape), _wspec(wqg, layer), _full(kv.shape), _wspec(wo, layer),
                  row, _full(gf.shape)],
        out_specs=[row, _full((1, 128)), row, _full((1, D))],
        out_shape=[jax.ShapeDtypeStruct((L, D), F32), jax.ShapeDtypeStruct((1, 128), F32),
                   jax.ShapeDtypeStruct((L, D), F32), jax.ShapeDtypeStruct((1, D), F32)],
        compiler_params=_cp("arbitrary"),
    )(x_in, r, wout, g, wqg, kv, wo, target, gf)


def xa_bwd(x, dxo, g, wqg, kv, wo, layer, name, tm=MM_TILE):
    L, D = x.shape
    tm = min(tm, L)

    def body(x_ref, dxo_ref, g_ref, wqg_ref, kv_ref, wo_ref, dx_ref, dqg_ref, h_ref, r_ref, dkv_ref, dg_ref):
        @pl.when(pl.program_id(0) == 0)
        def _():
            dkv_ref[...] = jnp.zeros_like(dkv_ref)
            dg_ref[...] = jnp.zeros_like(dg_ref)

        xv = x_ref[...]
        kv = kv_ref[...]
        hb, qb, gate, ps, o = _xa_forward(xv, g_ref[...], wqg_ref, kv)
        sgate, dsgate = _silu_pair(gate)
        h_ref[...] = hb
        r_ref[...] = (o * sgate).astype(BF16)
        dxo = dxo_ref[...]
        dr = _dot_nt_rows(dxo.astype(BF16), wo_ref)
        do = dr * sgate
        dqg_ref[:, D:] = (dr * o * dsgate).astype(BF16)
        dob = do.astype(BF16)
        doo = do * o
        for h in range(XA_HEADS):
            lo, hi = h * XA_HEAD_DIM, (h + 1) * XA_HEAD_DIM
            e, inv = ps[h]
            dp = _dot_nt(dob[:, lo:hi], kv[:, D + lo:D + hi])
            dkv_ref[:, D + lo:D + hi] += _dot_tn(e.astype(BF16), (do[:, lo:hi] * inv).astype(BF16))
            rs = jnp.sum(doo[:, lo:hi], axis=-1, keepdims=True)
            dsb = (e * ((dp - rs) * inv)).astype(BF16)
            dqg_ref[:, lo:hi] = (_dot(dsb, kv[:, lo:hi]) * XA_SCALE).astype(BF16)
            dkv_ref[:, lo:hi] += _dot_tn(dsb, qb[:, lo:hi])
        dh = _dot_nt_cols(_col_pieces(dqg_ref[...], D // 2), wqg_ref)
        dx, dg = _rms_bwd(xv, g_ref[...], dh)
        dx_ref[...] = dxo + dx
        dg_ref[...] += dg

    row = lambda n: pl.BlockSpec((tm, n), lambda t: (t, 0))
    return pl.pallas_call(
        body, name=name, grid=(L // tm,),
        in_specs=[row(D), row(D), _full(g.shape), _wspec(wqg, layer), _full(kv.shape), _wspec(wo, layer)],
        out_specs=[row(D), row(2 * D), row(D), row(D), _full(kv.shape), _full((1, D))],
        out_shape=[jax.ShapeDtypeStruct((L, D), F32), jax.ShapeDtypeStruct((L, 2 * D), BF16),
                   jax.ShapeDtypeStruct((L, D), BF16), jax.ShapeDtypeStruct((L, D), BF16),
                   jax.ShapeDtypeStruct(kv.shape, F32), jax.ShapeDtypeStruct((1, D), F32)],
        compiler_params=_cp("arbitrary"),
    )(x, dxo, g, wqg, kv, wo)


ATT_SCALE = ATT_HEAD_DIM ** -0.5
ATT_PAIRS = ATT_HEADS // 2
SKEW_LANES = 1024
REL_LANES = 384


def _skew(x, left):
    amt = (ATT_QB - 1) - lax.broadcasted_iota(jnp.int32, (ATT_QB, 1), 0)
    for bit in range(8):
        sh = (SKEW_LANES - (1 << bit)) if left else (1 << bit)
        x = jnp.where(((amt >> bit) & 1) == 1, pltpu.roll(x, sh, 1), x)
    return x


def _dist_onehot(shape, dist_axis):
    j = lax.broadcasted_iota(jnp.int32, shape, dist_axis)
    r = lax.broadcasted_iota(jnp.int32, shape, 1 - dist_axis)
    return (jnp.clip((ATT_WIN - 1) - j, -MAX_REL, MAX_REL) + MAX_REL == r).astype(BF16)


def _dot_exact(v, onehot):
    acc = jnp.zeros((v.shape[0], onehot.shape[1]), F32)
    rem = v
    for _ in range(3):
        part = rem.astype(BF16)
        acc = acc + _dot(part, onehot)
        rem = rem - part.astype(F32)
    return acc


ATT_EDGE = ATT_PAD // ATT_QB


def att_bias(rel_bias, carried=None):
    H = rel_bias.shape[0]
    rb = jnp.pad(rel_bias, ((0, 0), (0, REL_LANES - rel_bias.shape[1]))).reshape(H, 1, REL_LANES)

    def body(*refs):
        top = end = None
        if carried is not None:
            refs, parts = carried.split(refs, 1, 1, 0)
            top, end = carried.hooks(parts, (H,))
            top()
        rb_ref, o_ref = refs
        by_col = _dot_exact(jnp.broadcast_to(rb_ref[...], (8, REL_LANES)), _dist_onehot((REL_LANES, SKEW_LANES), 1))
        x = _skew(jnp.broadcast_to(by_col[0:1, :], (ATT_QB, SKEW_LANES)), left=True)[:, 0:ATT_WIN]
        qc = lax.broadcasted_iota(jnp.int32, (ATT_QB, 1), 0) // CHUNK + LEFT_CHUNKS
        col = lax.broadcasted_iota(jnp.int32, (1, ATT_WIN), 1)
        dc = qc - col // CHUNK
        band = (dc >= 0) & (dc <= LEFT_CHUNKS)
        for blk in range(ATT_EDGE + 1):
            o_ref[blk] = jnp.where(band & (col >= ATT_PAD - blk * ATT_QB), x, NEG)
        if end is not None:
            end()

    extra = carried.arrays if carried is not None else []
    extra_out = carried.out_shapes if carried is not None else []
    extra_sems = carried.sems if carried is not None else []
    return pl.pallas_call(
        body, name="att_bias", grid=(H,),
        in_specs=[pl.BlockSpec((None, 1, REL_LANES), lambda h: (h, 0, 0))] + [ANY] * len(extra),
        out_specs=[pl.BlockSpec((ATT_EDGE + 1, None, ATT_QB, ATT_WIN), lambda h: (0, h, 0, 0))] + [ANY] * len(extra_out),
        out_shape=[jax.ShapeDtypeStruct((ATT_EDGE + 1, H, ATT_QB, ATT_WIN), F32)] + extra_out,
        scratch_shapes=extra_sems,
        compiler_params=_cp("arbitrary" if carried is not None else "parallel"),
    )(rb, *extra)


def relbias_bwd(dbias):
    H = dbias.shape[0]

    def body(x_ref, o_ref):
        x = jnp.concatenate([x_ref[...], jnp.zeros((ATT_QB, SKEW_LANES - ATT_WIN), F32)], axis=1)
        col = jnp.sum(_skew(x, left=False), axis=0, keepdims=True)
        o_ref[...] = _dot_exact(jnp.broadcast_to(col, (8, SKEW_LANES)), _dist_onehot((SKEW_LANES, REL_LANES), 0))

    out = pl.pallas_call(
        body, name="relbias_bwd", grid=(H,),
        in_specs=[pl.BlockSpec((None, ATT_QB, ATT_WIN), lambda h: (h, 0, 0))],
        out_specs=pl.BlockSpec((None, 8, REL_LANES), lambda h: (h, 0, 0)),
        out_shape=jax.ShapeDtypeStruct((H, 8, REL_LANES), F32), compiler_params=_cp("parallel"),
    )(dbias)
    return out[:, 0, :2 * MAX_REL + 1]


def _ca_scores(qh, kw, bias):
    s = _dot_nt(qh, kw) + bias
    e = jnp.exp(s - jnp.max(s, axis=-1, keepdims=True))
    return e, 1.0 / jnp.sum(e, axis=-1, keepdims=True)


def _ca_head(qv, m):
    return jnp.where(m, qv, jnp.zeros_like(qv)) * ATT_SCALE


def _ca_bias_spec():
    return pl.BlockSpec((None, 2, ATT_QB, ATT_WIN), lambda hp, b: (jnp.minimum(b, ATT_EDGE), hp, 0, 0))


def ca_fwd(q, kvp, gate, bias):
    L, D = q.shape
    Lp = kvp.shape[0]
    nb = L // ATT_QB

    PP = 2
    W = PP * 128

    def body(q_ref, k_ref, v_ref, g_ref, b_ref, r_ref, o_ref):
        w = pl.multiple_of(pl.program_id(1) * ATT_QB, ATT_QB)
        first = lax.broadcasted_iota(jnp.int32, (1, 128), 1) < ATT_HEAD_DIM
        for pp in range(PP):
            sl = slice(pp * 128, (pp + 1) * 128)
            kw = k_ref[pl.ds(w, ATT_WIN), sl]
            vw = v_ref[pl.ds(w, ATT_WIN), sl]
            qv = q_ref[:, sl]
            outs = []
            for hh, m in enumerate((first, jnp.logical_not(first))):
                e, inv = _ca_scores(_ca_head(qv, m), kw, b_ref[2 * pp + hh])
                outs.append(_dot(e.astype(BF16), vw) * inv)
            o = jnp.where(first, outs[0], outs[1])
            r_ref[:, sl] = (o * _silu(g_ref[:, sl])).astype(BF16)
            o_ref[:, sl] = o.astype(BF16)

    blk = pl.BlockSpec((ATT_QB, W), lambda hp, b: (b, hp))
    bias_blk = pl.BlockSpec((None, 2 * PP, ATT_QB, ATT_WIN), lambda hp, b: (jnp.minimum(b, ATT_EDGE), hp, 0, 0))
    return pl.pallas_call(
        body, name="ca_fwd", grid=(ATT_PAIRS // PP, nb),
        in_specs=[blk, pl.BlockSpec((Lp, W), lambda hp, b: (0, hp)),
                  pl.BlockSpec((Lp, W), lambda hp, b: (0, ATT_PAIRS // PP + hp)), blk, bias_blk],
        out_specs=[blk, blk], out_shape=[jax.ShapeDtypeStruct((L, D), BF16), jax.ShapeDtypeStruct((L, D), BF16)],
        compiler_params=_cp("parallel", "arbitrary"),
    )(q, kvp, kvp, gate, bias)


def ca_bwd(q, kvp, gate, bias, dr, o):
    L, D = q.shape
    Lp = kvp.shape[0]
    nb = L // ATT_QB

    def body(q_ref, k_ref, v_ref, g_ref, b_ref, dr_ref, o_ref, dq_ref, dg_ref, dkb_ref, dvb_ref, db_ref,
             dk_ref, dv_ref):
        b = pl.program_id(1)

        @pl.when(b == 0)
        def _():
            for r in (dk_ref, dv_ref, db_ref):
                r[...] = jnp.zeros_like(r)

        w = pl.multiple_of(b * ATT_QB, ATT_QB)
        kw = k_ref[pl.ds(w, ATT_WIN), :]
        vw = v_ref[pl.ds(w, ATT_WIN), :]
        qv = q_ref[...]
        gate_v = g_ref[...]
        drv = dr_ref[...]
        o = o_ref[...].astype(F32)
        sgate, dsgate = _silu_pair(gate_v)
        do = drv * sgate
        doo = do * o
        first = lax.broadcasted_iota(jnp.int32, (1, 128), 1) < ATT_HEAD_DIM
        dqs = []
        dkw = jnp.zeros((ATT_WIN, 128), F32)
        dvw = jnp.zeros((ATT_WIN, 128), F32)
        for hh, m in enumerate((first, jnp.logical_not(first))):
            qh = _ca_head(qv, m)
            e, inv = _ca_scores(qh, kw, b_ref[hh])
            eb = e.astype(BF16)
            doh = jnp.where(m, do, 0.0)
            dp = _dot_nt(doh.astype(BF16), vw)
            dvw = dvw + _dot_tn(eb, (doh * inv).astype(BF16))
            rs = jnp.sum(jnp.where(m, doo, 0.0), axis=-1, keepdims=True)
            ds = e * ((dp - rs) * inv)
            db_ref[hh] += ds
            dsb = ds.astype(BF16)
            dqs.append(_dot(dsb, kw))
            dkw = dkw + _dot_tn(dsb, qh)
        dg_ref[...] = (drv * o * dsgate).astype(BF16)
        dq_ref[...] = (jnp.where(first, dqs[0], dqs[1]) * ATT_SCALE).astype(BF16)
        dk_ref[pl.ds(w, ATT_WIN), :] += dkw
        dv_ref[pl.ds(w, ATT_WIN), :] += dvw

        @pl.when(b == nb - 1)
        def _():
            dkb_ref[...] = dk_ref[...].astype(BF16)
            dvb_ref[...] = dv_ref[...].astype(BF16)

    blk = pl.BlockSpec((ATT_QB, 128), lambda hp, b: (b, hp))
    kblk = pl.BlockSpec((Lp, 128), lambda hp, b: (0, hp))
    vblk = pl.BlockSpec((Lp, 128), lambda hp, b: (0, ATT_PAIRS + hp))
    bblk = pl.BlockSpec((2, ATT_QB, ATT_WIN), lambda hp, b: (hp, 0, 0))
    return pl.pallas_call(
        body, name="ca_bwd", grid=(ATT_PAIRS, nb),
        in_specs=[blk, kblk, vblk, blk, _ca_bias_spec(), blk, blk],
        out_specs=[blk, blk, kblk, kblk, bblk],
        out_shape=[jax.ShapeDtypeStruct((L, D), BF16), jax.ShapeDtypeStruct((L, D), BF16),
                   jax.ShapeDtypeStruct((Lp, D), BF16), jax.ShapeDtypeStruct((Lp, D), BF16),
                   jax.ShapeDtypeStruct(bias.shape[1:], F32)],
        scratch_shapes=[pltpu.VMEM((Lp, 128), F32), pltpu.VMEM((Lp, 128), F32)],
        compiler_params=_cp("parallel", "arbitrary"),
    )(q, kvp, kvp, gate, bias, dr, o)


_ADAM_C1 = 1.0 / (1.0 - ADAM_B1 ** ADAM_STEP)
_ADAM_C2 = 1.0 / (1.0 - ADAM_B2 ** ADAM_STEP)


def _adam_update(w, g, m, v):
    mn = ADAM_B1 * m + (1.0 - ADAM_B1) * g
    vn = ADAM_B2 * v + (1.0 - ADAM_B2) * g * g
    delta = -ADAM_LR * ((mn * _ADAM_C1) / (jnp.sqrt(vn * _ADAM_C2) + ADAM_EPS) + ADAM_WD * w)
    return delta, mn, vn


def adamw(w, g, m, v, name, tr=512):
    R, C = w.shape
    tr = min(tr, R)

    def body(w_ref, g_ref, m_ref, v_ref, d_ref, mo_ref, vo_ref):
        d_ref[...], mo_ref[...], vo_ref[...] = _adam_update(w_ref[...], g_ref[...], m_ref[...], v_ref[...])

    blk = pl.BlockSpec((tr, C), lambda i: (i, 0))
    sh = jax.ShapeDtypeStruct((R, C), F32)
    return pl.pallas_call(
        body, name=name, grid=(R // tr,), in_specs=[blk] * 4, out_specs=[blk] * 3,
        out_shape=[sh] * 3, compiler_params=_cp("parallel"),
    )(w, g, m, v)


def adamw_allreduce(gathered, w, m, v, shard, name, slot=None):
    R, C = w.shape
    sharded = slot is None and gathered.shape[2] != C

    def body(s_ref, ga_ref, w_ref, m_ref, v_ref, g_ref, d_ref, mo_ref, vo_ref):
        take = (lambda d: ga_ref[d]) if slot is None else (lambda d: ga_ref[d, slot:slot + R, 0:C])
        g = take(0)
        for d in range(1, N_DEV):
            g = g + take(d)
        g_ref[...] = g
        d_ref[...], mo_ref[...], vo_ref[...] = _adam_update(w_ref[...], g, m_ref[...], v_ref[...])

    blk = pl.BlockSpec((R, C), lambda i, s_ref: (0, 0))
    if slot is not None:
        gblk = pl.BlockSpec(gathered.shape, lambda i, s_ref: (0, 0, 0))
    else:
        gblk = pl.BlockSpec((N_DEV, R, C),
                            (lambda i, s_ref: (0, 0, s_ref[0])) if sharded else (lambda i, s_ref: (0, 0, 0)))
    sh = jax.ShapeDtypeStruct((R, C), F32)
    return pl.pallas_call(
        body, name=name,
        grid_spec=pltpu.PrefetchScalarGridSpec(num_scalar_prefetch=1, grid=(1,), in_specs=[gblk, blk, blk, blk],
                                               out_specs=[blk] * 4),
        out_shape=[sh] * 4, compiler_params=_cp("arbitrary"),
    )(shard, gathered, w, m, v)


LATE = ("ev_s5_glu_w", "ev_w_out", "od_w_in", "od_w_out", "xa_w_qg", "xa_w_kv", "xa_w_o")
EARLY_GRADS = ("od_w_in", "od_w_out", "xa_w_qg", "xa_w_kv", "xa_w_o", "ev_w_out", "ev_s5_glu_w")


def _reduce_to_chip(gs, names, core, tag):
    from_sibling = sibling_send_other_half(gs, "sibling_send_" + tag)
    return [sum_with_sibling(gi, ri, core, "sum_sibling_" + n) for n, gi, ri in zip(names, gs, from_sibling)]


def local_step(x, mem, target, p, gw, late, bias, place, core):
    row = lambda a: a.reshape(1, -1)
    D = D_MODEL
    L = x.shape[0]
    g, big = {}, {}
    gw = dict(gw)

    z, h0b = norm_mm(x, p["ev_norm_g"], gw["ev_w_in"], [((0, 1, 2, 3), F32, 0)], "ev_in")
    ys, s5_saved, landed = s5_mixer_core_fwd(
        z, p["ev_s5_lambda_re"][0], p["ev_s5_lambda_im"][0], p["ev_s5_log_dt"][0], p["ev_s5_b_re"][0],
        p["ev_s5_b_im"][0], p["ev_s5_c_re"][0], p["ev_s5_c_im"][0], p["ev_s5_d"][0],
        carried=carried_allgather([late[n] for n in LATE]))
    for n, gth in zip(LATE, landed):
        rows = gth.shape[1]
        gw[n] = gth.reshape(N_CHIPS, 2, rows // 2, gth.shape[2]) if n.startswith("xa_") else gth
    memn_b = rms_fwd(mem, row(p["mem_norm_g"]), "mem_norm")
    kvs = [mm_cols(memn_b, gw["xa_w_kv"], l, f"xa_kv{l}", BF16) for l in range(2)]
    conv_w = p["ev_conv_w"][0]
    c = conv_fwd(z, conv_w, p["ev_conv_b"])
    tail = (gw["ev_s5_glu_w"], p["ev_s5_glu_b"], p["ev_conv_ln_g"], p["ev_conv_ln_b"], gw["ev_w_out"])
    x1 = ev_tail_fwd(ys, z, c, x, *tail)
    xa0 = (row(p["xa_norm_g"][0]), gw["xa_w_qg"], kvs[0], gw["xa_w_o"], 0)
    x2 = xa_fwd(x1, *xa0, "xa_fwd0")

    q, kvp, gate, h1b = norm_mm(x2, p["od_norm_g"], gw["od_w_in"],
                                [((0,), BF16, 0), ((1, 2), BF16, ATT_PAD), ((3,), F32, 0)], "od_in")
    kvp = zero_rows(kvp, ATT_PAD, "od_kv_pad")
    r, att_o = ca_fwd(q, kvp, gate, bias)
    xa1 = (row(p["xa_norm_g"][1]), gw["xa_w_qg"], kvs[1], gw["xa_w_o"], 1)
    x3, loss, dx4, dgf = xa_fwd_loss(x2, r, gw["od_w_out"], *xa1, target, row(p["final_norm_g"]), "od_out_xa_fwd1_loss")
    g["final_norm_g"] = dgf.reshape(D)

    dx3, dqg1, hx1, rx1, dkv1, dgxa1 = xa_bwd(x3, dx4, *xa1, "xa_bwd1")
    dwqg = mm_tn(hx1, dqg1, "xa_dwqg1", ("cols", 1))
    dwo = mm_tn(rx1, dx4, "xa_dwo1", ("rows", 1))

    big["od_w_out"] = mm_tn(r, dx3, "od_dwout", ("rows",))
    dr = mm_nt_rows(dx3, gw["od_w_out"], "od_out_bwd")
    dq, dgate, dkp, dvp, dbias = ca_bwd(q, kvp, gate, bias, dr, att_o)
    pieces, offs = (dq, dkp, dvp, dgate), (0, ATT_PAD, ATT_PAD, 0)
    dwin = None
    for s in range(N_CHIPS):
        dwin = mm_tn(h1b, pieces[s], f"od_dwin{s}", ("slab", s), into=dwin, b_off=offs[s],
                     bl=ATT_PAD if offs[s] else 1024)
    big["od_w_in"] = dwin
    dx2, dgod = mm_nt_normbwd(pieces, offs, gw["od_w_in"], x2, p["od_norm_g"], dx3, "od_in_bwd")
    g["od_norm_g"] = dgod
    g["od_rel_bias"] = relbias_bwd(dbias)[None]

    dx1, dqg0, hx0, rx0, dkv0, dgxa0 = xa_bwd(x1, dx2, *xa0, "xa_bwd0")
    big["xa_w_qg"] = mm_tn(hx0, dqg0, "xa_dwqg0", ("cols", 0), into=dwqg)
    big["xa_w_o"] = mm_tn(rx0, dx2, "xa_dwo0", ("rows", 0), into=dwo)
    g["xa_norm_g"] = jnp.concatenate([dgxa0, dgxa1], axis=0)

    dys, dc, dz, ra, z1b, dtb, dbglu, dlng, dlnb = ev_tail_bwd(ys, z, c, dx1, *tail)
    big["ev_w_out"] = mm_tn(ra, dx1, "ev_dwout", ("rows",))
    big["ev_s5_glu_w"] = mm_tn(z1b, dtb, "ev_dwglu", ("rows",))
    g["ev_s5_glu_b"], g["ev_conv_ln_g"], g["ev_conv_ln_b"] = dbglu, dlng, dlnb
    dwkv = mm_tn(memn_b, dkv1, "xa_dwkv1", ("cols", 1), bl=MEM_LEN)
    big["xa_w_kv"] = mm_tn(memn_b, dkv0, "xa_dwkv0", ("cols", 0), into=dwkv, bl=MEM_LEN)
    dmem0 = mm_nt_cols(dkv0, gw["xa_w_kv"], 0, "xa_kv_bwd0")
    dmem1 = mm_nt_cols(dkv1, gw["xa_w_kv"], 1, "xa_kv_bwd1")
    g["mem_norm_g"] = rms_dgain(mem, dmem0, dmem1, "mem_norm_bwd").reshape(D)

    shard_major = lambda t: t.reshape((-1,) + t.shape[-2:])
    gs = [shard_major(big[n]) for n in EARLY_GRADS]
    dz, dconvw, dconvb, *from_sibling = conv_bwd(z, dc, dz, conv_w, carried=carried_sibling_send(gs))
    g["ev_conv_w"] = dconvw[None, :CONV_KERNEL]
    g["ev_conv_b"] = dconvb
    chip_sums = [sum_with_sibling(gi, ri, core, "sum_sibling_" + n) for n, gi, ri in zip(EARLY_GRADS, gs, from_sibling)]
    dz, s5g, from_chips = s5_mixer_core_bwd(z, dys, dz, p["ev_s5_lambda_re"][0], p["ev_s5_lambda_im"][0], s5_saved,
                                            carried=carried_chips_exchange(chip_sums))
    reduced = {n: sum_chips(ci, ri, place, "sum_chips_" + n) for n, ci, ri in zip(EARLY_GRADS, chip_sums, from_chips)}
    for n, v in s5g.items():
        g["ev_s5_" + n] = v[None]
    packed, slots = pack_rows([_as2d(g[n]) for n in PACKED_SMALL], "pack_small_grads")
    dwin_ev, *gathered = mm_tn(h0b, dz, "ev_dwin", ("cols",),
                               carried=carried_allgather_devices([packed] + [_as2d(g[n]) for n in SINGLE_SMALL]))
    grad_x, dgev = mm_nt_normbwd((dz,), (0,), gw["ev_w_in"], x, p["ev_norm_g"], dx1, "ev_in_bwd")
    chip_sum = _reduce_to_chip([dwin_ev], ["ev_w_in"], core, "last")
    reduced["ev_w_in"] = sum_chips(chip_sum[0], chips_exchange(chip_sum)[0], place, "sum_chips_ev_w_in")
    return loss, grad_x, g, reduced, dgev, gathered, slots


def _me():
    return lax.axis_index("x"), lax.axis_index("y"), lax.axis_index("c")


def _other_chips(x, y):
    return [(1 - x, y), (x, 1 - y), (1 - x, 1 - y)]


def _remote(src, dst, send_sems, recv_sems, k, to):
    return pltpu.make_async_remote_copy(src_ref=src, dst_ref=dst, send_sem=send_sems.at[k], recv_sem=recv_sems.at[k],
                                        device_id=to, device_id_type=MESH)


def _rows_half(ref, h):
    H = ref.shape[-2] // 2
    return ref.at[(slice(None),) * (len(ref.shape) - 2) + (pl.ds(h * H, H), slice(None))]


def allgather_devices(vs):
    n = len(vs)

    def body(*refs):
        ins, outs = refs[:n], refs[n:2 * n]
        send_sems, recv_sems, local_sems = refs[2 * n:]
        x, y, c = _me()
        sib = (x, y, 1 - c)
        chips = _other_chips(x, y)
        me = 4 * x + 2 * y + c
        local = [pltpu.make_async_copy(ins[i], outs[i].at[me], local_sems.at[i]) for i in range(n)]
        for cp in local:
            cp.start()
        first, passed = [], []
        for i in range(n):
            first.append(_remote(ins[i], outs[i].at[me], send_sems, recv_sems, 7 * i, sib))
            for j, (cx, cy) in enumerate(chips):
                first.append(_remote(ins[i], outs[i].at[me], send_sems, recv_sems, 7 * i + 1 + j, (cx, cy, c)))
        for cp in first:
            cp.start()
        for j, (cx, cy) in enumerate(chips):
            for i in range(n):
                got = outs[i].at[4 * cx + 2 * cy + c]
                _remote(got, got, send_sems, recv_sems, 7 * i + 1 + j, (cx, cy, c)).wait_recv()
                fw = _remote(got, got, send_sems, recv_sems, 7 * i + 4 + j, sib)
                fw.start()
                passed.append(fw)
        for i in range(n):
            got = outs[i].at[4 * x + 2 * y + (1 - c)]
            _remote(got, got, send_sems, recv_sems, 7 * i, sib).wait_recv()
            for j, (cx, cy) in enumerate(chips):
                got = outs[i].at[4 * cx + 2 * cy + (1 - c)]
                _remote(got, got, send_sems, recv_sems, 7 * i + 4 + j, sib).wait_recv()
        for cp in first + passed:
            cp.wait_send()
        for cp in local:
            cp.wait()

    return pl.pallas_call(
        body, name="allgather_devices", in_specs=[ANY] * n, out_specs=[ANY] * n,
        out_shape=[jax.ShapeDtypeStruct((N_DEV,) + v.shape, v.dtype) for v in vs],
        scratch_shapes=[pltpu.SemaphoreType.DMA((7 * n,)), pltpu.SemaphoreType.DMA((7 * n,)),
                        pltpu.SemaphoreType.DMA((n,))],
    )(*vs)


def sibling_send_other_half(gs, name):
    n = len(gs)

    def body(*refs):
        ins, outs = refs[:n], refs[n:2 * n]
        send_sems, recv_sems = refs[2 * n:]
        x, y, c = _me()
        cps = [_remote(_rows_half(ins[i], 1 - c), outs[i], send_sems, recv_sems, i, (x, y, 1 - c)) for i in range(n)]
        for cp in cps:
            cp.start()
        for cp in cps:
            cp.wait()

    return pl.pallas_call(
        body, name=name, in_specs=[ANY] * n, out_specs=[ANY] * n,
        out_shape=[jax.ShapeDtypeStruct((g.shape[0], g.shape[1] // 2, g.shape[2]), g.dtype) for g in gs],
        scratch_shapes=[pltpu.SemaphoreType.DMA((n,)), pltpu.SemaphoreType.DMA((n,))],
    )(*gs)


def chips_exchange(parts):
    n = len(parts)

    def body(*refs):
        ins, outs = refs[:n], refs[n:2 * n]
        send_sems, recv_sems = refs[2 * n:]
        x, y, c = _me()
        cps = []
        for i in range(n):
            nl = ins[i].shape[0] // N_CHIPS
            for j, (cx, cy) in enumerate(_other_chips(x, y)):
                cps.append(_remote(ins[i].at[pl.ds((2 * cx + cy) * nl, nl)], outs[i].at[j], send_sems, recv_sems,
                                   3 * i + j, (cx, cy, c)))
        for cp in cps:
            cp.start()
        for cp in cps:
            cp.wait()

    return pl.pallas_call(
        body, name="chips_exchange", in_specs=[ANY] * n, out_specs=[ANY] * n,
        out_shape=[jax.ShapeDtypeStruct((3, a.shape[0] // N_CHIPS) + a.shape[1:], a.dtype) for a in parts],
        scratch_shapes=[pltpu.SemaphoreType.DMA((3 * n,)), pltpu.SemaphoreType.DMA((3 * n,))],
    )(*parts)


def sibling_share(fulls):
    n = len(fulls)

    def body(*refs):
        outs = refs[n:2 * n]
        send_sems, recv_sems = refs[2 * n:]
        x, y, c = _me()
        cps = [_remote(_rows_half(outs[i], c), _rows_half(outs[i], c), send_sems, recv_sems, i, (x, y, 1 - c))
               for i in range(n)]
        for cp in cps:
            cp.start()
        for i in range(n):
            got = _rows_half(outs[i], 1 - c)
            _remote(got, got, send_sems, recv_sems, i, (x, y, 1 - c)).wait_recv()
        for cp in cps:
            cp.wait_send()

    return pl.pallas_call(
        body, name="sibling_share", in_specs=[ANY] * n, out_specs=[ANY] * n,
        out_shape=[jax.ShapeDtypeStruct(f.shape, f.dtype) for f in fulls],
        input_output_aliases={i: i for i in range(n)},
        scratch_shapes=[pltpu.SemaphoreType.DMA((n,)), pltpu.SemaphoreType.DMA((n,))],
    )(*fulls)


def sum_with_sibling(g, recv, core, name):
    S, H, C = recv.shape
    tr = min(512, H)

    def body(c_ref, g_ref, r_ref, o_ref):
        o_ref[...] = (g_ref[...].astype(F32) + r_ref[...].astype(F32)).astype(o_ref.dtype)

    nb = H // tr
    return pl.pallas_call(
        body, name=name,
        grid_spec=pltpu.PrefetchScalarGridSpec(
            num_scalar_prefetch=1, grid=(S, nb),
            in_specs=[pl.BlockSpec((None, tr, C), lambda s, i, c_ref: (s, c_ref[0] * nb + i, 0)),
                      pl.BlockSpec((None, tr, C), lambda s, i, c_ref: (s, i, 0))],
            out_specs=pl.BlockSpec((None, tr, C), lambda s, i, c_ref: (s, i, 0))),
        out_shape=jax.ShapeDtypeStruct((S, H, C), g.dtype), compiler_params=_cp("parallel", "parallel"),
    )(core, g, recv)


def sum_chips(a, recv, place, name):
    _, nl, H, C = recv.shape
    tr = min(512, H)
    nb = H // tr

    def body(p_ref, a_ref, r_ref, o_ref):
        acc = a_ref[...].astype(F32)
        for j in range(3):
            acc = acc + r_ref[j].astype(F32)
        o_ref[...] = acc

    return pl.pallas_call(
        body, name=name,
        grid_spec=pltpu.PrefetchScalarGridSpec(
            num_scalar_prefetch=1, grid=(nl, nb),
            in_specs=[pl.BlockSpec((None, tr, C), lambda l, i, p_ref: (p_ref[0] * nl + l, i, 0)),
                      pl.BlockSpec((3, None, tr, C), lambda l, i, p_ref: (0, l, i, 0))],
            out_specs=pl.BlockSpec((None, tr, C), lambda l, i, p_ref: (l, p_ref[1] * nb + i, 0))),
        out_shape=jax.ShapeDtypeStruct((nl, 2 * H, C), F32), compiler_params=_cp("parallel", "parallel"),
    )(place, a, recv)


def pack_rows(arrays, name):
    starts, r0 = [], 0
    for a in arrays:
        if a.shape[0] >= SUBLANES:
            r0 = -(-r0 // SUBLANES) * SUBLANES
        starts.append(r0)
        r0 += a.shape[0]
    r0 = -(-r0 // SUBLANES) * SUBLANES
    n = len(arrays)

    def body(*refs):
        o_ref = refs[n]
        o_ref[...] = jnp.zeros_like(o_ref)
        for a_ref, s in zip(refs[:n], starts):
            r, c = a_ref.shape
            o_ref[s:s + r, 0:c] = a_ref[...]

    out = pl.pallas_call(body, name=name, out_shape=jax.ShapeDtypeStruct((r0, PACK_COLS), F32))(*arrays)
    return out, starts


def sum_slot(gathered, slot, shape, name):
    r, c = shape

    def body(ga_ref, o_ref):
        acc = ga_ref[0, slot:slot + r, 0:c]
        for d in range(1, N_DEV):
            acc = acc + ga_ref[d, slot:slot + r, 0:c]
        o_ref[...] = acc

    return pl.pallas_call(body, name=name, out_shape=jax.ShapeDtypeStruct((r, c), F32))(gathered)


def carried_allgather(blocks):
    n = len(blocks)

    def first_hop(ins, outs, sems, i, j, chip, x, y, c):
        me = 2 * x + y
        return _remote(_rows_half(ins[i], c), _rows_half(outs[i].at[me], c), sems[0], sems[1], 6 * i + j, (*chip, c))

    def start(ins, outs, sems):
        x, y, c = _me()
        for i in range(n):
            pltpu.make_async_copy(ins[i], outs[i].at[2 * x + y], sems[2].at[i]).start()
        for i in range(n):
            for j, chip in enumerate(_other_chips(x, y)):
                first_hop(ins, outs, sems, i, j, chip, x, y, c).start()

    def finish(ins, outs, sems):
        x, y, c = _me()
        sib = (x, y, 1 - c)
        chips = _other_chips(x, y)
        passed = []
        for j, (cx, cy) in enumerate(chips):
            for i in range(n):
                got = _rows_half(outs[i].at[2 * cx + cy], c)
                _remote(got, got, sems[0], sems[1], 6 * i + j, (cx, cy, c)).wait_recv()
                fw = _remote(got, got, sems[0], sems[1], 6 * i + 3 + j, sib)
                fw.start()
                passed.append(fw)
        for j, (cx, cy) in enumerate(chips):
            for i in range(n):
                got = _rows_half(outs[i].at[2 * cx + cy], 1 - c)
                _remote(got, got, sems[0], sems[1], 6 * i + 3 + j, sib).wait_recv()
        for i in range(n):
            for j, chip in enumerate(chips):
                first_hop(ins, outs, sems, i, j, chip, x, y, c).wait_send()
        for fw in passed:
            fw.wait_send()
        for i in range(n):
            pltpu.make_async_copy(ins[i], outs[i].at[2 * x + y], sems[2].at[i]).wait()

    return Carried(blocks, [jax.ShapeDtypeStruct((N_CHIPS,) + b.shape, b.dtype) for b in blocks],
                   [pltpu.SemaphoreType.DMA((6 * n,)), pltpu.SemaphoreType.DMA((6 * n,)), pltpu.SemaphoreType.DMA((n,))],
                   start, finish)


def carried_allgather_devices(vs):
    n = len(vs)

    def first_copies(ins, outs, sems):
        x, y, c = _me()
        me = 4 * x + 2 * y + c
        cps = []
        for i in range(n):
            cps.append(_remote(ins[i], outs[i].at[me], sems[0], sems[1], 7 * i, (x, y, 1 - c)))
            for j, (cx, cy) in enumerate(_other_chips(x, y)):
                cps.append(_remote(ins[i], outs[i].at[me], sems[0], sems[1], 7 * i + 1 + j, (cx, cy, c)))
        return cps

    def local_copies(ins, outs, sems):
        x, y, c = _me()
        return [pltpu.make_async_copy(ins[i], outs[i].at[4 * x + 2 * y + c], sems[2].at[i]) for i in range(n)]

    def start(ins, outs, sems):
        for cp in local_copies(ins, outs, sems) + first_copies(ins, outs, sems):
            cp.start()

    def finish(ins, outs, sems):
        x, y, c = _me()
        sib = (x, y, 1 - c)
        chips = _other_chips(x, y)
        passed = []
        for j, (cx, cy) in enumerate(chips):
            for i in range(n):
                got = outs[i].at[4 * cx + 2 * cy + c]
                _remote(got, got, sems[0], sems[1], 7 * i + 1 + j, (cx, cy, c)).wait_recv()
                fw = _remote(got, got, sems[0], sems[1], 7 * i + 4 + j, sib)
                fw.start()
                passed.append(fw)
        for i in range(n):
            got = outs[i].at[4 * x + 2 * y + (1 - c)]
            _remote(got, got, sems[0], sems[1], 7 * i, sib).wait_recv()
            for j, (cx, cy) in enumerate(chips):
                got = outs[i].at[4 * cx + 2 * cy + (1 - c)]
                _remote(got, got, sems[0], sems[1], 7 * i + 4 + j, sib).wait_recv()
        for cp in first_copies(ins, outs, sems) + passed:
            cp.wait_send()
        for cp in local_copies(ins, outs, sems):
            cp.wait()

    return Carried(vs, [jax.ShapeDtypeStruct((N_DEV,) + v.shape, v.dtype) for v in vs],
                   [pltpu.SemaphoreType.DMA((7 * n,)), pltpu.SemaphoreType.DMA((7 * n,)), pltpu.SemaphoreType.DMA((n,))],
                   start, finish)


def carried_sibling_send(gs):
    n = len(gs)

    def copies(ins, outs, sems):
        x, y, c = _me()
        return [_remote(_rows_half(ins[i], 1 - c), outs[i], sems[0], sems[1], i, (x, y, 1 - c)) for i in range(n)]

    def start(ins, outs, sems):
        for cp in copies(ins, outs, sems):
            cp.start()

    def finish(ins, outs, sems):
        for cp in copies(ins, outs, sems):
            cp.wait()

    return Carried(gs, [jax.ShapeDtypeStruct((g.shape[0], g.shape[1] // 2, g.shape[2]), g.dtype) for g in gs],
                   [pltpu.SemaphoreType.DMA((n,)), pltpu.SemaphoreType.DMA((n,))], start, finish)


def carried_chips_exchange(parts):
    n = len(parts)

    def copies(ins, outs, sems):
        x, y, c = _me()
        cps = []
        for i in range(n):
            nl = ins[i].shape[0] // N_CHIPS
            for j, (cx, cy) in enumerate(_other_chips(x, y)):
                cps.append(_remote(ins[i].at[pl.ds((2 * cx + cy) * nl, nl)], outs[i].at[j], sems[0], sems[1],
                                   3 * i + j, (cx, cy, c)))
        return cps

    def start(ins, outs, sems):
        for cp in copies(ins, outs, sems):
            cp.start()

    def finish(ins, outs, sems):
        for cp in copies(ins, outs, sems):
            cp.wait()

    return Carried(parts, [jax.ShapeDtypeStruct((3, a.shape[0] // N_CHIPS) + a.shape[1:], a.dtype) for a in parts],
                   [pltpu.SemaphoreType.DMA((3 * n,)), pltpu.SemaphoreType.DMA((3 * n,))], start, finish)


BIG = ("ev_w_in", "ev_s5_glu_w", "ev_w_out", "od_w_in", "od_w_out", "xa_w_qg", "xa_w_kv", "xa_w_o")
SHARDED_F32 = (("ev_conv_w", 2), ("od_norm_g", 1))
SMALL = ("mem_norm_g", "ev_norm_g", "ev_s5_lambda_re", "ev_s5_lambda_im", "ev_s5_log_dt", "ev_s5_b_re", "ev_s5_b_im",
         "ev_s5_c_re", "ev_s5_c_im", "ev_s5_d", "ev_s5_glu_b", "ev_conv_b", "ev_conv_ln_g", "ev_conv_ln_b",
         "od_rel_bias", "xa_norm_g", "final_norm_g")
NARROW = ("ev_s5_c_re", "ev_s5_c_im")
DENSE_B = ("ev_s5_b_re", "ev_s5_b_im")
PACK_COLS = 1024
PACKED_SMALL = tuple(n for n in SMALL if n not in NARROW and n != "ev_norm_g")
SINGLE_SMALL = NARROW + tuple(n for n, _ in SHARDED_F32)
WEIGHTS = ("mem_norm_g", "ev_norm_g", "ev_w_in", "ev_s5_lambda_re", "ev_s5_lambda_im", "ev_s5_log_dt", "ev_s5_b_re",
           "ev_s5_b_im", "ev_s5_c_re", "ev_s5_c_im", "ev_s5_d", "ev_s5_glu_w", "ev_s5_glu_b", "ev_conv_w", "ev_conv_b",
           "ev_conv_ln_g", "ev_conv_ln_b", "ev_w_out", "od_norm_g", "od_w_in", "od_rel_bias", "od_w_out", "xa_norm_g",
           "xa_w_qg", "xa_w_kv", "xa_w_o", "final_norm_g")


def _as2d(a):
    return a.reshape(1, -1) if a.ndim == 1 else a.reshape(-1, a.shape[-1])


def kernel(x, mem, mem_norm_g, ev_norm_g, ev_w_in, ev_s5_lambda_re, ev_s5_lambda_im, ev_s5_log_dt, ev_s5_b_re, ev_s5_b_im, ev_s5_c_re, ev_s5_c_im, ev_s5_d, ev_s5_glu_w, ev_s5_glu_b, ev_conv_w, ev_conv_b, ev_conv_ln_g, ev_conv_ln_b, ev_w_out, od_norm_g, od_w_in, od_rel_bias, od_w_out, xa_norm_g, xa_w_qg, xa_w_kv, xa_w_o, final_norm_g, loss_target, m_mem_norm_g, m_ev_norm_g, m_ev_w_in, m_ev_s5_lambda_re, m_ev_s5_lambda_im, m_ev_s5_log_dt, m_ev_s5_b_re, m_ev_s5_b_im, m_ev_s5_c_re, m_ev_s5_c_im, m_ev_s5_d, m_ev_s5_glu_w, m_ev_s5_glu_b, m_ev_conv_w, m_ev_conv_b, m_ev_conv_ln_g, m_ev_conv_ln_b, m_ev_w_out, m_od_norm_g, m_od_w_in, m_od_rel_bias, m_od_w_out, m_xa_norm_g, m_xa_w_qg, m_xa_w_kv, m_xa_w_o, m_final_norm_g, v_mem_norm_g, v_ev_norm_g, v_ev_w_in, v_ev_s5_lambda_re, v_ev_s5_lambda_im, v_ev_s5_log_dt, v_ev_s5_b_re, v_ev_s5_b_im, v_ev_s5_c_re, v_ev_s5_c_im, v_ev_s5_d, v_ev_s5_glu_w, v_ev_s5_glu_b, v_ev_conv_w, v_ev_conv_b, v_ev_conv_ln_g, v_ev_conv_ln_b, v_ev_w_out, v_od_norm_g, v_od_w_in, v_od_rel_bias, v_od_w_out, v_xa_norm_g, v_xa_w_qg, v_xa_w_kv, v_xa_w_o, v_final_norm_g):
    a = dict(locals())
    w = {n: a[n] for n in WEIGHTS}
    shard = (2 * lax.axis_index("x") + lax.axis_index("y")).reshape(1).astype(jnp.int32)
    core = lax.axis_index("c").reshape(1).astype(jnp.int32)

    place = jnp.concatenate([shard, core])

    blocks = {n: w[n].astype(BF16).reshape(-1, w[n].shape[-1]) for n in BIG}
    conv_blk = jnp.pad(_as2d(w["ev_conv_w"]), ((0, 1), (0, 0)))
    odn_blk = w["od_norm_g"].reshape(2, -1)
    bias, evin_g, conv_g, odn_g = att_bias(w["od_rel_bias"][0],
                                           carried=carried_allgather([blocks["ev_w_in"], conv_blk, odn_blk]))
    gw = {"ev_w_in": evin_g}
    p = {n: w[n] for n in SMALL}
    p["ev_conv_w"] = jnp.concatenate([conv_g[s, :CONV_KERNEL] for s in range(N_CHIPS)], axis=1)[None]
    p["od_norm_g"] = odn_g.reshape(1, D_MODEL)

    loss, grad_x, g, reduced, dgev, gath, slots = local_step(x[0], mem[0], loss_target[0], p, gw,
                                                             {n: blocks[n] for n in LATE}, bias, place, core)
    loss = lax.psum(loss[0, 0], ("x", "y", "c"))
    g_big = dict(zip(BIG, sibling_share([reduced[n] for n in BIG])))

    out = {tag: {} for tag in ("grad", "delta", "m", "v")}
    for n in BIG:
        sh = w[n].shape
        to2d = lambda t: t.reshape(-1, sh[-1])
        gn = to2d(g_big[n])
        d, mn, vn = adamw(to2d(w[n]), gn, to2d(a["m_" + n]), to2d(a["v_" + n]), "adamw_" + n)
        for tag, val in zip(("grad", "delta", "m", "v"), (gn, d, mn, vn)):
            out[tag][n] = val.reshape(sh)

    jobs = [(n, gath[0], s) for n, s in zip(PACKED_SMALL, slots)]
    jobs += [(n, gt, None) for n, gt in zip(SINGLE_SMALL, gath[1:])]
    jobs += [("ev_norm_g", allgather_devices([dgev])[0], None)]
    for n, gt, slot in jobs:
        sh = w[n].shape
        w2, m2, v2 = _as2d(w[n]), _as2d(a["m_" + n]), _as2d(a["v_" + n])
        if n in DENSE_B:
            gn = _as2d(s5_b_from_dense(sum_slot(gt, slot, g[n].shape[-2:], "sum_" + n)))
            d, mn, vn = adamw(w2, gn, m2, v2, "adamw_" + n)
        else:
            gn, d, mn, vn = adamw_allreduce(gt, w2, m2, v2, shard, "adamw_" + n, slot=slot)
        for tag, val in zip(("grad", "delta", "m", "v"), (gn, d, mn, vn)):
            out[tag][n] = val.reshape(sh)

    res = [loss, grad_x[None]]
    for tag in ("grad", "delta", "m", "v"):
        res += [out[tag][n] for n in WEIGHTS]
    return tuple(res)
```

```python
import math

import jax
import jax.numpy as jnp
import numpy as np
from jax import lax
from jax.experimental import pallas as pl
from jax.experimental.pallas import tpu as pltpu

F32 = jnp.float32
BF16 = jnp.bfloat16

D_MODEL = 1024
CHUNK = 64
LEFT_CHUNKS = 8
S5_WIDTH = 512
S5_GROUP = 16
S5_GROUPS = 32
S5_STATE = 64
S5_COLS = S5_GROUPS * S5_STATE
S5_SPLIT = 4
S5_CC = S5_COLS // S5_SPLIT
S5_UC = S5_WIDTH // S5_SPLIT
CONV_WIDTH = 512
CONV_KERNEL = 31
CONV_HALO = 32
ATT_HEADS = 16
ATT_HEAD_DIM = 64
MAX_REL = 128
MEM_LEN = 256
XA_HEADS = 4
XA_HEAD_DIM = 256
EPS = 1e-6
EVEN_IN = 2560
ODD_IN = 4096

ADAM_LR = 0.001
ADAM_B1 = 0.9
ADAM_B2 = 0.999
ADAM_EPS = 1e-08
ADAM_WD = 0.01
ADAM_STEP = 10

ROW_TILE = 512
MM_TILE = 512
S5_TILE = 1024
ATT_QB = 256
ATT_PAD = LEFT_CHUNKS * CHUNK
ATT_WIN = ATT_PAD + ATT_QB
VMEM_LIMIT_V7X = 56 * 1024 * 1024
NEG = -1e30
LANES = 128
N_CHIPS = 4
N_DEV = 8

MESH = pl.DeviceIdType.MESH
ANY = pl.BlockSpec(memory_space=pl.ANY)


def _cp(*sem, vmem=VMEM_LIMIT_V7X):
    return pltpu.CompilerParams(dimension_semantics=sem if sem else None, vmem_limit_bytes=vmem)


def _full(shape):
    n = len(shape)
    return pl.BlockSpec(shape, lambda *_: (0,) * n)


def _wspec(w, layer=None):
    if layer is None:
        return _full(w.shape)
    s, _, r, c = w.shape
    return pl.BlockSpec((s, None, r, c), lambda *_: (0, layer, 0, 0))


def _lane_tile(n, cap):
    return max(t for t in range(LANES, min(n, cap) + 1, LANES) if n % t == 0)


def _sigmoid(x):
    return 1.0 / (1.0 + jnp.exp(-x))


def _silu(x):
    return x * _sigmoid(x)


def _silu_pair(x):
    s = _sigmoid(x)
    return x * s, s * (1.0 + x * (1.0 - s))


_GELU_C = math.sqrt(2.0 / math.pi)


def _gelu(x):
    return 0.5 * x * (1.0 + jnp.tanh(_GELU_C * (x + 0.044715 * x * x * x)))


def _dgelu(x):
    t = jnp.tanh(_GELU_C * (x + 0.044715 * x * x * x))
    return 0.5 * (1.0 + t) + 0.5 * x * (1.0 - t * t) * _GELU_C * (1.0 + 3.0 * 0.044715 * x * x)


def _dot(a, b):
    return jnp.dot(a, b, preferred_element_type=F32)


def _dot_nt(a, b):
    return lax.dot_general(a, b, (((1,), (1,)), ((), ())), preferred_element_type=F32)


def _dot_tn(a, b):
    return lax.dot_general(a, b, (((0,), (0,)), ((), ())), preferred_element_type=F32)


def _dot_cols(a, w4, shards=range(N_CHIPS)):
    return jnp.concatenate([_dot(a, w4[s]) for s in shards], axis=1)


def _dot_rows(a, w4):
    r = w4.shape[1]
    acc = _dot(a[:, 0:r], w4[0])
    for s in range(1, N_CHIPS):
        acc = acc + _dot(a[:, s * r:(s + 1) * r], w4[s])
    return acc


def _dot_nt_cols(dys, w4):
    acc = _dot_nt(dys[0], w4[0])
    for s in range(1, N_CHIPS):
        acc = acc + _dot_nt(dys[s], w4[s])
    return acc


def _dot_nt_rows(dy, w4):
    return jnp.concatenate([_dot_nt(dy, w4[s]) for s in range(N_CHIPS)], axis=1)


def _col_pieces(v, n):
    return [v[:, s * n:(s + 1) * n] for s in range(N_CHIPS)]


def _rms_parts(xv):
    inv = lax.rsqrt(jnp.mean(xv * xv, axis=-1, keepdims=True) + EPS)
    return inv, xv * inv


def _rms_bwd(xv, g, dh):
    inv, xhat = _rms_parts(xv)
    dg = jnp.sum(dh * xhat, axis=0, keepdims=True)
    dxh = dh * g
    dx = inv * (dxh - xhat * jnp.mean(dxh * xhat, axis=-1, keepdims=True))
    return dx, dg


def norm_mm(x, g, w4, groups, name, tm=MM_TILE):
    M, D = x.shape
    n = w4.shape[2]
    tm = min(tm, M)

    def body(x_ref, g_ref, w_ref, *outs):
        _, xhat = _rms_parts(x_ref[...])
        hb = (xhat * g_ref[...]).astype(BF16)
        for o, (shards, dt, _) in zip(outs, groups):
            o[...] = _dot_cols(hb, w_ref, shards).astype(dt)
        outs[-1][...] = hb

    out_shape = [jax.ShapeDtypeStruct((M + pad, len(sh) * n), dt) for (sh, dt, pad) in groups]
    out_specs = [pl.BlockSpec((tm, len(sh) * n), lambda i, p=pad // tm: (i + p, 0)) for (sh, _, pad) in groups]
    out_shape.append(jax.ShapeDtypeStruct((M, D), BF16))
    out_specs.append(pl.BlockSpec((tm, D), lambda i: (i, 0)))
    return pl.pallas_call(
        body, name=name, grid=(M // tm,),
        in_specs=[pl.BlockSpec((tm, D), lambda i: (i, 0)), _full(g.shape), _full(w4.shape)],
        out_specs=out_specs, out_shape=out_shape, compiler_params=_cp("parallel"),
    )(x, g, w4)


def zero_rows(buf, rows, name, tm=ROW_TILE):
    C = buf.shape[1]

    def body(b_ref, o_ref):
        o_ref[...] = jnp.zeros_like(o_ref)

    return pl.pallas_call(
        body, name=name, grid=(rows // tm,), in_specs=[ANY],
        out_specs=pl.BlockSpec((tm, C), lambda i: (i, 0)),
        out_shape=jax.ShapeDtypeStruct(buf.shape, buf.dtype), input_output_aliases={0: 0},
        compiler_params=_cp("parallel"),
    )(buf)


def mm_cols(a, w, layer, name, out_dtype):
    M = a.shape[0]
    n = w.shape[3]

    def body(a_ref, w_ref, o_ref):
        o_ref[...] = _dot_cols(a_ref[...], w_ref).astype(out_dtype)

    return pl.pallas_call(
        body, name=name, grid=(1,), in_specs=[_full(a.shape), _wspec(w, layer)],
        out_specs=_full((M, N_CHIPS * n)), out_shape=jax.ShapeDtypeStruct((M, N_CHIPS * n), out_dtype),
        compiler_params=_cp("arbitrary"),
    )(a, w)


def mm_nt_cols(dy, w, layer, name):
    M = dy.shape[0]
    K, n = w.shape[2], w.shape[3]

    def body(d_ref, w_ref, o_ref):
        o_ref[...] = _dot_nt_cols(_col_pieces(d_ref[...].astype(BF16), n), w_ref)

    return pl.pallas_call(
        body, name=name, grid=(1,), in_specs=[_full(dy.shape), _wspec(w, layer)],
        out_specs=_full((M, K)), out_shape=jax.ShapeDtypeStruct((M, K), F32), compiler_params=_cp("arbitrary"),
    )(dy, w)


def mm_nt_rows(dy, w4, name, tm=MM_TILE):
    M, N = dy.shape
    K = N_CHIPS * w4.shape[1]
    tm = min(tm, M)

    def body(d_ref, w_ref, o_ref):
        o_ref[...] = _dot_nt_rows(d_ref[...].astype(BF16), w_ref)

    return pl.pallas_call(
        body, name=name, grid=(M // tm,),
        in_specs=[pl.BlockSpec((tm, N), lambda i: (i, 0)), _full(w4.shape)],
        out_specs=pl.BlockSpec((tm, K), lambda i: (i, 0)),
        out_shape=jax.ShapeDtypeStruct((M, K), F32), compiler_params=_cp("parallel"),
    )(dy, w4)


def mm_nt_normbwd(dys, offs, w4, x, g, dx_out, name, tm=MM_TILE):
    M, D = x.shape
    n = w4.shape[2]
    tm = min(tm, M)
    nd = len(dys)

    def body(*refs):
        d_refs = refs[:nd]
        w_ref, x_ref, g_ref, dxo_ref, dx_ref, dg_ref = refs[nd:]
        if nd == 1:
            pieces = _col_pieces(d_refs[0][...].astype(BF16), n)
        else:
            pieces = [r[...].astype(BF16) for r in d_refs]
        dh = _dot_nt_cols(pieces, w_ref)
        dx, dg = _rms_bwd(x_ref[...], g_ref[...], dh)
        dx_ref[...] = dxo_ref[...] + dx

        @pl.when(pl.program_id(0) == 0)
        def _():
            dg_ref[...] = jnp.zeros_like(dg_ref)

        dg_ref[...] += dg

    row = lambda c, off=0: pl.BlockSpec((tm, c), lambda i, p=off // tm: (i + p, 0))
    return pl.pallas_call(
        body, name=name, grid=(M // tm,),
        in_specs=[row(d.shape[1], off) for d, off in zip(dys, offs)] + [_full(w4.shape), row(D), _full(g.shape), row(D)],
        out_specs=[row(D), _full((1, D))],
        out_shape=[jax.ShapeDtypeStruct((M, D), F32), jax.ShapeDtypeStruct((1, D), F32)],
        compiler_params=_cp("arbitrary"),
    )(*dys, w4, x, g, dx_out)


def mm_tn(a, b, name, layout, into=None, b_off=0, out_dtype=BF16, bm=1024, bn=1280, bl=1024, carried=None):
    L, K = a.shape
    N = b.shape[1]
    kind = layout[0]
    arg = layout[1] if len(layout) > 1 else None
    bm, bn, bl = _lane_tile(K, bm), _lane_tile(N, bn), min(bl, L)
    assert L % bl == 0 and b_off % bl == 0, (L, bl, b_off)
    nl = L // bl
    n_sh, r_sh = N // N_CHIPS, K // N_CHIPS
    lay = (None,) if arg is None else (None, None)
    mid = () if arg is None else (arg,)
    gs = 1
    if kind == "plain":
        oshape, oblock, oidx = (K, N), (bm, bn), lambda i, j, l: (i, j)
    elif kind == "slab":
        oshape, oblock, oidx = (N_CHIPS, K, N), (None, bm, bn), lambda i, j, l: (arg, i, j)
    elif kind == "cols":
        bn = max(bn - bn % n_sh, n_sh) if bn >= n_sh else _lane_tile(n_sh, bn)
        gs = max(bn // n_sh, 1)
        per = n_sh // bn if gs == 1 else 1
        oshape = (N_CHIPS,) + ((2,) if arg is not None else ()) + (K, n_sh)
        oblock = ((gs,) if gs > 1 else (None,)) + lay[1:] + (bm, min(bn, n_sh))
        oidx = lambda i, j, l: (j // per,) + mid + (i, j % per)
    else:
        bm = max(bm - bm % r_sh, r_sh) if bm >= r_sh else _lane_tile(r_sh, bm)
        gs = max(bm // r_sh, 1)
        per = r_sh // bm if gs == 1 else 1
        oshape = (N_CHIPS,) + ((2,) if arg is not None else ()) + (r_sh, N)
        oblock = ((gs,) if gs > 1 else (None,)) + lay[1:] + (min(bm, r_sh), bn)
        oidx = lambda i, j, l: (i // per,) + mid + (i % per, j)
    assert K % bm == 0 and N % bn == 0, (K, bm, N, bn)

    grid = (K // bm, N // bn, nl)

    def body(*refs):
        top = end = None
        if carried is not None:
            refs, parts = carried.split(refs, 2 if into is None else 3, 1, 1)
            top, end = carried.hooks(parts, grid)
            top()
        a_ref, b_ref, o_ref, acc = refs[0], refs[1], refs[-2], refs[-1]
        l = pl.program_id(2)

        @pl.when(l == 0)
        def _():
            acc[...] = jnp.zeros_like(acc)

        acc[...] += _dot_tn(a_ref[...].astype(BF16), b_ref[...].astype(BF16))

        @pl.when(l == nl - 1)
        def _():
            if gs == 1:
                o_ref[...] = acc[...].astype(out_dtype)
            elif kind == "cols":
                for t in range(gs):
                    o_ref[t] = acc[:, t * n_sh:(t + 1) * n_sh].astype(out_dtype)
            else:
                for t in range(gs):
                    o_ref[t] = acc[t * r_sh:(t + 1) * r_sh, :].astype(out_dtype)

        if end is not None:
            end()

    in_specs = [pl.BlockSpec((bl, bm), lambda i, j, l: (l, i)),
                pl.BlockSpec((bl, bn), lambda i, j, l, p=b_off // bl: (l + p, j))]
    args = [a, b]
    alias = {}
    if into is not None:
        in_specs.append(ANY)
        args.append(into)
        alias = {2: 0}
    out_specs, out_shape = pl.BlockSpec(oblock, oidx), jax.ShapeDtypeStruct(oshape, out_dtype)
    scratch = [pltpu.VMEM((bm, bn), F32)]
    if carried is None:
        sem = ("parallel", "parallel", "arbitrary")
    else:
        in_specs += [ANY] * len(carried.arrays)
        args += carried.arrays
        out_specs, out_shape = [out_specs] + [ANY] * len(carried.out_shapes), [out_shape] + carried.out_shapes
        scratch += carried.sems
        sem = ("arbitrary",) * 3
    return pl.pallas_call(
        body, name=name, grid=grid, in_specs=in_specs, out_specs=out_specs, out_shape=out_shape,
        scratch_shapes=scratch, input_output_aliases=alias, compiler_params=_cp(*sem),
    )(*args)


def rms_fwd(x, g, name):
    def body(x_ref, g_ref, ob_ref):
        _, xhat = _rms_parts(x_ref[...])
        ob_ref[...] = (xhat * g_ref[...]).astype(BF16)

    return pl.pallas_call(body, name=name, out_shape=jax.ShapeDtypeStruct(x.shape, BF16))(x, g)


def rms_dgain(x, dy0, dy1, name):
    def body(x_ref, d0_ref, d1_ref, o_ref):
        _, xhat = _rms_parts(x_ref[...])
        o_ref[...] = jnp.sum((d0_ref[...] + d1_ref[...]) * xhat, axis=0, keepdims=True)

    return pl.pallas_call(body, name=name, out_shape=jax.ShapeDtypeStruct((1, x.shape[1]), F32))(x, dy0, dy1)


def _s5_discretise(lr, li, logdt, bt_re, bt_im):
    dt = jnp.exp(logdt)
    mag = jnp.exp(lr * dt)
    ab_re = mag * jnp.cos(li * dt)
    ab_im = mag * jnp.sin(li * dt)
    den = lr * lr + li * li
    nr = ab_re - 1.0
    coef_re = (nr * lr + ab_im * li) / den
    coef_im = (ab_im * lr - nr * li) / den
    cr = coef_re[:, None, :]
    ci = coef_im[:, None, :]
    bb_re = cr * bt_re - ci * bt_im
    bb_im = cr * bt_im + ci * bt_re
    return ab_re, ab_im, bb_re, bb_im


def s5_param_fwd(lr, li, logdt, bt_re, bt_im):
    def body(lr_ref, li_ref, ld_ref, br_ref, bi_ref, bbr_ref, bbi_ref):
        _, _, bb_re, bb_im = _s5_discretise(lr_ref[...], li_ref[...], ld_ref[...], br_ref[...], bi_ref[...])
        bbr_ref[...] = bb_re
        bbi_ref[...] = bb_im

    sh = jax.ShapeDtypeStruct(bt_re.shape, F32)
    return pl.pallas_call(body, name="s5_param_fwd", out_shape=[sh, sh])(lr, li, logdt, bt_re, bt_im)


def s5_param_bwd(lr, li, logdt, bt_re, bt_im, d_ab_re, d_ab_im, d_bb_re, d_bb_im):
    def body(lr_ref, li_ref, ld_ref, br_ref, bi_ref, dar_ref, dai_ref, dbr_ref, dbi_ref,
             o_lr, o_li, o_ld, o_br, o_bi):
        _, vjp = jax.vjp(_s5_discretise, lr_ref[...], li_ref[...], ld_ref[...], br_ref[...], bi_ref[...])
        g = vjp((dar_ref[...], dai_ref[...], dbr_ref[...], dbi_ref[...]))
        for o, v in zip((o_lr, o_li, o_ld), g[:3]):
            o[...] = v
        for o, v in zip((o_br, o_bi), g[3:]):
            for c in range(S5_GROUP):
                o[:, c * S5_STATE:(c + 1) * S5_STATE] = v[:, c, :]

    dense = jax.ShapeDtypeStruct((S5_GROUPS, S5_GROUP * S5_STATE), F32)
    shapes = [jax.ShapeDtypeStruct(a.shape, F32) for a in (lr, li, logdt)] + [dense, dense]
    return pl.pallas_call(body, name="s5_param_bwd", out_shape=shapes)(
        lr, li, logdt, bt_re, bt_im, d_ab_re, d_ab_im, d_bb_re, d_bb_im)


def s5_tables(lr_flat, li_flat, logdt_flat):
    def body(lr_ref, li_ref, ld_ref, tab_ref):
        dt = jnp.exp(ld_ref[...])
        a = lr_ref[...] * dt
        th = li_ref[...] * dt
        row = lax.broadcasted_iota(jnp.int32, (8, 1), 0)
        rowf = row.astype(F32)

        def power(e, sign):
            m = jnp.exp(e * a)
            return m * jnp.cos(e * th), sign * m * jnp.sin(e * th)

        k = 0
        for sign, fwd in ((1.0, True), (-1.0, False)):
            for s in (1, 2, 4):
                pr, pi = power(jnp.full((8, 1), float(s), F32), sign)
                keep = (row >= s) if fwd else (row + s < 8)
                tab_ref[k] = jnp.where(keep, pr, 0.0)
                tab_ref[k + 1] = jnp.where(keep, pi, 0.0)
                k += 2
            e = rowf + 1.0 if fwd else 8.0 - rowf
            pr, pi = power(e, sign)
            tab_ref[k] = pr
            tab_ref[k + 1] = pi
            k += 2

    return pl.pallas_call(body, name="s5_tables",
                          out_shape=jax.ShapeDtypeStruct((16, 8, S5_COLS), F32))(lr_flat, li_flat, logdt_flat)


def _scan_block(a, b, tabs, base, cr, ci, reverse):
    for n, s in enumerate((1, 2, 4)):
        mr = tabs[base + 2 * n]
        mi = tabs[base + 2 * n + 1]
        sh = (8 - s) if reverse else s
        ar = pltpu.roll(a, sh, 0)
        br = pltpu.roll(b, sh, 0)
        a, b = a + mr * ar - mi * br, b + mr * br + mi * ar
    pr = tabs[base + 6]
    pi = tabs[base + 7]
    a, b = a + pr * cr - pi * ci, b + pr * ci + pi * cr
    return a, b


class Carried:
    def __init__(self, arrays, out_shapes, sems, start, finish):
        self.arrays, self.out_shapes, self.sems = list(arrays), list(out_shapes), list(sems)
        self.start, self.finish = start, finish

    def split(self, refs, n_in, n_out, n_scratch):
        a, o, s = len(self.arrays), len(self.out_shapes), len(self.sems)
        own_in, car_in = refs[:n_in], refs[n_in:n_in + a]
        own_out, car_out = refs[n_in + a:n_in + a + n_out], refs[n_in + a + n_out:n_in + a + n_out + o]
        rest = refs[n_in + a + n_out + o:]
        return own_in + own_out + rest[:n_scratch], (car_in, car_out, rest[n_scratch:n_scratch + s])

    def hooks(self, parts, grid):
        first = last = None
        for k, n in enumerate(grid):
            i = pl.program_id(k)
            first = (i == 0) if first is None else first & (i == 0)
            last = (i == n - 1) if last is None else last & (i == n - 1)

        def top():
            pl.when(first)(lambda: self.start(*parts))

        def end():
            pl.when(last)(lambda: self.finish(*parts))

        return top, end


def s5_fwd(z, bbd_re, bbd_im, ccd_re, ccd_im, tab, dskip, tm=S5_TILE, carried=None):
    L = z.shape[0]
    tm = min(tm, L)
    nt = L // tm

    def body(*refs):
        top = end = None
        if carried is not None:
            refs, parts = carried.split(refs, 7, 4, 3)
            top, end = carried.hooks(parts, (S5_SPLIT, nt))
            top()
        u_ref, bbr_ref, bbi_ref, ccr_ref, cci_ref, tab_ref, d_ref, y_ref, ck_ref, hr_ref, hi_ref, xr, xi, car = refs
        t = pl.program_id(1)

        @pl.when(t == 0)
        def _():
            car[...] = jnp.zeros_like(car)

        u = u_ref[...]
        ub = u.astype(BF16)
        xr[...] = _dot(ub, bbr_ref[...])
        xi[...] = _dot(ub, bbi_ref[...])
        tabs = [tab_ref[k] for k in range(8)]

        def blk(i, c):
            r0 = pl.multiple_of(i * 8, 8)
            a, b = _scan_block(xr[pl.ds(r0, 8), :], xi[pl.ds(r0, 8), :], tabs, 0, c[0], c[1], False)
            xr[pl.ds(r0, 8), :] = a
            xi[pl.ds(r0, 8), :] = b
            return a[7:8, :], b[7:8, :]

        cr, ci = lax.fori_loop(0, tm // 8, blk, (car[0:1, :], car[1:2, :]))
        car[0:1, :] = cr
        car[1:2, :] = ci
        ck_ref[0:1, :] = cr
        ck_ref[1:2, :] = ci
        hrb = xr[...].astype(BF16)
        hib = xi[...].astype(BF16)
        hr_ref[...] = hrb
        hi_ref[...] = hib
        y_ref[...] = _dot(hrb, ccr_ref[...]) - _dot(hib, cci_ref[...]) + d_ref[...] * u
        if end is not None:
            end()

    extra = carried.arrays if carried is not None else []
    extra_out = carried.out_shapes if carried is not None else []
    extra_sems = carried.sems if carried is not None else []
    return pl.pallas_call(
        body, name="s5_fwd", grid=(S5_SPLIT, nt),
        in_specs=[pl.BlockSpec((tm, S5_UC), lambda j, t: (t, j)),
                  pl.BlockSpec((None, S5_UC, S5_CC), lambda j, t: (j, 0, 0)),
                  pl.BlockSpec((None, S5_UC, S5_CC), lambda j, t: (j, 0, 0)),
                  pl.BlockSpec((None, S5_CC, S5_UC), lambda j, t: (j, 0, 0)),
                  pl.BlockSpec((None, S5_CC, S5_UC), lambda j, t: (j, 0, 0)),
                  pl.BlockSpec((8, 8, S5_CC), lambda j, t: (0, 0, j)),
                  pl.BlockSpec((1, S5_UC), lambda j, t: (0, j))] + [ANY] * len(extra),
        out_specs=[pl.BlockSpec((tm, S5_UC), lambda j, t: (t, j)),
                   pl.BlockSpec((None, 2, S5_CC), lambda j, t: (t, 0, j)),
                   pl.BlockSpec((tm, S5_CC), lambda j, t: (t, j)),
                   pl.BlockSpec((tm, S5_CC), lambda j, t: (t, j))] + [ANY] * len(extra_out),
        out_shape=[jax.ShapeDtypeStruct((L, S5_WIDTH), F32), jax.ShapeDtypeStruct((nt, 2, S5_COLS), F32),
                   jax.ShapeDtypeStruct((L, S5_COLS), BF16), jax.ShapeDtypeStruct((L, S5_COLS), BF16)] + extra_out,
        scratch_shapes=[pltpu.VMEM((tm, S5_CC), F32), pltpu.VMEM((tm, S5_CC), F32), pltpu.VMEM((2, S5_CC), F32)]
        + extra_sems,
        compiler_params=_cp("arbitrary" if carried is not None else "parallel", "arbitrary"),
    )(z, bbd_re, bbd_im, ccd_re, ccd_im, tab, dskip, *extra)


def s5_bwd(z, dy, dz, ckpt, hrb, hib, bbd_re, bbd_im, ccd_re, ccd_im, tab, dskip, tm=S5_TILE, carried=None):
    L = z.shape[0]
    tm = min(tm, L)
    nt = L // tm

    def body(*refs):
        top = end = None
        if carried is not None:
            refs, parts = carried.split(refs, 12, 7, 7)
            top, end = carried.hooks(parts, (S5_SPLIT, nt))
            top()
        (u_ref, dy_ref, dz_ref, ck_ref, hrb_ref, hib_ref, bbr_ref, bbi_ref, ccr_ref, cci_ref, tab_ref, d_ref,
         du_ref, da_ref, dbr_ref, dbi_ref, dcr_ref, dci_ref, dd_ref, hr, hi, gr, gi, car, acr, aci) = refs
        t = pl.program_id(1)
        tt = nt - 1 - t

        @pl.when(t == 0)
        def _():
            for r in (car, acr, aci, dbr_ref, dbi_ref, dcr_ref, dci_ref, dd_ref):
                r[...] = jnp.zeros_like(r)

        u = u_ref[...]
        ub = u.astype(BF16)
        dyv = dy_ref[...]
        dyb = dyv.astype(BF16)
        tabs = [None] * 8 + [tab_ref[k] for k in range(8, 16)]

        live = (tt > 0).astype(F32)
        hr[0:8, :] = jnp.broadcast_to(ck_ref[0:1, :] * live, (8, S5_CC))
        hi[0:8, :] = jnp.broadcast_to(ck_ref[1:2, :] * live, (8, S5_CC))
        hrb = hrb_ref[...]
        hib = hib_ref[...]
        hr[8:, :] = hrb.astype(F32)
        hi[8:, :] = hib.astype(F32)
        dcr_ref[...] += _dot_tn(hrb, dyb)
        dci_ref[...] -= _dot_tn(hib, dyb)

        gr[...] = _dot_nt(dyb, ccr_ref[...])
        gi[...] = -_dot_nt(dyb, cci_ref[...])
        row0 = lax.broadcasted_iota(jnp.int32, (8, S5_CC), 0) == 0

        def rblk(k, c):
            i = tm // 8 - 1 - k
            r0 = pl.multiple_of(i * 8, 8)
            a, b = _scan_block(gr[pl.ds(r0, 8), :], gi[pl.ds(r0, 8), :], tabs, 8, c[0], c[1], True)
            gr[pl.ds(r0, 8), :] = a
            gi[pl.ds(r0, 8), :] = b
            r1 = pl.multiple_of(i * 8 + 8, 8)
            hpr = jnp.where(row0, pltpu.roll(hr[pl.ds(r0, 8), :], 1, 0), pltpu.roll(hr[pl.ds(r1, 8), :], 1, 0))
            hpi = jnp.where(row0, pltpu.roll(hi[pl.ds(r0, 8), :], 1, 0), pltpu.roll(hi[pl.ds(r1, 8), :], 1, 0))
            acr[...] += a * hpr + b * hpi
            aci[...] += b * hpr - a * hpi
            return a[0:1, :], b[0:1, :]

        cr, ci = lax.fori_loop(0, tm // 8, rblk, (car[0:1, :], car[1:2, :]))
        car[0:1, :] = cr
        car[1:2, :] = ci

        grb = gr[...].astype(BF16)
        gib = gi[...].astype(BF16)
        du_ref[...] = (_dot_nt(grb, bbr_ref[...]) + _dot_nt(gib, bbi_ref[...]) + d_ref[...] * dyv).astype(BF16)
        dbr_ref[...] += _dot_tn(ub, grb)
        dbi_ref[...] += _dot_tn(ub, gib)
        dd_ref[...] += jnp.sum(dyv * u, axis=0, keepdims=True)

        @pl.when(t == nt - 1)
        def _():
            da_ref[0:1, :] = jnp.sum(acr[...], axis=0, keepdims=True)
            da_ref[1:2, :] = jnp.sum(aci[...], axis=0, keepdims=True)

        if end is not None:
            end()

    extra = carried.arrays if carried is not None else []
    extra_out = carried.out_shapes if carried is not None else []
    extra_sems = carried.sems if carried is not None else []
    chunk = lambda a, b: pl.BlockSpec((None, a, b), lambda j, t: (j, 0, 0))
    return pl.pallas_call(
        body, name="s5_bwd", grid=(S5_SPLIT, nt),
        in_specs=[pl.BlockSpec((tm, S5_UC), lambda j, t: (nt - 1 - t, j)),
                  pl.BlockSpec((tm, S5_UC), lambda j, t: (nt - 1 - t, j)),
                  ANY,
                  pl.BlockSpec((None, 2, S5_CC), lambda j, t: (jnp.maximum(nt - 2 - t, 0), 0, j)),
                  pl.BlockSpec((tm, S5_CC), lambda j, t: (nt - 1 - t, j)),
                  pl.BlockSpec((tm, S5_CC), lambda j, t: (nt - 1 - t, j)),
                  chunk(S5_UC, S5_CC), chunk(S5_UC, S5_CC), chunk(S5_CC, S5_UC), chunk(S5_CC, S5_UC),
                  pl.BlockSpec((16, 8, S5_CC), lambda j, t: (0, 0, j)),
                  pl.BlockSpec((1, S5_UC), lambda j, t: (0, j))] + [ANY] * len(extra),
        out_specs=[pl.BlockSpec((tm, S5_UC), lambda j, t: (nt - 1 - t, j)),
                   pl.BlockSpec((None, 2, S5_CC), lambda j, t: (j, 0, 0)),
                   chunk(S5_UC, S5_CC), chunk(S5_UC, S5_CC), chunk(S5_CC, S5_UC), chunk(S5_CC, S5_UC),
                   pl.BlockSpec((1, S5_UC), lambda j, t: (0, j))] + [ANY] * len(extra_out),
        out_shape=[jax.ShapeDtypeStruct(dz.shape, dz.dtype),
                   jax.ShapeDtypeStruct((S5_SPLIT, 2, S5_CC), F32),
                   jax.ShapeDtypeStruct((S5_SPLIT, S5_UC, S5_CC), F32),
                   jax.ShapeDtypeStruct((S5_SPLIT, S5_UC, S5_CC), F32),
                   jax.ShapeDtypeStruct((S5_SPLIT, S5_CC, S5_UC), F32),
                   jax.ShapeDtypeStruct((S5_SPLIT, S5_CC, S5_UC), F32),
                   jax.ShapeDtypeStruct((1, S5_WIDTH), F32)] + extra_out,
        scratch_shapes=[pltpu.VMEM((tm + 8, S5_CC), F32), pltpu.VMEM((tm + 8, S5_CC), F32),
                        pltpu.VMEM((tm, S5_CC), F32), pltpu.VMEM((tm, S5_CC), F32),
                        pltpu.VMEM((2, S5_CC), F32), pltpu.VMEM((8, S5_CC), F32), pltpu.VMEM((8, S5_CC), F32)]
        + extra_sems,
        input_output_aliases={2: 0},
        compiler_params=_cp("arbitrary" if carried is not None else "parallel", "arbitrary"),
    )(z, dy, dz, ckpt, hrb, hib, bbd_re, bbd_im, ccd_re, ccd_im, tab, dskip, *extra)


_EYE8 = np.eye(S5_GROUPS // S5_SPLIT, dtype=np.float32)


def _blockdiag(a):
    g, r, c = a.shape
    a = a.reshape(S5_SPLIT, g // S5_SPLIT, r, c)
    out = a[:, :, :, None, :] * _EYE8[None, :, None, :, None].astype(a.dtype)
    return out.reshape(S5_SPLIT, (g // S5_SPLIT) * r, (g // S5_SPLIT) * c)


def _blockdiag_extract(a, r, c):
    n = S5_GROUPS // S5_SPLIT
    a = a.reshape(S5_SPLIT, n, r, n, c)
    d = jnp.stack([a[:, k, :, k, :] for k in range(n)], axis=1)
    return d.reshape(S5_GROUPS, r, c)


def s5_mixer_core_fwd(z, lam_re, lam_im, log_dt, b_re, b_im, c_re, c_im, d_skip, carried=None):
    bt_re = jnp.swapaxes(b_re, 1, 2)
    bt_im = jnp.swapaxes(b_im, 1, 2)
    logdt = log_dt.reshape(S5_GROUPS, 1)
    bb_re, bb_im = s5_param_fwd(lam_re, lam_im, logdt, bt_re, bt_im)
    flat = lambda a: a.reshape(1, S5_COLS)
    tab = s5_tables(flat(lam_re), flat(lam_im), flat(jnp.broadcast_to(logdt, (S5_GROUPS, S5_STATE))))
    bbd_re = _blockdiag(bb_re).astype(BF16)
    bbd_im = _blockdiag(bb_im).astype(BF16)
    ccd_re = _blockdiag(jnp.swapaxes(c_re, 1, 2)).astype(BF16)
    ccd_im = _blockdiag(jnp.swapaxes(c_im, 1, 2)).astype(BF16)
    dsk = d_skip.reshape(1, S5_WIDTH)
    y, ckpt, hrb, hib, *landed = s5_fwd(z, bbd_re, bbd_im, ccd_re, ccd_im, tab, dsk, carried=carried)
    saved = (logdt, bt_re, bt_im, bbd_re, bbd_im, ccd_re, ccd_im, tab, dsk, ckpt, hrb, hib)
    return y, saved, landed


def s5_b_from_dense(dense):
    return jnp.swapaxes(dense.reshape(S5_GROUPS, S5_GROUP, S5_STATE), 1, 2)


def s5_mixer_core_bwd(z, dy, dz, lam_re, lam_im, saved, carried=None):
    logdt, bt_re, bt_im, bbd_re, bbd_im, ccd_re, ccd_im, tab, dsk, ckpt, hrb, hib = saved
    dz, da, dbr, dbi, dcr, dci, dd, *landed = s5_bwd(z, dy, dz, ckpt, hrb, hib, bbd_re, bbd_im, ccd_re, ccd_im, tab,
                                                     dsk, carried=carried)
    d_ab_re = da[:, 0, :].reshape(S5_GROUPS, S5_STATE)
    d_ab_im = da[:, 1, :].reshape(S5_GROUPS, S5_STATE)
    d_bb_re = _blockdiag_extract(dbr, S5_GROUP, S5_STATE)
    d_bb_im = _blockdiag_extract(dbi, S5_GROUP, S5_STATE)
    g_lr, g_li, g_ld, g_btr, g_bti = s5_param_bwd(lam_re, lam_im, logdt, bt_re, bt_im,
                                                  d_ab_re, d_ab_im, d_bb_re, d_bb_im)
    g_cre = jnp.swapaxes(_blockdiag_extract(dcr, S5_STATE, S5_GROUP), 1, 2)
    g_cim = jnp.swapaxes(_blockdiag_extract(dci, S5_STATE, S5_GROUP), 1, 2)
    grads = dict(lambda_re=g_lr, lambda_im=g_li, log_dt=g_ld.reshape(S5_GROUPS), b_re=g_btr, b_im=g_bti,
                 c_re=g_cre, c_im=g_cim, d=dd.reshape(S5_WIDTH))
    return dz, grads, landed


Z_U, Z_GA, Z_VAL, Z_GLU, Z_GB = range(5)
SUBLANES = 8


def _shifted_copies(buf, tm):
    n = tm + CONV_HALO - SUBLANES
    for r in range(1, SUBLANES):
        buf[r, 0:n, :] = buf[0, pl.ds(r, n), :]


CONV_ROWS = 32


def _shifted_rows(buf, start, rows, base=0):
    return buf[start % SUBLANES, pl.ds(base + (start - start % SUBLANES), rows), :]


def conv_fwd(z, conv_w, conv_b, tm=ROW_TILE):
    L = z.shape[0]
    tm = min(tm, L)
    nt = L // tm
    hb = tm // CONV_HALO
    C = CONV_WIDTH

    def body(val_ref, glu_ref, valh_ref, gluh_ref, w_ref, b_ref, c_ref, vsh):
        live = (pl.program_id(0) > 0).astype(F32)
        vsh[0, 0:CONV_HALO, :] = valh_ref[...] * _sigmoid(gluh_ref[...]) * live
        vsh[0, CONV_HALO:, :] = val_ref[...] * _sigmoid(glu_ref[...])
        _shifted_copies(vsh, tm)

        def rows(i, carry):
            base = pl.multiple_of(i * CONV_ROWS, CONV_ROWS)
            acc = jnp.broadcast_to(b_ref[...], (CONV_ROWS, C))
            for k in range(CONV_KERNEL):
                acc = acc + w_ref[k:k + 1, :] * _shifted_rows(vsh, CONV_HALO - CONV_KERNEL + 1 + k, CONV_ROWS, base)
            c_ref[pl.ds(base, CONV_ROWS), :] = acc
            return carry

        lax.fori_loop(0, tm // CONV_ROWS, rows, 0)

    cur = lambda col: pl.BlockSpec((tm, C), lambda t: (t, col))
    prev = lambda col: pl.BlockSpec((CONV_HALO, C), lambda t: (jnp.maximum(t * hb - 1, 0), col))
    return pl.pallas_call(
        body, name="conv_fwd", grid=(nt,),
        in_specs=[cur(Z_VAL), cur(Z_GLU), prev(Z_VAL), prev(Z_GLU), _full(conv_w.shape), _full(conv_b.shape)],
        out_specs=pl.BlockSpec((tm, C), lambda t: (t, 0)),
        out_shape=jax.ShapeDtypeStruct((L, C), F32),
        scratch_shapes=[pltpu.VMEM((8, tm + CONV_HALO, C), F32)],
        compiler_params=_cp("parallel"),
    )(z, z, z, z, conv_w, conv_b)


def conv_bwd(z, dc, dz, conv_w, tm=ROW_TILE, carried=None):
    L = z.shape[0]
    tm = min(tm, L)
    nt = L // tm
    hb = tm // CONV_HALO
    nh = L // CONV_HALO
    C = CONV_WIDTH
    off = CONV_HALO - CONV_KERNEL + 1

    def body(*refs):
        top = end = None
        if carried is not None:
            refs, parts = carried.split(refs, 8, 3, 3)
            top, end = carried.hooks(parts, (nt,))
            top()
        val_ref, glu_ref, valh_ref, gluh_ref, dc_ref, dcn_ref, dz_ref, w_ref, dvg_ref, dw_ref, db_ref, vsh, dsh, wacc = refs
        t = pl.program_id(0)

        @pl.when(t == 0)
        def _():
            wacc[...] = jnp.zeros_like(wacc)
            db_ref[...] = jnp.zeros_like(db_ref)

        val = val_ref[...]
        sg = _sigmoid(glu_ref[...])
        vsh[0, 0:CONV_HALO, :] = valh_ref[...] * _sigmoid(gluh_ref[...]) * (t > 0).astype(F32)
        vsh[0, CONV_HALO:, :] = val * sg
        dcv = dc_ref[...]
        dsh[0, 0:tm, :] = dcv
        dsh[0, tm:, :] = dcn_ref[...] * (t < nt - 1).astype(F32)
        _shifted_copies(vsh, tm)
        _shifted_copies(dsh, tm)

        def rows(i, carry):
            base = pl.multiple_of(i * CONV_ROWS, CONV_ROWS)
            dcr = dc_ref[pl.ds(base, CONV_ROWS), :]
            dv = jnp.zeros((CONV_ROWS, C), F32)
            for k in range(CONV_KERNEL):
                dv = dv + w_ref[k:k + 1, :] * _shifted_rows(dsh, CONV_KERNEL - 1 - k, CONV_ROWS, base)
                prod = dcr * _shifted_rows(vsh, off + k, CONV_ROWS, base)
                wacc[k] += jnp.sum(prod.reshape(CONV_ROWS // SUBLANES, SUBLANES, C), axis=0)
            valr = val_ref[pl.ds(base, CONV_ROWS), :]
            sgr = _sigmoid(glu_ref[pl.ds(base, CONV_ROWS), :])
            dvg_ref[pl.ds(base, CONV_ROWS), 0:C] = (dv * sgr).astype(BF16)
            dvg_ref[pl.ds(base, CONV_ROWS), C:] = (dv * valr * sgr * (1.0 - sgr)).astype(BF16)
            return carry

        lax.fori_loop(0, tm // CONV_ROWS, rows, 0)
        db_ref[...] += jnp.sum(dcv, axis=0, keepdims=True)

        @pl.when(t == nt - 1)
        def _():
            dw_ref[...] = jnp.sum(wacc[...], axis=1)

        if end is not None:
            end()

    extra = carried.arrays if carried is not None else []
    extra_out = carried.out_shapes if carried is not None else []
    extra_sems = carried.sems if carried is not None else []
    cur = lambda col: pl.BlockSpec((tm, C), lambda t: (t, col))
    prev = lambda col: pl.BlockSpec((CONV_HALO, C), lambda t: (jnp.maximum(t * hb - 1, 0), col))
    nxt = pl.BlockSpec((CONV_HALO, C), lambda t: (jnp.minimum((t + 1) * hb, nh - 1), 0))
    row = pl.BlockSpec((tm, C), lambda t: (t, 0))
    return pl.pallas_call(
        body, name="conv_bwd", grid=(nt,),
        in_specs=[cur(Z_VAL), cur(Z_GLU), prev(Z_VAL), prev(Z_GLU), row, nxt, ANY, _full(conv_w.shape)]
        + [ANY] * len(extra),
        out_specs=[pl.BlockSpec((tm, 2 * C), lambda t: (t, 1)), _full((CONV_HALO, C)), _full((1, C))]
        + [ANY] * len(extra_out),
        out_shape=[jax.ShapeDtypeStruct(dz.shape, dz.dtype),
                   jax.ShapeDtypeStruct((CONV_HALO, C), F32), jax.ShapeDtypeStruct((1, C), F32)] + extra_out,
        scratch_shapes=[pltpu.VMEM((8, tm + CONV_HALO, C), F32), pltpu.VMEM((8, tm + CONV_HALO, C), F32),
                        pltpu.VMEM((CONV_HALO, SUBLANES, C), F32)] + extra_sems,
        input_output_aliases={6: 0},
        compiler_params=_cp("arbitrary"),
    )(z, z, z, z, dc, dc, dz, conv_w, *extra)


def _ln_parts(c):
    mu = jnp.mean(c, axis=-1, keepdims=True)
    cc = c - mu
    rstd = lax.rsqrt(jnp.mean(cc * cc, axis=-1, keepdims=True) + EPS)
    return rstd, cc * rstd


def _ev_tail_branches(ys, c, wglu, bglu, lng, lnb):
    z1 = _gelu(ys)
    z1b = z1.astype(BF16)
    sg = _sigmoid(_dot_rows(z1b, wglu) + bglu)
    out = z1 * sg
    rstd, chat = _ln_parts(c)
    cn = chat * lng + lnb
    return z1, z1b, sg, out, rstd, chat, cn


def ev_tail_fwd(ys, z, c, x0, wglu, bglu, lng, lnb, wout, tm=ROW_TILE):
    L, D = x0.shape
    tm = min(tm, L)
    W = S5_WIDTH

    def body(ys_ref, ga_ref, c_ref, gb_ref, x_ref, wglu_ref, bglu_ref, lng_ref, lnb_ref, wout_ref, o_ref):
        _, _, _, out, _, _, cn = _ev_tail_branches(ys_ref[...], c_ref[...], wglu_ref, bglu_ref[...],
                                                   lng_ref[...], lnb_ref[...])
        ya = (out * _silu(ga_ref[...])).astype(BF16)
        yb = (_silu(cn) * _silu(gb_ref[...])).astype(BF16)
        o_ref[...] = x_ref[...] + _dot_rows(jnp.concatenate([ya, yb], axis=1), wout_ref)

    row = lambda n, col=0: pl.BlockSpec((tm, n), lambda t: (t, col))
    return pl.pallas_call(
        body, name="ev_tail_fwd", grid=(L // tm,),
        in_specs=[row(W), row(W, Z_GA), row(W), row(W, Z_GB), row(D), _full(wglu.shape), _full(bglu.shape),
                  _full(lng.shape), _full(lnb.shape), _full(wout.shape)],
        out_specs=row(D), out_shape=jax.ShapeDtypeStruct((L, D), F32), compiler_params=_cp("parallel"),
    )(ys, z, c, z, x0, wglu, bglu, lng, lnb, wout)


def ev_tail_bwd(ys, z, c, dx1, wglu, bglu, lng, lnb, wout, tm=ROW_TILE):
    L, D = dx1.shape
    tm = min(tm, L)
    W = S5_WIDTH

    def body(ys_ref, ga_ref, c_ref, gb_ref, dx_ref, wglu_ref, bglu_ref, lng_ref, lnb_ref, wout_ref,
             dys_ref, dc_ref, dz_ref, r_ref, z1_ref, dt_ref, dbg_ref, dlg_ref, dlb_ref):
        @pl.when(pl.program_id(0) == 0)
        def _():
            for r in (dbg_ref, dlg_ref, dlb_ref):
                r[...] = jnp.zeros_like(r)

        ys, ga, gb = ys_ref[...], ga_ref[...], gb_ref[...]
        z1, z1b, sg, out, rstd, chat, cn = _ev_tail_branches(ys, c_ref[...], wglu_ref, bglu_ref[...],
                                                             lng_ref[...], lnb_ref[...])
        (sga, dsga), (sgb, dsgb), (scn, dscn) = _silu_pair(ga), _silu_pair(gb), _silu_pair(cn)
        r_ref[:, 0:W] = (out * sga).astype(BF16)
        r_ref[:, W:] = (scn * sgb).astype(BF16)
        dr = _dot_nt_rows(dx_ref[...].astype(BF16), wout_ref)
        dra, drb = dr[:, 0:W], dr[:, W:]
        dz_ref[...] = jnp.zeros_like(dz_ref)
        dz_ref[:, Z_GA * W:(Z_GA + 1) * W] = (dra * out * dsga).astype(BF16)
        dout = dra * sga
        dt = dout * z1 * sg * (1.0 - sg)
        dtb = dt.astype(BF16)
        dz1 = dout * sg + _dot_nt_rows(dtb, wglu_ref)
        dys_ref[...] = dz1 * _dgelu(ys)
        z1_ref[...] = z1b
        dt_ref[...] = dtb
        dbg_ref[...] += jnp.sum(dt, axis=0, keepdims=True)
        dz_ref[:, Z_GB * W:(Z_GB + 1) * W] = (drb * scn * dsgb).astype(BF16)
        dcn = drb * sgb * dscn
        dlg_ref[...] += jnp.sum(dcn * chat, axis=0, keepdims=True)
        dlb_ref[...] += jnp.sum(dcn, axis=0, keepdims=True)
        dch = dcn * lng_ref[...]
        dc_ref[...] = rstd * (dch - jnp.mean(dch, axis=-1, keepdims=True)
                              - chat * jnp.mean(dch * chat, axis=-1, keepdims=True))

    row = lambda n, col=0: pl.BlockSpec((tm, n), lambda t: (t, col))
    f = lambda n, dt: jax.ShapeDtypeStruct((L, n), dt)
    vec = jax.ShapeDtypeStruct((1, W), F32)
    return pl.pallas_call(
        body, name="ev_tail_bwd", grid=(L // tm,),
        in_specs=[row(W), row(W, Z_GA), row(W), row(W, Z_GB), row(D), _full(wglu.shape), _full(bglu.shape),
                  _full(lng.shape), _full(lnb.shape), _full(wout.shape)],
        out_specs=[row(W), row(W), row(EVEN_IN), row(D), row(W), row(W), _full((1, W)), _full((1, W)), _full((1, W))],
        out_shape=[f(W, F32), f(W, F32), f(EVEN_IN, BF16), f(D, BF16), f(W, BF16), f(W, BF16), vec, vec, vec],
        compiler_params=_cp("arbitrary"),
    )(ys, z, c, z, dx1, wglu, bglu, lng, lnb, wout)


XA_SCALE = XA_HEAD_DIM ** -0.5


def _xa_forward(xv, g, wqg, kv):
    D = D_MODEL
    _, xhat = _rms_parts(xv)
    hb = (xhat * g).astype(BF16)
    qb = (_dot_cols(hb, wqg, (0, 1)) * XA_SCALE).astype(BF16)
    gate = _dot_cols(hb, wqg, (2, 3))
    ps, os_ = [], []
    for h in range(XA_HEADS):
        lo, hi = h * XA_HEAD_DIM, (h + 1) * XA_HEAD_DIM
        s = _dot_nt(qb[:, lo:hi], kv[:, lo:hi])
        e = jnp.exp(s - jnp.max(s, axis=-1, keepdims=True))
        inv = 1.0 / jnp.sum(e, axis=-1, keepdims=True)
        ps.append((e, inv))
        os_.append(_dot(e.astype(BF16), kv[:, D + lo:D + hi]) * inv)
    return hb, qb, gate, ps, jnp.concatenate(os_, axis=1)


def xa_fwd(x, g, wqg, kv, wo, layer, name, tm=MM_TILE):
    L, D = x.shape
    tm = min(tm, L)

    def body(x_ref, g_ref, wqg_ref, kv_ref, wo_ref, o_ref):
        xv = x_ref[...]
        _, _, gate, _, o = _xa_forward(xv, g_ref[...], wqg_ref, kv_ref[...])
        o_ref[...] = xv + _dot_rows((o * _silu(gate)).astype(BF16), wo_ref)

    row = pl.BlockSpec((tm, D), lambda t: (t, 0))
    return pl.pallas_call(
        body, name=name, grid=(L // tm,),
        in_specs=[row, _full(g.shape), _wspec(wqg, layer), _full(kv.shape), _wspec(wo, layer)],
        out_specs=row, out_shape=jax.ShapeDtypeStruct((L, D), F32), compiler_params=_cp("parallel"),
    )(x, g, wqg, kv, wo)


def _loss_head(xv, gv, tv):
    D = xv.shape[-1]
    _, xhat = _rms_parts(xv)
    err = xhat * gv - tv
    loss = 0.5 * jnp.sum(jnp.sum(err * err, axis=-1, keepdims=True), axis=0, keepdims=True) / D
    dx, dg = _rms_bwd(xv, gv, err * (1.0 / D))
    return loss, dx, dg


def xa_fwd_loss(x_in, r, wout, g, wqg, kv, wo, layer, target, gf, name, tm=MM_TILE):
    L, D = x_in.shape
    tm = min(tm, L)

    def body(xi_ref, r_ref, wout_ref, g_ref, wqg_ref, kv_ref, wo_ref, t_ref, gf_ref, x_ref, loss_ref, dx_ref, dg_ref):
        @pl.when(pl.program_id(0) == 0)
        def _():
            loss_ref[...] = jnp.zeros_like(loss_ref)
            dg_ref[...] = jnp.zeros_like(dg_ref)

        xv = xi_ref[...] + _dot_rows(r_ref[...], wout_ref)
        x_ref[...] = xv
        _, _, gate, _, o = _xa_forward(xv, g_ref[...], wqg_ref, kv_ref[...])
        y = xv + _dot_rows((o * _silu(gate)).astype(BF16), wo_ref)
        loss, dx, dg = _loss_head(y, gf_ref[...], t_ref[...])
        loss_ref[...] += loss
        dx_ref[...] = dx
        dg_ref[...] += dg

    row = pl.BlockSpec((tm, D), lambda t: (t, 0))
    return pl.pallas_call(
        body, name=name, grid=(L // tm,),
        in_specs=[row, row, _full(wout.shape), _full(g.shape), _wspec(wqg, layer), _full(kv.shape), _wspec(wo, layer),
                  row, _full(gf.shape)],
        out_specs=[row, _full((1, 128)), row, _full((1, D))],
        out_shape=[jax.ShapeDtypeStruct((L, D), F32), jax.ShapeDtypeStruct((1, 128), F32),
                   jax.ShapeDtypeStruct((L, D), F32), jax.ShapeDtypeStruct((1, D), F32)],
        compiler_params=_cp("arbitrary"),
    )(x_in, r, wout, g, wqg, kv, wo, target, gf)


def xa_bwd(x, dxo, g, wqg, kv, wo, layer, name, tm=MM_TILE):
    L, D = x.shape
    tm = min(tm, L)

    def body(x_ref, dxo_ref, g_ref, wqg_ref, kv_ref, wo_ref, dx_ref, dqg_ref, h_ref, r_ref, dkv_ref, dg_ref):
        @pl.when(pl.program_id(0) == 0)
        def _():
            dkv_ref[...] = jnp.zeros_like(dkv_ref)
            dg_ref[...] = jnp.zeros_like(dg_ref)

        xv = x_ref[...]
        kv = kv_ref[...]
        hb, qb, gate, ps, o = _xa_forward(xv, g_ref[...], wqg_ref, kv)
        sgate, dsgate = _silu_pair(gate)
        h_ref[...] = hb
        r_ref[...] = (o * sgate).astype(BF16)
        dxo = dxo_ref[...]
        dr = _dot_nt_rows(dxo.astype(BF16), wo_ref)
        do = dr * sgate
        dqg_ref[:, D:] = (dr * o * dsgate).astype(BF16)
        dob = do.astype(BF16)
        doo = do * o
        for h in range(XA_HEADS):
            lo, hi = h * XA_HEAD_DIM, (h + 1) * XA_HEAD_DIM
            e, inv = ps[h]
            dp = _dot_nt(dob[:, lo:hi], kv[:, D + lo:D + hi])
            dkv_ref[:, D + lo:D + hi] += _dot_tn(e.astype(BF16), (do[:, lo:hi] * inv).astype(BF16))
            rs = jnp.sum(doo[:, lo:hi], axis=-1, keepdims=True)
            dsb = (e * ((dp - rs) * inv)).astype(BF16)
            dqg_ref[:, lo:hi] = (_dot(dsb, kv[:, lo:hi]) * XA_SCALE).astype(BF16)
            dkv_ref[:, lo:hi] += _dot_tn(dsb, qb[:, lo:hi])
        dh = _dot_nt_cols(_col_pieces(dqg_ref[...], D // 2), wqg_ref)
        dx, dg = _rms_bwd(xv, g_ref[...], dh)
        dx_ref[...] = dxo + dx
        dg_ref[...] += dg

    row = lambda n: pl.BlockSpec((tm, n), lambda t: (t, 0))
    return pl.pallas_call(
        body, name=name, grid=(L // tm,),
        in_specs=[row(D), row(D), _full(g.shape), _wspec(wqg, layer), _full(kv.shape), _wspec(wo, layer)],
        out_specs=[row(D), row(2 * D), row(D), row(D), _full(kv.shape), _full((1, D))],
        out_shape=[jax.ShapeDtypeStruct((L, D), F32), jax.ShapeDtypeStruct((L, 2 * D), BF16),
                   jax.ShapeDtypeStruct((L, D), BF16), jax.ShapeDtypeStruct((L, D), BF16),
                   jax.ShapeDtypeStruct(kv.shape, F32), jax.ShapeDtypeStruct((1, D), F32)],
        compiler_params=_cp("arbitrary"),
    )(x, dxo, g, wqg, kv, wo)


ATT_SCALE = ATT_HEAD_DIM ** -0.5
ATT_PAIRS = ATT_HEADS // 2
SKEW_LANES = 1024
REL_LANES = 384


def _skew(x, left):
    amt = (ATT_QB - 1) - lax.broadcasted_iota(jnp.int32, (ATT_QB, 1), 0)
    for bit in range(8):
        sh = (SKEW_LANES - (1 << bit)) if left else (1 << bit)
        x = jnp.where(((amt >> bit) & 1) == 1, pltpu.roll(x, sh, 1), x)
    return x


def _dist_onehot(shape, dist_axis):
    j = lax.broadcasted_iota(jnp.int32, shape, dist_axis)
    r = lax.broadcasted_iota(jnp.int32, shape, 1 - dist_axis)
    return (jnp.clip((ATT_WIN - 1) - j, -MAX_REL, MAX_REL) + MAX_REL == r).astype(BF16)


def _dot_exact(v, onehot):
    acc = jnp.zeros((v.shape[0], onehot.shape[1]), F32)
    rem = v
    for _ in range(3):
        part = rem.astype(BF16)
        acc = acc + _dot(part, onehot)
        rem = rem - part.astype(F32)
    return acc


ATT_EDGE = ATT_PAD // ATT_QB


def att_bias(rel_bias, carried=None):
    H = rel_bias.shape[0]
    rb = jnp.pad(rel_bias, ((0, 0), (0, REL_LANES - rel_bias.shape[1]))).reshape(H, 1, REL_LANES)

    def body(*refs):
        top = end = None
        if carried is not None:
            refs, parts = carried.split(refs, 1, 1, 0)
            top, end = carried.hooks(parts, (H,))
            top()
        rb_ref, o_ref = refs
        by_col = _dot_exact(jnp.broadcast_to(rb_ref[...], (8, REL_LANES)), _dist_onehot((REL_LANES, SKEW_LANES), 1))
        x = _skew(jnp.broadcast_to(by_col[0:1, :], (ATT_QB, SKEW_LANES)), left=True)[:, 0:ATT_WIN]
        qc = lax.broadcasted_iota(jnp.int32, (ATT_QB, 1), 0) // CHUNK + LEFT_CHUNKS
        col = lax.broadcasted_iota(jnp.int32, (1, ATT_WIN), 1)
        dc = qc - col // CHUNK
        band = (dc >= 0) & (dc <= LEFT_CHUNKS)
        for blk in range(ATT_EDGE + 1):
            o_ref[blk] = jnp.where(band & (col >= ATT_PAD - blk * ATT_QB), x, NEG)
        if end is not None:
            end()

    extra = carried.arrays if carried is not None else []
    extra_out = carried.out_shapes if carried is not None else []
    extra_sems = carried.sems if carried is not None else []
    return pl.pallas_call(
        body, name="att_bias", grid=(H,),
        in_specs=[pl.BlockSpec((None, 1, REL_LANES), lambda h: (h, 0, 0))] + [ANY] * len(extra),
        out_specs=[pl.BlockSpec((ATT_EDGE + 1, None, ATT_QB, ATT_WIN), lambda h: (0, h, 0, 0))] + [ANY] * len(extra_out),
        out_shape=[jax.ShapeDtypeStruct((ATT_EDGE + 1, H, ATT_QB, ATT_WIN), F32)] + extra_out,
        scratch_shapes=extra_sems,
        compiler_params=_cp("arbitrary" if carried is not None else "parallel"),
    )(rb, *extra)


def relbias_bwd(dbias):
    H = dbias.shape[0]

    def body(x_ref, o_ref):
        x = jnp.concatenate([x_ref[...], jnp.zeros((ATT_QB, SKEW_LANES - ATT_WIN), F32)], axis=1)
        col = jnp.sum(_skew(x, left=False), axis=0, keepdims=True)
        o_ref[...] = _dot_exact(jnp.broadcast_to(col, (8, SKEW_LANES)), _dist_onehot((SKEW_LANES, REL_LANES), 0))

    out = pl.pallas_call(
        body, name="relbias_bwd", grid=(H,),
        in_specs=[pl.BlockSpec((None, ATT_QB, ATT_WIN), lambda h: (h, 0, 0))],
        out_specs=pl.BlockSpec((None, 8, REL_LANES), lambda h: (h, 0, 0)),
        out_shape=jax.ShapeDtypeStruct((H, 8, REL_LANES), F32), compiler_params=_cp("parallel"),
    )(dbias)
    return out[:, 0, :2 * MAX_REL + 1]


def _ca_scores(qh, kw, bias):
    s = _dot_nt(qh, kw) + bias
    e = jnp.exp(s - jnp.max(s, axis=-1, keepdims=True))
    return e, 1.0 / jnp.sum(e, axis=-1, keepdims=True)


def _ca_head(qv, m):
    return jnp.where(m, qv, jnp.zeros_like(qv)) * ATT_SCALE


def _ca_bias_spec():
    return pl.BlockSpec((None, 2, ATT_QB, ATT_WIN), lambda hp, b: (jnp.minimum(b, ATT_EDGE), hp, 0, 0))


def ca_fwd(q, kvp, gate, bias):
    L, D = q.shape
    Lp = kvp.shape[0]
    nb = L // ATT_QB

    PP = 2
    W = PP * 128

    def body(q_ref, k_ref, v_ref, g_ref, b_ref, r_ref, o_ref):
        w = pl.multiple_of(pl.program_id(1) * ATT_QB, ATT_QB)
        first = lax.broadcasted_iota(jnp.int32, (1, 128), 1) < ATT_HEAD_DIM
        for pp in range(PP):
            sl = slice(pp * 128, (pp + 1) * 128)
            kw = k_ref[pl.ds(w, ATT_WIN), sl]
            vw = v_ref[pl.ds(w, ATT_WIN), sl]
            qv = q_ref[:, sl]
            outs = []
            for hh, m in enumerate((first, jnp.logical_not(first))):
                e, inv = _ca_scores(_ca_head(qv, m), kw, b_ref[2 * pp + hh])
                outs.append(_dot(e.astype(BF16), vw) * inv)
            o = jnp.where(first, outs[0], outs[1])
            r_ref[:, sl] = (o * _silu(g_ref[:, sl])).astype(BF16)
            o_ref[:, sl] = o.astype(BF16)

    blk = pl.BlockSpec((ATT_QB, W), lambda hp, b: (b, hp))
    bias_blk = pl.BlockSpec((None, 2 * PP, ATT_QB, ATT_WIN), lambda hp, b: (jnp.minimum(b, ATT_EDGE), hp, 0, 0))
    return pl.pallas_call(
        body, name="ca_fwd", grid=(ATT_PAIRS // PP, nb),
        in_specs=[blk, pl.BlockSpec((Lp, W), lambda hp, b: (0, hp)),
                  pl.BlockSpec((Lp, W), lambda hp, b: (0, ATT_PAIRS // PP + hp)), blk, bias_blk],
        out_specs=[blk, blk], out_shape=[jax.ShapeDtypeStruct((L, D), BF16), jax.ShapeDtypeStruct((L, D), BF16)],
        compiler_params=_cp("parallel", "arbitrary"),
    )(q, kvp, kvp, gate, bias)


def ca_bwd(q, kvp, gate, bias, dr, o):
    L, D = q.shape
    Lp = kvp.shape[0]
    nb = L // ATT_QB

    def body(q_ref, k_ref, v_ref, g_ref, b_ref, dr_ref, o_ref, dq_ref, dg_ref, dkb_ref, dvb_ref, db_ref,
             dk_ref, dv_ref):
        b = pl.program_id(1)

        @pl.when(b == 0)
        def _():
            for r in (dk_ref, dv_ref, db_ref):
                r[...] = jnp.zeros_like(r)

        w = pl.multiple_of(b * ATT_QB, ATT_QB)
        kw = k_ref[pl.ds(w, ATT_WIN), :]
        vw = v_ref[pl.ds(w, ATT_WIN), :]
        qv = q_ref[...]
        gate_v = g_ref[...]
        drv = dr_ref[...]
        o = o_ref[...].astype(F32)
        sgate, dsgate = _silu_pair(gate_v)
        do = drv * sgate
        doo = do * o
        first = lax.broadcasted_iota(jnp.int32, (1, 128), 1) < ATT_HEAD_DIM
        dqs = []
        dkw = jnp.zeros((ATT_WIN, 128), F32)
        dvw = jnp.zeros((ATT_WIN, 128), F32)
        for hh, m in enumerate((first, jnp.logical_not(first))):
            qh = _ca_head(qv, m)
            e, inv = _ca_scores(qh, kw, b_ref[hh])
            eb = e.astype(BF16)
            doh = jnp.where(m, do, 0.0)
            dp = _dot_nt(doh.astype(BF16), vw)
            dvw = dvw + _dot_tn(eb, (doh * inv).astype(BF16))
            rs = jnp.sum(jnp.where(m, doo, 0.0), axis=-1, keepdims=True)
            ds = e * ((dp - rs) * inv)
            db_ref[hh] += ds
            dsb = ds.astype(BF16)
            dqs.append(_dot(dsb, kw))
            dkw = dkw + _dot_tn(dsb, qh)
        dg_ref[...] = (drv * o * dsgate).astype(BF16)
        dq_ref[...] = (jnp.where(first, dqs[0], dqs[1]) * ATT_SCALE).astype(BF16)
        dk_ref[pl.ds(w, ATT_WIN), :] += dkw
        dv_ref[pl.ds(w, ATT_WIN), :] += dvw

        @pl.when(b == nb - 1)
        def _():
            dkb_ref[...] = dk_ref[...].astype(BF16)
            dvb_ref[...] = dv_ref[...].astype(BF16)

    blk = pl.BlockSpec((ATT_QB, 128), lambda hp, b: (b, hp))
    kblk = pl.BlockSpec((Lp, 128), lambda hp, b: (0, hp))
    vblk = pl.BlockSpec((Lp, 128), lambda hp, b: (0, ATT_PAIRS + hp))
    bblk = pl.BlockSpec((2, ATT_QB, ATT_WIN), lambda hp, b: (hp, 0, 0))
    return pl.pallas_call(
        body, name="ca_bwd", grid=(ATT_PAIRS, nb),
        in_specs=[blk, kblk, vblk, blk, _ca_bias_spec(), blk, blk],
        out_specs=[blk, blk, kblk, kblk, bblk],
        out_shape=[jax.ShapeDtypeStruct((L, D), BF16), jax.ShapeDtypeStruct((L, D), BF16),
                   jax.ShapeDtypeStruct((Lp, D), BF16), jax.ShapeDtypeStruct((Lp, D), BF16),
                   jax.ShapeDtypeStruct(bias.shape[1:], F32)],
        scratch_shapes=[pltpu.VMEM((Lp, 128), F32), pltpu.VMEM((Lp, 128), F32)],
        compiler_params=_cp("parallel", "arbitrary"),
    )(q, kvp, kvp, gate, bias, dr, o)


_ADAM_C1 = 1.0 / (1.0 - ADAM_B1 ** ADAM_STEP)
_ADAM_C2 = 1.0 / (1.0 - ADAM_B2 ** ADAM_STEP)


def _adam_update(w, g, m, v):
    mn = ADAM_B1 * m + (1.0 - ADAM_B1) * g
    vn = ADAM_B2 * v + (1.0 - ADAM_B2) * g * g
    delta = -ADAM_LR * ((mn * _ADAM_C1) / (jnp.sqrt(vn * _ADAM_C2) + ADAM_EPS) + ADAM_WD * w)
    return delta, mn, vn


def adamw(w, g, m, v, name, tr=512):
    R, C = w.shape
    tr = min(tr, R)

    def body(w_ref, g_ref, m_ref, v_ref, d_ref, mo_ref, vo_ref):
        d_ref[...], mo_ref[...], vo_ref[...] = _adam_update(w_ref[...], g_ref[...], m_ref[...], v_ref[...])

    blk = pl.BlockSpec((tr, C), lambda i: (i, 0))
    sh = jax.ShapeDtypeStruct((R, C), F32)
    return pl.pallas_call(
        body, name=name, grid=(R // tr,), in_specs=[blk] * 4, out_specs=[blk] * 3,
        out_shape=[sh] * 3, compiler_params=_cp("parallel"),
    )(w, g, m, v)


def adamw_allreduce(gathered, w, m, v, shard, name, slot=None):
    R, C = w.shape
    sharded = slot is None and gathered.shape[2] != C

    def body(s_ref, ga_ref, w_ref, m_ref, v_ref, g_ref, d_ref, mo_ref, vo_ref):
        take = (lambda d: ga_ref[d]) if slot is None else (lambda d: ga_ref[d, slot:slot + R, 0:C])
        g = take(0)
        for d in range(1, N_DEV):
            g = g + take(d)
        g_ref[...] = g
        d_ref[...], mo_ref[...], vo_ref[...] = _adam_update(w_ref[...], g, m_ref[...], v_ref[...])

    blk = pl.BlockSpec((R, C), lambda i, s_ref: (0, 0))
    if slot is not None:
        gblk = pl.BlockSpec(gathered.shape, lambda i, s_ref: (0, 0, 0))
    else:
        gblk = pl.BlockSpec((N_DEV, R, C),
                            (lambda i, s_ref: (0, 0, s_ref[0])) if sharded else (lambda i, s_ref: (0, 0, 0)))
    sh = jax.ShapeDtypeStruct((R, C), F32)
    return pl.pallas_call(
        body, name=name,
        grid_spec=pltpu.PrefetchScalarGridSpec(num_scalar_prefetch=1, grid=(1,), in_specs=[gblk, blk, blk, blk],
                                               out_specs=[blk] * 4),
        out_shape=[sh] * 4, compiler_params=_cp("arbitrary"),
    )(shard, gathered, w, m, v)


LATE = ("ev_s5_glu_w", "ev_w_out", "od_w_in", "od_w_out", "xa_w_qg", "xa_w_kv", "xa_w_o")
EARLY_GRADS = ("od_w_in", "od_w_out", "xa_w_qg", "xa_w_kv", "xa_w_o", "ev_w_out", "ev_s5_glu_w")


def _reduce_to_chip(gs, names, core, tag):
    from_sibling = sibling_send_other_half(gs, "sibling_send_" + tag)
    return [sum_with_sibling(gi, ri, core, "sum_sibling_" + n) for n, gi, ri in zip(names, gs, from_sibling)]


def local_step(x, mem, target, p, gw, late, bias, place, core):
    row = lambda a: a.reshape(1, -1)
    D = D_MODEL
    L = x.shape[0]
    g, big = {}, {}
    gw = dict(gw)

    z, h0b = norm_mm(x, p["ev_norm_g"], gw["ev_w_in"], [((0, 1, 2, 3), F32, 0)], "ev_in")
    ys, s5_saved, landed = s5_mixer_core_fwd(
        z, p["ev_s5_lambda_re"][0], p["ev_s5_lambda_im"][0], p["ev_s5_log_dt"][0], p["ev_s5_b_re"][0],
        p["ev_s5_b_im"][0], p["ev_s5_c_re"][0], p["ev_s5_c_im"][0], p["ev_s5_d"][0],
        carried=carried_allgather([late[n] for n in LATE]))
    for n, gth in zip(LATE, landed):
        rows = gth.shape[1]
        gw[n] = gth.reshape(N_CHIPS, 2, rows // 2, gth.shape[2]) if n.startswith("xa_") else gth
    memn_b = rms_fwd(mem, row(p["mem_norm_g"]), "mem_norm")
    kvs = [mm_cols(memn_b, gw["xa_w_kv"], l, f"xa_kv{l}", BF16) for l in range(2)]
    conv_w = p["ev_conv_w"][0]
    c = conv_fwd(z, conv_w, p["ev_conv_b"])
    tail = (gw["ev_s5_glu_w"], p["ev_s5_glu_b"], p["ev_conv_ln_g"], p["ev_conv_ln_b"], gw["ev_w_out"])
    x1 = ev_tail_fwd(ys, z, c, x, *tail)
    xa0 = (row(p["xa_norm_g"][0]), gw["xa_w_qg"], kvs[0], gw["xa_w_o"], 0)
    x2 = xa_fwd(x1, *xa0, "xa_fwd0")

    q, kvp, gate, h1b = norm_mm(x2, p["od_norm_g"], gw["od_w_in"],
                                [((0,), BF16, 0), ((1, 2), BF16, ATT_PAD), ((3,), F32, 0)], "od_in")
    kvp = zero_rows(kvp, ATT_PAD, "od_kv_pad")
    r, att_o = ca_fwd(q, kvp, gate, bias)
    xa1 = (row(p["xa_norm_g"][1]), gw["xa_w_qg"], kvs[1], gw["xa_w_o"], 1)
    x3, loss, dx4, dgf = xa_fwd_loss(x2, r, gw["od_w_out"], *xa1, target, row(p["final_norm_g"]), "od_out_xa_fwd1_loss")
    g["final_norm_g"] = dgf.reshape(D)

    dx3, dqg1, hx1, rx1, dkv1, dgxa1 = xa_bwd(x3, dx4, *xa1, "xa_bwd1")
    dwqg = mm_tn(hx1, dqg1, "xa_dwqg1", ("cols", 1))
    dwo = mm_tn(rx1, dx4, "xa_dwo1", ("rows", 1))

    big["od_w_out"] = mm_tn(r, dx3, "od_dwout", ("rows",))
    dr = mm_nt_rows(dx3, gw["od_w_out"], "od_out_bwd")
    dq, dgate, dkp, dvp, dbias = ca_bwd(q, kvp, gate, bias, dr, att_o)
    pieces, offs = (dq, dkp, dvp, dgate), (0, ATT_PAD, ATT_PAD, 0)
    dwin = None
    for s in range(N_CHIPS):
        dwin = mm_tn(h1b, pieces[s], f"od_dwin{s}", ("slab", s), into=dwin, b_off=offs[s],
                     bl=ATT_PAD if offs[s] else 1024)
    big["od_w_in"] = dwin
    dx2, dgod = mm_nt_normbwd(pieces, offs, gw["od_w_in"], x2, p["od_norm_g"], dx3, "od_in_bwd")
    g["od_norm_g"] = dgod
    g["od_rel_bias"] = relbias_bwd(dbias)[None]

    dx1, dqg0, hx0, rx0, dkv0, dgxa0 = xa_bwd(x1, dx2, *xa0, "xa_bwd0")
    big["xa_w_qg"] = mm_tn(hx0, dqg0, "xa_dwqg0", ("cols", 0), into=dwqg)
    big["xa_w_o"] = mm_tn(rx0, dx2, "xa_dwo0", ("rows", 0), into=dwo)
    g["xa_norm_g"] = jnp.concatenate([dgxa0, dgxa1], axis=0)

    dys, dc, dz, ra, z1b, dtb, dbglu, dlng, dlnb = ev_tail_bwd(ys, z, c, dx1, *tail)
    big["ev_w_out"] = mm_tn(ra, dx1, "ev_dwout", ("rows",))
    big["ev_s5_glu_w"] = mm_tn(z1b, dtb, "ev_dwglu", ("rows",))
    g["ev_s5_glu_b"], g["ev_conv_ln_g"], g["ev_conv_ln_b"] = dbglu, dlng, dlnb
    dwkv = mm_tn(memn_b, dkv1, "xa_dwkv1", ("cols", 1), bl=MEM_LEN)
    big["xa_w_kv"] = mm_tn(memn_b, dkv0, "xa_dwkv0", ("cols", 0), into=dwkv, bl=MEM_LEN)
    dmem0 = mm_nt_cols(dkv0, gw["xa_w_kv"], 0, "xa_kv_bwd0")
    dmem1 = mm_nt_cols(dkv1, gw["xa_w_kv"], 1, "xa_kv_bwd1")
    g["mem_norm_g"] = rms_dgain(mem, dmem0, dmem1, "mem_norm_bwd").reshape(D)

    shard_major = lambda t: t.reshape((-1,) + t.shape[-2:])
    gs = [shard_major(big[n]) for n in EARLY_GRADS]
    dz, dconvw, dconvb, *from_sibling = conv_bwd(z, dc, dz, conv_w, carried=carried_sibling_send(gs))
    g["ev_conv_w"] = dconvw[None, :CONV_KERNEL]
    g["ev_conv_b"] = dconvb
    chip_sums = [sum_with_sibling(gi, ri, core, "sum_sibling_" + n) for n, gi, ri in zip(EARLY_GRADS, gs, from_sibling)]
    dz, s5g, from_chips = s5_mixer_core_bwd(z, dys, dz, p["ev_s5_lambda_re"][0], p["ev_s5_lambda_im"][0], s5_saved,
                                            carried=carried_chips_exchange(chip_sums))
    reduced = {n: sum_chips(ci, ri, place, "sum_chips_" + n) for n, ci, ri in zip(EARLY_GRADS, chip_sums, from_chips)}
    for n, v in s5g.items():
        g["ev_s5_" + n] = v[None]
    packed, slots = pack_rows([_as2d(g[n]) for n in PACKED_SMALL], "pack_small_grads")
    dwin_ev, *gathered = mm_tn(h0b, dz, "ev_dwin", ("cols",),
                               carried=carried_allgather_devices([packed] + [_as2d(g[n]) for n in SINGLE_SMALL]))
    grad_x, dgev = mm_nt_normbwd((dz,), (0,), gw["ev_w_in"], x, p["ev_norm_g"], dx1, "ev_in_bwd")
    chip_sum = _reduce_to_chip([dwin_ev], ["ev_w_in"], core, "last")
    reduced["ev_w_in"] = sum_chips(chip_sum[0], chips_exchange(chip_sum)[0], place, "sum_chips_ev_w_in")
    return loss, grad_x, g, reduced, dgev, gathered, slots


def _me():
    return lax.axis_index("x"), lax.axis_index("y"), lax.axis_index("c")


def _other_chips(x, y):
    return [(1 - x, y), (x, 1 - y), (1 - x, 1 - y)]


def _remote(src, dst, send_sems, recv_sems, k, to):
    return pltpu.make_async_remote_copy(src_ref=src, dst_ref=dst, send_sem=send_sems.at[k], recv_sem=recv_sems.at[k],
                                        device_id=to, device_id_type=MESH)


def _rows_half(ref, h):
    H = ref.shape[-2] // 2
    return ref.at[(slice(None),) * (len(ref.shape) - 2) + (pl.ds(h * H, H), slice(None))]


def allgather_devices(vs):
    n = len(vs)

    def body(*refs):
        ins, outs = refs[:n], refs[n:2 * n]
        send_sems, recv_sems, local_sems = refs[2 * n:]
        x, y, c = _me()
        sib = (x, y, 1 - c)
        chips = _other_chips(x, y)
        me = 4 * x + 2 * y + c
        local = [pltpu.make_async_copy(ins[i], outs[i].at[me], local_sems.at[i]) for i in range(n)]
        for cp in local:
            cp.start()
        first, passed = [], []
        for i in range(n):
            first.append(_remote(ins[i], outs[i].at[me], send_sems, recv_sems, 7 * i, sib))
            for j, (cx, cy) in enumerate(chips):
                first.append(_remote(ins[i], outs[i].at[me], send_sems, recv_sems, 7 * i + 1 + j, (cx, cy, c)))
        for cp in first:
            cp.start()
        for j, (cx, cy) in enumerate(chips):
            for i in range(n):
                got = outs[i].at[4 * cx + 2 * cy + c]
                _remote(got, got, send_sems, recv_sems, 7 * i + 1 + j, (cx, cy, c)).wait_recv()
                fw = _remote(got, got, send_sems, recv_sems, 7 * i + 4 + j, sib)
                fw.start()
                passed.append(fw)
        for i in range(n):
            got = outs[i].at[4 * x + 2 * y + (1 - c)]
            _remote(got, got, send_sems, recv_sems, 7 * i, sib).wait_recv()
            for j, (cx, cy) in enumerate(chips):
                got = outs[i].at[4 * cx + 2 * cy + (1 - c)]
                _remote(got, got, send_sems, recv_sems, 7 * i + 4 + j, sib).wait_recv()
        for cp in first + passed:
            cp.wait_send()
        for cp in local:
            cp.wait()

    return pl.pallas_call(
        body, name="allgather_devices", in_specs=[ANY] * n, out_specs=[ANY] * n,
        out_shape=[jax.ShapeDtypeStruct((N_DEV,) + v.shape, v.dtype) for v in vs],
        scratch_shapes=[pltpu.SemaphoreType.DMA((7 * n,)), pltpu.SemaphoreType.DMA((7 * n,)),
                        pltpu.SemaphoreType.DMA((n,))],
    )(*vs)


def sibling_send_other_half(gs, name):
    n = len(gs)

    def body(*refs):
        ins, outs = refs[:n], refs[n:2 * n]
        send_sems, recv_sems = refs[2 * n:]
        x, y, c = _me()
        cps = [_remote(_rows_half(ins[i], 1 - c), outs[i], send_sems, recv_sems, i, (x, y, 1 - c)) for i in range(n)]
        for cp in cps:
            cp.start()
        for cp in cps:
            cp.wait()

    return pl.pallas_call(
        body, name=name, in_specs=[ANY] * n, out_specs=[ANY] * n,
        out_shape=[jax.ShapeDtypeStruct((g.shape[0], g.shape[1] // 2, g.shape[2]), g.dtype) for g in gs],
        scratch_shapes=[pltpu.SemaphoreType.DMA((n,)), pltpu.SemaphoreType.DMA((n,))],
    )(*gs)


def chips_exchange(parts):
    n = len(parts)

    def body(*refs):
        ins, outs = refs[:n], refs[n:2 * n]
        send_sems, recv_sems = refs[2 * n:]
        x, y, c = _me()
        cps = []
        for i in range(n):
            nl = ins[i].shape[0] // N_CHIPS
            for j, (cx, cy) in enumerate(_other_chips(x, y)):
                cps.append(_remote(ins[i].at[pl.ds((2 * cx + cy) * nl, nl)], outs[i].at[j], send_sems, recv_sems,
                                   3 * i + j, (cx, cy, c)))
        for cp in cps:
            cp.start()
        for cp in cps:
            cp.wait()

    return pl.pallas_call(
        body, name="chips_exchange", in_specs=[ANY] * n, out_specs=[ANY] * n,
        out_shape=[jax.ShapeDtypeStruct((3, a.shape[0] // N_CHIPS) + a.shape[1:], a.dtype) for a in parts],
        scratch_shapes=[pltpu.SemaphoreType.DMA((3 * n,)), pltpu.SemaphoreType.DMA((3 * n,))],
    )(*parts)


def sibling_share(fulls):
    n = len(fulls)

    def body(*refs):
        outs = refs[n:2 * n]
        send_sems, recv_sems = refs[2 * n:]
        x, y, c = _me()
        cps = [_remote(_rows_half(outs[i], c), _rows_half(outs[i], c), send_sems, recv_sems, i, (x, y, 1 - c))
               for i in range(n)]
        for cp in cps:
            cp.start()
        for i in range(n):
            got = _rows_half(outs[i], 1 - c)
            _remote(got, got, send_sems, recv_sems, i, (x, y, 1 - c)).wait_recv()
        for cp in cps:
            cp.wait_send()

    return pl.pallas_call(
        body, name="sibling_share", in_specs=[ANY] * n, out_specs=[ANY] * n,
        out_shape=[jax.ShapeDtypeStruct(f.shape, f.dtype) for f in fulls],
        input_output_aliases={i: i for i in range(n)},
        scratch_shapes=[pltpu.SemaphoreType.DMA((n,)), pltpu.SemaphoreType.DMA((n,))],
    )(*fulls)


def sum_with_sibling(g, recv, core, name):
    S, H, C = recv.shape
    tr = min(512, H)

    def body(c_ref, g_ref, r_ref, o_ref):
        o_ref[...] = (g_ref[...].astype(F32) + r_ref[...].astype(F32)).astype(o_ref.dtype)

    nb = H // tr
    return pl.pallas_call(
        body, name=name,
        grid_spec=pltpu.PrefetchScalarGridSpec(
            num_scalar_prefetch=1, grid=(S, nb),
            in_specs=[pl.BlockSpec((None, tr, C), lambda s, i, c_ref: (s, c_ref[0] * nb + i, 0)),
                      pl.BlockSpec((None, tr, C), lambda s, i, c_ref: (s, i, 0))],
            out_specs=pl.BlockSpec((None, tr, C), lambda s, i, c_ref: (s, i, 0))),
        out_shape=jax.ShapeDtypeStruct((S, H, C), g.dtype), compiler_params=_cp("parallel", "parallel"),
    )(core, g, recv)


def sum_chips(a, recv, place, name):
    _, nl, H, C = recv.shape
    tr = min(512, H)
    nb = H // tr

    def body(p_ref, a_ref, r_ref, o_ref):
        acc = a_ref[...].astype(F32)
        for j in range(3):
            acc = acc + r_ref[j].astype(F32)
        o_ref[...] = acc

    return pl.pallas_call(
        body, name=name,
        grid_spec=pltpu.PrefetchScalarGridSpec(
            num_scalar_prefetch=1, grid=(nl, nb),
            in_specs=[pl.BlockSpec((None, tr, C), lambda l, i, p_ref: (p_ref[0] * nl + l, i, 0)),
                      pl.BlockSpec((3, None, tr, C), lambda l, i, p_ref: (0, l, i, 0))],
            out_specs=pl.BlockSpec((None, tr, C), lambda l, i, p_ref: (l, p_ref[1] * nb + i, 0))),
        out_shape=jax.ShapeDtypeStruct((nl, 2 * H, C), F32), compiler_params=_cp("parallel", "parallel"),
    )(place, a, recv)


def pack_rows(arrays, name):
    starts, r0 = [], 0
    for a in arrays:
        if a.shape[0] >= SUBLANES:
            r0 = -(-r0 // SUBLANES) * SUBLANES
        starts.append(r0)
        r0 += a.shape[0]
    r0 = -(-r0 // SUBLANES) * SUBLANES
    n = len(arrays)

    def body(*refs):
        o_ref = refs[n]
        o_ref[...] = jnp.zeros_like(o_ref)
        for a_ref, s in zip(refs[:n], starts):
            r, c = a_ref.shape
            o_ref[s:s + r, 0:c] = a_ref[...]

    out = pl.pallas_call(body, name=name, out_shape=jax.ShapeDtypeStruct((r0, PACK_COLS), F32))(*arrays)
    return out, starts


def sum_slot(gathered, slot, shape, name):
    r, c = shape

    def body(ga_ref, o_ref):
        acc = ga_ref[0, slot:slot + r, 0:c]
        for d in range(1, N_DEV):
            acc = acc + ga_ref[d, slot:slot + r, 0:c]
        o_ref[...] = acc

    return pl.pallas_call(body, name=name, out_shape=jax.ShapeDtypeStruct((r, c), F32))(gathered)


def carried_allgather(blocks):
    n = len(blocks)

    def first_hop(ins, outs, sems, i, j, chip, x, y, c):
        me = 2 * x + y
        return _remote(_rows_half(ins[i], c), _rows_half(outs[i].at[me], c), sems[0], sems[1], 6 * i + j, (*chip, c))

    def start(ins, outs, sems):
        x, y, c = _me()
        for i in range(n):
            pltpu.make_async_copy(ins[i], outs[i].at[2 * x + y], sems[2].at[i]).start()
        for i in range(n):
            for j, chip in enumerate(_other_chips(x, y)):
                first_hop(ins, outs, sems, i, j, chip, x, y, c).start()

    def finish(ins, outs, sems):
        x, y, c = _me()
        sib = (x, y, 1 - c)
        chips = _other_chips(x, y)
        passed = []
        for j, (cx, cy) in enumerate(chips):
            for i in range(n):
                got = _rows_half(outs[i].at[2 * cx + cy], c)
                _remote(got, got, sems[0], sems[1], 6 * i + j, (cx, cy, c)).wait_recv()
                fw = _remote(got, got, sems[0], sems[1], 6 * i + 3 + j, sib)
                fw.start()
                passed.append(fw)
        for j, (cx, cy) in enumerate(chips):
            for i in range(n):
                got = _rows_half(outs[i].at[2 * cx + cy], 1 - c)
                _remote(got, got, sems[0], sems[1], 6 * i + 3 + j, sib).wait_recv()
        for i in range(n):
            for j, chip in enumerate(chips):
                first_hop(ins, outs, sems, i, j, chip, x, y, c).wait_send()
        for fw in passed:
            fw.wait_send()
        for i in range(n):
            pltpu.make_async_copy(ins[i], outs[i].at[2 * x + y], sems[2].at[i]).wait()

    return Carried(blocks, [jax.ShapeDtypeStruct((N_CHIPS,) + b.shape, b.dtype) for b in blocks],
                   [pltpu.SemaphoreType.DMA((6 * n,)), pltpu.SemaphoreType.DMA((6 * n,)), pltpu.SemaphoreType.DMA((n,))],
                   start, finish)


def carried_allgather_devices(vs):
    n = len(vs)

    def first_copies(ins, outs, sems):
        x, y, c = _me()
        me = 4 * x + 2 * y + c
        cps = []
        for i in range(n):
            cps.append(_remote(ins[i], outs[i].at[me], sems[0], sems[1], 7 * i, (x, y, 1 - c)))
            for j, (cx, cy) in enumerate(_other_chips(x, y)):
                cps.append(_remote(ins[i], outs[i].at[me], sems[0], sems[1], 7 * i + 1 + j, (cx, cy, c)))
        return cps

    def local_copies(ins, outs, sems):
        x, y, c = _me()
        return [pltpu.make_async_copy(ins[i], outs[i].at[4 * x + 2 * y + c], sems[2].at[i]) for i in range(n)]

    def start(ins, outs, sems):
        for cp in local_copies(ins, outs, sems) + first_copies(ins, outs, sems):
            cp.start()

    def finish(ins, outs, sems):
        x, y, c = _me()
        sib = (x, y, 1 - c)
        chips = _other_chips(x, y)
        passed = []
        for j, (cx, cy) in enumerate(chips):
            for i in range(n):
                got = outs[i].at[4 * cx + 2 * cy + c]
                _remote(got, got, sems[0], sems[1], 7 * i + 1 + j, (cx, cy, c)).wait_recv()
                fw = _remote(got, got, sems[0], sems[1], 7 * i + 4 + j, sib)
                fw.start()
                passed.append(fw)
        for i in range(n):
            got = outs[i].at[4 * x + 2 * y + (1 - c)]
            _remote(got, got, sems[0], sems[1], 7 * i, sib).wait_recv()
            for j, (cx, cy) in enumerate(chips):
                got = outs[i].at[4 * cx + 2 * cy + (1 - c)]
                _remote(got, got, sems[0], sems[1], 7 * i + 4 + j, sib).wait_recv()
        for cp in first_copies(ins, outs, sems) + passed:
            cp.wait_send()
        for cp in local_copies(ins, outs, sems):
            cp.wait()

    return Carried(vs, [jax.ShapeDtypeStruct((N_DEV,) + v.shape, v.dtype) for v in vs],
                   [pltpu.SemaphoreType.DMA((7 * n,)), pltpu.SemaphoreType.DMA((7 * n,)), pltpu.SemaphoreType.DMA((n,))],
                   start, finish)


def carried_sibling_send(gs):
    n = len(gs)

    def copies(ins, outs, sems):
        x, y, c = _me()
        return [_remote(_rows_half(ins[i], 1 - c), outs[i], sems[0], sems[1], i, (x, y, 1 - c)) for i in range(n)]

    def start(ins, outs, sems):
        for cp in copies(ins, outs, sems):
            cp.start()

    def finish(ins, outs, sems):
        for cp in copies(ins, outs, sems):
            cp.wait()

    return Carried(gs, [jax.ShapeDtypeStruct((g.shape[0], g.shape[1] // 2, g.shape[2]), g.dtype) for g in gs],
                   [pltpu.SemaphoreType.DMA((n,)), pltpu.SemaphoreType.DMA((n,))], start, finish)


def carried_chips_exchange(parts):
    n = len(parts)

    def copies(ins, outs, sems):
        x, y, c = _me()
        cps = []
        for i in range(n):
            nl = ins[i].shape[0] // N_CHIPS
            for j, (cx, cy) in enumerate(_other_chips(x, y)):
                cps.append(_remote(ins[i].at[pl.ds((2 * cx + cy) * nl, nl)], outs[i].at[j], sems[0], sems[1],
                                   3 * i + j, (cx, cy, c)))
        return cps

    def start(ins, outs, sems):
        for cp in copies(ins, outs, sems):
            cp.start()

    def finish(ins, outs, sems):
        for cp in copies(ins, outs, sems):
            cp.wait()

    return Carried(parts, [jax.ShapeDtypeStruct((3, a.shape[0] // N_CHIPS) + a.shape[1:], a.dtype) for a in parts],
                   [pltpu.SemaphoreType.DMA((3 * n,)), pltpu.SemaphoreType.DMA((3 * n,))], start, finish)


BIG = ("ev_w_in", "ev_s5_glu_w", "ev_w_out", "od_w_in", "od_w_out", "xa_w_qg", "xa_w_kv", "xa_w_o")
SHARDED_F32 = (("ev_conv_w", 2), ("od_norm_g", 1))
SMALL = ("mem_norm_g", "ev_norm_g", "ev_s5_lambda_re", "ev_s5_lambda_im", "ev_s5_log_dt", "ev_s5_b_re", "ev_s5_b_im",
         "ev_s5_c_re", "ev_s5_c_im", "ev_s5_d", "ev_s5_glu_b", "ev_conv_b", "ev_conv_ln_g", "ev_conv_ln_b",
         "od_rel_bias", "xa_norm_g", "final_norm_g")
NARROW = ("ev_s5_c_re", "ev_s5_c_im")
DENSE_B = ("ev_s5_b_re", "ev_s5_b_im")
PACK_COLS = 1024
PACKED_SMALL = tuple(n for n in SMALL if n not in NARROW and n != "ev_norm_g")
SINGLE_SMALL = NARROW + tuple(n for n, _ in SHARDED_F32)
WEIGHTS = ("mem_norm_g", "ev_norm_g", "ev_w_in", "ev_s5_lambda_re", "ev_s5_lambda_im", "ev_s5_log_dt", "ev_s5_b_re",
           "ev_s5_b_im", "ev_s5_c_re", "ev_s5_c_im", "ev_s5_d", "ev_s5_glu_w", "ev_s5_glu_b", "ev_conv_w", "ev_conv_b",
           "ev_conv_ln_g", "ev_conv_ln_b", "ev_w_out", "od_norm_g", "od_w_in", "od_rel_bias", "od_w_out", "xa_norm_g",
           "xa_w_qg", "xa_w_kv", "xa_w_o", "final_norm_g")


def _as2d(a):
    return a.reshape(1, -1) if a.ndim == 1 else a.reshape(-1, a.shape[-1])


def kernel(x, mem, mem_norm_g, ev_norm_g, ev_w_in, ev_s5_lambda_re, ev_s5_lambda_im, ev_s5_log_dt, ev_s5_b_re, ev_s5_b_im, ev_s5_c_re, ev_s5_c_im, ev_s5_d, ev_s5_glu_w, ev_s5_glu_b, ev_conv_w, ev_conv_b, ev_conv_ln_g, ev_conv_ln_b, ev_w_out, od_norm_g, od_w_in, od_rel_bias, od_w_out, xa_norm_g, xa_w_qg, xa_w_kv, xa_w_o, final_norm_g, loss_target, m_mem_norm_g, m_ev_norm_g, m_ev_w_in, m_ev_s5_lambda_re, m_ev_s5_lambda_im, m_ev_s5_log_dt, m_ev_s5_b_re, m_ev_s5_b_im, m_ev_s5_c_re, m_ev_s5_c_im, m_ev_s5_d, m_ev_s5_glu_w, m_ev_s5_glu_b, m_ev_conv_w, m_ev_conv_b, m_ev_conv_ln_g, m_ev_conv_ln_b, m_ev_w_out, m_od_norm_g, m_od_w_in, m_od_rel_bias, m_od_w_out, m_xa_norm_g, m_xa_w_qg, m_xa_w_kv, m_xa_w_o, m_final_norm_g, v_mem_norm_g, v_ev_norm_g, v_ev_w_in, v_ev_s5_lambda_re, v_ev_s5_lambda_im, v_ev_s5_log_dt, v_ev_s5_b_re, v_ev_s5_b_im, v_ev_s5_c_re, v_ev_s5_c_im, v_ev_s5_d, v_ev_s5_glu_w, v_ev_s5_glu_b, v_ev_conv_w, v_ev_conv_b, v_ev_conv_ln_g, v_ev_conv_ln_b, v_ev_w_out, v_od_norm_g, v_od_w_in, v_od_rel_bias, v_od_w_out, v_xa_norm_g, v_xa_w_qg, v_xa_w_kv, v_xa_w_o, v_final_norm_g):
    a = dict(locals())
    w = {n: a[n] for n in WEIGHTS}
    shard = (2 * lax.axis_index("x") + lax.axis_index("y")).reshape(1).astype(jnp.int32)
    core = lax.axis_index("c").reshape(1).astype(jnp.int32)

    place = jnp.concatenate([shard, core])

    blocks = {n: w[n].astype(BF16).reshape(-1, w[n].shape[-1]) for n in BIG}
    conv_blk = jnp.pad(_as2d(w["ev_conv_w"]), ((0, 1), (0, 0)))
    odn_blk = w["od_norm_g"].reshape(2, -1)
    bias, evin_g, conv_g, odn_g = att_bias(w["od_rel_bias"][0],
                                           carried=carried_allgather([blocks["ev_w_in"], conv_blk, odn_blk]))
    gw = {"ev_w_in": evin_g}
    p = {n: w[n] for n in SMALL}
    p["ev_conv_w"] = jnp.concatenate([conv_g[s, :CONV_KERNEL] for s in range(N_CHIPS)], axis=1)[None]
    p["od_norm_g"] = odn_g.reshape(1, D_MODEL)

    loss, grad_x, g, reduced, dgev, gath, slots = local_step(x[0], mem[0], loss_target[0], p, gw,
                                                             {n: blocks[n] for n in LATE}, bias, place, core)
    loss = lax.psum(loss[0, 0], ("x", "y", "c"))
    g_big = dict(zip(BIG, sibling_share([reduced[n] for n in BIG])))

    out = {tag: {} for tag in ("grad", "delta", "m", "v")}
    for n in BIG:
        sh = w[n].shape
        to2d = lambda t: t.reshape(-1, sh[-1])
        gn = to2d(g_big[n])
        d, mn, vn = adamw(to2d(w[n]), gn, to2d(a["m_" + n]), to2d(a["v_" + n]), "adamw_" + n)
        for tag, val in zip(("grad", "delta", "m", "v"), (gn, d, mn, vn)):
            out[tag][n] = val.reshape(sh)

    jobs = [(n, gath[0], s) for n, s in zip(PACKED_SMALL, slots)]
    jobs += [(n, gt, None) for n, gt in zip(SINGLE_SMALL, gath[1:])]
    jobs += [("ev_norm_g", allgather_devices([dgev])[0], None)]
    for n, gt, slot in jobs:
        sh = w[n].shape
        w2, m2, v2 = _as2d(w[n]), _as2d(a["m_" + n]), _as2d(a["v_" + n])
        if n in DENSE_B:
            gn = _as2d(s5_b_from_dense(sum_slot(gt, slot, g[n].shape[-2:], "sum_" + n)))
            d, mn, vn = adamw(w2, gn, m2, v2, "adamw_" + n)
        else:
            gn, d, mn, vn = adamw_allreduce(gt, w2, m2, v2, shard, "adamw_" + n, slot=slot)
        for tag, val in zip(("grad", "delta", "m", "v"), (gn, d, mn, vn)):
            out[tag][n] = val.reshape(sh)

    res = [loss, grad_x[None]]
    for tag in ("grad", "delta", "m", "v"):
        res += [out[tag][n] for n in WEIGHTS]
    return tuple(res)
```
